```python
import jax, jax.numpy as jnp
from jax import lax
import numpy as np

D_MODEL = 1024
BATCH = 32
SEQ = 256
DEPTH = 1
DEC_BATCH = 2
DEC_SEQ = 2048
PAST_LEN = 256

GRID_W = 64
CONV_DIM = 512
CONV_WIDTH = 31
N_HEADS = 8
QK_NOPE = 64
QK_ROPE = 32
V_DIM = 64
Q_LORA = 256
KV_LORA = 128
N_EXPERTS = 16
EXPERT_FF = 512
EC_FACTOR = 2
ROPE_BASE = 10000.0
Q_BLOCK = 128
EPS = 1e-6
IN_SIZES = (2 * CONV_DIM, Q_LORA, KV_LORA, QK_ROPE, 2 * D_MODEL)
IN_COLS = sum(IN_SIZES)
IN_SPLITS = tuple(int(s) for s in np.cumsum(IN_SIZES)[:-1])

kernel_name = 'hybrid_conformer_mla_ec_moe_diffusion_step'


def rmsnorm(x, g):
    xf = x.astype(jnp.float32)
    y = xf * lax.rsqrt(jnp.mean(xf * xf, axis=-1, keepdims=True) + EPS)
    return (y * g.astype(jnp.float32)).astype(x.dtype)


def layernorm(x, g, b):
    xf = x.astype(jnp.float32)
    mu = jnp.mean(xf, axis=-1, keepdims=True)
    var = jnp.mean(jnp.square(xf - mu), axis=-1, keepdims=True)
    y = (xf - mu) * lax.rsqrt(var + EPS)
    return (y * g.astype(jnp.float32) + b.astype(jnp.float32)).astype(x.dtype)


def grid_angles(n):
    rows = n // GRID_W
    r = jnp.repeat(jnp.arange(rows), GRID_W).astype(jnp.float32)
    col = jnp.tile(jnp.arange(GRID_W), rows).astype(jnp.float32)
    half = QK_ROPE // 2
    freqs = ROPE_BASE ** (-jnp.arange(0, half, 2, dtype=jnp.float32) / half)
    return r[:, None] * freqs, col[:, None] * freqs


def rotate(x, ang):
    x1, x2 = jnp.split(x.astype(jnp.float32), 2, axis=-1)
    cos, sin = jnp.cos(ang), jnp.sin(ang)
    return jnp.concatenate([x1 * cos - x2 * sin, x1 * sin + x2 * cos], axis=-1)


def rope_2d(x, ang_r, ang_c):
    half = QK_ROPE // 2
    out = jnp.concatenate([rotate(x[..., :half], ang_r), rotate(x[..., half:], ang_c)], axis=-1)
    return out.astype(x.dtype)


def conformer_conv(u, dw, dw_b, ln_g, ln_b, w_out):
    a, b = jnp.split(u, 2, axis=-1)
    v = a * jax.nn.sigmoid(b)
    pad = CONV_WIDTH // 2
    v = lax.conv_general_dilated(v, dw[:, None, :].astype(v.dtype), (1,), [(pad, pad)],
                                 dimension_numbers=('NWC', 'WIO', 'NWC'),
                                 feature_group_count=CONV_DIM) + dw_b
    v = jax.nn.silu(layernorm(v, ln_g, ln_b))
    return v @ w_out


def mla_attention(q, ckv, kpe, w_kvb, w_o):
    B, T = q.shape[0], q.shape[1]
    S = ckv.shape[1]
    kv = (ckv @ w_kvb).reshape(B, S, N_HEADS, QK_NOPE + V_DIM)
    k_nope, v = kv[..., :QK_NOPE], kv[..., QK_NOPE:]
    scale = (QK_NOPE + QK_ROPE) ** -0.5
    nb = T // Q_BLOCK
    qb = q.reshape(B, nb, Q_BLOCK, N_HEADS, QK_NOPE + QK_ROPE).transpose(1, 0, 2, 3, 4)

    def block(q_blk):
        s = (jnp.einsum('bqhd,bkhd->bhqk', q_blk[..., :QK_NOPE], k_nope)
             + jnp.einsum('bqhr,bkr->bhqk', q_blk[..., QK_NOPE:], kpe))
        p = jax.nn.softmax(s.astype(jnp.float32) * scale, axis=-1).astype(v.dtype)
        return jnp.einsum('bhqk,bkhd->bqhd', p, v)

    o = lax.map(block, qb)
    o = o.transpose(1, 0, 2, 3, 4).reshape(B, T, N_HEADS * V_DIM)
    return o @ w_o


def ec_moe(h, w_router, w_g, w_u, w_d):
    B, n, D = h.shape
    cap = EC_FACTOR * n // N_EXPERTS
    aff = jax.nn.softmax((h @ w_router).astype(jnp.float32), axis=-1)
    vals, idx = lax.top_k(aff.transpose(0, 2, 1), cap)
    xg = jax.vmap(lambda hb, ib: hb[ib])(h, idx)
    a = jnp.einsum('becd,edf->becf', xg, w_g)
    u = jnp.einsum('becd,edf->becf', xg, w_u)
    y = jnp.einsum('becf,efd->becd', jax.nn.silu(a) * u, w_d) * vals[..., None].astype(h.dtype)
    return jax.vmap(lambda ib, yb: jnp.zeros((n, D), yb.dtype).at[ib.reshape(-1)].add(yb.reshape(-1, D)))(idx, y)


def trunk_layer(x, mod_vec, p, ctx_ckv=None, ctx_kpe=None):
    B, n, _ = x.shape
    m = jnp.einsum('bd,de->be', jax.nn.silu(mod_vec), p['w_ada']) + p['b_ada']
    sh1, sc1, g1, sh2, sc2, g2 = jnp.split(m[:, None, :], 6, axis=-1)
    h = rmsnorm(x, p['norm1']) * (1 + sc1) + sh1
    proj = h @ p['w_in']
    conv_in, q_a, kv_a, k_rope, gate_lin = jnp.split(proj, IN_SPLITS, axis=-1)
    conv_out = conformer_conv(conv_in, p['conv_dw'], p['conv_dw_b'], p['conv_ln_g'], p['conv_ln_b'], p['w_conv_out'])
    q = (rmsnorm(q_a, p['q_norm']) @ p['w_qb']).reshape(B, n, N_HEADS, QK_NOPE + QK_ROPE)
    ckv = rmsnorm(kv_a, p['kv_norm'])
    if ctx_ckv is None:
        keys_ckv, keys_kpe = ckv, k_rope
    else:
        ang_r, ang_c = grid_angles(n)
        q = jnp.concatenate([q[..., :QK_NOPE], rope_2d(q[..., QK_NOPE:], ang_r[:, None], ang_c[:, None])], axis=-1)
        keys_ckv = jnp.concatenate([ctx_ckv, ckv], axis=1)
        keys_kpe = jnp.concatenate([ctx_kpe, rope_2d(k_rope, ang_r, ang_c)], axis=1)
    attn_out = mla_attention(q, keys_ckv, keys_kpe, p['w_kvb'], p['w_o_mla'])
    g_conv, g_attn = jnp.split(jax.nn.sigmoid(gate_lin), 2, axis=-1)
    x = x + g1 * ((g_conv * conv_out + g_attn * attn_out) @ p['w_out'])
    h2 = rmsnorm(x, p['norm2']) * (1 + sc2) + sh2
    x = x + g2 * ec_moe(h2, p['w_router'], p['w_e_gate'], p['w_e_up'], p['w_e_down'])
    return x, ckv, k_rope


def setup_inputs(seed: int = 0) -> dict:
    key = jax.random.key(seed)
    ks = jax.random.split(key, 32)
    f32 = jnp.float32

    def dense(k, shape, fan_in, scale=1.0):
        return jax.random.normal(k, shape, f32) * (fan_in ** -0.5) * scale

    def gain(k, shape):
        return 1.0 + 0.02 * jax.random.normal(k, shape, f32)

    def bias(k, shape):
        return 0.01 * jax.random.normal(k, shape, f32)

    return {
        'x_prompt': jax.random.normal(ks[0], (BATCH, SEQ, D_MODEL), f32),
        'x_sample': jax.random.normal(ks[1], (DEC_BATCH, DEC_SEQ, D_MODEL), f32),
        'cache_ckv': jax.random.normal(ks[2], (DEC_BATCH, DEPTH, PAST_LEN, KV_LORA), f32),
        'cache_kpe': jax.random.normal(ks[3], (DEC_BATCH, DEPTH, PAST_LEN, QK_ROPE), f32),
        'c': jax.random.normal(ks[4], (DEC_BATCH, D_MODEL), f32),
        'c_ctx': jax.random.normal(ks[5], (D_MODEL,), f32),
        'w_ada': dense(ks[6], (DEPTH, D_MODEL, 6 * D_MODEL), D_MODEL, 0.5),
        'b_ada': bias(ks[7], (DEPTH, 6 * D_MODEL)),
        'norm1': gain(ks[8], (DEPTH, D_MODEL)),
        'w_in': dense(ks[9], (DEPTH, D_MODEL, IN_COLS), D_MODEL),
        'conv_dw': dense(ks[10], (DEPTH, CONV_WIDTH, CONV_DIM), CONV_WIDTH),
        'conv_dw_b': bias(ks[11], (DEPTH, CONV_DIM)),
        'conv_ln_g': gain(ks[12], (DEPTH, CONV_DIM)),
        'conv_ln_b': bias(ks[13], (DEPTH, CONV_DIM)),
        'w_conv_out': dense(ks[14], (DEPTH, CONV_DIM, D_MODEL), CONV_DIM),
        'q_norm': gain(ks[15], (DEPTH, Q_LORA)),
        'w_qb': dense(ks[16], (DEPTH, Q_LORA, N_HEADS * (QK_NOPE + QK_ROPE)), Q_LORA),
        'kv_norm': gain(ks[17], (DEPTH, KV_LORA)),
        'w_kvb': dense(ks[18], (DEPTH, KV_LORA, N_HEADS * (QK_NOPE + V_DIM)), KV_LORA),
        'w_o_mla': dense(ks[19], (DEPTH, N_HEADS * V_DIM, D_MODEL), N_HEADS * V_DIM),
        'w_out': dense(ks[20], (DEPTH, D_MODEL, D_MODEL), D_MODEL),
        'norm2': gain(ks[21], (DEPTH, D_MODEL)),
        'w_router': dense(ks[22], (DEPTH, D_MODEL, N_EXPERTS), D_MODEL),
        'w_e_gate': dense(ks[23], (DEPTH, N_EXPERTS, D_MODEL, EXPERT_FF), D_MODEL),
        'w_e_up': dense(ks[24], (DEPTH, N_EXPERTS, D_MODEL, EXPERT_FF), D_MODEL),
        'w_e_down': dense(ks[25], (DEPTH, N_EXPERTS, EXPERT_FF, D_MODEL), EXPERT_FF),
        'final_norm': gain(ks[26], (D_MODEL,)),
    }


def reference(x_prompt, x_sample, cache_ckv, cache_kpe, c, c_ctx, w_ada, b_ada, norm1, w_in,
              conv_dw, conv_dw_b, conv_ln_g, conv_ln_b, w_conv_out, q_norm, w_qb, kv_norm, w_kvb,
              w_o_mla, w_out, norm2, w_router, w_e_gate, w_e_up, w_e_down, final_norm):
    xp, xs = x_prompt, x_sample
    new_ckv, new_kpe = [], []
    for l in range(DEPTH):
        p = {
            'w_ada': w_ada[l], 'b_ada': b_ada[l], 'norm1': norm1[l], 'w_in': w_in[l],
            'conv_dw': conv_dw[l], 'conv_dw_b': conv_dw_b[l], 'conv_ln_g': conv_ln_g[l],
            'conv_ln_b': conv_ln_b[l], 'w_conv_out': w_conv_out[l], 'q_norm': q_norm[l],
            'w_qb': w_qb[l], 'kv_norm': kv_norm[l], 'w_kvb': w_kvb[l], 'w_o_mla': w_o_mla[l],
            'w_out': w_out[l], 'norm2': norm2[l], 'w_router': w_router[l],
            'w_e_gate': w_e_gate[l], 'w_e_up': w_e_up[l], 'w_e_down': w_e_down[l],
        }
        xp, ckv_l, kpe_l = trunk_layer(xp, c_ctx[None, :], p)
        new_ckv.append(ckv_l)
        new_kpe.append(kpe_l)
        xs, _, _ = trunk_layer(xs, c, p, cache_ckv[:, l], cache_kpe[:, l])
    y_prompt = rmsnorm(xp, final_norm)
    y_sample = rmsnorm(xs, final_norm)
    new_cache_ckv = jnp.stack(new_ckv, axis=1)
    new_cache_kpe = jnp.stack(new_kpe, axis=1)
    return (y_prompt, y_sample, new_cache_ckv, new_cache_kpe)
```

```python
import functools

import jax
import jax.numpy as jnp
from jax import lax
from jax.experimental import pallas as pl
from jax.experimental.pallas import tpu as pltpu

D_MODEL = 1024
GRID_W = 64
CONV_DIM = 512
CONV_WIDTH = 31
N_HEADS = 8
QK_NOPE = 64
QK_ROPE = 32
V_DIM = 64
Q_LORA = 256
KV_LORA = 128
N_EXPERTS = 16
EXPERT_FF = 512
EC_FACTOR = 2
ROPE_BASE = 10000.0
EPS = 1e-6

LANES = 128
HEAD_W = LANES
ROPE_OFF = QK_NOPE
CONV_HALO = 16
LOG2E = 1.4426950408889634
VMEM_LIMIT = 48 * 1024 * 1024

C_CONV = 0
C_QA = 2 * CONV_DIM
C_KVA = C_QA + Q_LORA
C_KR = C_KVA + KV_LORA
C_GATE = C_KR + LANES
C_END = C_GATE + 2 * D_MODEL
C_KRP = C_END

f32 = jnp.float32
bf16 = jnp.bfloat16


def _params(*sem):
    return pltpu.CompilerParams(dimension_semantics=sem, vmem_limit_bytes=VMEM_LIMIT)


def _dot(a, b):
    return jnp.dot(a, b, preferred_element_type=f32)


def _rms(x, g):
    return x * lax.rsqrt(jnp.mean(x * x, axis=-1, keepdims=True) + EPS) * g


def _const_spec(shape):
    nd = len(shape)
    return pl.BlockSpec(shape, lambda *_: (0,) * nd)


def _mod_spec(m3, seq_of):
    if m3.shape[0] == 1:
        return _const_spec((None, 6, D_MODEL))
    return pl.BlockSpec((None, 6, D_MODEL), lambda *idx: (seq_of(*idx), 0, 0))


def _ada_kernel(s_ref, w_ref, b_ref, o_ref):
    s = s_ref[...]
    s = s * jax.nn.sigmoid(s)
    o_ref[...] = _dot(s.astype(bf16), w_ref[...].astype(bf16)) + b_ref[...]


def _ada(mod, w_ada, b_ada):
    rows = mod.shape[0]
    n_out = w_ada.shape[1]
    tn = D_MODEL
    return pl.pallas_call(
        _ada_kernel,
        grid=(n_out // tn,),
        in_specs=[
            _const_spec((rows, D_MODEL)),
            pl.BlockSpec((D_MODEL, tn), lambda j: (0, j)),
            pl.BlockSpec((1, tn), lambda j: (0, j)),
        ],
        out_specs=pl.BlockSpec((rows, tn), lambda j: (0, j)),
        out_shape=jax.ShapeDtypeStruct((rows, n_out), f32),
        compiler_params=_params("arbitrary"),
        name="ada",
    )(mod, w_ada, b_ada.reshape(1, n_out))


def _inproj_kernel(*refs, rope, q_scale):
    if rope:
        (x_ref, m_ref, n1_ref, win_ref, qn_ref, wqb_ref, wqbp_ref, kvn_ref, cos_ref, sin_ref,
         v_ref, q_ref, ckv_ref, kr_ref, g_ref) = refs
    else:
        (x_ref, m_ref, n1_ref, win_ref, qn_ref, wqb_ref, kvn_ref,
         v_ref, q_ref, ckv_ref, kr_ref, g_ref) = refs
    sh1 = m_ref[0:1, :]
    sc1 = m_ref[1:2, :]
    h = _rms(x_ref[...], n1_ref[...]) * (1.0 + sc1) + sh1
    hb = h.astype(bf16)

    a = _dot(hb, win_ref[:, C_CONV:C_CONV + CONV_DIM])
    b = _dot(hb, win_ref[:, C_CONV + CONV_DIM:C_QA])
    v_ref[...] = (a * jax.nn.sigmoid(b)).astype(v_ref.dtype)

    qa = _dot(hb, win_ref[:, C_QA:C_KVA])
    qn = _rms(qa, qn_ref[...]).astype(bf16)
    q = _dot(qn, wqb_ref[...])
    if rope:
        qp = _dot(qn, wqbp_ref[...])
        cos = cos_ref[...]
        sin = sin_ref[...]
        for hd in range(N_HEADS):
            sl = slice(hd * HEAD_W, (hd + 1) * HEAD_W)
            q_ref[:, sl] = ((q[:, sl] * cos + qp[:, sl] * sin) * q_scale).astype(q_ref.dtype)
    else:
        q_ref[...] = (q * q_scale).astype(q_ref.dtype)

    kva = _dot(hb, win_ref[:, C_KVA:C_KR])
    ckv_ref[...] = _rms(kva, kvn_ref[...]).astype(ckv_ref.dtype)

    kr = _dot(hb, win_ref[:, C_KR:C_GATE])
    if rope:
        krp = _dot(hb, win_ref[:, C_KRP:C_KRP + LANES])
        kr = kr * cos_ref[...] + krp * sin_ref[...]
    kr_ref[...] = kr.astype(kr_ref.dtype)

    gw = 512
    for j in range(2 * D_MODEL // gw):
        gl = _dot(hb, win_ref[:, C_GATE + j * gw:C_GATE + (j + 1) * gw])
        g_ref[:, j * gw:(j + 1) * gw] = jax.nn.sigmoid(gl).astype(g_ref.dtype)


def _inproj(x2d, m3, norm1, win, q_norm, wqb, wqbp, kv_norm, cos, sin, *, n, rope, tm=256):
    tokens = x2d.shape[0]
    tiles_per_seq = n // tm
    q_scale = float((QK_NOPE + QK_ROPE) ** -0.5 * LOG2E)
    tile = lambda w: pl.BlockSpec((tm, w), lambda i: (i, 0))
    in_specs = [
        tile(D_MODEL),
        _mod_spec(m3, lambda i: i // tiles_per_seq),
        _const_spec((1, D_MODEL)),
        _const_spec(win.shape),
        _const_spec((1, Q_LORA)),
        _const_spec(wqb.shape),
    ]
    args = [x2d, m3, norm1, win, q_norm, wqb]
    if rope:
        in_specs.append(_const_spec(wqbp.shape))
        args.append(wqbp)
    in_specs.append(_const_spec((1, KV_LORA)))
    args.append(kv_norm)
    if rope:
        tab = pl.BlockSpec((tm, LANES), lambda i: (i % tiles_per_seq, 0))
        in_specs += [tab, tab]
        args += [cos, sin]
    kdt = bf16 if rope else f32
    out_shape = (
        jax.ShapeDtypeStruct((tokens, CONV_DIM), bf16),
        jax.ShapeDtypeStruct((tokens, N_HEADS * HEAD_W), bf16),
        jax.ShapeDtypeStruct((tokens, KV_LORA), kdt),
        jax.ShapeDtypeStruct((tokens, LANES), kdt),
        jax.ShapeDtypeStruct((tokens, 2 * D_MODEL), bf16),
    )
    out_specs = (tile(CONV_DIM), tile(N_HEADS * HEAD_W), tile(KV_LORA), tile(LANES), tile(2 * D_MODEL))
    return pl.pallas_call(
        functools.partial(_inproj_kernel, rope=rope, q_scale=q_scale),
        grid=(tokens // tm,),
        in_specs=in_specs,
        out_specs=out_specs,
        out_shape=out_shape,
        compiler_params=_params("parallel"),
        name="inproj_rope" if rope else "inproj",
    )(*args)


def _conv_kernel(v_ref, g_ref, dw_ref, dwb_ref, lng_ref, lnb_ref, wco_ref, o_ref, vpad, ybuf, *, n, rt, ct):
    pad = CONV_WIDTH // 2
    zeros = jnp.zeros((CONV_HALO, CONV_DIM), f32)
    vpad[0:CONV_HALO, :] = zeros
    vpad[CONV_HALO + n:2 * CONV_HALO + n, :] = zeros
    vpad[CONV_HALO:CONV_HALO + n, :] = v_ref[...].astype(f32)

    sub = 8
    span = ((CONV_HALO - pad + CONV_WIDTH - 1) // sub) * sub

    def conv_chunk(c, carry):
        r0 = pl.multiple_of(c * ct, ct)
        for cb in range(CONV_DIM // LANES):
            sl = slice(cb * LANES, (cb + 1) * LANES)
            win = vpad[pl.ds(r0, ct + 2 * CONV_HALO), sl]
            acc = jnp.zeros((ct, LANES), f32)
            for ph in range(sub):
                wph = win[ph:ph + ct + span, :]
                for a in range(span // sub + 1):
                    k = a * sub + ph - (CONV_HALO - pad)
                    if 0 <= k < CONV_WIDTH:
                        acc = acc + wph[a * sub:a * sub + ct, :] * dw_ref[k:k + 1, sl]
            ybuf[pl.ds(r0, ct), sl] = acc + dwb_ref[:, sl]
        return carry

    lax.fori_loop(0, n // ct, conv_chunk, 0)

    def chunk(c, carry):
        r0 = pl.multiple_of(c * rt, rt)
        y = ybuf[pl.ds(r0, rt), :]
        mu = jnp.mean(y, axis=-1, keepdims=True)
        yc = y - mu
        var = jnp.mean(yc * yc, axis=-1, keepdims=True)
        z = yc * lax.rsqrt(var + EPS) * lng_ref[...] + lnb_ref[...]
        z = z * jax.nn.sigmoid(z)
        co = _dot(z.astype(bf16), wco_ref[...])
        o_ref[pl.ds(r0, rt), :] = (g_ref[pl.ds(r0, rt), :].astype(f32) * co).astype(o_ref.dtype)
        return carry

    lax.fori_loop(0, n // rt, chunk, 0)


def _conv(v3, g3, dw, dwb, lng, lnb, wco, *, rt=256, ct=64):
    nseq, n, _ = v3.shape
    return pl.pallas_call(
        functools.partial(_conv_kernel, n=n, rt=rt, ct=ct),
        grid=(nseq,),
        in_specs=[
            pl.BlockSpec((None, n, CONV_DIM), lambda s: (s, 0, 0)),
            pl.BlockSpec((None, n, D_MODEL), lambda s: (s, 0, 0)),
            _const_spec(dw.shape),
            _const_spec((1, CONV_DIM)),
            _const_spec((1, CONV_DIM)),
            _const_spec((1, CONV_DIM)),
            _const_spec(wco.shape),
        ],
        out_specs=pl.BlockSpec((None, n, D_MODEL), lambda s: (s, 0, 0)),
        out_shape=jax.ShapeDtypeStruct((nseq, n, D_MODEL), bf16),
        scratch_shapes=[
            pltpu.VMEM((n + 2 * CONV_HALO, CONV_DIM), f32),
            pltpu.VMEM((n, CONV_DIM), f32),
        ],
        compiler_params=_params("parallel"),
        name="conv",
    )(v3, g3, dw, dwb, lng, lnb, wco)


def _attn_kernel(q_ref, ckv_ref, kpe_ref, cm_ref, g_ref, wk_ref, wv_ref, wo_ref, o_ref, k_scr, v_scr):
    @pl.when(pl.program_id(1) == 0)
    def _():
        ckv = ckv_ref[...].astype(bf16)
        kpe = kpe_ref[...].astype(f32)
        for hd in range(N_HEADS):
            k = _dot(ckv, wk_ref[:, hd * HEAD_W:(hd + 1) * HEAD_W]) + kpe
            k_scr[hd] = k.astype(bf16)
        v = _dot(ckv, wv_ref[...])
        ones = jnp.ones((v.shape[0], LANES), bf16)
        for j in range(N_HEADS // 2):
            v_scr[j, :, 0:LANES] = v[:, j * LANES:(j + 1) * LANES].astype(bf16)
            v_scr[j, :, LANES:2 * LANES] = ones

    tq = q_ref.shape[0]
    lane = lax.broadcasted_iota(jnp.int32, (tq, LANES), 1)
    pairs = []
    for j in range(N_HEADS // 2):
        outs = []
        for hh in range(2):
            hd = 2 * j + hh
            qh = q_ref[:, hd * HEAD_W:(hd + 1) * HEAD_W]
            s = lax.dot_general(qh, k_scr[hd], (((1,), (1,)), ((), ())), preferred_element_type=f32)
            mx = jnp.max(s, axis=-1, keepdims=True)
            p = jnp.exp2(s - mx).astype(bf16)
            r = _dot(p, v_scr[j])
            outs.append(r[:, 0:LANES] / r[:, LANES:LANES + 1])
        pairs.append(jnp.where(lane < V_DIM, outs[0], outs[1]))
    attn = jnp.concatenate(pairs, axis=-1).astype(bf16)
    ao = _dot(attn, wo_ref[...])
    o_ref[...] = (cm_ref[...].astype(f32) + g_ref[...].astype(f32) * ao).astype(o_ref.dtype)


def _attn(q2d, ckv3, kpe3, cm2d, g2d, wk, wv, wo, *, n, tq=256):
    nseq, s_len, _ = ckv3.shape
    qb = n // tq
    tile = lambda w, c=0: pl.BlockSpec((tq, w), lambda s, i: (s * qb + i, c))
    return pl.pallas_call(
        _attn_kernel,
        grid=(nseq, qb),
        in_specs=[
            tile(N_HEADS * HEAD_W),
            pl.BlockSpec((None, s_len, KV_LORA), lambda s, i: (s, 0, 0)),
            pl.BlockSpec((None, s_len, LANES), lambda s, i: (s, 0, 0)),
            tile(D_MODEL),
            tile(D_MODEL, 1),
            _const_spec(wk.shape),
            _const_spec(wv.shape),
            _const_spec(wo.shape),
        ],
        out_specs=tile(D_MODEL),
        out_shape=jax.ShapeDtypeStruct((nseq * n, D_MODEL), bf16),
        scratch_shapes=[
            pltpu.VMEM((N_HEADS, s_len, HEAD_W), bf16),
            pltpu.VMEM((N_HEADS // 2, s_len, 2 * LANES), bf16),
        ],
        compiler_params=_params("parallel", "arbitrary"),
        name="attn",
    )(q2d, ckv3, kpe3, cm2d, g2d, wk, wv, wo)


def _outproj_kernel(mg_ref, x_ref, m_ref, wout_ref, n2_ref, wr_ref, x1_ref, h2_ref, lg_ref):
    g1 = m_ref[2:3, :]
    sh2 = m_ref[3:4, :]
    sc2 = m_ref[4:5, :]
    x1 = x_ref[...] + g1 * _dot(mg_ref[...], wout_ref[...])
    x1_ref[...] = x1
    h2 = _rms(x1, n2_ref[...]) * (1.0 + sc2) + sh2
    h2_ref[...] = h2.astype(h2_ref.dtype)
    lg_ref[...] = jnp.dot(h2, wr_ref[...], preferred_element_type=f32, precision=lax.Precision.HIGHEST)


def _outproj(mg2d, x2d, m3, wout, norm2, wr, *, n, tm=256):
    tokens = x2d.shape[0]
    tiles_per_seq = n // tm
    tile = lambda w: pl.BlockSpec((tm, w), lambda i: (i, 0))
    return pl.pallas_call(
        _outproj_kernel,
        grid=(tokens // tm,),
        in_specs=[
            tile(D_MODEL),
            tile(D_MODEL),
            _mod_spec(m3, lambda i: i // tiles_per_seq),
            _const_spec(wout.shape),
            _const_spec((1, D_MODEL)),
            _const_spec(wr.shape),
        ],
        out_specs=(tile(D_MODEL), tile(D_MODEL), tile(LANES)),
        out_shape=(
            jax.ShapeDtypeStruct((tokens, D_MODEL), f32),
            jax.ShapeDtypeStruct((tokens, D_MODEL), bf16),
            jax.ShapeDtypeStruct((tokens, LANES), f32),
        ),
        compiler_params=_params("parallel"),
        name="outproj",
    )(mg2d, x2d, m3, wout, norm2, wr)


def _route_kernel(lg_ref, pos_ref, aff_ref, *, nseq, n, cap):
    for s in range(nseq):
        lt = lg_ref[s].T[0:N_EXPERTS, :]
        e = jnp.exp(lt - jnp.max(lt, axis=0, keepdims=True))
        aff_ref[s * N_EXPERTS:(s + 1) * N_EXPERTS, :] = e / jnp.sum(e, axis=0, keepdims=True)
    rows = nseq * N_EXPERTS
    capf = float(cap)

    def bit_step(i, t):
        cand = t | (jnp.int32(1) << (30 - i))
        thr = lax.bitcast_convert_type(cand, f32)
        cnt = jnp.sum(jnp.where(aff_ref[...] >= thr, 1.0, 0.0), axis=1, keepdims=True)
        return jnp.where(cnt >= capf, cand, t)

    t = lax.fori_loop(0, 31, bit_step, jnp.zeros((rows, 1), jnp.int32))
    thr = lax.bitcast_convert_type(t, f32)
    need = capf - jnp.sum(jnp.where(aff_ref[...] > thr, 1.0, 0.0), axis=1, keepdims=True)

    blk = 2 * LANES
    tri = jnp.where(
        lax.broadcasted_iota(jnp.int32, (blk, blk), 0) < lax.broadcasted_iota(jnp.int32, (blk, blk), 1),
        1.0, 0.0).astype(bf16)
    carry_gt = jnp.zeros((rows, 1), f32)
    carry_eq = jnp.zeros((rows, 1), f32)
    for b in range(n // blk):
        sl = slice(b * blk, (b + 1) * blk)
        ab = aff_ref[:, sl]
        gt = ab > thr
        eq = ab == thr
        gtb = jnp.where(gt, 1.0, 0.0)
        eqb = jnp.where(eq, 1.0, 0.0)
        pre_gt = _dot(gtb.astype(bf16), tri) + carry_gt
        pre_eq = _dot(eqb.astype(bf16), tri) + carry_eq
        carry_gt = carry_gt + jnp.sum(gtb, axis=1, keepdims=True)
        carry_eq = carry_eq + jnp.sum(eqb, axis=1, keepdims=True)
        sel = gt | (eq & (pre_eq < need))
        slot = pre_gt + jnp.minimum(pre_eq, need)
        pos_ref[:, sl] = jnp.where(sel, slot, -1.0).astype(jnp.int32)


def _route(lg3, *, cap):
    nseq, n, _ = lg3.shape
    rows = nseq * N_EXPERTS
    return pl.pallas_call(
        functools.partial(_route_kernel, nseq=nseq, n=n, cap=cap),
        grid=(1,),
        in_specs=[_const_spec(lg3.shape)],
        out_specs=(_const_spec((rows, n)), _const_spec((rows, n))),
        out_shape=(jax.ShapeDtypeStruct((rows, n), jnp.int32), jax.ShapeDtypeStruct((rows, n), f32)),
        compiler_params=_params("arbitrary"),
        name="route",
    )(lg3)


def _slot_hits(pos_ref, e, cap):
    width = pos_ref.shape[1]
    return lax.broadcasted_iota(jnp.int32, (cap, width), 0) == pos_ref[e:e + 1, :]


def _one_hot(hits):
    return jnp.concatenate([jnp.where(h, 1.0, 0.0).astype(bf16) for h in hits], axis=0)


def _gather_kernel(pos_ref, aff_ref, h2_ref, xg_ref, vals_ref, *, cap, ne):
    for e0 in range(0, N_EXPERTS, ne):
        hits = [_slot_hits(pos_ref, e0 + e, cap) for e in range(ne)]
        xg = _dot(_one_hot(hits), h2_ref[...]).astype(xg_ref.dtype)
        for e in range(ne):
            xg_ref[e0 + e] = xg[e * cap:(e + 1) * cap, :]
            vals = jnp.sum(jnp.where(hits[e], aff_ref[e0 + e:e0 + e + 1, :], 0.0), axis=1, keepdims=True)
            vals_ref[e0 + e] = jnp.broadcast_to(vals, (cap, LANES))


def _gather(pos3, aff3, h23, *, cap, ne):
    nseq, _, n = pos3.shape
    return pl.pallas_call(
        functools.partial(_gather_kernel, cap=cap, ne=ne),
        grid=(nseq,),
        in_specs=[
            pl.BlockSpec((None, N_EXPERTS, n), lambda s: (s, 0, 0)),
            pl.BlockSpec((None, N_EXPERTS, n), lambda s: (s, 0, 0)),
            pl.BlockSpec((None, n, D_MODEL), lambda s: (s, 0, 0)),
        ],
        out_specs=(
            pl.BlockSpec((N_EXPERTS, cap, D_MODEL), lambda s: (0, s, 0)),
            pl.BlockSpec((N_EXPERTS, cap, LANES), lambda s: (0, s, 0)),
        ),
        out_shape=(
            jax.ShapeDtypeStruct((N_EXPERTS, nseq * cap, D_MODEL), bf16),
            jax.ShapeDtypeStruct((N_EXPERTS, nseq * cap, LANES), f32),
        ),
        compiler_params=_params("parallel"),
        name="gather",
    )(pos3, aff3, h23)


def _experts_kernel(xp_ref, vp_ref, xs_ref, vs_ref, wg_ref, wu_ref, wd_ref, yp_ref, ys_ref, *, rc):
    wg = wg_ref[...].astype(bf16)
    wu = wu_ref[...].astype(bf16)
    wd = wd_ref[...].astype(bf16)
    for x_ref, v_ref, y_ref in ((xp_ref, vp_ref, yp_ref), (xs_ref, vs_ref, ys_ref)):
        for r0 in range(0, x_ref.shape[0], rc):
            x = x_ref[r0:r0 + rc, :]
            a = _dot(x, wg)
            u = _dot(x, wu)
            hm = (a * jax.nn.sigmoid(a) * u).astype(bf16)
            y = _dot(hm, wd) * v_ref[r0:r0 + rc, 0:1]
            y_ref[r0:r0 + rc, :] = y.astype(y_ref.dtype)


def _experts(xg_p, vals_p, xg_s, vals_s, wg, wu, wd, *, rc=512):
    rp = xg_p.shape[1]
    rs = xg_s.shape[1]
    per_e = lambda r, w: pl.BlockSpec((None, r, w), lambda e: (e, 0, 0))
    return pl.pallas_call(
        functools.partial(_experts_kernel, rc=rc),
        grid=(N_EXPERTS,),
        in_specs=[
            per_e(rp, D_MODEL), per_e(rp, LANES), per_e(rs, D_MODEL), per_e(rs, LANES),
            per_e(D_MODEL, EXPERT_FF), per_e(D_MODEL, EXPERT_FF), per_e(EXPERT_FF, D_MODEL),
        ],
        out_specs=(per_e(rp, D_MODEL), per_e(rs, D_MODEL)),
        out_shape=(
            jax.ShapeDtypeStruct((N_EXPERTS, rp, D_MODEL), bf16),
            jax.ShapeDtypeStruct((N_EXPERTS, rs, D_MODEL), bf16),
        ),
        compiler_params=_params("parallel"),
        name="experts",
    )(xg_p, vals_p, xg_s, vals_s, wg, wu, wd)


def _scatter_kernel(pos_ref, y_ref, x1_ref, m_ref, fn_ref, o_ref, *, tn, cap, ne):
    moe = jnp.zeros((tn, D_MODEL), f32)
    for e0 in range(0, N_EXPERTS, ne):
        onehot = _one_hot([_slot_hits(pos_ref, e0 + e, cap) for e in range(ne)])
        y = jnp.concatenate([y_ref[e0 + e] for e in range(ne)], axis=0) if ne > 1 else y_ref[e0]
        moe = moe + lax.dot_general(onehot, y, (((0,), (0,)), ((), ())), preferred_element_type=f32)
    g2 = m_ref[5:6, :]
    xo = x1_ref[...] + g2 * moe
    o_ref[...] = _rms(xo, fn_ref[...])


def _scatter(pos3, y, x13, m3, fn, *, cap, ne, tn):
    nseq, _, n = pos3.shape
    return pl.pallas_call(
        functools.partial(_scatter_kernel, tn=tn, cap=cap, ne=ne),
        grid=(nseq, n // tn),
        in_specs=[
            pl.BlockSpec((None, N_EXPERTS, tn), lambda s, i: (s, 0, i)),
            pl.BlockSpec((N_EXPERTS, cap, D_MODEL), lambda s, i: (0, s, 0)),
            pl.BlockSpec((None, tn, D_MODEL), lambda s, i: (s, i, 0)),
            _mod_spec(m3, lambda s, i: s),
            _const_spec((1, D_MODEL)),
        ],
        out_specs=pl.BlockSpec((None, tn, D_MODEL), lambda s, i: (s, i, 0)),
        out_shape=jax.ShapeDtypeStruct((nseq, n, D_MODEL), f32),
        compiler_params=_params("parallel", "arbitrary"),
        name="scatter",
    )(pos3, y, x13, m3, fn)


def _rope_tables(n):
    rows = n // GRID_W
    r = jnp.repeat(jnp.arange(rows), GRID_W).astype(f32)
    col = jnp.tile(jnp.arange(GRID_W), rows).astype(f32)
    half = QK_ROPE // 2
    freqs = ROPE_BASE ** (-jnp.arange(0, half, 2, dtype=f32) / half)
    ang_r, ang_c = r[:, None] * freqs, col[:, None] * freqs
    cr, sr, cc, sc = jnp.cos(ang_r), jnp.sin(ang_r), jnp.cos(ang_c), jnp.sin(ang_c)
    cos = jnp.concatenate([cr, cr, cc, cc], axis=-1)
    sin = jnp.concatenate([-sr, sr, -sc, sc], axis=-1)
    lo = ROPE_OFF
    hi = HEAD_W - ROPE_OFF - QK_ROPE
    cos = jnp.concatenate([jnp.ones((n, lo), f32), cos, jnp.ones((n, hi), f32)], axis=-1)
    sin = jnp.concatenate([jnp.zeros((n, lo), f32), sin, jnp.zeros((n, hi), f32)], axis=-1)
    return cos, sin


def _rope_partner(w):
    q = QK_ROPE // 4
    return jnp.concatenate([w[..., q:2 * q], w[..., 0:q], w[..., 3 * q:4 * q], w[..., 2 * q:3 * q]], axis=-1)


def _pad_rope_block(w):
    rows = w.shape[0]
    return jnp.concatenate(
        [jnp.zeros((rows, ROPE_OFF), w.dtype), w, jnp.zeros((rows, LANES - ROPE_OFF - QK_ROPE), w.dtype)], axis=-1)


def _head_blocks(w_nope, w_rope):
    rows = w_nope.shape[0]
    if w_rope is None:
        w_rope = jnp.zeros((rows, N_HEADS, QK_ROPE), w_nope.dtype)
    z = jnp.zeros((rows, N_HEADS, HEAD_W - QK_NOPE - QK_ROPE), w_nope.dtype)
    return jnp.concatenate([w_nope, w_rope, z], axis=-1).reshape(rows, N_HEADS * HEAD_W)


def kernel(x_prompt, x_sample, cache_ckv, cache_kpe, c, c_ctx, w_ada, b_ada, norm1, w_in, conv_dw, conv_dw_b,
           conv_ln_g, conv_ln_b, w_conv_out, q_norm, w_qb, kv_norm, w_kvb, w_o_mla, w_out, norm2, w_router,
           w_e_gate, w_e_up, w_e_down, final_norm):
    assert w_ada.shape[0] == 1, "single trunk layer"
    nb_p, n_p, _ = x_prompt.shape
    nb_s, n_s, _ = x_sample.shape

    wi = w_in[0]
    s0, s1, s2, s3 = 2 * CONV_DIM, 2 * CONV_DIM + Q_LORA, 2 * CONV_DIM + Q_LORA + KV_LORA, \
        2 * CONV_DIM + Q_LORA + KV_LORA + QK_ROPE
    w_kr = wi[:, s2:s3]
    win_s = jnp.concatenate(
        [wi[:, :s2], _pad_rope_block(w_kr), wi[:, s3:], _pad_rope_block(_rope_partner(w_kr))], axis=-1).astype(bf16)
    win_p = win_s[:, :C_END]
    wq = w_qb[0].reshape(Q_LORA, N_HEADS, QK_NOPE + QK_ROPE)
    wqb = _head_blocks(wq[..., :QK_NOPE], wq[..., QK_NOPE:]).astype(bf16)
    wqbp = _head_blocks(jnp.zeros_like(wq[..., :QK_NOPE]), _rope_partner(wq[..., QK_NOPE:])).astype(bf16)
    wkv = w_kvb[0].reshape(KV_LORA, N_HEADS, QK_NOPE + V_DIM)
    wk = _head_blocks(wkv[..., :QK_NOPE], None).astype(bf16)
    wv = wkv[..., QK_NOPE:].reshape(KV_LORA, N_HEADS * V_DIM).astype(bf16)
    wco = w_conv_out[0].astype(bf16)
    wo = w_o_mla[0].astype(bf16)
    wout = w_out[0].astype(bf16)
    wr = jnp.concatenate([w_router[0], jnp.zeros((D_MODEL, LANES - N_EXPERTS), f32)], axis=-1)
    row = lambda a: a.reshape(1, -1)

    mod = jnp.concatenate([c_ctx[None, :], c, jnp.zeros((8 - 1 - nb_s, D_MODEL), f32)], axis=0)
    m = _ada(mod, w_ada[0], b_ada[0]).reshape(8, 6, D_MODEL)
    m_p, m_s = m[0:1], m[1:1 + nb_s]
    cos, sin = _rope_tables(n_s)

    def mixers(x, m3, win, rope, ctx_ckv, ctx_kpe):
        nseq, n, _ = x.shape
        x2d = x.reshape(nseq * n, D_MODEL)
        v, q, ckv, kr, g = _inproj(x2d, m3, row(norm1[0]), win, row(q_norm[0]), wqb, wqbp, row(kv_norm[0]),
                                   cos, sin, n=n, rope=rope)
        cm = _conv(v.reshape(nseq, n, CONV_DIM), g.reshape(nseq, n, 2 * D_MODEL), conv_dw[0], row(conv_dw_b[0]),
                   row(conv_ln_g[0]), row(conv_ln_b[0]), wco)
        keys_ckv = ckv.reshape(nseq, n, KV_LORA)
        keys_kpe = kr.reshape(nseq, n, LANES)
        if ctx_ckv is not None:
            keys_ckv = jnp.concatenate([ctx_ckv.astype(keys_ckv.dtype), keys_ckv], axis=1)
            keys_kpe = jnp.concatenate([ctx_kpe.astype(keys_kpe.dtype), keys_kpe], axis=1)
        mg = _attn(q, keys_ckv, keys_kpe, cm.reshape(nseq * n, D_MODEL), g, wk, wv, wo, n=n)
        x1, h2, lg = _outproj(mg, x2d, m3, wout, row(norm2[0]), wr, n=n)
        return x1, h2, lg, ckv, kr

    ctx_kpe = jnp.pad(cache_kpe[:, 0], ((0, 0), (0, 0), (ROPE_OFF, LANES - ROPE_OFF - QK_ROPE)))
    x1_p, h2_p, lg_p, ckv_p, kr_p = mixers(x_prompt, m_p, win_p, False, None, None)
    x1_s, h2_s, lg_s, _, _ = mixers(x_sample, m_s, win_s, True, cache_ckv[:, 0], ctx_kpe)

    def route_gather(h2, lg, nseq, n, ne):
        cap = EC_FACTOR * n // N_EXPERTS
        pos, aff = _route(lg.reshape(nseq, n, LANES), cap=cap)
        pos3 = pos.reshape(nseq, N_EXPERTS, n)
        xg, vals = _gather(pos3, aff.reshape(nseq, N_EXPERTS, n), h2.reshape(nseq, n, D_MODEL), cap=cap, ne=ne)
        return pos3, xg, vals, cap

    pos_p, xg_p, vals_p, cap_p = route_gather(h2_p, lg_p, nb_p, n_p, N_EXPERTS)
    pos_s, xg_s, vals_s, cap_s = route_gather(h2_s, lg_s, nb_s, n_s, 1)
    y_p, y_s = _experts(xg_p, vals_p, xg_s, vals_s, w_e_gate[0], w_e_up[0], w_e_down[0])
    fn = row(final_norm)
    y_prompt = _scatter(pos_p, y_p, x1_p.reshape(nb_p, n_p, D_MODEL), m_p, fn, cap=cap_p, ne=N_EXPERTS, tn=n_p)
    y_sample = _scatter(pos_s, y_s, x1_s.reshape(nb_s, n_s, D_MODEL), m_s, fn, cap=cap_s, ne=1, tn=512)

    new_ckv = ckv_p.reshape(nb_p, 1, n_p, KV_LORA)
    new_kpe = kr_p[:, ROPE_OFF:ROPE_OFF + QK_ROPE].reshape(nb_p, 1, n_p, QK_ROPE)
    return (y_prompt, y_sample, new_ckv, new_kpe)
```

```python
import functools

import jax
import jax.numpy as jnp
import numpy as np
from jax import lax
from jax.experimental import pallas as pl
from jax.experimental.pallas import tpu as pltpu

D_MODEL = 1024
GRID_W = 64
CONV_DIM = 512
CONV_WIDTH = 31
N_HEADS = 8
QK_NOPE = 64
QK_ROPE = 32
V_DIM = 64
Q_LORA = 256
KV_LORA = 128
N_EXPERTS = 16
EXPERT_FF = 512
EC_FACTOR = 2
ROPE_BASE = 10000.0
EPS = 1e-6

LANES = 128
HEAD_W = LANES
ROPE_OFF = QK_NOPE
CONV_HALO = 16
LOG2E = 1.4426950408889634
VMEM_LIMIT = 48 * 1024 * 1024

C_CONV = 0
C_QA = 2 * CONV_DIM
C_KVA = C_QA + Q_LORA
C_KR = C_KVA + KV_LORA
C_GATE = C_KR + LANES
C_END = C_GATE + 2 * D_MODEL
C_KRP = C_END

f32 = jnp.float32
bf16 = jnp.bfloat16


def _params(*sem):
    return pltpu.CompilerParams(dimension_semantics=sem, vmem_limit_bytes=VMEM_LIMIT)


def _dot(a, b):
    return jnp.dot(a, b, preferred_element_type=f32)


def _rms(x, g):
    return x * lax.rsqrt(jnp.mean(x * x, axis=-1, keepdims=True) + EPS) * g


def _const_spec(shape):
    nd = len(shape)
    return pl.BlockSpec(shape, lambda *_: (0,) * nd)


def _mod_spec(m3, seq_of):
    if m3.shape[0] == 1:
        return _const_spec((None, 6, D_MODEL))
    return pl.BlockSpec((None, 6, D_MODEL), lambda *idx: (seq_of(*idx), 0, 0))


def _ada_kernel(s_ref, w_ref, b_ref, o_ref):
    s = s_ref[...]
    s = s * jax.nn.sigmoid(s)
    o_ref[...] = _dot(s.astype(bf16), w_ref[...].astype(bf16)) + b_ref[...]


def _ada(mod, w_ada, b_ada):
    rows = mod.shape[0]
    n_out = w_ada.shape[1]
    tn = D_MODEL
    return pl.pallas_call(
        _ada_kernel,
        grid=(n_out // tn,),
        in_specs=[
            _const_spec((rows, D_MODEL)),
            pl.BlockSpec((D_MODEL, tn), lambda j: (0, j)),
            pl.BlockSpec((1, tn), lambda j: (0, j)),
        ],
        out_specs=pl.BlockSpec((rows, tn), lambda j: (0, j)),
        out_shape=jax.ShapeDtypeStruct((rows, n_out), f32),
        compiler_params=_params("arbitrary"),
        name="ada",
    )(mod, w_ada, b_ada.reshape(1, n_out))


def _wprep_kernel(w_ref, place_ref, o_ref):
    o_ref[:, 0:C_KR] = w_ref[:, 0:C_KR].astype(bf16)
    blk = w_ref[:, C_KR:C_KR + LANES].astype(bf16)
    placed = _dot(blk, place_ref[...])
    o_ref[:, C_KR:C_GATE] = placed[:, 0:LANES].astype(bf16)
    o_ref[:, C_KRP:C_KRP + LANES] = placed[:, LANES:2 * LANES].astype(bf16)
    o_ref[:, C_GATE:C_END] = w_ref[:, C_KR + QK_ROPE:C_KR + QK_ROPE + 2 * D_MODEL].astype(bf16)


def _wprep(w_in2d, place, *, tr=128):
    rows, cols = w_in2d.shape
    return pl.pallas_call(
        _wprep_kernel,
        grid=(rows // tr,),
        in_specs=[pl.BlockSpec((tr, cols), lambda i: (i, 0)), _const_spec(place.shape)],
        out_specs=pl.BlockSpec((tr, C_KRP + LANES), lambda i: (i, 0)),
        out_shape=jax.ShapeDtypeStruct((rows, C_KRP + LANES), bf16),
        compiler_params=_params("parallel"),
        name="wprep",
    )(w_in2d, place)


def _inproj_kernel(*refs, rope, q_scale):
    if rope:
        (x_ref, m_ref, n1_ref, win_ref, qn_ref, wqb_ref, wqbp_ref, kvn_ref, cos_ref, sin_ref,
         v_ref, q_ref, ckv_ref, kr_ref, g_ref) = refs
    else:
        (x_ref, m_ref, n1_ref, win_ref, qn_ref, wqb_ref, kvn_ref,
         v_ref, q_ref, ckv_ref, kr_ref, g_ref, kpe_ref) = refs
    sh1 = m_ref[0:1, :]
    sc1 = m_ref[1:2, :]
    h = _rms(x_ref[...], n1_ref[...]) * (1.0 + sc1) + sh1
    hb = h.astype(bf16)

    a = _dot(hb, win_ref[:, C_CONV:C_CONV + CONV_DIM])
    b = _dot(hb, win_ref[:, C_CONV + CONV_DIM:C_QA])
    v_ref[...] = (a * jax.nn.sigmoid(b)).astype(v_ref.dtype)

    qa = _dot(hb, win_ref[:, C_QA:C_KVA])
    qn = _rms(qa, qn_ref[...]).astype(bf16)
    q = _dot(qn, wqb_ref[...])
    if rope:
        qp = _dot(qn, wqbp_ref[...])
        cos = cos_ref[...]
        sin = sin_ref[...]
        for hd in range(N_HEADS):
            sl = slice(hd * HEAD_W, (hd + 1) * HEAD_W)
            q_ref[:, sl] = ((q[:, sl] * cos + qp[:, sl] * sin) * q_scale).astype(q_ref.dtype)
    else:
        q_ref[...] = (q * q_scale).astype(q_ref.dtype)

    kva = _dot(hb, win_ref[:, C_KVA:C_KR])
    ckv_ref[...] = _rms(kva, kvn_ref[...]).astype(ckv_ref.dtype)

    kr = _dot(hb, win_ref[:, C_KR:C_GATE])
    if rope:
        krp = _dot(hb, win_ref[:, C_KRP:C_KRP + LANES])
        kr = kr * cos_ref[...] + krp * sin_ref[...]
    else:
        kpe_ref[...] = kr[:, ROPE_OFF:ROPE_OFF + QK_ROPE]
    kr_ref[...] = kr.astype(kr_ref.dtype)

    gw = 512
    for j in range(2 * D_MODEL // gw):
        gl = _dot(hb, win_ref[:, C_GATE + j * gw:C_GATE + (j + 1) * gw])
        g_ref[:, j * gw:(j + 1) * gw] = jax.nn.sigmoid(gl).astype(g_ref.dtype)


def _inproj(x2d, m3, norm1, win, q_norm, wqb, wqbp, kv_norm, cos, sin, *, n, rope, tm=256):
    tokens = x2d.shape[0]
    tiles_per_seq = n // tm
    q_scale = float((QK_NOPE + QK_ROPE) ** -0.5 * LOG2E)
    tile = lambda w: pl.BlockSpec((tm, w), lambda i: (i, 0))
    in_specs = [
        tile(D_MODEL),
        _mod_spec(m3, lambda i: i // tiles_per_seq),
        _const_spec((1, D_MODEL)),
        _const_spec((D_MODEL, C_KRP + LANES if rope else C_END)),
        _const_spec((1, Q_LORA)),
        _const_spec(wqb.shape),
    ]
    args = [x2d, m3, norm1, win, q_norm, wqb]
    if rope:
        in_specs.append(_const_spec(wqbp.shape))
        args.append(wqbp)
    in_specs.append(_const_spec((1, KV_LORA)))
    args.append(kv_norm)
    if rope:
        tab = pl.BlockSpec((tm, LANES), lambda i: (i % tiles_per_seq, 0))
        in_specs += [tab, tab]
        args += [cos, sin]
    out_shape = [
        jax.ShapeDtypeStruct((tokens, CONV_DIM), bf16),
        jax.ShapeDtypeStruct((tokens, N_HEADS * HEAD_W), bf16),
        jax.ShapeDtypeStruct((tokens, KV_LORA), bf16 if rope else f32),
        jax.ShapeDtypeStruct((tokens, LANES), bf16),
        jax.ShapeDtypeStruct((tokens, 2 * D_MODEL), bf16),
    ]
    out_specs = [tile(CONV_DIM), tile(N_HEADS * HEAD_W), tile(KV_LORA), tile(LANES), tile(2 * D_MODEL)]
    if not rope:
        out_shape.append(jax.ShapeDtypeStruct((tokens, QK_ROPE), f32))
        out_specs.append(tile(QK_ROPE))
    return pl.pallas_call(
        functools.partial(_inproj_kernel, rope=rope, q_scale=q_scale),
        grid=(tokens // tm,),
        in_specs=in_specs,
        out_specs=out_specs,
        out_shape=out_shape,
        compiler_params=_params("parallel"),
        name="inproj_rope" if rope else "inproj",
    )(*args)


def _conv_kernel(v_ref, g_ref, dw_ref, dwb_ref, lng_ref, lnb_ref, wco_ref, o_ref, vpad, ybuf, *, n, rt, ct):
    pad = CONV_WIDTH // 2
    zeros = jnp.zeros((CONV_HALO, CONV_DIM), f32)
    vpad[0:CONV_HALO, :] = zeros
    vpad[CONV_HALO + n:2 * CONV_HALO + n, :] = zeros
    vpad[CONV_HALO:CONV_HALO + n, :] = v_ref[...].astype(f32)

    sub = 8
    span = ((CONV_HALO - pad + CONV_WIDTH - 1) // sub) * sub

    def conv_chunk(c, carry):
        r0 = pl.multiple_of(c * ct, ct)
        for cb in range(CONV_DIM // LANES):
            sl = slice(cb * LANES, (cb + 1) * LANES)
            win = vpad[pl.ds(r0, ct + 2 * CONV_HALO), sl]
            acc = jnp.zeros((ct, LANES), f32)
            for ph in range(sub):
                wph = win[ph:ph + ct + span, :]
                for a in range(span // sub + 1):
                    k = a * sub + ph - (CONV_HALO - pad)
                    if 0 <= k < CONV_WIDTH:
                        acc = acc + wph[a * sub:a * sub + ct, :] * dw_ref[k:k + 1, sl]
            ybuf[pl.ds(r0, ct), sl] = acc + dwb_ref[:, sl]
        return carry

    lax.fori_loop(0, n // ct, conv_chunk, 0)

    def chunk(c, carry):
        r0 = pl.multiple_of(c * rt, rt)
        y = ybuf[pl.ds(r0, rt), :]
        mu = jnp.mean(y, axis=-1, keepdims=True)
        yc = y - mu
        var = jnp.mean(yc * yc, axis=-1, keepdims=True)
        z = yc * lax.rsqrt(var + EPS) * lng_ref[...] + lnb_ref[...]
        z = z * jax.nn.sigmoid(z)
        co = _dot(z.astype(bf16), wco_ref[...])
        o_ref[pl.ds(r0, rt), :] = (g_ref[pl.ds(r0, rt), :].astype(f32) * co).astype(o_ref.dtype)
        return carry

    lax.fori_loop(0, n // rt, chunk, 0)


def _conv(v3, g3, dw, dwb, lng, lnb, wco, *, rt=256, ct=64):
    nseq, n, _ = v3.shape
    return pl.pallas_call(
        functools.partial(_conv_kernel, n=n, rt=rt, ct=ct),
        grid=(nseq,),
        in_specs=[
            pl.BlockSpec((None, n, CONV_DIM), lambda s: (s, 0, 0)),
            pl.BlockSpec((None, n, D_MODEL), lambda s: (s, 0, 0)),
            _const_spec(dw.shape),
            _const_spec((1, CONV_DIM)),
            _const_spec((1, CONV_DIM)),
            _const_spec((1, CONV_DIM)),
            _const_spec(wco.shape),
        ],
        out_specs=pl.BlockSpec((None, n, D_MODEL), lambda s: (s, 0, 0)),
        out_shape=jax.ShapeDtypeStruct((nseq, n, D_MODEL), bf16),
        scratch_shapes=[
            pltpu.VMEM((n + 2 * CONV_HALO, CONV_DIM), f32),
            pltpu.VMEM((n, CONV_DIM), f32),
        ],
        compiler_params=_params("parallel"),
        name="conv",
    )(v3, g3, dw, dwb, lng, lnb, wco)


def _attn_kernel(q_ref, ckv_ref, kpe_ref, cm_ref, g_ref, wk_ref, wv_ref, wo_ref, o_ref, k_scr, v_scr):
    @pl.when(pl.program_id(1) == 0)
    def _():
        ckv = ckv_ref[...].astype(bf16)
        kpe = kpe_ref[...].astype(f32)
        for hd in range(N_HEADS):
            k = _dot(ckv, wk_ref[:, hd * HEAD_W:(hd + 1) * HEAD_W]) + kpe
            k_scr[hd] = k.astype(bf16)
        v = _dot(ckv, wv_ref[...])
        ones = jnp.ones((v.shape[0], LANES), bf16)
        for j in range(N_HEADS // 2):
            v_scr[j, :, 0:LANES] = v[:, j * LANES:(j + 1) * LANES].astype(bf16)
            v_scr[j, :, LANES:2 * LANES] = ones

    tq = q_ref.shape[0]
    lane = lax.broadcasted_iota(jnp.int32, (tq, LANES), 1)
    pairs = []
    for j in range(N_HEADS // 2):
        outs = []
        for hh in range(2):
            hd = 2 * j + hh
            qh = q_ref[:, hd * HEAD_W:(hd + 1) * HEAD_W]
            s = lax.dot_general(qh, k_scr[hd], (((1,), (1,)), ((), ())), preferred_element_type=f32)
            mx = jnp.max(s, axis=-1, keepdims=True)
            p = jnp.exp2(s - mx).astype(bf16)
            r = _dot(p, v_scr[j])
            outs.append(r[:, 0:LANES] / r[:, LANES:LANES + 1])
        pairs.append(jnp.where(lane < V_DIM, outs[0], outs[1]))
    attn = jnp.concatenate(pairs, axis=-1).astype(bf16)
    ao = _dot(attn, wo_ref[...])
    o_ref[...] = (cm_ref[...].astype(f32) + g_ref[...].astype(f32) * ao).astype(o_ref.dtype)


def _attn(q2d, ckv3, kpe3, cm2d, g2d, wk, wv, wo, *, n, tq=256):
    nseq, s_len, _ = ckv3.shape
    qb = n // tq
    tile = lambda w, c=0: pl.BlockSpec((tq, w), lambda s, i: (s * qb + i, c))
    return pl.pallas_call(
        _attn_kernel,
        grid=(nseq, qb),
        in_specs=[
            tile(N_HEADS * HEAD_W),
            pl.BlockSpec((None, s_len, KV_LORA), lambda s, i: (s, 0, 0)),
            pl.BlockSpec((None, s_len, LANES), lambda s, i: (s, 0, 0)),
            tile(D_MODEL),
            tile(D_MODEL, 1),
            _const_spec(wk.shape),
            _const_spec(wv.shape),
            _const_spec(wo.shape),
        ],
        out_specs=tile(D_MODEL),
        out_shape=jax.ShapeDtypeStruct((nseq * n, D_MODEL), bf16),
        scratch_shapes=[
            pltpu.VMEM((N_HEADS, s_len, HEAD_W), bf16),
            pltpu.VMEM((N_HEADS // 2, s_len, 2 * LANES), bf16),
        ],
        compiler_params=_params("parallel", "arbitrary"),
        name="attn",
    )(q2d, ckv3, kpe3, cm2d, g2d, wk, wv, wo)


def _outproj_kernel(mg_ref, x_ref, m_ref, wout_ref, n2_ref, wr_ref, x1_ref, h2_ref, lg_ref):
    g1 = m_ref[2:3, :]
    sh2 = m_ref[3:4, :]
    sc2 = m_ref[4:5, :]
    x1 = x_ref[...] + g1 * _dot(mg_ref[...], wout_ref[...])
    x1_ref[...] = x1
    h2 = _rms(x1, n2_ref[...]) * (1.0 + sc2) + sh2
    hi = h2.astype(bf16)
    h2_ref[...] = hi
    lo = (h2 - hi.astype(f32)).astype(bf16)
    s = _dot(hi, wr_ref[...]) + _dot(lo, wr_ref[...])
    lg_ref[...] = s + pltpu.roll(s, LANES - N_EXPERTS, axis=1)


def _outproj(mg2d, x2d, m3, wout, norm2, wr, *, n, tm=256):
    tokens = x2d.shape[0]
    tiles_per_seq = n // tm
    tile = lambda w: pl.BlockSpec((tm, w), lambda i: (i, 0))
    return pl.pallas_call(
        _outproj_kernel,
        grid=(tokens // tm,),
        in_specs=[
            tile(D_MODEL),
            tile(D_MODEL),
            _mod_spec(m3, lambda i: i // tiles_per_seq),
            _const_spec(wout.shape),
            _const_spec((1, D_MODEL)),
            _const_spec(wr.shape),
        ],
        out_specs=(tile(D_MODEL), tile(D_MODEL), tile(LANES)),
        out_shape=(
            jax.ShapeDtypeStruct((tokens, D_MODEL), f32),
            jax.ShapeDtypeStruct((tokens, D_MODEL), bf16),
            jax.ShapeDtypeStruct((tokens, LANES), f32),
        ),
        compiler_params=_params("parallel"),
        name="outproj",
    )(mg2d, x2d, m3, wout, norm2, wr)


def _route_kernel(lg_ref, pos_ref, aff_ref, *, nseq, n, cap):
    for s in range(nseq):
        lt = lg_ref[s].T[0:N_EXPERTS, :]
        e = jnp.exp(lt - jnp.max(lt, axis=0, keepdims=True))
        aff_ref[s * N_EXPERTS:(s + 1) * N_EXPERTS, :] = e / jnp.sum(e, axis=0, keepdims=True)
    rows = nseq * N_EXPERTS
    capf = float(cap)

    def bit_step(i, t):
        cand = t | (jnp.int32(1) << (30 - i))
        thr = lax.bitcast_convert_type(cand, f32)
        cnt = jnp.sum(jnp.where(aff_ref[...] >= thr, 1.0, 0.0), axis=1, keepdims=True)
        return jnp.where(cnt >= capf, cand, t)

    t = lax.fori_loop(0, 31, bit_step, jnp.zeros((rows, 1), jnp.int32))
    thr = lax.bitcast_convert_type(t, f32)
    need = capf - jnp.sum(jnp.where(aff_ref[...] > thr, 1.0, 0.0), axis=1, keepdims=True)

    blk = 2 * LANES
    tri = jnp.where(
        lax.broadcasted_iota(jnp.int32, (blk, blk), 0) < lax.broadcasted_iota(jnp.int32, (blk, blk), 1),
        1.0, 0.0).astype(bf16)
    carry_gt = jnp.zeros((rows, 1), f32)
    carry_eq = jnp.zeros((rows, 1), f32)
    for b in range(n // blk):
        sl = slice(b * blk, (b + 1) * blk)
        ab = aff_ref[:, sl]
        gt = ab > thr
        eq = ab == thr
        gtb = jnp.where(gt, 1.0, 0.0)
        eqb = jnp.where(eq, 1.0, 0.0)
        pre_gt = _dot(gtb.astype(bf16), tri) + carry_gt
        pre_eq = _dot(eqb.astype(bf16), tri) + carry_eq
        carry_gt = carry_gt + jnp.sum(gtb, axis=1, keepdims=True)
        carry_eq = carry_eq + jnp.sum(eqb, axis=1, keepdims=True)
        sel = gt | (eq & (pre_eq < need))
        slot = pre_gt + jnp.minimum(pre_eq, need)
        pos_ref[:, sl] = jnp.where(sel, slot, -1.0).astype(jnp.int32)


def _route(lg3, *, cap):
    nseq, n, _ = lg3.shape
    rows = nseq * N_EXPERTS
    return pl.pallas_call(
        functools.partial(_route_kernel, nseq=nseq, n=n, cap=cap),
        grid=(1,),
        in_specs=[_const_spec(lg3.shape)],
        out_specs=(_const_spec((rows, n)), _const_spec((rows, n))),
        out_shape=(jax.ShapeDtypeStruct((rows, n), jnp.int32), jax.ShapeDtypeStruct((rows, n), f32)),
        compiler_params=_params("arbitrary"),
        name="route",
    )(lg3)


def _slot_hits(pos_ref, e, cap):
    width = pos_ref.shape[1]
    return lax.broadcasted_iota(jnp.int32, (cap, width), 0) == pos_ref[e:e + 1, :]


def _one_hot(hits):
    return jnp.concatenate([jnp.where(h, 1.0, 0.0).astype(bf16) for h in hits], axis=0)


def _gather_kernel(pos_ref, aff_ref, h2_ref, xg_ref, vals_ref, *, cap, ne):
    for e0 in range(0, N_EXPERTS, ne):
        hits = [_slot_hits(pos_ref, e0 + e, cap) for e in range(ne)]
        xg = _dot(_one_hot(hits), h2_ref[...]).astype(xg_ref.dtype)
        for e in range(ne):
            xg_ref[e0 + e] = xg[e * cap:(e + 1) * cap, :]
            vals = jnp.sum(jnp.where(hits[e], aff_ref[e0 + e:e0 + e + 1, :], 0.0), axis=1, keepdims=True)
            vals_ref[e0 + e] = jnp.broadcast_to(vals, (cap, LANES))


def _gather(pos3, aff3, h23, *, cap, ne):
    nseq, _, n = pos3.shape
    return pl.pallas_call(
        functools.partial(_gather_kernel, cap=cap, ne=ne),
        grid=(nseq,),
        in_specs=[
            pl.BlockSpec((None, N_EXPERTS, n), lambda s: (s, 0, 0)),
            pl.BlockSpec((None, N_EXPERTS, n), lambda s: (s, 0, 0)),
            pl.BlockSpec((None, n, D_MODEL), lambda s: (s, 0, 0)),
        ],
        out_specs=(
            pl.BlockSpec((N_EXPERTS, cap, D_MODEL), lambda s: (0, s, 0)),
            pl.BlockSpec((N_EXPERTS, cap, LANES), lambda s: (0, s, 0)),
        ),
        out_shape=(
            jax.ShapeDtypeStruct((N_EXPERTS, nseq * cap, D_MODEL), bf16),
            jax.ShapeDtypeStruct((N_EXPERTS, nseq * cap, LANES), f32),
        ),
        compiler_params=_params("parallel"),
        name="gather",
    )(pos3, aff3, h23)


def _experts_kernel(xp_ref, vp_ref, xs_ref, vs_ref, wg_ref, wu_ref, wd_ref, yp_ref, ys_ref, *, rc):
    wg = wg_ref[...].astype(bf16)
    wu = wu_ref[...].astype(bf16)
    wd = wd_ref[...].astype(bf16)
    for x_ref, v_ref, y_ref in ((xp_ref, vp_ref, yp_ref), (xs_ref, vs_ref, ys_ref)):
        for r0 in range(0, x_ref.shape[0], rc):
            x = x_ref[r0:r0 + rc, :]
            a = _dot(x, wg)
            u = _dot(x, wu)
            hm = (a * jax.nn.sigmoid(a) * u).astype(bf16)
            y = _dot(hm, wd) * v_ref[r0:r0 + rc, 0:1]
            y_ref[r0:r0 + rc, :] = y.astype(y_ref.dtype)


def _experts(xg_p, vals_p, xg_s, vals_s, wg, wu, wd, *, rc=512):
    rp = xg_p.shape[1]
    rs = xg_s.shape[1]
    per_e = lambda r, w: pl.BlockSpec((None, r, w), lambda e: (e, 0, 0))
    return pl.pallas_call(
        functools.partial(_experts_kernel, rc=rc),
        grid=(N_EXPERTS,),
        in_specs=[
            per_e(rp, D_MODEL), per_e(rp, LANES), per_e(rs, D_MODEL), per_e(rs, LANES),
            per_e(D_MODEL, EXPERT_FF), per_e(D_MODEL, EXPERT_FF), per_e(EXPERT_FF, D_MODEL),
        ],
        out_specs=(per_e(rp, D_MODEL), per_e(rs, D_MODEL)),
        out_shape=(
            jax.ShapeDtypeStruct((N_EXPERTS, rp, D_MODEL), bf16),
            jax.ShapeDtypeStruct((N_EXPERTS, rs, D_MODEL), bf16),
        ),
        compiler_params=_params("parallel"),
        name="experts",
    )(xg_p, vals_p, xg_s, vals_s, wg, wu, wd)


def _scatter_kernel(pos_ref, y_ref, x1_ref, m_ref, fn_ref, o_ref, *, tn, cap, ne):
    moe = jnp.zeros((tn, D_MODEL), f32)
    for e0 in range(0, N_EXPERTS, ne):
        onehot = _one_hot([_slot_hits(pos_ref, e0 + e, cap) for e in range(ne)])
        y = jnp.concatenate([y_ref[e0 + e] for e in range(ne)], axis=0) if ne > 1 else y_ref[e0]
        moe = moe + lax.dot_general(onehot, y, (((0,), (0,)), ((), ())), preferred_element_type=f32)
    g2 = m_ref[5:6, :]
    xo = x1_ref[...] + g2 * moe
    o_ref[...] = _rms(xo, fn_ref[...])


def _scatter(pos3, y, x13, m3, fn, *, cap, ne, tn):
    nseq, _, n = pos3.shape
    return pl.pallas_call(
        functools.partial(_scatter_kernel, tn=tn, cap=cap, ne=ne),
        grid=(nseq, n // tn),
        in_specs=[
            pl.BlockSpec((None, N_EXPERTS, tn), lambda s, i: (s, 0, i)),
            pl.BlockSpec((N_EXPERTS, cap, D_MODEL), lambda s, i: (0, s, 0)),
            pl.BlockSpec((None, tn, D_MODEL), lambda s, i: (s, i, 0)),
            _mod_spec(m3, lambda s, i: s),
            _const_spec((1, D_MODEL)),
        ],
        out_specs=pl.BlockSpec((None, tn, D_MODEL), lambda s, i: (s, i, 0)),
        out_shape=jax.ShapeDtypeStruct((nseq, n, D_MODEL), f32),
        compiler_params=_params("parallel", "arbitrary"),
        name="scatter",
    )(pos3, y, x13, m3, fn)


def _rope_tables(n):
    t = np.arange(n)
    half = QK_ROPE // 2
    freqs = ROPE_BASE ** (-np.arange(0, half, 2, dtype=np.float64) / half)
    ang_r = (t // GRID_W)[:, None] * freqs
    ang_c = (t % GRID_W)[:, None] * freqs
    cr, sr, cc, sc = np.cos(ang_r), np.sin(ang_r), np.cos(ang_c), np.sin(ang_c)
    cos = np.ones((n, HEAD_W))
    sin = np.zeros((n, HEAD_W))
    cos[:, ROPE_OFF:ROPE_OFF + QK_ROPE] = np.concatenate([cr, cr, cc, cc], axis=-1)
    sin[:, ROPE_OFF:ROPE_OFF + QK_ROPE] = np.concatenate([-sr, sr, -sc, sc], axis=-1)
    return jnp.asarray(cos, f32), jnp.asarray(sin, f32)


_PARTNER = np.concatenate([np.arange(8, 16), np.arange(0, 8), np.arange(24, 32), np.arange(16, 24)])


def _rope_partner(w):
    q = QK_ROPE // 4
    return jnp.concatenate([w[..., q:2 * q], w[..., 0:q], w[..., 3 * q:4 * q], w[..., 2 * q:3 * q]], axis=-1)


def _rope_placement():
    place = np.zeros((LANES, 2 * LANES), np.float32)
    d = np.arange(QK_ROPE)
    place[d, ROPE_OFF + d] = 1.0
    place[_PARTNER, LANES + ROPE_OFF + d] = 1.0
    return jnp.asarray(place, bf16)


def _head_blocks(w_nope, w_rope):
    rows = w_nope.shape[0]
    if w_rope is None:
        w_rope = jnp.zeros((rows, N_HEADS, QK_ROPE), w_nope.dtype)
    z = jnp.zeros((rows, N_HEADS, HEAD_W - QK_NOPE - QK_ROPE), w_nope.dtype)
    return jnp.concatenate([w_nope, w_rope, z], axis=-1).reshape(rows, N_HEADS * HEAD_W)


def kernel(x_prompt, x_sample, cache_ckv, cache_kpe, c, c_ctx, w_ada, b_ada, norm1, w_in, conv_dw, conv_dw_b,
           conv_ln_g, conv_ln_b, w_conv_out, q_norm, w_qb, kv_norm, w_kvb, w_o_mla, w_out, norm2, w_router,
           w_e_gate, w_e_up, w_e_down, final_norm):
    assert w_ada.shape[0] == 1, "single trunk layer"
    nb_p, n_p, _ = x_prompt.shape
    nb_s, n_s, _ = x_sample.shape

    win = _wprep(w_in[0], _rope_placement())
    wq = w_qb[0].reshape(Q_LORA, N_HEADS, QK_NOPE + QK_ROPE)
    wqb = _head_blocks(wq[..., :QK_NOPE], wq[..., QK_NOPE:]).astype(bf16)
    wqbp = _head_blocks(jnp.zeros_like(wq[..., :QK_NOPE]), _rope_partner(wq[..., QK_NOPE:])).astype(bf16)
    wkv = w_kvb[0].reshape(KV_LORA, N_HEADS, QK_NOPE + V_DIM)
    wk = _head_blocks(wkv[..., :QK_NOPE], None).astype(bf16)
    wv = wkv[..., QK_NOPE:].reshape(KV_LORA, N_HEADS * V_DIM).astype(bf16)
    wco = w_conv_out[0].astype(bf16)
    wo = w_o_mla[0].astype(bf16)
    wout = w_out[0].astype(bf16)
    wr_hi = w_router[0].astype(bf16)
    wr_lo = (w_router[0] - wr_hi.astype(f32)).astype(bf16)
    wr = jnp.concatenate([wr_hi, wr_lo, jnp.zeros((D_MODEL, LANES - 2 * N_EXPERTS), bf16)], axis=-1)
    row = lambda a: a.reshape(1, -1)

    mod = jnp.concatenate([c_ctx[None, :], c, jnp.zeros((8 - 1 - nb_s, D_MODEL), f32)], axis=0)
    m = _ada(mod, w_ada[0], b_ada[0]).reshape(8, 6, D_MODEL)
    m_p, m_s = m[0:1], m[1:1 + nb_s]
    cos, sin = _rope_tables(n_s)

    def mixers(x, m3, rope, ctx_ckv, ctx_kpe):
        nseq, n, _ = x.shape
        x2d = x.reshape(nseq * n, D_MODEL)
        v, q, ckv, kr, g, *kpe = _inproj(x2d, m3, row(norm1[0]), win, row(q_norm[0]), wqb, wqbp, row(kv_norm[0]),
                                         cos, sin, n=n, rope=rope)
        cm = _conv(v.reshape(nseq, n, CONV_DIM), g.reshape(nseq, n, 2 * D_MODEL), conv_dw[0], row(conv_dw_b[0]),
                   row(conv_ln_g[0]), row(conv_ln_b[0]), wco)
        keys_ckv = ckv.reshape(nseq, n, KV_LORA)
        keys_kpe = kr.reshape(nseq, n, LANES)
        if ctx_ckv is not None:
            keys_ckv = jnp.concatenate([ctx_ckv.astype(keys_ckv.dtype), keys_ckv], axis=1)
            keys_kpe = jnp.concatenate([ctx_kpe.astype(keys_kpe.dtype), keys_kpe], axis=1)
        mg = _attn(q, keys_ckv, keys_kpe, cm.reshape(nseq * n, D_MODEL), g, wk, wv, wo, n=n)
        x1, h2, lg = _outproj(mg, x2d, m3, wout, row(norm2[0]), wr, n=n)
        return x1, h2, lg, ckv, kpe

    ctx_kpe = jnp.pad(cache_kpe[:, 0], ((0, 0), (0, 0), (ROPE_OFF, LANES - ROPE_OFF - QK_ROPE)))
    x1_p, h2_p, lg_p, ckv_p, (kpe_p,) = mixers(x_prompt, m_p, False, None, None)
    x1_s, h2_s, lg_s, _, _ = mixers(x_sample, m_s, True, cache_ckv[:, 0], ctx_kpe)

    def route_gather(h2, lg, nseq, n, ne):
        cap = EC_FACTOR * n // N_EXPERTS
        pos, aff = _route(lg.reshape(nseq, n, LANES), cap=cap)
        pos3 = pos.reshape(nseq, N_EXPERTS, n)
        xg, vals = _gather(pos3, aff.reshape(nseq, N_EXPERTS, n), h2.reshape(nseq, n, D_MODEL), cap=cap, ne=ne)
        return pos3, xg, vals, cap

    pos_p, xg_p, vals_p, cap_p = route_gather(h2_p, lg_p, nb_p, n_p, N_EXPERTS)
    pos_s, xg_s, vals_s, cap_s = route_gather(h2_s, lg_s, nb_s, n_s, 1)
    y_p, y_s = _experts(xg_p, vals_p, xg_s, vals_s, w_e_gate[0], w_e_up[0], w_e_down[0])
    fn = row(final_norm)
    y_prompt = _scatter(pos_p, y_p, x1_p.reshape(nb_p, n_p, D_MODEL), m_p, fn, cap=cap_p, ne=N_EXPERTS, tn=n_p)
    y_sample = _scatter(pos_s, y_s, x1_s.reshape(nb_s, n_s, D_MODEL), m_s, fn, cap=cap_s, ne=1, tn=512)

    new_ckv = ckv_p.reshape(nb_p, 1, n_p, KV_LORA)
    new_kpe = kpe_p.reshape(nb_p, 1, n_p, QK_ROPE)
    return (y_prompt, y_sample, new_ckv, new_kpe)
```

```python
import functools

import jax
import jax.numpy as jnp
import numpy as np
from jax import lax
from jax.experimental import pallas as pl
from jax.experimental.pallas import tpu as pltpu

D_MODEL = 1024
GRID_W = 64
CONV_DIM = 512
CONV_WIDTH = 31
N_HEADS = 8
QK_NOPE = 64
QK_ROPE = 32
V_DIM = 64
Q_LORA = 256
KV_LORA = 128
N_EXPERTS = 16
EXPERT_FF = 512
EC_FACTOR = 2
ROPE_BASE = 10000.0
EPS = 1e-6

LANES = 128
HEAD_W = LANES
ROPE_OFF = QK_NOPE
CONV_HALO = 16
LOG2E = 1.4426950408889634
VMEM_LIMIT = 48 * 1024 * 1024

C_CONV = 0
C_QA = 2 * CONV_DIM
C_KVA = C_QA + Q_LORA
C_KR = C_KVA + KV_LORA
C_GATE = C_KR + LANES
C_END = C_GATE + 2 * D_MODEL
C_KRP = C_END

f32 = jnp.float32
bf16 = jnp.bfloat16


def _params(*sem):
    return pltpu.CompilerParams(dimension_semantics=sem, vmem_limit_bytes=VMEM_LIMIT)


def _dot(a, b):
    return jnp.dot(a, b, preferred_element_type=f32)


def _rms(x, g):
    return x * lax.rsqrt(jnp.mean(x * x, axis=-1, keepdims=True) + EPS) * g


def _const_spec(shape):
    nd = len(shape)
    return pl.BlockSpec(shape, lambda *_: (0,) * nd)


def _mod_spec(m3, seq_of):
    if m3.shape[0] == 1:
        return _const_spec((None, 6, D_MODEL))
    return pl.BlockSpec((None, 6, D_MODEL), lambda *idx: (seq_of(*idx), 0, 0))


def _ada_kernel(s_ref, w_ref, b_ref, o_ref):
    s = s_ref[...]
    s = s * jax.nn.sigmoid(s)
    o_ref[...] = _dot(s.astype(bf16), w_ref[...].astype(bf16)) + b_ref[...]


def _ada(mod, w_ada, b_ada):
    rows = mod.shape[0]
    n_out = w_ada.shape[1]
    tn = D_MODEL
    return pl.pallas_call(
        _ada_kernel,
        grid=(n_out // tn,),
        in_specs=[
            _const_spec((rows, D_MODEL)),
            pl.BlockSpec((D_MODEL, tn), lambda j: (0, j)),
            pl.BlockSpec((1, tn), lambda j: (0, j)),
        ],
        out_specs=pl.BlockSpec((rows, tn), lambda j: (0, j)),
        out_shape=jax.ShapeDtypeStruct((rows, n_out), f32),
        compiler_params=_params("arbitrary"),
        name="ada",
    )(mod, w_ada, b_ada.reshape(1, n_out))


def _wprep_kernel(wt_ref, place_ref, o_ref):
    def block(r0):
        return wt_ref[r0:r0 + LANES, :].T.astype(bf16)

    for j in range(C_KR // LANES):
        o_ref[:, j * LANES:(j + 1) * LANES] = block(j * LANES)
    placed = _dot(block(C_KR), place_ref[...])
    o_ref[:, C_KR:C_GATE] = placed[:, 0:LANES].astype(bf16)
    o_ref[:, C_KRP:C_KRP + LANES] = placed[:, LANES:2 * LANES].astype(bf16)
    for j in range(2 * D_MODEL // LANES):
        o_ref[:, C_GATE + j * LANES:C_GATE + (j + 1) * LANES] = block(C_KR + QK_ROPE + j * LANES)


def _wprep(w_in_t, place, *, tr=LANES):
    cols, rows = w_in_t.shape
    return pl.pallas_call(
        _wprep_kernel,
        grid=(rows // tr,),
        in_specs=[pl.BlockSpec((cols, tr), lambda i: (0, i)), _const_spec(place.shape)],
        out_specs=pl.BlockSpec((tr, C_KRP + LANES), lambda i: (i, 0)),
        out_shape=jax.ShapeDtypeStruct((rows, C_KRP + LANES), bf16),
        compiler_params=_params("parallel"),
        name="wprep",
    )(w_in_t, place)


def _inproj_kernel(*refs, rope, q_scale):
    if rope:
        (x_ref, m_ref, n1_ref, win_ref, qn_ref, wqb_ref, wqbp_ref, kvn_ref, cos_ref, sin_ref,
         v_ref, q_ref, ckv_ref, kr_ref, g_ref) = refs
    else:
        (x_ref, m_ref, n1_ref, win_ref, qn_ref, wqb_ref, kvn_ref,
         v_ref, q_ref, ckv_ref, kr_ref, g_ref, kpe_ref) = refs
    sh1 = m_ref[0:1, :]
    sc1 = m_ref[1:2, :]
    h = _rms(x_ref[...], n1_ref[...]) * (1.0 + sc1) + sh1
    hb = h.astype(bf16)

    a = _dot(hb, win_ref[:, C_CONV:C_CONV + CONV_DIM])
    b = _dot(hb, win_ref[:, C_CONV + CONV_DIM:C_QA])
    v_ref[...] = (a * jax.nn.sigmoid(b)).astype(v_ref.dtype)

    qa = _dot(hb, win_ref[:, C_QA:C_KVA])
    qn = _rms(qa, qn_ref[...]).astype(bf16)
    q = _dot(qn, wqb_ref[...])
    if rope:
        qp = _dot(qn, wqbp_ref[...])
        cos = cos_ref[...]
        sin = sin_ref[...]
        for hd in range(N_HEADS):
            sl = slice(hd * HEAD_W, (hd + 1) * HEAD_W)
            q_ref[:, sl] = ((q[:, sl] * cos + qp[:, sl] * sin) * q_scale).astype(q_ref.dtype)
    else:
        q_ref[...] = (q * q_scale).astype(q_ref.dtype)

    kva = _dot(hb, win_ref[:, C_KVA:C_KR])
    ckv_ref[...] = _rms(kva, kvn_ref[...]).astype(ckv_ref.dtype)

    kr = _dot(hb, win_ref[:, C_KR:C_GATE])
    if rope:
        krp = _dot(hb, win_ref[:, C_KRP:C_KRP + LANES])
        kr = kr * cos_ref[...] + krp * sin_ref[...]
    else:
        kpe_ref[...] = kr[:, ROPE_OFF:ROPE_OFF + QK_ROPE]
    kr_ref[...] = kr.astype(kr_ref.dtype)

    gw = 512
    for j in range(2 * D_MODEL // gw):
        gl = _dot(hb, win_ref[:, C_GATE + j * gw:C_GATE + (j + 1) * gw])
        g_ref[:, j * gw:(j + 1) * gw] = jax.nn.sigmoid(gl).astype(g_ref.dtype)


def _inproj(x2d, m3, norm1, win, q_norm, wqb, wqbp, kv_norm, cos, sin, *, n, rope, tm=256):
    tokens = x2d.shape[0]
    tiles_per_seq = n // tm
    q_scale = float((QK_NOPE + QK_ROPE) ** -0.5 * LOG2E)
    tile = lambda w: pl.BlockSpec((tm, w), lambda i: (i, 0))
    in_specs = [
        tile(D_MODEL),
        _mod_spec(m3, lambda i: i // tiles_per_seq),
        _const_spec((1, D_MODEL)),
        _const_spec((D_MODEL, C_KRP + LANES if rope else C_END)),
        _const_spec((1, Q_LORA)),
        _const_spec(wqb.shape),
    ]
    args = [x2d, m3, norm1, win, q_norm, wqb]
    if rope:
        in_specs.append(_const_spec(wqbp.shape))
        args.append(wqbp)
    in_specs.append(_const_spec((1, KV_LORA)))
    args.append(kv_norm)
    if rope:
        tab = pl.BlockSpec((tm, LANES), lambda i: (i % tiles_per_seq, 0))
        in_specs += [tab, tab]
        args += [cos, sin]
    out_shape = [
        jax.ShapeDtypeStruct((tokens, CONV_DIM), bf16),
        jax.ShapeDtypeStruct((tokens, N_HEADS * HEAD_W), bf16),
        jax.ShapeDtypeStruct((tokens, KV_LORA), bf16 if rope else f32),
        jax.ShapeDtypeStruct((tokens, LANES), bf16),
        jax.ShapeDtypeStruct((tokens, 2 * D_MODEL), bf16),
    ]
    out_specs = [tile(CONV_DIM), tile(N_HEADS * HEAD_W), tile(KV_LORA), tile(LANES), tile(2 * D_MODEL)]
    if not rope:
        out_shape.append(jax.ShapeDtypeStruct((tokens, QK_ROPE), f32))
        out_specs.append(tile(QK_ROPE))
    return pl.pallas_call(
        functools.partial(_inproj_kernel, rope=rope, q_scale=q_scale),
        grid=(tokens // tm,),
        in_specs=in_specs,
        out_specs=out_specs,
        out_shape=out_shape,
        compiler_params=_params("parallel"),
        name="inproj_rope" if rope else "inproj",
    )(*args)


def _conv_kernel(v_ref, g_ref, dw_ref, dwb_ref, lng_ref, lnb_ref, wco_ref, o_ref, vpad, ybuf, *, n, rt, ct):
    pad = CONV_WIDTH // 2
    zeros = jnp.zeros((CONV_HALO, CONV_DIM), f32)
    vpad[0:CONV_HALO, :] = zeros
    vpad[CONV_HALO + n:2 * CONV_HALO + n, :] = zeros
    vpad[CONV_HALO:CONV_HALO + n, :] = v_ref[...].astype(f32)

    sub = 8
    span = ((CONV_HALO - pad + CONV_WIDTH - 1) // sub) * sub

    def conv_chunk(c, carry):
        r0 = pl.multiple_of(c * ct, ct)
        for cb in range(CONV_DIM // LANES):
            sl = slice(cb * LANES, (cb + 1) * LANES)
            win = vpad[pl.ds(r0, ct + 2 * CONV_HALO), sl]
            acc = jnp.zeros((ct, LANES), f32)
            rows = ct + 2 * CONV_HALO
            for ph in range(sub):
                wph = win if ph == 0 else pltpu.roll(win, rows - ph, axis=0)
                for a in range(span // sub + 1):
                    k = a * sub + ph - (CONV_HALO - pad)
                    if 0 <= k < CONV_WIDTH:
                        acc = acc + wph[a * sub:a * sub + ct, :] * dw_ref[k:k + 1, sl]
            ybuf[pl.ds(r0, ct), sl] = acc + dwb_ref[:, sl]
        return carry

    lax.fori_loop(0, n // ct, conv_chunk, 0)

    def chunk(c, carry):
        r0 = pl.multiple_of(c * rt, rt)
        y = ybuf[pl.ds(r0, rt), :]
        mu = jnp.mean(y, axis=-1, keepdims=True)
        yc = y - mu
        var = jnp.mean(yc * yc, axis=-1, keepdims=True)
        z = yc * lax.rsqrt(var + EPS) * lng_ref[...] + lnb_ref[...]
        z = z * jax.nn.sigmoid(z)
        co = _dot(z.astype(bf16), wco_ref[...])
        o_ref[pl.ds(r0, rt), :] = (g_ref[pl.ds(r0, rt), :].astype(f32) * co).astype(o_ref.dtype)
        return carry

    lax.fori_loop(0, n // rt, chunk, 0)


def _conv(v3, g3, dw, dwb, lng, lnb, wco, *, rt=256, ct=64):
    nseq, n, _ = v3.shape
    return pl.pallas_call(
        functools.partial(_conv_kernel, n=n, rt=rt, ct=ct),
        grid=(nseq,),
        in_specs=[
            pl.BlockSpec((None, n, CONV_DIM), lambda s: (s, 0, 0)),
            pl.BlockSpec((None, n, D_MODEL), lambda s: (s, 0, 0)),
            _const_spec(dw.shape),
            _const_spec((1, CONV_DIM)),
            _const_spec((1, CONV_DIM)),
            _const_spec((1, CONV_DIM)),
            _const_spec(wco.shape),
        ],
        out_specs=pl.BlockSpec((None, n, D_MODEL), lambda s: (s, 0, 0)),
        out_shape=jax.ShapeDtypeStruct((nseq, n, D_MODEL), bf16),
        scratch_shapes=[
            pltpu.VMEM((n + 2 * CONV_HALO, CONV_DIM), f32),
            pltpu.VMEM((n, CONV_DIM), f32),
        ],
        compiler_params=_params("parallel"),
        name="conv",
    )(v3, g3, dw, dwb, lng, lnb, wco)


def _attn_kernel(q_ref, ckv_ref, kpe_ref, cm_ref, g_ref, wk_ref, wv_ref, wo_ref, o_ref, k_scr, v_scr):
    @pl.when(pl.program_id(1) == 0)
    def _():
        ckv = ckv_ref[...].astype(bf16)
        kpe = kpe_ref[...].astype(f32)
        for hd in range(N_HEADS):
            k = _dot(ckv, wk_ref[:, hd * HEAD_W:(hd + 1) * HEAD_W]) + kpe
            k_scr[hd] = k.astype(bf16)
        v = _dot(ckv, wv_ref[...])
        ones = jnp.ones((v.shape[0], LANES), bf16)
        for j in range(N_HEADS // 2):
            v_scr[j, :, 0:LANES] = v[:, j * LANES:(j + 1) * LANES].astype(bf16)
            v_scr[j, :, LANES:2 * LANES] = ones

    tq = q_ref.shape[0]
    lane = lax.broadcasted_iota(jnp.int32, (tq, LANES), 1)
    pairs = []
    for j in range(N_HEADS // 2):
        outs = []
        for hh in range(2):
            hd = 2 * j + hh
            qh = q_ref[:, hd * HEAD_W:(hd + 1) * HEAD_W]
            s = lax.dot_general(qh, k_scr[hd], (((1,), (1,)), ((), ())), preferred_element_type=f32)
            mx = jnp.max(s, axis=-1, keepdims=True)
            p = jnp.exp2(s - mx).astype(bf16)
            r = _dot(p, v_scr[j])
            outs.append(r[:, 0:LANES] / r[:, LANES:LANES + 1])
        pairs.append(jnp.where(lane < V_DIM, outs[0], outs[1]))
    attn = jnp.concatenate(pairs, axis=-1).astype(bf16)
    ao = _dot(attn, wo_ref[...])
    o_ref[...] = (cm_ref[...].astype(f32) + g_ref[...].astype(f32) * ao).astype(o_ref.dtype)


def _attn(q2d, ckv3, kpe3, cm2d, g2d, wk, wv, wo, *, n, tq=256):
    nseq, s_len, _ = ckv3.shape
    qb = n // tq
    tile = lambda w, c=0: pl.BlockSpec((tq, w), lambda s, i: (s * qb + i, c))
    return pl.pallas_call(
        _attn_kernel,
        grid=(nseq, qb),
        in_specs=[
            tile(N_HEADS * HEAD_W),
            pl.BlockSpec((None, s_len, KV_LORA), lambda s, i: (s, 0, 0)),
            pl.BlockSpec((None, s_len, LANES), lambda s, i: (s, 0, 0)),
            tile(D_MODEL),
            tile(D_MODEL, 1),
            _const_spec(wk.shape),
            _const_spec(wv.shape),
            _const_spec(wo.shape),
        ],
        out_specs=tile(D_MODEL),
        out_shape=jax.ShapeDtypeStruct((nseq * n, D_MODEL), bf16),
        scratch_shapes=[
            pltpu.VMEM((N_HEADS, s_len, HEAD_W), bf16),
            pltpu.VMEM((N_HEADS // 2, s_len, 2 * LANES), bf16),
        ],
        compiler_params=_params("parallel", "arbitrary"),
        name="attn",
    )(q2d, ckv3, kpe3, cm2d, g2d, wk, wv, wo)


def _outproj_kernel(mg_ref, x_ref, m_ref, wout_ref, n2_ref, wr_ref, x1_ref, h2_ref, lg_ref):
    g1 = m_ref[2:3, :]
    sh2 = m_ref[3:4, :]
    sc2 = m_ref[4:5, :]
    x1 = x_ref[...] + g1 * _dot(mg_ref[...], wout_ref[...])
    x1_ref[...] = x1
    h2 = _rms(x1, n2_ref[...]) * (1.0 + sc2) + sh2
    hi = h2.astype(bf16)
    h2_ref[...] = hi
    lo = (h2 - hi.astype(f32)).astype(bf16)
    s = _dot(hi, wr_ref[...]) + _dot(lo, wr_ref[...])
    lg_ref[...] = s + pltpu.roll(s, LANES - N_EXPERTS, axis=1)


def _outproj(mg2d, x2d, m3, wout, norm2, wr, *, n, tm=256):
    tokens = x2d.shape[0]
    tiles_per_seq = n // tm
    tile = lambda w: pl.BlockSpec((tm, w), lambda i: (i, 0))
    return pl.pallas_call(
        _outproj_kernel,
        grid=(tokens // tm,),
        in_specs=[
            tile(D_MODEL),
            tile(D_MODEL),
            _mod_spec(m3, lambda i: i // tiles_per_seq),
            _const_spec(wout.shape),
            _const_spec((1, D_MODEL)),
            _const_spec(wr.shape),
        ],
        out_specs=(tile(D_MODEL), tile(D_MODEL), tile(LANES)),
        out_shape=(
            jax.ShapeDtypeStruct((tokens, D_MODEL), f32),
            jax.ShapeDtypeStruct((tokens, D_MODEL), bf16),
            jax.ShapeDtypeStruct((tokens, LANES), f32),
        ),
        compiler_params=_params("parallel"),
        name="outproj",
    )(mg2d, x2d, m3, wout, norm2, wr)


def _route_kernel(lg_ref, pos_ref, aff_ref, *, nseq, n, cap):
    for s in range(nseq):
        lt = lg_ref[s].T[0:N_EXPERTS, :]
        e = jnp.exp(lt - jnp.max(lt, axis=0, keepdims=True))
        aff_ref[s * N_EXPERTS:(s + 1) * N_EXPERTS, :] = e / jnp.sum(e, axis=0, keepdims=True)
    rows = nseq * N_EXPERTS
    capf = float(cap)

    def bit_step(i, t):
        cand = t | (jnp.int32(1) << (30 - i))
        thr = lax.bitcast_convert_type(cand, f32)
        cnt = jnp.sum(jnp.where(aff_ref[...] >= thr, 1.0, 0.0), axis=1, keepdims=True)
        return jnp.where(cnt >= capf, cand, t)

    t = lax.fori_loop(0, 31, bit_step, jnp.zeros((rows, 1), jnp.int32))
    thr = lax.bitcast_convert_type(t, f32)
    need = capf - jnp.sum(jnp.where(aff_ref[...] > thr, 1.0, 0.0), axis=1, keepdims=True)

    blk = 2 * LANES
    tri = jnp.where(
        lax.broadcasted_iota(jnp.int32, (blk, blk), 0) < lax.broadcasted_iota(jnp.int32, (blk, blk), 1),
        1.0, 0.0).astype(bf16)
    carry_gt = jnp.zeros((rows, 1), f32)
    carry_eq = jnp.zeros((rows, 1), f32)
    for b in range(n // blk):
        sl = slice(b * blk, (b + 1) * blk)
        ab = aff_ref[:, sl]
        gt = ab > thr
        eq = ab == thr
        gtb = jnp.where(gt, 1.0, 0.0)
        eqb = jnp.where(eq, 1.0, 0.0)
        pre_gt = _dot(gtb.astype(bf16), tri) + carry_gt
        pre_eq = _dot(eqb.astype(bf16), tri) + carry_eq
        carry_gt = carry_gt + jnp.sum(gtb, axis=1, keepdims=True)
        carry_eq = carry_eq + jnp.sum(eqb, axis=1, keepdims=True)
        sel = gt | (eq & (pre_eq < need))
        slot = pre_gt + jnp.minimum(pre_eq, need)
        pos_ref[:, sl] = jnp.where(sel, slot, -1.0).astype(jnp.int32)


def _route(lg3, *, cap):
    nseq, n, _ = lg3.shape
    rows = nseq * N_EXPERTS
    return pl.pallas_call(
        functools.partial(_route_kernel, nseq=nseq, n=n, cap=cap),
        grid=(1,),
        in_specs=[_const_spec(lg3.shape)],
        out_specs=(_const_spec((rows, n)), _const_spec((rows, n))),
        out_shape=(jax.ShapeDtypeStruct((rows, n), jnp.int32), jax.ShapeDtypeStruct((rows, n), f32)),
        compiler_params=_params("arbitrary"),
        name="route",
    )(lg3)


def _slot_hits(pos_ref, e, cap):
    width = pos_ref.shape[1]
    return lax.broadcasted_iota(jnp.int32, (cap, width), 0) == pos_ref[e:e + 1, :]


def _one_hot(hits):
    return jnp.concatenate([jnp.where(h, 1.0, 0.0).astype(bf16) for h in hits], axis=0)


def _gather_kernel(pos_ref, aff_ref, h2_ref, xg_ref, vals_ref, *, cap, ne):
    for e0 in range(0, N_EXPERTS, ne):
        hits = [_slot_hits(pos_ref, e0 + e, cap) for e in range(ne)]
        xg = _dot(_one_hot(hits), h2_ref[...]).astype(xg_ref.dtype)
        for e in range(ne):
            xg_ref[e0 + e] = xg[e * cap:(e + 1) * cap, :]
            vals = jnp.sum(jnp.where(hits[e], aff_ref[e0 + e:e0 + e + 1, :], 0.0), axis=1, keepdims=True)
            vals_ref[e0 + e] = jnp.broadcast_to(vals, (cap, LANES))


def _gather(pos3, aff3, h23, *, cap, ne):
    nseq, _, n = pos3.shape
    return pl.pallas_call(
        functools.partial(_gather_kernel, cap=cap, ne=ne),
        grid=(nseq,),
        in_specs=[
            pl.BlockSpec((None, N_EXPERTS, n), lambda s: (s, 0, 0)),
            pl.BlockSpec((None, N_EXPERTS, n), lambda s: (s, 0, 0)),
            pl.BlockSpec((None, n, D_MODEL), lambda s: (s, 0, 0)),
        ],
        out_specs=(
            pl.BlockSpec((N_EXPERTS, cap, D_MODEL), lambda s: (0, s, 0)),
            pl.BlockSpec((N_EXPERTS, cap, LANES), lambda s: (0, s, 0)),
        ),
        out_shape=(
            jax.ShapeDtypeStruct((N_EXPERTS, nseq * cap, D_MODEL), bf16),
            jax.ShapeDtypeStruct((N_EXPERTS, nseq * cap, LANES), f32),
        ),
        compiler_params=_params("parallel"),
        name="gather",
    )(pos3, aff3, h23)


def _experts_kernel(xp_ref, vp_ref, xs_ref, vs_ref, wg_ref, wu_ref, wd_ref, yp_ref, ys_ref, *, rc):
    wg = wg_ref[...].astype(bf16)
    wu = wu_ref[...].astype(bf16)
    wd = wd_ref[...].astype(bf16)
    for x_ref, v_ref, y_ref in ((xp_ref, vp_ref, yp_ref), (xs_ref, vs_ref, ys_ref)):
        for r0 in range(0, x_ref.shape[0], rc):
            x = x_ref[r0:r0 + rc, :]
            a = _dot(x, wg)
            u = _dot(x, wu)
            hm = (a * jax.nn.sigmoid(a) * u).astype(bf16)
            y = _dot(hm, wd) * v_ref[r0:r0 + rc, 0:1]
            y_ref[r0:r0 + rc, :] = y.astype(y_ref.dtype)


def _experts(xg_p, vals_p, xg_s, vals_s, wg, wu, wd, *, rc=512):
    rp = xg_p.shape[1]
    rs = xg_s.shape[1]
    per_e = lambda r, w: pl.BlockSpec((None, r, w), lambda e: (e, 0, 0))
    return pl.pallas_call(
        functools.partial(_experts_kernel, rc=rc),
        grid=(N_EXPERTS,),
        in_specs=[
            per_e(rp, D_MODEL), per_e(rp, LANES), per_e(rs, D_MODEL), per_e(rs, LANES),
            per_e(D_MODEL, EXPERT_FF), per_e(D_MODEL, EXPERT_FF), per_e(EXPERT_FF, D_MODEL),
        ],
        out_specs=(per_e(rp, D_MODEL), per_e(rs, D_MODEL)),
        out_shape=(
            jax.ShapeDtypeStruct((N_EXPERTS, rp, D_MODEL), bf16),
            jax.ShapeDtypeStruct((N_EXPERTS, rs, D_MODEL), bf16),
        ),
        compiler_params=_params("parallel"),
        name="experts",
    )(xg_p, vals_p, xg_s, vals_s, wg, wu, wd)


def _scatter_kernel(pos_ref, y_ref, x1_ref, m_ref, fn_ref, o_ref, *, tn, cap, ne):
    moe = jnp.zeros((tn, D_MODEL), f32)
    for e0 in range(0, N_EXPERTS, ne):
        onehot = _one_hot([_slot_hits(pos_ref, e0 + e, cap) for e in range(ne)])
        y = jnp.concatenate([y_ref[e0 + e] for e in range(ne)], axis=0) if ne > 1 else y_ref[e0]
        moe = moe + lax.dot_general(onehot, y, (((0,), (0,)), ((), ())), preferred_element_type=f32)
    g2 = m_ref[5:6, :]
    xo = x1_ref[...] + g2 * moe
    o_ref[...] = _rms(xo, fn_ref[...])


def _scatter(pos3, y, x13, m3, fn, *, cap, ne, tn):
    nseq, _, n = pos3.shape
    return pl.pallas_call(
        functools.partial(_scatter_kernel, tn=tn, cap=cap, ne=ne),
        grid=(nseq, n // tn),
        in_specs=[
            pl.BlockSpec((None, N_EXPERTS, tn), lambda s, i: (s, 0, i)),
            pl.BlockSpec((N_EXPERTS, cap, D_MODEL), lambda s, i: (0, s, 0)),
            pl.BlockSpec((None, tn, D_MODEL), lambda s, i: (s, i, 0)),
            _mod_spec(m3, lambda s, i: s),
            _const_spec((1, D_MODEL)),
        ],
        out_specs=pl.BlockSpec((None, tn, D_MODEL), lambda s, i: (s, i, 0)),
        out_shape=jax.ShapeDtypeStruct((nseq, n, D_MODEL), f32),
        compiler_params=_params("parallel", "arbitrary"),
        name="scatter",
    )(pos3, y, x13, m3, fn)


def _rope_tables(n):
    t = np.arange(n)
    half = QK_ROPE // 2
    freqs = ROPE_BASE ** (-np.arange(0, half, 2, dtype=np.float64) / half)
    ang_r = (t // GRID_W)[:, None] * freqs
    ang_c = (t % GRID_W)[:, None] * freqs
    cr, sr, cc, sc = np.cos(ang_r), np.sin(ang_r), np.cos(ang_c), np.sin(ang_c)
    cos = np.ones((n, HEAD_W))
    sin = np.zeros((n, HEAD_W))
    cos[:, ROPE_OFF:ROPE_OFF + QK_ROPE] = np.concatenate([cr, cr, cc, cc], axis=-1)
    sin[:, ROPE_OFF:ROPE_OFF + QK_ROPE] = np.concatenate([-sr, sr, -sc, sc], axis=-1)
    return jnp.asarray(cos, f32), jnp.asarray(sin, f32)


_PARTNER = np.concatenate([np.arange(8, 16), np.arange(0, 8), np.arange(24, 32), np.arange(16, 24)])


def _rope_partner(w):
    q = QK_ROPE // 4
    return jnp.concatenate([w[..., q:2 * q], w[..., 0:q], w[..., 3 * q:4 * q], w[..., 2 * q:3 * q]], axis=-1)


def _rope_placement():
    place = np.zeros((LANES, 2 * LANES), np.float32)
    d = np.arange(QK_ROPE)
    place[d, ROPE_OFF + d] = 1.0
    place[_PARTNER, LANES + ROPE_OFF + d] = 1.0
    return jnp.asarray(place, bf16)


def _head_blocks(w_nope, w_rope):
    rows = w_nope.shape[0]
    if w_rope is None:
        w_rope = jnp.zeros((rows, N_HEADS, QK_ROPE), w_nope.dtype)
    z = jnp.zeros((rows, N_HEADS, HEAD_W - QK_NOPE - QK_ROPE), w_nope.dtype)
    return jnp.concatenate([w_nope, w_rope, z], axis=-1).reshape(rows, N_HEADS * HEAD_W)


def kernel(x_prompt, x_sample, cache_ckv, cache_kpe, c, c_ctx, w_ada, b_ada, norm1, w_in, conv_dw, conv_dw_b,
           conv_ln_g, conv_ln_b, w_conv_out, q_norm, w_qb, kv_norm, w_kvb, w_o_mla, w_out, norm2, w_router,
           w_e_gate, w_e_up, w_e_down, final_norm):
    assert w_ada.shape[0] == 1, "single trunk layer"
    nb_p, n_p, _ = x_prompt.shape
    nb_s, n_s, _ = x_sample.shape

    win = _wprep(w_in[0].T, _rope_placement())
    wq = w_qb[0].reshape(Q_LORA, N_HEADS, QK_NOPE + QK_ROPE)
    wqb = _head_blocks(wq[..., :QK_NOPE], wq[..., QK_NOPE:]).astype(bf16)
    wqbp = _head_blocks(jnp.zeros_like(wq[..., :QK_NOPE]), _rope_partner(wq[..., QK_NOPE:])).astype(bf16)
    wkv = w_kvb[0].reshape(KV_LORA, N_HEADS, QK_NOPE + V_DIM)
    wk = _head_blocks(wkv[..., :QK_NOPE], None).astype(bf16)
    wv = wkv[..., QK_NOPE:].reshape(KV_LORA, N_HEADS * V_DIM).astype(bf16)
    wco = w_conv_out[0].astype(bf16)
    wo = w_o_mla[0].astype(bf16)
    wout = w_out[0].astype(bf16)
    wr_hi = w_router[0].astype(bf16)
    wr_lo = (w_router[0] - wr_hi.astype(f32)).astype(bf16)
    wr = jnp.concatenate([wr_hi, wr_lo, jnp.zeros((D_MODEL, LANES - 2 * N_EXPERTS), bf16)], axis=-1)
    row = lambda a: a.reshape(1, -1)

    mod = jnp.concatenate([c_ctx[None, :], c, jnp.zeros((8 - 1 - nb_s, D_MODEL), f32)], axis=0)
    m = _ada(mod, w_ada[0], b_ada[0]).reshape(8, 6, D_MODEL)
    m_p, m_s = m[0:1], m[1:1 + nb_s]
    cos, sin = _rope_tables(n_s)

    def mixers(x, m3, rope, ctx_ckv, ctx_kpe):
        nseq, n, _ = x.shape
        x2d = x.reshape(nseq * n, D_MODEL)
        v, q, ckv, kr, g, *kpe = _inproj(x2d, m3, row(norm1[0]), win, row(q_norm[0]), wqb, wqbp, row(kv_norm[0]),
                                         cos, sin, n=n, rope=rope)
        cm = _conv(v.reshape(nseq, n, CONV_DIM), g.reshape(nseq, n, 2 * D_MODEL), conv_dw[0], row(conv_dw_b[0]),
                   row(conv_ln_g[0]), row(conv_ln_b[0]), wco)
        keys_ckv = ckv.reshape(nseq, n, KV_LORA)
        keys_kpe = kr.reshape(nseq, n, LANES)
        if ctx_ckv is not None:
            keys_ckv = jnp.concatenate([ctx_ckv.astype(keys_ckv.dtype), keys_ckv], axis=1)
            keys_kpe = jnp.concatenate([ctx_kpe.astype(keys_kpe.dtype), keys_kpe], axis=1)
        mg = _attn(q, keys_ckv, keys_kpe, cm.reshape(nseq * n, D_MODEL), g, wk, wv, wo, n=n)
        x1, h2, lg = _outproj(mg, x2d, m3, wout, row(norm2[0]), wr, n=n)
        return x1, h2, lg, ckv, kpe

    ctx_kpe = jnp.pad(cache_kpe[:, 0], ((0, 0), (0, 0), (ROPE_OFF, LANES - ROPE_OFF - QK_ROPE)))
    x1_p, h2_p, lg_p, ckv_p, (kpe_p,) = mixers(x_prompt, m_p, False, None, None)
    x1_s, h2_s, lg_s, _, _ = mixers(x_sample, m_s, True, cache_ckv[:, 0], ctx_kpe)

    def route_gather(h2, lg, nseq, n, ne):
        cap = EC_FACTOR * n // N_EXPERTS
        pos, aff = _route(lg.reshape(nseq, n, LANES), cap=cap)
        pos3 = pos.reshape(nseq, N_EXPERTS, n)
        xg, vals = _gather(pos3, aff.reshape(nseq, N_EXPERTS, n), h2.reshape(nseq, n, D_MODEL), cap=cap, ne=ne)
        return pos3, xg, vals, cap

    pos_p, xg_p, vals_p, cap_p = route_gather(h2_p, lg_p, nb_p, n_p, N_EXPERTS)
    pos_s, xg_s, vals_s, cap_s = route_gather(h2_s, lg_s, nb_s, n_s, 1)
    y_p, y_s = _experts(xg_p, vals_p, xg_s, vals_s, w_e_gate[0], w_e_up[0], w_e_down[0])
    fn = row(final_norm)
    y_prompt = _scatter(pos_p, y_p, x1_p.reshape(nb_p, n_p, D_MODEL), m_p, fn, cap=cap_p, ne=N_EXPERTS, tn=n_p)
    y_sample = _scatter(pos_s, y_s, x1_s.reshape(nb_s, n_s, D_MODEL), m_s, fn, cap=cap_s, ne=1, tn=512)

    new_ckv = ckv_p.reshape(nb_p, 1, n_p, KV_LORA)
    new_kpe = kpe_p.reshape(nb_p, 1, n_p, QK_ROPE)
    return (y_prompt, y_sample, new_ckv, new_kpe)
```

```python
import functools

import jax
import jax.numpy as jnp
import numpy as np
from jax import lax
from jax.experimental import pallas as pl
from jax.experimental.pallas import tpu as pltpu

D_MODEL = 1024
GRID_W = 64
CONV_DIM = 512
CONV_WIDTH = 31
N_HEADS = 8
QK_NOPE = 64
QK_ROPE = 32
V_DIM = 64
Q_LORA = 256
KV_LORA = 128
N_EXPERTS = 16
EXPERT_FF = 512
EC_FACTOR = 2
ROPE_BASE = 10000.0
EPS = 1e-6

LANES = 128
HEAD_W = LANES
ROPE_OFF = QK_NOPE
CONV_HALO = 16
LOG2E = 1.4426950408889634
VMEM_LIMIT = 48 * 1024 * 1024
IN_TILE = 512
Q_TILE = 512
MOE_ROWS = 512
MOE_TOKENS = 1024
SCATTER_TILE = 512

C_CONV = 0
C_QA = 2 * CONV_DIM
C_KVA = C_QA + Q_LORA
C_KR = C_KVA + KV_LORA
C_GATE = C_KR + LANES
C_END = C_GATE + 2 * D_MODEL
C_KRP = C_END

f32 = jnp.float32
bf16 = jnp.bfloat16


def _params(*sem):
    return pltpu.CompilerParams(dimension_semantics=sem, vmem_limit_bytes=VMEM_LIMIT)


def _dot(a, b):
    return jnp.dot(a, b, preferred_element_type=f32)


def _rms(x, g):
    return x * lax.rsqrt(jnp.mean(x * x, axis=-1, keepdims=True) + EPS) * g


def _const_spec(shape):
    nd = len(shape)
    return pl.BlockSpec(shape, lambda *_: (0,) * nd)


def _mod_spec(m3, seq_of):
    if m3.shape[0] == 1:
        return _const_spec((None, 6, D_MODEL))
    return pl.BlockSpec((None, 6, D_MODEL), lambda *idx: (seq_of(*idx), 0, 0))


def _ada_kernel(s_ref, w_ref, b_ref, o_ref):
    s = s_ref[...]
    s = s * jax.nn.sigmoid(s)
    o_ref[...] = _dot(s.astype(bf16), w_ref[...].astype(bf16)) + b_ref[...]


def _ada(mod, w_ada, b_ada):
    rows = mod.shape[0]
    n_out = w_ada.shape[1]
    tn = D_MODEL
    return pl.pallas_call(
        _ada_kernel,
        grid=(n_out // tn,),
        in_specs=[
            _const_spec((rows, D_MODEL)),
            pl.BlockSpec((D_MODEL, tn), lambda j: (0, j)),
            pl.BlockSpec((1, tn), lambda j: (0, j)),
        ],
        out_specs=pl.BlockSpec((rows, tn), lambda j: (0, j)),
        out_shape=jax.ShapeDtypeStruct((rows, n_out), f32),
        compiler_params=_params("arbitrary"),
        name="ada",
    )(mod, w_ada, b_ada.reshape(1, n_out))


def _wprep_kernel(wt_ref, place_ref, o_ref):
    def block(r0):
        return wt_ref[r0:r0 + LANES, :].T.astype(bf16)

    for j in range(C_KR // LANES):
        o_ref[:, j * LANES:(j + 1) * LANES] = block(j * LANES)
    placed = _dot(block(C_KR), place_ref[...])
    o_ref[:, C_KR:C_GATE] = placed[:, 0:LANES].astype(bf16)
    o_ref[:, C_KRP:C_KRP + LANES] = placed[:, LANES:2 * LANES].astype(bf16)
    for j in range(2 * D_MODEL // LANES):
        o_ref[:, C_GATE + j * LANES:C_GATE + (j + 1) * LANES] = block(C_KR + QK_ROPE + j * LANES)


def _wprep(w_in_t, place, *, tr=LANES):
    cols, rows = w_in_t.shape
    return pl.pallas_call(
        _wprep_kernel,
        grid=(rows // tr,),
        in_specs=[pl.BlockSpec((cols, tr), lambda i: (0, i)), _const_spec(place.shape)],
        out_specs=pl.BlockSpec((tr, C_KRP + LANES), lambda i: (i, 0)),
        out_shape=jax.ShapeDtypeStruct((rows, C_KRP + LANES), bf16),
        compiler_params=_params("parallel"),
        name="wprep",
    )(w_in_t, place)


def _inproj_kernel(*refs, rope, q_scale):
    if rope:
        (x_ref, m_ref, n1_ref, win_ref, qn_ref, wqb_ref, wqbp_ref, kvn_ref, cos_ref, sin_ref,
         v_ref, q_ref, ckv_ref, kr_ref, g_ref) = refs
    else:
        (x_ref, m_ref, n1_ref, win_ref, qn_ref, wqb_ref, kvn_ref,
         v_ref, q_ref, ckv_ref, kr_ref, g_ref, kpe_ref) = refs
    sh1 = m_ref[0:1, :]
    sc1 = m_ref[1:2, :]
    h = _rms(x_ref[...], n1_ref[...]) * (1.0 + sc1) + sh1
    hb = h.astype(bf16)

    a = _dot(hb, win_ref[:, C_CONV:C_CONV + CONV_DIM])
    b = _dot(hb, win_ref[:, C_CONV + CONV_DIM:C_QA])
    v_ref[...] = (a * jax.nn.sigmoid(b)).astype(v_ref.dtype)

    qa = _dot(hb, win_ref[:, C_QA:C_KVA])
    qn = _rms(qa, qn_ref[...]).astype(bf16)
    q = _dot(qn, wqb_ref[...])
    if rope:
        qp = _dot(qn, wqbp_ref[...])
        cos = cos_ref[...]
        sin = sin_ref[...]
        for hd in range(N_HEADS):
            sl = slice(hd * HEAD_W, (hd + 1) * HEAD_W)
            q_ref[:, sl] = ((q[:, sl] * cos + qp[:, sl] * sin) * q_scale).astype(q_ref.dtype)
    else:
        q_ref[...] = (q * q_scale).astype(q_ref.dtype)

    kva = _dot(hb, win_ref[:, C_KVA:C_KR])
    ckv_ref[...] = _rms(kva, kvn_ref[...]).astype(ckv_ref.dtype)

    kr = _dot(hb, win_ref[:, C_KR:C_GATE])
    if rope:
        krp = _dot(hb, win_ref[:, C_KRP:C_KRP + LANES])
        kr = kr * cos_ref[...] + krp * sin_ref[...]
    else:
        kpe_ref[...] = kr[:, ROPE_OFF:ROPE_OFF + QK_ROPE]
    kr_ref[...] = kr.astype(kr_ref.dtype)

    gw = 512
    for j in range(2 * D_MODEL // gw):
        gl = _dot(hb, win_ref[:, C_GATE + j * gw:C_GATE + (j + 1) * gw])
        g_ref[:, j * gw:(j + 1) * gw] = jax.nn.sigmoid(gl).astype(g_ref.dtype)


def _inproj(x2d, m3, norm1, win, q_norm, wqb, wqbp, kv_norm, cos, sin, *, n, rope, tm=256):
    tokens = x2d.shape[0]
    assert n % tm == 0 or (tm % n == 0 and m3.shape[0] == 1 and not rope)
    tiles_per_seq = max(1, n // tm)
    q_scale = float((QK_NOPE + QK_ROPE) ** -0.5 * LOG2E)
    tile = lambda w: pl.BlockSpec((tm, w), lambda i: (i, 0))
    in_specs = [
        tile(D_MODEL),
        _mod_spec(m3, lambda i: i // tiles_per_seq),
        _const_spec((1, D_MODEL)),
        _const_spec((D_MODEL, C_KRP + LANES if rope else C_END)),
        _const_spec((1, Q_LORA)),
        _const_spec(wqb.shape),
    ]
    args = [x2d, m3, norm1, win, q_norm, wqb]
    if rope:
        in_specs.append(_const_spec(wqbp.shape))
        args.append(wqbp)
    in_specs.append(_const_spec((1, KV_LORA)))
    args.append(kv_norm)
    if rope:
        tab = pl.BlockSpec((tm, LANES), lambda i: (i % tiles_per_seq, 0))
        in_specs += [tab, tab]
        args += [cos, sin]
    out_shape = [
        jax.ShapeDtypeStruct((tokens, CONV_DIM), bf16),
        jax.ShapeDtypeStruct((tokens, N_HEADS * HEAD_W), bf16),
        jax.ShapeDtypeStruct((tokens, KV_LORA), bf16 if rope else f32),
        jax.ShapeDtypeStruct((tokens, LANES), bf16),
        jax.ShapeDtypeStruct((tokens, 2 * D_MODEL), bf16),
    ]
    out_specs = [tile(CONV_DIM), tile(N_HEADS * HEAD_W), tile(KV_LORA), tile(LANES), tile(2 * D_MODEL)]
    if not rope:
        out_shape.append(jax.ShapeDtypeStruct((tokens, QK_ROPE), f32))
        out_specs.append(tile(QK_ROPE))
    return pl.pallas_call(
        functools.partial(_inproj_kernel, rope=rope, q_scale=q_scale),
        grid=(tokens // tm,),
        in_specs=in_specs,
        out_specs=out_specs,
        out_shape=out_shape,
        compiler_params=_params("parallel"),
        name="inproj_rope" if rope else "inproj",
    )(*args)


def _conv_kernel(v_ref, g_ref, dw_ref, dwb_ref, lng_ref, lnb_ref, wco_ref, o_ref, vpad, ybuf, *, n, rt, ct):
    pad = CONV_WIDTH // 2
    zeros = jnp.zeros((CONV_HALO, CONV_DIM), f32)
    vpad[0:CONV_HALO, :] = zeros
    vpad[CONV_HALO + n:2 * CONV_HALO + n, :] = zeros
    vpad[CONV_HALO:CONV_HALO + n, :] = v_ref[...].astype(f32)

    sub = 8
    span = ((CONV_HALO - pad + CONV_WIDTH - 1) // sub) * sub

    def conv_chunk(c, carry):
        r0 = pl.multiple_of(c * ct, ct)
        for cb in range(CONV_DIM // LANES):
            sl = slice(cb * LANES, (cb + 1) * LANES)
            win = vpad[pl.ds(r0, ct + 2 * CONV_HALO), sl]
            acc = jnp.zeros((ct, LANES), f32)
            rows = ct + 2 * CONV_HALO
            for ph in range(sub):
                wph = win if ph == 0 else pltpu.roll(win, rows - ph, axis=0)
                for a in range(span // sub + 1):
                    k = a * sub + ph - (CONV_HALO - pad)
                    if 0 <= k < CONV_WIDTH:
                        acc = acc + wph[a * sub:a * sub + ct, :] * dw_ref[k:k + 1, sl]
            ybuf[pl.ds(r0, ct), sl] = acc + dwb_ref[:, sl]
        return carry

    lax.fori_loop(0, n // ct, conv_chunk, 0)

    def chunk(c, carry):
        r0 = pl.multiple_of(c * rt, rt)
        y = ybuf[pl.ds(r0, rt), :]
        mu = jnp.mean(y, axis=-1, keepdims=True)
        yc = y - mu
        var = jnp.mean(yc * yc, axis=-1, keepdims=True)
        z = yc * lax.rsqrt(var + EPS) * lng_ref[...] + lnb_ref[...]
        z = z * jax.nn.sigmoid(z)
        co = _dot(z.astype(bf16), wco_ref[...])
        o_ref[pl.ds(r0, rt), :] = (g_ref[pl.ds(r0, rt), :].astype(f32) * co).astype(o_ref.dtype)
        return carry

    lax.fori_loop(0, n // rt, chunk, 0)


def _conv(v3, g3, dw, dwb, lng, lnb, wco, *, rt=256, ct=64):
    nseq, n, _ = v3.shape
    return pl.pallas_call(
        functools.partial(_conv_kernel, n=n, rt=rt, ct=ct),
        grid=(nseq,),
        in_specs=[
            pl.BlockSpec((None, n, CONV_DIM), lambda s: (s, 0, 0)),
            pl.BlockSpec((None, n, D_MODEL), lambda s: (s, 0, 0)),
            _const_spec(dw.shape),
            _const_spec((1, CONV_DIM)),
            _const_spec((1, CONV_DIM)),
            _const_spec((1, CONV_DIM)),
            _const_spec(wco.shape),
        ],
        out_specs=pl.BlockSpec((None, n, D_MODEL), lambda s: (s, 0, 0)),
        out_shape=jax.ShapeDtypeStruct((nseq, n, D_MODEL), bf16),
        scratch_shapes=[
            pltpu.VMEM((n + 2 * CONV_HALO, CONV_DIM), f32),
            pltpu.VMEM((n, CONV_DIM), f32),
        ],
        compiler_params=_params("parallel"),
        name="conv",
    )(v3, g3, dw, dwb, lng, lnb, wco)


def _attn_kernel(q_ref, ckv_ref, kpe_ref, cm_ref, g_ref, wk_ref, wv_ref, wo_ref, o_ref, k_scr, v_scr):
    @pl.when(pl.program_id(1) == 0)
    def _():
        ckv = ckv_ref[...].astype(bf16)
        kpe = kpe_ref[...].astype(f32)
        k = _dot(ckv, wk_ref[...])
        v = _dot(ckv, wv_ref[...])
        ones = jnp.ones((v.shape[0], LANES), bf16)
        for hd in range(N_HEADS):
            k_scr[hd] = (k[:, hd * HEAD_W:(hd + 1) * HEAD_W] + kpe).astype(bf16)
        for j in range(N_HEADS // 2):
            v_scr[j, :, 0:LANES] = v[:, j * LANES:(j + 1) * LANES].astype(bf16)
            v_scr[j, :, LANES:2 * LANES] = ones

    tq = q_ref.shape[0]
    lane = lax.broadcasted_iota(jnp.int32, (tq, LANES), 1)
    pairs = []
    for j in range(N_HEADS // 2):
        outs = []
        for hh in range(2):
            hd = 2 * j + hh
            qh = q_ref[:, hd * HEAD_W:(hd + 1) * HEAD_W]
            s = lax.dot_general(qh, k_scr[hd], (((1,), (1,)), ((), ())), preferred_element_type=f32)
            mx = jnp.max(s, axis=-1, keepdims=True)
            p = jnp.exp2(s - mx).astype(bf16)
            r = _dot(p, v_scr[j])
            outs.append(r[:, 0:LANES] * (1.0 / r[:, LANES:LANES + 1]))
        pairs.append(jnp.where(lane < V_DIM, outs[0], outs[1]))
    attn = jnp.concatenate(pairs, axis=-1).astype(bf16)
    ao = _dot(attn, wo_ref[...])
    o_ref[...] = (cm_ref[...].astype(f32) + g_ref[...].astype(f32) * ao).astype(o_ref.dtype)


def _attn(q2d, ckv3, kpe3, cm2d, g2d, wk, wv, wo, *, n, tq=256):
    nseq, s_len, _ = ckv3.shape
    qb = n // tq
    tile = lambda w, c=0: pl.BlockSpec((tq, w), lambda s, i: (s * qb + i, c))
    return pl.pallas_call(
        _attn_kernel,
        grid=(nseq, qb),
        in_specs=[
            tile(N_HEADS * HEAD_W),
            pl.BlockSpec((None, s_len, KV_LORA), lambda s, i: (s, 0, 0)),
            pl.BlockSpec((None, s_len, LANES), lambda s, i: (s, 0, 0)),
            tile(D_MODEL),
            tile(D_MODEL, 1),
            _const_spec(wk.shape),
            _const_spec(wv.shape),
            _const_spec(wo.shape),
        ],
        out_specs=tile(D_MODEL),
        out_shape=jax.ShapeDtypeStruct((nseq * n, D_MODEL), bf16),
        scratch_shapes=[
            pltpu.VMEM((N_HEADS, s_len, HEAD_W), bf16),
            pltpu.VMEM((N_HEADS // 2, s_len, 2 * LANES), bf16),
        ],
        compiler_params=_params("parallel", "arbitrary"),
        name="attn",
    )(q2d, ckv3, kpe3, cm2d, g2d, wk, wv, wo)


def _outproj_kernel(mg_ref, x_ref, m_ref, wout_ref, n2_ref, wr_ref, x1_ref, h2_ref, lg_ref):
    g1 = m_ref[2:3, :]
    sh2 = m_ref[3:4, :]
    sc2 = m_ref[4:5, :]
    x1 = x_ref[...] + g1 * _dot(mg_ref[...], wout_ref[...])
    x1_ref[...] = x1
    h2 = _rms(x1, n2_ref[...]) * (1.0 + sc2) + sh2
    hi = h2.astype(bf16)
    h2_ref[...] = hi
    lo = (h2 - hi.astype(f32)).astype(bf16)
    s = _dot(hi, wr_ref[...]) + _dot(lo, wr_ref[...])
    lg_ref[...] = s + pltpu.roll(s, LANES - N_EXPERTS, axis=1)


def _outproj(mg2d, x2d, m3, wout, norm2, wr, *, n, tm=256):
    tokens = x2d.shape[0]
    tiles_per_seq = n // tm
    tile = lambda w: pl.BlockSpec((tm, w), lambda i: (i, 0))
    return pl.pallas_call(
        _outproj_kernel,
        grid=(tokens // tm,),
        in_specs=[
            tile(D_MODEL),
            tile(D_MODEL),
            _mod_spec(m3, lambda i: i // tiles_per_seq),
            _const_spec(wout.shape),
            _const_spec((1, D_MODEL)),
            _const_spec(wr.shape),
        ],
        out_specs=(tile(D_MODEL), tile(D_MODEL), tile(LANES)),
        out_shape=(
            jax.ShapeDtypeStruct((tokens, D_MODEL), f32),
            jax.ShapeDtypeStruct((tokens, D_MODEL), bf16),
            jax.ShapeDtypeStruct((tokens, LANES), f32),
        ),
        compiler_params=_params("parallel"),
        name="outproj",
    )(mg2d, x2d, m3, wout, norm2, wr)


def _route_kernel(lg_ref, pos_ref, aff_ref, *, nseq, n, cap):
    for s in range(nseq):
        lt = lg_ref[s].T[0:N_EXPERTS, :]
        e = jnp.exp(lt - jnp.max(lt, axis=0, keepdims=True))
        aff_ref[s * N_EXPERTS:(s + 1) * N_EXPERTS, :] = e / jnp.sum(e, axis=0, keepdims=True)
    rows = nseq * N_EXPERTS
    capf = float(cap)

    def bit_step(i, t):
        cand = t | (jnp.int32(1) << (30 - i))
        thr = lax.bitcast_convert_type(cand, f32)
        cnt = jnp.sum(jnp.where(aff_ref[...] >= thr, 1.0, 0.0), axis=1, keepdims=True)
        return jnp.where(cnt >= capf, cand, t)

    t = lax.fori_loop(0, 31, bit_step, jnp.zeros((rows, 1), jnp.int32))
    thr = lax.bitcast_convert_type(t, f32)
    need = capf - jnp.sum(jnp.where(aff_ref[...] > thr, 1.0, 0.0), axis=1, keepdims=True)

    blk = 2 * LANES
    tri = jnp.where(
        lax.broadcasted_iota(jnp.int32, (blk, blk), 0) < lax.broadcasted_iota(jnp.int32, (blk, blk), 1),
        1.0, 0.0).astype(bf16)
    carry_gt = jnp.zeros((rows, 1), f32)
    carry_eq = jnp.zeros((rows, 1), f32)
    for b in range(n // blk):
        sl = slice(b * blk, (b + 1) * blk)
        ab = aff_ref[:, sl]
        gt = ab > thr
        eq = ab == thr
        gtb = jnp.where(gt, 1.0, 0.0)
        eqb = jnp.where(eq, 1.0, 0.0)
        pre_gt = _dot(gtb.astype(bf16), tri) + carry_gt
        pre_eq = _dot(eqb.astype(bf16), tri) + carry_eq
        carry_gt = carry_gt + jnp.sum(gtb, axis=1, keepdims=True)
        carry_eq = carry_eq + jnp.sum(eqb, axis=1, keepdims=True)
        sel = gt | (eq & (pre_eq < need))
        slot = pre_gt + jnp.minimum(pre_eq, need)
        pos_ref[:, sl] = jnp.where(sel, slot, -1.0).astype(jnp.int32)


def _route(lg3, *, cap):
    nseq, n, _ = lg3.shape
    rows = nseq * N_EXPERTS
    return pl.pallas_call(
        functools.partial(_route_kernel, nseq=nseq, n=n, cap=cap),
        grid=(1,),
        in_specs=[_const_spec(lg3.shape)],
        out_specs=(_const_spec((rows, n)), _const_spec((rows, n))),
        out_shape=(jax.ShapeDtypeStruct((rows, n), jnp.int32), jax.ShapeDtypeStruct((rows, n), f32)),
        compiler_params=_params("arbitrary"),
        name="route",
    )(lg3)


def _slot_hits(pos_ref, e, cap):
    width = pos_ref.shape[1]
    return lax.broadcasted_iota(jnp.int32, (cap, width), 0) == pos_ref[e:e + 1, :]


def _one_hot(hits):
    return jnp.concatenate([jnp.where(h, 1.0, 0.0).astype(bf16) for h in hits], axis=0)


def _gather_kernel(pos_ref, aff_ref, h2_ref, xg_ref, vals_ref, *, cap, ne, sb):
    for g in range(sb):
        rows = slice(g * cap, (g + 1) * cap)
        for e0 in range(0, N_EXPERTS, ne):
            hits = [_slot_hits(pos_ref.at[g], e0 + e, cap) for e in range(ne)]
            xg = _dot(_one_hot(hits), h2_ref[g]).astype(xg_ref.dtype)
            for e in range(ne):
                xg_ref[e0 + e, rows, :] = xg[e * cap:(e + 1) * cap, :]
                vals = jnp.sum(jnp.where(hits[e], aff_ref[g, e0 + e:e0 + e + 1, :], 0.0), axis=1, keepdims=True)
                vals_ref[e0 + e, rows, :] = jnp.broadcast_to(vals, (cap, LANES))


def _gather(pos3, aff3, h23, *, cap, ne, sb):
    nseq, _, n = pos3.shape
    return pl.pallas_call(
        functools.partial(_gather_kernel, cap=cap, ne=ne, sb=sb),
        grid=(nseq // sb,),
        in_specs=[
            pl.BlockSpec((sb, N_EXPERTS, n), lambda s: (s, 0, 0)),
            pl.BlockSpec((sb, N_EXPERTS, n), lambda s: (s, 0, 0)),
            pl.BlockSpec((sb, n, D_MODEL), lambda s: (s, 0, 0)),
        ],
        out_specs=(
            pl.BlockSpec((N_EXPERTS, sb * cap, D_MODEL), lambda s: (0, s, 0)),
            pl.BlockSpec((N_EXPERTS, sb * cap, LANES), lambda s: (0, s, 0)),
        ),
        out_shape=(
            jax.ShapeDtypeStruct((N_EXPERTS, nseq * cap, D_MODEL), bf16),
            jax.ShapeDtypeStruct((N_EXPERTS, nseq * cap, LANES), f32),
        ),
        compiler_params=_params("parallel"),
        name="gather",
    )(pos3, aff3, h23)


def _experts_kernel(xp_ref, vp_ref, xs_ref, vs_ref, wg_ref, wu_ref, wd_ref, yp_ref, ys_ref, *, rc):
    wg = wg_ref[...].astype(bf16)
    wu = wu_ref[...].astype(bf16)
    wd = wd_ref[...].astype(bf16)
    for x_ref, v_ref, y_ref in ((xp_ref, vp_ref, yp_ref), (xs_ref, vs_ref, ys_ref)):
        for r0 in range(0, x_ref.shape[0], rc):
            x = x_ref[r0:r0 + rc, :]
            a = _dot(x, wg)
            u = _dot(x, wu)
            hm = (a * jax.nn.sigmoid(a) * u).astype(bf16)
            y = _dot(hm, wd) * v_ref[r0:r0 + rc, 0:1]
            y_ref[r0:r0 + rc, :] = y.astype(y_ref.dtype)


def _experts(xg_p, vals_p, xg_s, vals_s, wg, wu, wd, *, rc=512):
    rp = xg_p.shape[1]
    rs = xg_s.shape[1]
    per_e = lambda r, w: pl.BlockSpec((None, r, w), lambda e: (e, 0, 0))
    return pl.pallas_call(
        functools.partial(_experts_kernel, rc=rc),
        grid=(N_EXPERTS,),
        in_specs=[
            per_e(rp, D_MODEL), per_e(rp, LANES), per_e(rs, D_MODEL), per_e(rs, LANES),
            per_e(D_MODEL, EXPERT_FF), per_e(D_MODEL, EXPERT_FF), per_e(EXPERT_FF, D_MODEL),
        ],
        out_specs=(per_e(rp, D_MODEL), per_e(rs, D_MODEL)),
        out_shape=(
            jax.ShapeDtypeStruct((N_EXPERTS, rp, D_MODEL), bf16),
            jax.ShapeDtypeStruct((N_EXPERTS, rs, D_MODEL), bf16),
        ),
        compiler_params=_params("parallel"),
        name="experts",
    )(xg_p, vals_p, xg_s, vals_s, wg, wu, wd)


def _scatter_kernel(pos_ref, y_ref, x1_ref, m_ref, fn_ref, o_ref, *, tn, cap, ne, sb):
    g2 = m_ref[5:6, :]
    for g in range(sb):
        rows = slice(g * cap, (g + 1) * cap)
        moe = jnp.zeros((tn, D_MODEL), f32)
        for e0 in range(0, N_EXPERTS, ne):
            onehot = _one_hot([_slot_hits(pos_ref.at[g], e0 + e, cap) for e in range(ne)])
            y = jnp.concatenate([y_ref[e0 + e, rows, :] for e in range(ne)], axis=0)
            moe = moe + lax.dot_general(onehot, y, (((0,), (0,)), ((), ())), preferred_element_type=f32)
        xo = x1_ref[g] + g2 * moe
        o_ref[g] = _rms(xo, fn_ref[...])


def _scatter(pos3, y, x13, m3, fn, *, cap, ne, tn, sb):
    nseq, _, n = pos3.shape
    assert sb == 1 or m3.shape[0] == 1, "sequences sharing a grid step must share their modulation rows"
    return pl.pallas_call(
        functools.partial(_scatter_kernel, tn=tn, cap=cap, ne=ne, sb=sb),
        grid=(nseq // sb, n // tn),
        in_specs=[
            pl.BlockSpec((sb, N_EXPERTS, tn), lambda s, i: (s, 0, i)),
            pl.BlockSpec((N_EXPERTS, sb * cap, D_MODEL), lambda s, i: (0, s, 0)),
            pl.BlockSpec((sb, tn, D_MODEL), lambda s, i: (s, i, 0)),
            _mod_spec(m3, lambda s, i: s),
            _const_spec((1, D_MODEL)),
        ],
        out_specs=pl.BlockSpec((sb, tn, D_MODEL), lambda s, i: (s, i, 0)),
        out_shape=jax.ShapeDtypeStruct((nseq, n, D_MODEL), f32),
        compiler_params=_params("parallel", "arbitrary"),
        name="scatter",
    )(pos3, y, x13, m3, fn)


def _rope_tables(n):
    t = np.arange(n)
    half = QK_ROPE // 2
    freqs = ROPE_BASE ** (-np.arange(0, half, 2, dtype=np.float64) / half)
    ang_r = (t // GRID_W)[:, None] * freqs
    ang_c = (t % GRID_W)[:, None] * freqs
    cr, sr, cc, sc = np.cos(ang_r), np.sin(ang_r), np.cos(ang_c), np.sin(ang_c)
    cos = np.ones((n, HEAD_W))
    sin = np.zeros((n, HEAD_W))
    cos[:, ROPE_OFF:ROPE_OFF + QK_ROPE] = np.concatenate([cr, cr, cc, cc], axis=-1)
    sin[:, ROPE_OFF:ROPE_OFF + QK_ROPE] = np.concatenate([-sr, sr, -sc, sc], axis=-1)
    return jnp.asarray(cos, f32), jnp.asarray(sin, f32)


_PARTNER = np.concatenate([np.arange(8, 16), np.arange(0, 8), np.arange(24, 32), np.arange(16, 24)])


def _rope_partner(w):
    q = QK_ROPE // 4
    return jnp.concatenate([w[..., q:2 * q], w[..., 0:q], w[..., 3 * q:4 * q], w[..., 2 * q:3 * q]], axis=-1)


def _rope_placement():
    place = np.zeros((LANES, 2 * LANES), np.float32)
    d = np.arange(QK_ROPE)
    place[d, ROPE_OFF + d] = 1.0
    place[_PARTNER, LANES + ROPE_OFF + d] = 1.0
    return jnp.asarray(place, bf16)


def _head_blocks(w_nope, w_rope):
    rows = w_nope.shape[0]
    if w_rope is None:
        w_rope = jnp.zeros((rows, N_HEADS, QK_ROPE), w_nope.dtype)
    z = jnp.zeros((rows, N_HEADS, HEAD_W - QK_NOPE - QK_ROPE), w_nope.dtype)
    return jnp.concatenate([w_nope, w_rope, z], axis=-1).reshape(rows, N_HEADS * HEAD_W)


def kernel(x_prompt, x_sample, cache_ckv, cache_kpe, c, c_ctx, w_ada, b_ada, norm1, w_in, conv_dw, conv_dw_b,
           conv_ln_g, conv_ln_b, w_conv_out, q_norm, w_qb, kv_norm, w_kvb, w_o_mla, w_out, norm2, w_router,
           w_e_gate, w_e_up, w_e_down, final_norm):
    assert w_ada.shape[0] == 1, "single trunk layer"
    nb_p, n_p, _ = x_prompt.shape
    nb_s, n_s, _ = x_sample.shape

    win = _wprep(w_in[0].T, _rope_placement())
    wq = w_qb[0].reshape(Q_LORA, N_HEADS, QK_NOPE + QK_ROPE)
    wqb = _head_blocks(wq[..., :QK_NOPE], wq[..., QK_NOPE:]).astype(bf16)
    wqbp = _head_blocks(jnp.zeros_like(wq[..., :QK_NOPE]), _rope_partner(wq[..., QK_NOPE:])).astype(bf16)
    wkv = w_kvb[0].reshape(KV_LORA, N_HEADS, QK_NOPE + V_DIM)
    wk = _head_blocks(wkv[..., :QK_NOPE], None).astype(bf16)
    wv = wkv[..., QK_NOPE:].reshape(KV_LORA, N_HEADS * V_DIM).astype(bf16)
    wco = w_conv_out[0].astype(bf16)
    wo = w_o_mla[0].astype(bf16)
    wout = w_out[0].astype(bf16)
    wr_hi = w_router[0].astype(bf16)
    wr_lo = (w_router[0] - wr_hi.astype(f32)).astype(bf16)
    wr = jnp.concatenate([wr_hi, wr_lo, jnp.zeros((D_MODEL, LANES - 2 * N_EXPERTS), bf16)], axis=-1)
    row = lambda a: a.reshape(1, -1)

    mod = jnp.concatenate([c_ctx[None, :], c, jnp.zeros((8 - 1 - nb_s, D_MODEL), f32)], axis=0)
    m = _ada(mod, w_ada[0], b_ada[0]).reshape(8, 6, D_MODEL)
    m_p, m_s = m[0:1], m[1:1 + nb_s]
    cos, sin = _rope_tables(n_s)

    def mixers(x, m3, rope, ctx_ckv, ctx_kpe):
        nseq, n, _ = x.shape
        x2d = x.reshape(nseq * n, D_MODEL)
        v, q, ckv, kr, g, *kpe = _inproj(x2d, m3, row(norm1[0]), win, row(q_norm[0]), wqb, wqbp, row(kv_norm[0]),
                                         cos, sin, n=n, rope=rope, tm=IN_TILE)
        cm = _conv(v.reshape(nseq, n, CONV_DIM), g.reshape(nseq, n, 2 * D_MODEL), conv_dw[0], row(conv_dw_b[0]),
                   row(conv_ln_g[0]), row(conv_ln_b[0]), wco)
        keys_ckv = ckv.reshape(nseq, n, KV_LORA)
        keys_kpe = kr.reshape(nseq, n, LANES)
        if ctx_ckv is not None:
            keys_ckv = jnp.concatenate([ctx_ckv.astype(keys_ckv.dtype), keys_ckv], axis=1)
            keys_kpe = jnp.concatenate([ctx_kpe.astype(keys_kpe.dtype), keys_kpe], axis=1)
        mg = _attn(q, keys_ckv, keys_kpe, cm.reshape(nseq * n, D_MODEL), g, wk, wv, wo, n=n, tq=min(n, Q_TILE))
        x1, h2, lg = _outproj(mg, x2d, m3, wout, row(norm2[0]), wr, n=n)
        return x1, h2, lg, ckv, kpe

    ctx_kpe = jnp.pad(cache_kpe[:, 0], ((0, 0), (0, 0), (ROPE_OFF, LANES - ROPE_OFF - QK_ROPE)))
    x1_p, h2_p, lg_p, ckv_p, (kpe_p,) = mixers(x_prompt, m_p, False, None, None)
    x1_s, h2_s, lg_s, _, _ = mixers(x_sample, m_s, True, cache_ckv[:, 0], ctx_kpe)

    def moe_tiles(n):
        cap = EC_FACTOR * n // N_EXPERTS
        ne = max(1, min(N_EXPERTS, MOE_ROWS // cap))
        sb = max(1, MOE_TOKENS // n)
        return cap, ne, sb

    def route_gather(h2, lg, nseq, n):
        cap, ne, sb = moe_tiles(n)
        pos, aff = _route(lg.reshape(nseq, n, LANES), cap=cap)
        pos3 = pos.reshape(nseq, N_EXPERTS, n)
        xg, vals = _gather(pos3, aff.reshape(nseq, N_EXPERTS, n), h2.reshape(nseq, n, D_MODEL),
                           cap=cap, ne=ne, sb=sb)
        return pos3, xg, vals

    pos_p, xg_p, vals_p = route_gather(h2_p, lg_p, nb_p, n_p)
    pos_s, xg_s, vals_s = route_gather(h2_s, lg_s, nb_s, n_s)
    y_p, y_s = _experts(xg_p, vals_p, xg_s, vals_s, w_e_gate[0], w_e_up[0], w_e_down[0])
    fn = row(final_norm)

    def scatter(pos, y, x1, m3, nseq, n):
        cap, ne, sb = moe_tiles(n)
        return _scatter(pos, y, x1.reshape(nseq, n, D_MODEL), m3, fn, cap=cap, ne=ne, tn=min(n, SCATTER_TILE), sb=sb)

    y_prompt = scatter(pos_p, y_p, x1_p, m_p, nb_p, n_p)
    y_sample = scatter(pos_s, y_s, x1_s, m_s, nb_s, n_s)

    new_ckv = ckv_p.reshape(nb_p, 1, n_p, KV_LORA)
    new_kpe = kpe_p.reshape(nb_p, 1, n_p, QK_ROPE)
    return (y_prompt, y_sample, new_ckv, new_kpe)
```

```python
import functools

import jax
import jax.numpy as jnp
import numpy as np
from jax import lax
from jax.experimental import pallas as pl
from jax.experimental.pallas import tpu as pltpu

D_MODEL = 1024
GRID_W = 64
CONV_DIM = 512
CONV_WIDTH = 31
N_HEADS = 8
QK_NOPE = 64
QK_ROPE = 32
V_DIM = 64
Q_LORA = 256
KV_LORA = 128
N_EXPERTS = 16
EXPERT_FF = 512
EC_FACTOR = 2
ROPE_BASE = 10000.0
EPS = 1e-6

LANES = 128
HEAD_W = LANES
ROPE_OFF = QK_NOPE
CONV_HALO = 16
LOG2E = 1.4426950408889634
VMEM_LIMIT = 48 * 1024 * 1024
IN_TILE = 512
Q_TILE = 512
MOE_ROWS = 512
MOE_TOKENS = 1024
SCATTER_TILE = 512

C_CONV = 0
C_QA = 2 * CONV_DIM
C_KVA = C_QA + Q_LORA
C_KR = C_KVA + KV_LORA
C_GATE = C_KR + LANES
C_END = C_GATE + 2 * D_MODEL
C_KRP = C_END

f32 = jnp.float32
bf16 = jnp.bfloat16


def _params(*sem):
    return pltpu.CompilerParams(dimension_semantics=sem, vmem_limit_bytes=VMEM_LIMIT)


def _dot(a, b):
    return jnp.dot(a, b, preferred_element_type=f32)


def _rms(x, g):
    return x * lax.rsqrt(jnp.mean(x * x, axis=-1, keepdims=True) + EPS) * g


def _const_spec(shape):
    nd = len(shape)
    return pl.BlockSpec(shape, lambda *_: (0,) * nd)


def _mod_spec(m3, seq_of):
    if m3.shape[0] == 1:
        return _const_spec((None, 6, D_MODEL))
    return pl.BlockSpec((None, 6, D_MODEL), lambda *idx: (seq_of(*idx), 0, 0))


def _ada_kernel(s_ref, w_ref, b_ref, o_ref):
    s = s_ref[...]
    s = s * jax.nn.sigmoid(s)
    o_ref[...] = _dot(s.astype(bf16), w_ref[...].astype(bf16)) + b_ref[...]


def _ada(mod, w_ada, b_ada):
    rows = mod.shape[0]
    n_out = w_ada.shape[1]
    tn = D_MODEL
    return pl.pallas_call(
        _ada_kernel,
        grid=(n_out // tn,),
        in_specs=[
            _const_spec((rows, D_MODEL)),
            pl.BlockSpec((D_MODEL, tn), lambda j: (0, j)),
            pl.BlockSpec((1, tn), lambda j: (0, j)),
        ],
        out_specs=pl.BlockSpec((rows, tn), lambda j: (0, j)),
        out_shape=jax.ShapeDtypeStruct((rows, n_out), f32),
        compiler_params=_params("arbitrary"),
        name="ada",
    )(mod, w_ada, b_ada.reshape(1, n_out))


def _wprep_kernel(wt_ref, place_ref, o_ref):
    def block(r0):
        return wt_ref[r0:r0 + LANES, :].T.astype(bf16)

    for j in range(C_KR // LANES):
        o_ref[:, j * LANES:(j + 1) * LANES] = block(j * LANES)
    placed = _dot(block(C_KR), place_ref[...])
    o_ref[:, C_KR:C_GATE] = placed[:, 0:LANES].astype(bf16)
    o_ref[:, C_KRP:C_KRP + LANES] = placed[:, LANES:2 * LANES].astype(bf16)
    for j in range(2 * D_MODEL // LANES):
        o_ref[:, C_GATE + j * LANES:C_GATE + (j + 1) * LANES] = block(C_KR + QK_ROPE + j * LANES)


def _wprep(w_in_t, place, *, tr=LANES):
    cols, rows = w_in_t.shape
    return pl.pallas_call(
        _wprep_kernel,
        grid=(rows // tr,),
        in_specs=[pl.BlockSpec((cols, tr), lambda i: (0, i)), _const_spec(place.shape)],
        out_specs=pl.BlockSpec((tr, C_KRP + LANES), lambda i: (i, 0)),
        out_shape=jax.ShapeDtypeStruct((rows, C_KRP + LANES), bf16),
        compiler_params=_params("parallel"),
        name="wprep",
    )(w_in_t, place)


def _inproj_kernel(*refs, rope, q_scale):
    if rope:
        (x_ref, m_ref, n1_ref, win_ref, qn_ref, wqb_ref, wqbp_ref, kvn_ref, cos_ref, sin_ref,
         v_ref, q_ref, ckv_ref, kr_ref, g_ref) = refs
    else:
        (x_ref, m_ref, n1_ref, win_ref, qn_ref, wqb_ref, kvn_ref,
         v_ref, q_ref, ckv_ref, kr_ref, g_ref, kpe_ref) = refs
    sh1 = m_ref[0:1, :]
    sc1 = m_ref[1:2, :]
    h = _rms(x_ref[...], n1_ref[...]) * (1.0 + sc1) + sh1
    hb = h.astype(bf16)

    a = _dot(hb, win_ref[:, C_CONV:C_CONV + CONV_DIM])
    b = _dot(hb, win_ref[:, C_CONV + CONV_DIM:C_QA])
    v_ref[...] = (a * jax.nn.sigmoid(b)).astype(v_ref.dtype)

    qa = _dot(hb, win_ref[:, C_QA:C_KVA])
    qn = _rms(qa, qn_ref[...]).astype(bf16)
    q = _dot(qn, wqb_ref[...])
    if rope:
        qp = _dot(qn, wqbp_ref[...])
        cos = cos_ref[...]
        sin = sin_ref[...]
        for hd in range(N_HEADS):
            sl = slice(hd * HEAD_W, (hd + 1) * HEAD_W)
            q_ref[:, sl] = ((q[:, sl] * cos + qp[:, sl] * sin) * q_scale).astype(q_ref.dtype)
    else:
        q_ref[...] = (q * q_scale).astype(q_ref.dtype)

    kva = _dot(hb, win_ref[:, C_KVA:C_KR])
    ckv_ref[...] = _rms(kva, kvn_ref[...]).astype(ckv_ref.dtype)

    kr = _dot(hb, win_ref[:, C_KR:C_GATE])
    if rope:
        krp = _dot(hb, win_ref[:, C_KRP:C_KRP + LANES])
        kr = kr * cos_ref[...] + krp * sin_ref[...]
    else:
        kpe_ref[...] = kr[:, ROPE_OFF:ROPE_OFF + QK_ROPE]
    kr_ref[...] = kr.astype(kr_ref.dtype)

    gw = 512
    for j in range(2 * D_MODEL // gw):
        gl = _dot(hb, win_ref[:, C_GATE + j * gw:C_GATE + (j + 1) * gw])
        g_ref[:, j * gw:(j + 1) * gw] = jax.nn.sigmoid(gl).astype(g_ref.dtype)


def _inproj(x2d, m3, norm1, win, q_norm, wqb, wqbp, kv_norm, cos, sin, *, n, rope, tm=256):
    tokens = x2d.shape[0]
    assert n % tm == 0 or (tm % n == 0 and m3.shape[0] == 1 and not rope)
    tiles_per_seq = max(1, n // tm)
    q_scale = float((QK_NOPE + QK_ROPE) ** -0.5 * LOG2E)
    tile = lambda w: pl.BlockSpec((tm, w), lambda i: (i, 0))
    in_specs = [
        tile(D_MODEL),
        _mod_spec(m3, lambda i: i // tiles_per_seq),
        _const_spec((1, D_MODEL)),
        _const_spec((D_MODEL, C_KRP + LANES if rope else C_END)),
        _const_spec((1, Q_LORA)),
        _const_spec(wqb.shape),
    ]
    args = [x2d, m3, norm1, win, q_norm, wqb]
    if rope:
        in_specs.append(_const_spec(wqbp.shape))
        args.append(wqbp)
    in_specs.append(_const_spec((1, KV_LORA)))
    args.append(kv_norm)
    if rope:
        tab = pl.BlockSpec((tm, LANES), lambda i: (i % tiles_per_seq, 0))
        in_specs += [tab, tab]
        args += [cos, sin]
    out_shape = [
        jax.ShapeDtypeStruct((tokens, CONV_DIM), bf16),
        jax.ShapeDtypeStruct((tokens, N_HEADS * HEAD_W), bf16),
        jax.ShapeDtypeStruct((tokens, KV_LORA), bf16 if rope else f32),
        jax.ShapeDtypeStruct((tokens, LANES), bf16),
        jax.ShapeDtypeStruct((tokens, 2 * D_MODEL), bf16),
    ]
    out_specs = [tile(CONV_DIM), tile(N_HEADS * HEAD_W), tile(KV_LORA), tile(LANES), tile(2 * D_MODEL)]
    if not rope:
        out_shape.append(jax.ShapeDtypeStruct((tokens, QK_ROPE), f32))
        out_specs.append(tile(QK_ROPE))
    return pl.pallas_call(
        functools.partial(_inproj_kernel, rope=rope, q_scale=q_scale),
        grid=(tokens // tm,),
        in_specs=in_specs,
        out_specs=out_specs,
        out_shape=out_shape,
        compiler_params=_params("parallel"),
        name="inproj_rope" if rope else "inproj",
    )(*args)


def _conv_kernel(v_ref, g_ref, shift_ref, dw_ref, dwb_ref, lng_ref, lnb_ref, wco_ref, o_ref, vpad, ybuf,
                 *, n, rt, ct):
    pad = CONV_WIDTH // 2
    zeros = jnp.zeros((CONV_HALO, CONV_DIM), vpad.dtype)
    vpad[0:CONV_HALO, :] = zeros
    vpad[CONV_HALO + n:2 * CONV_HALO + n, :] = zeros
    vpad[CONV_HALO:CONV_HALO + n, :] = v_ref[...]

    sub = 8
    span = ((CONV_HALO - pad + CONV_WIDTH - 1) // sub) * sub
    cw = 2 * LANES

    def conv_chunk(c, carry):
        r0 = pl.multiple_of(c * ct, ct)
        for cb in range(CONV_DIM // cw):
            sl = slice(cb * cw, (cb + 1) * cw)
            win = vpad[pl.ds(r0, ct + 2 * CONV_HALO), sl]
            acc = jnp.zeros((ct // sub, sub, cw), f32)
            for ph in range(sub):
                wph = _dot(shift_ref[ph], win)
                for a in range(span // sub + 1):
                    k = a * sub + ph - (CONV_HALO - pad)
                    if 0 <= k < CONV_WIDTH:
                        acc = acc + wph[a * sub:a * sub + ct, :].reshape(ct // sub, sub, cw) * dw_ref[k, :, sl][None]
            ybuf[pl.ds(r0, ct), sl] = acc.reshape(ct, cw) + dwb_ref[:, sl]
        return carry

    lax.fori_loop(0, n // ct, conv_chunk, 0, unroll=4)

    def chunk(c, carry):
        r0 = pl.multiple_of(c * rt, rt)
        y = ybuf[pl.ds(r0, rt), :]
        mu = jnp.mean(y, axis=-1, keepdims=True)
        yc = y - mu
        var = jnp.mean(yc * yc, axis=-1, keepdims=True)
        z = yc * lax.rsqrt(var + EPS) * lng_ref[...] + lnb_ref[...]
        z = z * jax.nn.sigmoid(z)
        co = _dot(z.astype(bf16), wco_ref[...])
        o_ref[pl.ds(r0, rt), :] = (g_ref[pl.ds(r0, rt), :].astype(f32) * co).astype(o_ref.dtype)
        return carry

    lax.fori_loop(0, n // rt, chunk, 0)


def _row_shifts(rows):
    i = np.arange(rows)
    return jnp.asarray(np.stack([(i[None, :] == i[:, None] + ph) for ph in range(8)]), bf16)


def _conv(v3, g3, dw, dwb, lng, lnb, wco, *, rt=256, ct=64):
    nseq, n, _ = v3.shape
    assert v3.dtype == bf16, "the one-hot row shifts are exact only for bf16 windows"
    shifts = _row_shifts(ct + 2 * CONV_HALO)
    dw = jnp.broadcast_to(dw[:, None, :], (CONV_WIDTH, 8, CONV_DIM))
    return pl.pallas_call(
        functools.partial(_conv_kernel, n=n, rt=rt, ct=ct),
        grid=(nseq,),
        in_specs=[
            pl.BlockSpec((None, n, CONV_DIM), lambda s: (s, 0, 0)),
            pl.BlockSpec((None, n, D_MODEL), lambda s: (s, 0, 0)),
            _const_spec(shifts.shape),
            _const_spec(dw.shape),
            _const_spec((1, CONV_DIM)),
            _const_spec((1, CONV_DIM)),
            _const_spec((1, CONV_DIM)),
            _const_spec(wco.shape),
        ],
        out_specs=pl.BlockSpec((None, n, D_MODEL), lambda s: (s, 0, 0)),
        out_shape=jax.ShapeDtypeStruct((nseq, n, D_MODEL), bf16),
        scratch_shapes=[
            pltpu.VMEM((n + 2 * CONV_HALO, CONV_DIM), bf16),
            pltpu.VMEM((n, CONV_DIM), f32),
        ],
        compiler_params=_params("parallel"),
        name="conv",
    )(v3, g3, shifts, dw, dwb, lng, lnb, wco)


def _attn_kernel(q_ref, ckv_ref, kpe_ref, cm_ref, g_ref, wk_ref, wv_ref, wo_ref, o_ref, k_scr, v_scr):
    @pl.when(pl.program_id(1) == 0)
    def _():
        ckv = ckv_ref[...].astype(bf16)
        kpe = kpe_ref[...].astype(f32)
        k = _dot(ckv, wk_ref[...])
        v = _dot(ckv, wv_ref[...])
        ones = jnp.ones((v.shape[0], LANES), bf16)
        for hd in range(N_HEADS):
            k_scr[hd] = (k[:, hd * HEAD_W:(hd + 1) * HEAD_W] + kpe).astype(bf16)
        for j in range(N_HEADS // 2):
            v_scr[j, :, 0:LANES] = v[:, j * LANES:(j + 1) * LANES].astype(bf16)
            v_scr[j, :, LANES:2 * LANES] = ones

    tq = q_ref.shape[0]
    lane = lax.broadcasted_iota(jnp.int32, (tq, LANES), 1)
    pairs = []
    for j in range(N_HEADS // 2):
        outs = []
        for hh in range(2):
            hd = 2 * j + hh
            qh = q_ref[:, hd * HEAD_W:(hd + 1) * HEAD_W]
            s = lax.dot_general(qh, k_scr[hd], (((1,), (1,)), ((), ())), preferred_element_type=f32)
            mx = jnp.max(s, axis=-1, keepdims=True)
            p = jnp.exp2(s - mx).astype(bf16)
            r = _dot(p, v_scr[j])
            outs.append(r[:, 0:LANES] * (1.0 / r[:, LANES:LANES + 1]))
        pairs.append(jnp.where(lane < V_DIM, outs[0], outs[1]))
    attn = jnp.concatenate(pairs, axis=-1).astype(bf16)
    ao = _dot(attn, wo_ref[...])
    o_ref[...] = (cm_ref[...].astype(f32) + g_ref[...].astype(f32) * ao).astype(o_ref.dtype)


def _attn(q2d, ckv3, kpe3, cm2d, g2d, wk, wv, wo, *, n, tq=256):
    nseq, s_len, _ = ckv3.shape
    qb = n // tq
    tile = lambda w, c=0: pl.BlockSpec((tq, w), lambda s, i: (s * qb + i, c))
    return pl.pallas_call(
        _attn_kernel,
        grid=(nseq, qb),
        in_specs=[
            tile(N_HEADS * HEAD_W),
            pl.BlockSpec((None, s_len, KV_LORA), lambda s, i: (s, 0, 0)),
            pl.BlockSpec((None, s_len, LANES), lambda s, i: (s, 0, 0)),
            tile(D_MODEL),
            tile(D_MODEL, 1),
            _const_spec(wk.shape),
            _const_spec(wv.shape),
            _const_spec(wo.shape),
        ],
        out_specs=tile(D_MODEL),
        out_shape=jax.ShapeDtypeStruct((nseq * n, D_MODEL), bf16),
        scratch_shapes=[
            pltpu.VMEM((N_HEADS, s_len, HEAD_W), bf16),
            pltpu.VMEM((N_HEADS // 2, s_len, 2 * LANES), bf16),
        ],
        compiler_params=_params("parallel", "arbitrary"),
        name="attn",
    )(q2d, ckv3, kpe3, cm2d, g2d, wk, wv, wo)


def _outproj_kernel(mg_ref, x_ref, m_ref, wout_ref, n2_ref, wr_ref, x1_ref, h2_ref, lg_ref):
    g1 = m_ref[2:3, :]
    sh2 = m_ref[3:4, :]
    sc2 = m_ref[4:5, :]
    x1 = x_ref[...] + g1 * _dot(mg_ref[...], wout_ref[...])
    x1_ref[...] = x1
    h2 = _rms(x1, n2_ref[...]) * (1.0 + sc2) + sh2
    hi = h2.astype(bf16)
    h2_ref[...] = hi
    lo = (h2 - hi.astype(f32)).astype(bf16)
    s = _dot(hi, wr_ref[...]) + _dot(lo, wr_ref[...])
    lg_ref[...] = s + pltpu.roll(s, LANES - N_EXPERTS, axis=1)


def _outproj(mg2d, x2d, m3, wout, norm2, wr, *, n, tm=256):
    tokens = x2d.shape[0]
    tiles_per_seq = n // tm
    tile = lambda w: pl.BlockSpec((tm, w), lambda i: (i, 0))
    return pl.pallas_call(
        _outproj_kernel,
        grid=(tokens // tm,),
        in_specs=[
            tile(D_MODEL),
            tile(D_MODEL),
            _mod_spec(m3, lambda i: i // tiles_per_seq),
            _const_spec(wout.shape),
            _const_spec((1, D_MODEL)),
            _const_spec(wr.shape),
        ],
        out_specs=(tile(D_MODEL), tile(D_MODEL), tile(LANES)),
        out_shape=(
            jax.ShapeDtypeStruct((tokens, D_MODEL), f32),
            jax.ShapeDtypeStruct((tokens, D_MODEL), bf16),
            jax.ShapeDtypeStruct((tokens, LANES), f32),
        ),
        compiler_params=_params("parallel"),
        name="outproj",
    )(mg2d, x2d, m3, wout, norm2, wr)


def _route_kernel(lg_ref, pos_ref, aff_ref, *, nseq, n, cap):
    for s in range(nseq):
        lt = lg_ref[s].T[0:N_EXPERTS, :]
        e = jnp.exp(lt - jnp.max(lt, axis=0, keepdims=True))
        aff_ref[s * N_EXPERTS:(s + 1) * N_EXPERTS, :] = e / jnp.sum(e, axis=0, keepdims=True)
    rows = nseq * N_EXPERTS
    capf = float(cap)

    def bit_step(i, t):
        cand = t | (jnp.int32(1) << (30 - i))
        thr = lax.bitcast_convert_type(cand, f32)
        cnt = jnp.sum(jnp.where(aff_ref[...] >= thr, 1.0, 0.0), axis=1, keepdims=True)
        return jnp.where(cnt >= capf, cand, t)

    t = lax.fori_loop(0, 31, bit_step, jnp.zeros((rows, 1), jnp.int32))
    thr = lax.bitcast_convert_type(t, f32)
    need = capf - jnp.sum(jnp.where(aff_ref[...] > thr, 1.0, 0.0), axis=1, keepdims=True)

    blk = 2 * LANES
    tri = jnp.where(
        lax.broadcasted_iota(jnp.int32, (blk, blk), 0) < lax.broadcasted_iota(jnp.int32, (blk, blk), 1),
        1.0, 0.0).astype(bf16)
    carry_gt = jnp.zeros((rows, 1), f32)
    carry_eq = jnp.zeros((rows, 1), f32)
    for b in range(n // blk):
        sl = slice(b * blk, (b + 1) * blk)
        ab = aff_ref[:, sl]
        gt = ab > thr
        eq = ab == thr
        gtb = jnp.where(gt, 1.0, 0.0)
        eqb = jnp.where(eq, 1.0, 0.0)
        pre_gt = _dot(gtb.astype(bf16), tri) + carry_gt
        pre_eq = _dot(eqb.astype(bf16), tri) + carry_eq
        carry_gt = carry_gt + jnp.sum(gtb, axis=1, keepdims=True)
        carry_eq = carry_eq + jnp.sum(eqb, axis=1, keepdims=True)
        sel = gt | (eq & (pre_eq < need))
        slot = pre_gt + jnp.minimum(pre_eq, need)
        pos_ref[:, sl] = jnp.where(sel, slot, -1.0).astype(jnp.int32)


def _route(lg3, *, cap):
    nseq, n, _ = lg3.shape
    rows = nseq * N_EXPERTS
    return pl.pallas_call(
        functools.partial(_route_kernel, nseq=nseq, n=n, cap=cap),
        grid=(1,),
        in_specs=[_const_spec(lg3.shape)],
        out_specs=(_const_spec((rows, n)), _const_spec((rows, n))),
        out_shape=(jax.ShapeDtypeStruct((rows, n), jnp.int32), jax.ShapeDtypeStruct((rows, n), f32)),
        compiler_params=_params("arbitrary"),
        name="route",
    )(lg3)


def _slot_hits(pos_ref, e, cap):
    width = pos_ref.shape[1]
    return lax.broadcasted_iota(jnp.int32, (cap, width), 0) == pos_ref[e:e + 1, :]


def _one_hot(hits):
    return jnp.concatenate([jnp.where(h, 1.0, 0.0).astype(bf16) for h in hits], axis=0)


def _gather_kernel(pos_ref, aff_ref, h2_ref, xg_ref, vals_ref, *, cap, ne, sb):
    for g in range(sb):
        rows = slice(g * cap, (g + 1) * cap)
        for e0 in range(0, N_EXPERTS, ne):
            hits = [_slot_hits(pos_ref.at[g], e0 + e, cap) for e in range(ne)]
            xg = _dot(_one_hot(hits), h2_ref[g]).astype(xg_ref.dtype)
            for e in range(ne):
                xg_ref[e0 + e, rows, :] = xg[e * cap:(e + 1) * cap, :]
                vals = jnp.sum(jnp.where(hits[e], aff_ref[g, e0 + e:e0 + e + 1, :], 0.0), axis=1, keepdims=True)
                vals_ref[e0 + e, rows, :] = jnp.broadcast_to(vals, (cap, LANES))


def _gather(pos3, aff3, h23, *, cap, ne, sb):
    nseq, _, n = pos3.shape
    return pl.pallas_call(
        functools.partial(_gather_kernel, cap=cap, ne=ne, sb=sb),
        grid=(nseq // sb,),
        in_specs=[
            pl.BlockSpec((sb, N_EXPERTS, n), lambda s: (s, 0, 0)),
            pl.BlockSpec((sb, N_EXPERTS, n), lambda s: (s, 0, 0)),
            pl.BlockSpec((sb, n, D_MODEL), lambda s: (s, 0, 0)),
        ],
        out_specs=(
            pl.BlockSpec((N_EXPERTS, sb * cap, D_MODEL), lambda s: (0, s, 0)),
            pl.BlockSpec((N_EXPERTS, sb * cap, LANES), lambda s: (0, s, 0)),
        ),
        out_shape=(
            jax.ShapeDtypeStruct((N_EXPERTS, nseq * cap, D_MODEL), bf16),
            jax.ShapeDtypeStruct((N_EXPERTS, nseq * cap, LANES), f32),
        ),
        compiler_params=_params("parallel"),
        name="gather",
    )(pos3, aff3, h23)


def _experts_kernel(xp_ref, vp_ref, xs_ref, vs_ref, wg_ref, wu_ref, wd_ref, yp_ref, ys_ref, *, rc):
    wg = wg_ref[...].astype(bf16)
    wu = wu_ref[...].astype(bf16)
    wd = wd_ref[...].astype(bf16)
    for x_ref, v_ref, y_ref in ((xp_ref, vp_ref, yp_ref), (xs_ref, vs_ref, ys_ref)):
        for r0 in range(0, x_ref.shape[0], rc):
            x = x_ref[r0:r0 + rc, :]
            a = _dot(x, wg)
            u = _dot(x, wu)
            hm = (a * jax.nn.sigmoid(a) * u).astype(bf16)
            y = _dot(hm, wd) * v_ref[r0:r0 + rc, 0:1]
            y_ref[r0:r0 + rc, :] = y.astype(y_ref.dtype)


def _experts(xg_p, vals_p, xg_s, vals_s, wg, wu, wd, *, rc=512):
    rp = xg_p.shape[1]
    rs = xg_s.shape[1]
    per_e = lambda r, w: pl.BlockSpec((None, r, w), lambda e: (e, 0, 0))
    return pl.pallas_call(
        functools.partial(_experts_kernel, rc=rc),
        grid=(N_EXPERTS,),
        in_specs=[
            per_e(rp, D_MODEL), per_e(rp, LANES), per_e(rs, D_MODEL), per_e(rs, LANES),
            per_e(D_MODEL, EXPERT_FF), per_e(D_MODEL, EXPERT_FF), per_e(EXPERT_FF, D_MODEL),
        ],
        out_specs=(per_e(rp, D_MODEL), per_e(rs, D_MODEL)),
        out_shape=(
            jax.ShapeDtypeStruct((N_EXPERTS, rp, D_MODEL), bf16),
            jax.ShapeDtypeStruct((N_EXPERTS, rs, D_MODEL), bf16),
        ),
        compiler_params=_params("parallel"),
        name="experts",
    )(xg_p, vals_p, xg_s, vals_s, wg, wu, wd)


def _scatter_kernel(pos_ref, y_ref, x1_ref, m_ref, fn_ref, o_ref, *, tn, cap, ne, sb):
    g2 = m_ref[5:6, :]
    for g in range(sb):
        rows = slice(g * cap, (g + 1) * cap)
        moe = jnp.zeros((tn, D_MODEL), f32)
        for e0 in range(0, N_EXPERTS, ne):
            onehot = _one_hot([_slot_hits(pos_ref.at[g], e0 + e, cap) for e in range(ne)])
            y = jnp.concatenate([y_ref[e0 + e, rows, :] for e in range(ne)], axis=0)
            moe = moe + lax.dot_general(onehot, y, (((0,), (0,)), ((), ())), preferred_element_type=f32)
        xo = x1_ref[g] + g2 * moe
        o_ref[g] = _rms(xo, fn_ref[...])


def _scatter(pos3, y, x13, m3, fn, *, cap, ne, tn, sb):
    nseq, _, n = pos3.shape
    assert sb == 1 or m3.shape[0] == 1, "sequences sharing a grid step must share their modulation rows"
    return pl.pallas_call(
        functools.partial(_scatter_kernel, tn=tn, cap=cap, ne=ne, sb=sb),
        grid=(nseq // sb, n // tn),
        in_specs=[
            pl.BlockSpec((sb, N_EXPERTS, tn), lambda s, i: (s, 0, i)),
            pl.BlockSpec((N_EXPERTS, sb * cap, D_MODEL), lambda s, i: (0, s, 0)),
            pl.BlockSpec((sb, tn, D_MODEL), lambda s, i: (s, i, 0)),
            _mod_spec(m3, lambda s, i: s),
            _const_spec((1, D_MODEL)),
        ],
        out_specs=pl.BlockSpec((sb, tn, D_MODEL), lambda s, i: (s, i, 0)),
        out_shape=jax.ShapeDtypeStruct((nseq, n, D_MODEL), f32),
        compiler_params=_params("parallel", "arbitrary"),
        name="scatter",
    )(pos3, y, x13, m3, fn)


def _rope_tables(n):
    t = np.arange(n)
    half = QK_ROPE // 2
    freqs = ROPE_BASE ** (-np.arange(0, half, 2, dtype=np.float64) / half)
    ang_r = (t // GRID_W)[:, None] * freqs
    ang_c = (t % GRID_W)[:, None] * freqs
    cr, sr, cc, sc = np.cos(ang_r), np.sin(ang_r), np.cos(ang_c), np.sin(ang_c)
    cos = np.ones((n, HEAD_W))
    sin = np.zeros((n, HEAD_W))
    cos[:, ROPE_OFF:ROPE_OFF + QK_ROPE] = np.concatenate([cr, cr, cc, cc], axis=-1)
    sin[:, ROPE_OFF:ROPE_OFF + QK_ROPE] = np.concatenate([-sr, sr, -sc, sc], axis=-1)
    return jnp.asarray(cos, f32), jnp.asarray(sin, f32)


_PARTNER = np.concatenate([np.arange(8, 16), np.arange(0, 8), np.arange(24, 32), np.arange(16, 24)])


def _rope_partner(w):
    q = QK_ROPE // 4
    return jnp.concatenate([w[..., q:2 * q], w[..., 0:q], w[..., 3 * q:4 * q], w[..., 2 * q:3 * q]], axis=-1)


def _rope_placement():
    place = np.zeros((LANES, 2 * LANES), np.float32)
    d = np.arange(QK_ROPE)
    place[d, ROPE_OFF + d] = 1.0
    place[_PARTNER, LANES + ROPE_OFF + d] = 1.0
    return jnp.asarray(place, bf16)


def _head_blocks(w_nope, w_rope):
    rows = w_nope.shape[0]
    if w_rope is None:
        w_rope = jnp.zeros((rows, N_HEADS, QK_ROPE), w_nope.dtype)
    z = jnp.zeros((rows, N_HEADS, HEAD_W - QK_NOPE - QK_ROPE), w_nope.dtype)
    return jnp.concatenate([w_nope, w_rope, z], axis=-1).reshape(rows, N_HEADS * HEAD_W)


def kernel(x_prompt, x_sample, cache_ckv, cache_kpe, c, c_ctx, w_ada, b_ada, norm1, w_in, conv_dw, conv_dw_b,
           conv_ln_g, conv_ln_b, w_conv_out, q_norm, w_qb, kv_norm, w_kvb, w_o_mla, w_out, norm2, w_router,
           w_e_gate, w_e_up, w_e_down, final_norm):
    assert w_ada.shape[0] == 1, "single trunk layer"
    nb_p, n_p, _ = x_prompt.shape
    nb_s, n_s, _ = x_sample.shape

    win = _wprep(w_in[0].T, _rope_placement())
    wq = w_qb[0].reshape(Q_LORA, N_HEADS, QK_NOPE + QK_ROPE)
    wqb = _head_blocks(wq[..., :QK_NOPE], wq[..., QK_NOPE:]).astype(bf16)
    wqbp = _head_blocks(jnp.zeros_like(wq[..., :QK_NOPE]), _rope_partner(wq[..., QK_NOPE:])).astype(bf16)
    wkv = w_kvb[0].reshape(KV_LORA, N_HEADS, QK_NOPE + V_DIM)
    wk = _head_blocks(wkv[..., :QK_NOPE], None).astype(bf16)
    wv = wkv[..., QK_NOPE:].reshape(KV_LORA, N_HEADS * V_DIM).astype(bf16)
    wco = w_conv_out[0].astype(bf16)
    wo = w_o_mla[0].astype(bf16)
    wout = w_out[0].astype(bf16)
    wr_hi = w_router[0].astype(bf16)
    wr_lo = (w_router[0] - wr_hi.astype(f32)).astype(bf16)
    wr = jnp.concatenate([wr_hi, wr_lo, jnp.zeros((D_MODEL, LANES - 2 * N_EXPERTS), bf16)], axis=-1)
    row = lambda a: a.reshape(1, -1)

    mod = jnp.concatenate([c_ctx[None, :], c, jnp.zeros((8 - 1 - nb_s, D_MODEL), f32)], axis=0)
    m = _ada(mod, w_ada[0], b_ada[0]).reshape(8, 6, D_MODEL)
    m_p, m_s = m[0:1], m[1:1 + nb_s]
    cos, sin = _rope_tables(n_s)

    def mixers(x, m3, rope, ctx_ckv, ctx_kpe):
        nseq, n, _ = x.shape
        x2d = x.reshape(nseq * n, D_MODEL)
        v, q, ckv, kr, g, *kpe = _inproj(x2d, m3, row(norm1[0]), win, row(q_norm[0]), wqb, wqbp, row(kv_norm[0]),
                                         cos, sin, n=n, rope=rope, tm=IN_TILE)
        cm = _conv(v.reshape(nseq, n, CONV_DIM), g.reshape(nseq, n, 2 * D_MODEL), conv_dw[0], row(conv_dw_b[0]),
                   row(conv_ln_g[0]), row(conv_ln_b[0]), wco)
        keys_ckv = ckv.reshape(nseq, n, KV_LORA)
        keys_kpe = kr.reshape(nseq, n, LANES)
        if ctx_ckv is not None:
            keys_ckv = jnp.concatenate([ctx_ckv.astype(keys_ckv.dtype), keys_ckv], axis=1)
            keys_kpe = jnp.concatenate([ctx_kpe.astype(keys_kpe.dtype), keys_kpe], axis=1)
        mg = _attn(q, keys_ckv, keys_kpe, cm.reshape(nseq * n, D_MODEL), g, wk, wv, wo, n=n, tq=min(n, Q_TILE))
        x1, h2, lg = _outproj(mg, x2d, m3, wout, row(norm2[0]), wr, n=n)
        return x1, h2, lg, ckv, kpe

    ctx_kpe = jnp.pad(cache_kpe[:, 0], ((0, 0), (0, 0), (ROPE_OFF, LANES - ROPE_OFF - QK_ROPE)))
    x1_p, h2_p, lg_p, ckv_p, (kpe_p,) = mixers(x_prompt, m_p, False, None, None)
    x1_s, h2_s, lg_s, _, _ = mixers(x_sample, m_s, True, cache_ckv[:, 0], ctx_kpe)

    def moe_tiles(n):
        cap = EC_FACTOR * n // N_EXPERTS
        ne = N_EXPERTS if N_EXPERTS * cap <= MOE_ROWS else 1
        sb = max(1, MOE_TOKENS // n)
        return cap, ne, sb

    def route_gather(h2, lg, nseq, n):
        cap, ne, sb = moe_tiles(n)
        pos, aff = _route(lg.reshape(nseq, n, LANES), cap=cap)
        pos3 = pos.reshape(nseq, N_EXPERTS, n)
        xg, vals = _gather(pos3, aff.reshape(nseq, N_EXPERTS, n), h2.reshape(nseq, n, D_MODEL),
                           cap=cap, ne=ne, sb=sb)
        return pos3, xg, vals

    pos_p, xg_p, vals_p = route_gather(h2_p, lg_p, nb_p, n_p)
    pos_s, xg_s, vals_s = route_gather(h2_s, lg_s, nb_s, n_s)
    y_p, y_s = _experts(xg_p, vals_p, xg_s, vals_s, w_e_gate[0], w_e_up[0], w_e_down[0])
    fn = row(final_norm)

    def scatter(pos, y, x1, m3, nseq, n):
        cap, ne, sb = moe_tiles(n)
        return _scatter(pos, y, x1.reshape(nseq, n, D_MODEL), m3, fn, cap=cap, ne=ne, tn=min(n, SCATTER_TILE), sb=sb)

    y_prompt = scatter(pos_p, y_p, x1_p, m_p, nb_p, n_p)
    y_sample = scatter(pos_s, y_s, x1_s, m_s, nb_s, n_s)

    new_ckv = ckv_p.reshape(nb_p, 1, n_p, KV_LORA)
    new_kpe = kpe_p.reshape(nb_p, 1, n_p, QK_ROPE)
    return (y_prompt, y_sample, new_ckv, new_kpe)
```

```python
import functools

import jax
import jax.numpy as jnp
import numpy as np
from jax import lax
from jax.experimental import pallas as pl
from jax.experimental.pallas import tpu as pltpu

D_MODEL = 1024
GRID_W = 64
CONV_DIM = 512
CONV_WIDTH = 31
N_HEADS = 8
QK_NOPE = 64
QK_ROPE = 32
V_DIM = 64
Q_LORA = 256
KV_LORA = 128
N_EXPERTS = 16
EXPERT_FF = 512
EC_FACTOR = 2
ROPE_BASE = 10000.0
EPS = 1e-6

LANES = 128
HEAD_W = LANES
ROPE_OFF = QK_NOPE
CONV_HALO = 16
LOG2E = 1.4426950408889634
VMEM_LIMIT = 48 * 1024 * 1024
IN_TILE = 512
CONV_CHUNK = 64
Q_TILE = 512
MOE_ROWS = 512
MOE_TOKENS = 1024
SCATTER_TILE = 512

C_CONV = 0
C_QA = 2 * CONV_DIM
C_KVA = C_QA + Q_LORA
C_KR = C_KVA + KV_LORA
C_GATE = C_KR + LANES
C_END = C_GATE + 2 * D_MODEL
C_KRP = C_END

f32 = jnp.float32
bf16 = jnp.bfloat16


def _params(*sem):
    return pltpu.CompilerParams(dimension_semantics=sem, vmem_limit_bytes=VMEM_LIMIT)


def _dot(a, b):
    return jnp.dot(a, b, preferred_element_type=f32)


def _rms(x, g):
    return x * lax.rsqrt(jnp.mean(x * x, axis=-1, keepdims=True) + EPS) * g


def _const_spec(shape):
    nd = len(shape)
    return pl.BlockSpec(shape, lambda *_: (0,) * nd)


def _mod_spec(m3, seq_of):
    if m3.shape[0] == 1:
        return _const_spec((None, 6, D_MODEL))
    return pl.BlockSpec((None, 6, D_MODEL), lambda *idx: (seq_of(*idx), 0, 0))


def _ada_kernel(s_ref, w_ref, b_ref, o_ref):
    s = s_ref[...]
    s = s * jax.nn.sigmoid(s)
    o_ref[...] = _dot(s.astype(bf16), w_ref[...].astype(bf16)) + b_ref[...]


def _ada(mod, w_ada, b_ada):
    rows = mod.shape[0]
    n_out = w_ada.shape[1]
    tn = D_MODEL
    return pl.pallas_call(
        _ada_kernel,
        grid=(n_out // tn,),
        in_specs=[
            _const_spec((rows, D_MODEL)),
            pl.BlockSpec((D_MODEL, tn), lambda j: (0, j)),
            pl.BlockSpec((1, tn), lambda j: (0, j)),
        ],
        out_specs=pl.BlockSpec((rows, tn), lambda j: (0, j)),
        out_shape=jax.ShapeDtypeStruct((rows, n_out), f32),
        compiler_params=_params("arbitrary"),
        name="ada",
    )(mod, w_ada, b_ada.reshape(1, n_out))


def _wprep_kernel(wt_ref, place_ref, o_ref):
    def block(r0):
        return wt_ref[r0:r0 + LANES, :].T.astype(bf16)

    for j in range(C_KR // LANES):
        o_ref[:, j * LANES:(j + 1) * LANES] = block(j * LANES)
    placed = _dot(block(C_KR), place_ref[...])
    o_ref[:, C_KR:C_GATE] = placed[:, 0:LANES].astype(bf16)
    o_ref[:, C_KRP:C_KRP + LANES] = placed[:, LANES:2 * LANES].astype(bf16)
    for j in range(2 * D_MODEL // LANES):
        o_ref[:, C_GATE + j * LANES:C_GATE + (j + 1) * LANES] = block(C_KR + QK_ROPE + j * LANES)


def _wprep(w_in_t, place, *, tr=LANES):
    cols, rows = w_in_t.shape
    return pl.pallas_call(
        _wprep_kernel,
        grid=(rows // tr,),
        in_specs=[pl.BlockSpec((cols, tr), lambda i: (0, i)), _const_spec(place.shape)],
        out_specs=pl.BlockSpec((tr, C_KRP + LANES), lambda i: (i, 0)),
        out_shape=jax.ShapeDtypeStruct((rows, C_KRP + LANES), bf16),
        compiler_params=_params("parallel"),
        name="wprep",
    )(w_in_t, place)


def _conv_taps(vpad, r0, shift_ref, dw_ref, dwb_ref, ybuf, y0, ct):
    pad = CONV_WIDTH // 2
    sub = 8
    span = ((CONV_HALO - pad + CONV_WIDTH - 1) // sub) * sub
    cw = 2 * LANES
    for cb in range(CONV_DIM // cw):
        sl = slice(cb * cw, (cb + 1) * cw)
        win = vpad[pl.ds(r0, ct + 2 * CONV_HALO), sl]
        acc = jnp.zeros((ct // sub, sub, cw), f32)
        for ph in range(sub):
            wph = _dot(shift_ref[ph], win)
            for a in range(span // sub + 1):
                k = a * sub + ph - (CONV_HALO - pad)
                if 0 <= k < CONV_WIDTH:
                    acc = acc + wph[a * sub:a * sub + ct, :].reshape(ct // sub, sub, cw) * dw_ref[k, :, sl][None]
        ybuf[pl.ds(y0, ct), sl] = acc.reshape(ct, cw) + dwb_ref[:, sl]


def _conv_out(y, lng_ref, lnb_ref, wco_ref, gate):
    mu = jnp.mean(y, axis=-1, keepdims=True)
    yc = y - mu
    var = jnp.mean(yc * yc, axis=-1, keepdims=True)
    z = yc * lax.rsqrt(var + EPS) * lng_ref[...] + lnb_ref[...]
    z = z * jax.nn.sigmoid(z)
    return (gate.astype(f32) * _dot(z.astype(bf16), wco_ref[...])).astype(bf16)


def _fill_padded(vpad, v, n):
    zeros = jnp.zeros((CONV_HALO, CONV_DIM), vpad.dtype)
    vpad[0:CONV_HALO, :] = zeros
    vpad[CONV_HALO + n:2 * CONV_HALO + n, :] = zeros
    vpad[CONV_HALO:CONV_HALO + n, :] = v


def _inproj_kernel(*refs, rope, fuse_conv, q_scale, n, rt, ct):
    refs = list(refs)
    x_ref, m_ref, n1_ref, win_ref, qn_ref, wqb_ref = refs[:6]
    del refs[:6]
    if rope:
        wqbp_ref = refs.pop(0)
    kvn_ref = refs.pop(0)
    if rope:
        cos_ref, sin_ref = refs[:2]
        del refs[:2]
    if fuse_conv:
        shift_ref, dw_ref, dwb_ref, lng_ref, lnb_ref, wco_ref = refs[:6]
        del refs[:6]
    vc_ref, q_ref, ckv_ref, kr_ref, g_ref = refs[:5]
    del refs[:5]
    if not rope:
        kpe_ref = refs.pop(0)
    if fuse_conv:
        vpad, ybuf, gc = refs
    tm = x_ref.shape[0]

    sh1 = m_ref[0:1, :]
    sc1 = m_ref[1:2, :]
    h = _rms(x_ref[...], n1_ref[...]) * (1.0 + sc1) + sh1
    hb = h.astype(bf16)

    a = _dot(hb, win_ref[:, C_CONV:C_CONV + CONV_DIM])
    b = _dot(hb, win_ref[:, C_CONV + CONV_DIM:C_QA])
    v = (a * jax.nn.sigmoid(b)).astype(bf16)
    if fuse_conv:
        for s in range(tm // n):
            _fill_padded(vpad.at[s], v[s * n:(s + 1) * n, :], n)
    else:
        vc_ref[...] = v

    chunks = []
    if fuse_conv:
        chunks = [(s, c) for s in range(tm // n) for c in range(n // ct)]

    def conv_some(count):
        for _ in range(min(count, len(chunks))):
            s, c = chunks.pop(0)
            _conv_taps(vpad.at[s], c * ct, shift_ref, dw_ref, dwb_ref, ybuf, s * n + c * ct, ct)

    steps = 2 + 2 * D_MODEL // 512
    per_step = -(-len(chunks) // steps)

    qa = _dot(hb, win_ref[:, C_QA:C_KVA])
    qn = _rms(qa, qn_ref[...]).astype(bf16)
    q = _dot(qn, wqb_ref[...])
    if rope:
        qp = _dot(qn, wqbp_ref[...])
        cos = cos_ref[...]
        sin = sin_ref[...]
        for hd in range(N_HEADS):
            sl = slice(hd * HEAD_W, (hd + 1) * HEAD_W)
            q_ref[:, sl] = ((q[:, sl] * cos + qp[:, sl] * sin) * q_scale).astype(q_ref.dtype)
    else:
        q_ref[...] = (q * q_scale).astype(q_ref.dtype)
    conv_some(per_step)

    kva = _dot(hb, win_ref[:, C_KVA:C_KR])
    ckv_ref[...] = _rms(kva, kvn_ref[...]).astype(ckv_ref.dtype)

    kr = _dot(hb, win_ref[:, C_KR:C_GATE])
    if rope:
        krp = _dot(hb, win_ref[:, C_KRP:C_KRP + LANES])
        kr = kr * cos_ref[...] + krp * sin_ref[...]
    else:
        kpe_ref[...] = kr[:, ROPE_OFF:ROPE_OFF + QK_ROPE]
    kr_ref[...] = kr.astype(kr_ref.dtype)
    conv_some(per_step)

    gw = 512
    for j in range(2 * D_MODEL // gw):
        gate = jax.nn.sigmoid(_dot(hb, win_ref[:, C_GATE + j * gw:C_GATE + (j + 1) * gw])).astype(bf16)
        if not fuse_conv:
            g_ref[:, j * gw:(j + 1) * gw] = gate
        elif j * gw < D_MODEL:
            gc[:, j * gw:(j + 1) * gw] = gate
        else:
            g_ref[:, j * gw - D_MODEL:(j + 1) * gw - D_MODEL] = gate
        conv_some(per_step)

    if fuse_conv:
        conv_some(len(chunks))
        for r0 in range(0, tm, rt):
            vc_ref[r0:r0 + rt, :] = _conv_out(ybuf[r0:r0 + rt, :], lng_ref, lnb_ref, wco_ref, gc[r0:r0 + rt, :])


def _inproj(x2d, m3, norm1, win, q_norm, wqb, wqbp, kv_norm, cos, sin, conv_w, *, n, rope, tm, rt=256, ct=64):
    tokens = x2d.shape[0]
    fuse_conv = conv_w is not None
    assert n % tm == 0 or (tm % n == 0 and m3.shape[0] == 1 and not rope)
    assert not fuse_conv or tm % n == 0
    tiles_per_seq = max(1, n // tm)
    q_scale = float((QK_NOPE + QK_ROPE) ** -0.5 * LOG2E)
    tile = lambda w: pl.BlockSpec((tm, w), lambda i: (i, 0))
    in_specs = [
        tile(D_MODEL),
        _mod_spec(m3, lambda i: i // tiles_per_seq),
        _const_spec((1, D_MODEL)),
        _const_spec((D_MODEL, C_KRP + LANES if rope else C_END)),
        _const_spec((1, Q_LORA)),
        _const_spec(wqb.shape),
    ]
    args = [x2d, m3, norm1, win, q_norm, wqb]
    if rope:
        in_specs.append(_const_spec(wqbp.shape))
        args.append(wqbp)
    in_specs.append(_const_spec((1, KV_LORA)))
    args.append(kv_norm)
    if rope:
        tab = pl.BlockSpec((tm, LANES), lambda i: (i % tiles_per_seq, 0))
        in_specs += [tab, tab]
        args += [cos, sin]
    scratch = []
    if fuse_conv:
        args += list(conv_w)
        in_specs += [_const_spec(w.shape) for w in conv_w]
        scratch = [
            pltpu.VMEM((tm // n, n + 2 * CONV_HALO, CONV_DIM), bf16),
            pltpu.VMEM((tm, CONV_DIM), f32),
            pltpu.VMEM((tm, D_MODEL), bf16),
        ]
    gate_w = D_MODEL if fuse_conv else 2 * D_MODEL
    vc_w = D_MODEL if fuse_conv else CONV_DIM
    out_shape = [
        jax.ShapeDtypeStruct((tokens, vc_w), bf16),
        jax.ShapeDtypeStruct((tokens, N_HEADS * HEAD_W), bf16),
        jax.ShapeDtypeStruct((tokens, KV_LORA), bf16 if rope else f32),
        jax.ShapeDtypeStruct((tokens, LANES), bf16),
        jax.ShapeDtypeStruct((tokens, gate_w), bf16),
    ]
    out_specs = [tile(vc_w), tile(N_HEADS * HEAD_W), tile(KV_LORA), tile(LANES), tile(gate_w)]
    if not rope:
        out_shape.append(jax.ShapeDtypeStruct((tokens, QK_ROPE), f32))
        out_specs.append(tile(QK_ROPE))
    return pl.pallas_call(
        functools.partial(_inproj_kernel, rope=rope, fuse_conv=fuse_conv, q_scale=q_scale, n=n, rt=rt, ct=ct),
        grid=(tokens // tm,),
        in_specs=in_specs,
        out_specs=out_specs,
        out_shape=out_shape,
        scratch_shapes=scratch,
        compiler_params=_params("parallel"),
        name="inproj_rope" if rope else "inproj_conv",
    )(*args)


def _conv_kernel(v_ref, g_ref, shift_ref, dw_ref, dwb_ref, lng_ref, lnb_ref, wco_ref, o_ref, vpad, ybuf,
                 *, n, rt, ct):
    _fill_padded(vpad, v_ref[...], n)

    def conv_chunk(c, carry):
        r0 = pl.multiple_of(c * ct, ct)
        _conv_taps(vpad, r0, shift_ref, dw_ref, dwb_ref, ybuf, r0, ct)
        return carry

    lax.fori_loop(0, n // ct, conv_chunk, 0, unroll=4)

    def chunk(c, carry):
        r0 = pl.multiple_of(c * rt, rt)
        o_ref[pl.ds(r0, rt), :] = _conv_out(ybuf[pl.ds(r0, rt), :], lng_ref, lnb_ref, wco_ref, g_ref[pl.ds(r0, rt), :])
        return carry

    lax.fori_loop(0, n // rt, chunk, 0)


def _conv_weights(dw, dwb, lng, lnb, wco, ct):
    rows = ct + 2 * CONV_HALO
    i = np.arange(rows)
    shifts = jnp.asarray(np.stack([(i[None, :] == i[:, None] + ph) for ph in range(8)]), bf16)
    dw_tiles = jnp.broadcast_to(dw[:, None, :], (CONV_WIDTH, 8, CONV_DIM))
    return shifts, dw_tiles, dwb, lng, lnb, wco


def _conv(v3, g3, conv_w, *, rt=256, ct=64):
    nseq, n, _ = v3.shape
    assert v3.dtype == bf16, "the one-hot row shifts are exact only for bf16 windows"
    return pl.pallas_call(
        functools.partial(_conv_kernel, n=n, rt=rt, ct=ct),
        grid=(nseq,),
        in_specs=[
            pl.BlockSpec((None, n, CONV_DIM), lambda s: (s, 0, 0)),
            pl.BlockSpec((None, n, D_MODEL), lambda s: (s, 0, 0)),
        ] + [_const_spec(w.shape) for w in conv_w],
        out_specs=pl.BlockSpec((None, n, D_MODEL), lambda s: (s, 0, 0)),
        out_shape=jax.ShapeDtypeStruct((nseq, n, D_MODEL), bf16),
        scratch_shapes=[
            pltpu.VMEM((n + 2 * CONV_HALO, CONV_DIM), bf16),
            pltpu.VMEM((n, CONV_DIM), f32),
        ],
        compiler_params=_params("parallel"),
        name="conv",
    )(v3, g3, *conv_w)


def _attn_kernel(q_ref, ckv_ref, kpe_ref, cm_ref, g_ref, wk_ref, wv_ref, wo_ref, o_ref, k_scr, v_scr):
    @pl.when(pl.program_id(1) == 0)
    def _():
        ckv = ckv_ref[...].astype(bf16)
        kpe = kpe_ref[...].astype(f32)
        k = _dot(ckv, wk_ref[...])
        v = _dot(ckv, wv_ref[...])
        ones = jnp.ones((v.shape[0], LANES), bf16)
        for hd in range(N_HEADS):
            k_scr[hd] = (k[:, hd * HEAD_W:(hd + 1) * HEAD_W] + kpe).astype(bf16)
        for j in range(N_HEADS // 2):
            v_scr[j, :, 0:LANES] = v[:, j * LANES:(j + 1) * LANES].astype(bf16)
            v_scr[j, :, LANES:2 * LANES] = ones

    tq = q_ref.shape[0]
    lane = lax.broadcasted_iota(jnp.int32, (tq, LANES), 1)
    pairs = []
    for j in range(N_HEADS // 2):
        outs = []
        for hh in range(2):
            hd = 2 * j + hh
            qh = q_ref[:, hd * HEAD_W:(hd + 1) * HEAD_W]
            s = lax.dot_general(qh, k_scr[hd], (((1,), (1,)), ((), ())), preferred_element_type=f32)
            mx = jnp.max(s, axis=-1, keepdims=True)
            p = jnp.exp2(s - mx).astype(bf16)
            r = _dot(p, v_scr[j])
            outs.append(r[:, 0:LANES] * (1.0 / r[:, LANES:LANES + 1]))
        pairs.append(jnp.where(lane < V_DIM, outs[0], outs[1]))
    attn = jnp.concatenate(pairs, axis=-1).astype(bf16)
    ao = _dot(attn, wo_ref[...])
    o_ref[...] = (cm_ref[...].astype(f32) + g_ref[...].astype(f32) * ao).astype(o_ref.dtype)


def _attn(q2d, ckv3, kpe3, cm2d, g2d, wk, wv, wo, *, n, tq):
    nseq, s_len, _ = ckv3.shape
    gate_col = g2d.shape[1] // D_MODEL - 1
    qb = n // tq
    tile = lambda w, c=0: pl.BlockSpec((tq, w), lambda s, i: (s * qb + i, c))
    return pl.pallas_call(
        _attn_kernel,
        grid=(nseq, qb),
        in_specs=[
            tile(N_HEADS * HEAD_W),
            pl.BlockSpec((None, s_len, KV_LORA), lambda s, i: (s, 0, 0)),
            pl.BlockSpec((None, s_len, LANES), lambda s, i: (s, 0, 0)),
            tile(D_MODEL),
            tile(D_MODEL, gate_col),
            _const_spec(wk.shape),
            _const_spec(wv.shape),
            _const_spec(wo.shape),
        ],
        out_specs=tile(D_MODEL),
        out_shape=jax.ShapeDtypeStruct((nseq * n, D_MODEL), bf16),
        scratch_shapes=[
            pltpu.VMEM((N_HEADS, s_len, HEAD_W), bf16),
            pltpu.VMEM((N_HEADS // 2, s_len, 2 * LANES), bf16),
        ],
        compiler_params=_params("parallel", "arbitrary"),
        name="attn",
    )(q2d, ckv3, kpe3, cm2d, g2d, wk, wv, wo)


def _outproj_kernel(mg_ref, x_ref, m_ref, wout_ref, n2_ref, wr_ref, x1_ref, h2_ref, lg_ref):
    g1 = m_ref[2:3, :]
    sh2 = m_ref[3:4, :]
    sc2 = m_ref[4:5, :]
    x1 = x_ref[...] + g1 * _dot(mg_ref[...], wout_ref[...])
    x1_ref[...] = x1
    h2 = _rms(x1, n2_ref[...]) * (1.0 + sc2) + sh2
    hi = h2.astype(bf16)
    h2_ref[...] = hi
    lo = (h2 - hi.astype(f32)).astype(bf16)
    s = _dot(hi, wr_ref[...]) + _dot(lo, wr_ref[...])
    lg_ref[...] = s + pltpu.roll(s, LANES - N_EXPERTS, axis=1)


def _outproj(mg2d, x2d, m3, wout, norm2, wr, *, n, tm=256):
    tokens = x2d.shape[0]
    tiles_per_seq = n // tm
    tile = lambda w: pl.BlockSpec((tm, w), lambda i: (i, 0))
    return pl.pallas_call(
        _outproj_kernel,
        grid=(tokens // tm,),
        in_specs=[
            tile(D_MODEL),
            tile(D_MODEL),
            _mod_spec(m3, lambda i: i // tiles_per_seq),
            _const_spec(wout.shape),
            _const_spec((1, D_MODEL)),
            _const_spec(wr.shape),
        ],
        out_specs=(tile(D_MODEL), tile(D_MODEL), tile(LANES)),
        out_shape=(
            jax.ShapeDtypeStruct((tokens, D_MODEL), f32),
            jax.ShapeDtypeStruct((tokens, D_MODEL), bf16),
            jax.ShapeDtypeStruct((tokens, LANES), f32),
        ),
        compiler_params=_params("parallel"),
        name="outproj",
    )(mg2d, x2d, m3, wout, norm2, wr)


def _route_kernel(lg_ref, pos_ref, aff_ref, *, nseq, n, cap):
    for s in range(nseq):
        lt = lg_ref[s].T[0:N_EXPERTS, :]
        e = jnp.exp(lt - jnp.max(lt, axis=0, keepdims=True))
        aff_ref[s * N_EXPERTS:(s + 1) * N_EXPERTS, :] = e / jnp.sum(e, axis=0, keepdims=True)
    rows = nseq * N_EXPERTS
    capf = float(cap)

    def bit_step(i, t):
        cand = t | (jnp.int32(1) << (30 - i))
        thr = lax.bitcast_convert_type(cand, f32)
        cnt = jnp.sum(jnp.where(aff_ref[...] >= thr, 1.0, 0.0), axis=1, keepdims=True)
        return jnp.where(cnt >= capf, cand, t)

    t = lax.fori_loop(0, 31, bit_step, jnp.zeros((rows, 1), jnp.int32))
    thr = lax.bitcast_convert_type(t, f32)
    need = capf - jnp.sum(jnp.where(aff_ref[...] > thr, 1.0, 0.0), axis=1, keepdims=True)

    blk = 2 * LANES
    tri = jnp.where(
        lax.broadcasted_iota(jnp.int32, (blk, blk), 0) < lax.broadcasted_iota(jnp.int32, (blk, blk), 1),
        1.0, 0.0).astype(bf16)
    carry_gt = jnp.zeros((rows, 1), f32)
    carry_eq = jnp.zeros((rows, 1), f32)
    for b in range(n // blk):
        sl = slice(b * blk, (b + 1) * blk)
        ab = aff_ref[:, sl]
        gt = ab > thr
        eq = ab == thr
        gtb = jnp.where(gt, 1.0, 0.0)
        eqb = jnp.where(eq, 1.0, 0.0)
        pre_gt = _dot(gtb.astype(bf16), tri) + carry_gt
        pre_eq = _dot(eqb.astype(bf16), tri) + carry_eq
        carry_gt = carry_gt + jnp.sum(gtb, axis=1, keepdims=True)
        carry_eq = carry_eq + jnp.sum(eqb, axis=1, keepdims=True)
        sel = gt | (eq & (pre_eq < need))
        slot = pre_gt + jnp.minimum(pre_eq, need)
        pos_ref[:, sl] = jnp.where(sel, slot, -1.0).astype(jnp.int32)


def _route(lg3, *, cap):
    nseq, n, _ = lg3.shape
    rows = nseq * N_EXPERTS
    return pl.pallas_call(
        functools.partial(_route_kernel, nseq=nseq, n=n, cap=cap),
        grid=(1,),
        in_specs=[_const_spec(lg3.shape)],
        out_specs=(_const_spec((rows, n)), _const_spec((rows, n))),
        out_shape=(jax.ShapeDtypeStruct((rows, n), jnp.int32), jax.ShapeDtypeStruct((rows, n), f32)),
        compiler_params=_params("arbitrary"),
        name="route",
    )(lg3)


def _slot_hits(pos_ref, e, cap):
    width = pos_ref.shape[1]
    return lax.broadcasted_iota(jnp.int32, (cap, width), 0) == pos_ref[e:e + 1, :]


def _one_hot(hits):
    return jnp.concatenate([jnp.where(h, 1.0, 0.0).astype(bf16) for h in hits], axis=0)


def _gather_kernel(pos_ref, aff_ref, h2_ref, xg_ref, vals_ref, *, cap, ne, sb):
    for g in range(sb):
        rows = slice(g * cap, (g + 1) * cap)
        for e0 in range(0, N_EXPERTS, ne):
            hits = [_slot_hits(pos_ref.at[g], e0 + e, cap) for e in range(ne)]
            xg = _dot(_one_hot(hits), h2_ref[g]).astype(xg_ref.dtype)
            for e in range(ne):
                xg_ref[e0 + e, rows, :] = xg[e * cap:(e + 1) * cap, :]
                vals = jnp.sum(jnp.where(hits[e], aff_ref[g, e0 + e:e0 + e + 1, :], 0.0), axis=1, keepdims=True)
                vals_ref[e0 + e, rows, :] = jnp.broadcast_to(vals, (cap, LANES))


def _gather(pos3, aff3, h23, *, cap, ne, sb):
    nseq, _, n = pos3.shape
    return pl.pallas_call(
        functools.partial(_gather_kernel, cap=cap, ne=ne, sb=sb),
        grid=(nseq // sb,),
        in_specs=[
            pl.BlockSpec((sb, N_EXPERTS, n), lambda s: (s, 0, 0)),
            pl.BlockSpec((sb, N_EXPERTS, n), lambda s: (s, 0, 0)),
            pl.BlockSpec((sb, n, D_MODEL), lambda s: (s, 0, 0)),
        ],
        out_specs=(
            pl.BlockSpec((N_EXPERTS, sb * cap, D_MODEL), lambda s: (0, s, 0)),
            pl.BlockSpec((N_EXPERTS, sb * cap, LANES), lambda s: (0, s, 0)),
        ),
        out_shape=(
            jax.ShapeDtypeStruct((N_EXPERTS, nseq * cap, D_MODEL), bf16),
            jax.ShapeDtypeStruct((N_EXPERTS, nseq * cap, LANES), f32),
        ),
        compiler_params=_params("parallel"),
        name="gather",
    )(pos3, aff3, h23)


def _experts_kernel(xp_ref, vp_ref, xs_ref, vs_ref, wg_ref, wu_ref, wd_ref, yp_ref, ys_ref, *, rc):
    wg = wg_ref[...].astype(bf16)
    wu = wu_ref[...].astype(bf16)
    wd = wd_ref[...].astype(bf16)
    for x_ref, v_ref, y_ref in ((xp_ref, vp_ref, yp_ref), (xs_ref, vs_ref, ys_ref)):
        for r0 in range(0, x_ref.shape[0], rc):
            x = x_ref[r0:r0 + rc, :]
            a = _dot(x, wg)
            u = _dot(x, wu)
            hm = (a * jax.nn.sigmoid(a) * u).astype(bf16)
            y = _dot(hm, wd) * v_ref[r0:r0 + rc, 0:1]
            y_ref[r0:r0 + rc, :] = y.astype(y_ref.dtype)


def _experts(xg_p, vals_p, xg_s, vals_s, wg, wu, wd, *, rc=512):
    rp = xg_p.shape[1]
    rs = xg_s.shape[1]
    per_e = lambda r, w: pl.BlockSpec((None, r, w), lambda e: (e, 0, 0))
    return pl.pallas_call(
        functools.partial(_experts_kernel, rc=rc),
        grid=(N_EXPERTS,),
        in_specs=[
            per_e(rp, D_MODEL), per_e(rp, LANES), per_e(rs, D_MODEL), per_e(rs, LANES),
            per_e(D_MODEL, EXPERT_FF), per_e(D_MODEL, EXPERT_FF), per_e(EXPERT_FF, D_MODEL),
        ],
        out_specs=(per_e(rp, D_MODEL), per_e(rs, D_MODEL)),
        out_shape=(
            jax.ShapeDtypeStruct((N_EXPERTS, rp, D_MODEL), bf16),
            jax.ShapeDtypeStruct((N_EXPERTS, rs, D_MODEL), bf16),
        ),
        compiler_params=_params("parallel"),
        name="experts",
    )(xg_p, vals_p, xg_s, vals_s, wg, wu, wd)


def _scatter_kernel(pos_ref, y_ref, x1_ref, m_ref, fn_ref, o_ref, *, tn, cap, ne, sb):
    g2 = m_ref[5:6, :]
    for g in range(sb):
        rows = slice(g * cap, (g + 1) * cap)
        moe = jnp.zeros((tn, D_MODEL), f32)
        for e0 in range(0, N_EXPERTS, ne):
            onehot = _one_hot([_slot_hits(pos_ref.at[g], e0 + e, cap) for e in range(ne)])
            y = jnp.concatenate([y_ref[e0 + e, rows, :] for e in range(ne)], axis=0)
            moe = moe + lax.dot_general(onehot, y, (((0,), (0,)), ((), ())), preferred_element_type=f32)
        xo = x1_ref[g] + g2 * moe
        o_ref[g] = _rms(xo, fn_ref[...])


def _scatter(pos3, y, x13, m3, fn, *, cap, ne, tn, sb):
    nseq, _, n = pos3.shape
    assert sb == 1 or m3.shape[0] == 1, "sequences sharing a grid step must share their modulation rows"
    return pl.pallas_call(
        functools.partial(_scatter_kernel, tn=tn, cap=cap, ne=ne, sb=sb),
        grid=(nseq // sb, n // tn),
        in_specs=[
            pl.BlockSpec((sb, N_EXPERTS, tn), lambda s, i: (s, 0, i)),
            pl.BlockSpec((N_EXPERTS, sb * cap, D_MODEL), lambda s, i: (0, s, 0)),
            pl.BlockSpec((sb, tn, D_MODEL), lambda s, i: (s, i, 0)),
            _mod_spec(m3, lambda s, i: s),
            _const_spec((1, D_MODEL)),
        ],
        out_specs=pl.BlockSpec((sb, tn, D_MODEL), lambda s, i: (s, i, 0)),
        out_shape=jax.ShapeDtypeStruct((nseq, n, D_MODEL), f32),
        compiler_params=_params("parallel", "arbitrary"),
        name="scatter",
    )(pos3, y, x13, m3, fn)


def _rope_tables(n):
    t = np.arange(n)
    half = QK_ROPE // 2
    freqs = ROPE_BASE ** (-np.arange(0, half, 2, dtype=np.float64) / half)
    ang_r = (t // GRID_W)[:, None] * freqs
    ang_c = (t % GRID_W)[:, None] * freqs
    cr, sr, cc, sc = np.cos(ang_r), np.sin(ang_r), np.cos(ang_c), np.sin(ang_c)
    cos = np.ones((n, HEAD_W))
    sin = np.zeros((n, HEAD_W))
    cos[:, ROPE_OFF:ROPE_OFF + QK_ROPE] = np.concatenate([cr, cr, cc, cc], axis=-1)
    sin[:, ROPE_OFF:ROPE_OFF + QK_ROPE] = np.concatenate([-sr, sr, -sc, sc], axis=-1)
    return jnp.asarray(cos, f32), jnp.asarray(sin, f32)


_PARTNER = np.concatenate([np.arange(8, 16), np.arange(0, 8), np.arange(24, 32), np.arange(16, 24)])


def _rope_partner(w):
    q = QK_ROPE // 4
    return jnp.concatenate([w[..., q:2 * q], w[..., 0:q], w[..., 3 * q:4 * q], w[..., 2 * q:3 * q]], axis=-1)


def _rope_placement():
    place = np.zeros((LANES, 2 * LANES), np.float32)
    d = np.arange(QK_ROPE)
    place[d, ROPE_OFF + d] = 1.0
    place[_PARTNER, LANES + ROPE_OFF + d] = 1.0
    return jnp.asarray(place, bf16)


def _head_blocks(w_nope, w_rope):
    rows = w_nope.shape[0]
    if w_rope is None:
        w_rope = jnp.zeros((rows, N_HEADS, QK_ROPE), w_nope.dtype)
    z = jnp.zeros((rows, N_HEADS, HEAD_W - QK_NOPE - QK_ROPE), w_nope.dtype)
    return jnp.concatenate([w_nope, w_rope, z], axis=-1).reshape(rows, N_HEADS * HEAD_W)


def kernel(x_prompt, x_sample, cache_ckv, cache_kpe, c, c_ctx, w_ada, b_ada, norm1, w_in, conv_dw, conv_dw_b,
           conv_ln_g, conv_ln_b, w_conv_out, q_norm, w_qb, kv_norm, w_kvb, w_o_mla, w_out, norm2, w_router,
           w_e_gate, w_e_up, w_e_down, final_norm):
    assert w_ada.shape[0] == 1, "single trunk layer"
    nb_p, n_p, _ = x_prompt.shape
    nb_s, n_s, _ = x_sample.shape

    win = _wprep(w_in[0].T, _rope_placement())
    wq = w_qb[0].reshape(Q_LORA, N_HEADS, QK_NOPE + QK_ROPE)
    wqb = _head_blocks(wq[..., :QK_NOPE], wq[..., QK_NOPE:]).astype(bf16)
    wqbp = _head_blocks(jnp.zeros_like(wq[..., :QK_NOPE]), _rope_partner(wq[..., QK_NOPE:])).astype(bf16)
    wkv = w_kvb[0].reshape(KV_LORA, N_HEADS, QK_NOPE + V_DIM)
    wk = _head_blocks(wkv[..., :QK_NOPE], None).astype(bf16)
    wv = wkv[..., QK_NOPE:].reshape(KV_LORA, N_HEADS * V_DIM).astype(bf16)
    wco = w_conv_out[0].astype(bf16)
    wo = w_o_mla[0].astype(bf16)
    wout = w_out[0].astype(bf16)
    wr_hi = w_router[0].astype(bf16)
    wr_lo = (w_router[0] - wr_hi.astype(f32)).astype(bf16)
    wr = jnp.concatenate([wr_hi, wr_lo, jnp.zeros((D_MODEL, LANES - 2 * N_EXPERTS), bf16)], axis=-1)
    row = lambda a: a.reshape(1, -1)

    mod = jnp.concatenate([c_ctx[None, :], c, jnp.zeros((8 - 1 - nb_s, D_MODEL), f32)], axis=0)
    m = _ada(mod, w_ada[0], b_ada[0]).reshape(8, 6, D_MODEL)
    m_p, m_s = m[0:1], m[1:1 + nb_s]
    cos, sin = _rope_tables(n_s)

    conv_w = _conv_weights(conv_dw[0], row(conv_dw_b[0]), row(conv_ln_g[0]), row(conv_ln_b[0]), wco, CONV_CHUNK)

    def mixers(x, m3, rope, ctx_ckv, ctx_kpe):
        nseq, n, _ = x.shape
        x2d = x.reshape(nseq * n, D_MODEL)
        fuse_conv = IN_TILE % n == 0
        cm, q, ckv, kr, g, *kpe = _inproj(x2d, m3, row(norm1[0]), win, row(q_norm[0]), wqb, wqbp, row(kv_norm[0]),
                                          cos, sin, conv_w if fuse_conv else None, n=n, rope=rope, tm=IN_TILE,
                                          ct=CONV_CHUNK)
        if not fuse_conv:
            cm = _conv(cm.reshape(nseq, n, CONV_DIM), g.reshape(nseq, n, 2 * D_MODEL), conv_w, ct=CONV_CHUNK)
        keys_ckv = ckv.reshape(nseq, n, KV_LORA)
        keys_kpe = kr.reshape(nseq, n, LANES)
        if ctx_ckv is not None:
            keys_ckv = jnp.concatenate([ctx_ckv.astype(keys_ckv.dtype), keys_ckv], axis=1)
            keys_kpe = jnp.concatenate([ctx_kpe.astype(keys_kpe.dtype), keys_kpe], axis=1)
        mg = _attn(q, keys_ckv, keys_kpe, cm.reshape(nseq * n, D_MODEL), g, wk, wv, wo, n=n, tq=min(n, Q_TILE))
        x1, h2, lg = _outproj(mg, x2d, m3, wout, row(norm2[0]), wr, n=n)
        return x1, h2, lg, ckv, kpe

    ctx_kpe = jnp.pad(cache_kpe[:, 0], ((0, 0), (0, 0), (ROPE_OFF, LANES - ROPE_OFF - QK_ROPE)))
    x1_p, h2_p, lg_p, ckv_p, (kpe_p,) = mixers(x_prompt, m_p, False, None, None)
    x1_s, h2_s, lg_s, _, _ = mixers(x_sample, m_s, True, cache_ckv[:, 0], ctx_kpe)

    def moe_tiles(n):
        cap = EC_FACTOR * n // N_EXPERTS
        ne = N_EXPERTS if N_EXPERTS * cap <= MOE_ROWS else 1
        sb = max(1, MOE_TOKENS // n)
        return cap, ne, sb

    def route_gather(h2, lg, nseq, n):
        cap, ne, sb = moe_tiles(n)
        pos, aff = _route(lg.reshape(nseq, n, LANES), cap=cap)
        pos3 = pos.reshape(nseq, N_EXPERTS, n)
        xg, vals = _gather(pos3, aff.reshape(nseq, N_EXPERTS, n), h2.reshape(nseq, n, D_MODEL),
                           cap=cap, ne=ne, sb=sb)
        return pos3, xg, vals

    pos_p, xg_p, vals_p = route_gather(h2_p, lg_p, nb_p, n_p)
    pos_s, xg_s, vals_s = route_gather(h2_s, lg_s, nb_s, n_s)
    y_p, y_s = _experts(xg_p, vals_p, xg_s, vals_s, w_e_gate[0], w_e_up[0], w_e_down[0])
    fn = row(final_norm)

    def scatter(pos, y, x1, m3, nseq, n):
        cap, ne, sb = moe_tiles(n)
        return _scatter(pos, y, x1.reshape(nseq, n, D_MODEL), m3, fn, cap=cap, ne=ne, tn=min(n, SCATTER_TILE), sb=sb)

    y_prompt = scatter(pos_p, y_p, x1_p, m_p, nb_p, n_p)
    y_sample = scatter(pos_s, y_s, x1_s, m_s, nb_s, n_s)

    new_ckv = ckv_p.reshape(nb_p, 1, n_p, KV_LORA)
    new_kpe = kpe_p.reshape(nb_p, 1, n_p, QK_ROPE)
    return (y_prompt, y_sample, new_ckv, new_kpe)
```

```python
import functools

import jax
import jax.numpy as jnp
import numpy as np
from jax import lax
from jax.experimental import pallas as pl
from jax.experimental.pallas import tpu as pltpu

D_MODEL = 1024
GRID_W = 64
CONV_DIM = 512
CONV_WIDTH = 31
N_HEADS = 8
QK_NOPE = 64
QK_ROPE = 32
V_DIM = 64
Q_LORA = 256
KV_LORA = 128
N_EXPERTS = 16
EXPERT_FF = 512
EC_FACTOR = 2
ROPE_BASE = 10000.0
EPS = 1e-6

LANES = 128
HEAD_W = LANES
ROPE_OFF = QK_NOPE
CONV_HALO = 16
LOG2E = 1.4426950408889634
VMEM_LIMIT = 48 * 1024 * 1024
IN_TILE = 512
CONV_CHUNK = 64
Q_TILE = 512
SCORE_BYTES = 12 * 1024 * 1024
MOE_ROWS = 512
MOE_TOKENS = 1024
SCATTER_TILE = 512

C_CONV = 0
C_QA = 2 * CONV_DIM
C_KVA = C_QA + Q_LORA
C_KR = C_KVA + KV_LORA
C_GATE = C_KR + LANES
C_END = C_GATE + 2 * D_MODEL
C_KRP = C_END

f32 = jnp.float32
bf16 = jnp.bfloat16


def _params(*sem):
    return pltpu.CompilerParams(dimension_semantics=sem, vmem_limit_bytes=VMEM_LIMIT)


def _dot(a, b):
    return jnp.dot(a, b, preferred_element_type=f32)


def _rms(x, g):
    return x * lax.rsqrt(jnp.mean(x * x, axis=-1, keepdims=True) + EPS) * g


def _const_spec(shape):
    nd = len(shape)
    return pl.BlockSpec(shape, lambda *_: (0,) * nd)


def _mod_spec(m3, seq_of):
    if m3.shape[0] == 1:
        return _const_spec((None, 6, D_MODEL))
    return pl.BlockSpec((None, 6, D_MODEL), lambda *idx: (seq_of(*idx), 0, 0))


def _ada_kernel(s_ref, w_ref, b_ref, o_ref):
    s = s_ref[...]
    s = s * jax.nn.sigmoid(s)
    o_ref[...] = _dot(s.astype(bf16), w_ref[...].astype(bf16)) + b_ref[...]


def _ada(mod, w_ada, b_ada):
    rows = mod.shape[0]
    n_out = w_ada.shape[1]
    tn = D_MODEL
    return pl.pallas_call(
        _ada_kernel,
        grid=(n_out // tn,),
        in_specs=[
            _const_spec((rows, D_MODEL)),
            pl.BlockSpec((D_MODEL, tn), lambda j: (0, j)),
            pl.BlockSpec((1, tn), lambda j: (0, j)),
        ],
        out_specs=pl.BlockSpec((rows, tn), lambda j: (0, j)),
        out_shape=jax.ShapeDtypeStruct((rows, n_out), f32),
        compiler_params=_params("arbitrary"),
        name="ada",
    )(mod, w_ada, b_ada.reshape(1, n_out))


def _wprep_kernel(wt_ref, place_ref, o_ref):
    def block(r0):
        return wt_ref[r0:r0 + LANES, :].T.astype(bf16)

    for j in range(C_KR // LANES):
        o_ref[:, j * LANES:(j + 1) * LANES] = block(j * LANES)
    placed = _dot(block(C_KR), place_ref[...])
    o_ref[:, C_KR:C_GATE] = placed[:, 0:LANES].astype(bf16)
    o_ref[:, C_KRP:C_KRP + LANES] = placed[:, LANES:2 * LANES].astype(bf16)
    for j in range(2 * D_MODEL // LANES):
        o_ref[:, C_GATE + j * LANES:C_GATE + (j + 1) * LANES] = block(C_KR + QK_ROPE + j * LANES)


def _wprep(w_in_t, place, *, tr=LANES):
    cols, rows = w_in_t.shape
    return pl.pallas_call(
        _wprep_kernel,
        grid=(rows // tr,),
        in_specs=[pl.BlockSpec((cols, tr), lambda i: (0, i)), _const_spec(place.shape)],
        out_specs=pl.BlockSpec((tr, C_KRP + LANES), lambda i: (i, 0)),
        out_shape=jax.ShapeDtypeStruct((rows, C_KRP + LANES), bf16),
        compiler_params=_params("parallel"),
        name="wprep",
    )(w_in_t, place)


def _conv_taps(vpad, r0, shift_ref, dw_ref, dwb_ref, ybuf, y0, ct):
    pad = CONV_WIDTH // 2
    sub = 8
    span = ((CONV_HALO - pad + CONV_WIDTH - 1) // sub) * sub
    cw = 2 * LANES
    for cb in range(CONV_DIM // cw):
        sl = slice(cb * cw, (cb + 1) * cw)
        win = vpad[pl.ds(r0, ct + 2 * CONV_HALO), sl]
        acc = jnp.zeros((ct // sub, sub, cw), f32)
        for ph in range(sub):
            wph = _dot(shift_ref[ph], win)
            for a in range(span // sub + 1):
                k = a * sub + ph - (CONV_HALO - pad)
                if 0 <= k < CONV_WIDTH:
                    acc = acc + wph[a * sub:a * sub + ct, :].reshape(ct // sub, sub, cw) * dw_ref[k, :, sl][None]
        ybuf[pl.ds(y0, ct), sl] = acc.reshape(ct, cw) + dwb_ref[:, sl]


def _conv_out(y, lng_ref, lnb_ref, wco_ref, gate):
    mu = jnp.mean(y, axis=-1, keepdims=True)
    yc = y - mu
    var = jnp.mean(yc * yc, axis=-1, keepdims=True)
    z = yc * lax.rsqrt(var + EPS) * lng_ref[...] + lnb_ref[...]
    z = z * jax.nn.sigmoid(z)
    return (gate.astype(f32) * _dot(z.astype(bf16), wco_ref[...])).astype(bf16)


def _fill_padded(vpad, v, n):
    zeros = jnp.zeros((CONV_HALO, CONV_DIM), vpad.dtype)
    vpad[0:CONV_HALO, :] = zeros
    vpad[CONV_HALO + n:2 * CONV_HALO + n, :] = zeros
    vpad[CONV_HALO:CONV_HALO + n, :] = v


def _inproj_kernel(*refs, rope, fuse_conv, q_scale, n, rt, ct):
    refs = list(refs)
    x_ref, m_ref, n1_ref, win_ref, qn_ref, wqb_ref = refs[:6]
    del refs[:6]
    if rope:
        wqbp_ref = refs.pop(0)
    kvn_ref = refs.pop(0)
    if rope:
        cos_ref, sin_ref = refs[:2]
        del refs[:2]
    if fuse_conv:
        shift_ref, dw_ref, dwb_ref, lng_ref, lnb_ref, wco_ref = refs[:6]
        del refs[:6]
    vc_ref, q_ref, ckv_ref, kr_ref, g_ref = refs[:5]
    del refs[:5]
    if not rope:
        kpe_ref = refs.pop(0)
    if fuse_conv:
        vpad, ybuf, gc = refs
    tm = x_ref.shape[0]

    sh1 = m_ref[0:1, :]
    sc1 = m_ref[1:2, :]
    h = _rms(x_ref[...], n1_ref[...]) * (1.0 + sc1) + sh1
    hb = h.astype(bf16)

    a = _dot(hb, win_ref[:, C_CONV:C_CONV + CONV_DIM])
    b = _dot(hb, win_ref[:, C_CONV + CONV_DIM:C_QA])
    v = (a * jax.nn.sigmoid(b)).astype(bf16)
    if fuse_conv:
        for s in range(tm // n):
            _fill_padded(vpad.at[s], v[s * n:(s + 1) * n, :], n)
    else:
        vc_ref[...] = v

    chunks = []
    if fuse_conv:
        chunks = [(s, c) for s in range(tm // n) for c in range(n // ct)]

    def conv_some(count):
        for _ in range(min(count, len(chunks))):
            s, c = chunks.pop(0)
            _conv_taps(vpad.at[s], c * ct, shift_ref, dw_ref, dwb_ref, ybuf, s * n + c * ct, ct)

    steps = 2 + 2 * D_MODEL // 512
    per_step = -(-len(chunks) // steps)

    qa = _dot(hb, win_ref[:, C_QA:C_KVA])
    qn = _rms(qa, qn_ref[...]).astype(bf16)
    q = _dot(qn, wqb_ref[...])
    if rope:
        qp = _dot(qn, wqbp_ref[...])
        cos = cos_ref[...]
        sin = sin_ref[...]
        for hd in range(N_HEADS):
            sl = slice(hd * HEAD_W, (hd + 1) * HEAD_W)
            q_ref[:, sl] = ((q[:, sl] * cos + qp[:, sl] * sin) * q_scale).astype(q_ref.dtype)
    else:
        q_ref[...] = (q * q_scale).astype(q_ref.dtype)
    conv_some(per_step)

    kva = _dot(hb, win_ref[:, C_KVA:C_KR])
    ckv_ref[...] = _rms(kva, kvn_ref[...]).astype(ckv_ref.dtype)

    kr = _dot(hb, win_ref[:, C_KR:C_GATE])
    if rope:
        krp = _dot(hb, win_ref[:, C_KRP:C_KRP + LANES])
        kr = kr * cos_ref[...] + krp * sin_ref[...]
    else:
        kpe_ref[...] = kr[:, ROPE_OFF:ROPE_OFF + QK_ROPE]
    kr_ref[...] = kr.astype(kr_ref.dtype)
    conv_some(per_step)

    gw = 512
    for j in range(2 * D_MODEL // gw):
        gate = jax.nn.sigmoid(_dot(hb, win_ref[:, C_GATE + j * gw:C_GATE + (j + 1) * gw])).astype(bf16)
        if not fuse_conv:
            g_ref[:, j * gw:(j + 1) * gw] = gate
        elif j * gw < D_MODEL:
            gc[:, j * gw:(j + 1) * gw] = gate
        else:
            g_ref[:, j * gw - D_MODEL:(j + 1) * gw - D_MODEL] = gate
        conv_some(per_step)

    if fuse_conv:
        conv_some(len(chunks))
        for r0 in range(0, tm, rt):
            vc_ref[r0:r0 + rt, :] = _conv_out(ybuf[r0:r0 + rt, :], lng_ref, lnb_ref, wco_ref, gc[r0:r0 + rt, :])


def _inproj(x2d, m3, norm1, win, q_norm, wqb, wqbp, kv_norm, cos, sin, conv_w, *, n, rope, tm, rt=256, ct=64):
    tokens = x2d.shape[0]
    fuse_conv = conv_w is not None
    assert n % tm == 0 or (tm % n == 0 and m3.shape[0] == 1 and not rope)
    assert not fuse_conv or tm % n == 0
    tiles_per_seq = max(1, n // tm)
    q_scale = float((QK_NOPE + QK_ROPE) ** -0.5 * LOG2E)
    tile = lambda w: pl.BlockSpec((tm, w), lambda i: (i, 0))
    in_specs = [
        tile(D_MODEL),
        _mod_spec(m3, lambda i: i // tiles_per_seq),
        _const_spec((1, D_MODEL)),
        _const_spec((D_MODEL, C_KRP + LANES if rope else C_END)),
        _const_spec((1, Q_LORA)),
        _const_spec(wqb.shape),
    ]
    args = [x2d, m3, norm1, win, q_norm, wqb]
    if rope:
        in_specs.append(_const_spec(wqbp.shape))
        args.append(wqbp)
    in_specs.append(_const_spec((1, KV_LORA)))
    args.append(kv_norm)
    if rope:
        tab = pl.BlockSpec((tm, LANES), lambda i: (i % tiles_per_seq, 0))
        in_specs += [tab, tab]
        args += [cos, sin]
    scratch = []
    if fuse_conv:
        args += list(conv_w)
        in_specs += [_const_spec(w.shape) for w in conv_w]
        scratch = [
            pltpu.VMEM((tm // n, n + 2 * CONV_HALO, CONV_DIM), bf16),
            pltpu.VMEM((tm, CONV_DIM), f32),
            pltpu.VMEM((tm, D_MODEL), bf16),
        ]
    gate_w = D_MODEL if fuse_conv else 2 * D_MODEL
    vc_w = D_MODEL if fuse_conv else CONV_DIM
    out_shape = [
        jax.ShapeDtypeStruct((tokens, vc_w), bf16),
        jax.ShapeDtypeStruct((tokens, N_HEADS * HEAD_W), bf16),
        jax.ShapeDtypeStruct((tokens, KV_LORA), bf16 if rope else f32),
        jax.ShapeDtypeStruct((tokens, LANES), bf16),
        jax.ShapeDtypeStruct((tokens, gate_w), bf16),
    ]
    out_specs = [tile(vc_w), tile(N_HEADS * HEAD_W), tile(KV_LORA), tile(LANES), tile(gate_w)]
    if not rope:
        out_shape.append(jax.ShapeDtypeStruct((tokens, QK_ROPE), f32))
        out_specs.append(tile(QK_ROPE))
    return pl.pallas_call(
        functools.partial(_inproj_kernel, rope=rope, fuse_conv=fuse_conv, q_scale=q_scale, n=n, rt=rt, ct=ct),
        grid=(tokens // tm,),
        in_specs=in_specs,
        out_specs=out_specs,
        out_shape=out_shape,
        scratch_shapes=scratch,
        compiler_params=_params("parallel"),
        name="inproj_rope" if rope else "inproj_conv",
    )(*args)


def _conv_kernel(v_ref, g_ref, shift_ref, dw_ref, dwb_ref, lng_ref, lnb_ref, wco_ref, o_ref, vpad, ybuf,
                 *, n, rt, ct):
    _fill_padded(vpad, v_ref[...], n)

    def conv_chunk(c, carry):
        r0 = pl.multiple_of(c * ct, ct)
        _conv_taps(vpad, r0, shift_ref, dw_ref, dwb_ref, ybuf, r0, ct)
        return carry

    lax.fori_loop(0, n // ct, conv_chunk, 0, unroll=4)

    def chunk(c, carry):
        r0 = pl.multiple_of(c * rt, rt)
        o_ref[pl.ds(r0, rt), :] = _conv_out(ybuf[pl.ds(r0, rt), :], lng_ref, lnb_ref, wco_ref, g_ref[pl.ds(r0, rt), :])
        return carry

    lax.fori_loop(0, n // rt, chunk, 0)


def _conv_weights(dw, dwb, lng, lnb, wco, ct):
    rows = ct + 2 * CONV_HALO
    i = np.arange(rows)
    shifts = jnp.asarray(np.stack([(i[None, :] == i[:, None] + ph) for ph in range(8)]), bf16)
    dw_tiles = jnp.broadcast_to(dw[:, None, :], (CONV_WIDTH, 8, CONV_DIM))
    return shifts, dw_tiles, dwb, lng, lnb, wco


def _conv(v3, g3, conv_w, *, rt=256, ct=64):
    nseq, n, _ = v3.shape
    assert v3.dtype == bf16, "the one-hot row shifts are exact only for bf16 windows"
    return pl.pallas_call(
        functools.partial(_conv_kernel, n=n, rt=rt, ct=ct),
        grid=(nseq,),
        in_specs=[
            pl.BlockSpec((None, n, CONV_DIM), lambda s: (s, 0, 0)),
            pl.BlockSpec((None, n, D_MODEL), lambda s: (s, 0, 0)),
        ] + [_const_spec(w.shape) for w in conv_w],
        out_specs=pl.BlockSpec((None, n, D_MODEL), lambda s: (s, 0, 0)),
        out_shape=jax.ShapeDtypeStruct((nseq, n, D_MODEL), bf16),
        scratch_shapes=[
            pltpu.VMEM((n + 2 * CONV_HALO, CONV_DIM), bf16),
            pltpu.VMEM((n, CONV_DIM), f32),
        ],
        compiler_params=_params("parallel"),
        name="conv",
    )(v3, g3, *conv_w)


def _attn_kernel(q_ref, ckv_ref, kpe_ref, cm_ref, g_ref, wk_ref, wv_ref, wo_ref, o_ref, k_scr, v_scr, q_scr, *, hg):
    @pl.when(pl.program_id(1) == 0)
    def _():
        ckv = ckv_ref[...].astype(bf16)
        kpe = kpe_ref[...].astype(f32)
        k = _dot(ckv, wk_ref[...])
        v = _dot(ckv, wv_ref[...])
        ones = jnp.ones((v.shape[0], V_DIM), bf16)
        for hd in range(N_HEADS):
            k_scr[hd] = (k[:, hd * HEAD_W:(hd + 1) * HEAD_W] + kpe).astype(bf16)
            vh = v[:, hd * V_DIM:(hd + 1) * V_DIM].astype(bf16)
            if hd % 2 == 0:
                v_scr[hd, :, 0:V_DIM] = vh
                v_scr[hd, :, V_DIM:LANES] = ones
            else:
                v_scr[hd, :, 0:V_DIM] = ones
                v_scr[hd, :, V_DIM:LANES] = vh

    tq = q_ref.shape[0]
    for hd in range(N_HEADS):
        q_scr[hd] = q_ref[:, hd * HEAD_W:(hd + 1) * HEAD_W]
    lane = lax.broadcasted_iota(jnp.int32, (tq, LANES), 1)
    pairs = []
    for h0 in range(0, N_HEADS, hg):
        hs = slice(h0, h0 + hg)
        s = lax.dot_general(q_scr[hs], k_scr[hs], (((2,), (2,)), ((0,), (0,))), preferred_element_type=f32)
        mx = jnp.max(s, axis=-1, keepdims=True)
        p = jnp.exp2(s - mx).astype(bf16)
        r = lax.dot_general(p, v_scr[hs], (((2,), (1,)), ((0,), (0,))), preferred_element_type=f32)
        for j in range(hg // 2):
            re, ro = r[2 * j], r[2 * j + 1]
            oe = re * (1.0 / re[:, V_DIM:V_DIM + 1])
            oo = ro * (1.0 / ro[:, 0:1])
            pairs.append(jnp.where(lane < V_DIM, oe, oo))
    attn = jnp.concatenate(pairs, axis=-1).astype(bf16)
    ao = _dot(attn, wo_ref[...])
    o_ref[...] = (cm_ref[...].astype(f32) + g_ref[...].astype(f32) * ao).astype(o_ref.dtype)


def _attn(q2d, ckv3, kpe3, cm2d, g2d, wk, wv, wo, *, n, tq):
    nseq, s_len, _ = ckv3.shape
    hg = max(2, min(N_HEADS, SCORE_BYTES // (tq * s_len * 4)))
    assert N_HEADS % hg == 0 and hg % 2 == 0
    gate_col = g2d.shape[1] // D_MODEL - 1
    qb = n // tq
    tile = lambda w, c=0: pl.BlockSpec((tq, w), lambda s, i: (s * qb + i, c))
    return pl.pallas_call(
        functools.partial(_attn_kernel, hg=hg),
        grid=(nseq, qb),
        in_specs=[
            tile(N_HEADS * HEAD_W),
            pl.BlockSpec((None, s_len, KV_LORA), lambda s, i: (s, 0, 0)),
            pl.BlockSpec((None, s_len, LANES), lambda s, i: (s, 0, 0)),
            tile(D_MODEL),
            tile(D_MODEL, gate_col),
            _const_spec(wk.shape),
            _const_spec(wv.shape),
            _const_spec(wo.shape),
        ],
        out_specs=tile(D_MODEL),
        out_shape=jax.ShapeDtypeStruct((nseq * n, D_MODEL), bf16),
        scratch_shapes=[
            pltpu.VMEM((N_HEADS, s_len, HEAD_W), bf16),
            pltpu.VMEM((N_HEADS, s_len, LANES), bf16),
            pltpu.VMEM((N_HEADS, tq, HEAD_W), bf16),
        ],
        compiler_params=_params("parallel", "arbitrary"),
        name="attn",
    )(q2d, ckv3, kpe3, cm2d, g2d, wk, wv, wo)


def _outproj_kernel(mg_ref, x_ref, m_ref, wout_ref, n2_ref, wr_ref, x1_ref, h2_ref, lg_ref):
    g1 = m_ref[2:3, :]
    sh2 = m_ref[3:4, :]
    sc2 = m_ref[4:5, :]
    x1 = x_ref[...] + g1 * _dot(mg_ref[...], wout_ref[...])
    x1_ref[...] = x1
    h2 = _rms(x1, n2_ref[...]) * (1.0 + sc2) + sh2
    hi = h2.astype(bf16)
    h2_ref[...] = hi
    lo = (h2 - hi.astype(f32)).astype(bf16)
    s = _dot(hi, wr_ref[...]) + _dot(lo, wr_ref[...])
    lg_ref[...] = s + pltpu.roll(s, LANES - N_EXPERTS, axis=1)


def _outproj(mg2d, x2d, m3, wout, norm2, wr, *, n, tm=256):
    tokens = x2d.shape[0]
    tiles_per_seq = n // tm
    tile = lambda w: pl.BlockSpec((tm, w), lambda i: (i, 0))
    return pl.pallas_call(
        _outproj_kernel,
        grid=(tokens // tm,),
        in_specs=[
            tile(D_MODEL),
            tile(D_MODEL),
            _mod_spec(m3, lambda i: i // tiles_per_seq),
            _const_spec(wout.shape),
            _const_spec((1, D_MODEL)),
            _const_spec(wr.shape),
        ],
        out_specs=(tile(D_MODEL), tile(D_MODEL), tile(LANES)),
        out_shape=(
            jax.ShapeDtypeStruct((tokens, D_MODEL), f32),
            jax.ShapeDtypeStruct((tokens, D_MODEL), bf16),
            jax.ShapeDtypeStruct((tokens, LANES), f32),
        ),
        compiler_params=_params("parallel"),
        name="outproj",
    )(mg2d, x2d, m3, wout, norm2, wr)


def _route_kernel(lg_ref, pos_ref, aff_ref, *, nseq, n, cap):
    for s in range(nseq):
        lt = lg_ref[s].T[0:N_EXPERTS, :]
        e = jnp.exp(lt - jnp.max(lt, axis=0, keepdims=True))
        aff_ref[s * N_EXPERTS:(s + 1) * N_EXPERTS, :] = e / jnp.sum(e, axis=0, keepdims=True)
    rows = nseq * N_EXPERTS
    capf = float(cap)

    def bit_step(i, t):
        cand = t | (jnp.int32(1) << (30 - i))
        thr = lax.bitcast_convert_type(cand, f32)
        cnt = jnp.sum(jnp.where(aff_ref[...] >= thr, 1.0, 0.0), axis=1, keepdims=True)
        return jnp.where(cnt >= capf, cand, t)

    t = lax.fori_loop(0, 31, bit_step, jnp.zeros((rows, 1), jnp.int32))
    thr = lax.bitcast_convert_type(t, f32)
    need = capf - jnp.sum(jnp.where(aff_ref[...] > thr, 1.0, 0.0), axis=1, keepdims=True)

    blk = 2 * LANES
    tri = jnp.where(
        lax.broadcasted_iota(jnp.int32, (blk, blk), 0) < lax.broadcasted_iota(jnp.int32, (blk, blk), 1),
        1.0, 0.0).astype(bf16)
    carry_gt = jnp.zeros((rows, 1), f32)
    carry_eq = jnp.zeros((rows, 1), f32)
    for b in range(n // blk):
        sl = slice(b * blk, (b + 1) * blk)
        ab = aff_ref[:, sl]
        gt = ab > thr
        eq = ab == thr
        gtb = jnp.where(gt, 1.0, 0.0)
        eqb = jnp.where(eq, 1.0, 0.0)
        pre_gt = _dot(gtb.astype(bf16), tri) + carry_gt
        pre_eq = _dot(eqb.astype(bf16), tri) + carry_eq
        carry_gt = carry_gt + jnp.sum(gtb, axis=1, keepdims=True)
        carry_eq = carry_eq + jnp.sum(eqb, axis=1, keepdims=True)
        sel = gt | (eq & (pre_eq < need))
        slot = pre_gt + jnp.minimum(pre_eq, need)
        pos_ref[:, sl] = jnp.where(sel, slot, -1.0).astype(jnp.int32)


def _route(lg3, *, cap):
    nseq, n, _ = lg3.shape
    rows = nseq * N_EXPERTS
    return pl.pallas_call(
        functools.partial(_route_kernel, nseq=nseq, n=n, cap=cap),
        grid=(1,),
        in_specs=[_const_spec(lg3.shape)],
        out_specs=(_const_spec((rows, n)), _const_spec((rows, n))),
        out_shape=(jax.ShapeDtypeStruct((rows, n), jnp.int32), jax.ShapeDtypeStruct((rows, n), f32)),
        compiler_params=_params("arbitrary"),
        name="route",
    )(lg3)


def _slot_hits(pos_ref, e, cap):
    width = pos_ref.shape[1]
    return lax.broadcasted_iota(jnp.int32, (cap, width), 0) == pos_ref[e:e + 1, :]


def _one_hot(hits):
    return jnp.concatenate([jnp.where(h, 1.0, 0.0).astype(bf16) for h in hits], axis=0)


def _gather_kernel(pos_ref, aff_ref, h2_ref, xg_ref, vals_ref, *, cap, ne, sb):
    for g in range(sb):
        rows = slice(g * cap, (g + 1) * cap)
        for e0 in range(0, N_EXPERTS, ne):
            hits = [_slot_hits(pos_ref.at[g], e0 + e, cap) for e in range(ne)]
            xg = _dot(_one_hot(hits), h2_ref[g]).astype(xg_ref.dtype)
            for e in range(ne):
                xg_ref[e0 + e, rows, :] = xg[e * cap:(e + 1) * cap, :]
                vals = jnp.sum(jnp.where(hits[e], aff_ref[g, e0 + e:e0 + e + 1, :], 0.0), axis=1, keepdims=True)
                vals_ref[e0 + e, rows, :] = jnp.broadcast_to(vals, (cap, LANES))


def _gather(pos3, aff3, h23, *, cap, ne, sb):
    nseq, _, n = pos3.shape
    return pl.pallas_call(
        functools.partial(_gather_kernel, cap=cap, ne=ne, sb=sb),
        grid=(nseq // sb,),
        in_specs=[
            pl.BlockSpec((sb, N_EXPERTS, n), lambda s: (s, 0, 0)),
            pl.BlockSpec((sb, N_EXPERTS, n), lambda s: (s, 0, 0)),
            pl.BlockSpec((sb, n, D_MODEL), lambda s: (s, 0, 0)),
        ],
        out_specs=(
            pl.BlockSpec((N_EXPERTS, sb * cap, D_MODEL), lambda s: (0, s, 0)),
            pl.BlockSpec((N_EXPERTS, sb * cap, LANES), lambda s: (0, s, 0)),
        ),
        out_shape=(
            jax.ShapeDtypeStruct((N_EXPERTS, nseq * cap, D_MODEL), bf16),
            jax.ShapeDtypeStruct((N_EXPERTS, nseq * cap, LANES), f32),
        ),
        compiler_params=_params("parallel"),
        name="gather",
    )(pos3, aff3, h23)


def _experts_kernel(xp_ref, vp_ref, xs_ref, vs_ref, wg_ref, wu_ref, wd_ref, yp_ref, ys_ref, *, rc):
    wg = wg_ref[...].astype(bf16)
    wu = wu_ref[...].astype(bf16)
    wd = wd_ref[...].astype(bf16)
    for x_ref, v_ref, y_ref in ((xp_ref, vp_ref, yp_ref), (xs_ref, vs_ref, ys_ref)):
        for r0 in range(0, x_ref.shape[0], rc):
            x = x_ref[r0:r0 + rc, :]
            a = _dot(x, wg)
            u = _dot(x, wu)
            hm = (a * jax.nn.sigmoid(a) * u).astype(bf16)
            y = _dot(hm, wd) * v_ref[r0:r0 + rc, 0:1]
            y_ref[r0:r0 + rc, :] = y.astype(y_ref.dtype)


def _experts(xg_p, vals_p, xg_s, vals_s, wg, wu, wd, *, rc=512):
    rp = xg_p.shape[1]
    rs = xg_s.shape[1]
    per_e = lambda r, w: pl.BlockSpec((None, r, w), lambda e: (e, 0, 0))
    return pl.pallas_call(
        functools.partial(_experts_kernel, rc=rc),
        grid=(N_EXPERTS,),
        in_specs=[
            per_e(rp, D_MODEL), per_e(rp, LANES), per_e(rs, D_MODEL), per_e(rs, LANES),
            per_e(D_MODEL, EXPERT_FF), per_e(D_MODEL, EXPERT_FF), per_e(EXPERT_FF, D_MODEL),
        ],
        out_specs=(per_e(rp, D_MODEL), per_e(rs, D_MODEL)),
        out_shape=(
            jax.ShapeDtypeStruct((N_EXPERTS, rp, D_MODEL), bf16),
            jax.ShapeDtypeStruct((N_EXPERTS, rs, D_MODEL), bf16),
        ),
        compiler_params=_params("parallel"),
        name="experts",
    )(xg_p, vals_p, xg_s, vals_s, wg, wu, wd)


def _scatter_kernel(pos_ref, y_ref, x1_ref, m_ref, fn_ref, o_ref, *, tn, cap, ne, sb):
    g2 = m_ref[5:6, :]
    for g in range(sb):
        rows = slice(g * cap, (g + 1) * cap)
        moe = jnp.zeros((tn, D_MODEL), f32)
        for e0 in range(0, N_EXPERTS, ne):
            onehot = _one_hot([_slot_hits(pos_ref.at[g], e0 + e, cap) for e in range(ne)])
            y = jnp.concatenate([y_ref[e0 + e, rows, :] for e in range(ne)], axis=0)
            moe = moe + lax.dot_general(onehot, y, (((0,), (0,)), ((), ())), preferred_element_type=f32)
        xo = x1_ref[g] + g2 * moe
        o_ref[g] = _rms(xo, fn_ref[...])


def _scatter(pos3, y, x13, m3, fn, *, cap, ne, tn, sb):
    nseq, _, n = pos3.shape
    assert sb == 1 or m3.shape[0] == 1, "sequences sharing a grid step must share their modulation rows"
    return pl.pallas_call(
        functools.partial(_scatter_kernel, tn=tn, cap=cap, ne=ne, sb=sb),
        grid=(nseq // sb, n // tn),
        in_specs=[
            pl.BlockSpec((sb, N_EXPERTS, tn), lambda s, i: (s, 0, i)),
            pl.BlockSpec((N_EXPERTS, sb * cap, D_MODEL), lambda s, i: (0, s, 0)),
            pl.BlockSpec((sb, tn, D_MODEL), lambda s, i: (s, i, 0)),
            _mod_spec(m3, lambda s, i: s),
            _const_spec((1, D_MODEL)),
        ],
        out_specs=pl.BlockSpec((sb, tn, D_MODEL), lambda s, i: (s, i, 0)),
        out_shape=jax.ShapeDtypeStruct((nseq, n, D_MODEL), f32),
        compiler_params=_params("parallel", "arbitrary"),
        name="scatter",
    )(pos3, y, x13, m3, fn)


def _rope_tables(n):
    t = np.arange(n)
    half = QK_ROPE // 2
    freqs = ROPE_BASE ** (-np.arange(0, half, 2, dtype=np.float64) / half)
    ang_r = (t // GRID_W)[:, None] * freqs
    ang_c = (t % GRID_W)[:, None] * freqs
    cr, sr, cc, sc = np.cos(ang_r), np.sin(ang_r), np.cos(ang_c), np.sin(ang_c)
    cos = np.ones((n, HEAD_W))
    sin = np.zeros((n, HEAD_W))
    cos[:, ROPE_OFF:ROPE_OFF + QK_ROPE] = np.concatenate([cr, cr, cc, cc], axis=-1)
    sin[:, ROPE_OFF:ROPE_OFF + QK_ROPE] = np.concatenate([-sr, sr, -sc, sc], axis=-1)
    return jnp.asarray(cos, f32), jnp.asarray(sin, f32)


_PARTNER = np.concatenate([np.arange(8, 16), np.arange(0, 8), np.arange(24, 32), np.arange(16, 24)])


def _rope_partner(w):
    q = QK_ROPE // 4
    return jnp.concatenate([w[..., q:2 * q], w[..., 0:q], w[..., 3 * q:4 * q], w[..., 2 * q:3 * q]], axis=-1)


def _rope_placement():
    place = np.zeros((LANES, 2 * LANES), np.float32)
    d = np.arange(QK_ROPE)
    place[d, ROPE_OFF + d] = 1.0
    place[_PARTNER, LANES + ROPE_OFF + d] = 1.0
    return jnp.asarray(place, bf16)


def _head_blocks(w_nope, w_rope):
    rows = w_nope.shape[0]
    if w_rope is None:
        w_rope = jnp.zeros((rows, N_HEADS, QK_ROPE), w_nope.dtype)
    z = jnp.zeros((rows, N_HEADS, HEAD_W - QK_NOPE - QK_ROPE), w_nope.dtype)
    return jnp.concatenate([w_nope, w_rope, z], axis=-1).reshape(rows, N_HEADS * HEAD_W)


def kernel(x_prompt, x_sample, cache_ckv, cache_kpe, c, c_ctx, w_ada, b_ada, norm1, w_in, conv_dw, conv_dw_b,
           conv_ln_g, conv_ln_b, w_conv_out, q_norm, w_qb, kv_norm, w_kvb, w_o_mla, w_out, norm2, w_router,
           w_e_gate, w_e_up, w_e_down, final_norm):
    assert w_ada.shape[0] == 1, "single trunk layer"
    nb_p, n_p, _ = x_prompt.shape
    nb_s, n_s, _ = x_sample.shape

    win = _wprep(w_in[0].T, _rope_placement())
    wq = w_qb[0].reshape(Q_LORA, N_HEADS, QK_NOPE + QK_ROPE)
    wqb = _head_blocks(wq[..., :QK_NOPE], wq[..., QK_NOPE:]).astype(bf16)
    wqbp = _head_blocks(jnp.zeros_like(wq[..., :QK_NOPE]), _rope_partner(wq[..., QK_NOPE:])).astype(bf16)
    wkv = w_kvb[0].reshape(KV_LORA, N_HEADS, QK_NOPE + V_DIM)
    wk = _head_blocks(wkv[..., :QK_NOPE], None).astype(bf16)
    wv = wkv[..., QK_NOPE:].reshape(KV_LORA, N_HEADS * V_DIM).astype(bf16)
    wco = w_conv_out[0].astype(bf16)
    wo = w_o_mla[0].astype(bf16)
    wout = w_out[0].astype(bf16)
    wr_hi = w_router[0].astype(bf16)
    wr_lo = (w_router[0] - wr_hi.astype(f32)).astype(bf16)
    wr = jnp.concatenate([wr_hi, wr_lo, jnp.zeros((D_MODEL, LANES - 2 * N_EXPERTS), bf16)], axis=-1)
    row = lambda a: a.reshape(1, -1)

    mod = jnp.concatenate([c_ctx[None, :], c, jnp.zeros((8 - 1 - nb_s, D_MODEL), f32)], axis=0)
    m = _ada(mod, w_ada[0], b_ada[0]).reshape(8, 6, D_MODEL)
    m_p, m_s = m[0:1], m[1:1 + nb_s]
    cos, sin = _rope_tables(n_s)

    conv_w = _conv_weights(conv_dw[0], row(conv_dw_b[0]), row(conv_ln_g[0]), row(conv_ln_b[0]), wco, CONV_CHUNK)

    def mixers(x, m3, rope, ctx_ckv, ctx_kpe):
        nseq, n, _ = x.shape
        x2d = x.reshape(nseq * n, D_MODEL)
        fuse_conv = IN_TILE % n == 0
        cm, q, ckv, kr, g, *kpe = _inproj(x2d, m3, row(norm1[0]), win, row(q_norm[0]), wqb, wqbp, row(kv_norm[0]),
                                          cos, sin, conv_w if fuse_conv else None, n=n, rope=rope, tm=IN_TILE,
                                          ct=CONV_CHUNK)
        if not fuse_conv:
            cm = _conv(cm.reshape(nseq, n, CONV_DIM), g.reshape(nseq, n, 2 * D_MODEL), conv_w, ct=CONV_CHUNK)
        keys_ckv = ckv.reshape(nseq, n, KV_LORA)
        keys_kpe = kr.reshape(nseq, n, LANES)
        if ctx_ckv is not None:
            keys_ckv = jnp.concatenate([ctx_ckv.astype(keys_ckv.dtype), keys_ckv], axis=1)
            keys_kpe = jnp.concatenate([ctx_kpe.astype(keys_kpe.dtype), keys_kpe], axis=1)
        mg = _attn(q, keys_ckv, keys_kpe, cm.reshape(nseq * n, D_MODEL), g, wk, wv, wo, n=n, tq=min(n, Q_TILE))
        x1, h2, lg = _outproj(mg, x2d, m3, wout, row(norm2[0]), wr, n=n)
        return x1, h2, lg, ckv, kpe

    ctx_kpe = jnp.pad(cache_kpe[:, 0], ((0, 0), (0, 0), (ROPE_OFF, LANES - ROPE_OFF - QK_ROPE)))
    x1_p, h2_p, lg_p, ckv_p, (kpe_p,) = mixers(x_prompt, m_p, False, None, None)
    x1_s, h2_s, lg_s, _, _ = mixers(x_sample, m_s, True, cache_ckv[:, 0], ctx_kpe)

    def moe_tiles(n):
        cap = EC_FACTOR * n // N_EXPERTS
        ne = N_EXPERTS if N_EXPERTS * cap <= MOE_ROWS else 1
        sb = max(1, MOE_TOKENS // n)
        return cap, ne, sb

    def route_gather(h2, lg, nseq, n):
        cap, ne, sb = moe_tiles(n)
        pos, aff = _route(lg.reshape(nseq, n, LANES), cap=cap)
        pos3 = pos.reshape(nseq, N_EXPERTS, n)
        xg, vals = _gather(pos3, aff.reshape(nseq, N_EXPERTS, n), h2.reshape(nseq, n, D_MODEL),
                           cap=cap, ne=ne, sb=sb)
        return pos3, xg, vals

    pos_p, xg_p, vals_p = route_gather(h2_p, lg_p, nb_p, n_p)
    pos_s, xg_s, vals_s = route_gather(h2_s, lg_s, nb_s, n_s)
    y_p, y_s = _experts(xg_p, vals_p, xg_s, vals_s, w_e_gate[0], w_e_up[0], w_e_down[0])
    fn = row(final_norm)

    def scatter(pos, y, x1, m3, nseq, n):
        cap, ne, sb = moe_tiles(n)
        return _scatter(pos, y, x1.reshape(nseq, n, D_MODEL), m3, fn, cap=cap, ne=ne, tn=min(n, SCATTER_TILE), sb=sb)

    y_prompt = scatter(pos_p, y_p, x1_p, m_p, nb_p, n_p)
    y_sample = scatter(pos_s, y_s, x1_s, m_s, nb_s, n_s)

    new_ckv = ckv_p.reshape(nb_p, 1, n_p, KV_LORA)
    new_kpe = kpe_p.reshape(nb_p, 1, n_p, QK_ROPE)
    return (y_prompt, y_sample, new_ckv, new_kpe)
```

```python
import functools

import jax
import jax.numpy as jnp
import numpy as np
from jax import lax
from jax.experimental import pallas as pl
from jax.experimental.pallas import tpu as pltpu

D_MODEL = 1024
GRID_W = 64
CONV_DIM = 512
CONV_WIDTH = 31
N_HEADS = 8
QK_NOPE = 64
QK_ROPE = 32
V_DIM = 64
Q_LORA = 256
KV_LORA = 128
N_EXPERTS = 16
EXPERT_FF = 512
EC_FACTOR = 2
ROPE_BASE = 10000.0
EPS = 1e-6

LANES = 128
HEAD_W = LANES
ROPE_OFF = QK_NOPE
CONV_HALO = 16
LOG2E = 1.4426950408889634
VMEM_LIMIT = 48 * 1024 * 1024
IN_TILE = 512
CONV_CHUNK = 64
Q_TILE = 512
SCORE_BYTES = 12 * 1024 * 1024
MOE_ROWS = 512
MOE_TOKENS = 1024
SCATTER_TILE = 512

C_CONV = 0
C_QA = 2 * CONV_DIM
C_KVA = C_QA + Q_LORA
C_KR = C_KVA + KV_LORA
C_GATE = C_KR + LANES
C_END = C_GATE + 2 * D_MODEL
C_KRP = C_END

f32 = jnp.float32
bf16 = jnp.bfloat16


def _params(*sem):
    return pltpu.CompilerParams(dimension_semantics=sem, vmem_limit_bytes=VMEM_LIMIT)


def _dot(a, b):
    return jnp.dot(a, b, preferred_element_type=f32)


def _rms(x, g):
    return x * lax.rsqrt(jnp.mean(x * x, axis=-1, keepdims=True) + EPS) * g


def _const_spec(shape):
    nd = len(shape)
    return pl.BlockSpec(shape, lambda *_: (0,) * nd)


def _mod_spec(m3, seq_of):
    if m3.shape[0] == 1:
        return _const_spec((None, 6, D_MODEL))
    return pl.BlockSpec((None, 6, D_MODEL), lambda *idx: (seq_of(*idx), 0, 0))


def _ada_kernel(s_ref, w_ref, b_ref, o_ref):
    s = s_ref[...]
    s = s * jax.nn.sigmoid(s)
    o_ref[...] = _dot(s.astype(bf16), w_ref[...].astype(bf16)) + b_ref[...]


def _ada(mod, w_ada, b_ada):
    rows = mod.shape[0]
    n_out = w_ada.shape[1]
    tn = D_MODEL
    return pl.pallas_call(
        _ada_kernel,
        grid=(n_out // tn,),
        in_specs=[
            _const_spec((rows, D_MODEL)),
            pl.BlockSpec((D_MODEL, tn), lambda j: (0, j)),
            pl.BlockSpec((1, tn), lambda j: (0, j)),
        ],
        out_specs=pl.BlockSpec((rows, tn), lambda j: (0, j)),
        out_shape=jax.ShapeDtypeStruct((rows, n_out), f32),
        compiler_params=_params("arbitrary"),
        name="ada",
    )(mod, w_ada, b_ada.reshape(1, n_out))


def _wprep_kernel(wt_ref, place_ref, o_ref):
    def block(r0):
        return wt_ref[r0:r0 + LANES, :].T.astype(bf16)

    for j in range(C_KR // LANES):
        o_ref[:, j * LANES:(j + 1) * LANES] = block(j * LANES)
    placed = _dot(block(C_KR), place_ref[...])
    o_ref[:, C_KR:C_GATE] = placed[:, 0:LANES].astype(bf16)
    o_ref[:, C_KRP:C_KRP + LANES] = placed[:, LANES:2 * LANES].astype(bf16)
    for j in range(2 * D_MODEL // LANES):
        o_ref[:, C_GATE + j * LANES:C_GATE + (j + 1) * LANES] = block(C_KR + QK_ROPE + j * LANES)


def _wprep(w_in_t, place, *, tr=LANES):
    cols, rows = w_in_t.shape
    return pl.pallas_call(
        _wprep_kernel,
        grid=(rows // tr,),
        in_specs=[pl.BlockSpec((cols, tr), lambda i: (0, i)), _const_spec(place.shape)],
        out_specs=pl.BlockSpec((tr, C_KRP + LANES), lambda i: (i, 0)),
        out_shape=jax.ShapeDtypeStruct((rows, C_KRP + LANES), bf16),
        compiler_params=_params("parallel"),
        name="wprep",
    )(w_in_t, place)


def _conv_taps(vpad, r0, shift_ref, dw_ref, dwb_ref, ybuf, y0, ct):
    pad = CONV_WIDTH // 2
    sub = 8
    span = ((CONV_HALO - pad + CONV_WIDTH - 1) // sub) * sub
    cw = 2 * LANES
    for cb in range(CONV_DIM // cw):
        sl = slice(cb * cw, (cb + 1) * cw)
        win = vpad[pl.ds(r0, ct + 2 * CONV_HALO), sl]
        acc = jnp.zeros((ct // sub, sub, cw), f32)
        for ph in range(sub):
            wph = _dot(shift_ref[ph], win)
            for a in range(span // sub + 1):
                k = a * sub + ph - (CONV_HALO - pad)
                if 0 <= k < CONV_WIDTH:
                    acc = acc + wph[a * sub:a * sub + ct, :].reshape(ct // sub, sub, cw) * dw_ref[k, :, sl][None]
        ybuf[pl.ds(y0, ct), sl] = acc.reshape(ct, cw) + dwb_ref[:, sl]


def _conv_out(y, lng_ref, lnb_ref, wco_ref, gate):
    mu = jnp.mean(y, axis=-1, keepdims=True)
    yc = y - mu
    var = jnp.mean(yc * yc, axis=-1, keepdims=True)
    z = yc * lax.rsqrt(var + EPS) * lng_ref[...] + lnb_ref[...]
    z = z * jax.nn.sigmoid(z)
    return (gate.astype(f32) * _dot(z.astype(bf16), wco_ref[...])).astype(bf16)


def _fill_padded(vpad, v, n):
    zeros = jnp.zeros((CONV_HALO, CONV_DIM), vpad.dtype)
    vpad[0:CONV_HALO, :] = zeros
    vpad[CONV_HALO + n:2 * CONV_HALO + n, :] = zeros
    vpad[CONV_HALO:CONV_HALO + n, :] = v


def _inproj_kernel(*refs, rope, fuse_conv, q_scale, n, rt, ct):
    refs = list(refs)
    x_ref, m_ref, n1_ref, win_ref, qn_ref, wqb_ref = refs[:6]
    del refs[:6]
    if rope:
        wqbp_ref = refs.pop(0)
    kvn_ref = refs.pop(0)
    if rope:
        cos_ref, sin_ref = refs[:2]
        del refs[:2]
    if fuse_conv:
        shift_ref, dw_ref, dwb_ref, lng_ref, lnb_ref, wco_ref = refs[:6]
        del refs[:6]
    vc_ref, q_ref, ckv_ref, kr_ref, g_ref = refs[:5]
    del refs[:5]
    if not rope:
        kpe_ref = refs.pop(0)
    if fuse_conv:
        vpad, ybuf, gc = refs
    tm = x_ref.shape[0]

    sh1 = m_ref[0:1, :]
    sc1 = m_ref[1:2, :]
    h = _rms(x_ref[...], n1_ref[...]) * (1.0 + sc1) + sh1
    hb = h.astype(bf16)

    left = _dot(hb, win_ref[:, 0:C_GATE])
    v = (left[:, C_CONV:C_CONV + CONV_DIM] * jax.nn.sigmoid(left[:, C_CONV + CONV_DIM:C_QA])).astype(bf16)
    if fuse_conv:
        for s in range(tm // n):
            _fill_padded(vpad.at[s], v[s * n:(s + 1) * n, :], n)
    else:
        vc_ref[...] = v

    chunks = []
    if fuse_conv:
        chunks = [(s, c) for s in range(tm // n) for c in range(n // ct)]

    def conv_some(count):
        for _ in range(min(count, len(chunks))):
            s, c = chunks.pop(0)
            _conv_taps(vpad.at[s], c * ct, shift_ref, dw_ref, dwb_ref, ybuf, s * n + c * ct, ct)

    steps = 2 + 2 * D_MODEL // 512
    per_step = -(-len(chunks) // steps)

    qa = left[:, C_QA:C_KVA]
    qn = _rms(qa, qn_ref[...]).astype(bf16)
    q = _dot(qn, wqb_ref[...])
    if rope:
        qp = _dot(qn, wqbp_ref[...])
        cos = cos_ref[...]
        sin = sin_ref[...]
        for hd in range(N_HEADS):
            sl = slice(hd * HEAD_W, (hd + 1) * HEAD_W)
            q_ref[:, sl] = ((q[:, sl] * cos + qp[:, sl] * sin) * q_scale).astype(q_ref.dtype)
    else:
        q_ref[...] = (q * q_scale).astype(q_ref.dtype)
    conv_some(per_step)

    kva = left[:, C_KVA:C_KR]
    ckv_ref[...] = _rms(kva, kvn_ref[...]).astype(ckv_ref.dtype)

    kr = left[:, C_KR:C_GATE]
    if rope:
        krp = _dot(hb, win_ref[:, C_KRP:C_KRP + LANES])
        kr = kr * cos_ref[...] + krp * sin_ref[...]
    else:
        kpe_ref[...] = kr[:, ROPE_OFF:ROPE_OFF + QK_ROPE]
    kr_ref[...] = kr.astype(kr_ref.dtype)
    conv_some(per_step)

    gw = 512
    for j in range(2 * D_MODEL // gw):
        gate = jax.nn.sigmoid(_dot(hb, win_ref[:, C_GATE + j * gw:C_GATE + (j + 1) * gw])).astype(bf16)
        if not fuse_conv:
            g_ref[:, j * gw:(j + 1) * gw] = gate
        elif j * gw < D_MODEL:
            gc[:, j * gw:(j + 1) * gw] = gate
        else:
            g_ref[:, j * gw - D_MODEL:(j + 1) * gw - D_MODEL] = gate
        conv_some(per_step)

    if fuse_conv:
        conv_some(len(chunks))
        for r0 in range(0, tm, rt):
            vc_ref[r0:r0 + rt, :] = _conv_out(ybuf[r0:r0 + rt, :], lng_ref, lnb_ref, wco_ref, gc[r0:r0 + rt, :])


def _inproj(x2d, m3, norm1, win, q_norm, wqb, wqbp, kv_norm, cos, sin, conv_w, *, n, rope, tm, rt=256, ct=64):
    tokens = x2d.shape[0]
    fuse_conv = conv_w is not None
    assert n % tm == 0 or (tm % n == 0 and m3.shape[0] == 1 and not rope)
    assert not fuse_conv or tm % n == 0
    tiles_per_seq = max(1, n // tm)
    q_scale = float((QK_NOPE + QK_ROPE) ** -0.5 * LOG2E)
    tile = lambda w: pl.BlockSpec((tm, w), lambda i: (i, 0))
    in_specs = [
        tile(D_MODEL),
        _mod_spec(m3, lambda i: i // tiles_per_seq),
        _const_spec((1, D_MODEL)),
        _const_spec((D_MODEL, C_KRP + LANES if rope else C_END)),
        _const_spec((1, Q_LORA)),
        _const_spec(wqb.shape),
    ]
    args = [x2d, m3, norm1, win, q_norm, wqb]
    if rope:
        in_specs.append(_const_spec(wqbp.shape))
        args.append(wqbp)
    in_specs.append(_const_spec((1, KV_LORA)))
    args.append(kv_norm)
    if rope:
        tab = pl.BlockSpec((tm, LANES), lambda i: (i % tiles_per_seq, 0))
        in_specs += [tab, tab]
        args += [cos, sin]
    scratch = []
    if fuse_conv:
        args += list(conv_w)
        in_specs += [_const_spec(w.shape) for w in conv_w]
        scratch = [
            pltpu.VMEM((tm // n, n + 2 * CONV_HALO, CONV_DIM), bf16),
            pltpu.VMEM((tm, CONV_DIM), f32),
            pltpu.VMEM((tm, D_MODEL), bf16),
        ]
    gate_w = D_MODEL if fuse_conv else 2 * D_MODEL
    vc_w = D_MODEL if fuse_conv else CONV_DIM
    out_shape = [
        jax.ShapeDtypeStruct((tokens, vc_w), bf16),
        jax.ShapeDtypeStruct((tokens, N_HEADS * HEAD_W), bf16),
        jax.ShapeDtypeStruct((tokens, KV_LORA), bf16 if rope else f32),
        jax.ShapeDtypeStruct((tokens, LANES), bf16),
        jax.ShapeDtypeStruct((tokens, gate_w), bf16),
    ]
    out_specs = [tile(vc_w), tile(N_HEADS * HEAD_W), tile(KV_LORA), tile(LANES), tile(gate_w)]
    if not rope:
        out_shape.append(jax.ShapeDtypeStruct((tokens, QK_ROPE), f32))
        out_specs.append(tile(QK_ROPE))
    return pl.pallas_call(
        functools.partial(_inproj_kernel, rope=rope, fuse_conv=fuse_conv, q_scale=q_scale, n=n, rt=rt, ct=ct),
        grid=(tokens // tm,),
        in_specs=in_specs,
        out_specs=out_specs,
        out_shape=out_shape,
        scratch_shapes=scratch,
        compiler_params=_params("parallel"),
        name="inproj_rope" if rope else "inproj_conv",
    )(*args)


def _conv_kernel(v_ref, g_ref, shift_ref, dw_ref, dwb_ref, lng_ref, lnb_ref, wco_ref, o_ref, vpad, ybuf,
                 *, n, rt, ct):
    _fill_padded(vpad, v_ref[...], n)

    def conv_chunk(c, carry):
        r0 = pl.multiple_of(c * ct, ct)
        _conv_taps(vpad, r0, shift_ref, dw_ref, dwb_ref, ybuf, r0, ct)
        return carry

    lax.fori_loop(0, n // ct, conv_chunk, 0, unroll=4)

    def chunk(c, carry):
        r0 = pl.multiple_of(c * rt, rt)
        o_ref[pl.ds(r0, rt), :] = _conv_out(ybuf[pl.ds(r0, rt), :], lng_ref, lnb_ref, wco_ref, g_ref[pl.ds(r0, rt), :])
        return carry

    lax.fori_loop(0, n // rt, chunk, 0)


def _conv_weights(dw, dwb, lng, lnb, wco, ct):
    rows = ct + 2 * CONV_HALO
    i = np.arange(rows)
    shifts = jnp.asarray(np.stack([(i[None, :] == i[:, None] + ph) for ph in range(8)]), bf16)
    dw_tiles = jnp.broadcast_to(dw[:, None, :], (CONV_WIDTH, 8, CONV_DIM))
    return shifts, dw_tiles, dwb, lng, lnb, wco


def _conv(v3, g3, conv_w, *, rt=256, ct=64):
    nseq, n, _ = v3.shape
    assert v3.dtype == bf16, "the one-hot row shifts are exact only for bf16 windows"
    return pl.pallas_call(
        functools.partial(_conv_kernel, n=n, rt=rt, ct=ct),
        grid=(nseq,),
        in_specs=[
            pl.BlockSpec((None, n, CONV_DIM), lambda s: (s, 0, 0)),
            pl.BlockSpec((None, n, D_MODEL), lambda s: (s, 0, 0)),
        ] + [_const_spec(w.shape) for w in conv_w],
        out_specs=pl.BlockSpec((None, n, D_MODEL), lambda s: (s, 0, 0)),
        out_shape=jax.ShapeDtypeStruct((nseq, n, D_MODEL), bf16),
        scratch_shapes=[
            pltpu.VMEM((n + 2 * CONV_HALO, CONV_DIM), bf16),
            pltpu.VMEM((n, CONV_DIM), f32),
        ],
        compiler_params=_params("parallel"),
        name="conv",
    )(v3, g3, *conv_w)


def _attn_kernel(q_ref, ckv_ref, kpe_ref, cm_ref, g_ref, wk_ref, wv_ref, wo_ref, o_ref, k_scr, v_scr, q_scr, *, hg):
    @pl.when(pl.program_id(1) == 0)
    def _():
        ckv = ckv_ref[...].astype(bf16)
        kpe = kpe_ref[...].astype(f32)
        k = _dot(ckv, wk_ref[...])
        v = _dot(ckv, wv_ref[...])
        ones = jnp.ones((v.shape[0], V_DIM), bf16)
        for hd in range(N_HEADS):
            k_scr[hd] = (k[:, hd * HEAD_W:(hd + 1) * HEAD_W] + kpe).astype(bf16)
            vh = v[:, hd * V_DIM:(hd + 1) * V_DIM].astype(bf16)
            if hd % 2 == 0:
                v_scr[hd, :, 0:V_DIM] = vh
                v_scr[hd, :, V_DIM:LANES] = ones
            else:
                v_scr[hd, :, 0:V_DIM] = ones
                v_scr[hd, :, V_DIM:LANES] = vh

    tq = q_ref.shape[0]
    for hd in range(N_HEADS):
        q_scr[hd] = q_ref[:, hd * HEAD_W:(hd + 1) * HEAD_W]
    lane = lax.broadcasted_iota(jnp.int32, (tq, LANES), 1)
    pairs = []
    for h0 in range(0, N_HEADS, hg):
        hs = slice(h0, h0 + hg)
        s = lax.dot_general(q_scr[hs], k_scr[hs], (((2,), (2,)), ((0,), (0,))), preferred_element_type=f32)
        mx = jnp.max(s, axis=-1, keepdims=True)
        p = jnp.exp2(s - mx).astype(bf16)
        r = lax.dot_general(p, v_scr[hs], (((2,), (1,)), ((0,), (0,))), preferred_element_type=f32)
        for j in range(hg // 2):
            re, ro = r[2 * j], r[2 * j + 1]
            oe = re * (1.0 / re[:, V_DIM:V_DIM + 1])
            oo = ro * (1.0 / ro[:, 0:1])
            pairs.append(jnp.where(lane < V_DIM, oe, oo))
    attn = jnp.concatenate(pairs, axis=-1).astype(bf16)
    ao = _dot(attn, wo_ref[...])
    o_ref[...] = (cm_ref[...].astype(f32) + g_ref[...].astype(f32) * ao).astype(o_ref.dtype)


def _attn(q2d, ckv3, kpe3, cm2d, g2d, wk, wv, wo, *, n, tq):
    nseq, s_len, _ = ckv3.shape
    hg = max(2, min(N_HEADS, SCORE_BYTES // (tq * s_len * 4)))
    assert N_HEADS % hg == 0 and hg % 2 == 0
    gate_col = g2d.shape[1] // D_MODEL - 1
    qb = n // tq
    tile = lambda w, c=0: pl.BlockSpec((tq, w), lambda s, i: (s * qb + i, c))
    return pl.pallas_call(
        functools.partial(_attn_kernel, hg=hg),
        grid=(nseq, qb),
        in_specs=[
            tile(N_HEADS * HEAD_W),
            pl.BlockSpec((None, s_len, KV_LORA), lambda s, i: (s, 0, 0)),
            pl.BlockSpec((None, s_len, LANES), lambda s, i: (s, 0, 0)),
            tile(D_MODEL),
            tile(D_MODEL, gate_col),
            _const_spec(wk.shape),
            _const_spec(wv.shape),
            _const_spec(wo.shape),
        ],
        out_specs=tile(D_MODEL),
        out_shape=jax.ShapeDtypeStruct((nseq * n, D_MODEL), bf16),
        scratch_shapes=[
            pltpu.VMEM((N_HEADS, s_len, HEAD_W), bf16),
            pltpu.VMEM((N_HEADS, s_len, LANES), bf16),
            pltpu.VMEM((N_HEADS, tq, HEAD_W), bf16),
        ],
        compiler_params=_params("parallel", "arbitrary"),
        name="attn",
    )(q2d, ckv3, kpe3, cm2d, g2d, wk, wv, wo)


def _outproj_kernel(mg_ref, x_ref, m_ref, wout_ref, n2_ref, wr_ref, x1_ref, h2_ref, lg_ref):
    g1 = m_ref[2:3, :]
    sh2 = m_ref[3:4, :]
    sc2 = m_ref[4:5, :]
    x1 = x_ref[...] + g1 * _dot(mg_ref[...], wout_ref[...])
    x1_ref[...] = x1
    h2 = _rms(x1, n2_ref[...]) * (1.0 + sc2) + sh2
    hi = h2.astype(bf16)
    h2_ref[...] = hi
    lo = (h2 - hi.astype(f32)).astype(bf16)
    tm = hi.shape[0]
    both = _dot(jnp.concatenate([hi, lo], axis=0), wr_ref[...])
    s = both[0:tm] + both[tm:2 * tm]
    lg_ref[...] = s + pltpu.roll(s, LANES - N_EXPERTS, axis=1)


def _outproj(mg2d, x2d, m3, wout, norm2, wr, *, n, tm=256):
    tokens = x2d.shape[0]
    tiles_per_seq = n // tm
    tile = lambda w: pl.BlockSpec((tm, w), lambda i: (i, 0))
    return pl.pallas_call(
        _outproj_kernel,
        grid=(tokens // tm,),
        in_specs=[
            tile(D_MODEL),
            tile(D_MODEL),
            _mod_spec(m3, lambda i: i // tiles_per_seq),
            _const_spec(wout.shape),
            _const_spec((1, D_MODEL)),
            _const_spec(wr.shape),
        ],
        out_specs=(tile(D_MODEL), tile(D_MODEL), tile(LANES)),
        out_shape=(
            jax.ShapeDtypeStruct((tokens, D_MODEL), f32),
            jax.ShapeDtypeStruct((tokens, D_MODEL), bf16),
            jax.ShapeDtypeStruct((tokens, LANES), f32),
        ),
        compiler_params=_params("parallel"),
        name="outproj",
    )(mg2d, x2d, m3, wout, norm2, wr)


def _route_kernel(lg_ref, pos_ref, aff_ref, *, nseq, n, cap):
    for s in range(nseq):
        lt = lg_ref[s].T[0:N_EXPERTS, :]
        e = jnp.exp(lt - jnp.max(lt, axis=0, keepdims=True))
        aff_ref[s * N_EXPERTS:(s + 1) * N_EXPERTS, :] = e / jnp.sum(e, axis=0, keepdims=True)
    rows = nseq * N_EXPERTS
    capf = float(cap)

    def bit_step(i, t):
        cand = t | (jnp.int32(1) << (30 - i))
        thr = lax.bitcast_convert_type(cand, f32)
        cnt = jnp.sum(jnp.where(aff_ref[...] >= thr, 1.0, 0.0), axis=1, keepdims=True)
        return jnp.where(cnt >= capf, cand, t)

    t = lax.fori_loop(0, 31, bit_step, jnp.zeros((rows, 1), jnp.int32))
    thr = lax.bitcast_convert_type(t, f32)
    need = capf - jnp.sum(jnp.where(aff_ref[...] > thr, 1.0, 0.0), axis=1, keepdims=True)

    blk = 2 * LANES
    tri = jnp.where(
        lax.broadcasted_iota(jnp.int32, (blk, blk), 0) < lax.broadcasted_iota(jnp.int32, (blk, blk), 1),
        1.0, 0.0).astype(bf16)
    carry_gt = jnp.zeros((rows, 1), f32)
    carry_eq = jnp.zeros((rows, 1), f32)
    for b in range(n // blk):
        sl = slice(b * blk, (b + 1) * blk)
        ab = aff_ref[:, sl]
        gt = ab > thr
        eq = ab == thr
        gtb = jnp.where(gt, 1.0, 0.0)
        eqb = jnp.where(eq, 1.0, 0.0)
        pre_gt = _dot(gtb.astype(bf16), tri) + carry_gt
        pre_eq = _dot(eqb.astype(bf16), tri) + carry_eq
        carry_gt = carry_gt + jnp.sum(gtb, axis=1, keepdims=True)
        carry_eq = carry_eq + jnp.sum(eqb, axis=1, keepdims=True)
        sel = gt | (eq & (pre_eq < need))
        slot = pre_gt + jnp.minimum(pre_eq, need)
        pos_ref[:, sl] = jnp.where(sel, slot, -1.0).astype(jnp.int32)


def _route(lg3, *, cap):
    nseq, n, _ = lg3.shape
    rows = nseq * N_EXPERTS
    return pl.pallas_call(
        functools.partial(_route_kernel, nseq=nseq, n=n, cap=cap),
        grid=(1,),
        in_specs=[_const_spec(lg3.shape)],
        out_specs=(_const_spec((rows, n)), _const_spec((rows, n))),
        out_shape=(jax.ShapeDtypeStruct((rows, n), jnp.int32), jax.ShapeDtypeStruct((rows, n), f32)),
        compiler_params=_params("arbitrary"),
        name="route",
    )(lg3)


def _slot_hits(pos_ref, e, cap):
    width = pos_ref.shape[1]
    return lax.broadcasted_iota(jnp.int32, (cap, width), 0) == pos_ref[e:e + 1, :]


def _one_hot(hits):
    return jnp.concatenate([jnp.where(h, 1.0, 0.0).astype(bf16) for h in hits], axis=0)


def _gather_kernel(pos_ref, aff_ref, h2_ref, xg_ref, vals_ref, *, cap, ne, sb):
    for g in range(sb):
        rows = slice(g * cap, (g + 1) * cap)
        for e0 in range(0, N_EXPERTS, ne):
            hits = [_slot_hits(pos_ref.at[g], e0 + e, cap) for e in range(ne)]
            xg = _dot(_one_hot(hits), h2_ref[g]).astype(xg_ref.dtype)
            for e in range(ne):
                xg_ref[e0 + e, rows, :] = xg[e * cap:(e + 1) * cap, :]
                vals = jnp.sum(jnp.where(hits[e], aff_ref[g, e0 + e:e0 + e + 1, :], 0.0), axis=1, keepdims=True)
                vals_ref[e0 + e, rows, :] = jnp.broadcast_to(vals, (cap, LANES))


def _gather(pos3, aff3, h23, *, cap, ne, sb):
    nseq, _, n = pos3.shape
    return pl.pallas_call(
        functools.partial(_gather_kernel, cap=cap, ne=ne, sb=sb),
        grid=(nseq // sb,),
        in_specs=[
            pl.BlockSpec((sb, N_EXPERTS, n), lambda s: (s, 0, 0)),
            pl.BlockSpec((sb, N_EXPERTS, n), lambda s: (s, 0, 0)),
            pl.BlockSpec((sb, n, D_MODEL), lambda s: (s, 0, 0)),
        ],
        out_specs=(
            pl.BlockSpec((N_EXPERTS, sb * cap, D_MODEL), lambda s: (0, s, 0)),
            pl.BlockSpec((N_EXPERTS, sb * cap, LANES), lambda s: (0, s, 0)),
        ),
        out_shape=(
            jax.ShapeDtypeStruct((N_EXPERTS, nseq * cap, D_MODEL), bf16),
            jax.ShapeDtypeStruct((N_EXPERTS, nseq * cap, LANES), f32),
        ),
        compiler_params=_params("parallel"),
        name="gather",
    )(pos3, aff3, h23)


def _experts_kernel(xp_ref, vp_ref, xs_ref, vs_ref, wg_ref, wu_ref, wd_ref, yp_ref, ys_ref, *, rc):
    wg = wg_ref[...].astype(bf16)
    wu = wu_ref[...].astype(bf16)
    wd = wd_ref[...].astype(bf16)
    for x_ref, v_ref, y_ref in ((xp_ref, vp_ref, yp_ref), (xs_ref, vs_ref, ys_ref)):
        for r0 in range(0, x_ref.shape[0], rc):
            x = x_ref[r0:r0 + rc, :]
            a = _dot(x, wg)
            u = _dot(x, wu)
            hm = (a * jax.nn.sigmoid(a) * u).astype(bf16)
            y = _dot(hm, wd) * v_ref[r0:r0 + rc, 0:1]
            y_ref[r0:r0 + rc, :] = y.astype(y_ref.dtype)


def _experts(xg_p, vals_p, xg_s, vals_s, wg, wu, wd, *, rc=512):
    rp = xg_p.shape[1]
    rs = xg_s.shape[1]
    per_e = lambda r, w: pl.BlockSpec((None, r, w), lambda e: (e, 0, 0))
    return pl.pallas_call(
        functools.partial(_experts_kernel, rc=rc),
        grid=(N_EXPERTS,),
        in_specs=[
            per_e(rp, D_MODEL), per_e(rp, LANES), per_e(rs, D_MODEL), per_e(rs, LANES),
            per_e(D_MODEL, EXPERT_FF), per_e(D_MODEL, EXPERT_FF), per_e(EXPERT_FF, D_MODEL),
        ],
        out_specs=(per_e(rp, D_MODEL), per_e(rs, D_MODEL)),
        out_shape=(
            jax.ShapeDtypeStruct((N_EXPERTS, rp, D_MODEL), bf16),
            jax.ShapeDtypeStruct((N_EXPERTS, rs, D_MODEL), bf16),
        ),
        compiler_params=_params("parallel"),
        name="experts",
    )(xg_p, vals_p, xg_s, vals_s, wg, wu, wd)


def _scatter_kernel(pos_ref, y_ref, x1_ref, m_ref, fn_ref, o_ref, *, tn, cap, ne, sb):
    g2 = m_ref[5:6, :]
    for g in range(sb):
        rows = slice(g * cap, (g + 1) * cap)
        moe = jnp.zeros((tn, D_MODEL), f32)
        for e0 in range(0, N_EXPERTS, ne):
            onehot = _one_hot([_slot_hits(pos_ref.at[g], e0 + e, cap) for e in range(ne)])
            y = jnp.concatenate([y_ref[e0 + e, rows, :] for e in range(ne)], axis=0)
            moe = moe + lax.dot_general(onehot, y, (((0,), (0,)), ((), ())), preferred_element_type=f32)
        xo = x1_ref[g] + g2 * moe
        o_ref[g] = _rms(xo, fn_ref[...])


def _scatter(pos3, y, x13, m3, fn, *, cap, ne, tn, sb):
    nseq, _, n = pos3.shape
    assert sb == 1 or m3.shape[0] == 1, "sequences sharing a grid step must share their modulation rows"
    return pl.pallas_call(
        functools.partial(_scatter_kernel, tn=tn, cap=cap, ne=ne, sb=sb),
        grid=(nseq // sb, n // tn),
        in_specs=[
            pl.BlockSpec((sb, N_EXPERTS, tn), lambda s, i: (s, 0, i)),
            pl.BlockSpec((N_EXPERTS, sb * cap, D_MODEL), lambda s, i: (0, s, 0)),
            pl.BlockSpec((sb, tn, D_MODEL), lambda s, i: (s, i, 0)),
            _mod_spec(m3, lambda s, i: s),
            _const_spec((1, D_MODEL)),
        ],
        out_specs=pl.BlockSpec((sb, tn, D_MODEL), lambda s, i: (s, i, 0)),
        out_shape=jax.ShapeDtypeStruct((nseq, n, D_MODEL), f32),
        compiler_params=_params("parallel", "arbitrary"),
        name="scatter",
    )(pos3, y, x13, m3, fn)


def _rope_tables(n):
    t = np.arange(n)
    half = QK_ROPE // 2
    freqs = ROPE_BASE ** (-np.arange(0, half, 2, dtype=np.float64) / half)
    ang_r = (t // GRID_W)[:, None] * freqs
    ang_c = (t % GRID_W)[:, None] * freqs
    cr, sr, cc, sc = np.cos(ang_r), np.sin(ang_r), np.cos(ang_c), np.sin(ang_c)
    cos = np.ones((n, HEAD_W))
    sin = np.zeros((n, HEAD_W))
    cos[:, ROPE_OFF:ROPE_OFF + QK_ROPE] = np.concatenate([cr, cr, cc, cc], axis=-1)
    sin[:, ROPE_OFF:ROPE_OFF + QK_ROPE] = np.concatenate([-sr, sr, -sc, sc], axis=-1)
    return jnp.asarray(cos, f32), jnp.asarray(sin, f32)


_PARTNER = np.concatenate([np.arange(8, 16), np.arange(0, 8), np.arange(24, 32), np.arange(16, 24)])


def _rope_partner(w):
    q = QK_ROPE // 4
    return jnp.concatenate([w[..., q:2 * q], w[..., 0:q], w[..., 3 * q:4 * q], w[..., 2 * q:3 * q]], axis=-1)


def _rope_placement():
    place = np.zeros((LANES, 2 * LANES), np.float32)
    d = np.arange(QK_ROPE)
    place[d, ROPE_OFF + d] = 1.0
    place[_PARTNER, LANES + ROPE_OFF + d] = 1.0
    return jnp.asarray(place, bf16)


def _head_blocks(w_nope, w_rope):
    rows = w_nope.shape[0]
    if w_rope is None:
        w_rope = jnp.zeros((rows, N_HEADS, QK_ROPE), w_nope.dtype)
    z = jnp.zeros((rows, N_HEADS, HEAD_W - QK_NOPE - QK_ROPE), w_nope.dtype)
    return jnp.concatenate([w_nope, w_rope, z], axis=-1).reshape(rows, N_HEADS * HEAD_W)


def kernel(x_prompt, x_sample, cache_ckv, cache_kpe, c, c_ctx, w_ada, b_ada, norm1, w_in, conv_dw, conv_dw_b,
           conv_ln_g, conv_ln_b, w_conv_out, q_norm, w_qb, kv_norm, w_kvb, w_o_mla, w_out, norm2, w_router,
           w_e_gate, w_e_up, w_e_down, final_norm):
    assert w_ada.shape[0] == 1, "single trunk layer"
    nb_p, n_p, _ = x_prompt.shape
    nb_s, n_s, _ = x_sample.shape

    win = _wprep(w_in[0].T, _rope_placement())
    wq = w_qb[0].reshape(Q_LORA, N_HEADS, QK_NOPE + QK_ROPE)
    wqb = _head_blocks(wq[..., :QK_NOPE], wq[..., QK_NOPE:]).astype(bf16)
    wqbp = _head_blocks(jnp.zeros_like(wq[..., :QK_NOPE]), _rope_partner(wq[..., QK_NOPE:])).astype(bf16)
    wkv = w_kvb[0].reshape(KV_LORA, N_HEADS, QK_NOPE + V_DIM)
    wk = _head_blocks(wkv[..., :QK_NOPE], None).astype(bf16)
    wv = wkv[..., QK_NOPE:].reshape(KV_LORA, N_HEADS * V_DIM).astype(bf16)
    wco = w_conv_out[0].astype(bf16)
    wo = w_o_mla[0].astype(bf16)
    wout = w_out[0].astype(bf16)
    wr_hi = w_router[0].astype(bf16)
    wr_lo = (w_router[0] - wr_hi.astype(f32)).astype(bf16)
    wr = jnp.concatenate([wr_hi, wr_lo, jnp.zeros((D_MODEL, LANES - 2 * N_EXPERTS), bf16)], axis=-1)
    row = lambda a: a.reshape(1, -1)

    mod = jnp.concatenate([c_ctx[None, :], c, jnp.zeros((8 - 1 - nb_s, D_MODEL), f32)], axis=0)
    m = _ada(mod, w_ada[0], b_ada[0]).reshape(8, 6, D_MODEL)
    m_p, m_s = m[0:1], m[1:1 + nb_s]
    cos, sin = _rope_tables(n_s)

    conv_w = _conv_weights(conv_dw[0], row(conv_dw_b[0]), row(conv_ln_g[0]), row(conv_ln_b[0]), wco, CONV_CHUNK)

    def mixers(x, m3, rope, ctx_ckv, ctx_kpe):
        nseq, n, _ = x.shape
        x2d = x.reshape(nseq * n, D_MODEL)
        fuse_conv = IN_TILE % n == 0
        cm, q, ckv, kr, g, *kpe = _inproj(x2d, m3, row(norm1[0]), win, row(q_norm[0]), wqb, wqbp, row(kv_norm[0]),
                                          cos, sin, conv_w if fuse_conv else None, n=n, rope=rope, tm=IN_TILE,
                                          ct=CONV_CHUNK)
        if not fuse_conv:
            cm = _conv(cm.reshape(nseq, n, CONV_DIM), g.reshape(nseq, n, 2 * D_MODEL), conv_w, ct=CONV_CHUNK)
        keys_ckv = ckv.reshape(nseq, n, KV_LORA)
        keys_kpe = kr.reshape(nseq, n, LANES)
        if ctx_ckv is not None:
            keys_ckv = jnp.concatenate([ctx_ckv.astype(keys_ckv.dtype), keys_ckv], axis=1)
            keys_kpe = jnp.concatenate([ctx_kpe.astype(keys_kpe.dtype), keys_kpe], axis=1)
        mg = _attn(q, keys_ckv, keys_kpe, cm.reshape(nseq * n, D_MODEL), g, wk, wv, wo, n=n, tq=min(n, Q_TILE))
        x1, h2, lg = _outproj(mg, x2d, m3, wout, row(norm2[0]), wr, n=n)
        return x1, h2, lg, ckv, kpe

    ctx_kpe = jnp.pad(cache_kpe[:, 0], ((0, 0), (0, 0), (ROPE_OFF, LANES - ROPE_OFF - QK_ROPE)))
    x1_p, h2_p, lg_p, ckv_p, (kpe_p,) = mixers(x_prompt, m_p, False, None, None)
    x1_s, h2_s, lg_s, _, _ = mixers(x_sample, m_s, True, cache_ckv[:, 0], ctx_kpe)

    def moe_tiles(n):
        cap = EC_FACTOR * n // N_EXPERTS
        ne = N_EXPERTS if N_EXPERTS * cap <= MOE_ROWS else 1
        sb = max(1, MOE_TOKENS // n)
        return cap, ne, sb

    def route_gather(h2, lg, nseq, n):
        cap, ne, sb = moe_tiles(n)
        pos, aff = _route(lg.reshape(nseq, n, LANES), cap=cap)
        pos3 = pos.reshape(nseq, N_EXPERTS, n)
        xg, vals = _gather(pos3, aff.reshape(nseq, N_EXPERTS, n), h2.reshape(nseq, n, D_MODEL),
                           cap=cap, ne=ne, sb=sb)
        return pos3, xg, vals

    pos_p, xg_p, vals_p = route_gather(h2_p, lg_p, nb_p, n_p)
    pos_s, xg_s, vals_s = route_gather(h2_s, lg_s, nb_s, n_s)
    y_p, y_s = _experts(xg_p, vals_p, xg_s, vals_s, w_e_gate[0], w_e_up[0], w_e_down[0])
    fn = row(final_norm)

    def scatter(pos, y, x1, m3, nseq, n):
        cap, ne, sb = moe_tiles(n)
        return _scatter(pos, y, x1.reshape(nseq, n, D_MODEL), m3, fn, cap=cap, ne=ne, tn=min(n, SCATTER_TILE), sb=sb)

    y_prompt = scatter(pos_p, y_p, x1_p, m_p, nb_p, n_p)
    y_sample = scatter(pos_s, y_s, x1_s, m_s, nb_s, n_s)

    new_ckv = ckv_p.reshape(nb_p, 1, n_p, KV_LORA)
    new_kpe = kpe_p.reshape(nb_p, 1, n_p, QK_ROPE)
    return (y_prompt, y_sample, new_ckv, new_kpe)
```

```python
import functools

import jax
import jax.numpy as jnp
import numpy as np
from jax import lax
from jax.experimental import pallas as pl
from jax.experimental.pallas import tpu as pltpu

D_MODEL = 1024
GRID_W = 64
CONV_DIM = 512
CONV_WIDTH = 31
N_HEADS = 8
QK_NOPE = 64
QK_ROPE = 32
V_DIM = 64
Q_LORA = 256
KV_LORA = 128
N_EXPERTS = 16
EXPERT_FF = 512
EC_FACTOR = 2
ROPE_BASE = 10000.0
EPS = 1e-6

LANES = 128
HEAD_W = LANES
ROPE_OFF = QK_NOPE
CONV_HALO = 16
LOG2E = 1.4426950408889634
VMEM_LIMIT = 48 * 1024 * 1024
IN_TILE = 512
CONV_CHUNK = 64
Q_TILE = 512
SCORE_BYTES = 12 * 1024 * 1024
MOE_ROWS = 512
MOE_TOKENS = 1024
SCATTER_TILE = 512

C_CONV = 0
C_QA = 2 * CONV_DIM
C_KVA = C_QA + Q_LORA
C_KR = C_KVA + KV_LORA
C_GATE = C_KR + LANES
C_END = C_GATE + 2 * D_MODEL
C_KRP = C_END

f32 = jnp.float32
bf16 = jnp.bfloat16


def _params(*sem):
    return pltpu.CompilerParams(dimension_semantics=sem, vmem_limit_bytes=VMEM_LIMIT)


def _dot(a, b):
    return jnp.dot(a, b, preferred_element_type=f32)


def _rms(x, g):
    return x * lax.rsqrt(jnp.mean(x * x, axis=-1, keepdims=True) + EPS) * g


def _const_spec(shape):
    nd = len(shape)
    return pl.BlockSpec(shape, lambda *_: (0,) * nd)


def _mod_spec(m3, seq_of):
    if m3.shape[0] == 1:
        return _const_spec((None, 6, D_MODEL))
    return pl.BlockSpec((None, 6, D_MODEL), lambda *idx: (seq_of(*idx), 0, 0))


def _ada_kernel(s_ref, w_ref, b_ref, o_ref):
    s = s_ref[...]
    s = s * jax.nn.sigmoid(s)
    o_ref[...] = _dot(s.astype(bf16), w_ref[...].astype(bf16)) + b_ref[...]


def _ada(mod, w_ada, b_ada):
    rows = mod.shape[0]
    n_out = w_ada.shape[1]
    tn = D_MODEL
    return pl.pallas_call(
        _ada_kernel,
        grid=(n_out // tn,),
        in_specs=[
            _const_spec((rows, D_MODEL)),
            pl.BlockSpec((D_MODEL, tn), lambda j: (0, j)),
            pl.BlockSpec((1, tn), lambda j: (0, j)),
        ],
        out_specs=pl.BlockSpec((rows, tn), lambda j: (0, j)),
        out_shape=jax.ShapeDtypeStruct((rows, n_out), f32),
        compiler_params=_params("arbitrary"),
        name="ada",
    )(mod, w_ada, b_ada.reshape(1, n_out))


def _wprep_kernel(wt_ref, place_ref, o_ref):
    def block(r0):
        return wt_ref[r0:r0 + LANES, :].T.astype(bf16)

    for j in range(C_KR // LANES):
        o_ref[:, j * LANES:(j + 1) * LANES] = block(j * LANES)
    placed = _dot(block(C_KR), place_ref[...])
    o_ref[:, C_KR:C_GATE] = placed[:, 0:LANES].astype(bf16)
    o_ref[:, C_KRP:C_KRP + LANES] = placed[:, LANES:2 * LANES].astype(bf16)
    for j in range(2 * D_MODEL // LANES):
        o_ref[:, C_GATE + j * LANES:C_GATE + (j + 1) * LANES] = block(C_KR + QK_ROPE + j * LANES)


def _wprep(w_in_t, place, *, tr=LANES):
    cols, rows = w_in_t.shape
    return pl.pallas_call(
        _wprep_kernel,
        grid=(rows // tr,),
        in_specs=[pl.BlockSpec((cols, tr), lambda i: (0, i)), _const_spec(place.shape)],
        out_specs=pl.BlockSpec((tr, C_KRP + LANES), lambda i: (i, 0)),
        out_shape=jax.ShapeDtypeStruct((rows, C_KRP + LANES), bf16),
        compiler_params=_params("parallel"),
        name="wprep",
    )(w_in_t, place)


def _conv_taps(vpad, r0, shift_ref, dw_ref, dwb_ref, ybuf, y0, ct):
    pad = CONV_WIDTH // 2
    sub = 8
    span = ((CONV_HALO - pad + CONV_WIDTH - 1) // sub) * sub
    cw = 2 * LANES
    for cb in range(CONV_DIM // cw):
        sl = slice(cb * cw, (cb + 1) * cw)
        win = vpad[pl.ds(r0, ct + 2 * CONV_HALO), sl]
        acc = jnp.zeros((ct // sub, sub, cw), f32)
        for ph in range(sub):
            wph = _dot(shift_ref[ph], win)
            for a in range(span // sub + 1):
                k = a * sub + ph - (CONV_HALO - pad)
                if 0 <= k < CONV_WIDTH:
                    acc = acc + wph[a * sub:a * sub + ct, :].reshape(ct // sub, sub, cw) * dw_ref[k, :, sl][None]
        ybuf[pl.ds(y0, ct), sl] = acc.reshape(ct, cw) + dwb_ref[:, sl]


def _conv_out(y, lng_ref, lnb_ref, wco_ref, gate):
    mu = jnp.mean(y, axis=-1, keepdims=True)
    yc = y - mu
    var = jnp.mean(yc * yc, axis=-1, keepdims=True)
    z = yc * lax.rsqrt(var + EPS) * lng_ref[...] + lnb_ref[...]
    z = z * jax.nn.sigmoid(z)
    return (gate.astype(f32) * _dot(z.astype(bf16), wco_ref[...])).astype(bf16)


def _fill_padded(vpad, v, n):
    zeros = jnp.zeros((CONV_HALO, CONV_DIM), vpad.dtype)
    vpad[0:CONV_HALO, :] = zeros
    vpad[CONV_HALO + n:2 * CONV_HALO + n, :] = zeros
    vpad[CONV_HALO:CONV_HALO + n, :] = v


def _inproj_kernel(*refs, rope, fuse_conv, q_scale, n, rt, ct):
    refs = list(refs)
    x_ref, m_ref, n1_ref, win_ref, qn_ref, wqb_ref = refs[:6]
    del refs[:6]
    if rope:
        wqbp_ref = refs.pop(0)
    kvn_ref = refs.pop(0)
    if rope:
        cos_ref, sin_ref = refs[:2]
        del refs[:2]
    if fuse_conv:
        shift_ref, dw_ref, dwb_ref, lng_ref, lnb_ref, wco_ref = refs[:6]
        del refs[:6]
    vc_ref, q_ref, ckv_ref, kr_ref, g_ref = refs[:5]
    del refs[:5]
    if not rope:
        kpe_ref = refs.pop(0)
    if fuse_conv:
        vpad, ybuf, gc = refs
    tm = x_ref.shape[0]

    sh1 = m_ref[0:1, :]
    sc1 = m_ref[1:2, :]
    h = _rms(x_ref[...], n1_ref[...]) * (1.0 + sc1) + sh1
    hb = h.astype(bf16)

    left = _dot(hb, win_ref[:, 0:C_GATE])
    v = (left[:, C_CONV:C_CONV + CONV_DIM] * jax.nn.sigmoid(left[:, C_CONV + CONV_DIM:C_QA])).astype(bf16)
    if fuse_conv:
        for s in range(tm // n):
            _fill_padded(vpad.at[s], v[s * n:(s + 1) * n, :], n)
    else:
        vc_ref[...] = v

    chunks = []
    if fuse_conv:
        chunks = [(s, c) for s in range(tm // n) for c in range(n // ct)]

    def conv_some(count):
        for _ in range(min(count, len(chunks))):
            s, c = chunks.pop(0)
            _conv_taps(vpad.at[s], c * ct, shift_ref, dw_ref, dwb_ref, ybuf, s * n + c * ct, ct)

    steps = 2 + 2 * D_MODEL // 512
    per_step = -(-len(chunks) // steps)

    qa = left[:, C_QA:C_KVA]
    qn = _rms(qa, qn_ref[...]).astype(bf16)
    q = _dot(qn, wqb_ref[...])
    if rope:
        qp = _dot(qn, wqbp_ref[...])
        cos = cos_ref[...]
        sin = sin_ref[...]
        for hd in range(N_HEADS):
            sl = slice(hd * HEAD_W, (hd + 1) * HEAD_W)
            q_ref[:, sl] = ((q[:, sl] * cos + qp[:, sl] * sin) * q_scale).astype(q_ref.dtype)
    else:
        q_ref[...] = (q * q_scale).astype(q_ref.dtype)
    conv_some(per_step)

    kva = left[:, C_KVA:C_KR]
    ckv_ref[...] = _rms(kva, kvn_ref[...]).astype(ckv_ref.dtype)

    kr = left[:, C_KR:C_GATE]
    if rope:
        krp = _dot(hb, win_ref[:, C_KRP:C_KRP + LANES])
        kr = kr * cos_ref[...] + krp * sin_ref[...]
    else:
        kpe_ref[...] = kr[:, ROPE_OFF:ROPE_OFF + QK_ROPE]
    kr_ref[...] = kr.astype(kr_ref.dtype)
    conv_some(per_step)

    gw = 512
    for j in range(2 * D_MODEL // gw):
        gate = jax.nn.sigmoid(_dot(hb, win_ref[:, C_GATE + j * gw:C_GATE + (j + 1) * gw])).astype(bf16)
        if not fuse_conv:
            g_ref[:, j * gw:(j + 1) * gw] = gate
        elif j * gw < D_MODEL:
            gc[:, j * gw:(j + 1) * gw] = gate
        else:
            g_ref[:, j * gw - D_MODEL:(j + 1) * gw - D_MODEL] = gate
        conv_some(per_step)

    if fuse_conv:
        conv_some(len(chunks))
        for r0 in range(0, tm, rt):
            vc_ref[r0:r0 + rt, :] = _conv_out(ybuf[r0:r0 + rt, :], lng_ref, lnb_ref, wco_ref, gc[r0:r0 + rt, :])


def _inproj(x2d, m3, norm1, win, q_norm, wqb, wqbp, kv_norm, cos, sin, conv_w, *, n, rope, tm, rt=256, ct=64):
    tokens = x2d.shape[0]
    fuse_conv = conv_w is not None
    assert n % tm == 0 or (tm % n == 0 and m3.shape[0] == 1 and not rope)
    assert not fuse_conv or tm % n == 0
    tiles_per_seq = max(1, n // tm)
    q_scale = float((QK_NOPE + QK_ROPE) ** -0.5 * LOG2E)
    tile = lambda w: pl.BlockSpec((tm, w), lambda i: (i, 0))
    in_specs = [
        tile(D_MODEL),
        _mod_spec(m3, lambda i: i // tiles_per_seq),
        _const_spec((1, D_MODEL)),
        _const_spec((D_MODEL, C_KRP + LANES if rope else C_END)),
        _const_spec((1, Q_LORA)),
        _const_spec(wqb.shape),
    ]
    args = [x2d, m3, norm1, win, q_norm, wqb]
    if rope:
        in_specs.append(_const_spec(wqbp.shape))
        args.append(wqbp)
    in_specs.append(_const_spec((1, KV_LORA)))
    args.append(kv_norm)
    if rope:
        tab = pl.BlockSpec((tm, LANES), lambda i: (i % tiles_per_seq, 0))
        in_specs += [tab, tab]
        args += [cos, sin]
    scratch = []
    if fuse_conv:
        args += list(conv_w)
        in_specs += [_const_spec(w.shape) for w in conv_w]
        scratch = [
            pltpu.VMEM((tm // n, n + 2 * CONV_HALO, CONV_DIM), bf16),
            pltpu.VMEM((tm, CONV_DIM), f32),
            pltpu.VMEM((tm, D_MODEL), bf16),
        ]
    gate_w = D_MODEL if fuse_conv else 2 * D_MODEL
    vc_w = D_MODEL if fuse_conv else CONV_DIM
    out_shape = [
        jax.ShapeDtypeStruct((tokens, vc_w), bf16),
        jax.ShapeDtypeStruct((tokens, N_HEADS * HEAD_W), bf16),
        jax.ShapeDtypeStruct((tokens, KV_LORA), bf16 if rope else f32),
        jax.ShapeDtypeStruct((tokens, LANES), bf16),
        jax.ShapeDtypeStruct((tokens, gate_w), bf16),
    ]
    out_specs = [tile(vc_w), tile(N_HEADS * HEAD_W), tile(KV_LORA), tile(LANES), tile(gate_w)]
    if not rope:
        out_shape.append(jax.ShapeDtypeStruct((tokens, QK_ROPE), f32))
        out_specs.append(tile(QK_ROPE))
    return pl.pallas_call(
        functools.partial(_inproj_kernel, rope=rope, fuse_conv=fuse_conv, q_scale=q_scale, n=n, rt=rt, ct=ct),
        grid=(tokens // tm,),
        in_specs=in_specs,
        out_specs=out_specs,
        out_shape=out_shape,
        scratch_shapes=scratch,
        compiler_params=_params("parallel"),
        name="inproj_rope" if rope else "inproj_conv",
    )(*args)


def _conv_kernel(v_ref, g_ref, shift_ref, dw_ref, dwb_ref, lng_ref, lnb_ref, wco_ref, o_ref, vpad, ybuf,
                 *, n, rt, ct):
    _fill_padded(vpad, v_ref[...], n)

    def conv_chunk(c, carry):
        r0 = pl.multiple_of(c * ct, ct)
        _conv_taps(vpad, r0, shift_ref, dw_ref, dwb_ref, ybuf, r0, ct)
        return carry

    lax.fori_loop(0, n // ct, conv_chunk, 0, unroll=4)

    def chunk(c, carry):
        r0 = pl.multiple_of(c * rt, rt)
        o_ref[pl.ds(r0, rt), :] = _conv_out(ybuf[pl.ds(r0, rt), :], lng_ref, lnb_ref, wco_ref, g_ref[pl.ds(r0, rt), :])
        return carry

    lax.fori_loop(0, n // rt, chunk, 0)


def _conv_weights(dw, dwb, lng, lnb, wco, ct):
    rows = ct + 2 * CONV_HALO
    i = np.arange(rows)
    shifts = jnp.asarray(np.stack([(i[None, :] == i[:, None] + ph) for ph in range(8)]), bf16)
    dw_tiles = jnp.broadcast_to(dw[:, None, :], (CONV_WIDTH, 8, CONV_DIM))
    return shifts, dw_tiles, dwb, lng, lnb, wco


def _conv(v3, g3, conv_w, *, rt=256, ct=64):
    nseq, n, _ = v3.shape
    assert v3.dtype == bf16, "the one-hot row shifts are exact only for bf16 windows"
    return pl.pallas_call(
        functools.partial(_conv_kernel, n=n, rt=rt, ct=ct),
        grid=(nseq,),
        in_specs=[
            pl.BlockSpec((None, n, CONV_DIM), lambda s: (s, 0, 0)),
            pl.BlockSpec((None, n, D_MODEL), lambda s: (s, 0, 0)),
        ] + [_const_spec(w.shape) for w in conv_w],
        out_specs=pl.BlockSpec((None, n, D_MODEL), lambda s: (s, 0, 0)),
        out_shape=jax.ShapeDtypeStruct((nseq, n, D_MODEL), bf16),
        scratch_shapes=[
            pltpu.VMEM((n + 2 * CONV_HALO, CONV_DIM), bf16),
            pltpu.VMEM((n, CONV_DIM), f32),
        ],
        compiler_params=_params("parallel"),
        name="conv",
    )(v3, g3, *conv_w)


def _attn_kernel(q_ref, ckv_ref, kpe_ref, cm_ref, g_ref, x_ref, m_ref, wk_ref, wv_ref, wo_ref, wout_ref, n2_ref,
                 wr_ref, x1_ref, h2_ref, lg_ref, k_scr, v_scr, q_scr, *, hg):
    @pl.when(pl.program_id(1) == 0)
    def _():
        ckv = ckv_ref[...].astype(bf16)
        kpe = kpe_ref[...].astype(f32)
        k = _dot(ckv, wk_ref[...])
        v = _dot(ckv, wv_ref[...])
        ones = jnp.ones((v.shape[0], V_DIM), bf16)
        for hd in range(N_HEADS):
            k_scr[hd] = (k[:, hd * HEAD_W:(hd + 1) * HEAD_W] + kpe).astype(bf16)
            vh = v[:, hd * V_DIM:(hd + 1) * V_DIM].astype(bf16)
            if hd % 2 == 0:
                v_scr[hd, :, 0:V_DIM] = vh
                v_scr[hd, :, V_DIM:LANES] = ones
            else:
                v_scr[hd, :, 0:V_DIM] = ones
                v_scr[hd, :, V_DIM:LANES] = vh

    tq = q_ref.shape[0]
    for hd in range(N_HEADS):
        q_scr[hd] = q_ref[:, hd * HEAD_W:(hd + 1) * HEAD_W]
    lane = lax.broadcasted_iota(jnp.int32, (tq, LANES), 1)
    pairs = []
    for h0 in range(0, N_HEADS, hg):
        hs = slice(h0, h0 + hg)
        s = lax.dot_general(q_scr[hs], k_scr[hs], (((2,), (2,)), ((0,), (0,))), preferred_element_type=f32)
        mx = jnp.max(s, axis=-1, keepdims=True)
        p = jnp.exp2(s - mx).astype(bf16)
        r = lax.dot_general(p, v_scr[hs], (((2,), (1,)), ((0,), (0,))), preferred_element_type=f32)
        for j in range(hg // 2):
            re, ro = r[2 * j], r[2 * j + 1]
            oe = re * (1.0 / re[:, V_DIM:V_DIM + 1])
            oo = ro * (1.0 / ro[:, 0:1])
            pairs.append(jnp.where(lane < V_DIM, oe, oo))
    attn = jnp.concatenate(pairs, axis=-1).astype(bf16)
    merged = (cm_ref[...].astype(f32) + g_ref[...].astype(f32) * _dot(attn, wo_ref[...])).astype(bf16)

    g1 = m_ref[2:3, :]
    sh2 = m_ref[3:4, :]
    sc2 = m_ref[4:5, :]
    x1 = x_ref[...] + g1 * _dot(merged, wout_ref[...])
    x1_ref[...] = x1
    h2 = _rms(x1, n2_ref[...]) * (1.0 + sc2) + sh2
    hi = h2.astype(bf16)
    h2_ref[...] = hi
    lo = (h2 - hi.astype(f32)).astype(bf16)
    both = _dot(jnp.concatenate([hi, lo], axis=0), wr_ref[...])
    s = both[0:tq] + both[tq:2 * tq]
    lg_ref[...] = s + pltpu.roll(s, LANES - N_EXPERTS, axis=1)


def _attn(q2d, ckv3, kpe3, cm2d, g2d, x2d, m3, wk, wv, wo, wout, norm2, wr, *, n, tq):
    nseq, s_len, _ = ckv3.shape
    hg = max(2, min(N_HEADS, SCORE_BYTES // (tq * s_len * 4)))
    assert N_HEADS % hg == 0 and hg % 2 == 0
    gate_col = g2d.shape[1] // D_MODEL - 1
    qb = n // tq
    tokens = nseq * n
    tile = lambda w, c=0: pl.BlockSpec((tq, w), lambda s, i: (s * qb + i, c))
    return pl.pallas_call(
        functools.partial(_attn_kernel, hg=hg),
        grid=(nseq, qb),
        in_specs=[
            tile(N_HEADS * HEAD_W),
            pl.BlockSpec((None, s_len, KV_LORA), lambda s, i: (s, 0, 0)),
            pl.BlockSpec((None, s_len, LANES), lambda s, i: (s, 0, 0)),
            tile(D_MODEL),
            tile(D_MODEL, gate_col),
            tile(D_MODEL),
            _mod_spec(m3, lambda s, i: s),
            _const_spec(wk.shape),
            _const_spec(wv.shape),
            _const_spec(wo.shape),
            _const_spec(wout.shape),
            _const_spec((1, D_MODEL)),
            _const_spec(wr.shape),
        ],
        out_specs=(tile(D_MODEL), tile(D_MODEL), tile(LANES)),
        out_shape=(
            jax.ShapeDtypeStruct((tokens, D_MODEL), f32),
            jax.ShapeDtypeStruct((tokens, D_MODEL), bf16),
            jax.ShapeDtypeStruct((tokens, LANES), f32),
        ),
        scratch_shapes=[
            pltpu.VMEM((N_HEADS, s_len, HEAD_W), bf16),
            pltpu.VMEM((N_HEADS, s_len, LANES), bf16),
            pltpu.VMEM((N_HEADS, tq, HEAD_W), bf16),
        ],
        compiler_params=_params("parallel", "arbitrary"),
        name="attn_out",
    )(q2d, ckv3, kpe3, cm2d, g2d, x2d, m3, wk, wv, wo, wout, norm2, wr)


def _route_kernel(lg_ref, pos_ref, aff_ref, *, nseq, n, cap):
    for s in range(nseq):
        lt = lg_ref[s].T[0:N_EXPERTS, :]
        e = jnp.exp(lt - jnp.max(lt, axis=0, keepdims=True))
        aff_ref[s * N_EXPERTS:(s + 1) * N_EXPERTS, :] = e / jnp.sum(e, axis=0, keepdims=True)
    rows = nseq * N_EXPERTS
    capf = float(cap)

    def bit_step(i, t):
        cand = t | (jnp.int32(1) << (30 - i))
        thr = lax.bitcast_convert_type(cand, f32)
        cnt = jnp.sum(jnp.where(aff_ref[...] >= thr, 1.0, 0.0), axis=1, keepdims=True)
        return jnp.where(cnt >= capf, cand, t)

    t = lax.fori_loop(0, 31, bit_step, jnp.zeros((rows, 1), jnp.int32))
    thr = lax.bitcast_convert_type(t, f32)
    need = capf - jnp.sum(jnp.where(aff_ref[...] > thr, 1.0, 0.0), axis=1, keepdims=True)

    blk = 2 * LANES
    tri = jnp.where(
        lax.broadcasted_iota(jnp.int32, (blk, blk), 0) < lax.broadcasted_iota(jnp.int32, (blk, blk), 1),
        1.0, 0.0).astype(bf16)
    carry_gt = jnp.zeros((rows, 1), f32)
    carry_eq = jnp.zeros((rows, 1), f32)
    for b in range(n // blk):
        sl = slice(b * blk, (b + 1) * blk)
        ab = aff_ref[:, sl]
        gt = ab > thr
        eq = ab == thr
        gtb = jnp.where(gt, 1.0, 0.0)
        eqb = jnp.where(eq, 1.0, 0.0)
        pre_gt = _dot(gtb.astype(bf16), tri) + carry_gt
        pre_eq = _dot(eqb.astype(bf16), tri) + carry_eq
        carry_gt = carry_gt + jnp.sum(gtb, axis=1, keepdims=True)
        carry_eq = carry_eq + jnp.sum(eqb, axis=1, keepdims=True)
        sel = gt | (eq & (pre_eq < need))
        slot = pre_gt + jnp.minimum(pre_eq, need)
        pos_ref[:, sl] = jnp.where(sel, slot, -1.0).astype(jnp.int32)


def _route(lg3, *, cap):
    nseq, n, _ = lg3.shape
    rows = nseq * N_EXPERTS
    return pl.pallas_call(
        functools.partial(_route_kernel, nseq=nseq, n=n, cap=cap),
        grid=(1,),
        in_specs=[_const_spec(lg3.shape)],
        out_specs=(_const_spec((rows, n)), _const_spec((rows, n))),
        out_shape=(jax.ShapeDtypeStruct((rows, n), jnp.int32), jax.ShapeDtypeStruct((rows, n), f32)),
        compiler_params=_params("arbitrary"),
        name="route",
    )(lg3)


def _slot_hits(pos_ref, e, cap):
    width = pos_ref.shape[1]
    return lax.broadcasted_iota(jnp.int32, (cap, width), 0) == pos_ref[e:e + 1, :]


def _one_hot(hits):
    return jnp.concatenate([jnp.where(h, 1.0, 0.0).astype(bf16) for h in hits], axis=0)


def _gather_kernel(pos_ref, aff_ref, h2_ref, xg_ref, vals_ref, *, cap, ne, sb):
    for g in range(sb):
        rows = slice(g * cap, (g + 1) * cap)
        for e0 in range(0, N_EXPERTS, ne):
            hits = [_slot_hits(pos_ref.at[g], e0 + e, cap) for e in range(ne)]
            xg = _dot(_one_hot(hits), h2_ref[g]).astype(xg_ref.dtype)
            for e in range(ne):
                xg_ref[e0 + e, rows, :] = xg[e * cap:(e + 1) * cap, :]
                vals = jnp.sum(jnp.where(hits[e], aff_ref[g, e0 + e:e0 + e + 1, :], 0.0), axis=1, keepdims=True)
                vals_ref[e0 + e, rows, :] = jnp.broadcast_to(vals, (cap, LANES))


def _gather(pos3, aff3, h23, *, cap, ne, sb):
    nseq, _, n = pos3.shape
    return pl.pallas_call(
        functools.partial(_gather_kernel, cap=cap, ne=ne, sb=sb),
        grid=(nseq // sb,),
        in_specs=[
            pl.BlockSpec((sb, N_EXPERTS, n), lambda s: (s, 0, 0)),
            pl.BlockSpec((sb, N_EXPERTS, n), lambda s: (s, 0, 0)),
            pl.BlockSpec((sb, n, D_MODEL), lambda s: (s, 0, 0)),
        ],
        out_specs=(
            pl.BlockSpec((N_EXPERTS, sb * cap, D_MODEL), lambda s: (0, s, 0)),
            pl.BlockSpec((N_EXPERTS, sb * cap, LANES), lambda s: (0, s, 0)),
        ),
        out_shape=(
            jax.ShapeDtypeStruct((N_EXPERTS, nseq * cap, D_MODEL), bf16),
            jax.ShapeDtypeStruct((N_EXPERTS, nseq * cap, LANES), f32),
        ),
        compiler_params=_params("parallel"),
        name="gather",
    )(pos3, aff3, h23)


def _experts_kernel(xp_ref, vp_ref, xs_ref, vs_ref, wg_ref, wu_ref, wd_ref, yp_ref, ys_ref, *, rc):
    wg = wg_ref[...].astype(bf16)
    wu = wu_ref[...].astype(bf16)
    wd = wd_ref[...].astype(bf16)
    for x_ref, v_ref, y_ref in ((xp_ref, vp_ref, yp_ref), (xs_ref, vs_ref, ys_ref)):
        for r0 in range(0, x_ref.shape[0], rc):
            x = x_ref[r0:r0 + rc, :]
            a = _dot(x, wg)
            u = _dot(x, wu)
            hm = (a * jax.nn.sigmoid(a) * u).astype(bf16)
            y = _dot(hm, wd) * v_ref[r0:r0 + rc, 0:1]
            y_ref[r0:r0 + rc, :] = y.astype(y_ref.dtype)


def _experts(xg_p, vals_p, xg_s, vals_s, wg, wu, wd, *, rc=512):
    rp = xg_p.shape[1]
    rs = xg_s.shape[1]
    per_e = lambda r, w: pl.BlockSpec((None, r, w), lambda e: (e, 0, 0))
    return pl.pallas_call(
        functools.partial(_experts_kernel, rc=rc),
        grid=(N_EXPERTS,),
        in_specs=[
            per_e(rp, D_MODEL), per_e(rp, LANES), per_e(rs, D_MODEL), per_e(rs, LANES),
            per_e(D_MODEL, EXPERT_FF), per_e(D_MODEL, EXPERT_FF), per_e(EXPERT_FF, D_MODEL),
        ],
        out_specs=(per_e(rp, D_MODEL), per_e(rs, D_MODEL)),
        out_shape=(
            jax.ShapeDtypeStruct((N_EXPERTS, rp, D_MODEL), bf16),
            jax.ShapeDtypeStruct((N_EXPERTS, rs, D_MODEL), bf16),
        ),
        compiler_params=_params("parallel"),
        name="experts",
    )(xg_p, vals_p, xg_s, vals_s, wg, wu, wd)


def _scatter_kernel(pos_ref, y_ref, x1_ref, m_ref, fn_ref, o_ref, *, tn, cap, ne, sb):
    g2 = m_ref[5:6, :]
    for g in range(sb):
        rows = slice(g * cap, (g + 1) * cap)
        moe = jnp.zeros((tn, D_MODEL), f32)
        for e0 in range(0, N_EXPERTS, ne):
            onehot = _one_hot([_slot_hits(pos_ref.at[g], e0 + e, cap) for e in range(ne)])
            y = jnp.concatenate([y_ref[e0 + e, rows, :] for e in range(ne)], axis=0)
            moe = moe + lax.dot_general(onehot, y, (((0,), (0,)), ((), ())), preferred_element_type=f32)
        xo = x1_ref[g] + g2 * moe
        o_ref[g] = _rms(xo, fn_ref[...])


def _scatter(pos3, y, x13, m3, fn, *, cap, ne, tn, sb):
    nseq, _, n = pos3.shape
    assert sb == 1 or m3.shape[0] == 1, "sequences sharing a grid step must share their modulation rows"
    return pl.pallas_call(
        functools.partial(_scatter_kernel, tn=tn, cap=cap, ne=ne, sb=sb),
        grid=(nseq // sb, n // tn),
        in_specs=[
            pl.BlockSpec((sb, N_EXPERTS, tn), lambda s, i: (s, 0, i)),
            pl.BlockSpec((N_EXPERTS, sb * cap, D_MODEL), lambda s, i: (0, s, 0)),
            pl.BlockSpec((sb, tn, D_MODEL), lambda s, i: (s, i, 0)),
            _mod_spec(m3, lambda s, i: s),
            _const_spec((1, D_MODEL)),
        ],
        out_specs=pl.BlockSpec((sb, tn, D_MODEL), lambda s, i: (s, i, 0)),
        out_shape=jax.ShapeDtypeStruct((nseq, n, D_MODEL), f32),
        compiler_params=_params("parallel", "arbitrary"),
        name="scatter",
    )(pos3, y, x13, m3, fn)


def _rope_tables(n):
    t = np.arange(n)
    half = QK_ROPE // 2
    freqs = ROPE_BASE ** (-np.arange(0, half, 2, dtype=np.float64) / half)
    ang_r = (t // GRID_W)[:, None] * freqs
    ang_c = (t % GRID_W)[:, None] * freqs
    cr, sr, cc, sc = np.cos(ang_r), np.sin(ang_r), np.cos(ang_c), np.sin(ang_c)
    cos = np.ones((n, HEAD_W))
    sin = np.zeros((n, HEAD_W))
    cos[:, ROPE_OFF:ROPE_OFF + QK_ROPE] = np.concatenate([cr, cr, cc, cc], axis=-1)
    sin[:, ROPE_OFF:ROPE_OFF + QK_ROPE] = np.concatenate([-sr, sr, -sc, sc], axis=-1)
    return jnp.asarray(cos, f32), jnp.asarray(sin, f32)


_PARTNER = np.concatenate([np.arange(8, 16), np.arange(0, 8), np.arange(24, 32), np.arange(16, 24)])


def _rope_partner(w):
    q = QK_ROPE // 4
    return jnp.concatenate([w[..., q:2 * q], w[..., 0:q], w[..., 3 * q:4 * q], w[..., 2 * q:3 * q]], axis=-1)


def _rope_placement():
    place = np.zeros((LANES, 2 * LANES), np.float32)
    d = np.arange(QK_ROPE)
    place[d, ROPE_OFF + d] = 1.0
    place[_PARTNER, LANES + ROPE_OFF + d] = 1.0
    return jnp.asarray(place, bf16)


def _head_blocks(w_nope, w_rope):
    rows = w_nope.shape[0]
    if w_rope is None:
        w_rope = jnp.zeros((rows, N_HEADS, QK_ROPE), w_nope.dtype)
    z = jnp.zeros((rows, N_HEADS, HEAD_W - QK_NOPE - QK_ROPE), w_nope.dtype)
    return jnp.concatenate([w_nope, w_rope, z], axis=-1).reshape(rows, N_HEADS * HEAD_W)


def kernel(x_prompt, x_sample, cache_ckv, cache_kpe, c, c_ctx, w_ada, b_ada, norm1, w_in, conv_dw, conv_dw_b,
           conv_ln_g, conv_ln_b, w_conv_out, q_norm, w_qb, kv_norm, w_kvb, w_o_mla, w_out, norm2, w_router,
           w_e_gate, w_e_up, w_e_down, final_norm):
    assert w_ada.shape[0] == 1, "single trunk layer"
    nb_p, n_p, _ = x_prompt.shape
    nb_s, n_s, _ = x_sample.shape

    win = _wprep(w_in[0].T, _rope_placement())
    wq = w_qb[0].reshape(Q_LORA, N_HEADS, QK_NOPE + QK_ROPE)
    wqb = _head_blocks(wq[..., :QK_NOPE], wq[..., QK_NOPE:]).astype(bf16)
    wqbp = _head_blocks(jnp.zeros_like(wq[..., :QK_NOPE]), _rope_partner(wq[..., QK_NOPE:])).astype(bf16)
    wkv = w_kvb[0].reshape(KV_LORA, N_HEADS, QK_NOPE + V_DIM)
    wk = _head_blocks(wkv[..., :QK_NOPE], None).astype(bf16)
    wv = wkv[..., QK_NOPE:].reshape(KV_LORA, N_HEADS * V_DIM).astype(bf16)
    wco = w_conv_out[0].astype(bf16)
    wo = w_o_mla[0].astype(bf16)
    wout = w_out[0].astype(bf16)
    wr_hi = w_router[0].astype(bf16)
    wr_lo = (w_router[0] - wr_hi.astype(f32)).astype(bf16)
    wr = jnp.concatenate([wr_hi, wr_lo, jnp.zeros((D_MODEL, LANES - 2 * N_EXPERTS), bf16)], axis=-1)
    row = lambda a: a.reshape(1, -1)

    mod = jnp.concatenate([c_ctx[None, :], c, jnp.zeros((8 - 1 - nb_s, D_MODEL), f32)], axis=0)
    m = _ada(mod, w_ada[0], b_ada[0]).reshape(8, 6, D_MODEL)
    m_p, m_s = m[0:1], m[1:1 + nb_s]
    cos, sin = _rope_tables(n_s)

    conv_w = _conv_weights(conv_dw[0], row(conv_dw_b[0]), row(conv_ln_g[0]), row(conv_ln_b[0]), wco, CONV_CHUNK)

    def mixers(x, m3, rope, ctx_ckv, ctx_kpe):
        nseq, n, _ = x.shape
        x2d = x.reshape(nseq * n, D_MODEL)
        fuse_conv = IN_TILE % n == 0
        cm, q, ckv, kr, g, *kpe = _inproj(x2d, m3, row(norm1[0]), win, row(q_norm[0]), wqb, wqbp, row(kv_norm[0]),
                                          cos, sin, conv_w if fuse_conv else None, n=n, rope=rope, tm=IN_TILE,
                                          ct=CONV_CHUNK)
        if not fuse_conv:
            cm = _conv(cm.reshape(nseq, n, CONV_DIM), g.reshape(nseq, n, 2 * D_MODEL), conv_w, ct=CONV_CHUNK)
        keys_ckv = ckv.reshape(nseq, n, KV_LORA)
        keys_kpe = kr.reshape(nseq, n, LANES)
        if ctx_ckv is not None:
            keys_ckv = jnp.concatenate([ctx_ckv.astype(keys_ckv.dtype), keys_ckv], axis=1)
            keys_kpe = jnp.concatenate([ctx_kpe.astype(keys_kpe.dtype), keys_kpe], axis=1)
        x1, h2, lg = _attn(q, keys_ckv, keys_kpe, cm.reshape(nseq * n, D_MODEL), g, x2d, m3, wk, wv, wo, wout,
                           row(norm2[0]), wr, n=n, tq=min(n, Q_TILE))
        return x1, h2, lg, ckv, kpe

    ctx_kpe = jnp.pad(cache_kpe[:, 0], ((0, 0), (0, 0), (ROPE_OFF, LANES - ROPE_OFF - QK_ROPE)))
    x1_p, h2_p, lg_p, ckv_p, (kpe_p,) = mixers(x_prompt, m_p, False, None, None)
    x1_s, h2_s, lg_s, _, _ = mixers(x_sample, m_s, True, cache_ckv[:, 0], ctx_kpe)

    def moe_tiles(n):
        cap = EC_FACTOR * n // N_EXPERTS
        ne = N_EXPERTS if N_EXPERTS * cap <= MOE_ROWS else 1
        sb = max(1, MOE_TOKENS // n)
        return cap, ne, sb

    def route_gather(h2, lg, nseq, n):
        cap, ne, sb = moe_tiles(n)
        pos, aff = _route(lg.reshape(nseq, n, LANES), cap=cap)
        pos3 = pos.reshape(nseq, N_EXPERTS, n)
        xg, vals = _gather(pos3, aff.reshape(nseq, N_EXPERTS, n), h2.reshape(nseq, n, D_MODEL),
                           cap=cap, ne=ne, sb=sb)
        return pos3, xg, vals

    pos_p, xg_p, vals_p = route_gather(h2_p, lg_p, nb_p, n_p)
    pos_s, xg_s, vals_s = route_gather(h2_s, lg_s, nb_s, n_s)
    y_p, y_s = _experts(xg_p, vals_p, xg_s, vals_s, w_e_gate[0], w_e_up[0], w_e_down[0])
    fn = row(final_norm)

    def scatter(pos, y, x1, m3, nseq, n):
        cap, ne, sb = moe_tiles(n)
        return _scatter(pos, y, x1.reshape(nseq, n, D_MODEL), m3, fn, cap=cap, ne=ne, tn=min(n, SCATTER_TILE), sb=sb)

    y_prompt = scatter(pos_p, y_p, x1_p, m_p, nb_p, n_p)
    y_sample = scatter(pos_s, y_s, x1_s, m_s, nb_s, n_s)

    new_ckv = ckv_p.reshape(nb_p, 1, n_p, KV_LORA)
    new_kpe = kpe_p.reshape(nb_p, 1, n_p, QK_ROPE)
    return (y_prompt, y_sample, new_ckv, new_kpe)
```

```python
import functools

import jax
import jax.numpy as jnp
import numpy as np
from jax import lax
from jax.experimental import pallas as pl
from jax.experimental.pallas import tpu as pltpu

D_MODEL = 1024
GRID_W = 64
CONV_DIM = 512
CONV_WIDTH = 31
N_HEADS = 8
QK_NOPE = 64
QK_ROPE = 32
V_DIM = 64
Q_LORA = 256
KV_LORA = 128
N_EXPERTS = 16
EXPERT_FF = 512
EC_FACTOR = 2
ROPE_BASE = 10000.0
EPS = 1e-6

LANES = 128
HEAD_W = LANES
ROPE_OFF = QK_NOPE
CONV_HALO = 16
LOG2E = 1.4426950408889634
VMEM_LIMIT = 48 * 1024 * 1024
IN_TILE = 512
CONV_CHUNK = 64
Q_TILE = 512
SCORE_BYTES = 12 * 1024 * 1024
MOE_ROWS = 512
MOE_TOKENS = 1024
SCATTER_TILE = 512
MOE_BLOCK = 2 * LANES
MOE_WIN = 80
WIN_ALIGN = 16

C_CONV = 0
C_QA = 2 * CONV_DIM
C_KVA = C_QA + Q_LORA
C_KR = C_KVA + KV_LORA
C_GATE = C_KR + LANES
C_END = C_GATE + 2 * D_MODEL
C_KRP = C_END

f32 = jnp.float32
bf16 = jnp.bfloat16


def _params(*sem):
    return pltpu.CompilerParams(dimension_semantics=sem, vmem_limit_bytes=VMEM_LIMIT)


def _dot(a, b):
    return jnp.dot(a, b, preferred_element_type=f32)


def _rms(x, g):
    return x * lax.rsqrt(jnp.mean(x * x, axis=-1, keepdims=True) + EPS) * g


def _const_spec(shape):
    nd = len(shape)
    return pl.BlockSpec(shape, lambda *_: (0,) * nd)


def _mod_spec(m3, seq_of):
    if m3.shape[0] == 1:
        return _const_spec((None, 6, D_MODEL))
    return pl.BlockSpec((None, 6, D_MODEL), lambda *idx: (seq_of(*idx), 0, 0))


def _ada_kernel(s_ref, w_ref, b_ref, o_ref):
    s = s_ref[...]
    s = s * jax.nn.sigmoid(s)
    o_ref[...] = _dot(s.astype(bf16), w_ref[...].astype(bf16)) + b_ref[...]


def _ada(mod, w_ada, b_ada):
    rows = mod.shape[0]
    n_out = w_ada.shape[1]
    tn = D_MODEL
    return pl.pallas_call(
        _ada_kernel,
        grid=(n_out // tn,),
        in_specs=[
            _const_spec((rows, D_MODEL)),
            pl.BlockSpec((D_MODEL, tn), lambda j: (0, j)),
            pl.BlockSpec((1, tn), lambda j: (0, j)),
        ],
        out_specs=pl.BlockSpec((rows, tn), lambda j: (0, j)),
        out_shape=jax.ShapeDtypeStruct((rows, n_out), f32),
        compiler_params=_params("arbitrary"),
        name="ada",
    )(mod, w_ada, b_ada.reshape(1, n_out))


def _wprep_kernel(wt_ref, place_ref, o_ref):
    def block(r0):
        return wt_ref[r0:r0 + LANES, :].T.astype(bf16)

    for j in range(C_KR // LANES):
        o_ref[:, j * LANES:(j + 1) * LANES] = block(j * LANES)
    placed = _dot(block(C_KR), place_ref[...])
    o_ref[:, C_KR:C_GATE] = placed[:, 0:LANES].astype(bf16)
    o_ref[:, C_KRP:C_KRP + LANES] = placed[:, LANES:2 * LANES].astype(bf16)
    for j in range(2 * D_MODEL // LANES):
        o_ref[:, C_GATE + j * LANES:C_GATE + (j + 1) * LANES] = block(C_KR + QK_ROPE + j * LANES)


def _wprep(w_in_t, place, *, tr=LANES):
    cols, rows = w_in_t.shape
    return pl.pallas_call(
        _wprep_kernel,
        grid=(rows // tr,),
        in_specs=[pl.BlockSpec((cols, tr), lambda i: (0, i)), _const_spec(place.shape)],
        out_specs=pl.BlockSpec((tr, C_KRP + LANES), lambda i: (i, 0)),
        out_shape=jax.ShapeDtypeStruct((rows, C_KRP + LANES), bf16),
        compiler_params=_params("parallel"),
        name="wprep",
    )(w_in_t, place)


def _conv_taps(vpad, r0, shift_ref, dw_ref, dwb_ref, ybuf, y0, ct):
    pad = CONV_WIDTH // 2
    sub = 8
    span = ((CONV_HALO - pad + CONV_WIDTH - 1) // sub) * sub
    cw = 2 * LANES
    for cb in range(CONV_DIM // cw):
        sl = slice(cb * cw, (cb + 1) * cw)
        win = vpad[pl.ds(r0, ct + 2 * CONV_HALO), sl]
        acc = jnp.zeros((ct // sub, sub, cw), f32)
        for ph in range(sub):
            wph = _dot(shift_ref[ph], win)
            for a in range(span // sub + 1):
                k = a * sub + ph - (CONV_HALO - pad)
                if 0 <= k < CONV_WIDTH:
                    acc = acc + wph[a * sub:a * sub + ct, :].reshape(ct // sub, sub, cw) * dw_ref[k, :, sl][None]
        ybuf[pl.ds(y0, ct), sl] = acc.reshape(ct, cw) + dwb_ref[:, sl]


def _conv_out(y, lng_ref, lnb_ref, wco_ref, gate):
    mu = jnp.mean(y, axis=-1, keepdims=True)
    yc = y - mu
    var = jnp.mean(yc * yc, axis=-1, keepdims=True)
    z = yc * lax.rsqrt(var + EPS) * lng_ref[...] + lnb_ref[...]
    z = z * jax.nn.sigmoid(z)
    return (gate.astype(f32) * _dot(z.astype(bf16), wco_ref[...])).astype(bf16)


def _fill_padded(vpad, v, n):
    zeros = jnp.zeros((CONV_HALO, CONV_DIM), vpad.dtype)
    vpad[0:CONV_HALO, :] = zeros
    vpad[CONV_HALO + n:2 * CONV_HALO + n, :] = zeros
    vpad[CONV_HALO:CONV_HALO + n, :] = v


def _inproj_kernel(*refs, rope, fuse_conv, q_scale, n, rt, ct):
    refs = list(refs)
    x_ref, m_ref, n1_ref, win_ref, qn_ref, wqb_ref = refs[:6]
    del refs[:6]
    if rope:
        wqbp_ref = refs.pop(0)
    kvn_ref = refs.pop(0)
    if rope:
        cos_ref, sin_ref = refs[:2]
        del refs[:2]
    if fuse_conv:
        shift_ref, dw_ref, dwb_ref, lng_ref, lnb_ref, wco_ref = refs[:6]
        del refs[:6]
    vc_ref, q_ref, ckv_ref, kr_ref, g_ref = refs[:5]
    del refs[:5]
    if not rope:
        kpe_ref = refs.pop(0)
    if fuse_conv:
        vpad, ybuf, gc = refs
    tm = x_ref.shape[0]

    sh1 = m_ref[0:1, :]
    sc1 = m_ref[1:2, :]
    h = _rms(x_ref[...], n1_ref[...]) * (1.0 + sc1) + sh1
    hb = h.astype(bf16)

    left = _dot(hb, win_ref[:, 0:C_GATE])
    v = (left[:, C_CONV:C_CONV + CONV_DIM] * jax.nn.sigmoid(left[:, C_CONV + CONV_DIM:C_QA])).astype(bf16)
    if fuse_conv:
        for s in range(tm // n):
            _fill_padded(vpad.at[s], v[s * n:(s + 1) * n, :], n)
    else:
        vc_ref[...] = v

    chunks = []
    if fuse_conv:
        chunks = [(s, c) for s in range(tm // n) for c in range(n // ct)]

    def conv_some(count):
        for _ in range(min(count, len(chunks))):
            s, c = chunks.pop(0)
            _conv_taps(vpad.at[s], c * ct, shift_ref, dw_ref, dwb_ref, ybuf, s * n + c * ct, ct)

    steps = 2 + 2 * D_MODEL // 512
    per_step = -(-len(chunks) // steps)

    qa = left[:, C_QA:C_KVA]
    qn = _rms(qa, qn_ref[...]).astype(bf16)
    q = _dot(qn, wqb_ref[...])
    if rope:
        qp = _dot(qn, wqbp_ref[...])
        cos = cos_ref[...]
        sin = sin_ref[...]
        for hd in range(N_HEADS):
            sl = slice(hd * HEAD_W, (hd + 1) * HEAD_W)
            q_ref[:, sl] = ((q[:, sl] * cos + qp[:, sl] * sin) * q_scale).astype(q_ref.dtype)
    else:
        q_ref[...] = (q * q_scale).astype(q_ref.dtype)
    conv_some(per_step)

    kva = left[:, C_KVA:C_KR]
    ckv_ref[...] = _rms(kva, kvn_ref[...]).astype(ckv_ref.dtype)

    kr = left[:, C_KR:C_GATE]
    if rope:
        krp = _dot(hb, win_ref[:, C_KRP:C_KRP + LANES])
        kr = kr * cos_ref[...] + krp * sin_ref[...]
    else:
        kpe_ref[...] = kr[:, ROPE_OFF:ROPE_OFF + QK_ROPE]
    kr_ref[...] = kr.astype(kr_ref.dtype)
    conv_some(per_step)

    gw = 512
    for j in range(2 * D_MODEL // gw):
        gate = jax.nn.sigmoid(_dot(hb, win_ref[:, C_GATE + j * gw:C_GATE + (j + 1) * gw])).astype(bf16)
        if not fuse_conv:
            g_ref[:, j * gw:(j + 1) * gw] = gate
        elif j * gw < D_MODEL:
            gc[:, j * gw:(j + 1) * gw] = gate
        else:
            g_ref[:, j * gw - D_MODEL:(j + 1) * gw - D_MODEL] = gate
        conv_some(per_step)

    if fuse_conv:
        conv_some(len(chunks))
        for r0 in range(0, tm, rt):
            vc_ref[r0:r0 + rt, :] = _conv_out(ybuf[r0:r0 + rt, :], lng_ref, lnb_ref, wco_ref, gc[r0:r0 + rt, :])


def _inproj(x2d, m3, norm1, win, q_norm, wqb, wqbp, kv_norm, cos, sin, conv_w, *, n, rope, tm, rt=256, ct=64):
    tokens = x2d.shape[0]
    fuse_conv = conv_w is not None
    assert n % tm == 0 or (tm % n == 0 and m3.shape[0] == 1 and not rope)
    assert not fuse_conv or tm % n == 0
    tiles_per_seq = max(1, n // tm)
    q_scale = float((QK_NOPE + QK_ROPE) ** -0.5 * LOG2E)
    tile = lambda w: pl.BlockSpec((tm, w), lambda i: (i, 0))
    in_specs = [
        tile(D_MODEL),
        _mod_spec(m3, lambda i: i // tiles_per_seq),
        _const_spec((1, D_MODEL)),
        _const_spec((D_MODEL, C_KRP + LANES if rope else C_END)),
        _const_spec((1, Q_LORA)),
        _const_spec(wqb.shape),
    ]
    args = [x2d, m3, norm1, win, q_norm, wqb]
    if rope:
        in_specs.append(_const_spec(wqbp.shape))
        args.append(wqbp)
    in_specs.append(_const_spec((1, KV_LORA)))
    args.append(kv_norm)
    if rope:
        tab = pl.BlockSpec((tm, LANES), lambda i: (i % tiles_per_seq, 0))
        in_specs += [tab, tab]
        args += [cos, sin]
    scratch = []
    if fuse_conv:
        args += list(conv_w)
        in_specs += [_const_spec(w.shape) for w in conv_w]
        scratch = [
            pltpu.VMEM((tm // n, n + 2 * CONV_HALO, CONV_DIM), bf16),
            pltpu.VMEM((tm, CONV_DIM), f32),
            pltpu.VMEM((tm, D_MODEL), bf16),
        ]
    gate_w = D_MODEL if fuse_conv else 2 * D_MODEL
    vc_w = D_MODEL if fuse_conv else CONV_DIM
    out_shape = [
        jax.ShapeDtypeStruct((tokens, vc_w), bf16),
        jax.ShapeDtypeStruct((tokens, N_HEADS * HEAD_W), bf16),
        jax.ShapeDtypeStruct((tokens, KV_LORA), bf16 if rope else f32),
        jax.ShapeDtypeStruct((tokens, LANES), bf16),
        jax.ShapeDtypeStruct((tokens, gate_w), bf16),
    ]
    out_specs = [tile(vc_w), tile(N_HEADS * HEAD_W), tile(KV_LORA), tile(LANES), tile(gate_w)]
    if not rope:
        out_shape.append(jax.ShapeDtypeStruct((tokens, QK_ROPE), f32))
        out_specs.append(tile(QK_ROPE))
    return pl.pallas_call(
        functools.partial(_inproj_kernel, rope=rope, fuse_conv=fuse_conv, q_scale=q_scale, n=n, rt=rt, ct=ct),
        grid=(tokens // tm,),
        in_specs=in_specs,
        out_specs=out_specs,
        out_shape=out_shape,
        scratch_shapes=scratch,
        compiler_params=_params("parallel"),
        name="inproj_rope" if rope else "inproj_conv",
    )(*args)


def _conv_kernel(v_ref, g_ref, shift_ref, dw_ref, dwb_ref, lng_ref, lnb_ref, wco_ref, o_ref, vpad, ybuf,
                 *, n, rt, ct):
    _fill_padded(vpad, v_ref[...], n)

    def conv_chunk(c, carry):
        r0 = pl.multiple_of(c * ct, ct)
        _conv_taps(vpad, r0, shift_ref, dw_ref, dwb_ref, ybuf, r0, ct)
        return carry

    lax.fori_loop(0, n // ct, conv_chunk, 0, unroll=4)

    def chunk(c, carry):
        r0 = pl.multiple_of(c * rt, rt)
        o_ref[pl.ds(r0, rt), :] = _conv_out(ybuf[pl.ds(r0, rt), :], lng_ref, lnb_ref, wco_ref, g_ref[pl.ds(r0, rt), :])
        return carry

    lax.fori_loop(0, n // rt, chunk, 0)


def _conv_weights(dw, dwb, lng, lnb, wco, ct):
    rows = ct + 2 * CONV_HALO
    i = np.arange(rows)
    shifts = jnp.asarray(np.stack([(i[None, :] == i[:, None] + ph) for ph in range(8)]), bf16)
    dw_tiles = jnp.broadcast_to(dw[:, None, :], (CONV_WIDTH, 8, CONV_DIM))
    return shifts, dw_tiles, dwb, lng, lnb, wco


def _conv(v3, g3, conv_w, *, rt=256, ct=64):
    nseq, n, _ = v3.shape
    assert v3.dtype == bf16, "the one-hot row shifts are exact only for bf16 windows"
    return pl.pallas_call(
        functools.partial(_conv_kernel, n=n, rt=rt, ct=ct),
        grid=(nseq,),
        in_specs=[
            pl.BlockSpec((None, n, CONV_DIM), lambda s: (s, 0, 0)),
            pl.BlockSpec((None, n, D_MODEL), lambda s: (s, 0, 0)),
        ] + [_const_spec(w.shape) for w in conv_w],
        out_specs=pl.BlockSpec((None, n, D_MODEL), lambda s: (s, 0, 0)),
        out_shape=jax.ShapeDtypeStruct((nseq, n, D_MODEL), bf16),
        scratch_shapes=[
            pltpu.VMEM((n + 2 * CONV_HALO, CONV_DIM), bf16),
            pltpu.VMEM((n, CONV_DIM), f32),
        ],
        compiler_params=_params("parallel"),
        name="conv",
    )(v3, g3, *conv_w)


def _attn_kernel(q_ref, ckv_ref, kpe_ref, cm_ref, g_ref, x_ref, m_ref, wk_ref, wv_ref, wo_ref, wout_ref, n2_ref,
                 wr_ref, x1_ref, h2_ref, lg_ref, k_scr, v_scr, q_scr, *, hg):
    @pl.when(pl.program_id(1) == 0)
    def _():
        ckv = ckv_ref[...].astype(bf16)
        kpe = kpe_ref[...].astype(f32)
        k = _dot(ckv, wk_ref[...])
        v = _dot(ckv, wv_ref[...])
        ones = jnp.ones((v.shape[0], V_DIM), bf16)
        for hd in range(N_HEADS):
            k_scr[hd] = (k[:, hd * HEAD_W:(hd + 1) * HEAD_W] + kpe).astype(bf16)
            vh = v[:, hd * V_DIM:(hd + 1) * V_DIM].astype(bf16)
            if hd % 2 == 0:
                v_scr[hd, :, 0:V_DIM] = vh
                v_scr[hd, :, V_DIM:LANES] = ones
            else:
                v_scr[hd, :, 0:V_DIM] = ones
                v_scr[hd, :, V_DIM:LANES] = vh

    tq = q_ref.shape[0]
    for hd in range(N_HEADS):
        q_scr[hd] = q_ref[:, hd * HEAD_W:(hd + 1) * HEAD_W]
    lane = lax.broadcasted_iota(jnp.int32, (tq, LANES), 1)
    pairs = []
    for h0 in range(0, N_HEADS, hg):
        hs = slice(h0, h0 + hg)
        s = lax.dot_general(q_scr[hs], k_scr[hs], (((2,), (2,)), ((0,), (0,))), preferred_element_type=f32)
        mx = jnp.max(s, axis=-1, keepdims=True)
        p = jnp.exp2(s - mx).astype(bf16)
        r = lax.dot_general(p, v_scr[hs], (((2,), (1,)), ((0,), (0,))), preferred_element_type=f32)
        for j in range(hg // 2):
            re, ro = r[2 * j], r[2 * j + 1]
            oe = re * (1.0 / re[:, V_DIM:V_DIM + 1])
            oo = ro * (1.0 / ro[:, 0:1])
            pairs.append(jnp.where(lane < V_DIM, oe, oo))
    attn = jnp.concatenate(pairs, axis=-1).astype(bf16)
    merged = (cm_ref[...].astype(f32) + g_ref[...].astype(f32) * _dot(attn, wo_ref[...])).astype(bf16)

    g1 = m_ref[2:3, :]
    sh2 = m_ref[3:4, :]
    sc2 = m_ref[4:5, :]
    x1 = x_ref[...] + g1 * _dot(merged, wout_ref[...])
    x1_ref[...] = x1
    h2 = _rms(x1, n2_ref[...]) * (1.0 + sc2) + sh2
    hi = h2.astype(bf16)
    h2_ref[...] = hi
    lo = (h2 - hi.astype(f32)).astype(bf16)
    both = _dot(jnp.concatenate([hi, lo], axis=0), wr_ref[...])
    s = both[0:tq] + both[tq:2 * tq]
    lg_ref[...] = s + pltpu.roll(s, LANES - N_EXPERTS, axis=1)


def _attn(q2d, ckv3, kpe3, cm2d, g2d, x2d, m3, wk, wv, wo, wout, norm2, wr, *, n, tq):
    nseq, s_len, _ = ckv3.shape
    hg = max(2, min(N_HEADS, SCORE_BYTES // (tq * s_len * 4)))
    assert N_HEADS % hg == 0 and hg % 2 == 0
    gate_col = g2d.shape[1] // D_MODEL - 1
    qb = n // tq
    tokens = nseq * n
    tile = lambda w, c=0: pl.BlockSpec((tq, w), lambda s, i: (s * qb + i, c))
    return pl.pallas_call(
        functools.partial(_attn_kernel, hg=hg),
        grid=(nseq, qb),
        in_specs=[
            tile(N_HEADS * HEAD_W),
            pl.BlockSpec((None, s_len, KV_LORA), lambda s, i: (s, 0, 0)),
            pl.BlockSpec((None, s_len, LANES), lambda s, i: (s, 0, 0)),
            tile(D_MODEL),
            tile(D_MODEL, gate_col),
            tile(D_MODEL),
            _mod_spec(m3, lambda s, i: s),
            _const_spec(wk.shape),
            _const_spec(wv.shape),
            _const_spec(wo.shape),
            _const_spec(wout.shape),
            _const_spec((1, D_MODEL)),
            _const_spec(wr.shape),
        ],
        out_specs=(tile(D_MODEL), tile(D_MODEL), tile(LANES)),
        out_shape=(
            jax.ShapeDtypeStruct((tokens, D_MODEL), f32),
            jax.ShapeDtypeStruct((tokens, D_MODEL), bf16),
            jax.ShapeDtypeStruct((tokens, LANES), f32),
        ),
        scratch_shapes=[
            pltpu.VMEM((N_HEADS, s_len, HEAD_W), bf16),
            pltpu.VMEM((N_HEADS, s_len, LANES), bf16),
            pltpu.VMEM((N_HEADS, tq, HEAD_W), bf16),
        ],
        compiler_params=_params("parallel", "arbitrary"),
        name="attn_out",
    )(q2d, ckv3, kpe3, cm2d, g2d, x2d, m3, wk, wv, wo, wout, norm2, wr)


def _route_kernel(lg_ref, pos_ref, aff_ref, meta_ref, *, nseq, n, cap):
    for s in range(nseq):
        lt = lg_ref[s].T[0:N_EXPERTS, :]
        e = jnp.exp(lt - jnp.max(lt, axis=0, keepdims=True))
        aff_ref[s * N_EXPERTS:(s + 1) * N_EXPERTS, :] = e / jnp.sum(e, axis=0, keepdims=True)
    rows = nseq * N_EXPERTS
    capf = float(cap)

    def bit_step(i, t):
        cand = t | (jnp.int32(1) << (30 - i))
        thr = lax.bitcast_convert_type(cand, f32)
        cnt = jnp.sum(jnp.where(aff_ref[...] >= thr, 1.0, 0.0), axis=1, keepdims=True)
        return jnp.where(cnt >= capf, cand, t)

    t = lax.fori_loop(0, 31, bit_step, jnp.zeros((rows, 1), jnp.int32))
    thr = lax.bitcast_convert_type(t, f32)
    need = capf - jnp.sum(jnp.where(aff_ref[...] > thr, 1.0, 0.0), axis=1, keepdims=True)

    blk = MOE_BLOCK
    tri = jnp.where(
        lax.broadcasted_iota(jnp.int32, (blk, blk), 0) < lax.broadcasted_iota(jnp.int32, (blk, blk), 1),
        1.0, 0.0).astype(bf16)
    carry_gt = jnp.zeros((rows, 1), f32)
    carry_eq = jnp.zeros((rows, 1), f32)
    lane = lax.broadcasted_iota(jnp.int32, (rows, LANES), 1)
    meta = jnp.zeros((rows, LANES), f32)
    cmax = jnp.zeros((rows, 1), f32)
    for b in range(n // blk):
        sl = slice(b * blk, (b + 1) * blk)
        ab = aff_ref[:, sl]
        gt = ab > thr
        eq = ab == thr
        gtb = jnp.where(gt, 1.0, 0.0)
        eqb = jnp.where(eq, 1.0, 0.0)
        pre_gt = _dot(gtb.astype(bf16), tri) + carry_gt
        pre_eq = _dot(eqb.astype(bf16), tri) + carry_eq
        meta = jnp.where(lane == b, carry_gt + jnp.minimum(carry_eq, need), meta)
        carry_gt = carry_gt + jnp.sum(gtb, axis=1, keepdims=True)
        carry_eq = carry_eq + jnp.sum(eqb, axis=1, keepdims=True)
        sel = gt | (eq & (pre_eq < need))
        slot = pre_gt + jnp.minimum(pre_eq, need)
        pos_ref[:, sl] = jnp.where(sel, slot, -1.0).astype(jnp.int32)
        cmax = jnp.maximum(cmax, jnp.sum(jnp.where(sel, 1.0, 0.0), axis=1, keepdims=True))
    meta_ref[...] = jnp.where(lane == LANES - 1, cmax, meta).astype(jnp.int32)


def _route(lg3, *, cap):
    nseq, n, _ = lg3.shape
    rows = nseq * N_EXPERTS
    assert n // MOE_BLOCK < LANES
    return pl.pallas_call(
        functools.partial(_route_kernel, nseq=nseq, n=n, cap=cap),
        grid=(1,),
        in_specs=[_const_spec(lg3.shape)],
        out_specs=(_const_spec((rows, n)), _const_spec((rows, n)), _const_spec((rows, LANES))),
        out_shape=(jax.ShapeDtypeStruct((rows, n), jnp.int32), jax.ShapeDtypeStruct((rows, n), f32),
                   jax.ShapeDtypeStruct((rows, LANES), jnp.int32)),
        compiler_params=_params("arbitrary"),
        name="route",
    )(lg3)


def _slot_hits(pos_ref, e, cap):
    width = pos_ref.shape[1]
    return lax.broadcasted_iota(jnp.int32, (cap, width), 0) == pos_ref[e:e + 1, :]


def _one_hot(hits):
    return jnp.concatenate([jnp.where(h, 1.0, 0.0).astype(bf16) for h in hits], axis=0)


def _gather_dense(pos_ref, aff_ref, h2_ref, xg_ref, vals_ref, rows, cap, ne):
    for e0 in range(0, N_EXPERTS, ne):
        hits = [_slot_hits(pos_ref, e0 + e, cap) for e in range(ne)]
        xg = _dot(_one_hot(hits), h2_ref[...]).astype(xg_ref.dtype)
        for e in range(ne):
            xg_ref[e0 + e, rows, :] = xg[e * cap:(e + 1) * cap, :]
            vals = jnp.sum(jnp.where(hits[e], aff_ref[e0 + e:e0 + e + 1, :], 0.0), axis=1, keepdims=True)
            vals_ref[e0 + e, rows, :] = jnp.broadcast_to(vals, (cap, LANES))


def _gather_kernel(pos_ref, aff_ref, h2_ref, xg_ref, vals_ref, *, cap, ne, sb):
    for g in range(sb):
        _gather_dense(pos_ref.at[g], aff_ref.at[g], h2_ref.at[g], xg_ref, vals_ref,
                      slice(g * cap, (g + 1) * cap), cap, ne)


def _gather(pos3, aff3, h23, *, cap, ne, sb):
    nseq, _, n = pos3.shape
    return pl.pallas_call(
        functools.partial(_gather_kernel, cap=cap, ne=ne, sb=sb),
        grid=(nseq // sb,),
        in_specs=[
            pl.BlockSpec((sb, N_EXPERTS, n), lambda s: (s, 0, 0)),
            pl.BlockSpec((sb, N_EXPERTS, n), lambda s: (s, 0, 0)),
            pl.BlockSpec((sb, n, D_MODEL), lambda s: (s, 0, 0)),
        ],
        out_specs=(
            pl.BlockSpec((N_EXPERTS, sb * cap, D_MODEL), lambda s: (0, s, 0)),
            pl.BlockSpec((N_EXPERTS, sb * cap, LANES), lambda s: (0, s, 0)),
        ),
        out_shape=(
            jax.ShapeDtypeStruct((N_EXPERTS, nseq * cap, D_MODEL), bf16),
            jax.ShapeDtypeStruct((N_EXPERTS, nseq * cap, LANES), f32),
        ),
        compiler_params=_params("parallel"),
        name="gather",
    )(pos3, aff3, h23)


def _windows_fit(meta_ref):
    cmax = meta_ref[0, LANES - 1]
    for e in range(1, N_EXPERTS):
        cmax = jnp.maximum(cmax, meta_ref[e, LANES - 1])
    return cmax <= MOE_WIN - WIN_ALIGN


def _win_base(meta_ref, e, b, cap):
    start = meta_ref[e, b]
    return pl.multiple_of(jnp.minimum(start - start % WIN_ALIGN, cap - MOE_WIN), WIN_ALIGN)


def _win_hits(pos_row, base):
    slot = lax.broadcasted_iota(jnp.int32, (MOE_WIN, pos_row.shape[1]), 0) + base
    return slot == pos_row


def _gather_win_kernel(meta_ref, pos_ref, aff_ref, h2_ref, xg_ref, vals_ref, *, n, cap):
    fits = _windows_fit(meta_ref)

    @pl.when(fits)
    def _():
        xg_ref[...] = jnp.zeros(xg_ref.shape, xg_ref.dtype)
        vals_ref[...] = jnp.zeros(vals_ref.shape, vals_ref.dtype)

        def block(b, carry):
            cols = pl.ds(pl.multiple_of(b * MOE_BLOCK, MOE_BLOCK), MOE_BLOCK)
            bases = [_win_base(meta_ref, e, b, cap) for e in range(N_EXPERTS)]
            hits = [_win_hits(pos_ref[e:e + 1, cols], bases[e]) for e in range(N_EXPERTS)]
            part = _dot(_one_hot(hits), h2_ref[cols, :])
            for e in range(N_EXPERTS):
                rows = pl.ds(bases[e], MOE_WIN)
                xg_ref[e, rows, :] += part[e * MOE_WIN:(e + 1) * MOE_WIN, :].astype(xg_ref.dtype)
                vals = jnp.sum(jnp.where(hits[e], aff_ref[e:e + 1, cols], 0.0), axis=1, keepdims=True)
                vals_ref[e, rows, :] += jnp.broadcast_to(vals, (MOE_WIN, LANES))
            return carry

        lax.fori_loop(0, n // MOE_BLOCK, block, 0)

    @pl.when(jnp.logical_not(fits))
    def _():
        _gather_dense(pos_ref, aff_ref, h2_ref, xg_ref, vals_ref, slice(0, cap), cap, 1)


def _gather_win(meta, pos3, aff3, h23, *, cap):
    nseq, _, n = pos3.shape
    return pl.pallas_call(
        functools.partial(_gather_win_kernel, n=n, cap=cap),
        grid=(nseq,),
        in_specs=[
            pl.BlockSpec((N_EXPERTS, LANES), lambda s: (s, 0), memory_space=pltpu.SMEM),
            pl.BlockSpec((None, N_EXPERTS, n), lambda s: (s, 0, 0)),
            pl.BlockSpec((None, N_EXPERTS, n), lambda s: (s, 0, 0)),
            pl.BlockSpec((None, n, D_MODEL), lambda s: (s, 0, 0)),
        ],
        out_specs=(
            pl.BlockSpec((N_EXPERTS, cap, D_MODEL), lambda s: (0, s, 0)),
            pl.BlockSpec((N_EXPERTS, cap, LANES), lambda s: (0, s, 0)),
        ),
        out_shape=(
            jax.ShapeDtypeStruct((N_EXPERTS, nseq * cap, D_MODEL), bf16),
            jax.ShapeDtypeStruct((N_EXPERTS, nseq * cap, LANES), f32),
        ),
        compiler_params=_params("parallel"),
        name="gather_win",
    )(meta, pos3, aff3, h23)


def _experts_kernel(xp_ref, vp_ref, xs_ref, vs_ref, wg_ref, wu_ref, wd_ref, yp_ref, ys_ref, *, rc):
    wg = wg_ref[...].astype(bf16)
    wu = wu_ref[...].astype(bf16)
    wd = wd_ref[...].astype(bf16)
    for x_ref, v_ref, y_ref in ((xp_ref, vp_ref, yp_ref), (xs_ref, vs_ref, ys_ref)):
        for r0 in range(0, x_ref.shape[0], rc):
            x = x_ref[r0:r0 + rc, :]
            a = _dot(x, wg)
            u = _dot(x, wu)
            hm = (a * jax.nn.sigmoid(a) * u).astype(bf16)
            y = _dot(hm, wd) * v_ref[r0:r0 + rc, 0:1]
            y_ref[r0:r0 + rc, :] = y.astype(y_ref.dtype)


def _experts(xg_p, vals_p, xg_s, vals_s, wg, wu, wd, *, rc=512):
    rp = xg_p.shape[1]
    rs = xg_s.shape[1]
    per_e = lambda r, w: pl.BlockSpec((None, r, w), lambda e: (e, 0, 0))
    return pl.pallas_call(
        functools.partial(_experts_kernel, rc=rc),
        grid=(N_EXPERTS,),
        in_specs=[
            per_e(rp, D_MODEL), per_e(rp, LANES), per_e(rs, D_MODEL), per_e(rs, LANES),
            per_e(D_MODEL, EXPERT_FF), per_e(D_MODEL, EXPERT_FF), per_e(EXPERT_FF, D_MODEL),
        ],
        out_specs=(per_e(rp, D_MODEL), per_e(rs, D_MODEL)),
        out_shape=(
            jax.ShapeDtypeStruct((N_EXPERTS, rp, D_MODEL), bf16),
            jax.ShapeDtypeStruct((N_EXPERTS, rs, D_MODEL), bf16),
        ),
        compiler_params=_params("parallel"),
        name="experts",
    )(xg_p, vals_p, xg_s, vals_s, wg, wu, wd)


def _scatter_dense(pos_ref, y_ref, rows, tn, cap, ne):
    moe = jnp.zeros((tn, D_MODEL), f32)
    for e0 in range(0, N_EXPERTS, ne):
        onehot = _one_hot([_slot_hits(pos_ref, e0 + e, cap) for e in range(ne)])
        y = jnp.concatenate([y_ref[e0 + e, rows, :] for e in range(ne)], axis=0)
        moe = moe + lax.dot_general(onehot, y, (((0,), (0,)), ((), ())), preferred_element_type=f32)
    return moe


def _scatter_kernel(pos_ref, y_ref, x1_ref, m_ref, fn_ref, o_ref, *, tn, cap, ne, sb):
    g2 = m_ref[5:6, :]
    for g in range(sb):
        moe = _scatter_dense(pos_ref.at[g], y_ref, slice(g * cap, (g + 1) * cap), tn, cap, ne)
        o_ref[g] = _rms(x1_ref[g] + g2 * moe, fn_ref[...])


def _scatter(pos3, y, x13, m3, fn, *, cap, ne, tn, sb):
    nseq, _, n = pos3.shape
    assert sb == 1 or m3.shape[0] == 1, "sequences sharing a grid step must share their modulation rows"
    return pl.pallas_call(
        functools.partial(_scatter_kernel, tn=tn, cap=cap, ne=ne, sb=sb),
        grid=(nseq // sb, n // tn),
        in_specs=[
            pl.BlockSpec((sb, N_EXPERTS, tn), lambda s, i: (s, 0, i)),
            pl.BlockSpec((N_EXPERTS, sb * cap, D_MODEL), lambda s, i: (0, s, 0)),
            pl.BlockSpec((sb, tn, D_MODEL), lambda s, i: (s, i, 0)),
            _mod_spec(m3, lambda s, i: s),
            _const_spec((1, D_MODEL)),
        ],
        out_specs=pl.BlockSpec((sb, tn, D_MODEL), lambda s, i: (s, i, 0)),
        out_shape=jax.ShapeDtypeStruct((nseq, n, D_MODEL), f32),
        compiler_params=_params("parallel", "arbitrary"),
        name="scatter",
    )(pos3, y, x13, m3, fn)


def _scatter_win_kernel(meta_ref, pos_ref, y_ref, x1_ref, m_ref, fn_ref, o_ref, moe_scr, *, cap):
    b = pl.program_id(1)
    fits = _windows_fit(meta_ref)

    @pl.when(fits)
    def _():
        bases = [_win_base(meta_ref, e, b, cap) for e in range(N_EXPERTS)]
        onehot = _one_hot([_win_hits(pos_ref[e:e + 1, :], bases[e]) for e in range(N_EXPERTS)])
        y = jnp.concatenate([y_ref[e, pl.ds(bases[e], MOE_WIN), :] for e in range(N_EXPERTS)], axis=0)
        moe_scr[...] = lax.dot_general(onehot, y, (((0,), (0,)), ((), ())), preferred_element_type=f32)

    @pl.when(jnp.logical_not(fits))
    def _():
        moe_scr[...] = _scatter_dense(pos_ref, y_ref, slice(0, cap), MOE_BLOCK, cap, 1)

    o_ref[...] = _rms(x1_ref[...] + m_ref[5:6, :] * moe_scr[...], fn_ref[...])


def _scatter_win(meta, pos3, y, x13, m3, fn, *, cap):
    nseq, _, n = pos3.shape
    tn = MOE_BLOCK
    return pl.pallas_call(
        functools.partial(_scatter_win_kernel, cap=cap),
        grid=(nseq, n // tn),
        in_specs=[
            pl.BlockSpec((N_EXPERTS, LANES), lambda s, i: (s, 0), memory_space=pltpu.SMEM),
            pl.BlockSpec((None, N_EXPERTS, tn), lambda s, i: (s, 0, i)),
            pl.BlockSpec((N_EXPERTS, cap, D_MODEL), lambda s, i: (0, s, 0)),
            pl.BlockSpec((None, tn, D_MODEL), lambda s, i: (s, i, 0)),
            _mod_spec(m3, lambda s, i: s),
            _const_spec((1, D_MODEL)),
        ],
        out_specs=pl.BlockSpec((None, tn, D_MODEL), lambda s, i: (s, i, 0)),
        out_shape=jax.ShapeDtypeStruct((nseq, n, D_MODEL), f32),
        scratch_shapes=[pltpu.VMEM((tn, D_MODEL), f32)],
        compiler_params=_params("parallel", "arbitrary"),
        name="scatter_win",
    )(meta, pos3, y, x13, m3, fn)


def _rope_tables(n):
    t = np.arange(n)
    half = QK_ROPE // 2
    freqs = ROPE_BASE ** (-np.arange(0, half, 2, dtype=np.float64) / half)
    ang_r = (t // GRID_W)[:, None] * freqs
    ang_c = (t % GRID_W)[:, None] * freqs
    cr, sr, cc, sc = np.cos(ang_r), np.sin(ang_r), np.cos(ang_c), np.sin(ang_c)
    cos = np.ones((n, HEAD_W))
    sin = np.zeros((n, HEAD_W))
    cos[:, ROPE_OFF:ROPE_OFF + QK_ROPE] = np.concatenate([cr, cr, cc, cc], axis=-1)
    sin[:, ROPE_OFF:ROPE_OFF + QK_ROPE] = np.concatenate([-sr, sr, -sc, sc], axis=-1)
    return jnp.asarray(cos, f32), jnp.asarray(sin, f32)


_PARTNER = np.concatenate([np.arange(8, 16), np.arange(0, 8), np.arange(24, 32), np.arange(16, 24)])


def _rope_partner(w):
    q = QK_ROPE // 4
    return jnp.concatenate([w[..., q:2 * q], w[..., 0:q], w[..., 3 * q:4 * q], w[..., 2 * q:3 * q]], axis=-1)


def _rope_placement():
    place = np.zeros((LANES, 2 * LANES), np.float32)
    d = np.arange(QK_ROPE)
    place[d, ROPE_OFF + d] = 1.0
    place[_PARTNER, LANES + ROPE_OFF + d] = 1.0
    return jnp.asarray(place, bf16)


def _head_blocks(w_nope, w_rope):
    rows = w_nope.shape[0]
    if w_rope is None:
        w_rope = jnp.zeros((rows, N_HEADS, QK_ROPE), w_nope.dtype)
    z = jnp.zeros((rows, N_HEADS, HEAD_W - QK_NOPE - QK_ROPE), w_nope.dtype)
    return jnp.concatenate([w_nope, w_rope, z], axis=-1).reshape(rows, N_HEADS * HEAD_W)


def kernel(x_prompt, x_sample, cache_ckv, cache_kpe, c, c_ctx, w_ada, b_ada, norm1, w_in, conv_dw, conv_dw_b,
           conv_ln_g, conv_ln_b, w_conv_out, q_norm, w_qb, kv_norm, w_kvb, w_o_mla, w_out, norm2, w_router,
           w_e_gate, w_e_up, w_e_down, final_norm):
    assert w_ada.shape[0] == 1, "single trunk layer"
    nb_p, n_p, _ = x_prompt.shape
    nb_s, n_s, _ = x_sample.shape

    win = _wprep(w_in[0].T, _rope_placement())
    wq = w_qb[0].reshape(Q_LORA, N_HEADS, QK_NOPE + QK_ROPE)
    wqb = _head_blocks(wq[..., :QK_NOPE], wq[..., QK_NOPE:]).astype(bf16)
    wqbp = _head_blocks(jnp.zeros_like(wq[..., :QK_NOPE]), _rope_partner(wq[..., QK_NOPE:])).astype(bf16)
    wkv = w_kvb[0].reshape(KV_LORA, N_HEADS, QK_NOPE + V_DIM)
    wk = _head_blocks(wkv[..., :QK_NOPE], None).astype(bf16)
    wv = wkv[..., QK_NOPE:].reshape(KV_LORA, N_HEADS * V_DIM).astype(bf16)
    wco = w_conv_out[0].astype(bf16)
    wo = w_o_mla[0].astype(bf16)
    wout = w_out[0].astype(bf16)
    wr_hi = w_router[0].astype(bf16)
    wr_lo = (w_router[0] - wr_hi.astype(f32)).astype(bf16)
    wr = jnp.concatenate([wr_hi, wr_lo, jnp.zeros((D_MODEL, LANES - 2 * N_EXPERTS), bf16)], axis=-1)
    row = lambda a: a.reshape(1, -1)

    mod = jnp.concatenate([c_ctx[None, :], c, jnp.zeros((8 - 1 - nb_s, D_MODEL), f32)], axis=0)
    m = _ada(mod, w_ada[0], b_ada[0]).reshape(8, 6, D_MODEL)
    m_p, m_s = m[0:1], m[1:1 + nb_s]
    cos, sin = _rope_tables(n_s)

    conv_w = _conv_weights(conv_dw[0], row(conv_dw_b[0]), row(conv_ln_g[0]), row(conv_ln_b[0]), wco, CONV_CHUNK)

    def mixers(x, m3, rope, ctx_ckv, ctx_kpe):
        nseq, n, _ = x.shape
        x2d = x.reshape(nseq * n, D_MODEL)
        fuse_conv = IN_TILE % n == 0
        cm, q, ckv, kr, g, *kpe = _inproj(x2d, m3, row(norm1[0]), win, row(q_norm[0]), wqb, wqbp, row(kv_norm[0]),
                                          cos, sin, conv_w if fuse_conv else None, n=n, rope=rope, tm=IN_TILE,
                                          ct=CONV_CHUNK)
        if not fuse_conv:
            cm = _conv(cm.reshape(nseq, n, CONV_DIM), g.reshape(nseq, n, 2 * D_MODEL), conv_w, ct=CONV_CHUNK)
        keys_ckv = ckv.reshape(nseq, n, KV_LORA)
        keys_kpe = kr.reshape(nseq, n, LANES)
        if ctx_ckv is not None:
            keys_ckv = jnp.concatenate([ctx_ckv.astype(keys_ckv.dtype), keys_ckv], axis=1)
            keys_kpe = jnp.concatenate([ctx_kpe.astype(keys_kpe.dtype), keys_kpe], axis=1)
        x1, h2, lg = _attn(q, keys_ckv, keys_kpe, cm.reshape(nseq * n, D_MODEL), g, x2d, m3, wk, wv, wo, wout,
                           row(norm2[0]), wr, n=n, tq=min(n, Q_TILE))
        return x1, h2, lg, ckv, kpe

    ctx_kpe = jnp.pad(cache_kpe[:, 0], ((0, 0), (0, 0), (ROPE_OFF, LANES - ROPE_OFF - QK_ROPE)))
    x1_p, h2_p, lg_p, ckv_p, (kpe_p,) = mixers(x_prompt, m_p, False, None, None)
    x1_s, h2_s, lg_s, _, _ = mixers(x_sample, m_s, True, cache_ckv[:, 0], ctx_kpe)

    def moe_tiles(n):
        cap = EC_FACTOR * n // N_EXPERTS
        ne = N_EXPERTS if N_EXPERTS * cap <= MOE_ROWS else 1
        sb = max(1, MOE_TOKENS // n)
        return cap, ne, sb

    def windowed(n):
        return n >= 4 * MOE_BLOCK and EC_FACTOR * n // N_EXPERTS >= 2 * MOE_WIN

    def route_gather(h2, lg, nseq, n):
        cap, ne, sb = moe_tiles(n)
        pos, aff, meta = _route(lg.reshape(nseq, n, LANES), cap=cap)
        pos3 = pos.reshape(nseq, N_EXPERTS, n)
        aff3 = aff.reshape(nseq, N_EXPERTS, n)
        h23 = h2.reshape(nseq, n, D_MODEL)
        if windowed(n):
            xg, vals = _gather_win(meta, pos3, aff3, h23, cap=cap)
        else:
            xg, vals = _gather(pos3, aff3, h23, cap=cap, ne=ne, sb=sb)
        return pos3, meta, xg, vals

    pos_p, meta_p, xg_p, vals_p = route_gather(h2_p, lg_p, nb_p, n_p)
    pos_s, meta_s, xg_s, vals_s = route_gather(h2_s, lg_s, nb_s, n_s)
    y_p, y_s = _experts(xg_p, vals_p, xg_s, vals_s, w_e_gate[0], w_e_up[0], w_e_down[0])
    fn = row(final_norm)

    def scatter(pos, meta, y, x1, m3, nseq, n):
        cap, ne, sb = moe_tiles(n)
        x13 = x1.reshape(nseq, n, D_MODEL)
        if windowed(n):
            return _scatter_win(meta, pos, y, x13, m3, fn, cap=cap)
        return _scatter(pos, y, x13, m3, fn, cap=cap, ne=ne, tn=min(n, SCATTER_TILE), sb=sb)

    y_prompt = scatter(pos_p, meta_p, y_p, x1_p, m_p, nb_p, n_p)
    y_sample = scatter(pos_s, meta_s, y_s, x1_s, m_s, nb_s, n_s)

    new_ckv = ckv_p.reshape(nb_p, 1, n_p, KV_LORA)
    new_kpe = kpe_p.reshape(nb_p, 1, n_p, QK_ROPE)
    return (y_prompt, y_sample, new_ckv, new_kpe)
```

```python
import functools

import jax
import jax.numpy as jnp
import numpy as np
from jax import lax
from jax.experimental import pallas as pl
from jax.experimental.pallas import tpu as pltpu

D_MODEL = 1024
GRID_W = 64
CONV_DIM = 512
CONV_WIDTH = 31
N_HEADS = 8
QK_NOPE = 64
QK_ROPE = 32
V_DIM = 64
Q_LORA = 256
KV_LORA = 128
N_EXPERTS = 16
EXPERT_FF = 512
EC_FACTOR = 2
ROPE_BASE = 10000.0
EPS = 1e-6

LANES = 128
HEAD_W = LANES
ROPE_OFF = QK_NOPE
CONV_HALO = 16
LOG2E = 1.4426950408889634
VMEM_LIMIT = 48 * 1024 * 1024
IN_TILE = 512
CONV_CHUNK = 64
CONV_OUT_ROWS = 256
GATE_CHUNK = 512
EXPERT_ROWS = 512
Q_TILE = 512
ATTN_TOKENS = 1024
SCORE_BYTES = 12 * 1024 * 1024
MOE_ROWS = 512
MOE_TOKENS = 1024
SCATTER_TILE = 512
MOE_BLOCK = 2 * LANES
MOE_WIN = 80
WIN_ALIGN = 16

C_CONV = 0
C_QA = 2 * CONV_DIM
C_KVA = C_QA + Q_LORA
C_KR = C_KVA + KV_LORA
C_GATE = C_KR + LANES
C_END = C_GATE + 2 * D_MODEL
C_KRP = C_END

f32 = jnp.float32
bf16 = jnp.bfloat16


def _params(*sem):
    return pltpu.CompilerParams(dimension_semantics=sem, vmem_limit_bytes=VMEM_LIMIT)


def _dot(a, b):
    return jnp.dot(a, b, preferred_element_type=f32)


def _rms(x, g):
    return x * lax.rsqrt(jnp.mean(x * x, axis=-1, keepdims=True) + EPS) * g


def _const_spec(shape):
    nd = len(shape)
    return pl.BlockSpec(shape, lambda *_: (0,) * nd)


def _mod_spec(m3, seq_of):
    if m3.shape[0] == 1:
        return _const_spec((None, 6, D_MODEL))
    return pl.BlockSpec((None, 6, D_MODEL), lambda *idx: (seq_of(*idx), 0, 0))


def _ada_kernel(s_ref, w_ref, b_ref, o_ref):
    s = s_ref[...]
    s = s * jax.nn.sigmoid(s)
    o_ref[...] = _dot(s.astype(bf16), w_ref[...].astype(bf16)) + b_ref[...]


def _ada(mod, w_ada, b_ada):
    rows = mod.shape[0]
    n_out = w_ada.shape[1]
    tn = D_MODEL
    return pl.pallas_call(
        _ada_kernel,
        grid=(n_out // tn,),
        in_specs=[
            _const_spec((rows, D_MODEL)),
            pl.BlockSpec((D_MODEL, tn), lambda j: (0, j)),
            pl.BlockSpec((1, tn), lambda j: (0, j)),
        ],
        out_specs=pl.BlockSpec((rows, tn), lambda j: (0, j)),
        out_shape=jax.ShapeDtypeStruct((rows, n_out), f32),
        compiler_params=_params("arbitrary"),
        name="ada",
    )(mod, w_ada, b_ada.reshape(1, n_out))


def _wprep_kernel(wt_ref, place_ref, o_ref):
    def block(r0):
        return wt_ref[r0:r0 + LANES, :].T.astype(bf16)

    for j in range(C_KR // LANES):
        o_ref[:, j * LANES:(j + 1) * LANES] = block(j * LANES)
    placed = _dot(block(C_KR), place_ref[...])
    o_ref[:, C_KR:C_GATE] = placed[:, 0:LANES].astype(bf16)
    o_ref[:, C_KRP:C_KRP + LANES] = placed[:, LANES:2 * LANES].astype(bf16)
    for j in range(2 * D_MODEL // LANES):
        o_ref[:, C_GATE + j * LANES:C_GATE + (j + 1) * LANES] = block(C_KR + QK_ROPE + j * LANES)


def _wprep(w_in_t, place):
    cols, rows = w_in_t.shape
    return pl.pallas_call(
        _wprep_kernel,
        grid=(rows // LANES,),
        in_specs=[pl.BlockSpec((cols, LANES), lambda i: (0, i)), _const_spec(place.shape)],
        out_specs=pl.BlockSpec((LANES, C_KRP + LANES), lambda i: (i, 0)),
        out_shape=jax.ShapeDtypeStruct((rows, C_KRP + LANES), bf16),
        compiler_params=_params("parallel"),
        name="wprep",
    )(w_in_t, place)


def _conv_taps(vpad, r0, shift_ref, dw_ref, dwb_ref, ybuf, y0, ct):
    pad = CONV_WIDTH // 2
    sub = 8
    span = ((CONV_HALO - pad + CONV_WIDTH - 1) // sub) * sub
    cw = 2 * LANES
    for cb in range(CONV_DIM // cw):
        sl = slice(cb * cw, (cb + 1) * cw)
        win = vpad[pl.ds(r0, ct + 2 * CONV_HALO), sl]
        acc = jnp.zeros((ct // sub, sub, cw), f32)
        for ph in range(sub):
            wph = _dot(shift_ref[ph], win)
            for a in range(span // sub + 1):
                k = a * sub + ph - (CONV_HALO - pad)
                if 0 <= k < CONV_WIDTH:
                    acc = acc + wph[a * sub:a * sub + ct, :].reshape(ct // sub, sub, cw) * dw_ref[k, :, sl][None]
        ybuf[pl.ds(y0, ct), sl] = acc.reshape(ct, cw) + dwb_ref[:, sl]


def _conv_out(y, lng_ref, lnb_ref, wco_ref, gate):
    mu = jnp.mean(y, axis=-1, keepdims=True)
    yc = y - mu
    var = jnp.mean(yc * yc, axis=-1, keepdims=True)
    z = yc * lax.rsqrt(var + EPS) * lng_ref[...] + lnb_ref[...]
    z = z * jax.nn.sigmoid(z)
    return (gate.astype(f32) * _dot(z.astype(bf16), wco_ref[...])).astype(bf16)


def _fill_padded(vpad, v, n):
    zeros = jnp.zeros((CONV_HALO, CONV_DIM), vpad.dtype)
    vpad[0:CONV_HALO, :] = zeros
    vpad[CONV_HALO + n:2 * CONV_HALO + n, :] = zeros
    vpad[CONV_HALO:CONV_HALO + n, :] = v


def _inproj_kernel(*refs, rope, fuse_conv, q_scale, n, rt, ct):
    refs = list(refs)
    x_ref, m_ref, n1_ref, win_ref, qn_ref, wqb_ref = refs[:6]
    del refs[:6]
    if rope:
        wqbp_ref = refs.pop(0)
    kvn_ref = refs.pop(0)
    if rope:
        cos_ref, sin_ref = refs[:2]
        del refs[:2]
    if fuse_conv:
        shift_ref, dw_ref, dwb_ref, lng_ref, lnb_ref, wco_ref = refs[:6]
        del refs[:6]
    vc_ref, q_ref, ckv_ref, kr_ref, g_ref = refs[:5]
    del refs[:5]
    if not rope:
        kpe_ref = refs.pop(0)
    if fuse_conv:
        vpad, ybuf, gc = refs
    tm = x_ref.shape[0]

    sh1 = m_ref[0:1, :]
    sc1 = m_ref[1:2, :]
    h = _rms(x_ref[...], n1_ref[...]) * (1.0 + sc1) + sh1
    hb = h.astype(bf16)

    left = _dot(hb, win_ref[:, 0:C_GATE])
    v = (left[:, C_CONV:C_CONV + CONV_DIM] * jax.nn.sigmoid(left[:, C_CONV + CONV_DIM:C_QA])).astype(bf16)
    if fuse_conv:
        for s in range(tm // n):
            _fill_padded(vpad.at[s], v[s * n:(s + 1) * n, :], n)
    else:
        vc_ref[...] = v

    chunks = []
    if fuse_conv:
        chunks = [(s, c) for s in range(tm // n) for c in range(n // ct)]

    def conv_some(count):
        for _ in range(min(count, len(chunks))):
            s, c = chunks.pop(0)
            _conv_taps(vpad.at[s], c * ct, shift_ref, dw_ref, dwb_ref, ybuf, s * n + c * ct, ct)

    steps = 2 + 2 * D_MODEL // GATE_CHUNK
    per_step = -(-len(chunks) // steps)

    qa = left[:, C_QA:C_KVA]
    qn = _rms(qa, qn_ref[...]).astype(bf16)
    q = _dot(qn, wqb_ref[...])
    if rope:
        qp = _dot(qn, wqbp_ref[...])
        cos = cos_ref[...]
        sin = sin_ref[...]
        for hd in range(N_HEADS):
            sl = slice(hd * HEAD_W, (hd + 1) * HEAD_W)
            q_ref[:, sl] = ((q[:, sl] * cos + qp[:, sl] * sin) * q_scale).astype(q_ref.dtype)
    else:
        q_ref[...] = (q * q_scale).astype(q_ref.dtype)
    conv_some(per_step)

    kva = left[:, C_KVA:C_KR]
    ckv_ref[...] = _rms(kva, kvn_ref[...]).astype(ckv_ref.dtype)

    kr = left[:, C_KR:C_GATE]
    if rope:
        krp = _dot(hb, win_ref[:, C_KRP:C_KRP + LANES])
        kr = kr * cos_ref[...] + krp * sin_ref[...]
    else:
        kpe_ref[...] = kr[:, ROPE_OFF:ROPE_OFF + QK_ROPE]
    kr_ref[...] = kr.astype(kr_ref.dtype)
    conv_some(per_step)

    gw = GATE_CHUNK
    for j in range(2 * D_MODEL // gw):
        gate = jax.nn.sigmoid(_dot(hb, win_ref[:, C_GATE + j * gw:C_GATE + (j + 1) * gw])).astype(bf16)
        if not fuse_conv:
            g_ref[:, j * gw:(j + 1) * gw] = gate
        elif j * gw < D_MODEL:
            gc[:, j * gw:(j + 1) * gw] = gate
        else:
            g_ref[:, j * gw - D_MODEL:(j + 1) * gw - D_MODEL] = gate
        conv_some(per_step)

    if fuse_conv:
        conv_some(len(chunks))
        for r0 in range(0, tm, rt):
            vc_ref[r0:r0 + rt, :] = _conv_out(ybuf[r0:r0 + rt, :], lng_ref, lnb_ref, wco_ref, gc[r0:r0 + rt, :])


def _inproj(x2d, m3, norm1, win, q_norm, wqb, wqbp, kv_norm, cos, sin, conv_w, *, n, rope, tm, rt=CONV_OUT_ROWS, ct=CONV_CHUNK):
    tokens = x2d.shape[0]
    fuse_conv = conv_w is not None
    assert n % tm == 0 or (tm % n == 0 and m3.shape[0] == 1 and not rope)
    assert not fuse_conv or tm % n == 0
    tiles_per_seq = max(1, n // tm)
    q_scale = float((QK_NOPE + QK_ROPE) ** -0.5 * LOG2E)
    tile = lambda w: pl.BlockSpec((tm, w), lambda i: (i, 0))
    in_specs = [
        tile(D_MODEL),
        _mod_spec(m3, lambda i: i // tiles_per_seq),
        _const_spec((1, D_MODEL)),
        _const_spec((D_MODEL, C_KRP + LANES if rope else C_END)),
        _const_spec((1, Q_LORA)),
        _const_spec(wqb.shape),
    ]
    args = [x2d, m3, norm1, win, q_norm, wqb]
    if rope:
        in_specs.append(_const_spec(wqbp.shape))
        args.append(wqbp)
    in_specs.append(_const_spec((1, KV_LORA)))
    args.append(kv_norm)
    if rope:
        tab = pl.BlockSpec((tm, LANES), lambda i: (i % tiles_per_seq, 0))
        in_specs += [tab, tab]
        args += [cos, sin]
    scratch = []
    if fuse_conv:
        args += list(conv_w)
        in_specs += [_const_spec(w.shape) for w in conv_w]
        scratch = [
            pltpu.VMEM((tm // n, n + 2 * CONV_HALO, CONV_DIM), bf16),
            pltpu.VMEM((tm, CONV_DIM), f32),
            pltpu.VMEM((tm, D_MODEL), bf16),
        ]
    gate_w = D_MODEL if fuse_conv else 2 * D_MODEL
    vc_w = D_MODEL if fuse_conv else CONV_DIM
    out_shape = [
        jax.ShapeDtypeStruct((tokens, vc_w), bf16),
        jax.ShapeDtypeStruct((tokens, N_HEADS * HEAD_W), bf16),
        jax.ShapeDtypeStruct((tokens, KV_LORA), bf16 if rope else f32),
        jax.ShapeDtypeStruct((tokens, LANES), bf16),
        jax.ShapeDtypeStruct((tokens, gate_w), bf16),
    ]
    out_specs = [tile(vc_w), tile(N_HEADS * HEAD_W), tile(KV_LORA), tile(LANES), tile(gate_w)]
    if not rope:
        out_shape.append(jax.ShapeDtypeStruct((tokens, QK_ROPE), f32))
        out_specs.append(tile(QK_ROPE))
    return pl.pallas_call(
        functools.partial(_inproj_kernel, rope=rope, fuse_conv=fuse_conv, q_scale=q_scale, n=n, rt=rt, ct=ct),
        grid=(tokens // tm,),
        in_specs=in_specs,
        out_specs=out_specs,
        out_shape=out_shape,
        scratch_shapes=scratch,
        compiler_params=_params("parallel"),
        name="inproj_rope" if rope else "inproj_conv",
    )(*args)


def _conv_kernel(v_ref, g_ref, shift_ref, dw_ref, dwb_ref, lng_ref, lnb_ref, wco_ref, o_ref, vpad, ybuf,
                 *, n, rt, ct):
    _fill_padded(vpad, v_ref[...], n)

    def conv_chunk(c, carry):
        r0 = pl.multiple_of(c * ct, ct)
        _conv_taps(vpad, r0, shift_ref, dw_ref, dwb_ref, ybuf, r0, ct)
        return carry

    lax.fori_loop(0, n // ct, conv_chunk, 0, unroll=4)

    def chunk(c, carry):
        r0 = pl.multiple_of(c * rt, rt)
        o_ref[pl.ds(r0, rt), :] = _conv_out(ybuf[pl.ds(r0, rt), :], lng_ref, lnb_ref, wco_ref, g_ref[pl.ds(r0, rt), :])
        return carry

    lax.fori_loop(0, n // rt, chunk, 0)


def _conv_weights(dw, dwb, lng, lnb, wco, ct):
    rows = ct + 2 * CONV_HALO
    i = np.arange(rows)
    shifts = jnp.asarray(np.stack([(i[None, :] == i[:, None] + ph) for ph in range(8)]), bf16)
    dw_tiles = jnp.broadcast_to(dw[:, None, :], (CONV_WIDTH, 8, CONV_DIM))
    return shifts, dw_tiles, dwb, lng, lnb, wco


def _conv(v3, g3, conv_w, *, rt=CONV_OUT_ROWS, ct=CONV_CHUNK):
    nseq, n, _ = v3.shape
    assert v3.dtype == bf16, "the one-hot row shifts are exact only for bf16 windows"
    return pl.pallas_call(
        functools.partial(_conv_kernel, n=n, rt=rt, ct=ct),
        grid=(nseq,),
        in_specs=[
            pl.BlockSpec((None, n, CONV_DIM), lambda s: (s, 0, 0)),
            pl.BlockSpec((None, n, D_MODEL), lambda s: (s, 0, 0)),
        ] + [_const_spec(w.shape) for w in conv_w],
        out_specs=pl.BlockSpec((None, n, D_MODEL), lambda s: (s, 0, 0)),
        out_shape=jax.ShapeDtypeStruct((nseq, n, D_MODEL), bf16),
        scratch_shapes=[
            pltpu.VMEM((n + 2 * CONV_HALO, CONV_DIM), bf16),
            pltpu.VMEM((n, CONV_DIM), f32),
        ],
        compiler_params=_params("parallel"),
        name="conv",
    )(v3, g3, *conv_w)


def _attn_kernel(q_ref, ckv_ref, kpe_ref, cm_ref, g_ref, x_ref, m_ref, wk_ref, wv_ref, wo_ref, wout_ref, n2_ref,
                 wr_ref, x1_ref, h2_ref, lg_ref, k_scr, v_scr, q_scr, *, sb, hg):
    s_len = ckv_ref.shape[1]

    @pl.when(pl.program_id(1) == 0)
    def _():
        ckv = ckv_ref[...].reshape(sb * s_len, KV_LORA).astype(bf16)
        kpe = kpe_ref[...].reshape(sb * s_len, LANES).astype(f32)
        k = _dot(ckv, wk_ref[...])
        v = _dot(ckv, wv_ref[...])
        ones = jnp.ones((s_len, V_DIM), bf16)
        for g in range(sb):
            rows = slice(g * s_len, (g + 1) * s_len)
            for hd in range(N_HEADS):
                u = g * N_HEADS + hd
                k_scr[u] = (k[rows, hd * HEAD_W:(hd + 1) * HEAD_W] + kpe[rows, :]).astype(bf16)
                vh = v[rows, hd * V_DIM:(hd + 1) * V_DIM].astype(bf16)
                if hd % 2 == 0:
                    v_scr[u, :, 0:V_DIM] = vh
                    v_scr[u, :, V_DIM:LANES] = ones
                else:
                    v_scr[u, :, 0:V_DIM] = ones
                    v_scr[u, :, V_DIM:LANES] = vh

    tq = q_ref.shape[0] // sb
    units = sb * N_HEADS
    for g in range(sb):
        for hd in range(N_HEADS):
            q_scr[g * N_HEADS + hd] = q_ref[g * tq:(g + 1) * tq, hd * HEAD_W:(hd + 1) * HEAD_W]
    lane = lax.broadcasted_iota(jnp.int32, (tq, LANES), 1)
    pairs = []
    for u0 in range(0, units, hg):
        us = slice(u0, u0 + hg)
        s = lax.dot_general(q_scr[us], k_scr[us], (((2,), (2,)), ((0,), (0,))), preferred_element_type=f32)
        mx = jnp.max(s, axis=-1, keepdims=True)
        p = jnp.exp2(s - mx).astype(bf16)
        r = lax.dot_general(p, v_scr[us], (((2,), (1,)), ((0,), (0,))), preferred_element_type=f32)
        for j in range(hg // 2):
            re, ro = r[2 * j], r[2 * j + 1]
            oe = re * (1.0 / re[:, V_DIM:V_DIM + 1])
            oo = ro * (1.0 / ro[:, 0:1])
            pairs.append(jnp.where(lane < V_DIM, oe, oo))
    per_seq = N_HEADS // 2
    attn = jnp.concatenate(
        [jnp.concatenate(pairs[g * per_seq:(g + 1) * per_seq], axis=-1) for g in range(sb)], axis=0).astype(bf16)
    merged = (cm_ref[...].astype(f32) + g_ref[...].astype(f32) * _dot(attn, wo_ref[...])).astype(bf16)

    g1 = m_ref[2:3, :]
    sh2 = m_ref[3:4, :]
    sc2 = m_ref[4:5, :]
    x1 = x_ref[...] + g1 * _dot(merged, wout_ref[...])
    x1_ref[...] = x1
    h2 = _rms(x1, n2_ref[...]) * (1.0 + sc2) + sh2
    hi = h2.astype(bf16)
    h2_ref[...] = hi
    lo = (h2 - hi.astype(f32)).astype(bf16)
    both = _dot(jnp.concatenate([hi, lo], axis=0), wr_ref[...])
    rows = hi.shape[0]
    s = both[0:rows] + both[rows:2 * rows]
    lg_ref[...] = s + pltpu.roll(s, LANES - N_EXPERTS, axis=1)


def _attn(q2d, ckv3, kpe3, cm2d, g2d, x2d, m3, wk, wv, wo, wout, norm2, wr, *, n, tq):
    nseq, s_len, _ = ckv3.shape
    sb = max(1, ATTN_TOKENS // n) if (tq == n and m3.shape[0] == 1) else 1
    units = sb * N_HEADS
    hg = max(2, min(units, SCORE_BYTES // (tq * s_len * 4)))
    assert units % hg == 0 and hg % 2 == 0 and nseq % sb == 0
    gate_col = g2d.shape[1] // D_MODEL - 1
    qb = n // tq
    tokens = nseq * n
    tile = lambda w, c=0: pl.BlockSpec((sb * tq, w), lambda s, i: (s * qb + i, c))
    return pl.pallas_call(
        functools.partial(_attn_kernel, sb=sb, hg=hg),
        grid=(nseq // sb, qb),
        in_specs=[
            tile(N_HEADS * HEAD_W),
            pl.BlockSpec((sb, s_len, KV_LORA), lambda s, i: (s, 0, 0)),
            pl.BlockSpec((sb, s_len, LANES), lambda s, i: (s, 0, 0)),
            tile(D_MODEL),
            tile(D_MODEL, gate_col),
            tile(D_MODEL),
            _mod_spec(m3, lambda s, i: s),
            _const_spec(wk.shape),
            _const_spec(wv.shape),
            _const_spec(wo.shape),
            _const_spec(wout.shape),
            _const_spec((1, D_MODEL)),
            _const_spec(wr.shape),
        ],
        out_specs=(tile(D_MODEL), tile(D_MODEL), tile(LANES)),
        out_shape=(
            jax.ShapeDtypeStruct((tokens, D_MODEL), f32),
            jax.ShapeDtypeStruct((tokens, D_MODEL), bf16),
            jax.ShapeDtypeStruct((tokens, LANES), f32),
        ),
        scratch_shapes=[
            pltpu.VMEM((units, s_len, HEAD_W), bf16),
            pltpu.VMEM((units, s_len, LANES), bf16),
            pltpu.VMEM((units, tq, HEAD_W), bf16),
        ],
        compiler_params=_params("parallel", "arbitrary"),
        name="attn_out",
    )(q2d, ckv3, kpe3, cm2d, g2d, x2d, m3, wk, wv, wo, wout, norm2, wr)


def _route_kernel(lg_ref, pos_ref, aff_ref, meta_ref, *, nseq, n, cap):
    for s in range(nseq):
        lt = lg_ref[s].T[0:N_EXPERTS, :]
        e = jnp.exp(lt - jnp.max(lt, axis=0, keepdims=True))
        aff_ref[s * N_EXPERTS:(s + 1) * N_EXPERTS, :] = e / jnp.sum(e, axis=0, keepdims=True)
    rows = nseq * N_EXPERTS
    capf = float(cap)

    def bit_step(i, t):
        cand = t | (jnp.int32(1) << (30 - i))
        thr = lax.bitcast_convert_type(cand, f32)
        cnt = jnp.sum(jnp.where(aff_ref[...] >= thr, 1.0, 0.0), axis=1, keepdims=True)
        return jnp.where(cnt >= capf, cand, t)

    t = lax.fori_loop(0, 31, bit_step, jnp.zeros((rows, 1), jnp.int32))
    thr = lax.bitcast_convert_type(t, f32)
    need = capf - jnp.sum(jnp.where(aff_ref[...] > thr, 1.0, 0.0), axis=1, keepdims=True)

    blk = MOE_BLOCK
    tri = jnp.where(
        lax.broadcasted_iota(jnp.int32, (blk, blk), 0) < lax.broadcasted_iota(jnp.int32, (blk, blk), 1),
        1.0, 0.0).astype(bf16)
    carry_gt = jnp.zeros((rows, 1), f32)
    carry_eq = jnp.zeros((rows, 1), f32)
    lane = lax.broadcasted_iota(jnp.int32, (rows, LANES), 1)
    meta = jnp.zeros((rows, LANES), f32)
    cmax = jnp.zeros((rows, 1), f32)
    for b in range(n // blk):
        sl = slice(b * blk, (b + 1) * blk)
        ab = aff_ref[:, sl]
        gt = ab > thr
        eq = ab == thr
        gtb = jnp.where(gt, 1.0, 0.0)
        eqb = jnp.where(eq, 1.0, 0.0)
        pre_gt = _dot(gtb.astype(bf16), tri) + carry_gt
        pre_eq = _dot(eqb.astype(bf16), tri) + carry_eq
        meta = jnp.where(lane == b, carry_gt + jnp.minimum(carry_eq, need), meta)
        carry_gt = carry_gt + jnp.sum(gtb, axis=1, keepdims=True)
        carry_eq = carry_eq + jnp.sum(eqb, axis=1, keepdims=True)
        sel = gt | (eq & (pre_eq < need))
        slot = pre_gt + jnp.minimum(pre_eq, need)
        pos_ref[:, sl] = jnp.where(sel, slot, -1.0).astype(jnp.int32)
        cmax = jnp.maximum(cmax, jnp.sum(jnp.where(sel, 1.0, 0.0), axis=1, keepdims=True))
    meta_ref[...] = jnp.where(lane == LANES - 1, cmax, meta).astype(jnp.int32)


def _route(lg3, *, cap):
    nseq, n, _ = lg3.shape
    rows = nseq * N_EXPERTS
    assert n // MOE_BLOCK < LANES
    return pl.pallas_call(
        functools.partial(_route_kernel, nseq=nseq, n=n, cap=cap),
        grid=(1,),
        in_specs=[_const_spec(lg3.shape)],
        out_specs=(_const_spec((rows, n)), _const_spec((rows, n)), _const_spec((rows, LANES))),
        out_shape=(jax.ShapeDtypeStruct((rows, n), jnp.int32), jax.ShapeDtypeStruct((rows, n), f32),
                   jax.ShapeDtypeStruct((rows, LANES), jnp.int32)),
        compiler_params=_params("arbitrary"),
        name="route",
    )(lg3)


def _slot_hits(pos_ref, e, cap):
    width = pos_ref.shape[1]
    return lax.broadcasted_iota(jnp.int32, (cap, width), 0) == pos_ref[e:e + 1, :]


def _one_hot(hits):
    return jnp.concatenate([jnp.where(h, 1.0, 0.0).astype(bf16) for h in hits], axis=0)


def _gather_dense(pos_ref, aff_ref, h2_ref, xg_ref, vals_ref, rows, cap, ne):
    for e0 in range(0, N_EXPERTS, ne):
        hits = [_slot_hits(pos_ref, e0 + e, cap) for e in range(ne)]
        xg = _dot(_one_hot(hits), h2_ref[...]).astype(xg_ref.dtype)
        for e in range(ne):
            xg_ref[e0 + e, rows, :] = xg[e * cap:(e + 1) * cap, :]
            vals = jnp.sum(jnp.where(hits[e], aff_ref[e0 + e:e0 + e + 1, :], 0.0), axis=1, keepdims=True)
            vals_ref[e0 + e, rows, :] = jnp.broadcast_to(vals, (cap, LANES))


def _gather_kernel(pos_ref, aff_ref, h2_ref, xg_ref, vals_ref, *, cap, ne, sb):
    for g in range(sb):
        _gather_dense(pos_ref.at[g], aff_ref.at[g], h2_ref.at[g], xg_ref, vals_ref,
                      slice(g * cap, (g + 1) * cap), cap, ne)


def _gather(pos3, aff3, h23, *, cap, ne, sb):
    nseq, _, n = pos3.shape
    return pl.pallas_call(
        functools.partial(_gather_kernel, cap=cap, ne=ne, sb=sb),
        grid=(nseq // sb,),
        in_specs=[
            pl.BlockSpec((sb, N_EXPERTS, n), lambda s: (s, 0, 0)),
            pl.BlockSpec((sb, N_EXPERTS, n), lambda s: (s, 0, 0)),
            pl.BlockSpec((sb, n, D_MODEL), lambda s: (s, 0, 0)),
        ],
        out_specs=(
            pl.BlockSpec((N_EXPERTS, sb * cap, D_MODEL), lambda s: (0, s, 0)),
            pl.BlockSpec((N_EXPERTS, sb * cap, LANES), lambda s: (0, s, 0)),
        ),
        out_shape=(
            jax.ShapeDtypeStruct((N_EXPERTS, nseq * cap, D_MODEL), bf16),
            jax.ShapeDtypeStruct((N_EXPERTS, nseq * cap, LANES), f32),
        ),
        compiler_params=_params("parallel"),
        name="gather",
    )(pos3, aff3, h23)


def _windows_fit(meta_ref):
    cmax = meta_ref[0, LANES - 1]
    for e in range(1, N_EXPERTS):
        cmax = jnp.maximum(cmax, meta_ref[e, LANES - 1])
    return cmax <= MOE_WIN - WIN_ALIGN


def _win_base(meta_ref, e, b, cap):
    start = meta_ref[e, b]
    return pl.multiple_of(jnp.minimum(start - start % WIN_ALIGN, cap - MOE_WIN), WIN_ALIGN)


def _win_hits(pos_row, base):
    slot = lax.broadcasted_iota(jnp.int32, (MOE_WIN, pos_row.shape[1]), 0) + base
    return slot == pos_row


def _gather_win_kernel(meta_ref, pos_ref, aff_ref, h2_ref, xg_ref, vals_ref, *, n, cap):
    fits = _windows_fit(meta_ref)

    @pl.when(fits)
    def _():
        xg_ref[...] = jnp.zeros(xg_ref.shape, xg_ref.dtype)
        vals_ref[...] = jnp.zeros(vals_ref.shape, vals_ref.dtype)

        def block(b, carry):
            cols = pl.ds(pl.multiple_of(b * MOE_BLOCK, MOE_BLOCK), MOE_BLOCK)
            bases = [_win_base(meta_ref, e, b, cap) for e in range(N_EXPERTS)]
            hits = [_win_hits(pos_ref[e:e + 1, cols], bases[e]) for e in range(N_EXPERTS)]
            part = _dot(_one_hot(hits), h2_ref[cols, :])
            for e in range(N_EXPERTS):
                rows = pl.ds(bases[e], MOE_WIN)
                xg_ref[e, rows, :] += part[e * MOE_WIN:(e + 1) * MOE_WIN, :].astype(xg_ref.dtype)
                vals = jnp.sum(jnp.where(hits[e], aff_ref[e:e + 1, cols], 0.0), axis=1, keepdims=True)
                vals_ref[e, rows, :] += jnp.broadcast_to(vals, (MOE_WIN, LANES))
            return carry

        lax.fori_loop(0, n // MOE_BLOCK, block, 0)

    @pl.when(jnp.logical_not(fits))
    def _():
        _gather_dense(pos_ref, aff_ref, h2_ref, xg_ref, vals_ref, slice(0, cap), cap, 1)


def _gather_win(meta, pos3, aff3, h23, *, cap):
    nseq, _, n = pos3.shape
    return pl.pallas_call(
        functools.partial(_gather_win_kernel, n=n, cap=cap),
        grid=(nseq,),
        in_specs=[
            pl.BlockSpec((N_EXPERTS, LANES), lambda s: (s, 0), memory_space=pltpu.SMEM),
            pl.BlockSpec((None, N_EXPERTS, n), lambda s: (s, 0, 0)),
            pl.BlockSpec((None, N_EXPERTS, n), lambda s: (s, 0, 0)),
            pl.BlockSpec((None, n, D_MODEL), lambda s: (s, 0, 0)),
        ],
        out_specs=(
            pl.BlockSpec((N_EXPERTS, cap, D_MODEL), lambda s: (0, s, 0)),
            pl.BlockSpec((N_EXPERTS, cap, LANES), lambda s: (0, s, 0)),
        ),
        out_shape=(
            jax.ShapeDtypeStruct((N_EXPERTS, nseq * cap, D_MODEL), bf16),
            jax.ShapeDtypeStruct((N_EXPERTS, nseq * cap, LANES), f32),
        ),
        compiler_params=_params("parallel"),
        name="gather_win",
    )(meta, pos3, aff3, h23)


def _experts_kernel(xp_ref, vp_ref, xs_ref, vs_ref, wg_ref, wu_ref, wd_ref, yp_ref, ys_ref, *, rc):
    wg = wg_ref[...].astype(bf16)
    wu = wu_ref[...].astype(bf16)
    wd = wd_ref[...].astype(bf16)
    for x_ref, v_ref, y_ref in ((xp_ref, vp_ref, yp_ref), (xs_ref, vs_ref, ys_ref)):
        for r0 in range(0, x_ref.shape[0], rc):
            x = x_ref[r0:r0 + rc, :]
            a = _dot(x, wg)
            u = _dot(x, wu)
            hm = (a * jax.nn.sigmoid(a) * u).astype(bf16)
            y = _dot(hm, wd) * v_ref[r0:r0 + rc, 0:1]
            y_ref[r0:r0 + rc, :] = y.astype(y_ref.dtype)


def _experts(xg_p, vals_p, xg_s, vals_s, wg, wu, wd, *, rc=EXPERT_ROWS):
    rp = xg_p.shape[1]
    rs = xg_s.shape[1]
    per_e = lambda r, w: pl.BlockSpec((None, r, w), lambda e: (e, 0, 0))
    return pl.pallas_call(
        functools.partial(_experts_kernel, rc=rc),
        grid=(N_EXPERTS,),
        in_specs=[
            per_e(rp, D_MODEL), per_e(rp, LANES), per_e(rs, D_MODEL), per_e(rs, LANES),
            per_e(D_MODEL, EXPERT_FF), per_e(D_MODEL, EXPERT_FF), per_e(EXPERT_FF, D_MODEL),
        ],
        out_specs=(per_e(rp, D_MODEL), per_e(rs, D_MODEL)),
        out_shape=(
            jax.ShapeDtypeStruct((N_EXPERTS, rp, D_MODEL), bf16),
            jax.ShapeDtypeStruct((N_EXPERTS, rs, D_MODEL), bf16),
        ),
        compiler_params=_params("parallel"),
        name="experts",
    )(xg_p, vals_p, xg_s, vals_s, wg, wu, wd)


def _scatter_dense(pos_ref, y_ref, rows, tn, cap, ne):
    moe = jnp.zeros((tn, D_MODEL), f32)
    for e0 in range(0, N_EXPERTS, ne):
        onehot = _one_hot([_slot_hits(pos_ref, e0 + e, cap) for e in range(ne)])
        y = jnp.concatenate([y_ref[e0 + e, rows, :] for e in range(ne)], axis=0)
        moe = moe + lax.dot_general(onehot, y, (((0,), (0,)), ((), ())), preferred_element_type=f32)
    return moe


def _scatter_kernel(pos_ref, y_ref, x1_ref, m_ref, fn_ref, o_ref, *, tn, cap, ne, sb):
    g2 = m_ref[5:6, :]
    for g in range(sb):
        moe = _scatter_dense(pos_ref.at[g], y_ref, slice(g * cap, (g + 1) * cap), tn, cap, ne)
        o_ref[g] = _rms(x1_ref[g] + g2 * moe, fn_ref[...])


def _scatter(pos3, y, x13, m3, fn, *, cap, ne, tn, sb):
    nseq, _, n = pos3.shape
    assert sb == 1 or m3.shape[0] == 1, "sequences sharing a grid step must share their modulation rows"
    return pl.pallas_call(
        functools.partial(_scatter_kernel, tn=tn, cap=cap, ne=ne, sb=sb),
        grid=(nseq // sb, n // tn),
        in_specs=[
            pl.BlockSpec((sb, N_EXPERTS, tn), lambda s, i: (s, 0, i)),
            pl.BlockSpec((N_EXPERTS, sb * cap, D_MODEL), lambda s, i: (0, s, 0)),
            pl.BlockSpec((sb, tn, D_MODEL), lambda s, i: (s, i, 0)),
            _mod_spec(m3, lambda s, i: s),
            _const_spec((1, D_MODEL)),
        ],
        out_specs=pl.BlockSpec((sb, tn, D_MODEL), lambda s, i: (s, i, 0)),
        out_shape=jax.ShapeDtypeStruct((nseq, n, D_MODEL), f32),
        compiler_params=_params("parallel", "arbitrary"),
        name="scatter",
    )(pos3, y, x13, m3, fn)


def _scatter_win_kernel(meta_ref, pos_ref, y_ref, x1_ref, m_ref, fn_ref, o_ref, moe_scr, *, cap):
    b = pl.program_id(1)
    fits = _windows_fit(meta_ref)

    @pl.when(fits)
    def _():
        bases = [_win_base(meta_ref, e, b, cap) for e in range(N_EXPERTS)]
        onehot = _one_hot([_win_hits(pos_ref[e:e + 1, :], bases[e]) for e in range(N_EXPERTS)])
        y = jnp.concatenate([y_ref[e, pl.ds(bases[e], MOE_WIN), :] for e in range(N_EXPERTS)], axis=0)
        moe_scr[...] = lax.dot_general(onehot, y, (((0,), (0,)), ((), ())), preferred_element_type=f32)

    @pl.when(jnp.logical_not(fits))
    def _():
        moe_scr[...] = _scatter_dense(pos_ref, y_ref, slice(0, cap), MOE_BLOCK, cap, 1)

    o_ref[...] = _rms(x1_ref[...] + m_ref[5:6, :] * moe_scr[...], fn_ref[...])


def _scatter_win(meta, pos3, y, x13, m3, fn, *, cap):
    nseq, _, n = pos3.shape
    tn = MOE_BLOCK
    return pl.pallas_call(
        functools.partial(_scatter_win_kernel, cap=cap),
        grid=(nseq, n // tn),
        in_specs=[
            pl.BlockSpec((N_EXPERTS, LANES), lambda s, i: (s, 0), memory_space=pltpu.SMEM),
            pl.BlockSpec((None, N_EXPERTS, tn), lambda s, i: (s, 0, i)),
            pl.BlockSpec((N_EXPERTS, cap, D_MODEL), lambda s, i: (0, s, 0)),
            pl.BlockSpec((None, tn, D_MODEL), lambda s, i: (s, i, 0)),
            _mod_spec(m3, lambda s, i: s),
            _const_spec((1, D_MODEL)),
        ],
        out_specs=pl.BlockSpec((None, tn, D_MODEL), lambda s, i: (s, i, 0)),
        out_shape=jax.ShapeDtypeStruct((nseq, n, D_MODEL), f32),
        scratch_shapes=[pltpu.VMEM((tn, D_MODEL), f32)],
        compiler_params=_params("parallel", "arbitrary"),
        name="scatter_win",
    )(meta, pos3, y, x13, m3, fn)


def _rope_tables(n):
    t = np.arange(n)
    half = QK_ROPE // 2
    freqs = ROPE_BASE ** (-np.arange(0, half, 2, dtype=np.float64) / half)
    ang_r = (t // GRID_W)[:, None] * freqs
    ang_c = (t % GRID_W)[:, None] * freqs
    cr, sr, cc, sc = np.cos(ang_r), np.sin(ang_r), np.cos(ang_c), np.sin(ang_c)
    cos = np.ones((n, HEAD_W))
    sin = np.zeros((n, HEAD_W))
    cos[:, ROPE_OFF:ROPE_OFF + QK_ROPE] = np.concatenate([cr, cr, cc, cc], axis=-1)
    sin[:, ROPE_OFF:ROPE_OFF + QK_ROPE] = np.concatenate([-sr, sr, -sc, sc], axis=-1)
    return jnp.asarray(cos, f32), jnp.asarray(sin, f32)


_PARTNER = np.concatenate([np.arange(8, 16), np.arange(0, 8), np.arange(24, 32), np.arange(16, 24)])


def _rope_partner(w):
    q = QK_ROPE // 4
    return jnp.concatenate([w[..., q:2 * q], w[..., 0:q], w[..., 3 * q:4 * q], w[..., 2 * q:3 * q]], axis=-1)


def _rope_placement():
    place = np.zeros((LANES, 2 * LANES), np.float32)
    d = np.arange(QK_ROPE)
    place[d, ROPE_OFF + d] = 1.0
    place[_PARTNER, LANES + ROPE_OFF + d] = 1.0
    return jnp.asarray(place, bf16)


def _head_blocks(w_nope, w_rope):
    rows = w_nope.shape[0]
    if w_rope is None:
        w_rope = jnp.zeros((rows, N_HEADS, QK_ROPE), w_nope.dtype)
    z = jnp.zeros((rows, N_HEADS, HEAD_W - QK_NOPE - QK_ROPE), w_nope.dtype)
    return jnp.concatenate([w_nope, w_rope, z], axis=-1).reshape(rows, N_HEADS * HEAD_W)


def kernel(x_prompt, x_sample, cache_ckv, cache_kpe, c, c_ctx, w_ada, b_ada, norm1, w_in, conv_dw, conv_dw_b,
           conv_ln_g, conv_ln_b, w_conv_out, q_norm, w_qb, kv_norm, w_kvb, w_o_mla, w_out, norm2, w_router,
           w_e_gate, w_e_up, w_e_down, final_norm):
    assert w_ada.shape[0] == 1, "single trunk layer"
    nb_p, n_p, _ = x_prompt.shape
    nb_s, n_s, _ = x_sample.shape

    win = _wprep(w_in[0].T, _rope_placement())
    wq = w_qb[0].reshape(Q_LORA, N_HEADS, QK_NOPE + QK_ROPE)
    wqb = _head_blocks(wq[..., :QK_NOPE], wq[..., QK_NOPE:]).astype(bf16)
    wqbp = _head_blocks(jnp.zeros_like(wq[..., :QK_NOPE]), _rope_partner(wq[..., QK_NOPE:])).astype(bf16)
    wkv = w_kvb[0].reshape(KV_LORA, N_HEADS, QK_NOPE + V_DIM)
    wk = _head_blocks(wkv[..., :QK_NOPE], None).astype(bf16)
    wv = wkv[..., QK_NOPE:].reshape(KV_LORA, N_HEADS * V_DIM).astype(bf16)
    wco = w_conv_out[0].astype(bf16)
    wo = w_o_mla[0].astype(bf16)
    wout = w_out[0].astype(bf16)
    wr_hi = w_router[0].astype(bf16)
    wr_lo = (w_router[0] - wr_hi.astype(f32)).astype(bf16)
    wr = jnp.concatenate([wr_hi, wr_lo, jnp.zeros((D_MODEL, LANES - 2 * N_EXPERTS), bf16)], axis=-1)
    row = lambda a: a.reshape(1, -1)

    mod = jnp.concatenate([c_ctx[None, :], c, jnp.zeros((8 - 1 - nb_s, D_MODEL), f32)], axis=0)
    m = _ada(mod, w_ada[0], b_ada[0]).reshape(8, 6, D_MODEL)
    m_p, m_s = m[0:1], m[1:1 + nb_s]
    cos, sin = _rope_tables(n_s)

    conv_w = _conv_weights(conv_dw[0], row(conv_dw_b[0]), row(conv_ln_g[0]), row(conv_ln_b[0]), wco, CONV_CHUNK)

    def mixers(x, m3, rope, ctx_ckv, ctx_kpe):
        nseq, n, _ = x.shape
        x2d = x.reshape(nseq * n, D_MODEL)
        fuse_conv = IN_TILE % n == 0
        cm, q, ckv, kr, g, *kpe = _inproj(x2d, m3, row(norm1[0]), win, row(q_norm[0]), wqb, wqbp, row(kv_norm[0]),
                                          cos, sin, conv_w if fuse_conv else None, n=n, rope=rope, tm=IN_TILE,
                                          ct=CONV_CHUNK)
        if not fuse_conv:
            cm = _conv(cm.reshape(nseq, n, CONV_DIM), g.reshape(nseq, n, 2 * D_MODEL), conv_w, ct=CONV_CHUNK)
        keys_ckv = ckv.reshape(nseq, n, KV_LORA)
        keys_kpe = kr.reshape(nseq, n, LANES)
        if ctx_ckv is not None:
            keys_ckv = jnp.concatenate([ctx_ckv.astype(keys_ckv.dtype), keys_ckv], axis=1)
            keys_kpe = jnp.concatenate([ctx_kpe.astype(keys_kpe.dtype), keys_kpe], axis=1)
        x1, h2, lg = _attn(q, keys_ckv, keys_kpe, cm.reshape(nseq * n, D_MODEL), g, x2d, m3, wk, wv, wo, wout,
                           row(norm2[0]), wr, n=n, tq=min(n, Q_TILE))
        return x1, h2, lg, ckv, kpe

    ctx_kpe = jnp.pad(cache_kpe[:, 0], ((0, 0), (0, 0), (ROPE_OFF, LANES - ROPE_OFF - QK_ROPE)))
    x1_p, h2_p, lg_p, ckv_p, (kpe_p,) = mixers(x_prompt, m_p, False, None, None)
    x1_s, h2_s, lg_s, _, _ = mixers(x_sample, m_s, True, cache_ckv[:, 0], ctx_kpe)

    def moe_tiles(n):
        cap = EC_FACTOR * n // N_EXPERTS
        ne = N_EXPERTS if N_EXPERTS * cap <= MOE_ROWS else 1
        sb = max(1, MOE_TOKENS // n)
        return cap, ne, sb

    def windowed(n):
        return n >= 4 * MOE_BLOCK and EC_FACTOR * n // N_EXPERTS >= 2 * MOE_WIN

    def route_gather(h2, lg, nseq, n):
        cap, ne, sb = moe_tiles(n)
        pos, aff, meta = _route(lg.reshape(nseq, n, LANES), cap=cap)
        pos3 = pos.reshape(nseq, N_EXPERTS, n)
        aff3 = aff.reshape(nseq, N_EXPERTS, n)
        h23 = h2.reshape(nseq, n, D_MODEL)
        if windowed(n):
            xg, vals = _gather_win(meta, pos3, aff3, h23, cap=cap)
        else:
            xg, vals = _gather(pos3, aff3, h23, cap=cap, ne=ne, sb=sb)
        return pos3, meta, xg, vals

    pos_p, meta_p, xg_p, vals_p = route_gather(h2_p, lg_p, nb_p, n_p)
    pos_s, meta_s, xg_s, vals_s = route_gather(h2_s, lg_s, nb_s, n_s)
    y_p, y_s = _experts(xg_p, vals_p, xg_s, vals_s, w_e_gate[0], w_e_up[0], w_e_down[0])
    fn = row(final_norm)

    def scatter(pos, meta, y, x1, m3, nseq, n):
        cap, ne, sb = moe_tiles(n)
        x13 = x1.reshape(nseq, n, D_MODEL)
        if windowed(n):
            return _scatter_win(meta, pos, y, x13, m3, fn, cap=cap)
        return _scatter(pos, y, x13, m3, fn, cap=cap, ne=ne, tn=min(n, SCATTER_TILE), sb=sb)

    y_prompt = scatter(pos_p, meta_p, y_p, x1_p, m_p, nb_p, n_p)
    y_sample = scatter(pos_s, meta_s, y_s, x1_s, m_s, nb_s, n_s)

    new_ckv = ckv_p.reshape(nb_p, 1, n_p, KV_LORA)
    new_kpe = kpe_p.reshape(nb_p, 1, n_p, QK_ROPE)
    return (y_prompt, y_sample, new_ckv, new_kpe)
```

```python
import functools

import jax
import jax.numpy as jnp
import numpy as np
from jax import lax
from jax.experimental import pallas as pl
from jax.experimental.pallas import tpu as pltpu

D_MODEL = 1024
GRID_W = 64
CONV_DIM = 512
CONV_WIDTH = 31
N_HEADS = 8
QK_NOPE = 64
QK_ROPE = 32
V_DIM = 64
Q_LORA = 256
KV_LORA = 128
N_EXPERTS = 16
EXPERT_FF = 512
EC_FACTOR = 2
ROPE_BASE = 10000.0
EPS = 1e-6

LANES = 128
HEAD_W = LANES
ROPE_OFF = QK_NOPE
CONV_HALO = 16
LOG2E = 1.4426950408889634
VMEM_LIMIT = 48 * 1024 * 1024
IN_TILE = 512
CONV_CHUNK = 64
CONV_OUT_ROWS = 256
GATE_CHUNK = 512
EXPERT_ROWS = 512
Q_TILE = 512
ATTN_TOKENS = 1024
SCORE_BYTES = 12 * 1024 * 1024
MOE_ROWS = 512
MOE_TOKENS = 1024
SCATTER_TILE = 512
MOE_BLOCK = 2 * LANES
MOE_WIN = 80
WIN_ALIGN = 16

C_CONV = 0
C_QA = 2 * CONV_DIM
C_KVA = C_QA + Q_LORA
C_KR = C_KVA + KV_LORA
C_GATE = C_KR + LANES
C_END = C_GATE + 2 * D_MODEL
C_KRP = C_END

f32 = jnp.float32
bf16 = jnp.bfloat16


def _params(*sem):
    return pltpu.CompilerParams(dimension_semantics=sem, vmem_limit_bytes=VMEM_LIMIT)


def _dot(a, b):
    return jnp.dot(a, b, preferred_element_type=f32)


def _rms(x, g):
    return x * lax.rsqrt(jnp.mean(x * x, axis=-1, keepdims=True) + EPS) * g


def _const_spec(shape):
    nd = len(shape)
    return pl.BlockSpec(shape, lambda *_: (0,) * nd)


def _mod_spec(m3, seq_of):
    if m3.shape[0] == 1:
        return _const_spec((None, 6, D_MODEL))
    return pl.BlockSpec((None, 6, D_MODEL), lambda *idx: (seq_of(*idx), 0, 0))


def _ada_kernel(s_ref, w_ref, b_ref, o_ref):
    s = s_ref[...]
    s = s * jax.nn.sigmoid(s)
    o_ref[...] = _dot(s.astype(bf16), w_ref[...].astype(bf16)) + b_ref[...]


def _ada(mod, w_ada, b_ada):
    rows = mod.shape[0]
    n_out = w_ada.shape[1]
    tn = D_MODEL
    return pl.pallas_call(
        _ada_kernel,
        grid=(n_out // tn,),
        in_specs=[
            _const_spec((rows, D_MODEL)),
            pl.BlockSpec((D_MODEL, tn), lambda j: (0, j)),
            pl.BlockSpec((1, tn), lambda j: (0, j)),
        ],
        out_specs=pl.BlockSpec((rows, tn), lambda j: (0, j)),
        out_shape=jax.ShapeDtypeStruct((rows, n_out), f32),
        compiler_params=_params("arbitrary"),
        name="ada",
    )(mod, w_ada, b_ada.reshape(1, n_out))


def _wprep_kernel(wt_ref, place_ref, o_ref):
    def block(r0):
        return wt_ref[r0:r0 + LANES, :].T.astype(bf16)

    for j in range(C_KR // LANES):
        o_ref[:, j * LANES:(j + 1) * LANES] = block(j * LANES)
    placed = _dot(block(C_KR), place_ref[...])
    o_ref[:, C_KR:C_GATE] = placed[:, 0:LANES].astype(bf16)
    o_ref[:, C_KRP:C_KRP + LANES] = placed[:, LANES:2 * LANES].astype(bf16)
    for j in range(2 * D_MODEL // LANES):
        o_ref[:, C_GATE + j * LANES:C_GATE + (j + 1) * LANES] = block(C_KR + QK_ROPE + j * LANES)


def _wprep(w_in_t, place):
    cols, rows = w_in_t.shape
    return pl.pallas_call(
        _wprep_kernel,
        grid=(rows // LANES,),
        in_specs=[pl.BlockSpec((cols, LANES), lambda i: (0, i)), _const_spec(place.shape)],
        out_specs=pl.BlockSpec((LANES, C_KRP + LANES), lambda i: (i, 0)),
        out_shape=jax.ShapeDtypeStruct((rows, C_KRP + LANES), bf16),
        compiler_params=_params("parallel"),
        name="wprep",
    )(w_in_t, place)


def _conv_taps(vpad, r0, shift_ref, dw_ref, dwb_ref, ybuf, y0, ct):
    pad = CONV_WIDTH // 2
    sub = 8
    span = ((CONV_HALO - pad + CONV_WIDTH - 1) // sub) * sub
    cw = 2 * LANES
    for cb in range(CONV_DIM // cw):
        sl = slice(cb * cw, (cb + 1) * cw)
        win = vpad[pl.ds(r0, ct + 2 * CONV_HALO), sl]
        acc = jnp.zeros((ct // sub, sub, cw), f32)
        for ph in range(sub):
            wph = _dot(shift_ref[ph], win)
            for a in range(span // sub + 1):
                k = a * sub + ph - (CONV_HALO - pad)
                if 0 <= k < CONV_WIDTH:
                    acc = acc + wph[a * sub:a * sub + ct, :].reshape(ct // sub, sub, cw) * dw_ref[k, :, sl][None]
        ybuf[pl.ds(y0, ct), sl] = acc.reshape(ct, cw) + dwb_ref[:, sl]


def _conv_out(y, lng_ref, lnb_ref, wco_ref, gate):
    mu = jnp.mean(y, axis=-1, keepdims=True)
    yc = y - mu
    var = jnp.mean(yc * yc, axis=-1, keepdims=True)
    z = yc * lax.rsqrt(var + EPS) * lng_ref[...] + lnb_ref[...]
    z = z * jax.nn.sigmoid(z)
    return (gate.astype(f32) * _dot(z.astype(bf16), wco_ref[...])).astype(bf16)


def _fill_padded(vpad, v, n):
    zeros = jnp.zeros((CONV_HALO, CONV_DIM), vpad.dtype)
    vpad[0:CONV_HALO, :] = zeros
    vpad[CONV_HALO + n:2 * CONV_HALO + n, :] = zeros
    vpad[CONV_HALO:CONV_HALO + n, :] = v


def _inproj_kernel(*refs, rope, halo, tiles_per_seq, q_scale, n, rt, ct):
    refs = list(refs)
    x_ref = refs.pop(0)
    if halo:
        xp_ref, xn_ref = refs[:2]
        del refs[:2]
    m_ref, n1_ref, win_ref, qn_ref, wqb_ref = refs[:5]
    del refs[:5]
    if rope:
        wqbp_ref = refs.pop(0)
    kvn_ref = refs.pop(0)
    if rope:
        cos_ref, sin_ref = refs[:2]
        del refs[:2]
    shift_ref, dw_ref, dwb_ref, lng_ref, lnb_ref, wco_ref = refs[:6]
    del refs[:6]
    cm_ref, q_ref, ckv_ref, kr_ref, g_ref = refs[:5]
    del refs[:5]
    if not rope:
        kpe_ref = refs.pop(0)
    vpad, ybuf, gc = refs
    tm = x_ref.shape[0]
    lo = CONV_HALO if halo else 0

    sh1 = m_ref[0:1, :]
    sc1 = m_ref[1:2, :]
    x = jnp.concatenate([xp_ref[...], x_ref[...], xn_ref[...]], axis=0) if halo else x_ref[...]
    hb_all = (_rms(x, n1_ref[...]) * (1.0 + sc1) + sh1).astype(bf16)

    left_all = _dot(hb_all, win_ref[:, 0:C_GATE])
    v = (left_all[:, C_CONV:C_CONV + CONV_DIM] * jax.nn.sigmoid(left_all[:, C_CONV + CONV_DIM:C_QA])).astype(bf16)
    hb = hb_all[lo:lo + tm, :]
    left = left_all[lo:lo + tm, :]

    if halo:
        j = pl.program_id(0) % tiles_per_seq
        zeros = jnp.zeros((CONV_HALO, CONV_DIM), bf16)
        vpad[0, 0:lo, :] = jnp.where(j > 0, v[0:lo, :], zeros)
        vpad[0, lo:lo + tm, :] = v[lo:lo + tm, :]
        vpad[0, lo + tm:2 * lo + tm, :] = jnp.where(j < tiles_per_seq - 1, v[lo + tm:2 * lo + tm, :], zeros)
        chunks = [(0, c * ct, c * ct) for c in range(tm // ct)]
    else:
        for s in range(tm // n):
            _fill_padded(vpad.at[s], v[s * n:(s + 1) * n, :], n)
        chunks = [(s, c * ct, s * n + c * ct) for s in range(tm // n) for c in range(n // ct)]

    def conv_some(count):
        for _ in range(min(count, len(chunks))):
            s, r0, y0 = chunks.pop(0)
            _conv_taps(vpad.at[s], r0, shift_ref, dw_ref, dwb_ref, ybuf, y0, ct)

    steps = 2 + 2 * D_MODEL // GATE_CHUNK
    per_step = -(-len(chunks) // steps)

    qa = left[:, C_QA:C_KVA]
    qn = _rms(qa, qn_ref[...]).astype(bf16)
    q = _dot(qn, wqb_ref[...])
    if rope:
        qp = _dot(qn, wqbp_ref[...])
        cos = cos_ref[...]
        sin = sin_ref[...]
        for hd in range(N_HEADS):
            sl = slice(hd * HEAD_W, (hd + 1) * HEAD_W)
            q_ref[:, sl] = ((q[:, sl] * cos + qp[:, sl] * sin) * q_scale).astype(q_ref.dtype)
    else:
        q_ref[...] = (q * q_scale).astype(q_ref.dtype)
    conv_some(per_step)

    kva = left[:, C_KVA:C_KR]
    ckv_ref[...] = _rms(kva, kvn_ref[...]).astype(ckv_ref.dtype)

    kr = left[:, C_KR:C_GATE]
    if rope:
        krp = _dot(hb, win_ref[:, C_KRP:C_KRP + LANES])
        kr = kr * cos_ref[...] + krp * sin_ref[...]
    else:
        kpe_ref[...] = kr[:, ROPE_OFF:ROPE_OFF + QK_ROPE]
    kr_ref[...] = kr.astype(kr_ref.dtype)
    conv_some(per_step)

    gw = GATE_CHUNK
    for jg in range(2 * D_MODEL // gw):
        gate = jax.nn.sigmoid(_dot(hb, win_ref[:, C_GATE + jg * gw:C_GATE + (jg + 1) * gw])).astype(bf16)
        if jg * gw < D_MODEL:
            gc[:, jg * gw:(jg + 1) * gw] = gate
        else:
            g_ref[:, jg * gw - D_MODEL:(jg + 1) * gw - D_MODEL] = gate
        conv_some(per_step)

    conv_some(len(chunks))
    for r0 in range(0, tm, rt):
        cm_ref[r0:r0 + rt, :] = _conv_out(ybuf[r0:r0 + rt, :], lng_ref, lnb_ref, wco_ref, gc[r0:r0 + rt, :])


def _inproj(x2d, m3, norm1, win, q_norm, wqb, wqbp, kv_norm, cos, sin, conv_w, *, n, rope, tm,
            rt=CONV_OUT_ROWS, ct=CONV_CHUNK):
    tokens = x2d.shape[0]
    halo = n > tm
    assert n % tm == 0 if halo else (tm % n == 0 and m3.shape[0] == 1 and not rope)
    tiles_per_seq = max(1, n // tm)
    q_scale = float((QK_NOPE + QK_ROPE) ** -0.5 * LOG2E)
    tile = lambda w: pl.BlockSpec((tm, w), lambda i: (i, 0))
    in_specs = [tile(D_MODEL)]
    args = [x2d]
    if halo:
        hb_tile = tm // CONV_HALO
        hb_seq = n // CONV_HALO
        prev = lambda i: (jnp.maximum(i * hb_tile - 1, (i // tiles_per_seq) * hb_seq), 0)
        nxt = lambda i: (jnp.minimum((i + 1) * hb_tile, (i // tiles_per_seq + 1) * hb_seq - 1), 0)
        in_specs += [pl.BlockSpec((CONV_HALO, D_MODEL), prev), pl.BlockSpec((CONV_HALO, D_MODEL), nxt)]
        args += [x2d, x2d]
    in_specs += [
        _mod_spec(m3, lambda i: i // tiles_per_seq),
        _const_spec((1, D_MODEL)),
        _const_spec((D_MODEL, C_KRP + LANES if rope else C_END)),
        _const_spec((1, Q_LORA)),
        _const_spec(wqb.shape),
    ]
    args += [m3, norm1, win, q_norm, wqb]
    if rope:
        in_specs.append(_const_spec(wqbp.shape))
        args.append(wqbp)
    in_specs.append(_const_spec((1, KV_LORA)))
    args.append(kv_norm)
    if rope:
        tab = pl.BlockSpec((tm, LANES), lambda i: (i % tiles_per_seq, 0))
        in_specs += [tab, tab]
        args += [cos, sin]
    args += list(conv_w)
    in_specs += [_const_spec(w.shape) for w in conv_w]
    pieces = max(1, tm // n)
    scratch = [
        pltpu.VMEM((pieces, tm // pieces + 2 * CONV_HALO, CONV_DIM), bf16),
        pltpu.VMEM((tm, CONV_DIM), f32),
        pltpu.VMEM((tm, D_MODEL), bf16),
    ]
    out_shape = [
        jax.ShapeDtypeStruct((tokens, D_MODEL), bf16),
        jax.ShapeDtypeStruct((tokens, N_HEADS * HEAD_W), bf16),
        jax.ShapeDtypeStruct((tokens, KV_LORA), bf16 if rope else f32),
        jax.ShapeDtypeStruct((tokens, LANES), bf16),
        jax.ShapeDtypeStruct((tokens, D_MODEL), bf16),
    ]
    out_specs = [tile(D_MODEL), tile(N_HEADS * HEAD_W), tile(KV_LORA), tile(LANES), tile(D_MODEL)]
    if not rope:
        out_shape.append(jax.ShapeDtypeStruct((tokens, QK_ROPE), f32))
        out_specs.append(tile(QK_ROPE))
    return pl.pallas_call(
        functools.partial(_inproj_kernel, rope=rope, halo=halo, tiles_per_seq=tiles_per_seq, q_scale=q_scale,
                          n=n, rt=rt, ct=ct),
        grid=(tokens // tm,),
        in_specs=in_specs,
        out_specs=out_specs,
        out_shape=out_shape,
        scratch_shapes=scratch,
        compiler_params=_params("parallel"),
        name="inproj_conv_rope" if rope else "inproj_conv",
    )(*args)


def _conv_weights(dw, dwb, lng, lnb, wco, ct):
    rows = ct + 2 * CONV_HALO
    i = np.arange(rows)
    shifts = jnp.asarray(np.stack([(i[None, :] == i[:, None] + ph) for ph in range(8)]), bf16)
    dw_tiles = jnp.broadcast_to(dw[:, None, :], (CONV_WIDTH, 8, CONV_DIM))
    return shifts, dw_tiles, dwb, lng, lnb, wco


def _attn_kernel(q_ref, ckv_ref, kpe_ref, cm_ref, g_ref, x_ref, m_ref, wk_ref, wv_ref, wo_ref, wout_ref, n2_ref,
                 wr_ref, x1_ref, h2_ref, lg_ref, k_scr, v_scr, q_scr, *, sb, hg):
    s_len = ckv_ref.shape[1]

    @pl.when(pl.program_id(1) == 0)
    def _():
        ckv = ckv_ref[...].reshape(sb * s_len, KV_LORA).astype(bf16)
        kpe = kpe_ref[...].reshape(sb * s_len, LANES).astype(f32)
        k = _dot(ckv, wk_ref[...])
        v = _dot(ckv, wv_ref[...])
        ones = jnp.ones((s_len, V_DIM), bf16)
        for g in range(sb):
            rows = slice(g * s_len, (g + 1) * s_len)
            for hd in range(N_HEADS):
                u = g * N_HEADS + hd
                k_scr[u] = (k[rows, hd * HEAD_W:(hd + 1) * HEAD_W] + kpe[rows, :]).astype(bf16)
                vh = v[rows, hd * V_DIM:(hd + 1) * V_DIM].astype(bf16)
                if hd % 2 == 0:
                    v_scr[u, :, 0:V_DIM] = vh
                    v_scr[u, :, V_DIM:LANES] = ones
                else:
                    v_scr[u, :, 0:V_DIM] = ones
                    v_scr[u, :, V_DIM:LANES] = vh

    tq = q_ref.shape[0] // sb
    units = sb * N_HEADS
    for g in range(sb):
        for hd in range(N_HEADS):
            q_scr[g * N_HEADS + hd] = q_ref[g * tq:(g + 1) * tq, hd * HEAD_W:(hd + 1) * HEAD_W]
    lane = lax.broadcasted_iota(jnp.int32, (tq, LANES), 1)
    pairs = []
    for u0 in range(0, units, hg):
        us = slice(u0, u0 + hg)
        s = lax.dot_general(q_scr[us], k_scr[us], (((2,), (2,)), ((0,), (0,))), preferred_element_type=f32)
        mx = jnp.max(s, axis=-1, keepdims=True)
        p = jnp.exp2(s - mx).astype(bf16)
        r = lax.dot_general(p, v_scr[us], (((2,), (1,)), ((0,), (0,))), preferred_element_type=f32)
        for j in range(hg // 2):
            re, ro = r[2 * j], r[2 * j + 1]
            oe = re * (1.0 / re[:, V_DIM:V_DIM + 1])
            oo = ro * (1.0 / ro[:, 0:1])
            pairs.append(jnp.where(lane < V_DIM, oe, oo))
    per_seq = N_HEADS // 2
    attn = jnp.concatenate(
        [jnp.concatenate(pairs[g * per_seq:(g + 1) * per_seq], axis=-1) for g in range(sb)], axis=0).astype(bf16)
    merged = (cm_ref[...].astype(f32) + g_ref[...].astype(f32) * _dot(attn, wo_ref[...])).astype(bf16)

    g1 = m_ref[2:3, :]
    sh2 = m_ref[3:4, :]
    sc2 = m_ref[4:5, :]
    x1 = x_ref[...] + g1 * _dot(merged, wout_ref[...])
    x1_ref[...] = x1
    h2 = _rms(x1, n2_ref[...]) * (1.0 + sc2) + sh2
    hi = h2.astype(bf16)
    h2_ref[...] = hi
    lo = (h2 - hi.astype(f32)).astype(bf16)
    both = _dot(jnp.concatenate([hi, lo], axis=0), wr_ref[...])
    rows = hi.shape[0]
    s = both[0:rows] + both[rows:2 * rows]
    lg_ref[...] = s + pltpu.roll(s, LANES - N_EXPERTS, axis=1)


def _attn(q2d, ckv3, kpe3, cm2d, g2d, x2d, m3, wk, wv, wo, wout, norm2, wr, *, n, tq):
    nseq, s_len, _ = ckv3.shape
    sb = max(1, ATTN_TOKENS // n) if (tq == n and m3.shape[0] == 1) else 1
    units = sb * N_HEADS
    hg = max(2, min(units, SCORE_BYTES // (tq * s_len * 4)))
    assert units % hg == 0 and hg % 2 == 0 and nseq % sb == 0
    qb = n // tq
    tokens = nseq * n
    tile = lambda w: pl.BlockSpec((sb * tq, w), lambda s, i: (s * qb + i, 0))
    return pl.pallas_call(
        functools.partial(_attn_kernel, sb=sb, hg=hg),
        grid=(nseq // sb, qb),
        in_specs=[
            tile(N_HEADS * HEAD_W),
            pl.BlockSpec((sb, s_len, KV_LORA), lambda s, i: (s, 0, 0)),
            pl.BlockSpec((sb, s_len, LANES), lambda s, i: (s, 0, 0)),
            tile(D_MODEL),
            tile(D_MODEL),
            tile(D_MODEL),
            _mod_spec(m3, lambda s, i: s),
            _const_spec(wk.shape),
            _const_spec(wv.shape),
            _const_spec(wo.shape),
            _const_spec(wout.shape),
            _const_spec((1, D_MODEL)),
            _const_spec(wr.shape),
        ],
        out_specs=(tile(D_MODEL), tile(D_MODEL), tile(LANES)),
        out_shape=(
            jax.ShapeDtypeStruct((tokens, D_MODEL), f32),
            jax.ShapeDtypeStruct((tokens, D_MODEL), bf16),
            jax.ShapeDtypeStruct((tokens, LANES), f32),
        ),
        scratch_shapes=[
            pltpu.VMEM((units, s_len, HEAD_W), bf16),
            pltpu.VMEM((units, s_len, LANES), bf16),
            pltpu.VMEM((units, tq, HEAD_W), bf16),
        ],
        compiler_params=_params("parallel", "arbitrary"),
        name="attn_out",
    )(q2d, ckv3, kpe3, cm2d, g2d, x2d, m3, wk, wv, wo, wout, norm2, wr)


def _route_kernel(lg_ref, pos_ref, aff_ref, meta_ref, *, nseq, n, cap):
    for s in range(nseq):
        lt = lg_ref[s].T[0:N_EXPERTS, :]
        e = jnp.exp(lt - jnp.max(lt, axis=0, keepdims=True))
        aff_ref[s * N_EXPERTS:(s + 1) * N_EXPERTS, :] = e / jnp.sum(e, axis=0, keepdims=True)
    rows = nseq * N_EXPERTS
    capf = float(cap)

    def bit_step(i, t):
        cand = t | (jnp.int32(1) << (30 - i))
        thr = lax.bitcast_convert_type(cand, f32)
        cnt = jnp.sum(jnp.where(aff_ref[...] >= thr, 1.0, 0.0), axis=1, keepdims=True)
        return jnp.where(cnt >= capf, cand, t)

    t = lax.fori_loop(0, 31, bit_step, jnp.zeros((rows, 1), jnp.int32))
    thr = lax.bitcast_convert_type(t, f32)
    need = capf - jnp.sum(jnp.where(aff_ref[...] > thr, 1.0, 0.0), axis=1, keepdims=True)

    blk = MOE_BLOCK
    tri = jnp.where(
        lax.broadcasted_iota(jnp.int32, (blk, blk), 0) < lax.broadcasted_iota(jnp.int32, (blk, blk), 1),
        1.0, 0.0).astype(bf16)
    carry_gt = jnp.zeros((rows, 1), f32)
    carry_eq = jnp.zeros((rows, 1), f32)
    lane = lax.broadcasted_iota(jnp.int32, (rows, LANES), 1)
    meta = jnp.zeros((rows, LANES), f32)
    cmax = jnp.zeros((rows, 1), f32)
    for b in range(n // blk):
        sl = slice(b * blk, (b + 1) * blk)
        ab = aff_ref[:, sl]
        gt = ab > thr
        eq = ab == thr
        gtb = jnp.where(gt, 1.0, 0.0)
        eqb = jnp.where(eq, 1.0, 0.0)
        pre_gt = _dot(gtb.astype(bf16), tri) + carry_gt
        pre_eq = _dot(eqb.astype(bf16), tri) + carry_eq
        meta = jnp.where(lane == b, carry_gt + jnp.minimum(carry_eq, need), meta)
        carry_gt = carry_gt + jnp.sum(gtb, axis=1, keepdims=True)
        carry_eq = carry_eq + jnp.sum(eqb, axis=1, keepdims=True)
        sel = gt | (eq & (pre_eq < need))
        slot = pre_gt + jnp.minimum(pre_eq, need)
        pos_ref[:, sl] = jnp.where(sel, slot, -1.0).astype(jnp.int32)
        cmax = jnp.maximum(cmax, jnp.sum(jnp.where(sel, 1.0, 0.0), axis=1, keepdims=True))
    meta_ref[...] = jnp.where(lane == LANES - 1, cmax, meta).astype(jnp.int32)


def _route(lg3, *, cap):
    nseq, n, _ = lg3.shape
    rows = nseq * N_EXPERTS
    assert n // MOE_BLOCK < LANES
    return pl.pallas_call(
        functools.partial(_route_kernel, nseq=nseq, n=n, cap=cap),
        grid=(1,),
        in_specs=[_const_spec(lg3.shape)],
        out_specs=(_const_spec((rows, n)), _const_spec((rows, n)), _const_spec((rows, LANES))),
        out_shape=(jax.ShapeDtypeStruct((rows, n), jnp.int32), jax.ShapeDtypeStruct((rows, n), f32),
                   jax.ShapeDtypeStruct((rows, LANES), jnp.int32)),
        compiler_params=_params("arbitrary"),
        name="route",
    )(lg3)


def _slot_hits(pos_ref, e, cap):
    width = pos_ref.shape[1]
    return lax.broadcasted_iota(jnp.int32, (cap, width), 0) == pos_ref[e:e + 1, :]


def _one_hot(hits):
    return jnp.concatenate([jnp.where(h, 1.0, 0.0).astype(bf16) for h in hits], axis=0)


def _gather_dense(pos_ref, aff_ref, h2_ref, xg_ref, vals_ref, rows, cap, ne):
    for e0 in range(0, N_EXPERTS, ne):
        hits = [_slot_hits(pos_ref, e0 + e, cap) for e in range(ne)]
        xg = _dot(_one_hot(hits), h2_ref[...]).astype(xg_ref.dtype)
        for e in range(ne):
            xg_ref[e0 + e, rows, :] = xg[e * cap:(e + 1) * cap, :]
            vals = jnp.sum(jnp.where(hits[e], aff_ref[e0 + e:e0 + e + 1, :], 0.0), axis=1, keepdims=True)
            vals_ref[e0 + e, rows, :] = jnp.broadcast_to(vals, (cap, LANES))


def _gather_kernel(pos_ref, aff_ref, h2_ref, xg_ref, vals_ref, *, cap, ne, sb):
    for g in range(sb):
        _gather_dense(pos_ref.at[g], aff_ref.at[g], h2_ref.at[g], xg_ref, vals_ref,
                      slice(g * cap, (g + 1) * cap), cap, ne)


def _gather(pos3, aff3, h23, *, cap, ne, sb):
    nseq, _, n = pos3.shape
    return pl.pallas_call(
        functools.partial(_gather_kernel, cap=cap, ne=ne, sb=sb),
        grid=(nseq // sb,),
        in_specs=[
            pl.BlockSpec((sb, N_EXPERTS, n), lambda s: (s, 0, 0)),
            pl.BlockSpec((sb, N_EXPERTS, n), lambda s: (s, 0, 0)),
            pl.BlockSpec((sb, n, D_MODEL), lambda s: (s, 0, 0)),
        ],
        out_specs=(
            pl.BlockSpec((N_EXPERTS, sb * cap, D_MODEL), lambda s: (0, s, 0)),
            pl.BlockSpec((N_EXPERTS, sb * cap, LANES), lambda s: (0, s, 0)),
        ),
        out_shape=(
            jax.ShapeDtypeStruct((N_EXPERTS, nseq * cap, D_MODEL), bf16),
            jax.ShapeDtypeStruct((N_EXPERTS, nseq * cap, LANES), f32),
        ),
        compiler_params=_params("parallel"),
        name="gather",
    )(pos3, aff3, h23)


def _windows_fit(meta_ref):
    cmax = meta_ref[0, LANES - 1]
    for e in range(1, N_EXPERTS):
        cmax = jnp.maximum(cmax, meta_ref[e, LANES - 1])
    return cmax <= MOE_WIN - WIN_ALIGN


def _win_base(meta_ref, e, b, cap):
    start = meta_ref[e, b]
    return pl.multiple_of(jnp.minimum(start - start % WIN_ALIGN, cap - MOE_WIN), WIN_ALIGN)


def _win_hits(pos_row, base):
    slot = lax.broadcasted_iota(jnp.int32, (MOE_WIN, pos_row.shape[1]), 0) + base
    return slot == pos_row


def _gather_win_kernel(meta_ref, pos_ref, aff_ref, h2_ref, xg_ref, vals_ref, *, n, cap):
    fits = _windows_fit(meta_ref)

    @pl.when(fits)
    def _():
        xg_ref[...] = jnp.zeros(xg_ref.shape, xg_ref.dtype)
        vals_ref[...] = jnp.zeros(vals_ref.shape, vals_ref.dtype)

        def block(b, carry):
            cols = pl.ds(pl.multiple_of(b * MOE_BLOCK, MOE_BLOCK), MOE_BLOCK)
            bases = [_win_base(meta_ref, e, b, cap) for e in range(N_EXPERTS)]
            hits = [_win_hits(pos_ref[e:e + 1, cols], bases[e]) for e in range(N_EXPERTS)]
            part = _dot(_one_hot(hits), h2_ref[cols, :])
            for e in range(N_EXPERTS):
                rows = pl.ds(bases[e], MOE_WIN)
                xg_ref[e, rows, :] += part[e * MOE_WIN:(e + 1) * MOE_WIN, :].astype(xg_ref.dtype)
                vals = jnp.sum(jnp.where(hits[e], aff_ref[e:e + 1, cols], 0.0), axis=1, keepdims=True)
                vals_ref[e, rows, :] += jnp.broadcast_to(vals, (MOE_WIN, LANES))
            return carry

        lax.fori_loop(0, n // MOE_BLOCK, block, 0)

    @pl.when(jnp.logical_not(fits))
    def _():
        _gather_dense(pos_ref, aff_ref, h2_ref, xg_ref, vals_ref, slice(0, cap), cap, 1)


def _gather_win(meta, pos3, aff3, h23, *, cap):
    nseq, _, n = pos3.shape
    return pl.pallas_call(
        functools.partial(_gather_win_kernel, n=n, cap=cap),
        grid=(nseq,),
        in_specs=[
            pl.BlockSpec((N_EXPERTS, LANES), lambda s: (s, 0), memory_space=pltpu.SMEM),
            pl.BlockSpec((None, N_EXPERTS, n), lambda s: (s, 0, 0)),
            pl.BlockSpec((None, N_EXPERTS, n), lambda s: (s, 0, 0)),
            pl.BlockSpec((None, n, D_MODEL), lambda s: (s, 0, 0)),
        ],
        out_specs=(
            pl.BlockSpec((N_EXPERTS, cap, D_MODEL), lambda s: (0, s, 0)),
            pl.BlockSpec((N_EXPERTS, cap, LANES), lambda s: (0, s, 0)),
        ),
        out_shape=(
            jax.ShapeDtypeStruct((N_EXPERTS, nseq * cap, D_MODEL), bf16),
            jax.ShapeDtypeStruct((N_EXPERTS, nseq * cap, LANES), f32),
        ),
        compiler_params=_params("parallel"),
        name="gather_win",
    )(meta, pos3, aff3, h23)


def _experts_kernel(xp_ref, vp_ref, xs_ref, vs_ref, wg_ref, wu_ref, wd_ref, yp_ref, ys_ref, *, rc):
    wg = wg_ref[...].astype(bf16)
    wu = wu_ref[...].astype(bf16)
    wd = wd_ref[...].astype(bf16)
    for x_ref, v_ref, y_ref in ((xp_ref, vp_ref, yp_ref), (xs_ref, vs_ref, ys_ref)):
        for r0 in range(0, x_ref.shape[0], rc):
            x = x_ref[r0:r0 + rc, :]
            a = _dot(x, wg)
            u = _dot(x, wu)
            hm = (a * jax.nn.sigmoid(a) * u).astype(bf16)
            y = _dot(hm, wd) * v_ref[r0:r0 + rc, 0:1]
            y_ref[r0:r0 + rc, :] = y.astype(y_ref.dtype)


def _experts(xg_p, vals_p, xg_s, vals_s, wg, wu, wd, *, rc=EXPERT_ROWS):
    rp = xg_p.shape[1]
    rs = xg_s.shape[1]
    per_e = lambda r, w: pl.BlockSpec((None, r, w), lambda e: (e, 0, 0))
    return pl.pallas_call(
        functools.partial(_experts_kernel, rc=rc),
        grid=(N_EXPERTS,),
        in_specs=[
            per_e(rp, D_MODEL), per_e(rp, LANES), per_e(rs, D_MODEL), per_e(rs, LANES),
            per_e(D_MODEL, EXPERT_FF), per_e(D_MODEL, EXPERT_FF), per_e(EXPERT_FF, D_MODEL),
        ],
        out_specs=(per_e(rp, D_MODEL), per_e(rs, D_MODEL)),
        out_shape=(
            jax.ShapeDtypeStruct((N_EXPERTS, rp, D_MODEL), bf16),
            jax.ShapeDtypeStruct((N_EXPERTS, rs, D_MODEL), bf16),
        ),
        compiler_params=_params("parallel"),
        name="experts",
    )(xg_p, vals_p, xg_s, vals_s, wg, wu, wd)


def _scatter_dense(pos_ref, y_ref, rows, tn, cap, ne):
    moe = jnp.zeros((tn, D_MODEL), f32)
    for e0 in range(0, N_EXPERTS, ne):
        onehot = _one_hot([_slot_hits(pos_ref, e0 + e, cap) for e in range(ne)])
        y = jnp.concatenate([y_ref[e0 + e, rows, :] for e in range(ne)], axis=0)
        moe = moe + lax.dot_general(onehot, y, (((0,), (0,)), ((), ())), preferred_element_type=f32)
    return moe


def _scatter_kernel(pos_ref, y_ref, x1_ref, m_ref, fn_ref, o_ref, *, tn, cap, ne, sb):
    g2 = m_ref[5:6, :]
    for g in range(sb):
        moe = _scatter_dense(pos_ref.at[g], y_ref, slice(g * cap, (g + 1) * cap), tn, cap, ne)
        o_ref[g] = _rms(x1_ref[g] + g2 * moe, fn_ref[...])


def _scatter(pos3, y, x13, m3, fn, *, cap, ne, tn, sb):
    nseq, _, n = pos3.shape
    assert sb == 1 or m3.shape[0] == 1, "sequences sharing a grid step must share their modulation rows"
    return pl.pallas_call(
        functools.partial(_scatter_kernel, tn=tn, cap=cap, ne=ne, sb=sb),
        grid=(nseq // sb, n // tn),
        in_specs=[
            pl.BlockSpec((sb, N_EXPERTS, tn), lambda s, i: (s, 0, i)),
            pl.BlockSpec((N_EXPERTS, sb * cap, D_MODEL), lambda s, i: (0, s, 0)),
            pl.BlockSpec((sb, tn, D_MODEL), lambda s, i: (s, i, 0)),
            _mod_spec(m3, lambda s, i: s),
            _const_spec((1, D_MODEL)),
        ],
        out_specs=pl.BlockSpec((sb, tn, D_MODEL), lambda s, i: (s, i, 0)),
        out_shape=jax.ShapeDtypeStruct((nseq, n, D_MODEL), f32),
        compiler_params=_params("parallel", "arbitrary"),
        name="scatter",
    )(pos3, y, x13, m3, fn)


def _scatter_win_kernel(meta_ref, pos_ref, y_ref, x1_ref, m_ref, fn_ref, o_ref, moe_scr, *, cap):
    b = pl.program_id(1)
    fits = _windows_fit(meta_ref)

    @pl.when(fits)
    def _():
        bases = [_win_base(meta_ref, e, b, cap) for e in range(N_EXPERTS)]
        onehot = _one_hot([_win_hits(pos_ref[e:e + 1, :], bases[e]) for e in range(N_EXPERTS)])
        y = jnp.concatenate([y_ref[e, pl.ds(bases[e], MOE_WIN), :] for e in range(N_EXPERTS)], axis=0)
        moe_scr[...] = lax.dot_general(onehot, y, (((0,), (0,)), ((), ())), preferred_element_type=f32)

    @pl.when(jnp.logical_not(fits))
    def _():
        moe_scr[...] = _scatter_dense(pos_ref, y_ref, slice(0, cap), MOE_BLOCK, cap, 1)

    o_ref[...] = _rms(x1_ref[...] + m_ref[5:6, :] * moe_scr[...], fn_ref[...])


def _scatter_win(meta, pos3, y, x13, m3, fn, *, cap):
    nseq, _, n = pos3.shape
    tn = MOE_BLOCK
    return pl.pallas_call(
        functools.partial(_scatter_win_kernel, cap=cap),
        grid=(nseq, n // tn),
        in_specs=[
            pl.BlockSpec((N_EXPERTS, LANES), lambda s, i: (s, 0), memory_space=pltpu.SMEM),
            pl.BlockSpec((None, N_EXPERTS, tn), lambda s, i: (s, 0, i)),
            pl.BlockSpec((N_EXPERTS, cap, D_MODEL), lambda s, i: (0, s, 0)),
            pl.BlockSpec((None, tn, D_MODEL), lambda s, i: (s, i, 0)),
            _mod_spec(m3, lambda s, i: s),
            _const_spec((1, D_MODEL)),
        ],
        out_specs=pl.BlockSpec((None, tn, D_MODEL), lambda s, i: (s, i, 0)),
        out_shape=jax.ShapeDtypeStruct((nseq, n, D_MODEL), f32),
        scratch_shapes=[pltpu.VMEM((tn, D_MODEL), f32)],
        compiler_params=_params("parallel", "arbitrary"),
        name="scatter_win",
    )(meta, pos3, y, x13, m3, fn)


def _rope_tables(n):
    t = np.arange(n)
    half = QK_ROPE // 2
    freqs = ROPE_BASE ** (-np.arange(0, half, 2, dtype=np.float64) / half)
    ang_r = (t // GRID_W)[:, None] * freqs
    ang_c = (t % GRID_W)[:, None] * freqs
    cr, sr, cc, sc = np.cos(ang_r), np.sin(ang_r), np.cos(ang_c), np.sin(ang_c)
    cos = np.ones((n, HEAD_W))
    sin = np.zeros((n, HEAD_W))
    cos[:, ROPE_OFF:ROPE_OFF + QK_ROPE] = np.concatenate([cr, cr, cc, cc], axis=-1)
    sin[:, ROPE_OFF:ROPE_OFF + QK_ROPE] = np.concatenate([-sr, sr, -sc, sc], axis=-1)
    return jnp.asarray(cos, f32), jnp.asarray(sin, f32)


_PARTNER = np.concatenate([np.arange(8, 16), np.arange(0, 8), np.arange(24, 32), np.arange(16, 24)])


def _rope_partner(w):
    q = QK_ROPE // 4
    return jnp.concatenate([w[..., q:2 * q], w[..., 0:q], w[..., 3 * q:4 * q], w[..., 2 * q:3 * q]], axis=-1)


def _rope_placement():
    place = np.zeros((LANES, 2 * LANES), np.float32)
    d = np.arange(QK_ROPE)
    place[d, ROPE_OFF + d] = 1.0
    place[_PARTNER, LANES + ROPE_OFF + d] = 1.0
    return jnp.asarray(place, bf16)


def _head_blocks(w_nope, w_rope):
    rows = w_nope.shape[0]
    if w_rope is None:
        w_rope = jnp.zeros((rows, N_HEADS, QK_ROPE), w_nope.dtype)
    z = jnp.zeros((rows, N_HEADS, HEAD_W - QK_NOPE - QK_ROPE), w_nope.dtype)
    return jnp.concatenate([w_nope, w_rope, z], axis=-1).reshape(rows, N_HEADS * HEAD_W)


def kernel(x_prompt, x_sample, cache_ckv, cache_kpe, c, c_ctx, w_ada, b_ada, norm1, w_in, conv_dw, conv_dw_b,
           conv_ln_g, conv_ln_b, w_conv_out, q_norm, w_qb, kv_norm, w_kvb, w_o_mla, w_out, norm2, w_router,
           w_e_gate, w_e_up, w_e_down, final_norm):
    assert w_ada.shape[0] == 1, "single trunk layer"
    nb_p, n_p, _ = x_prompt.shape
    nb_s, n_s, _ = x_sample.shape

    win = _wprep(w_in[0].T, _rope_placement())
    wq = w_qb[0].reshape(Q_LORA, N_HEADS, QK_NOPE + QK_ROPE)
    wqb = _head_blocks(wq[..., :QK_NOPE], wq[..., QK_NOPE:]).astype(bf16)
    wqbp = _head_blocks(jnp.zeros_like(wq[..., :QK_NOPE]), _rope_partner(wq[..., QK_NOPE:])).astype(bf16)
    wkv = w_kvb[0].reshape(KV_LORA, N_HEADS, QK_NOPE + V_DIM)
    wk = _head_blocks(wkv[..., :QK_NOPE], None).astype(bf16)
    wv = wkv[..., QK_NOPE:].reshape(KV_LORA, N_HEADS * V_DIM).astype(bf16)
    wco = w_conv_out[0].astype(bf16)
    wo = w_o_mla[0].astype(bf16)
    wout = w_out[0].astype(bf16)
    wr_hi = w_router[0].astype(bf16)
    wr_lo = (w_router[0] - wr_hi.astype(f32)).astype(bf16)
    wr = jnp.concatenate([wr_hi, wr_lo, jnp.zeros((D_MODEL, LANES - 2 * N_EXPERTS), bf16)], axis=-1)
    row = lambda a: a.reshape(1, -1)

    mod = jnp.concatenate([c_ctx[None, :], c, jnp.zeros((8 - 1 - nb_s, D_MODEL), f32)], axis=0)
    m = _ada(mod, w_ada[0], b_ada[0]).reshape(8, 6, D_MODEL)
    m_p, m_s = m[0:1], m[1:1 + nb_s]
    cos, sin = _rope_tables(n_s)

    conv_w = _conv_weights(conv_dw[0], row(conv_dw_b[0]), row(conv_ln_g[0]), row(conv_ln_b[0]), wco, CONV_CHUNK)

    def mixers(x, m3, rope, ctx_ckv, ctx_kpe):
        nseq, n, _ = x.shape
        x2d = x.reshape(nseq * n, D_MODEL)
        cm, q, ckv, kr, g, *kpe = _inproj(x2d, m3, row(norm1[0]), win, row(q_norm[0]), wqb, wqbp, row(kv_norm[0]),
                                          cos, sin, conv_w, n=n, rope=rope, tm=IN_TILE)
        keys_ckv = ckv.reshape(nseq, n, KV_LORA)
        keys_kpe = kr.reshape(nseq, n, LANES)
        if ctx_ckv is not None:
            keys_ckv = jnp.concatenate([ctx_ckv.astype(keys_ckv.dtype), keys_ckv], axis=1)
            keys_kpe = jnp.concatenate([ctx_kpe.astype(keys_kpe.dtype), keys_kpe], axis=1)
        x1, h2, lg = _attn(q, keys_ckv, keys_kpe, cm.reshape(nseq * n, D_MODEL), g, x2d, m3, wk, wv, wo, wout,
                           row(norm2[0]), wr, n=n, tq=min(n, Q_TILE))
        return x1, h2, lg, ckv, kpe

    ctx_kpe = jnp.pad(cache_kpe[:, 0], ((0, 0), (0, 0), (ROPE_OFF, LANES - ROPE_OFF - QK_ROPE)))
    x1_p, h2_p, lg_p, ckv_p, (kpe_p,) = mixers(x_prompt, m_p, False, None, None)
    x1_s, h2_s, lg_s, _, _ = mixers(x_sample, m_s, True, cache_ckv[:, 0], ctx_kpe)

    def moe_tiles(n):
        cap = EC_FACTOR * n // N_EXPERTS
        ne = N_EXPERTS if N_EXPERTS * cap <= MOE_ROWS else 1
        sb = max(1, MOE_TOKENS // n)
        return cap, ne, sb

    def windowed(n):
        return n >= 4 * MOE_BLOCK and EC_FACTOR * n // N_EXPERTS >= 2 * MOE_WIN

    def route_gather(h2, lg, nseq, n):
        cap, ne, sb = moe_tiles(n)
        pos, aff, meta = _route(lg.reshape(nseq, n, LANES), cap=cap)
        pos3 = pos.reshape(nseq, N_EXPERTS, n)
        aff3 = aff.reshape(nseq, N_EXPERTS, n)
        h23 = h2.reshape(nseq, n, D_MODEL)
        if windowed(n):
            xg, vals = _gather_win(meta, pos3, aff3, h23, cap=cap)
        else:
            xg, vals = _gather(pos3, aff3, h23, cap=cap, ne=ne, sb=sb)
        return pos3, meta, xg, vals

    pos_p, meta_p, xg_p, vals_p = route_gather(h2_p, lg_p, nb_p, n_p)
    pos_s, meta_s, xg_s, vals_s = route_gather(h2_s, lg_s, nb_s, n_s)
    y_p, y_s = _experts(xg_p, vals_p, xg_s, vals_s, w_e_gate[0], w_e_up[0], w_e_down[0])
    fn = row(final_norm)

    def scatter(pos, meta, y, x1, m3, nseq, n):
        cap, ne, sb = moe_tiles(n)
        x13 = x1.reshape(nseq, n, D_MODEL)
        if windowed(n):
            return _scatter_win(meta, pos, y, x13, m3, fn, cap=cap)
        return _scatter(pos, y, x13, m3, fn, cap=cap, ne=ne, tn=min(n, SCATTER_TILE), sb=sb)

    y_prompt = scatter(pos_p, meta_p, y_p, x1_p, m_p, nb_p, n_p)
    y_sample = scatter(pos_s, meta_s, y_s, x1_s, m_s, nb_s, n_s)

    new_ckv = ckv_p.reshape(nb_p, 1, n_p, KV_LORA)
    new_kpe = kpe_p.reshape(nb_p, 1, n_p, QK_ROPE)
    return (y_prompt, y_sample, new_ckv, new_kpe)
```

```python
import functools

import jax
import jax.numpy as jnp
import numpy as np
from jax import lax
from jax.experimental import pallas as pl
from jax.experimental.pallas import tpu as pltpu

D_MODEL = 1024
GRID_W = 64
CONV_DIM = 512
CONV_WIDTH = 31
N_HEADS = 8
QK_NOPE = 64
QK_ROPE = 32
V_DIM = 64
Q_LORA = 256
KV_LORA = 128
N_EXPERTS = 16
EXPERT_FF = 512
EC_FACTOR = 2
ROPE_BASE = 10000.0
EPS = 1e-6

LANES = 128
HEAD_W = LANES
ROPE_OFF = QK_NOPE
CONV_HALO = 16
LOG2E = 1.4426950408889634
VMEM_LIMIT = 48 * 1024 * 1024
IN_TILE = 512
CONV_CHUNK = 64
CONV_OUT_ROWS = 256
GATE_CHUNK = 512
EXPERT_ROWS = 512
Q_TILE = 512
ATTN_TOKENS = 1024
SCORE_BYTES = 12 * 1024 * 1024
MOE_ROWS = 512
MOE_TOKENS = 1024
SCATTER_TILE = 512
MOE_BLOCK = 2 * LANES
MOE_WIN = 80
WIN_ALIGN = 16

C_CONV = 0
C_QA = 2 * CONV_DIM
C_KVA = C_QA + Q_LORA
C_KR = C_KVA + KV_LORA
C_GATE = C_KR + LANES
C_END = C_GATE + 2 * D_MODEL
C_KRP = C_END

f32 = jnp.float32
bf16 = jnp.bfloat16


def _params(*sem):
    return pltpu.CompilerParams(dimension_semantics=sem, vmem_limit_bytes=VMEM_LIMIT)


def _dot(a, b):
    return jnp.dot(a, b, preferred_element_type=f32)


def _rms(x, g):
    return x * lax.rsqrt(jnp.mean(x * x, axis=-1, keepdims=True) + EPS) * g


def _const_spec(shape):
    nd = len(shape)
    return pl.BlockSpec(shape, lambda *_: (0,) * nd)


def _mod_spec(m3, seq_of):
    if m3.shape[0] == 1:
        return _const_spec((None, 6, D_MODEL))
    return pl.BlockSpec((None, 6, D_MODEL), lambda *idx: (seq_of(*idx), 0, 0))


def _ada_kernel(s_ref, w_ref, b_ref, o_ref):
    s = s_ref[...]
    s = s * jax.nn.sigmoid(s)
    o_ref[...] = _dot(s.astype(bf16), w_ref[...].astype(bf16)) + b_ref[...]


def _ada(mod, w_ada, b_ada):
    rows = mod.shape[0]
    n_out = w_ada.shape[1]
    tn = D_MODEL
    return pl.pallas_call(
        _ada_kernel,
        grid=(n_out // tn,),
        in_specs=[
            _const_spec((rows, D_MODEL)),
            pl.BlockSpec((D_MODEL, tn), lambda j: (0, j)),
            pl.BlockSpec((1, tn), lambda j: (0, j)),
        ],
        out_specs=pl.BlockSpec((rows, tn), lambda j: (0, j)),
        out_shape=jax.ShapeDtypeStruct((rows, n_out), f32),
        compiler_params=_params("arbitrary"),
        name="ada",
    )(mod, w_ada, b_ada.reshape(1, n_out))


def _wprep_kernel(wt_ref, place_ref, o_ref):
    def block(r0):
        return wt_ref[r0:r0 + LANES, :].T.astype(bf16)

    for j in range(C_KR // LANES):
        o_ref[:, j * LANES:(j + 1) * LANES] = block(j * LANES)
    placed = _dot(block(C_KR), place_ref[...])
    o_ref[:, C_KR:C_GATE] = placed[:, 0:LANES].astype(bf16)
    o_ref[:, C_KRP:C_KRP + LANES] = placed[:, LANES:2 * LANES].astype(bf16)
    for j in range(2 * D_MODEL // LANES):
        o_ref[:, C_GATE + j * LANES:C_GATE + (j + 1) * LANES] = block(C_KR + QK_ROPE + j * LANES)


def _wprep(w_in_t, place):
    cols, rows = w_in_t.shape
    return pl.pallas_call(
        _wprep_kernel,
        grid=(rows // LANES,),
        in_specs=[pl.BlockSpec((cols, LANES), lambda i: (0, i)), _const_spec(place.shape)],
        out_specs=pl.BlockSpec((LANES, C_KRP + LANES), lambda i: (i, 0)),
        out_shape=jax.ShapeDtypeStruct((rows, C_KRP + LANES), bf16),
        compiler_params=_params("parallel"),
        name="wprep",
    )(w_in_t, place)


def _conv_taps(vpad, r0, shift_ref, dw_ref, dwb_ref, ybuf, y0, ct):
    pad = CONV_WIDTH // 2
    sub = 8
    span = ((CONV_HALO - pad + CONV_WIDTH - 1) // sub) * sub
    cw = 2 * LANES
    for cb in range(CONV_DIM // cw):
        sl = slice(cb * cw, (cb + 1) * cw)
        win = vpad[pl.ds(r0, ct + 2 * CONV_HALO), sl]
        acc = jnp.zeros((ct // sub, sub, cw), f32)
        for ph in range(sub):
            wph = win.astype(f32) if ph == 0 else _dot(shift_ref[ph - 1], win)
            for a in range(span // sub + 1):
                k = a * sub + ph - (CONV_HALO - pad)
                if 0 <= k < CONV_WIDTH:
                    acc = acc + wph[a * sub:a * sub + ct, :].reshape(ct // sub, sub, cw) * dw_ref[k, :, sl][None]
        ybuf[pl.ds(y0, ct), sl] = acc.reshape(ct, cw) + dwb_ref[:, sl]


def _conv_out(y, lng_ref, lnb_ref, wco_ref, gate):
    mu = jnp.mean(y, axis=-1, keepdims=True)
    yc = y - mu
    var = jnp.mean(yc * yc, axis=-1, keepdims=True)
    z = yc * lax.rsqrt(var + EPS) * lng_ref[...] + lnb_ref[...]
    z = z * jax.nn.sigmoid(z)
    return (gate.astype(f32) * _dot(z.astype(bf16), wco_ref[...])).astype(bf16)


def _fill_padded(vpad, v, n):
    zeros = jnp.zeros((CONV_HALO, CONV_DIM), vpad.dtype)
    vpad[0:CONV_HALO, :] = zeros
    vpad[CONV_HALO + n:2 * CONV_HALO + n, :] = zeros
    vpad[CONV_HALO:CONV_HALO + n, :] = v


def _inproj_kernel(*refs, rope, halo, tiles_per_seq, q_scale, n, rt, ct):
    refs = list(refs)
    x_ref = refs.pop(0)
    if halo:
        xp_ref, xn_ref = refs[:2]
        del refs[:2]
    m_ref, n1_ref, win_ref, qn_ref, wqb_ref = refs[:5]
    del refs[:5]
    if rope:
        wqbp_ref = refs.pop(0)
    kvn_ref = refs.pop(0)
    if rope:
        cos_ref, sin_ref = refs[:2]
        del refs[:2]
    shift_ref, dw_ref, dwb_ref, lng_ref, lnb_ref, wco_ref = refs[:6]
    del refs[:6]
    cm_ref, q_ref, ckv_ref, kr_ref, g_ref = refs[:5]
    del refs[:5]
    if not rope:
        kpe_ref = refs.pop(0)
    vpad, ybuf, gc = refs
    tm = x_ref.shape[0]
    lo = CONV_HALO if halo else 0

    sh1 = m_ref[0:1, :]
    sc1 = m_ref[1:2, :]
    x = jnp.concatenate([xp_ref[...], x_ref[...], xn_ref[...]], axis=0) if halo else x_ref[...]
    hb_all = (_rms(x, n1_ref[...]) * (1.0 + sc1) + sh1).astype(bf16)

    left_all = _dot(hb_all, win_ref[:, 0:C_GATE])
    v = (left_all[:, C_CONV:C_CONV + CONV_DIM] * jax.nn.sigmoid(left_all[:, C_CONV + CONV_DIM:C_QA])).astype(bf16)
    hb = hb_all[lo:lo + tm, :]
    left = left_all[lo:lo + tm, :]

    if halo:
        j = pl.program_id(0) % tiles_per_seq
        zeros = jnp.zeros((CONV_HALO, CONV_DIM), bf16)
        vpad[0, 0:lo, :] = jnp.where(j > 0, v[0:lo, :], zeros)
        vpad[0, lo:lo + tm, :] = v[lo:lo + tm, :]
        vpad[0, lo + tm:2 * lo + tm, :] = jnp.where(j < tiles_per_seq - 1, v[lo + tm:2 * lo + tm, :], zeros)
        chunks = [(0, c * ct, c * ct) for c in range(tm // ct)]
    else:
        for s in range(tm // n):
            _fill_padded(vpad.at[s], v[s * n:(s + 1) * n, :], n)
        chunks = [(s, c * ct, s * n + c * ct) for s in range(tm // n) for c in range(n // ct)]

    def conv_some(count):
        for _ in range(min(count, len(chunks))):
            s, r0, y0 = chunks.pop(0)
            _conv_taps(vpad.at[s], r0, shift_ref, dw_ref, dwb_ref, ybuf, y0, ct)

    steps = 2 + 2 * D_MODEL // GATE_CHUNK
    per_step = -(-len(chunks) // steps)

    qa = left[:, C_QA:C_KVA]
    qn = _rms(qa, qn_ref[...]).astype(bf16)
    q = _dot(qn, wqb_ref[...])
    if rope:
        qp = _dot(qn, wqbp_ref[...])
        cos = cos_ref[...]
        sin = sin_ref[...]
        for hd in range(N_HEADS):
            sl = slice(hd * HEAD_W, (hd + 1) * HEAD_W)
            q_ref[:, sl] = ((q[:, sl] * cos + qp[:, sl] * sin) * q_scale).astype(q_ref.dtype)
    else:
        q_ref[...] = (q * q_scale).astype(q_ref.dtype)
    conv_some(per_step)

    kva = left[:, C_KVA:C_KR]
    ckv_ref[...] = _rms(kva, kvn_ref[...]).astype(ckv_ref.dtype)

    kr = left[:, C_KR:C_GATE]
    if rope:
        krp = _dot(hb, win_ref[:, C_KRP:C_KRP + LANES])
        kr = kr * cos_ref[...] + krp * sin_ref[...]
    else:
        kpe_ref[...] = kr[:, ROPE_OFF:ROPE_OFF + QK_ROPE]
    kr_ref[...] = kr.astype(kr_ref.dtype)
    conv_some(per_step)

    gw = GATE_CHUNK
    for jg in range(2 * D_MODEL // gw):
        gate = jax.nn.sigmoid(_dot(hb, win_ref[:, C_GATE + jg * gw:C_GATE + (jg + 1) * gw])).astype(bf16)
        if jg * gw < D_MODEL:
            gc[:, jg * gw:(jg + 1) * gw] = gate
        else:
            g_ref[:, jg * gw - D_MODEL:(jg + 1) * gw - D_MODEL] = gate
        conv_some(per_step)

    conv_some(len(chunks))
    for r0 in range(0, tm, rt):
        cm_ref[r0:r0 + rt, :] = _conv_out(ybuf[r0:r0 + rt, :], lng_ref, lnb_ref, wco_ref, gc[r0:r0 + rt, :])


def _inproj(x2d, m3, norm1, win, q_norm, wqb, wqbp, kv_norm, cos, sin, conv_w, *, n, rope, tm,
            rt=CONV_OUT_ROWS, ct=CONV_CHUNK):
    tokens = x2d.shape[0]
    halo = n > tm
    assert n % tm == 0 if halo else (tm % n == 0 and m3.shape[0] == 1 and not rope)
    tiles_per_seq = max(1, n // tm)
    q_scale = float((QK_NOPE + QK_ROPE) ** -0.5 * LOG2E)
    tile = lambda w: pl.BlockSpec((tm, w), lambda i: (i, 0))
    in_specs = [tile(D_MODEL)]
    args = [x2d]
    if halo:
        hb_tile = tm // CONV_HALO
        hb_seq = n // CONV_HALO
        prev = lambda i: (jnp.maximum(i * hb_tile - 1, (i // tiles_per_seq) * hb_seq), 0)
        nxt = lambda i: (jnp.minimum((i + 1) * hb_tile, (i // tiles_per_seq + 1) * hb_seq - 1), 0)
        in_specs += [pl.BlockSpec((CONV_HALO, D_MODEL), prev), pl.BlockSpec((CONV_HALO, D_MODEL), nxt)]
        args += [x2d, x2d]
    in_specs += [
        _mod_spec(m3, lambda i: i // tiles_per_seq),
        _const_spec((1, D_MODEL)),
        _const_spec((D_MODEL, C_KRP + LANES if rope else C_END)),
        _const_spec((1, Q_LORA)),
        _const_spec(wqb.shape),
    ]
    args += [m3, norm1, win, q_norm, wqb]
    if rope:
        in_specs.append(_const_spec(wqbp.shape))
        args.append(wqbp)
    in_specs.append(_const_spec((1, KV_LORA)))
    args.append(kv_norm)
    if rope:
        tab = pl.BlockSpec((tm, LANES), lambda i: (i % tiles_per_seq, 0))
        in_specs += [tab, tab]
        args += [cos, sin]
    args += list(conv_w)
    in_specs += [_const_spec(w.shape) for w in conv_w]
    pieces = max(1, tm // n)
    scratch = [
        pltpu.VMEM((pieces, tm // pieces + 2 * CONV_HALO, CONV_DIM), bf16),
        pltpu.VMEM((tm, CONV_DIM), f32),
        pltpu.VMEM((tm, D_MODEL), bf16),
    ]
    out_shape = [
        jax.ShapeDtypeStruct((tokens, D_MODEL), bf16),
        jax.ShapeDtypeStruct((tokens, N_HEADS * HEAD_W), bf16),
        jax.ShapeDtypeStruct((tokens, KV_LORA), bf16 if rope else f32),
        jax.ShapeDtypeStruct((tokens, LANES), bf16),
        jax.ShapeDtypeStruct((tokens, D_MODEL), bf16),
    ]
    out_specs = [tile(D_MODEL), tile(N_HEADS * HEAD_W), tile(KV_LORA), tile(LANES), tile(D_MODEL)]
    if not rope:
        out_shape.append(jax.ShapeDtypeStruct((tokens, QK_ROPE), f32))
        out_specs.append(tile(QK_ROPE))
    return pl.pallas_call(
        functools.partial(_inproj_kernel, rope=rope, halo=halo, tiles_per_seq=tiles_per_seq, q_scale=q_scale,
                          n=n, rt=rt, ct=ct),
        grid=(tokens // tm,),
        in_specs=in_specs,
        out_specs=out_specs,
        out_shape=out_shape,
        scratch_shapes=scratch,
        compiler_params=_params("parallel"),
        name="inproj_conv_rope" if rope else "inproj_conv",
    )(*args)


def _conv_weights(dw, dwb, lng, lnb, wco, ct):
    rows = ct + 2 * CONV_HALO
    i = np.arange(rows)
    shifts = jnp.asarray(np.stack([(i[None, :] == i[:, None] + ph) for ph in range(1, 8)]), bf16)
    dw_tiles = jnp.broadcast_to(dw[:, None, :], (CONV_WIDTH, 8, CONV_DIM))
    return shifts, dw_tiles, dwb, lng, lnb, wco


def _attn_kernel(q_ref, ckv_ref, kpe_ref, cm_ref, g_ref, x_ref, m_ref, wk_ref, wv_ref, wo_ref, wout_ref, n2_ref,
                 wr_ref, x1_ref, h2_ref, lg_ref, k_scr, v_scr, q_scr, *, sb, hg):
    s_len = ckv_ref.shape[1]

    @pl.when(pl.program_id(1) == 0)
    def _():
        ckv = ckv_ref[...].reshape(sb * s_len, KV_LORA).astype(bf16)
        kpe = kpe_ref[...].reshape(sb * s_len, LANES).astype(f32)
        k = _dot(ckv, wk_ref[...])
        v = _dot(ckv, wv_ref[...])
        ones = jnp.ones((s_len, V_DIM), bf16)
        for g in range(sb):
            rows = slice(g * s_len, (g + 1) * s_len)
            for hd in range(N_HEADS):
                u = g * N_HEADS + hd
                k_scr[u] = (k[rows, hd * HEAD_W:(hd + 1) * HEAD_W] + kpe[rows, :]).astype(bf16)
                vh = v[rows, hd * V_DIM:(hd + 1) * V_DIM].astype(bf16)
                if hd % 2 == 0:
                    v_scr[u, :, 0:V_DIM] = vh
                    v_scr[u, :, V_DIM:LANES] = ones
                else:
                    v_scr[u, :, 0:V_DIM] = ones
                    v_scr[u, :, V_DIM:LANES] = vh

    tq = q_ref.shape[0] // sb
    units = sb * N_HEADS
    for g in range(sb):
        for hd in range(N_HEADS):
            q_scr[g * N_HEADS + hd] = q_ref[g * tq:(g + 1) * tq, hd * HEAD_W:(hd + 1) * HEAD_W]
    lane = lax.broadcasted_iota(jnp.int32, (tq, LANES), 1)
    pairs = []
    for u0 in range(0, units, hg):
        us = slice(u0, u0 + hg)
        s = lax.dot_general(q_scr[us], k_scr[us], (((2,), (2,)), ((0,), (0,))), preferred_element_type=f32)
        mx = jnp.max(s, axis=-1, keepdims=True)
        p = jnp.exp2(s - mx).astype(bf16)
        r = lax.dot_general(p, v_scr[us], (((2,), (1,)), ((0,), (0,))), preferred_element_type=f32)
        for j in range(hg // 2):
            re, ro = r[2 * j], r[2 * j + 1]
            oe = re * (1.0 / re[:, V_DIM:V_DIM + 1])
            oo = ro * (1.0 / ro[:, 0:1])
            pairs.append(jnp.where(lane < V_DIM, oe, oo))
    per_seq = N_HEADS // 2
    attn = jnp.concatenate(
        [jnp.concatenate(pairs[g * per_seq:(g + 1) * per_seq], axis=-1) for g in range(sb)], axis=0).astype(bf16)
    merged = (cm_ref[...].astype(f32) + g_ref[...].astype(f32) * _dot(attn, wo_ref[...])).astype(bf16)

    g1 = m_ref[2:3, :]
    sh2 = m_ref[3:4, :]
    sc2 = m_ref[4:5, :]
    x1 = x_ref[...] + g1 * _dot(merged, wout_ref[...])
    x1_ref[...] = x1
    h2 = _rms(x1, n2_ref[...]) * (1.0 + sc2) + sh2
    hi = h2.astype(bf16)
    h2_ref[...] = hi
    lo = (h2 - hi.astype(f32)).astype(bf16)
    both = _dot(jnp.concatenate([hi, lo], axis=0), wr_ref[...])
    rows = hi.shape[0]
    s = both[0:rows] + both[rows:2 * rows]
    lg_ref[...] = s + pltpu.roll(s, LANES - N_EXPERTS, axis=1)


def _attn(q2d, ckv3, kpe3, cm2d, g2d, x2d, m3, wk, wv, wo, wout, norm2, wr, *, n, tq):
    nseq, s_len, _ = ckv3.shape
    sb = max(1, ATTN_TOKENS // n) if (tq == n and m3.shape[0] == 1) else 1
    units = sb * N_HEADS
    hg = max(2, min(units, SCORE_BYTES // (tq * s_len * 4)))
    assert units % hg == 0 and hg % 2 == 0 and nseq % sb == 0
    qb = n // tq
    tokens = nseq * n
    tile = lambda w: pl.BlockSpec((sb * tq, w), lambda s, i: (s * qb + i, 0))
    return pl.pallas_call(
        functools.partial(_attn_kernel, sb=sb, hg=hg),
        grid=(nseq // sb, qb),
        in_specs=[
            tile(N_HEADS * HEAD_W),
            pl.BlockSpec((sb, s_len, KV_LORA), lambda s, i: (s, 0, 0)),
            pl.BlockSpec((sb, s_len, LANES), lambda s, i: (s, 0, 0)),
            tile(D_MODEL),
            tile(D_MODEL),
            tile(D_MODEL),
            _mod_spec(m3, lambda s, i: s),
            _const_spec(wk.shape),
            _const_spec(wv.shape),
            _const_spec(wo.shape),
            _const_spec(wout.shape),
            _const_spec((1, D_MODEL)),
            _const_spec(wr.shape),
        ],
        out_specs=(tile(D_MODEL), tile(D_MODEL), tile(LANES)),
        out_shape=(
            jax.ShapeDtypeStruct((tokens, D_MODEL), f32),
            jax.ShapeDtypeStruct((tokens, D_MODEL), bf16),
            jax.ShapeDtypeStruct((tokens, LANES), f32),
        ),
        scratch_shapes=[
            pltpu.VMEM((units, s_len, HEAD_W), bf16),
            pltpu.VMEM((units, s_len, LANES), bf16),
            pltpu.VMEM((units, tq, HEAD_W), bf16),
        ],
        compiler_params=_params("parallel", "arbitrary"),
        name="attn_out",
    )(q2d, ckv3, kpe3, cm2d, g2d, x2d, m3, wk, wv, wo, wout, norm2, wr)


def _route_kernel(lg_ref, pos_ref, aff_ref, meta_ref, *, nseq, n, cap):
    for s in range(nseq):
        lt = lg_ref[s].T[0:N_EXPERTS, :]
        e = jnp.exp(lt - jnp.max(lt, axis=0, keepdims=True))
        aff_ref[s * N_EXPERTS:(s + 1) * N_EXPERTS, :] = e / jnp.sum(e, axis=0, keepdims=True)
    rows = nseq * N_EXPERTS
    capf = float(cap)

    def bit_step(i, t):
        cand = t | (jnp.int32(1) << (30 - i))
        thr = lax.bitcast_convert_type(cand, f32)
        cnt = jnp.sum(jnp.where(aff_ref[...] >= thr, 1.0, 0.0), axis=1, keepdims=True)
        return jnp.where(cnt >= capf, cand, t)

    t = lax.fori_loop(0, 31, bit_step, jnp.zeros((rows, 1), jnp.int32))
    thr = lax.bitcast_convert_type(t, f32)
    need = capf - jnp.sum(jnp.where(aff_ref[...] > thr, 1.0, 0.0), axis=1, keepdims=True)

    blk = MOE_BLOCK
    tri = jnp.where(
        lax.broadcasted_iota(jnp.int32, (blk, blk), 0) < lax.broadcasted_iota(jnp.int32, (blk, blk), 1),
        1.0, 0.0).astype(bf16)
    carry_gt = jnp.zeros((rows, 1), f32)
    carry_eq = jnp.zeros((rows, 1), f32)
    lane = lax.broadcasted_iota(jnp.int32, (rows, LANES), 1)
    meta = jnp.zeros((rows, LANES), f32)
    cmax = jnp.zeros((rows, 1), f32)
    for b in range(n // blk):
        sl = slice(b * blk, (b + 1) * blk)
        ab = aff_ref[:, sl]
        gt = ab > thr
        eq = ab == thr
        gtb = jnp.where(gt, 1.0, 0.0)
        eqb = jnp.where(eq, 1.0, 0.0)
        pre_gt = _dot(gtb.astype(bf16), tri) + carry_gt
        pre_eq = _dot(eqb.astype(bf16), tri) + carry_eq
        meta = jnp.where(lane == b, carry_gt + jnp.minimum(carry_eq, need), meta)
        carry_gt = carry_gt + jnp.sum(gtb, axis=1, keepdims=True)
        carry_eq = carry_eq + jnp.sum(eqb, axis=1, keepdims=True)
        sel = gt | (eq & (pre_eq < need))
        slot = pre_gt + jnp.minimum(pre_eq, need)
        pos_ref[:, sl] = jnp.where(sel, slot, -1.0).astype(jnp.int32)
        cmax = jnp.maximum(cmax, jnp.sum(jnp.where(sel, 1.0, 0.0), axis=1, keepdims=True))
    meta_ref[...] = jnp.where(lane == LANES - 1, cmax, meta).astype(jnp.int32)


def _route(lg3, *, cap):
    nseq, n, _ = lg3.shape
    rows = nseq * N_EXPERTS
    assert n // MOE_BLOCK < LANES
    return pl.pallas_call(
        functools.partial(_route_kernel, nseq=nseq, n=n, cap=cap),
        grid=(1,),
        in_specs=[_const_spec(lg3.shape)],
        out_specs=(_const_spec((rows, n)), _const_spec((rows, n)), _const_spec((rows, LANES))),
        out_shape=(jax.ShapeDtypeStruct((rows, n), jnp.int32), jax.ShapeDtypeStruct((rows, n), f32),
                   jax.ShapeDtypeStruct((rows, LANES), jnp.int32)),
        compiler_params=_params("arbitrary"),
        name="route",
    )(lg3)


def _slot_hits(pos_ref, e, cap):
    width = pos_ref.shape[1]
    return lax.broadcasted_iota(jnp.int32, (cap, width), 0) == pos_ref[e:e + 1, :]


def _one_hot(hits):
    return jnp.concatenate([jnp.where(h, 1.0, 0.0).astype(bf16) for h in hits], axis=0)


def _gather_dense(pos_ref, aff_ref, h2_ref, xg_ref, vals_ref, rows, cap, ne):
    for e0 in range(0, N_EXPERTS, ne):
        hits = [_slot_hits(pos_ref, e0 + e, cap) for e in range(ne)]
        xg = _dot(_one_hot(hits), h2_ref[...]).astype(xg_ref.dtype)
        for e in range(ne):
            xg_ref[e0 + e, rows, :] = xg[e * cap:(e + 1) * cap, :]
            vals = jnp.sum(jnp.where(hits[e], aff_ref[e0 + e:e0 + e + 1, :], 0.0), axis=1, keepdims=True)
            vals_ref[e0 + e, rows, :] = jnp.broadcast_to(vals, (cap, LANES))


def _gather_kernel(pos_ref, aff_ref, h2_ref, xg_ref, vals_ref, *, cap, ne, sb):
    for g in range(sb):
        _gather_dense(pos_ref.at[g], aff_ref.at[g], h2_ref.at[g], xg_ref, vals_ref,
                      slice(g * cap, (g + 1) * cap), cap, ne)


def _gather(pos3, aff3, h23, *, cap, ne, sb):
    nseq, _, n = pos3.shape
    return pl.pallas_call(
        functools.partial(_gather_kernel, cap=cap, ne=ne, sb=sb),
        grid=(nseq // sb,),
        in_specs=[
            pl.BlockSpec((sb, N_EXPERTS, n), lambda s: (s, 0, 0)),
            pl.BlockSpec((sb, N_EXPERTS, n), lambda s: (s, 0, 0)),
            pl.BlockSpec((sb, n, D_MODEL), lambda s: (s, 0, 0)),
        ],
        out_specs=(
            pl.BlockSpec((N_EXPERTS, sb * cap, D_MODEL), lambda s: (0, s, 0)),
            pl.BlockSpec((N_EXPERTS, sb * cap, LANES), lambda s: (0, s, 0)),
        ),
        out_shape=(
            jax.ShapeDtypeStruct((N_EXPERTS, nseq * cap, D_MODEL), bf16),
            jax.ShapeDtypeStruct((N_EXPERTS, nseq * cap, LANES), f32),
        ),
        compiler_params=_params("parallel"),
        name="gather",
    )(pos3, aff3, h23)


def _windows_fit(meta_ref):
    cmax = meta_ref[0, LANES - 1]
    for e in range(1, N_EXPERTS):
        cmax = jnp.maximum(cmax, meta_ref[e, LANES - 1])
    return cmax <= MOE_WIN - WIN_ALIGN


def _win_base(meta_ref, e, b, cap):
    start = meta_ref[e, b]
    return pl.multiple_of(jnp.minimum(start - start % WIN_ALIGN, cap - MOE_WIN), WIN_ALIGN)


def _win_hits(pos_row, base):
    slot = lax.broadcasted_iota(jnp.int32, (MOE_WIN, pos_row.shape[1]), 0) + base
    return slot == pos_row


def _gather_win_kernel(meta_ref, pos_ref, aff_ref, h2_ref, xg_ref, vals_ref, *, n, cap):
    fits = _windows_fit(meta_ref)

    @pl.when(fits)
    def _():
        xg_ref[...] = jnp.zeros(xg_ref.shape, xg_ref.dtype)
        vals_ref[...] = jnp.zeros(vals_ref.shape, vals_ref.dtype)

        def block(b, carry):
            cols = pl.ds(pl.multiple_of(b * MOE_BLOCK, MOE_BLOCK), MOE_BLOCK)
            bases = [_win_base(meta_ref, e, b, cap) for e in range(N_EXPERTS)]
            hits = [_win_hits(pos_ref[e:e + 1, cols], bases[e]) for e in range(N_EXPERTS)]
            part = _dot(_one_hot(hits), h2_ref[cols, :])
            for e in range(N_EXPERTS):
                rows = pl.ds(bases[e], MOE_WIN)
                xg_ref[e, rows, :] += part[e * MOE_WIN:(e + 1) * MOE_WIN, :].astype(xg_ref.dtype)
                vals = jnp.sum(jnp.where(hits[e], aff_ref[e:e + 1, cols], 0.0), axis=1, keepdims=True)
                vals_ref[e, rows, :] += jnp.broadcast_to(vals, (MOE_WIN, LANES))
            return carry

        lax.fori_loop(0, n // MOE_BLOCK, block, 0)

    @pl.when(jnp.logical_not(fits))
    def _():
        _gather_dense(pos_ref, aff_ref, h2_ref, xg_ref, vals_ref, slice(0, cap), cap, 1)


def _gather_win(meta, pos3, aff3, h23, *, cap):
    nseq, _, n = pos3.shape
    return pl.pallas_call(
        functools.partial(_gather_win_kernel, n=n, cap=cap),
        grid=(nseq,),
        in_specs=[
            pl.BlockSpec((N_EXPERTS, LANES), lambda s: (s, 0), memory_space=pltpu.SMEM),
            pl.BlockSpec((None, N_EXPERTS, n), lambda s: (s, 0, 0)),
            pl.BlockSpec((None, N_EXPERTS, n), lambda s: (s, 0, 0)),
            pl.BlockSpec((None, n, D_MODEL), lambda s: (s, 0, 0)),
        ],
        out_specs=(
            pl.BlockSpec((N_EXPERTS, cap, D_MODEL), lambda s: (0, s, 0)),
            pl.BlockSpec((N_EXPERTS, cap, LANES), lambda s: (0, s, 0)),
        ),
        out_shape=(
            jax.ShapeDtypeStruct((N_EXPERTS, nseq * cap, D_MODEL), bf16),
            jax.ShapeDtypeStruct((N_EXPERTS, nseq * cap, LANES), f32),
        ),
        compiler_params=_params("parallel"),
        name="gather_win",
    )(meta, pos3, aff3, h23)


def _experts_kernel(xp_ref, vp_ref, xs_ref, vs_ref, wg_ref, wu_ref, wd_ref, yp_ref, ys_ref, *, rc):
    wg = wg_ref[...].astype(bf16)
    wu = wu_ref[...].astype(bf16)
    wd = wd_ref[...].astype(bf16)
    for x_ref, v_ref, y_ref in ((xp_ref, vp_ref, yp_ref), (xs_ref, vs_ref, ys_ref)):
        for r0 in range(0, x_ref.shape[0], rc):
            x = x_ref[r0:r0 + rc, :]
            a = _dot(x, wg)
            u = _dot(x, wu)
            hm = (a * jax.nn.sigmoid(a) * u).astype(bf16)
            y = _dot(hm, wd) * v_ref[r0:r0 + rc, 0:1]
            y_ref[r0:r0 + rc, :] = y.astype(y_ref.dtype)


def _experts(xg_p, vals_p, xg_s, vals_s, wg, wu, wd, *, rc=EXPERT_ROWS):
    rp = xg_p.shape[1]
    rs = xg_s.shape[1]
    per_e = lambda r, w: pl.BlockSpec((None, r, w), lambda e: (e, 0, 0))
    return pl.pallas_call(
        functools.partial(_experts_kernel, rc=rc),
        grid=(N_EXPERTS,),
        in_specs=[
            per_e(rp, D_MODEL), per_e(rp, LANES), per_e(rs, D_MODEL), per_e(rs, LANES),
            per_e(D_MODEL, EXPERT_FF), per_e(D_MODEL, EXPERT_FF), per_e(EXPERT_FF, D_MODEL),
        ],
        out_specs=(per_e(rp, D_MODEL), per_e(rs, D_MODEL)),
        out_shape=(
            jax.ShapeDtypeStruct((N_EXPERTS, rp, D_MODEL), bf16),
            jax.ShapeDtypeStruct((N_EXPERTS, rs, D_MODEL), bf16),
        ),
        compiler_params=_params("parallel"),
        name="experts",
    )(xg_p, vals_p, xg_s, vals_s, wg, wu, wd)


def _scatter_dense(pos_ref, y_ref, rows, tn, cap, ne):
    moe = jnp.zeros((tn, D_MODEL), f32)
    for e0 in range(0, N_EXPERTS, ne):
        onehot = _one_hot([_slot_hits(pos_ref, e0 + e, cap) for e in range(ne)])
        y = jnp.concatenate([y_ref[e0 + e, rows, :] for e in range(ne)], axis=0)
        moe = moe + lax.dot_general(onehot, y, (((0,), (0,)), ((), ())), preferred_element_type=f32)
    return moe


def _scatter_kernel(pos_ref, y_ref, x1_ref, m_ref, fn_ref, o_ref, *, tn, cap, ne, sb):
    g2 = m_ref[5:6, :]
    for g in range(sb):
        moe = _scatter_dense(pos_ref.at[g], y_ref, slice(g * cap, (g + 1) * cap), tn, cap, ne)
        o_ref[g] = _rms(x1_ref[g] + g2 * moe, fn_ref[...])


def _scatter(pos3, y, x13, m3, fn, *, cap, ne, tn, sb):
    nseq, _, n = pos3.shape
    assert sb == 1 or m3.shape[0] == 1, "sequences sharing a grid step must share their modulation rows"
    return pl.pallas_call(
        functools.partial(_scatter_kernel, tn=tn, cap=cap, ne=ne, sb=sb),
        grid=(nseq // sb, n // tn),
        in_specs=[
            pl.BlockSpec((sb, N_EXPERTS, tn), lambda s, i: (s, 0, i)),
            pl.BlockSpec((N_EXPERTS, sb * cap, D_MODEL), lambda s, i: (0, s, 0)),
            pl.BlockSpec((sb, tn, D_MODEL), lambda s, i: (s, i, 0)),
            _mod_spec(m3, lambda s, i: s),
            _const_spec((1, D_MODEL)),
        ],
        out_specs=pl.BlockSpec((sb, tn, D_MODEL), lambda s, i: (s, i, 0)),
        out_shape=jax.ShapeDtypeStruct((nseq, n, D_MODEL), f32),
        compiler_params=_params("parallel", "arbitrary"),
        name="scatter",
    )(pos3, y, x13, m3, fn)


def _scatter_win_kernel(meta_ref, pos_ref, y_ref, x1_ref, m_ref, fn_ref, o_ref, moe_scr, *, cap):
    b = pl.program_id(1)
    fits = _windows_fit(meta_ref)

    @pl.when(fits)
    def _():
        bases = [_win_base(meta_ref, e, b, cap) for e in range(N_EXPERTS)]
        onehot = _one_hot([_win_hits(pos_ref[e:e + 1, :], bases[e]) for e in range(N_EXPERTS)])
        y = jnp.concatenate([y_ref[e, pl.ds(bases[e], MOE_WIN), :] for e in range(N_EXPERTS)], axis=0)
        moe_scr[...] = lax.dot_general(onehot, y, (((0,), (0,)), ((), ())), preferred_element_type=f32)

    @pl.when(jnp.logical_not(fits))
    def _():
        moe_scr[...] = _scatter_dense(pos_ref, y_ref, slice(0, cap), MOE_BLOCK, cap, 1)

    o_ref[...] = _rms(x1_ref[...] + m_ref[5:6, :] * moe_scr[...], fn_ref[...])


def _scatter_win(meta, pos3, y, x13, m3, fn, *, cap):
    nseq, _, n = pos3.shape
    tn = MOE_BLOCK
    return pl.pallas_call(
        functools.partial(_scatter_win_kernel, cap=cap),
        grid=(nseq, n // tn),
        in_specs=[
            pl.BlockSpec((N_EXPERTS, LANES), lambda s, i: (s, 0), memory_space=pltpu.SMEM),
            pl.BlockSpec((None, N_EXPERTS, tn), lambda s, i: (s, 0, i)),
            pl.BlockSpec((N_EXPERTS, cap, D_MODEL), lambda s, i: (0, s, 0)),
            pl.BlockSpec((None, tn, D_MODEL), lambda s, i: (s, i, 0)),
            _mod_spec(m3, lambda s, i: s),
            _const_spec((1, D_MODEL)),
        ],
        out_specs=pl.BlockSpec((None, tn, D_MODEL), lambda s, i: (s, i, 0)),
        out_shape=jax.ShapeDtypeStruct((nseq, n, D_MODEL), f32),
        scratch_shapes=[pltpu.VMEM((tn, D_MODEL), f32)],
        compiler_params=_params("parallel", "arbitrary"),
        name="scatter_win",
    )(meta, pos3, y, x13, m3, fn)


def _rope_tables(n):
    t = np.arange(n)
    half = QK_ROPE // 2
    freqs = ROPE_BASE ** (-np.arange(0, half, 2, dtype=np.float64) / half)
    ang_r = (t // GRID_W)[:, None] * freqs
    ang_c = (t % GRID_W)[:, None] * freqs
    cr, sr, cc, sc = np.cos(ang_r), np.sin(ang_r), np.cos(ang_c), np.sin(ang_c)
    cos = np.ones((n, HEAD_W))
    sin = np.zeros((n, HEAD_W))
    cos[:, ROPE_OFF:ROPE_OFF + QK_ROPE] = np.concatenate([cr, cr, cc, cc], axis=-1)
    sin[:, ROPE_OFF:ROPE_OFF + QK_ROPE] = np.concatenate([-sr, sr, -sc, sc], axis=-1)
    return jnp.asarray(cos, f32), jnp.asarray(sin, f32)


_PARTNER = np.concatenate([np.arange(8, 16), np.arange(0, 8), np.arange(24, 32), np.arange(16, 24)])


def _rope_partner(w):
    q = QK_ROPE // 4
    return jnp.concatenate([w[..., q:2 * q], w[..., 0:q], w[..., 3 * q:4 * q], w[..., 2 * q:3 * q]], axis=-1)


def _rope_placement():
    place = np.zeros((LANES, 2 * LANES), np.float32)
    d = np.arange(QK_ROPE)
    place[d, ROPE_OFF + d] = 1.0
    place[_PARTNER, LANES + ROPE_OFF + d] = 1.0
    return jnp.asarray(place, bf16)


def _head_blocks(w_nope, w_rope):
    rows = w_nope.shape[0]
    if w_rope is None:
        w_rope = jnp.zeros((rows, N_HEADS, QK_ROPE), w_nope.dtype)
    z = jnp.zeros((rows, N_HEADS, HEAD_W - QK_NOPE - QK_ROPE), w_nope.dtype)
    return jnp.concatenate([w_nope, w_rope, z], axis=-1).reshape(rows, N_HEADS * HEAD_W)


def kernel(x_prompt, x_sample, cache_ckv, cache_kpe, c, c_ctx, w_ada, b_ada, norm1, w_in, conv_dw, conv_dw_b,
           conv_ln_g, conv_ln_b, w_conv_out, q_norm, w_qb, kv_norm, w_kvb, w_o_mla, w_out, norm2, w_router,
           w_e_gate, w_e_up, w_e_down, final_norm):
    assert w_ada.shape[0] == 1, "single trunk layer"
    nb_p, n_p, _ = x_prompt.shape
    nb_s, n_s, _ = x_sample.shape

    win = _wprep(w_in[0].T, _rope_placement())
    wq = w_qb[0].reshape(Q_LORA, N_HEADS, QK_NOPE + QK_ROPE)
    wqb = _head_blocks(wq[..., :QK_NOPE], wq[..., QK_NOPE:]).astype(bf16)
    wqbp = _head_blocks(jnp.zeros_like(wq[..., :QK_NOPE]), _rope_partner(wq[..., QK_NOPE:])).astype(bf16)
    wkv = w_kvb[0].reshape(KV_LORA, N_HEADS, QK_NOPE + V_DIM)
    wk = _head_blocks(wkv[..., :QK_NOPE], None).astype(bf16)
    wv = wkv[..., QK_NOPE:].reshape(KV_LORA, N_HEADS * V_DIM).astype(bf16)
    wco = w_conv_out[0].astype(bf16)
    wo = w_o_mla[0].astype(bf16)
    wout = w_out[0].astype(bf16)
    wr_hi = w_router[0].astype(bf16)
    wr_lo = (w_router[0] - wr_hi.astype(f32)).astype(bf16)
    wr = jnp.concatenate([wr_hi, wr_lo, jnp.zeros((D_MODEL, LANES - 2 * N_EXPERTS), bf16)], axis=-1)
    row = lambda a: a.reshape(1, -1)

    mod = jnp.concatenate([c_ctx[None, :], c, jnp.zeros((8 - 1 - nb_s, D_MODEL), f32)], axis=0)
    m = _ada(mod, w_ada[0], b_ada[0]).reshape(8, 6, D_MODEL)
    m_p, m_s = m[0:1], m[1:1 + nb_s]
    cos, sin = _rope_tables(n_s)

    conv_w = _conv_weights(conv_dw[0], row(conv_dw_b[0]), row(conv_ln_g[0]), row(conv_ln_b[0]), wco, CONV_CHUNK)

    def mixers(x, m3, rope, ctx_ckv, ctx_kpe):
        nseq, n, _ = x.shape
        x2d = x.reshape(nseq * n, D_MODEL)
        cm, q, ckv, kr, g, *kpe = _inproj(x2d, m3, row(norm1[0]), win, row(q_norm[0]), wqb, wqbp, row(kv_norm[0]),
                                          cos, sin, conv_w, n=n, rope=rope, tm=IN_TILE)
        keys_ckv = ckv.reshape(nseq, n, KV_LORA)
        keys_kpe = kr.reshape(nseq, n, LANES)
        if ctx_ckv is not None:
            keys_ckv = jnp.concatenate([ctx_ckv.astype(keys_ckv.dtype), keys_ckv], axis=1)
            keys_kpe = jnp.concatenate([ctx_kpe.astype(keys_kpe.dtype), keys_kpe], axis=1)
        x1, h2, lg = _attn(q, keys_ckv, keys_kpe, cm.reshape(nseq * n, D_MODEL), g, x2d, m3, wk, wv, wo, wout,
                           row(norm2[0]), wr, n=n, tq=min(n, Q_TILE))
        return x1, h2, lg, ckv, kpe

    ctx_kpe = jnp.pad(cache_kpe[:, 0], ((0, 0), (0, 0), (ROPE_OFF, LANES - ROPE_OFF - QK_ROPE)))
    x1_p, h2_p, lg_p, ckv_p, (kpe_p,) = mixers(x_prompt, m_p, False, None, None)
    x1_s, h2_s, lg_s, _, _ = mixers(x_sample, m_s, True, cache_ckv[:, 0], ctx_kpe)

    def moe_tiles(n):
        cap = EC_FACTOR * n // N_EXPERTS
        ne = N_EXPERTS if N_EXPERTS * cap <= MOE_ROWS else 1
        sb = max(1, MOE_TOKENS // n)
        return cap, ne, sb

    def windowed(n):
        return n >= 4 * MOE_BLOCK and EC_FACTOR * n // N_EXPERTS >= 2 * MOE_WIN

    def route_gather(h2, lg, nseq, n):
        cap, ne, sb = moe_tiles(n)
        pos, aff, meta = _route(lg.reshape(nseq, n, LANES), cap=cap)
        pos3 = pos.reshape(nseq, N_EXPERTS, n)
        aff3 = aff.reshape(nseq, N_EXPERTS, n)
        h23 = h2.reshape(nseq, n, D_MODEL)
        if windowed(n):
            xg, vals = _gather_win(meta, pos3, aff3, h23, cap=cap)
        else:
            xg, vals = _gather(pos3, aff3, h23, cap=cap, ne=ne, sb=sb)
        return pos3, meta, xg, vals

    pos_p, meta_p, xg_p, vals_p = route_gather(h2_p, lg_p, nb_p, n_p)
    pos_s, meta_s, xg_s, vals_s = route_gather(h2_s, lg_s, nb_s, n_s)
    y_p, y_s = _experts(xg_p, vals_p, xg_s, vals_s, w_e_gate[0], w_e_up[0], w_e_down[0])
    fn = row(final_norm)

    def scatter(pos, meta, y, x1, m3, nseq, n):
        cap, ne, sb = moe_tiles(n)
        x13 = x1.reshape(nseq, n, D_MODEL)
        if windowed(n):
            return _scatter_win(meta, pos, y, x13, m3, fn, cap=cap)
        return _scatter(pos, y, x13, m3, fn, cap=cap, ne=ne, tn=min(n, SCATTER_TILE), sb=sb)

    y_prompt = scatter(pos_p, meta_p, y_p, x1_p, m_p, nb_p, n_p)
    y_sample = scatter(pos_s, meta_s, y_s, x1_s, m_s, nb_s, n_s)

    new_ckv = ckv_p.reshape(nb_p, 1, n_p, KV_LORA)
    new_kpe = kpe_p.reshape(nb_p, 1, n_p, QK_ROPE)
    return (y_prompt, y_sample, new_ckv, new_kpe)
```

```python
import functools

import jax
import jax.numpy as jnp
import numpy as np
from jax import lax
from jax.experimental import pallas as pl
from jax.experimental.pallas import tpu as pltpu

D_MODEL = 1024
GRID_W = 64
CONV_DIM = 512
CONV_WIDTH = 31
N_HEADS = 8
QK_NOPE = 64
QK_ROPE = 32
V_DIM = 64
Q_LORA = 256
KV_LORA = 128
N_EXPERTS = 16
EXPERT_FF = 512
EC_FACTOR = 2
ROPE_BASE = 10000.0
EPS = 1e-6

LANES = 128
HEAD_W = LANES
ROPE_OFF = QK_NOPE
CONV_HALO = 16
LOG2E = 1.4426950408889634
VMEM_LIMIT = 48 * 1024 * 1024
IN_TILE = 512
CONV_CHUNK = 64
CONV_OUT_ROWS = 256
GATE_CHUNK = 512
EXPERT_ROWS = 512
Q_TILE = 512
ATTN_TOKENS = 1024
SCORE_BYTES = 12 * 1024 * 1024
MOE_ROWS = 512
MOE_TOKENS = 1024
SCATTER_TILE = 512
MOE_BLOCK = 2 * LANES
MOE_WIN = 80
WIN_ALIGN = 16

C_CONV = 0
C_QA = 2 * CONV_DIM
C_KVA = C_QA + Q_LORA
C_KR = C_KVA + KV_LORA
C_GATE = C_KR + LANES
C_END = C_GATE + 2 * D_MODEL
C_KRP = C_END

f32 = jnp.float32
bf16 = jnp.bfloat16


MIB = 1024 * 1024
VMEM_SMALL, VMEM_MID, VMEM_LARGE = 16 * MIB, 32 * MIB, 40 * MIB


def _params(vmem_bytes, *sem):
    assert vmem_bytes <= VMEM_LIMIT
    return pltpu.CompilerParams(dimension_semantics=sem, vmem_limit_bytes=vmem_bytes)


def _dot(a, b):
    return jnp.dot(a, b, preferred_element_type=f32)


def _rms(x, g):
    return x * lax.rsqrt(jnp.mean(x * x, axis=-1, keepdims=True) + EPS) * g


def _const_spec(shape):
    nd = len(shape)
    return pl.BlockSpec(shape, lambda *_: (0,) * nd)


def _mod_spec(m3, seq_of):
    if m3.shape[0] == 1:
        return _const_spec((None, 6, D_MODEL))
    return pl.BlockSpec((None, 6, D_MODEL), lambda *idx: (seq_of(*idx), 0, 0))


def _ada_kernel(s_ref, w_ref, b_ref, o_ref):
    s = s_ref[...]
    s = s * jax.nn.sigmoid(s)
    o_ref[...] = _dot(s.astype(bf16), w_ref[...].astype(bf16)) + b_ref[...]


def _ada(mod, w_ada, b_ada):
    rows = mod.shape[0]
    n_out = w_ada.shape[1]
    tn = D_MODEL
    return pl.pallas_call(
        _ada_kernel,
        grid=(n_out // tn,),
        in_specs=[
            _const_spec((rows, D_MODEL)),
            pl.BlockSpec((D_MODEL, tn), lambda j: (0, j)),
            pl.BlockSpec((1, tn), lambda j: (0, j)),
        ],
        out_specs=pl.BlockSpec((rows, tn), lambda j: (0, j)),
        out_shape=jax.ShapeDtypeStruct((rows, n_out), f32),
        compiler_params=_params(VMEM_SMALL, "arbitrary"),
        name="ada",
    )(mod, w_ada, b_ada.reshape(1, n_out))


def _wprep_kernel(wt_ref, place_ref, o_ref):
    def block(r0):
        return wt_ref[r0:r0 + LANES, :].T.astype(bf16)

    for j in range(C_KR // LANES):
        o_ref[:, j * LANES:(j + 1) * LANES] = block(j * LANES)
    placed = _dot(block(C_KR), place_ref[...])
    o_ref[:, C_KR:C_GATE] = placed[:, 0:LANES].astype(bf16)
    o_ref[:, C_KRP:C_KRP + LANES] = placed[:, LANES:2 * LANES].astype(bf16)
    for j in range(2 * D_MODEL // LANES):
        o_ref[:, C_GATE + j * LANES:C_GATE + (j + 1) * LANES] = block(C_KR + QK_ROPE + j * LANES)


def _wprep(w_in_t, place):
    cols, rows = w_in_t.shape
    return pl.pallas_call(
        _wprep_kernel,
        grid=(rows // LANES,),
        in_specs=[pl.BlockSpec((cols, LANES), lambda i: (0, i)), _const_spec(place.shape)],
        out_specs=pl.BlockSpec((LANES, C_KRP + LANES), lambda i: (i, 0)),
        out_shape=jax.ShapeDtypeStruct((rows, C_KRP + LANES), bf16),
        compiler_params=_params(VMEM_SMALL, "parallel"),
        name="wprep",
    )(w_in_t, place)


def _conv_taps(vpad, r0, shift_ref, dw_ref, dwb_ref, ybuf, y0, ct):
    pad = CONV_WIDTH // 2
    sub = 8
    span = ((CONV_HALO - pad + CONV_WIDTH - 1) // sub) * sub
    cw = 2 * LANES
    for cb in range(CONV_DIM // cw):
        sl = slice(cb * cw, (cb + 1) * cw)
        win = vpad[pl.ds(r0, ct + 2 * CONV_HALO), sl]
        acc = jnp.zeros((ct // sub, sub, cw), f32)
        for ph in range(sub):
            wph = win.astype(f32) if ph == 0 else _dot(shift_ref[ph - 1], win)
            for a in range(span // sub + 1):
                k = a * sub + ph - (CONV_HALO - pad)
                if 0 <= k < CONV_WIDTH:
                    acc = acc + wph[a * sub:a * sub + ct, :].reshape(ct // sub, sub, cw) * dw_ref[k, :, sl][None]
        ybuf[pl.ds(y0, ct), sl] = acc.reshape(ct, cw) + dwb_ref[:, sl]


def _conv_out(y, lng_ref, lnb_ref, wco_ref, gate):
    mu = jnp.mean(y, axis=-1, keepdims=True)
    yc = y - mu
    var = jnp.mean(yc * yc, axis=-1, keepdims=True)
    z = yc * lax.rsqrt(var + EPS) * lng_ref[...] + lnb_ref[...]
    z = z * jax.nn.sigmoid(z)
    return (gate.astype(f32) * _dot(z.astype(bf16), wco_ref[...])).astype(bf16)


def _fill_padded(vpad, v, n):
    zeros = jnp.zeros((CONV_HALO, CONV_DIM), vpad.dtype)
    vpad[0:CONV_HALO, :] = zeros
    vpad[CONV_HALO + n:2 * CONV_HALO + n, :] = zeros
    vpad[CONV_HALO:CONV_HALO + n, :] = v


def _inproj_kernel(*refs, rope, halo, tiles_per_seq, q_scale, n, rt, ct):
    refs = list(refs)
    x_ref = refs.pop(0)
    if halo:
        xp_ref, xn_ref = refs[:2]
        del refs[:2]
    m_ref, n1_ref, win_ref, qn_ref, wqb_ref = refs[:5]
    del refs[:5]
    if rope:
        wqbp_ref = refs.pop(0)
    kvn_ref = refs.pop(0)
    if rope:
        cos_ref, sin_ref = refs[:2]
        del refs[:2]
    shift_ref, dw_ref, dwb_ref, lng_ref, lnb_ref, wco_ref = refs[:6]
    del refs[:6]
    cm_ref, q_ref, ckv_ref, kr_ref, g_ref = refs[:5]
    del refs[:5]
    if not rope:
        kpe_ref = refs.pop(0)
    vpad, ybuf, gc = refs
    tm = x_ref.shape[0]
    lo = CONV_HALO if halo else 0

    sh1 = m_ref[0:1, :]
    sc1 = m_ref[1:2, :]
    x = jnp.concatenate([xp_ref[...], x_ref[...], xn_ref[...]], axis=0) if halo else x_ref[...]
    hb_all = (_rms(x, n1_ref[...]) * (1.0 + sc1) + sh1).astype(bf16)

    left_all = _dot(hb_all, win_ref[:, 0:C_GATE])
    v = (left_all[:, C_CONV:C_CONV + CONV_DIM] * jax.nn.sigmoid(left_all[:, C_CONV + CONV_DIM:C_QA])).astype(bf16)
    hb = hb_all[lo:lo + tm, :]
    left = left_all[lo:lo + tm, :]

    if halo:
        j = pl.program_id(0) % tiles_per_seq
        zeros = jnp.zeros((CONV_HALO, CONV_DIM), bf16)
        vpad[0, 0:lo, :] = jnp.where(j > 0, v[0:lo, :], zeros)
        vpad[0, lo:lo + tm, :] = v[lo:lo + tm, :]
        vpad[0, lo + tm:2 * lo + tm, :] = jnp.where(j < tiles_per_seq - 1, v[lo + tm:2 * lo + tm, :], zeros)
        chunks = [(0, c * ct, c * ct) for c in range(tm // ct)]
    else:
        for s in range(tm // n):
            _fill_padded(vpad.at[s], v[s * n:(s + 1) * n, :], n)
        chunks = [(s, c * ct, s * n + c * ct) for s in range(tm // n) for c in range(n // ct)]

    def conv_some(count):
        for _ in range(min(count, len(chunks))):
            s, r0, y0 = chunks.pop(0)
            _conv_taps(vpad.at[s], r0, shift_ref, dw_ref, dwb_ref, ybuf, y0, ct)

    steps = 2 + 2 * D_MODEL // GATE_CHUNK
    per_step = -(-len(chunks) // steps)

    qa = left[:, C_QA:C_KVA]
    qn = _rms(qa, qn_ref[...]).astype(bf16)
    q = _dot(qn, wqb_ref[...])
    if rope:
        qp = _dot(qn, wqbp_ref[...])
        cos = cos_ref[...]
        sin = sin_ref[...]
        for hd in range(N_HEADS):
            sl = slice(hd * HEAD_W, (hd + 1) * HEAD_W)
            q_ref[:, sl] = ((q[:, sl] * cos + qp[:, sl] * sin) * q_scale).astype(q_ref.dtype)
    else:
        q_ref[...] = (q * q_scale).astype(q_ref.dtype)
    conv_some(per_step)

    kva = left[:, C_KVA:C_KR]
    ckv_ref[...] = _rms(kva, kvn_ref[...]).astype(ckv_ref.dtype)

    kr = left[:, C_KR:C_GATE]
    if rope:
        krp = _dot(hb, win_ref[:, C_KRP:C_KRP + LANES])
        kr = kr * cos_ref[...] + krp * sin_ref[...]
    else:
        kpe_ref[...] = kr[:, ROPE_OFF:ROPE_OFF + QK_ROPE]
    kr_ref[...] = kr.astype(kr_ref.dtype)
    conv_some(per_step)

    gw = GATE_CHUNK
    for jg in range(2 * D_MODEL // gw):
        gate = jax.nn.sigmoid(_dot(hb, win_ref[:, C_GATE + jg * gw:C_GATE + (jg + 1) * gw])).astype(bf16)
        if jg * gw < D_MODEL:
            gc[:, jg * gw:(jg + 1) * gw] = gate
        else:
            g_ref[:, jg * gw - D_MODEL:(jg + 1) * gw - D_MODEL] = gate
        conv_some(per_step)

    conv_some(len(chunks))
    for r0 in range(0, tm, rt):
        cm_ref[r0:r0 + rt, :] = _conv_out(ybuf[r0:r0 + rt, :], lng_ref, lnb_ref, wco_ref, gc[r0:r0 + rt, :])


def _inproj(x2d, m3, norm1, win, q_norm, wqb, wqbp, kv_norm, cos, sin, conv_w, *, n, rope, tm,
            rt=CONV_OUT_ROWS, ct=CONV_CHUNK):
    tokens = x2d.shape[0]
    halo = n > tm
    assert n % tm == 0 if halo else (tm % n == 0 and m3.shape[0] == 1 and not rope)
    tiles_per_seq = max(1, n // tm)
    q_scale = float((QK_NOPE + QK_ROPE) ** -0.5 * LOG2E)
    tile = lambda w: pl.BlockSpec((tm, w), lambda i: (i, 0))
    in_specs = [tile(D_MODEL)]
    args = [x2d]
    if halo:
        hb_tile = tm // CONV_HALO
        hb_seq = n // CONV_HALO
        prev = lambda i: (jnp.maximum(i * hb_tile - 1, (i // tiles_per_seq) * hb_seq), 0)
        nxt = lambda i: (jnp.minimum((i + 1) * hb_tile, (i // tiles_per_seq + 1) * hb_seq - 1), 0)
        in_specs += [pl.BlockSpec((CONV_HALO, D_MODEL), prev), pl.BlockSpec((CONV_HALO, D_MODEL), nxt)]
        args += [x2d, x2d]
    in_specs += [
        _mod_spec(m3, lambda i: i // tiles_per_seq),
        _const_spec((1, D_MODEL)),
        _const_spec((D_MODEL, C_KRP + LANES if rope else C_END)),
        _const_spec((1, Q_LORA)),
        _const_spec(wqb.shape),
    ]
    args += [m3, norm1, win, q_norm, wqb]
    if rope:
        in_specs.append(_const_spec(wqbp.shape))
        args.append(wqbp)
    in_specs.append(_const_spec((1, KV_LORA)))
    args.append(kv_norm)
    if rope:
        tab = pl.BlockSpec((tm, LANES), lambda i: (i % tiles_per_seq, 0))
        in_specs += [tab, tab]
        args += [cos, sin]
    args += list(conv_w)
    in_specs += [_const_spec(w.shape) for w in conv_w]
    pieces = max(1, tm // n)
    scratch = [
        pltpu.VMEM((pieces, tm // pieces + 2 * CONV_HALO, CONV_DIM), bf16),
        pltpu.VMEM((tm, CONV_DIM), f32),
        pltpu.VMEM((tm, D_MODEL), bf16),
    ]
    out_shape = [
        jax.ShapeDtypeStruct((tokens, D_MODEL), bf16),
        jax.ShapeDtypeStruct((tokens, N_HEADS * HEAD_W), bf16),
        jax.ShapeDtypeStruct((tokens, KV_LORA), bf16 if rope else f32),
        jax.ShapeDtypeStruct((tokens, LANES), bf16),
        jax.ShapeDtypeStruct((tokens, D_MODEL), bf16),
    ]
    out_specs = [tile(D_MODEL), tile(N_HEADS * HEAD_W), tile(KV_LORA), tile(LANES), tile(D_MODEL)]
    if not rope:
        out_shape.append(jax.ShapeDtypeStruct((tokens, QK_ROPE), f32))
        out_specs.append(tile(QK_ROPE))
    return pl.pallas_call(
        functools.partial(_inproj_kernel, rope=rope, halo=halo, tiles_per_seq=tiles_per_seq, q_scale=q_scale,
                          n=n, rt=rt, ct=ct),
        grid=(tokens // tm,),
        in_specs=in_specs,
        out_specs=out_specs,
        out_shape=out_shape,
        scratch_shapes=scratch,
        compiler_params=_params(VMEM_MID, "parallel"),
        name="inproj_conv_rope" if rope else "inproj_conv",
    )(*args)


def _conv_weights(dw, dwb, lng, lnb, wco, ct):
    rows = ct + 2 * CONV_HALO
    i = np.arange(rows)
    shifts = jnp.asarray(np.stack([(i[None, :] == i[:, None] + ph) for ph in range(1, 8)]), bf16)
    dw_tiles = jnp.broadcast_to(dw[:, None, :], (CONV_WIDTH, 8, CONV_DIM))
    return shifts, dw_tiles, dwb, lng, lnb, wco


def _attn_kernel(q_ref, ckv_ref, kpe_ref, cm_ref, g_ref, x_ref, m_ref, wk_ref, wv_ref, wo_ref, wout_ref, n2_ref,
                 wr_ref, x1_ref, h2_ref, lg_ref, k_scr, v_scr, q_scr, *, sb, hg):
    s_len = ckv_ref.shape[1]

    @pl.when(pl.program_id(1) == 0)
    def _():
        ckv = ckv_ref[...].reshape(sb * s_len, KV_LORA).astype(bf16)
        kpe = kpe_ref[...].reshape(sb * s_len, LANES).astype(f32)
        k = _dot(ckv, wk_ref[...])
        v = _dot(ckv, wv_ref[...])
        ones = jnp.ones((s_len, V_DIM), bf16)
        for g in range(sb):
            rows = slice(g * s_len, (g + 1) * s_len)
            for hd in range(N_HEADS):
                u = g * N_HEADS + hd
                k_scr[u] = (k[rows, hd * HEAD_W:(hd + 1) * HEAD_W] + kpe[rows, :]).astype(bf16)
                vh = v[rows, hd * V_DIM:(hd + 1) * V_DIM].astype(bf16)
                if hd % 2 == 0:
                    v_scr[u, :, 0:V_DIM] = vh
                    v_scr[u, :, V_DIM:LANES] = ones
                else:
                    v_scr[u, :, 0:V_DIM] = ones
                    v_scr[u, :, V_DIM:LANES] = vh

    tq = q_ref.shape[0] // sb
    units = sb * N_HEADS
    for g in range(sb):
        for hd in range(N_HEADS):
            q_scr[g * N_HEADS + hd] = q_ref[g * tq:(g + 1) * tq, hd * HEAD_W:(hd + 1) * HEAD_W]
    lane = lax.broadcasted_iota(jnp.int32, (tq, LANES), 1)
    pairs = []
    for u0 in range(0, units, hg):
        us = slice(u0, u0 + hg)
        s = lax.dot_general(q_scr[us], k_scr[us], (((2,), (2,)), ((0,), (0,))), preferred_element_type=f32)
        mx = jnp.max(s, axis=-1, keepdims=True)
        p = jnp.exp2(s - mx).astype(bf16)
        r = lax.dot_general(p, v_scr[us], (((2,), (1,)), ((0,), (0,))), preferred_element_type=f32)
        for j in range(hg // 2):
            re, ro = r[2 * j], r[2 * j + 1]
            oe = re * (1.0 / re[:, V_DIM:V_DIM + 1])
            oo = ro * (1.0 / ro[:, 0:1])
            pairs.append(jnp.where(lane < V_DIM, oe, oo))
    per_seq = N_HEADS // 2
    attn = jnp.concatenate(
        [jnp.concatenate(pairs[g * per_seq:(g + 1) * per_seq], axis=-1) for g in range(sb)], axis=0).astype(bf16)
    merged = (cm_ref[...].astype(f32) + g_ref[...].astype(f32) * _dot(attn, wo_ref[...])).astype(bf16)

    g1 = m_ref[2:3, :]
    sh2 = m_ref[3:4, :]
    sc2 = m_ref[4:5, :]
    x1 = x_ref[...] + g1 * _dot(merged, wout_ref[...])
    x1_ref[...] = x1
    h2 = _rms(x1, n2_ref[...]) * (1.0 + sc2) + sh2
    hi = h2.astype(bf16)
    h2_ref[...] = hi
    lo = (h2 - hi.astype(f32)).astype(bf16)
    both = _dot(jnp.concatenate([hi, lo], axis=0), wr_ref[...])
    rows = hi.shape[0]
    s = both[0:rows] + both[rows:2 * rows]
    lg_ref[...] = s + pltpu.roll(s, LANES - N_EXPERTS, axis=1)


def _attn(q2d, ckv3, kpe3, cm2d, g2d, x2d, m3, wk, wv, wo, wout, norm2, wr, *, n, tq):
    nseq, s_len, _ = ckv3.shape
    sb = max(1, ATTN_TOKENS // n) if (tq == n and m3.shape[0] == 1) else 1
    units = sb * N_HEADS
    hg = max(2, min(units, SCORE_BYTES // (tq * s_len * 4)))
    assert units % hg == 0 and hg % 2 == 0 and nseq % sb == 0
    qb = n // tq
    tokens = nseq * n
    tile = lambda w: pl.BlockSpec((sb * tq, w), lambda s, i: (s * qb + i, 0))
    return pl.pallas_call(
        functools.partial(_attn_kernel, sb=sb, hg=hg),
        grid=(nseq // sb, qb),
        in_specs=[
            tile(N_HEADS * HEAD_W),
            pl.BlockSpec((sb, s_len, KV_LORA), lambda s, i: (s, 0, 0)),
            pl.BlockSpec((sb, s_len, LANES), lambda s, i: (s, 0, 0)),
            tile(D_MODEL),
            tile(D_MODEL),
            tile(D_MODEL),
            _mod_spec(m3, lambda s, i: s),
            _const_spec(wk.shape),
            _const_spec(wv.shape),
            _const_spec(wo.shape),
            _const_spec(wout.shape),
            _const_spec((1, D_MODEL)),
            _const_spec(wr.shape),
        ],
        out_specs=(tile(D_MODEL), tile(D_MODEL), tile(LANES)),
        out_shape=(
            jax.ShapeDtypeStruct((tokens, D_MODEL), f32),
            jax.ShapeDtypeStruct((tokens, D_MODEL), bf16),
            jax.ShapeDtypeStruct((tokens, LANES), f32),
        ),
        scratch_shapes=[
            pltpu.VMEM((units, s_len, HEAD_W), bf16),
            pltpu.VMEM((units, s_len, LANES), bf16),
            pltpu.VMEM((units, tq, HEAD_W), bf16),
        ],
        compiler_params=_params(VMEM_LIMIT, "parallel", "arbitrary"),
        name="attn_out",
    )(q2d, ckv3, kpe3, cm2d, g2d, x2d, m3, wk, wv, wo, wout, norm2, wr)


def _route_kernel(lg_ref, pos_ref, aff_ref, meta_ref, *, nseq, n, cap):
    for s in range(nseq):
        lt = lg_ref[s].T[0:N_EXPERTS, :]
        e = jnp.exp(lt - jnp.max(lt, axis=0, keepdims=True))
        aff_ref[s * N_EXPERTS:(s + 1) * N_EXPERTS, :] = e / jnp.sum(e, axis=0, keepdims=True)
    rows = nseq * N_EXPERTS
    capf = float(cap)

    def bit_step(i, t):
        cand = t | (jnp.int32(1) << (30 - i))
        thr = lax.bitcast_convert_type(cand, f32)
        cnt = jnp.sum(jnp.where(aff_ref[...] >= thr, 1.0, 0.0), axis=1, keepdims=True)
        return jnp.where(cnt >= capf, cand, t)

    t = lax.fori_loop(0, 31, bit_step, jnp.zeros((rows, 1), jnp.int32))
    thr = lax.bitcast_convert_type(t, f32)
    need = capf - jnp.sum(jnp.where(aff_ref[...] > thr, 1.0, 0.0), axis=1, keepdims=True)

    blk = MOE_BLOCK
    tri = jnp.where(
        lax.broadcasted_iota(jnp.int32, (blk, blk), 0) < lax.broadcasted_iota(jnp.int32, (blk, blk), 1),
        1.0, 0.0).astype(bf16)
    carry_gt = jnp.zeros((rows, 1), f32)
    carry_eq = jnp.zeros((rows, 1), f32)
    lane = lax.broadcasted_iota(jnp.int32, (rows, LANES), 1)
    meta = jnp.zeros((rows, LANES), f32)
    cmax = jnp.zeros((rows, 1), f32)
    for b in range(n // blk):
        sl = slice(b * blk, (b + 1) * blk)
        ab = aff_ref[:, sl]
        gt = ab > thr
        eq = ab == thr
        gtb = jnp.where(gt, 1.0, 0.0)
        eqb = jnp.where(eq, 1.0, 0.0)
        pre_gt = _dot(gtb.astype(bf16), tri) + carry_gt
        pre_eq = _dot(eqb.astype(bf16), tri) + carry_eq
        meta = jnp.where(lane == b, carry_gt + jnp.minimum(carry_eq, need), meta)
        carry_gt = carry_gt + jnp.sum(gtb, axis=1, keepdims=True)
        carry_eq = carry_eq + jnp.sum(eqb, axis=1, keepdims=True)
        sel = gt | (eq & (pre_eq < need))
        slot = pre_gt + jnp.minimum(pre_eq, need)
        pos_ref[:, sl] = jnp.where(sel, slot, -1.0).astype(jnp.int32)
        cmax = jnp.maximum(cmax, jnp.sum(jnp.where(sel, 1.0, 0.0), axis=1, keepdims=True))
    meta_ref[...] = jnp.where(lane == LANES - 1, cmax, meta).astype(jnp.int32)


def _route(lg3, *, cap):
    nseq, n, _ = lg3.shape
    rows = nseq * N_EXPERTS
    assert n // MOE_BLOCK < LANES
    return pl.pallas_call(
        functools.partial(_route_kernel, nseq=nseq, n=n, cap=cap),
        grid=(1,),
        in_specs=[_const_spec(lg3.shape)],
        out_specs=(_const_spec((rows, n)), _const_spec((rows, n)), _const_spec((rows, LANES))),
        out_shape=(jax.ShapeDtypeStruct((rows, n), jnp.int32), jax.ShapeDtypeStruct((rows, n), f32),
                   jax.ShapeDtypeStruct((rows, LANES), jnp.int32)),
        compiler_params=_params(VMEM_SMALL, "arbitrary"),
        name="route",
    )(lg3)


def _slot_hits(pos_ref, e, cap):
    width = pos_ref.shape[1]
    return lax.broadcasted_iota(jnp.int32, (cap, width), 0) == pos_ref[e:e + 1, :]


def _one_hot(hits):
    return jnp.concatenate([jnp.where(h, 1.0, 0.0).astype(bf16) for h in hits], axis=0)


def _gather_dense(pos_ref, aff_ref, h2_ref, xg_ref, vals_ref, rows, cap, ne):
    for e0 in range(0, N_EXPERTS, ne):
        hits = [_slot_hits(pos_ref, e0 + e, cap) for e in range(ne)]
        xg = _dot(_one_hot(hits), h2_ref[...]).astype(xg_ref.dtype)
        for e in range(ne):
            xg_ref[e0 + e, rows, :] = xg[e * cap:(e + 1) * cap, :]
            vals = jnp.sum(jnp.where(hits[e], aff_ref[e0 + e:e0 + e + 1, :], 0.0), axis=1, keepdims=True)
            vals_ref[e0 + e, rows, :] = jnp.broadcast_to(vals, (cap, LANES))


def _gather_kernel(pos_ref, aff_ref, h2_ref, xg_ref, vals_ref, *, cap, ne, sb):
    for g in range(sb):
        _gather_dense(pos_ref.at[g], aff_ref.at[g], h2_ref.at[g], xg_ref, vals_ref,
                      slice(g * cap, (g + 1) * cap), cap, ne)


def _gather(pos3, aff3, h23, *, cap, ne, sb):
    nseq, _, n = pos3.shape
    return pl.pallas_call(
        functools.partial(_gather_kernel, cap=cap, ne=ne, sb=sb),
        grid=(nseq // sb,),
        in_specs=[
            pl.BlockSpec((sb, N_EXPERTS, n), lambda s: (s, 0, 0)),
            pl.BlockSpec((sb, N_EXPERTS, n), lambda s: (s, 0, 0)),
            pl.BlockSpec((sb, n, D_MODEL), lambda s: (s, 0, 0)),
        ],
        out_specs=(
            pl.BlockSpec((N_EXPERTS, sb * cap, D_MODEL), lambda s: (0, s, 0)),
            pl.BlockSpec((N_EXPERTS, sb * cap, LANES), lambda s: (0, s, 0)),
        ),
        out_shape=(
            jax.ShapeDtypeStruct((N_EXPERTS, nseq * cap, D_MODEL), bf16),
            jax.ShapeDtypeStruct((N_EXPERTS, nseq * cap, LANES), f32),
        ),
        compiler_params=_params(VMEM_MID, "parallel"),
        name="gather",
    )(pos3, aff3, h23)


def _windows_fit(meta_ref):
    cmax = meta_ref[0, LANES - 1]
    for e in range(1, N_EXPERTS):
        cmax = jnp.maximum(cmax, meta_ref[e, LANES - 1])
    return cmax <= MOE_WIN - WIN_ALIGN


def _win_base(meta_ref, e, b, cap):
    start = meta_ref[e, b]
    return pl.multiple_of(jnp.minimum(start - start % WIN_ALIGN, cap - MOE_WIN), WIN_ALIGN)


def _win_hits(pos_row, base):
    slot = lax.broadcasted_iota(jnp.int32, (MOE_WIN, pos_row.shape[1]), 0) + base
    return slot == pos_row


def _gather_win_kernel(meta_ref, pos_ref, aff_ref, h2_ref, xg_ref, vals_ref, *, n, cap):
    fits = _windows_fit(meta_ref)

    @pl.when(fits)
    def _():
        xg_ref[...] = jnp.zeros(xg_ref.shape, xg_ref.dtype)
        vals_ref[...] = jnp.zeros(vals_ref.shape, vals_ref.dtype)

        def block(b, carry):
            cols = pl.ds(pl.multiple_of(b * MOE_BLOCK, MOE_BLOCK), MOE_BLOCK)
            bases = [_win_base(meta_ref, e, b, cap) for e in range(N_EXPERTS)]
            hits = [_win_hits(pos_ref[e:e + 1, cols], bases[e]) for e in range(N_EXPERTS)]
            part = _dot(_one_hot(hits), h2_ref[cols, :])
            for e in range(N_EXPERTS):
                rows = pl.ds(bases[e], MOE_WIN)
                xg_ref[e, rows, :] += part[e * MOE_WIN:(e + 1) * MOE_WIN, :].astype(xg_ref.dtype)
                vals = jnp.sum(jnp.where(hits[e], aff_ref[e:e + 1, cols], 0.0), axis=1, keepdims=True)
                vals_ref[e, rows, :] += jnp.broadcast_to(vals, (MOE_WIN, LANES))
            return carry

        lax.fori_loop(0, n // MOE_BLOCK, block, 0)

    @pl.when(jnp.logical_not(fits))
    def _():
        _gather_dense(pos_ref, aff_ref, h2_ref, xg_ref, vals_ref, slice(0, cap), cap, 1)


def _gather_win(meta, pos3, aff3, h23, *, cap):
    nseq, _, n = pos3.shape
    return pl.pallas_call(
        functools.partial(_gather_win_kernel, n=n, cap=cap),
        grid=(nseq,),
        in_specs=[
            pl.BlockSpec((N_EXPERTS, LANES), lambda s: (s, 0), memory_space=pltpu.SMEM),
            pl.BlockSpec((None, N_EXPERTS, n), lambda s: (s, 0, 0)),
            pl.BlockSpec((None, N_EXPERTS, n), lambda s: (s, 0, 0)),
            pl.BlockSpec((None, n, D_MODEL), lambda s: (s, 0, 0)),
        ],
        out_specs=(
            pl.BlockSpec((N_EXPERTS, cap, D_MODEL), lambda s: (0, s, 0)),
            pl.BlockSpec((N_EXPERTS, cap, LANES), lambda s: (0, s, 0)),
        ),
        out_shape=(
            jax.ShapeDtypeStruct((N_EXPERTS, nseq * cap, D_MODEL), bf16),
            jax.ShapeDtypeStruct((N_EXPERTS, nseq * cap, LANES), f32),
        ),
        compiler_params=_params(VMEM_LARGE, "parallel"),
        name="gather_win",
    )(meta, pos3, aff3, h23)


def _experts_kernel(xp_ref, vp_ref, xs_ref, vs_ref, wg_ref, wu_ref, wd_ref, yp_ref, ys_ref, *, rc):
    wg = wg_ref[...].astype(bf16)
    wu = wu_ref[...].astype(bf16)
    wd = wd_ref[...].astype(bf16)
    for x_ref, v_ref, y_ref in ((xp_ref, vp_ref, yp_ref), (xs_ref, vs_ref, ys_ref)):
        for r0 in range(0, x_ref.shape[0], rc):
            x = x_ref[r0:r0 + rc, :]
            a = _dot(x, wg)
            u = _dot(x, wu)
            hm = (a * jax.nn.sigmoid(a) * u).astype(bf16)
            y = _dot(hm, wd) * v_ref[r0:r0 + rc, 0:1]
            y_ref[r0:r0 + rc, :] = y.astype(y_ref.dtype)


def _experts(xg_p, vals_p, xg_s, vals_s, wg, wu, wd, *, rc=EXPERT_ROWS):
    rp = xg_p.shape[1]
    rs = xg_s.shape[1]
    per_e = lambda r, w: pl.BlockSpec((None, r, w), lambda e: (e, 0, 0))
    return pl.pallas_call(
        functools.partial(_experts_kernel, rc=rc),
        grid=(N_EXPERTS,),
        in_specs=[
            per_e(rp, D_MODEL), per_e(rp, LANES), per_e(rs, D_MODEL), per_e(rs, LANES),
            per_e(D_MODEL, EXPERT_FF), per_e(D_MODEL, EXPERT_FF), per_e(EXPERT_FF, D_MODEL),
        ],
        out_specs=(per_e(rp, D_MODEL), per_e(rs, D_MODEL)),
        out_shape=(
            jax.ShapeDtypeStruct((N_EXPERTS, rp, D_MODEL), bf16),
            jax.ShapeDtypeStruct((N_EXPERTS, rs, D_MODEL), bf16),
        ),
        compiler_params=_params(VMEM_LARGE, "parallel"),
        name="experts",
    )(xg_p, vals_p, xg_s, vals_s, wg, wu, wd)


def _scatter_dense(pos_ref, y_ref, rows, tn, cap, ne):
    moe = jnp.zeros((tn, D_MODEL), f32)
    for e0 in range(0, N_EXPERTS, ne):
        onehot = _one_hot([_slot_hits(pos_ref, e0 + e, cap) for e in range(ne)])
        y = jnp.concatenate([y_ref[e0 + e, rows, :] for e in range(ne)], axis=0)
        moe = moe + lax.dot_general(onehot, y, (((0,), (0,)), ((), ())), preferred_element_type=f32)
    return moe


def _scatter_kernel(pos_ref, y_ref, x1_ref, m_ref, fn_ref, o_ref, *, tn, cap, ne, sb):
    g2 = m_ref[5:6, :]
    for g in range(sb):
        moe = _scatter_dense(pos_ref.at[g], y_ref, slice(g * cap, (g + 1) * cap), tn, cap, ne)
        o_ref[g] = _rms(x1_ref[g] + g2 * moe, fn_ref[...])


def _scatter(pos3, y, x13, m3, fn, *, cap, ne, tn, sb):
    nseq, _, n = pos3.shape
    assert sb == 1 or m3.shape[0] == 1, "sequences sharing a grid step must share their modulation rows"
    return pl.pallas_call(
        functools.partial(_scatter_kernel, tn=tn, cap=cap, ne=ne, sb=sb),
        grid=(nseq // sb, n // tn),
        in_specs=[
            pl.BlockSpec((sb, N_EXPERTS, tn), lambda s, i: (s, 0, i)),
            pl.BlockSpec((N_EXPERTS, sb * cap, D_MODEL), lambda s, i: (0, s, 0)),
            pl.BlockSpec((sb, tn, D_MODEL), lambda s, i: (s, i, 0)),
            _mod_spec(m3, lambda s, i: s),
            _const_spec((1, D_MODEL)),
        ],
        out_specs=pl.BlockSpec((sb, tn, D_MODEL), lambda s, i: (s, i, 0)),
        out_shape=jax.ShapeDtypeStruct((nseq, n, D_MODEL), f32),
        compiler_params=_params(VMEM_MID, "parallel", "arbitrary"),
        name="scatter",
    )(pos3, y, x13, m3, fn)


def _scatter_win_kernel(meta_ref, pos_ref, y_ref, x1_ref, m_ref, fn_ref, o_ref, moe_scr, *, cap):
    b = pl.program_id(1)
    fits = _windows_fit(meta_ref)

    @pl.when(fits)
    def _():
        bases = [_win_base(meta_ref, e, b, cap) for e in range(N_EXPERTS)]
        onehot = _one_hot([_win_hits(pos_ref[e:e + 1, :], bases[e]) for e in range(N_EXPERTS)])
        y = jnp.concatenate([y_ref[e, pl.ds(bases[e], MOE_WIN), :] for e in range(N_EXPERTS)], axis=0)
        moe_scr[...] = lax.dot_general(onehot, y, (((0,), (0,)), ((), ())), preferred_element_type=f32)

    @pl.when(jnp.logical_not(fits))
    def _():
        moe_scr[...] = _scatter_dense(pos_ref, y_ref, slice(0, cap), MOE_BLOCK, cap, 1)

    o_ref[...] = _rms(x1_ref[...] + m_ref[5:6, :] * moe_scr[...], fn_ref[...])


def _scatter_win(meta, pos3, y, x13, m3, fn, *, cap):
    nseq, _, n = pos3.shape
    tn = MOE_BLOCK
    return pl.pallas_call(
        functools.partial(_scatter_win_kernel, cap=cap),
        grid=(nseq, n // tn),
        in_specs=[
            pl.BlockSpec((N_EXPERTS, LANES), lambda s, i: (s, 0), memory_space=pltpu.SMEM),
            pl.BlockSpec((None, N_EXPERTS, tn), lambda s, i: (s, 0, i)),
            pl.BlockSpec((N_EXPERTS, cap, D_MODEL), lambda s, i: (0, s, 0)),
            pl.BlockSpec((None, tn, D_MODEL), lambda s, i: (s, i, 0)),
            _mod_spec(m3, lambda s, i: s),
            _const_spec((1, D_MODEL)),
        ],
        out_specs=pl.BlockSpec((None, tn, D_MODEL), lambda s, i: (s, i, 0)),
        out_shape=jax.ShapeDtypeStruct((nseq, n, D_MODEL), f32),
        scratch_shapes=[pltpu.VMEM((tn, D_MODEL), f32)],
        compiler_params=_params(VMEM_MID, "parallel", "arbitrary"),
        name="scatter_win",
    )(meta, pos3, y, x13, m3, fn)


def _rope_tables(n):
    t = np.arange(n)
    half = QK_ROPE // 2
    freqs = ROPE_BASE ** (-np.arange(0, half, 2, dtype=np.float64) / half)
    ang_r = (t // GRID_W)[:, None] * freqs
    ang_c = (t % GRID_W)[:, None] * freqs
    cr, sr, cc, sc = np.cos(ang_r), np.sin(ang_r), np.cos(ang_c), np.sin(ang_c)
    cos = np.ones((n, HEAD_W))
    sin = np.zeros((n, HEAD_W))
    cos[:, ROPE_OFF:ROPE_OFF + QK_ROPE] = np.concatenate([cr, cr, cc, cc], axis=-1)
    sin[:, ROPE_OFF:ROPE_OFF + QK_ROPE] = np.concatenate([-sr, sr, -sc, sc], axis=-1)
    return jnp.asarray(cos, f32), jnp.asarray(sin, f32)


_PARTNER = np.concatenate([np.arange(8, 16), np.arange(0, 8), np.arange(24, 32), np.arange(16, 24)])


def _rope_partner(w):
    q = QK_ROPE // 4
    return jnp.concatenate([w[..., q:2 * q], w[..., 0:q], w[..., 3 * q:4 * q], w[..., 2 * q:3 * q]], axis=-1)


def _rope_placement():
    place = np.zeros((LANES, 2 * LANES), np.float32)
    d = np.arange(QK_ROPE)
    place[d, ROPE_OFF + d] = 1.0
    place[_PARTNER, LANES + ROPE_OFF + d] = 1.0
    return jnp.asarray(place, bf16)


def _head_blocks(w_nope, w_rope):
    rows = w_nope.shape[0]
    if w_rope is None:
        w_rope = jnp.zeros((rows, N_HEADS, QK_ROPE), w_nope.dtype)
    z = jnp.zeros((rows, N_HEADS, HEAD_W - QK_NOPE - QK_ROPE), w_nope.dtype)
    return jnp.concatenate([w_nope, w_rope, z], axis=-1).reshape(rows, N_HEADS * HEAD_W)


def kernel(x_prompt, x_sample, cache_ckv, cache_kpe, c, c_ctx, w_ada, b_ada, norm1, w_in, conv_dw, conv_dw_b,
           conv_ln_g, conv_ln_b, w_conv_out, q_norm, w_qb, kv_norm, w_kvb, w_o_mla, w_out, norm2, w_router,
           w_e_gate, w_e_up, w_e_down, final_norm):
    assert w_ada.shape[0] == 1, "single trunk layer"
    nb_p, n_p, _ = x_prompt.shape
    nb_s, n_s, _ = x_sample.shape

    win = _wprep(w_in[0].T, _rope_placement())
    wq = w_qb[0].reshape(Q_LORA, N_HEADS, QK_NOPE + QK_ROPE)
    wqb = _head_blocks(wq[..., :QK_NOPE], wq[..., QK_NOPE:]).astype(bf16)
    wqbp = _head_blocks(jnp.zeros_like(wq[..., :QK_NOPE]), _rope_partner(wq[..., QK_NOPE:])).astype(bf16)
    wkv = w_kvb[0].reshape(KV_LORA, N_HEADS, QK_NOPE + V_DIM)
    wk = _head_blocks(wkv[..., :QK_NOPE], None).astype(bf16)
    wv = wkv[..., QK_NOPE:].reshape(KV_LORA, N_HEADS * V_DIM).astype(bf16)
    wco = w_conv_out[0].astype(bf16)
    wo = w_o_mla[0].astype(bf16)
    wout = w_out[0].astype(bf16)
    wr_hi = w_router[0].astype(bf16)
    wr_lo = (w_router[0] - wr_hi.astype(f32)).astype(bf16)
    wr = jnp.concatenate([wr_hi, wr_lo, jnp.zeros((D_MODEL, LANES - 2 * N_EXPERTS), bf16)], axis=-1)
    row = lambda a: a.reshape(1, -1)

    mod = jnp.concatenate([c_ctx[None, :], c, jnp.zeros((8 - 1 - nb_s, D_MODEL), f32)], axis=0)
    m = _ada(mod, w_ada[0], b_ada[0]).reshape(8, 6, D_MODEL)
    m_p, m_s = m[0:1], m[1:1 + nb_s]
    cos, sin = _rope_tables(n_s)

    conv_w = _conv_weights(conv_dw[0], row(conv_dw_b[0]), row(conv_ln_g[0]), row(conv_ln_b[0]), wco, CONV_CHUNK)

    def mixers(x, m3, rope, ctx_ckv, ctx_kpe):
        nseq, n, _ = x.shape
        x2d = x.reshape(nseq * n, D_MODEL)
        cm, q, ckv, kr, g, *kpe = _inproj(x2d, m3, row(norm1[0]), win, row(q_norm[0]), wqb, wqbp, row(kv_norm[0]),
                                          cos, sin, conv_w, n=n, rope=rope, tm=IN_TILE)
        keys_ckv = ckv.reshape(nseq, n, KV_LORA)
        keys_kpe = kr.reshape(nseq, n, LANES)
        if ctx_ckv is not None:
            keys_ckv = jnp.concatenate([ctx_ckv.astype(keys_ckv.dtype), keys_ckv], axis=1)
            keys_kpe = jnp.concatenate([ctx_kpe.astype(keys_kpe.dtype), keys_kpe], axis=1)
        x1, h2, lg = _attn(q, keys_ckv, keys_kpe, cm.reshape(nseq * n, D_MODEL), g, x2d, m3, wk, wv, wo, wout,
                           row(norm2[0]), wr, n=n, tq=min(n, Q_TILE))
        return x1, h2, lg, ckv, kpe

    ctx_kpe = jnp.pad(cache_kpe[:, 0], ((0, 0), (0, 0), (ROPE_OFF, LANES - ROPE_OFF - QK_ROPE)))
    x1_p, h2_p, lg_p, ckv_p, (kpe_p,) = mixers(x_prompt, m_p, False, None, None)
    x1_s, h2_s, lg_s, _, _ = mixers(x_sample, m_s, True, cache_ckv[:, 0], ctx_kpe)

    def moe_tiles(n):
        cap = EC_FACTOR * n // N_EXPERTS
        ne = N_EXPERTS if N_EXPERTS * cap <= MOE_ROWS else 1
        sb = max(1, MOE_TOKENS // n)
        return cap, ne, sb

    def windowed(n):
        return n >= 4 * MOE_BLOCK and EC_FACTOR * n // N_EXPERTS >= 2 * MOE_WIN

    def route_gather(h2, lg, nseq, n):
        cap, ne, sb = moe_tiles(n)
        pos, aff, meta = _route(lg.reshape(nseq, n, LANES), cap=cap)
        pos3 = pos.reshape(nseq, N_EXPERTS, n)
        aff3 = aff.reshape(nseq, N_EXPERTS, n)
        h23 = h2.reshape(nseq, n, D_MODEL)
        if windowed(n):
            xg, vals = _gather_win(meta, pos3, aff3, h23, cap=cap)
        else:
            xg, vals = _gather(pos3, aff3, h23, cap=cap, ne=ne, sb=sb)
        return pos3, meta, xg, vals

    pos_p, meta_p, xg_p, vals_p = route_gather(h2_p, lg_p, nb_p, n_p)
    pos_s, meta_s, xg_s, vals_s = route_gather(h2_s, lg_s, nb_s, n_s)
    y_p, y_s = _experts(xg_p, vals_p, xg_s, vals_s, w_e_gate[0], w_e_up[0], w_e_down[0])
    fn = row(final_norm)

    def scatter(pos, meta, y, x1, m3, nseq, n):
        cap, ne, sb = moe_tiles(n)
        x13 = x1.reshape(nseq, n, D_MODEL)
        if windowed(n):
            return _scatter_win(meta, pos, y, x13, m3, fn, cap=cap)
        return _scatter(pos, y, x13, m3, fn, cap=cap, ne=ne, tn=min(n, SCATTER_TILE), sb=sb)

    y_prompt = scatter(pos_p, meta_p, y_p, x1_p, m_p, nb_p, n_p)
    y_sample = scatter(pos_s, meta_s, y_s, x1_s, m_s, nb_s, n_s)

    new_ckv = ckv_p.reshape(nb_p, 1, n_p, KV_LORA)
    new_kpe = kpe_p.reshape(nb_p, 1, n_p, QK_ROPE)
    return (y_prompt, y_sample, new_ckv, new_kpe)
```

```python
import functools

import jax
import jax.numpy as jnp
import numpy as np
from jax import lax
from jax.experimental import pallas as pl
from jax.experimental.pallas import tpu as pltpu

D_MODEL = 1024
GRID_W = 64
CONV_DIM = 512
CONV_WIDTH = 31
N_HEADS = 8
QK_NOPE = 64
QK_ROPE = 32
V_DIM = 64
Q_LORA = 256
KV_LORA = 128
N_EXPERTS = 16
EXPERT_FF = 512
EC_FACTOR = 2
ROPE_BASE = 10000.0
EPS = 1e-6

LANES = 128
HEAD_W = LANES
ROPE_OFF = QK_NOPE
CONV_HALO = 16
LOG2E = 1.4426950408889634
VMEM_LIMIT = 48 * 1024 * 1024
IN_TILE = 512
CONV_CHUNK = 64
CONV_OUT_ROWS = 256
GATE_CHUNK = 512
EXPERT_ROWS = 512
Q_TILE = 512
ATTN_TOKENS = 1024
SCORE_BYTES = 12 * 1024 * 1024
MOE_ROWS = 512
MOE_TOKENS = 1024
SCATTER_TILE = 512
MOE_BLOCK = 2 * LANES
MOE_WIN = 80
WIN_ALIGN = 16

C_CONV = 0
C_QA = 2 * CONV_DIM
C_KVA = C_QA + Q_LORA
C_KR = C_KVA + KV_LORA
C_GATE = C_KR + LANES
C_END = C_GATE + 2 * D_MODEL
C_KRP = C_END

f32 = jnp.float32
bf16 = jnp.bfloat16


MIB = 1024 * 1024
VMEM_SMALL, VMEM_MID, VMEM_LARGE = 16 * MIB, 32 * MIB, 40 * MIB


def _params(vmem_bytes, *sem):
    assert vmem_bytes <= VMEM_LIMIT
    return pltpu.CompilerParams(dimension_semantics=sem, vmem_limit_bytes=vmem_bytes)


def _dot(a, b):
    return jnp.dot(a, b, preferred_element_type=f32)


def _rms(x, g):
    return x * lax.rsqrt(jnp.mean(x * x, axis=-1, keepdims=True) + EPS) * g


def _const_spec(shape):
    nd = len(shape)
    return pl.BlockSpec(shape, lambda *_: (0,) * nd)


def _mod_spec(m3, seq_of):
    if m3.shape[0] == 1:
        return _const_spec((None, 6, D_MODEL))
    return pl.BlockSpec((None, 6, D_MODEL), lambda *idx: (seq_of(*idx), 0, 0))


def _ada_kernel(s_ref, w_ref, b_ref, o_ref):
    s = s_ref[...]
    s = s * jax.nn.sigmoid(s)
    o_ref[...] = _dot(s.astype(bf16), w_ref[...].astype(bf16)) + b_ref[...]


def _ada(mod, w_ada, b_ada):
    rows = mod.shape[0]
    n_out = w_ada.shape[1]
    tn = D_MODEL
    return pl.pallas_call(
        _ada_kernel,
        grid=(n_out // tn,),
        in_specs=[
            _const_spec((rows, D_MODEL)),
            pl.BlockSpec((D_MODEL, tn), lambda j: (0, j)),
            pl.BlockSpec((1, tn), lambda j: (0, j)),
        ],
        out_specs=pl.BlockSpec((rows, tn), lambda j: (0, j)),
        out_shape=jax.ShapeDtypeStruct((rows, n_out), f32),
        compiler_params=_params(VMEM_LIMIT, "arbitrary"),
        name="ada",
    )(mod, w_ada, b_ada.reshape(1, n_out))


def _wprep_kernel(wt_ref, place_ref, o_ref):
    def block(r0):
        return wt_ref[r0:r0 + LANES, :].T.astype(bf16)

    for j in range(C_KR // LANES):
        o_ref[:, j * LANES:(j + 1) * LANES] = block(j * LANES)
    placed = _dot(block(C_KR), place_ref[...])
    o_ref[:, C_KR:C_GATE] = placed[:, 0:LANES].astype(bf16)
    o_ref[:, C_KRP:C_KRP + LANES] = placed[:, LANES:2 * LANES].astype(bf16)
    for j in range(2 * D_MODEL // LANES):
        o_ref[:, C_GATE + j * LANES:C_GATE + (j + 1) * LANES] = block(C_KR + QK_ROPE + j * LANES)


def _wprep(w_in_t, place):
    cols, rows = w_in_t.shape
    return pl.pallas_call(
        _wprep_kernel,
        grid=(rows // LANES,),
        in_specs=[pl.BlockSpec((cols, LANES), lambda i: (0, i)), _const_spec(place.shape)],
        out_specs=pl.BlockSpec((LANES, C_KRP + LANES), lambda i: (i, 0)),
        out_shape=jax.ShapeDtypeStruct((rows, C_KRP + LANES), bf16),
        compiler_params=_params(VMEM_SMALL, "parallel"),
        name="wprep",
    )(w_in_t, place)


def _conv_taps(vpad, r0, shift_ref, dw_ref, dwb_ref, ybuf, y0, ct):
    pad = CONV_WIDTH // 2
    sub = 8
    span = ((CONV_HALO - pad + CONV_WIDTH - 1) // sub) * sub
    cw = 2 * LANES
    for cb in range(CONV_DIM // cw):
        sl = slice(cb * cw, (cb + 1) * cw)
        win = vpad[pl.ds(r0, ct + 2 * CONV_HALO), sl]
        acc = jnp.zeros((ct // sub, sub, cw), f32)
        for ph in range(sub):
            wph = win.astype(f32) if ph == 0 else _dot(shift_ref[ph - 1], win)
            for a in range(span // sub + 1):
                k = a * sub + ph - (CONV_HALO - pad)
                if 0 <= k < CONV_WIDTH:
                    acc = acc + wph[a * sub:a * sub + ct, :].reshape(ct // sub, sub, cw) * dw_ref[k, :, sl][None]
        ybuf[pl.ds(y0, ct), sl] = acc.reshape(ct, cw) + dwb_ref[:, sl]


def _conv_out(y, lng_ref, lnb_ref, wco_ref, gate):
    mu = jnp.mean(y, axis=-1, keepdims=True)
    yc = y - mu
    var = jnp.mean(yc * yc, axis=-1, keepdims=True)
    z = yc * lax.rsqrt(var + EPS) * lng_ref[...] + lnb_ref[...]
    z = z * jax.nn.sigmoid(z)
    return (gate.astype(f32) * _dot(z.astype(bf16), wco_ref[...])).astype(bf16)


def _fill_padded(vpad, v, n):
    zeros = jnp.zeros((CONV_HALO, CONV_DIM), vpad.dtype)
    vpad[0:CONV_HALO, :] = zeros
    vpad[CONV_HALO + n:2 * CONV_HALO + n, :] = zeros
    vpad[CONV_HALO:CONV_HALO + n, :] = v


def _inproj_kernel(*refs, rope, halo, tiles_per_seq, q_scale, n, rt, ct):
    refs = list(refs)
    x_ref = refs.pop(0)
    if halo:
        xp_ref, xn_ref = refs[:2]
        del refs[:2]
    m_ref, n1_ref, win_ref, qn_ref, wqb_ref = refs[:5]
    del refs[:5]
    if rope:
        wqbp_ref = refs.pop(0)
    kvn_ref = refs.pop(0)
    if rope:
        cos_ref, sin_ref = refs[:2]
        del refs[:2]
    shift_ref, dw_ref, dwb_ref, lng_ref, lnb_ref, wco_ref = refs[:6]
    del refs[:6]
    cm_ref, q_ref, ckv_ref, kr_ref, g_ref = refs[:5]
    del refs[:5]
    if not rope:
        kpe_ref = refs.pop(0)
    vpad, ybuf, gc = refs
    tm = x_ref.shape[0]
    lo = CONV_HALO if halo else 0

    sh1 = m_ref[0:1, :]
    sc1 = m_ref[1:2, :]
    x = jnp.concatenate([xp_ref[...], x_ref[...], xn_ref[...]], axis=0) if halo else x_ref[...]
    hb_all = (_rms(x, n1_ref[...]) * (1.0 + sc1) + sh1).astype(bf16)

    left_all = _dot(hb_all, win_ref[:, 0:C_GATE])
    v = (left_all[:, C_CONV:C_CONV + CONV_DIM] * jax.nn.sigmoid(left_all[:, C_CONV + CONV_DIM:C_QA])).astype(bf16)
    hb = hb_all[lo:lo + tm, :]
    left = left_all[lo:lo + tm, :]

    if halo:
        j = pl.program_id(0) % tiles_per_seq
        zeros = jnp.zeros((CONV_HALO, CONV_DIM), bf16)
        vpad[0, 0:lo, :] = jnp.where(j > 0, v[0:lo, :], zeros)
        vpad[0, lo:lo + tm, :] = v[lo:lo + tm, :]
        vpad[0, lo + tm:2 * lo + tm, :] = jnp.where(j < tiles_per_seq - 1, v[lo + tm:2 * lo + tm, :], zeros)
        chunks = [(0, c * ct, c * ct) for c in range(tm // ct)]
    else:
        for s in range(tm // n):
            _fill_padded(vpad.at[s], v[s * n:(s + 1) * n, :], n)
        chunks = [(s, c * ct, s * n + c * ct) for s in range(tm // n) for c in range(n // ct)]

    def conv_some(count):
        for _ in range(min(count, len(chunks))):
            s, r0, y0 = chunks.pop(0)
            _conv_taps(vpad.at[s], r0, shift_ref, dw_ref, dwb_ref, ybuf, y0, ct)

    steps = 2 + 2 * D_MODEL // GATE_CHUNK
    per_step = -(-len(chunks) // steps)

    qa = left[:, C_QA:C_KVA]
    qn = _rms(qa, qn_ref[...]).astype(bf16)
    q = _dot(qn, wqb_ref[...])
    if rope:
        qp = _dot(qn, wqbp_ref[...])
        cos = cos_ref[...]
        sin = sin_ref[...]
        for hd in range(N_HEADS):
            sl = slice(hd * HEAD_W, (hd + 1) * HEAD_W)
            q_ref[:, sl] = ((q[:, sl] * cos + qp[:, sl] * sin) * q_scale).astype(q_ref.dtype)
    else:
        q_ref[...] = (q * q_scale).astype(q_ref.dtype)
    conv_some(per_step)

    kva = left[:, C_KVA:C_KR]
    ckv_ref[...] = _rms(kva, kvn_ref[...]).astype(ckv_ref.dtype)

    kr = left[:, C_KR:C_GATE]
    if rope:
        krp = _dot(hb, win_ref[:, C_KRP:C_KRP + LANES])
        kr = kr * cos_ref[...] + krp * sin_ref[...]
    else:
        kpe_ref[...] = kr[:, ROPE_OFF:ROPE_OFF + QK_ROPE]
    kr_ref[...] = kr.astype(kr_ref.dtype)
    conv_some(per_step)

    gw = GATE_CHUNK
    for jg in range(2 * D_MODEL // gw):
        gate = jax.nn.sigmoid(_dot(hb, win_ref[:, C_GATE + jg * gw:C_GATE + (jg + 1) * gw])).astype(bf16)
        if jg * gw < D_MODEL:
            gc[:, jg * gw:(jg + 1) * gw] = gate
        else:
            g_ref[:, jg * gw - D_MODEL:(jg + 1) * gw - D_MODEL] = gate
        conv_some(per_step)

    conv_some(len(chunks))
    for r0 in range(0, tm, rt):
        cm_ref[r0:r0 + rt, :] = _conv_out(ybuf[r0:r0 + rt, :], lng_ref, lnb_ref, wco_ref, gc[r0:r0 + rt, :])


def _inproj(x2d, m3, norm1, win, q_norm, wqb, wqbp, kv_norm, cos, sin, conv_w, *, n, rope, tm,
            rt=CONV_OUT_ROWS, ct=CONV_CHUNK):
    tokens = x2d.shape[0]
    halo = n > tm
    assert n % tm == 0 if halo else (tm % n == 0 and m3.shape[0] == 1 and not rope)
    tiles_per_seq = max(1, n // tm)
    q_scale = float((QK_NOPE + QK_ROPE) ** -0.5 * LOG2E)
    tile = lambda w: pl.BlockSpec((tm, w), lambda i: (i, 0))
    in_specs = [tile(D_MODEL)]
    args = [x2d]
    if halo:
        hb_tile = tm // CONV_HALO
        hb_seq = n // CONV_HALO
        prev = lambda i: (jnp.maximum(i * hb_tile - 1, (i // tiles_per_seq) * hb_seq), 0)
        nxt = lambda i: (jnp.minimum((i + 1) * hb_tile, (i // tiles_per_seq + 1) * hb_seq - 1), 0)
        in_specs += [pl.BlockSpec((CONV_HALO, D_MODEL), prev), pl.BlockSpec((CONV_HALO, D_MODEL), nxt)]
        args += [x2d, x2d]
    in_specs += [
        _mod_spec(m3, lambda i: i // tiles_per_seq),
        _const_spec((1, D_MODEL)),
        _const_spec((D_MODEL, C_KRP + LANES if rope else C_END)),
        _const_spec((1, Q_LORA)),
        _const_spec(wqb.shape),
    ]
    args += [m3, norm1, win, q_norm, wqb]
    if rope:
        in_specs.append(_const_spec(wqbp.shape))
        args.append(wqbp)
    in_specs.append(_const_spec((1, KV_LORA)))
    args.append(kv_norm)
    if rope:
        tab = pl.BlockSpec((tm, LANES), lambda i: (i % tiles_per_seq, 0))
        in_specs += [tab, tab]
        args += [cos, sin]
    args += list(conv_w)
    in_specs += [_const_spec(w.shape) for w in conv_w]
    pieces = max(1, tm // n)
    scratch = [
        pltpu.VMEM((pieces, tm // pieces + 2 * CONV_HALO, CONV_DIM), bf16),
        pltpu.VMEM((tm, CONV_DIM), f32),
        pltpu.VMEM((tm, D_MODEL), bf16),
    ]
    out_shape = [
        jax.ShapeDtypeStruct((tokens, D_MODEL), bf16),
        jax.ShapeDtypeStruct((tokens, N_HEADS * HEAD_W), bf16),
        jax.ShapeDtypeStruct((tokens, KV_LORA), bf16 if rope else f32),
        jax.ShapeDtypeStruct((tokens, LANES), bf16),
        jax.ShapeDtypeStruct((tokens, D_MODEL), bf16),
    ]
    out_specs = [tile(D_MODEL), tile(N_HEADS * HEAD_W), tile(KV_LORA), tile(LANES), tile(D_MODEL)]
    if not rope:
        out_shape.append(jax.ShapeDtypeStruct((tokens, QK_ROPE), f32))
        out_specs.append(tile(QK_ROPE))
    return pl.pallas_call(
        functools.partial(_inproj_kernel, rope=rope, halo=halo, tiles_per_seq=tiles_per_seq, q_scale=q_scale,
                          n=n, rt=rt, ct=ct),
        grid=(tokens // tm,),
        in_specs=in_specs,
        out_specs=out_specs,
        out_shape=out_shape,
        scratch_shapes=scratch,
        compiler_params=_params(VMEM_MID, "parallel"),
        name="inproj_conv_rope" if rope else "inproj_conv",
    )(*args)


def _conv_weights(dw, dwb, lng, lnb, wco, ct):
    rows = ct + 2 * CONV_HALO
    i = np.arange(rows)
    shifts = jnp.asarray(np.stack([(i[None, :] == i[:, None] + ph) for ph in range(1, 8)]), bf16)
    dw_tiles = jnp.broadcast_to(dw[:, None, :], (CONV_WIDTH, 8, CONV_DIM))
    return shifts, dw_tiles, dwb, lng, lnb, wco


def _attn_kernel(q_ref, ckv_ref, kpe_ref, cm_ref, g_ref, x_ref, m_ref, wk_ref, wv_ref, wo_ref, wout_ref, n2_ref,
                 wr_ref, x1_ref, h2_ref, lg_ref, k_scr, v_scr, q_scr, *, sb, hg):
    s_len = ckv_ref.shape[1]

    @pl.when(pl.program_id(1) == 0)
    def _():
        ckv = ckv_ref[...].reshape(sb * s_len, KV_LORA).astype(bf16)
        kpe = kpe_ref[...].reshape(sb * s_len, LANES).astype(f32)
        k = _dot(ckv, wk_ref[...])
        v = _dot(ckv, wv_ref[...])
        ones = jnp.ones((s_len, V_DIM), bf16)
        for g in range(sb):
            rows = slice(g * s_len, (g + 1) * s_len)
            for hd in range(N_HEADS):
                u = g * N_HEADS + hd
                k_scr[u] = (k[rows, hd * HEAD_W:(hd + 1) * HEAD_W] + kpe[rows, :]).astype(bf16)
                vh = v[rows, hd * V_DIM:(hd + 1) * V_DIM].astype(bf16)
                if hd % 2 == 0:
                    v_scr[u, :, 0:V_DIM] = vh
                    v_scr[u, :, V_DIM:LANES] = ones
                else:
                    v_scr[u, :, 0:V_DIM] = ones
                    v_scr[u, :, V_DIM:LANES] = vh

    tq = q_ref.shape[0] // sb
    units = sb * N_HEADS
    for g in range(sb):
        for hd in range(N_HEADS):
            q_scr[g * N_HEADS + hd] = q_ref[g * tq:(g + 1) * tq, hd * HEAD_W:(hd + 1) * HEAD_W]
    lane = lax.broadcasted_iota(jnp.int32, (tq, LANES), 1)
    pairs = []
    for u0 in range(0, units, hg):
        us = slice(u0, u0 + hg)
        s = lax.dot_general(q_scr[us], k_scr[us], (((2,), (2,)), ((0,), (0,))), preferred_element_type=f32)
        mx = jnp.max(s, axis=-1, keepdims=True)
        p = jnp.exp2(s - mx).astype(bf16)
        r = lax.dot_general(p, v_scr[us], (((2,), (1,)), ((0,), (0,))), preferred_element_type=f32)
        for j in range(hg // 2):
            re, ro = r[2 * j], r[2 * j + 1]
            oe = re * (1.0 / re[:, V_DIM:V_DIM + 1])
            oo = ro * (1.0 / ro[:, 0:1])
            pairs.append(jnp.where(lane < V_DIM, oe, oo))
    per_seq = N_HEADS // 2
    attn = jnp.concatenate(
        [jnp.concatenate(pairs[g * per_seq:(g + 1) * per_seq], axis=-1) for g in range(sb)], axis=0).astype(bf16)
    merged = (cm_ref[...].astype(f32) + g_ref[...].astype(f32) * _dot(attn, wo_ref[...])).astype(bf16)

    g1 = m_ref[2:3, :]
    sh2 = m_ref[3:4, :]
    sc2 = m_ref[4:5, :]
    x1 = x_ref[...] + g1 * _dot(merged, wout_ref[...])
    x1_ref[...] = x1
    h2 = _rms(x1, n2_ref[...]) * (1.0 + sc2) + sh2
    hi = h2.astype(bf16)
    h2_ref[...] = hi
    lo = (h2 - hi.astype(f32)).astype(bf16)
    both = _dot(jnp.concatenate([hi, lo], axis=0), wr_ref[...])
    rows = hi.shape[0]
    s = both[0:rows] + both[rows:2 * rows]
    lg_ref[...] = s + pltpu.roll(s, LANES - N_EXPERTS, axis=1)


def _attn(q2d, ckv3, kpe3, cm2d, g2d, x2d, m3, wk, wv, wo, wout, norm2, wr, *, n, tq):
    nseq, s_len, _ = ckv3.shape
    sb = max(1, ATTN_TOKENS // n) if (tq == n and m3.shape[0] == 1) else 1
    units = sb * N_HEADS
    hg = max(2, min(units, SCORE_BYTES // (tq * s_len * 4)))
    assert units % hg == 0 and hg % 2 == 0 and nseq % sb == 0
    qb = n // tq
    tokens = nseq * n
    tile = lambda w: pl.BlockSpec((sb * tq, w), lambda s, i: (s * qb + i, 0))
    return pl.pallas_call(
        functools.partial(_attn_kernel, sb=sb, hg=hg),
        grid=(nseq // sb, qb),
        in_specs=[
            tile(N_HEADS * HEAD_W),
            pl.BlockSpec((sb, s_len, KV_LORA), lambda s, i: (s, 0, 0)),
            pl.BlockSpec((sb, s_len, LANES), lambda s, i: (s, 0, 0)),
            tile(D_MODEL),
            tile(D_MODEL),
            tile(D_MODEL),
            _mod_spec(m3, lambda s, i: s),
            _const_spec(wk.shape),
            _const_spec(wv.shape),
            _const_spec(wo.shape),
            _const_spec(wout.shape),
            _const_spec((1, D_MODEL)),
            _const_spec(wr.shape),
        ],
        out_specs=(tile(D_MODEL), tile(D_MODEL), tile(LANES)),
        out_shape=(
            jax.ShapeDtypeStruct((tokens, D_MODEL), f32),
            jax.ShapeDtypeStruct((tokens, D_MODEL), bf16),
            jax.ShapeDtypeStruct((tokens, LANES), f32),
        ),
        scratch_shapes=[
            pltpu.VMEM((units, s_len, HEAD_W), bf16),
            pltpu.VMEM((units, s_len, LANES), bf16),
            pltpu.VMEM((units, tq, HEAD_W), bf16),
        ],
        compiler_params=_params(VMEM_LIMIT, "parallel", "arbitrary"),
        name="attn_out",
    )(q2d, ckv3, kpe3, cm2d, g2d, x2d, m3, wk, wv, wo, wout, norm2, wr)


def _route_kernel(lg_ref, pos_ref, aff_ref, meta_ref, *, nseq, n, cap):
    for s in range(nseq):
        lt = lg_ref[s].T[0:N_EXPERTS, :]
        e = jnp.exp(lt - jnp.max(lt, axis=0, keepdims=True))
        aff_ref[s * N_EXPERTS:(s + 1) * N_EXPERTS, :] = e / jnp.sum(e, axis=0, keepdims=True)
    rows = nseq * N_EXPERTS
    capf = float(cap)

    def bit_step(i, t):
        cand = t | (jnp.int32(1) << (30 - i))
        thr = lax.bitcast_convert_type(cand, f32)
        cnt = jnp.sum(jnp.where(aff_ref[...] >= thr, 1.0, 0.0), axis=1, keepdims=True)
        return jnp.where(cnt >= capf, cand, t)

    t = lax.fori_loop(0, 31, bit_step, jnp.zeros((rows, 1), jnp.int32))
    thr = lax.bitcast_convert_type(t, f32)
    need = capf - jnp.sum(jnp.where(aff_ref[...] > thr, 1.0, 0.0), axis=1, keepdims=True)

    blk = MOE_BLOCK
    tri = jnp.where(
        lax.broadcasted_iota(jnp.int32, (blk, blk), 0) < lax.broadcasted_iota(jnp.int32, (blk, blk), 1),
        1.0, 0.0).astype(bf16)
    carry_gt = jnp.zeros((rows, 1), f32)
    carry_eq = jnp.zeros((rows, 1), f32)
    lane = lax.broadcasted_iota(jnp.int32, (rows, LANES), 1)
    meta = jnp.zeros((rows, LANES), f32)
    cmax = jnp.zeros((rows, 1), f32)
    for b in range(n // blk):
        sl = slice(b * blk, (b + 1) * blk)
        ab = aff_ref[:, sl]
        gt = ab > thr
        eq = ab == thr
        gtb = jnp.where(gt, 1.0, 0.0)
        eqb = jnp.where(eq, 1.0, 0.0)
        pre_gt = _dot(gtb.astype(bf16), tri) + carry_gt
        pre_eq = _dot(eqb.astype(bf16), tri) + carry_eq
        meta = jnp.where(lane == b, carry_gt + jnp.minimum(carry_eq, need), meta)
        carry_gt = carry_gt + jnp.sum(gtb, axis=1, keepdims=True)
        carry_eq = carry_eq + jnp.sum(eqb, axis=1, keepdims=True)
        sel = gt | (eq & (pre_eq < need))
        slot = pre_gt + jnp.minimum(pre_eq, need)
        pos_ref[:, sl] = jnp.where(sel, slot, -1.0).astype(jnp.int32)
        cmax = jnp.maximum(cmax, jnp.sum(jnp.where(sel, 1.0, 0.0), axis=1, keepdims=True))
    meta_ref[...] = jnp.where(lane == LANES - 1, cmax, meta).astype(jnp.int32)


def _route(lg3, *, cap):
    nseq, n, _ = lg3.shape
    rows = nseq * N_EXPERTS
    assert n // MOE_BLOCK < LANES
    return pl.pallas_call(
        functools.partial(_route_kernel, nseq=nseq, n=n, cap=cap),
        grid=(1,),
        in_specs=[_const_spec(lg3.shape)],
        out_specs=(_const_spec((rows, n)), _const_spec((rows, n)), _const_spec((rows, LANES))),
        out_shape=(jax.ShapeDtypeStruct((rows, n), jnp.int32), jax.ShapeDtypeStruct((rows, n), f32),
                   jax.ShapeDtypeStruct((rows, LANES), jnp.int32)),
        compiler_params=_params(VMEM_SMALL, "arbitrary"),
        name="route",
    )(lg3)


def _slot_hits(pos_ref, e, cap):
    width = pos_ref.shape[1]
    return lax.broadcasted_iota(jnp.int32, (cap, width), 0) == pos_ref[e:e + 1, :]


def _one_hot(hits):
    return jnp.concatenate([jnp.where(h, 1.0, 0.0).astype(bf16) for h in hits], axis=0)


def _gather_dense(pos_ref, aff_ref, h2_ref, xg_ref, vals_ref, rows, cap, ne):
    for e0 in range(0, N_EXPERTS, ne):
        hits = [_slot_hits(pos_ref, e0 + e, cap) for e in range(ne)]
        xg = _dot(_one_hot(hits), h2_ref[...]).astype(xg_ref.dtype)
        for e in range(ne):
            xg_ref[e0 + e, rows, :] = xg[e * cap:(e + 1) * cap, :]
            vals = jnp.sum(jnp.where(hits[e], aff_ref[e0 + e:e0 + e + 1, :], 0.0), axis=1, keepdims=True)
            vals_ref[e0 + e, rows, :] = jnp.broadcast_to(vals, (cap, LANES))


def _gather_kernel(pos_ref, aff_ref, h2_ref, xg_ref, vals_ref, *, cap, ne, sb):
    for g in range(sb):
        _gather_dense(pos_ref.at[g], aff_ref.at[g], h2_ref.at[g], xg_ref, vals_ref,
                      slice(g * cap, (g + 1) * cap), cap, ne)


def _gather(pos3, aff3, h23, *, cap, ne, sb):
    nseq, _, n = pos3.shape
    return pl.pallas_call(
        functools.partial(_gather_kernel, cap=cap, ne=ne, sb=sb),
        grid=(nseq // sb,),
        in_specs=[
            pl.BlockSpec((sb, N_EXPERTS, n), lambda s: (s, 0, 0)),
            pl.BlockSpec((sb, N_EXPERTS, n), lambda s: (s, 0, 0)),
            pl.BlockSpec((sb, n, D_MODEL), lambda s: (s, 0, 0)),
        ],
        out_specs=(
            pl.BlockSpec((N_EXPERTS, sb * cap, D_MODEL), lambda s: (0, s, 0)),
            pl.BlockSpec((N_EXPERTS, sb * cap, LANES), lambda s: (0, s, 0)),
        ),
        out_shape=(
            jax.ShapeDtypeStruct((N_EXPERTS, nseq * cap, D_MODEL), bf16),
            jax.ShapeDtypeStruct((N_EXPERTS, nseq * cap, LANES), f32),
        ),
        compiler_params=_params(VMEM_MID, "parallel"),
        name="gather",
    )(pos3, aff3, h23)


def _windows_fit(meta_ref):
    cmax = meta_ref[0, LANES - 1]
    for e in range(1, N_EXPERTS):
        cmax = jnp.maximum(cmax, meta_ref[e, LANES - 1])
    return cmax <= MOE_WIN - WIN_ALIGN


def _win_base(meta_ref, e, b, cap):
    start = meta_ref[e, b]
    return pl.multiple_of(jnp.minimum(start - start % WIN_ALIGN, cap - MOE_WIN), WIN_ALIGN)


def _win_hits(pos_row, base):
    slot = lax.broadcasted_iota(jnp.int32, (MOE_WIN, pos_row.shape[1]), 0) + base
    return slot == pos_row


def _gather_win_kernel(meta_ref, pos_ref, aff_ref, h2_ref, xg_ref, vals_ref, *, n, cap):
    fits = _windows_fit(meta_ref)

    @pl.when(fits)
    def _():
        xg_ref[...] = jnp.zeros(xg_ref.shape, xg_ref.dtype)
        vals_ref[...] = jnp.zeros(vals_ref.shape, vals_ref.dtype)

        def block(b, carry):
            cols = pl.ds(pl.multiple_of(b * MOE_BLOCK, MOE_BLOCK), MOE_BLOCK)
            bases = [_win_base(meta_ref, e, b, cap) for e in range(N_EXPERTS)]
            hits = [_win_hits(pos_ref[e:e + 1, cols], bases[e]) for e in range(N_EXPERTS)]
            part = _dot(_one_hot(hits), h2_ref[cols, :])
            for e in range(N_EXPERTS):
                rows = pl.ds(bases[e], MOE_WIN)
                xg_ref[e, rows, :] += part[e * MOE_WIN:(e + 1) * MOE_WIN, :].astype(xg_ref.dtype)
                vals = jnp.sum(jnp.where(hits[e], aff_ref[e:e + 1, cols], 0.0), axis=1, keepdims=True)
                vals_ref[e, rows, :] += jnp.broadcast_to(vals, (MOE_WIN, LANES))
            return carry

        lax.fori_loop(0, n // MOE_BLOCK, block, 0)

    @pl.when(jnp.logical_not(fits))
    def _():
        _gather_dense(pos_ref, aff_ref, h2_ref, xg_ref, vals_ref, slice(0, cap), cap, 1)


def _gather_win(meta, pos3, aff3, h23, *, cap):
    nseq, _, n = pos3.shape
    return pl.pallas_call(
        functools.partial(_gather_win_kernel, n=n, cap=cap),
        grid=(nseq,),
        in_specs=[
            pl.BlockSpec((N_EXPERTS, LANES), lambda s: (s, 0), memory_space=pltpu.SMEM),
            pl.BlockSpec((None, N_EXPERTS, n), lambda s: (s, 0, 0)),
            pl.BlockSpec((None, N_EXPERTS, n), lambda s: (s, 0, 0)),
            pl.BlockSpec((None, n, D_MODEL), lambda s: (s, 0, 0)),
        ],
        out_specs=(
            pl.BlockSpec((N_EXPERTS, cap, D_MODEL), lambda s: (0, s, 0)),
            pl.BlockSpec((N_EXPERTS, cap, LANES), lambda s: (0, s, 0)),
        ),
        out_shape=(
            jax.ShapeDtypeStruct((N_EXPERTS, nseq * cap, D_MODEL), bf16),
            jax.ShapeDtypeStruct((N_EXPERTS, nseq * cap, LANES), f32),
        ),
        compiler_params=_params(VMEM_LARGE, "parallel"),
        name="gather_win",
    )(meta, pos3, aff3, h23)


def _experts_kernel(xp_ref, vp_ref, xs_ref, vs_ref, wg_ref, wu_ref, wd_ref, yp_ref, ys_ref, *, rc):
    wg = wg_ref[...].astype(bf16)
    wu = wu_ref[...].astype(bf16)
    wd = wd_ref[...].astype(bf16)
    for x_ref, v_ref, y_ref in ((xp_ref, vp_ref, yp_ref), (xs_ref, vs_ref, ys_ref)):
        for r0 in range(0, x_ref.shape[0], rc):
            x = x_ref[r0:r0 + rc, :]
            a = _dot(x, wg)
            u = _dot(x, wu)
            hm = (a * jax.nn.sigmoid(a) * u).astype(bf16)
            y = _dot(hm, wd) * v_ref[r0:r0 + rc, 0:1]
            y_ref[r0:r0 + rc, :] = y.astype(y_ref.dtype)


def _experts(xg_p, vals_p, xg_s, vals_s, wg, wu, wd, *, rc=EXPERT_ROWS):
    rp = xg_p.shape[1]
    rs = xg_s.shape[1]
    per_e = lambda r, w: pl.BlockSpec((None, r, w), lambda e: (e, 0, 0))
    return pl.pallas_call(
        functools.partial(_experts_kernel, rc=rc),
        grid=(N_EXPERTS,),
        in_specs=[
            per_e(rp, D_MODEL), per_e(rp, LANES), per_e(rs, D_MODEL), per_e(rs, LANES),
            per_e(D_MODEL, EXPERT_FF), per_e(D_MODEL, EXPERT_FF), per_e(EXPERT_FF, D_MODEL),
        ],
        out_specs=(per_e(rp, D_MODEL), per_e(rs, D_MODEL)),
        out_shape=(
            jax.ShapeDtypeStruct((N_EXPERTS, rp, D_MODEL), bf16),
            jax.ShapeDtypeStruct((N_EXPERTS, rs, D_MODEL), bf16),
        ),
        compiler_params=_params(VMEM_LIMIT, "parallel"),
        name="experts",
    )(xg_p, vals_p, xg_s, vals_s, wg, wu, wd)


def _scatter_dense(pos_ref, y_ref, rows, tn, cap, ne):
    moe = jnp.zeros((tn, D_MODEL), f32)
    for e0 in range(0, N_EXPERTS, ne):
        onehot = _one_hot([_slot_hits(pos_ref, e0 + e, cap) for e in range(ne)])
        y = jnp.concatenate([y_ref[e0 + e, rows, :] for e in range(ne)], axis=0)
        moe = moe + lax.dot_general(onehot, y, (((0,), (0,)), ((), ())), preferred_element_type=f32)
    return moe


def _scatter_kernel(pos_ref, y_ref, x1_ref, m_ref, fn_ref, o_ref, *, tn, cap, ne, sb):
    g2 = m_ref[5:6, :]
    for g in range(sb):
        moe = _scatter_dense(pos_ref.at[g], y_ref, slice(g * cap, (g + 1) * cap), tn, cap, ne)
        o_ref[g] = _rms(x1_ref[g] + g2 * moe, fn_ref[...])


def _scatter(pos3, y, x13, m3, fn, *, cap, ne, tn, sb):
    nseq, _, n = pos3.shape
    assert sb == 1 or m3.shape[0] == 1, "sequences sharing a grid step must share their modulation rows"
    return pl.pallas_call(
        functools.partial(_scatter_kernel, tn=tn, cap=cap, ne=ne, sb=sb),
        grid=(nseq // sb, n // tn),
        in_specs=[
            pl.BlockSpec((sb, N_EXPERTS, tn), lambda s, i: (s, 0, i)),
            pl.BlockSpec((N_EXPERTS, sb * cap, D_MODEL), lambda s, i: (0, s, 0)),
            pl.BlockSpec((sb, tn, D_MODEL), lambda s, i: (s, i, 0)),
            _mod_spec(m3, lambda s, i: s),
            _const_spec((1, D_MODEL)),
        ],
        out_specs=pl.BlockSpec((sb, tn, D_MODEL), lambda s, i: (s, i, 0)),
        out_shape=jax.ShapeDtypeStruct((nseq, n, D_MODEL), f32),
        compiler_params=_params(VMEM_MID, "parallel", "arbitrary"),
        name="scatter",
    )(pos3, y, x13, m3, fn)


def _scatter_win_kernel(meta_ref, pos_ref, y_ref, x1_ref, m_ref, fn_ref, o_ref, moe_scr, *, cap):
    b = pl.program_id(1)
    fits = _windows_fit(meta_ref)

    @pl.when(fits)
    def _():
        bases = [_win_base(meta_ref, e, b, cap) for e in range(N_EXPERTS)]
        onehot = _one_hot([_win_hits(pos_ref[e:e + 1, :], bases[e]) for e in range(N_EXPERTS)])
        y = jnp.concatenate([y_ref[e, pl.ds(bases[e], MOE_WIN), :] for e in range(N_EXPERTS)], axis=0)
        moe_scr[...] = lax.dot_general(onehot, y, (((0,), (0,)), ((), ())), preferred_element_type=f32)

    @pl.when(jnp.logical_not(fits))
    def _():
        moe_scr[...] = _scatter_dense(pos_ref, y_ref, slice(0, cap), MOE_BLOCK, cap, 1)

    o_ref[...] = _rms(x1_ref[...] + m_ref[5:6, :] * moe_scr[...], fn_ref[...])


def _scatter_win(meta, pos3, y, x13, m3, fn, *, cap):
    nseq, _, n = pos3.shape
    tn = MOE_BLOCK
    return pl.pallas_call(
        functools.partial(_scatter_win_kernel, cap=cap),
        grid=(nseq, n // tn),
        in_specs=[
            pl.BlockSpec((N_EXPERTS, LANES), lambda s, i: (s, 0), memory_space=pltpu.SMEM),
            pl.BlockSpec((None, N_EXPERTS, tn), lambda s, i: (s, 0, i)),
            pl.BlockSpec((N_EXPERTS, cap, D_MODEL), lambda s, i: (0, s, 0)),
            pl.BlockSpec((None, tn, D_MODEL), lambda s, i: (s, i, 0)),
            _mod_spec(m3, lambda s, i: s),
            _const_spec((1, D_MODEL)),
        ],
        out_specs=pl.BlockSpec((None, tn, D_MODEL), lambda s, i: (s, i, 0)),
        out_shape=jax.ShapeDtypeStruct((nseq, n, D_MODEL), f32),
        scratch_shapes=[pltpu.VMEM((tn, D_MODEL), f32)],
        compiler_params=_params(VMEM_MID, "parallel", "arbitrary"),
        name="scatter_win",
    )(meta, pos3, y, x13, m3, fn)


def _rope_tables(n):
    t = np.arange(n)
    half = QK_ROPE // 2
    freqs = ROPE_BASE ** (-np.arange(0, half, 2, dtype=np.float64) / half)
    ang_r = (t // GRID_W)[:, None] * freqs
    ang_c = (t % GRID_W)[:, None] * freqs
    cr, sr, cc, sc = np.cos(ang_r), np.sin(ang_r), np.cos(ang_c), np.sin(ang_c)
    cos = np.ones((n, HEAD_W))
    sin = np.zeros((n, HEAD_W))
    cos[:, ROPE_OFF:ROPE_OFF + QK_ROPE] = np.concatenate([cr, cr, cc, cc], axis=-1)
    sin[:, ROPE_OFF:ROPE_OFF + QK_ROPE] = np.concatenate([-sr, sr, -sc, sc], axis=-1)
    return jnp.asarray(cos, f32), jnp.asarray(sin, f32)


_PARTNER = np.concatenate([np.arange(8, 16), np.arange(0, 8), np.arange(24, 32), np.arange(16, 24)])


def _rope_partner(w):
    q = QK_ROPE // 4
    return jnp.concatenate([w[..., q:2 * q], w[..., 0:q], w[..., 3 * q:4 * q], w[..., 2 * q:3 * q]], axis=-1)


def _rope_placement():
    place = np.zeros((LANES, 2 * LANES), np.float32)
    d = np.arange(QK_ROPE)
    place[d, ROPE_OFF + d] = 1.0
    place[_PARTNER, LANES + ROPE_OFF + d] = 1.0
    return jnp.asarray(place, bf16)


def _head_blocks(w_nope, w_rope):
    rows = w_nope.shape[0]
    if w_rope is None:
        w_rope = jnp.zeros((rows, N_HEADS, QK_ROPE), w_nope.dtype)
    z = jnp.zeros((rows, N_HEADS, HEAD_W - QK_NOPE - QK_ROPE), w_nope.dtype)
    return jnp.concatenate([w_nope, w_rope, z], axis=-1).reshape(rows, N_HEADS * HEAD_W)


def kernel(x_prompt, x_sample, cache_ckv, cache_kpe, c, c_ctx, w_ada, b_ada, norm1, w_in, conv_dw, conv_dw_b,
           conv_ln_g, conv_ln_b, w_conv_out, q_norm, w_qb, kv_norm, w_kvb, w_o_mla, w_out, norm2, w_router,
           w_e_gate, w_e_up, w_e_down, final_norm):
    assert w_ada.shape[0] == 1, "single trunk layer"
    nb_p, n_p, _ = x_prompt.shape
    nb_s, n_s, _ = x_sample.shape

    win = _wprep(w_in[0].T, _rope_placement())
    wq = w_qb[0].reshape(Q_LORA, N_HEADS, QK_NOPE + QK_ROPE)
    wqb = _head_blocks(wq[..., :QK_NOPE], wq[..., QK_NOPE:]).astype(bf16)
    wqbp = _head_blocks(jnp.zeros_like(wq[..., :QK_NOPE]), _rope_partner(wq[..., QK_NOPE:])).astype(bf16)
    wkv = w_kvb[0].reshape(KV_LORA, N_HEADS, QK_NOPE + V_DIM)
    wk = _head_blocks(wkv[..., :QK_NOPE], None).astype(bf16)
    wv = wkv[..., QK_NOPE:].reshape(KV_LORA, N_HEADS * V_DIM).astype(bf16)
    wco = w_conv_out[0].astype(bf16)
    wo = w_o_mla[0].astype(bf16)
    wout = w_out[0].astype(bf16)
    wr_hi = w_router[0].astype(bf16)
    wr_lo = (w_router[0] - wr_hi.astype(f32)).astype(bf16)
    wr = jnp.concatenate([wr_hi, wr_lo, jnp.zeros((D_MODEL, LANES - 2 * N_EXPERTS), bf16)], axis=-1)
    row = lambda a: a.reshape(1, -1)

    mod = jnp.concatenate([c_ctx[None, :], c, jnp.zeros((8 - 1 - nb_s, D_MODEL), f32)], axis=0)
    m = _ada(mod, w_ada[0], b_ada[0]).reshape(8, 6, D_MODEL)
    m_p, m_s = m[0:1], m[1:1 + nb_s]
    cos, sin = _rope_tables(n_s)

    conv_w = _conv_weights(conv_dw[0], row(conv_dw_b[0]), row(conv_ln_g[0]), row(conv_ln_b[0]), wco, CONV_CHUNK)

    def mixers(x, m3, rope, ctx_ckv, ctx_kpe):
        nseq, n, _ = x.shape
        x2d = x.reshape(nseq * n, D_MODEL)
        cm, q, ckv, kr, g, *kpe = _inproj(x2d, m3, row(norm1[0]), win, row(q_norm[0]), wqb, wqbp, row(kv_norm[0]),
                                          cos, sin, conv_w, n=n, rope=rope, tm=IN_TILE)
        keys_ckv = ckv.reshape(nseq, n, KV_LORA)
        keys_kpe = kr.reshape(nseq, n, LANES)
        if ctx_ckv is not None:
            keys_ckv = jnp.concatenate([ctx_ckv.astype(keys_ckv.dtype), keys_ckv], axis=1)
            keys_kpe = jnp.concatenate([ctx_kpe.astype(keys_kpe.dtype), keys_kpe], axis=1)
        x1, h2, lg = _attn(q, keys_ckv, keys_kpe, cm.reshape(nseq * n, D_MODEL), g, x2d, m3, wk, wv, wo, wout,
                           row(norm2[0]), wr, n=n, tq=min(n, Q_TILE))
        return x1, h2, lg, ckv, kpe

    ctx_kpe = jnp.pad(cache_kpe[:, 0], ((0, 0), (0, 0), (ROPE_OFF, LANES - ROPE_OFF - QK_ROPE)))
    x1_p, h2_p, lg_p, ckv_p, (kpe_p,) = mixers(x_prompt, m_p, False, None, None)
    x1_s, h2_s, lg_s, _, _ = mixers(x_sample, m_s, True, cache_ckv[:, 0], ctx_kpe)

    def moe_tiles(n):
        cap = EC_FACTOR * n // N_EXPERTS
        ne = N_EXPERTS if N_EXPERTS * cap <= MOE_ROWS else 1
        sb = max(1, MOE_TOKENS // n)
        return cap, ne, sb

    def windowed(n):
        return n >= 4 * MOE_BLOCK and EC_FACTOR * n // N_EXPERTS >= 2 * MOE_WIN

    def route_gather(h2, lg, nseq, n):
        cap, ne, sb = moe_tiles(n)
        pos, aff, meta = _route(lg.reshape(nseq, n, LANES), cap=cap)
        pos3 = pos.reshape(nseq, N_EXPERTS, n)
        aff3 = aff.reshape(nseq, N_EXPERTS, n)
        h23 = h2.reshape(nseq, n, D_MODEL)
        if windowed(n):
            xg, vals = _gather_win(meta, pos3, aff3, h23, cap=cap)
        else:
            xg, vals = _gather(pos3, aff3, h23, cap=cap, ne=ne, sb=sb)
        return pos3, meta, xg, vals

    pos_p, meta_p, xg_p, vals_p = route_gather(h2_p, lg_p, nb_p, n_p)
    pos_s, meta_s, xg_s, vals_s = route_gather(h2_s, lg_s, nb_s, n_s)
    y_p, y_s = _experts(xg_p, vals_p, xg_s, vals_s, w_e_gate[0], w_e_up[0], w_e_down[0])
    fn = row(final_norm)

    def scatter(pos, meta, y, x1, m3, nseq, n):
        cap, ne, sb = moe_tiles(n)
        x13 = x1.reshape(nseq, n, D_MODEL)
        if windowed(n):
            return _scatter_win(meta, pos, y, x13, m3, fn, cap=cap)
        return _scatter(pos, y, x13, m3, fn, cap=cap, ne=ne, tn=min(n, SCATTER_TILE), sb=sb)

    y_prompt = scatter(pos_p, meta_p, y_p, x1_p, m_p, nb_p, n_p)
    y_sample = scatter(pos_s, meta_s, y_s, x1_s, m_s, nb_s, n_s)

    new_ckv = ckv_p.reshape(nb_p, 1, n_p, KV_LORA)
    new_kpe = kpe_p.reshape(nb_p, 1, n_p, QK_ROPE)
    return (y_prompt, y_sample, new_ckv, new_kpe)
```

```python
import functools

import jax
import jax.numpy as jnp
import numpy as np
from jax import lax
from jax.experimental import pallas as pl
from jax.experimental.pallas import tpu as pltpu

D_MODEL = 1024
GRID_W = 64
CONV_DIM = 512
CONV_WIDTH = 31
N_HEADS = 8
QK_NOPE = 64
QK_ROPE = 32
V_DIM = 64
Q_LORA = 256
KV_LORA = 128
N_EXPERTS = 16
EXPERT_FF = 512
EC_FACTOR = 2
ROPE_BASE = 10000.0
EPS = 1e-6

LANES = 128
HEAD_W = LANES
ROPE_OFF = QK_NOPE
CONV_HALO = 16
LOG2E = 1.4426950408889634
VMEM_LIMIT = 60 * 1024 * 1024
IN_TILE = 512
CONV_CHUNK = 64
CONV_OUT_ROWS = 256
GATE_CHUNK = 512
EXPERT_ROWS = 512
Q_TILE = 512
ATTN_TOKENS = 1024
SCORE_BYTES = 12 * 1024 * 1024
MOE_ROWS = 512
MOE_TOKENS = 1024
SCATTER_TILE = 512
MOE_BLOCK = 2 * LANES
MOE_WIN = 80
WIN_ALIGN = 16

C_CONV = 0
C_QA = 2 * CONV_DIM
C_KVA = C_QA + Q_LORA
C_KR = C_KVA + KV_LORA
C_GATE = C_KR + LANES
C_END = C_GATE + 2 * D_MODEL
C_KRP = C_END

f32 = jnp.float32
bf16 = jnp.bfloat16


MIB = 1024 * 1024
VMEM_SMALL, VMEM_MID, VMEM_LARGE = 16 * MIB, 32 * MIB, 40 * MIB


def _params(vmem_bytes, *sem):
    assert vmem_bytes <= VMEM_LIMIT
    return pltpu.CompilerParams(dimension_semantics=sem, vmem_limit_bytes=vmem_bytes)


def _dot(a, b):
    return jnp.dot(a, b, preferred_element_type=f32)


def _rms(x, g):
    return x * lax.rsqrt(jnp.mean(x * x, axis=-1, keepdims=True) + EPS) * g


def _const_spec(shape):
    nd = len(shape)
    return pl.BlockSpec(shape, lambda *_: (0,) * nd)


def _mod_spec(m3, seq_of):
    if m3.shape[0] == 1:
        return _const_spec((None, 6, D_MODEL))
    return pl.BlockSpec((None, 6, D_MODEL), lambda *idx: (seq_of(*idx), 0, 0))


def _ada_kernel(s_ref, w_ref, b_ref, o_ref):
    s = s_ref[...]
    s = s * jax.nn.sigmoid(s)
    o_ref[...] = _dot(s.astype(bf16), w_ref[...].astype(bf16)) + b_ref[...]


def _ada(mod, w_ada, b_ada):
    rows = mod.shape[0]
    n_out = w_ada.shape[1]
    tn = D_MODEL
    return pl.pallas_call(
        _ada_kernel,
        grid=(n_out // tn,),
        in_specs=[
            _const_spec((rows, D_MODEL)),
            pl.BlockSpec((D_MODEL, tn), lambda j: (0, j)),
            pl.BlockSpec((1, tn), lambda j: (0, j)),
        ],
        out_specs=pl.BlockSpec((rows, tn), lambda j: (0, j)),
        out_shape=jax.ShapeDtypeStruct((rows, n_out), f32),
        compiler_params=_params(VMEM_LIMIT, "arbitrary"),
        name="ada",
    )(mod, w_ada, b_ada.reshape(1, n_out))


def _wprep_kernel(wt_ref, place_ref, o_ref):
    def block(r0):
        return wt_ref[r0:r0 + LANES, :].T.astype(bf16)

    for j in range(C_KR // LANES):
        o_ref[:, j * LANES:(j + 1) * LANES] = block(j * LANES)
    placed = _dot(block(C_KR), place_ref[...])
    o_ref[:, C_KR:C_GATE] = placed[:, 0:LANES].astype(bf16)
    o_ref[:, C_KRP:C_KRP + LANES] = placed[:, LANES:2 * LANES].astype(bf16)
    for j in range(2 * D_MODEL // LANES):
        o_ref[:, C_GATE + j * LANES:C_GATE + (j + 1) * LANES] = block(C_KR + QK_ROPE + j * LANES)


def _wprep(w_in_t, place):
    cols, rows = w_in_t.shape
    return pl.pallas_call(
        _wprep_kernel,
        grid=(rows // LANES,),
        in_specs=[pl.BlockSpec((cols, LANES), lambda i: (0, i)), _const_spec(place.shape)],
        out_specs=pl.BlockSpec((LANES, C_KRP + LANES), lambda i: (i, 0)),
        out_shape=jax.ShapeDtypeStruct((rows, C_KRP + LANES), bf16),
        compiler_params=_params(VMEM_SMALL, "parallel"),
        name="wprep",
    )(w_in_t, place)


def _conv_taps(vpad, r0, shift_ref, dw_ref, dwb_ref, ybuf, y0, ct):
    pad = CONV_WIDTH // 2
    sub = 8
    span = ((CONV_HALO - pad + CONV_WIDTH - 1) // sub) * sub
    cw = 2 * LANES
    for cb in range(CONV_DIM // cw):
        sl = slice(cb * cw, (cb + 1) * cw)
        win = vpad[pl.ds(r0, ct + 2 * CONV_HALO), sl]
        acc = jnp.zeros((ct // sub, sub, cw), f32)
        for ph in range(sub):
            wph = win.astype(f32) if ph == 0 else _dot(shift_ref[ph - 1], win)
            for a in range(span // sub + 1):
                k = a * sub + ph - (CONV_HALO - pad)
                if 0 <= k < CONV_WIDTH:
                    acc = acc + wph[a * sub:a * sub + ct, :].reshape(ct // sub, sub, cw) * dw_ref[k, :, sl][None]
        ybuf[pl.ds(y0, ct), sl] = acc.reshape(ct, cw) + dwb_ref[:, sl]


def _conv_out(y, lng_ref, lnb_ref, wco_ref, gate):
    mu = jnp.mean(y, axis=-1, keepdims=True)
    yc = y - mu
    var = jnp.mean(yc * yc, axis=-1, keepdims=True)
    z = yc * lax.rsqrt(var + EPS) * lng_ref[...] + lnb_ref[...]
    z = z * jax.nn.sigmoid(z)
    return (gate.astype(f32) * _dot(z.astype(bf16), wco_ref[...])).astype(bf16)


def _fill_padded(vpad, v, n):
    zeros = jnp.zeros((CONV_HALO, CONV_DIM), vpad.dtype)
    vpad[0:CONV_HALO, :] = zeros
    vpad[CONV_HALO + n:2 * CONV_HALO + n, :] = zeros
    vpad[CONV_HALO:CONV_HALO + n, :] = v


def _inproj_kernel(*refs, rope, halo, tiles_per_seq, q_scale, n, rt, ct):
    refs = list(refs)
    x_ref = refs.pop(0)
    if halo:
        xp_ref, xn_ref = refs[:2]
        del refs[:2]
    m_ref, n1_ref, win_ref, qn_ref, wqb_ref = refs[:5]
    del refs[:5]
    if rope:
        wqbp_ref = refs.pop(0)
    kvn_ref = refs.pop(0)
    if rope:
        cos_ref, sin_ref = refs[:2]
        del refs[:2]
    shift_ref, dw_ref, dwb_ref, lng_ref, lnb_ref, wco_ref = refs[:6]
    del refs[:6]
    cm_ref, q_ref, ckv_ref, kr_ref, g_ref = refs[:5]
    del refs[:5]
    if not rope:
        kpe_ref = refs.pop(0)
    vpad, ybuf, gc = refs
    tm = x_ref.shape[0]
    lo = CONV_HALO if halo else 0

    sh1 = m_ref[0:1, :]
    sc1 = m_ref[1:2, :]
    x = jnp.concatenate([xp_ref[...], x_ref[...], xn_ref[...]], axis=0) if halo else x_ref[...]
    hb_all = (_rms(x, n1_ref[...]) * (1.0 + sc1) + sh1).astype(bf16)

    left_all = _dot(hb_all, win_ref[:, 0:C_GATE])
    v = (left_all[:, C_CONV:C_CONV + CONV_DIM] * jax.nn.sigmoid(left_all[:, C_CONV + CONV_DIM:C_QA])).astype(bf16)
    hb = hb_all[lo:lo + tm, :]
    left = left_all[lo:lo + tm, :]

    if halo:
        j = pl.program_id(0) % tiles_per_seq
        zeros = jnp.zeros((CONV_HALO, CONV_DIM), bf16)
        vpad[0, 0:lo, :] = jnp.where(j > 0, v[0:lo, :], zeros)
        vpad[0, lo:lo + tm, :] = v[lo:lo + tm, :]
        vpad[0, lo + tm:2 * lo + tm, :] = jnp.where(j < tiles_per_seq - 1, v[lo + tm:2 * lo + tm, :], zeros)
        chunks = [(0, c * ct, c * ct) for c in range(tm // ct)]
    else:
        for s in range(tm // n):
            _fill_padded(vpad.at[s], v[s * n:(s + 1) * n, :], n)
        chunks = [(s, c * ct, s * n + c * ct) for s in range(tm // n) for c in range(n // ct)]

    def conv_some(count):
        for _ in range(min(count, len(chunks))):
            s, r0, y0 = chunks.pop(0)
            _conv_taps(vpad.at[s], r0, shift_ref, dw_ref, dwb_ref, ybuf, y0, ct)

    steps = 2 + 2 * D_MODEL // GATE_CHUNK
    per_step = -(-len(chunks) // steps)

    qa = left[:, C_QA:C_KVA]
    qn = _rms(qa, qn_ref[...]).astype(bf16)
    q = _dot(qn, wqb_ref[...])
    if rope:
        qp = _dot(qn, wqbp_ref[...])
        cos = cos_ref[...]
        sin = sin_ref[...]
        for hd in range(N_HEADS):
            sl = slice(hd * HEAD_W, (hd + 1) * HEAD_W)
            q_ref[:, sl] = ((q[:, sl] * cos + qp[:, sl] * sin) * q_scale).astype(q_ref.dtype)
    else:
        q_ref[...] = (q * q_scale).astype(q_ref.dtype)
    conv_some(per_step)

    kva = left[:, C_KVA:C_KR]
    ckv_ref[...] = _rms(kva, kvn_ref[...]).astype(ckv_ref.dtype)

    kr = left[:, C_KR:C_GATE]
    if rope:
        krp = _dot(hb, win_ref[:, C_KRP:C_KRP + LANES])
        kr = kr * cos_ref[...] + krp * sin_ref[...]
    else:
        kpe_ref[...] = kr[:, ROPE_OFF:ROPE_OFF + QK_ROPE]
    kr_ref[...] = kr.astype(kr_ref.dtype)
    conv_some(per_step)

    gw = GATE_CHUNK
    for jg in range(2 * D_MODEL // gw):
        gate = jax.nn.sigmoid(_dot(hb, win_ref[:, C_GATE + jg * gw:C_GATE + (jg + 1) * gw])).astype(bf16)
        if jg * gw < D_MODEL:
            gc[:, jg * gw:(jg + 1) * gw] = gate
        else:
            g_ref[:, jg * gw - D_MODEL:(jg + 1) * gw - D_MODEL] = gate
        conv_some(per_step)

    conv_some(len(chunks))
    for r0 in range(0, tm, rt):
        cm_ref[r0:r0 + rt, :] = _conv_out(ybuf[r0:r0 + rt, :], lng_ref, lnb_ref, wco_ref, gc[r0:r0 + rt, :])


def _inproj(x2d, m3, norm1, win, q_norm, wqb, wqbp, kv_norm, cos, sin, conv_w, *, n, rope, tm,
            rt=CONV_OUT_ROWS, ct=CONV_CHUNK):
    tokens = x2d.shape[0]
    halo = n > tm
    assert n % tm == 0 if halo else (tm % n == 0 and m3.shape[0] == 1 and not rope)
    tiles_per_seq = max(1, n // tm)
    q_scale = float((QK_NOPE + QK_ROPE) ** -0.5 * LOG2E)
    tile = lambda w: pl.BlockSpec((tm, w), lambda i: (i, 0))
    in_specs = [tile(D_MODEL)]
    args = [x2d]
    if halo:
        hb_tile = tm // CONV_HALO
        hb_seq = n // CONV_HALO
        prev = lambda i: (jnp.maximum(i * hb_tile - 1, (i // tiles_per_seq) * hb_seq), 0)
        nxt = lambda i: (jnp.minimum((i + 1) * hb_tile, (i // tiles_per_seq + 1) * hb_seq - 1), 0)
        in_specs += [pl.BlockSpec((CONV_HALO, D_MODEL), prev), pl.BlockSpec((CONV_HALO, D_MODEL), nxt)]
        args += [x2d, x2d]
    in_specs += [
        _mod_spec(m3, lambda i: i // tiles_per_seq),
        _const_spec((1, D_MODEL)),
        _const_spec((D_MODEL, C_KRP + LANES if rope else C_END)),
        _const_spec((1, Q_LORA)),
        _const_spec(wqb.shape),
    ]
    args += [m3, norm1, win, q_norm, wqb]
    if rope:
        in_specs.append(_const_spec(wqbp.shape))
        args.append(wqbp)
    in_specs.append(_const_spec((1, KV_LORA)))
    args.append(kv_norm)
    if rope:
        tab = pl.BlockSpec((tm, LANES), lambda i: (i % tiles_per_seq, 0))
        in_specs += [tab, tab]
        args += [cos, sin]
    args += list(conv_w)
    in_specs += [_const_spec(w.shape) for w in conv_w]
    pieces = max(1, tm // n)
    scratch = [
        pltpu.VMEM((pieces, tm // pieces + 2 * CONV_HALO, CONV_DIM), bf16),
        pltpu.VMEM((tm, CONV_DIM), f32),
        pltpu.VMEM((tm, D_MODEL), bf16),
    ]
    out_shape = [
        jax.ShapeDtypeStruct((tokens, D_MODEL), bf16),
        jax.ShapeDtypeStruct((tokens, N_HEADS * HEAD_W), bf16),
        jax.ShapeDtypeStruct((tokens, KV_LORA), bf16 if rope else f32),
        jax.ShapeDtypeStruct((tokens, LANES), bf16),
        jax.ShapeDtypeStruct((tokens, D_MODEL), bf16),
    ]
    out_specs = [tile(D_MODEL), tile(N_HEADS * HEAD_W), tile(KV_LORA), tile(LANES), tile(D_MODEL)]
    if not rope:
        out_shape.append(jax.ShapeDtypeStruct((tokens, QK_ROPE), f32))
        out_specs.append(tile(QK_ROPE))
    return pl.pallas_call(
        functools.partial(_inproj_kernel, rope=rope, halo=halo, tiles_per_seq=tiles_per_seq, q_scale=q_scale,
                          n=n, rt=rt, ct=ct),
        grid=(tokens // tm,),
        in_specs=in_specs,
        out_specs=out_specs,
        out_shape=out_shape,
        scratch_shapes=scratch,
        compiler_params=_params(VMEM_MID, "parallel"),
        name="inproj_conv_rope" if rope else "inproj_conv",
    )(*args)


def _conv_weights(dw, dwb, lng, lnb, wco, ct):
    rows = ct + 2 * CONV_HALO
    i = np.arange(rows)
    shifts = jnp.asarray(np.stack([(i[None, :] == i[:, None] + ph) for ph in range(1, 8)]), bf16)
    dw_tiles = jnp.broadcast_to(dw[:, None, :], (CONV_WIDTH, 8, CONV_DIM))
    return shifts, dw_tiles, dwb, lng, lnb, wco


def _attn_kernel(q_ref, ckv_ref, kpe_ref, cm_ref, g_ref, x_ref, m_ref, wk_ref, wv_ref, wo_ref, wout_ref, n2_ref,
                 wr_ref, x1_ref, h2_ref, lg_ref, k_scr, v_scr, q_scr, *, sb, hg):
    s_len = ckv_ref.shape[1]

    @pl.when(pl.program_id(1) == 0)
    def _():
        ckv = ckv_ref[...].reshape(sb * s_len, KV_LORA).astype(bf16)
        kpe = kpe_ref[...].reshape(sb * s_len, LANES).astype(f32)
        k = _dot(ckv, wk_ref[...])
        v = _dot(ckv, wv_ref[...])
        ones = jnp.ones((s_len, V_DIM), bf16)
        for g in range(sb):
            rows = slice(g * s_len, (g + 1) * s_len)
            for hd in range(N_HEADS):
                u = g * N_HEADS + hd
                k_scr[u] = (k[rows, hd * HEAD_W:(hd + 1) * HEAD_W] + kpe[rows, :]).astype(bf16)
                vh = v[rows, hd * V_DIM:(hd + 1) * V_DIM].astype(bf16)
                if hd % 2 == 0:
                    v_scr[u, :, 0:V_DIM] = vh
                    v_scr[u, :, V_DIM:LANES] = ones
                else:
                    v_scr[u, :, 0:V_DIM] = ones
                    v_scr[u, :, V_DIM:LANES] = vh

    tq = q_ref.shape[0] // sb
    units = sb * N_HEADS
    for g in range(sb):
        for hd in range(N_HEADS):
            q_scr[g * N_HEADS + hd] = q_ref[g * tq:(g + 1) * tq, hd * HEAD_W:(hd + 1) * HEAD_W]
    lane = lax.broadcasted_iota(jnp.int32, (tq, LANES), 1)
    pairs = []
    for u0 in range(0, units, hg):
        us = slice(u0, u0 + hg)
        s = lax.dot_general(q_scr[us], k_scr[us], (((2,), (2,)), ((0,), (0,))), preferred_element_type=f32)
        mx = jnp.max(s, axis=-1, keepdims=True)
        p = jnp.exp2(s - mx).astype(bf16)
        r = lax.dot_general(p, v_scr[us], (((2,), (1,)), ((0,), (0,))), preferred_element_type=f32)
        for j in range(hg // 2):
            re, ro = r[2 * j], r[2 * j + 1]
            oe = re * (1.0 / re[:, V_DIM:V_DIM + 1])
            oo = ro * (1.0 / ro[:, 0:1])
            pairs.append(jnp.where(lane < V_DIM, oe, oo))
    per_seq = N_HEADS // 2
    attn = jnp.concatenate(
        [jnp.concatenate(pairs[g * per_seq:(g + 1) * per_seq], axis=-1) for g in range(sb)], axis=0).astype(bf16)
    merged = (cm_ref[...].astype(f32) + g_ref[...].astype(f32) * _dot(attn, wo_ref[...])).astype(bf16)

    g1 = m_ref[2:3, :]
    sh2 = m_ref[3:4, :]
    sc2 = m_ref[4:5, :]
    x1 = x_ref[...] + g1 * _dot(merged, wout_ref[...])
    x1_ref[...] = x1
    h2 = _rms(x1, n2_ref[...]) * (1.0 + sc2) + sh2
    hi = h2.astype(bf16)
    h2_ref[...] = hi
    lo = (h2 - hi.astype(f32)).astype(bf16)
    both = _dot(jnp.concatenate([hi, lo], axis=0), wr_ref[...])
    rows = hi.shape[0]
    s = both[0:rows] + both[rows:2 * rows]
    lg_ref[...] = s + pltpu.roll(s, LANES - N_EXPERTS, axis=1)


def _attn(q2d, ckv3, kpe3, cm2d, g2d, x2d, m3, wk, wv, wo, wout, norm2, wr, *, n, tq):
    nseq, s_len, _ = ckv3.shape
    sb = max(1, ATTN_TOKENS // n) if (tq == n and m3.shape[0] == 1) else 1
    units = sb * N_HEADS
    hg = max(2, min(units, SCORE_BYTES // (tq * s_len * 4)))
    assert units % hg == 0 and hg % 2 == 0 and nseq % sb == 0
    qb = n // tq
    tokens = nseq * n
    tile = lambda w: pl.BlockSpec((sb * tq, w), lambda s, i: (s * qb + i, 0))
    return pl.pallas_call(
        functools.partial(_attn_kernel, sb=sb, hg=hg),
        grid=(nseq // sb, qb),
        in_specs=[
            tile(N_HEADS * HEAD_W),
            pl.BlockSpec((sb, s_len, KV_LORA), lambda s, i: (s, 0, 0)),
            pl.BlockSpec((sb, s_len, LANES), lambda s, i: (s, 0, 0)),
            tile(D_MODEL),
            tile(D_MODEL),
            tile(D_MODEL),
            _mod_spec(m3, lambda s, i: s),
            _const_spec(wk.shape),
            _const_spec(wv.shape),
            _const_spec(wo.shape),
            _const_spec(wout.shape),
            _const_spec((1, D_MODEL)),
            _const_spec(wr.shape),
        ],
        out_specs=(tile(D_MODEL), tile(D_MODEL), tile(LANES)),
        out_shape=(
            jax.ShapeDtypeStruct((tokens, D_MODEL), f32),
            jax.ShapeDtypeStruct((tokens, D_MODEL), bf16),
            jax.ShapeDtypeStruct((tokens, LANES), f32),
        ),
        scratch_shapes=[
            pltpu.VMEM((units, s_len, HEAD_W), bf16),
            pltpu.VMEM((units, s_len, LANES), bf16),
            pltpu.VMEM((units, tq, HEAD_W), bf16),
        ],
        compiler_params=_params(VMEM_LIMIT, "parallel", "arbitrary"),
        name="attn_out",
    )(q2d, ckv3, kpe3, cm2d, g2d, x2d, m3, wk, wv, wo, wout, norm2, wr)


def _route_kernel(lg_ref, pos_ref, aff_ref, meta_ref, *, nseq, n, cap):
    for s in range(nseq):
        lt = lg_ref[s].T[0:N_EXPERTS, :]
        e = jnp.exp(lt - jnp.max(lt, axis=0, keepdims=True))
        aff_ref[s * N_EXPERTS:(s + 1) * N_EXPERTS, :] = e / jnp.sum(e, axis=0, keepdims=True)
    rows = nseq * N_EXPERTS
    capf = float(cap)

    def bit_step(i, t):
        cand = t | (jnp.int32(1) << (30 - i))
        thr = lax.bitcast_convert_type(cand, f32)
        cnt = jnp.sum(jnp.where(aff_ref[...] >= thr, 1.0, 0.0), axis=1, keepdims=True)
        return jnp.where(cnt >= capf, cand, t)

    t = lax.fori_loop(0, 31, bit_step, jnp.zeros((rows, 1), jnp.int32))
    thr = lax.bitcast_convert_type(t, f32)
    need = capf - jnp.sum(jnp.where(aff_ref[...] > thr, 1.0, 0.0), axis=1, keepdims=True)

    blk = MOE_BLOCK
    tri = jnp.where(
        lax.broadcasted_iota(jnp.int32, (blk, blk), 0) < lax.broadcasted_iota(jnp.int32, (blk, blk), 1),
        1.0, 0.0).astype(bf16)
    carry_gt = jnp.zeros((rows, 1), f32)
    carry_eq = jnp.zeros((rows, 1), f32)
    lane = lax.broadcasted_iota(jnp.int32, (rows, LANES), 1)
    meta = jnp.zeros((rows, LANES), f32)
    cmax = jnp.zeros((rows, 1), f32)
    for b in range(n // blk):
        sl = slice(b * blk, (b + 1) * blk)
        ab = aff_ref[:, sl]
        gt = ab > thr
        eq = ab == thr
        gtb = jnp.where(gt, 1.0, 0.0)
        eqb = jnp.where(eq, 1.0, 0.0)
        pre_gt = _dot(gtb.astype(bf16), tri) + carry_gt
        pre_eq = _dot(eqb.astype(bf16), tri) + carry_eq
        meta = jnp.where(lane == b, carry_gt + jnp.minimum(carry_eq, need), meta)
        carry_gt = carry_gt + jnp.sum(gtb, axis=1, keepdims=True)
        carry_eq = carry_eq + jnp.sum(eqb, axis=1, keepdims=True)
        sel = gt | (eq & (pre_eq < need))
        slot = pre_gt + jnp.minimum(pre_eq, need)
        pos_ref[:, sl] = jnp.where(sel, slot, -1.0).astype(jnp.int32)
        cmax = jnp.maximum(cmax, jnp.sum(jnp.where(sel, 1.0, 0.0), axis=1, keepdims=True))
    meta_ref[...] = jnp.where(lane == LANES - 1, cmax, meta).astype(jnp.int32)


def _route(lg3, *, cap):
    nseq, n, _ = lg3.shape
    rows = nseq * N_EXPERTS
    assert n // MOE_BLOCK < LANES
    return pl.pallas_call(
        functools.partial(_route_kernel, nseq=nseq, n=n, cap=cap),
        grid=(1,),
        in_specs=[_const_spec(lg3.shape)],
        out_specs=(_const_spec((rows, n)), _const_spec((rows, n)), _const_spec((rows, LANES))),
        out_shape=(jax.ShapeDtypeStruct((rows, n), jnp.int32), jax.ShapeDtypeStruct((rows, n), f32),
                   jax.ShapeDtypeStruct((rows, LANES), jnp.int32)),
        compiler_params=_params(VMEM_SMALL, "arbitrary"),
        name="route",
    )(lg3)


def _slot_hits(pos_ref, e, cap):
    width = pos_ref.shape[1]
    return lax.broadcasted_iota(jnp.int32, (cap, width), 0) == pos_ref[e:e + 1, :]


def _one_hot(hits):
    return jnp.concatenate([jnp.where(h, 1.0, 0.0).astype(bf16) for h in hits], axis=0)


def _gather_dense(pos_ref, aff_ref, h2_ref, xg_ref, vals_ref, rows, cap, ne):
    for e0 in range(0, N_EXPERTS, ne):
        hits = [_slot_hits(pos_ref, e0 + e, cap) for e in range(ne)]
        xg = _dot(_one_hot(hits), h2_ref[...]).astype(xg_ref.dtype)
        for e in range(ne):
            xg_ref[e0 + e, rows, :] = xg[e * cap:(e + 1) * cap, :]
            vals = jnp.sum(jnp.where(hits[e], aff_ref[e0 + e:e0 + e + 1, :], 0.0), axis=1, keepdims=True)
            vals_ref[e0 + e, rows, :] = jnp.broadcast_to(vals, (cap, LANES))


def _gather_kernel(pos_ref, aff_ref, h2_ref, xg_ref, vals_ref, *, cap, ne, sb):
    for g in range(sb):
        _gather_dense(pos_ref.at[g], aff_ref.at[g], h2_ref.at[g], xg_ref, vals_ref,
                      slice(g * cap, (g + 1) * cap), cap, ne)


def _gather(pos3, aff3, h23, *, cap, ne, sb):
    nseq, _, n = pos3.shape
    return pl.pallas_call(
        functools.partial(_gather_kernel, cap=cap, ne=ne, sb=sb),
        grid=(nseq // sb,),
        in_specs=[
            pl.BlockSpec((sb, N_EXPERTS, n), lambda s: (s, 0, 0)),
            pl.BlockSpec((sb, N_EXPERTS, n), lambda s: (s, 0, 0)),
            pl.BlockSpec((sb, n, D_MODEL), lambda s: (s, 0, 0)),
        ],
        out_specs=(
            pl.BlockSpec((N_EXPERTS, sb * cap, D_MODEL), lambda s: (0, s, 0)),
            pl.BlockSpec((N_EXPERTS, sb * cap, LANES), lambda s: (0, s, 0)),
        ),
        out_shape=(
            jax.ShapeDtypeStruct((N_EXPERTS, nseq * cap, D_MODEL), bf16),
            jax.ShapeDtypeStruct((N_EXPERTS, nseq * cap, LANES), f32),
        ),
        compiler_params=_params(VMEM_MID, "parallel"),
        name="gather",
    )(pos3, aff3, h23)


def _windows_fit(meta_ref):
    cmax = meta_ref[0, LANES - 1]
    for e in range(1, N_EXPERTS):
        cmax = jnp.maximum(cmax, meta_ref[e, LANES - 1])
    return cmax <= MOE_WIN - WIN_ALIGN


def _win_base(meta_ref, e, b, cap):
    start = meta_ref[e, b]
    return pl.multiple_of(jnp.minimum(start - start % WIN_ALIGN, cap - MOE_WIN), WIN_ALIGN)


def _win_hits(pos_row, base):
    slot = lax.broadcasted_iota(jnp.int32, (MOE_WIN, pos_row.shape[1]), 0) + base
    return slot == pos_row


def _gather_win_kernel(meta_ref, pos_ref, aff_ref, h2_ref, xg_ref, vals_ref, *, n, cap):
    fits = _windows_fit(meta_ref)

    @pl.when(fits)
    def _():
        xg_ref[...] = jnp.zeros(xg_ref.shape, xg_ref.dtype)
        vals_ref[...] = jnp.zeros(vals_ref.shape, vals_ref.dtype)

        def block(b, carry):
            cols = pl.ds(pl.multiple_of(b * MOE_BLOCK, MOE_BLOCK), MOE_BLOCK)
            bases = [_win_base(meta_ref, e, b, cap) for e in range(N_EXPERTS)]
            hits = [_win_hits(pos_ref[e:e + 1, cols], bases[e]) for e in range(N_EXPERTS)]
            part = _dot(_one_hot(hits), h2_ref[cols, :])
            for e in range(N_EXPERTS):
                rows = pl.ds(bases[e], MOE_WIN)
                xg_ref[e, rows, :] += part[e * MOE_WIN:(e + 1) * MOE_WIN, :].astype(xg_ref.dtype)
                vals = jnp.sum(jnp.where(hits[e], aff_ref[e:e + 1, cols], 0.0), axis=1, keepdims=True)
                vals_ref[e, rows, :] += jnp.broadcast_to(vals, (MOE_WIN, LANES))
            return carry

        lax.fori_loop(0, n // MOE_BLOCK, block, 0)

    @pl.when(jnp.logical_not(fits))
    def _():
        _gather_dense(pos_ref, aff_ref, h2_ref, xg_ref, vals_ref, slice(0, cap), cap, 1)


def _gather_win(meta, pos3, aff3, h23, *, cap):
    nseq, _, n = pos3.shape
    return pl.pallas_call(
        functools.partial(_gather_win_kernel, n=n, cap=cap),
        grid=(nseq,),
        in_specs=[
            pl.BlockSpec((N_EXPERTS, LANES), lambda s: (s, 0), memory_space=pltpu.SMEM),
            pl.BlockSpec((None, N_EXPERTS, n), lambda s: (s, 0, 0)),
            pl.BlockSpec((None, N_EXPERTS, n), lambda s: (s, 0, 0)),
            pl.BlockSpec((None, n, D_MODEL), lambda s: (s, 0, 0)),
        ],
        out_specs=(
            pl.BlockSpec((N_EXPERTS, cap, D_MODEL), lambda s: (0, s, 0)),
            pl.BlockSpec((N_EXPERTS, cap, LANES), lambda s: (0, s, 0)),
        ),
        out_shape=(
            jax.ShapeDtypeStruct((N_EXPERTS, nseq * cap, D_MODEL), bf16),
            jax.ShapeDtypeStruct((N_EXPERTS, nseq * cap, LANES), f32),
        ),
        compiler_params=_params(VMEM_LARGE, "parallel"),
        name="gather_win",
    )(meta, pos3, aff3, h23)


def _experts_kernel(xp_ref, vp_ref, xs_ref, vs_ref, wg_ref, wu_ref, wd_ref, yp_ref, ys_ref, *, rc):
    wg = wg_ref[...].astype(bf16)
    wu = wu_ref[...].astype(bf16)
    wd = wd_ref[...].astype(bf16)
    for x_ref, v_ref, y_ref in ((xp_ref, vp_ref, yp_ref), (xs_ref, vs_ref, ys_ref)):
        for r0 in range(0, x_ref.shape[0], rc):
            x = x_ref[r0:r0 + rc, :]
            a = _dot(x, wg)
            u = _dot(x, wu)
            hm = (a * jax.nn.sigmoid(a) * u).astype(bf16)
            y = _dot(hm, wd) * v_ref[r0:r0 + rc, 0:1]
            y_ref[r0:r0 + rc, :] = y.astype(y_ref.dtype)


def _experts(xg_p, vals_p, xg_s, vals_s, wg, wu, wd, *, rc=EXPERT_ROWS):
    rp = xg_p.shape[1]
    rs = xg_s.shape[1]
    per_e = lambda r, w: pl.BlockSpec((None, r, w), lambda e: (e, 0, 0))
    return pl.pallas_call(
        functools.partial(_experts_kernel, rc=rc),
        grid=(N_EXPERTS,),
        in_specs=[
            per_e(rp, D_MODEL), per_e(rp, LANES), per_e(rs, D_MODEL), per_e(rs, LANES),
            per_e(D_MODEL, EXPERT_FF), per_e(D_MODEL, EXPERT_FF), per_e(EXPERT_FF, D_MODEL),
        ],
        out_specs=(per_e(rp, D_MODEL), per_e(rs, D_MODEL)),
        out_shape=(
            jax.ShapeDtypeStruct((N_EXPERTS, rp, D_MODEL), bf16),
            jax.ShapeDtypeStruct((N_EXPERTS, rs, D_MODEL), bf16),
        ),
        compiler_params=_params(VMEM_LIMIT, "parallel"),
        name="experts",
    )(xg_p, vals_p, xg_s, vals_s, wg, wu, wd)


def _scatter_dense(pos_ref, y_ref, rows, tn, cap, ne):
    moe = jnp.zeros((tn, D_MODEL), f32)
    for e0 in range(0, N_EXPERTS, ne):
        onehot = _one_hot([_slot_hits(pos_ref, e0 + e, cap) for e in range(ne)])
        y = jnp.concatenate([y_ref[e0 + e, rows, :] for e in range(ne)], axis=0)
        moe = moe + lax.dot_general(onehot, y, (((0,), (0,)), ((), ())), preferred_element_type=f32)
    return moe


def _scatter_kernel(pos_ref, y_ref, x1_ref, m_ref, fn_ref, o_ref, *, tn, cap, ne, sb):
    g2 = m_ref[5:6, :]
    for g in range(sb):
        moe = _scatter_dense(pos_ref.at[g], y_ref, slice(g * cap, (g + 1) * cap), tn, cap, ne)
        o_ref[g] = _rms(x1_ref[g] + g2 * moe, fn_ref[...])


def _scatter(pos3, y, x13, m3, fn, *, cap, ne, tn, sb):
    nseq, _, n = pos3.shape
    assert sb == 1 or m3.shape[0] == 1, "sequences sharing a grid step must share their modulation rows"
    return pl.pallas_call(
        functools.partial(_scatter_kernel, tn=tn, cap=cap, ne=ne, sb=sb),
        grid=(nseq // sb, n // tn),
        in_specs=[
            pl.BlockSpec((sb, N_EXPERTS, tn), lambda s, i: (s, 0, i)),
            pl.BlockSpec((N_EXPERTS, sb * cap, D_MODEL), lambda s, i: (0, s, 0)),
            pl.BlockSpec((sb, tn, D_MODEL), lambda s, i: (s, i, 0)),
            _mod_spec(m3, lambda s, i: s),
            _const_spec((1, D_MODEL)),
        ],
        out_specs=pl.BlockSpec((sb, tn, D_MODEL), lambda s, i: (s, i, 0)),
        out_shape=jax.ShapeDtypeStruct((nseq, n, D_MODEL), f32),
        compiler_params=_params(VMEM_MID, "parallel", "arbitrary"),
        name="scatter",
    )(pos3, y, x13, m3, fn)


def _scatter_win_kernel(meta_ref, pos_ref, y_ref, x1_ref, m_ref, fn_ref, o_ref, moe_scr, *, cap):
    b = pl.program_id(1)
    fits = _windows_fit(meta_ref)

    @pl.when(fits)
    def _():
        bases = [_win_base(meta_ref, e, b, cap) for e in range(N_EXPERTS)]
        onehot = _one_hot([_win_hits(pos_ref[e:e + 1, :], bases[e]) for e in range(N_EXPERTS)])
        y = jnp.concatenate([y_ref[e, pl.ds(bases[e], MOE_WIN), :] for e in range(N_EXPERTS)], axis=0)
        moe_scr[...] = lax.dot_general(onehot, y, (((0,), (0,)), ((), ())), preferred_element_type=f32)

    @pl.when(jnp.logical_not(fits))
    def _():
        moe_scr[...] = _scatter_dense(pos_ref, y_ref, slice(0, cap), MOE_BLOCK, cap, 1)

    o_ref[...] = _rms(x1_ref[...] + m_ref[5:6, :] * moe_scr[...], fn_ref[...])


def _scatter_win(meta, pos3, y, x13, m3, fn, *, cap):
    nseq, _, n = pos3.shape
    tn = MOE_BLOCK
    return pl.pallas_call(
        functools.partial(_scatter_win_kernel, cap=cap),
        grid=(nseq, n // tn),
        in_specs=[
            pl.BlockSpec((N_EXPERTS, LANES), lambda s, i: (s, 0), memory_space=pltpu.SMEM),
            pl.BlockSpec((None, N_EXPERTS, tn), lambda s, i: (s, 0, i)),
            pl.BlockSpec((N_EXPERTS, cap, D_MODEL), lambda s, i: (0, s, 0)),
            pl.BlockSpec((None, tn, D_MODEL), lambda s, i: (s, i, 0)),
            _mod_spec(m3, lambda s, i: s),
            _const_spec((1, D_MODEL)),
        ],
        out_specs=pl.BlockSpec((None, tn, D_MODEL), lambda s, i: (s, i, 0)),
        out_shape=jax.ShapeDtypeStruct((nseq, n, D_MODEL), f32),
        scratch_shapes=[pltpu.VMEM((tn, D_MODEL), f32)],
        compiler_params=_params(VMEM_MID, "parallel", "arbitrary"),
        name="scatter_win",
    )(meta, pos3, y, x13, m3, fn)


def _rope_tables(n):
    t = np.arange(n)
    half = QK_ROPE // 2
    freqs = ROPE_BASE ** (-np.arange(0, half, 2, dtype=np.float64) / half)
    ang_r = (t // GRID_W)[:, None] * freqs
    ang_c = (t % GRID_W)[:, None] * freqs
    cr, sr, cc, sc = np.cos(ang_r), np.sin(ang_r), np.cos(ang_c), np.sin(ang_c)
    cos = np.ones((n, HEAD_W))
    sin = np.zeros((n, HEAD_W))
    cos[:, ROPE_OFF:ROPE_OFF + QK_ROPE] = np.concatenate([cr, cr, cc, cc], axis=-1)
    sin[:, ROPE_OFF:ROPE_OFF + QK_ROPE] = np.concatenate([-sr, sr, -sc, sc], axis=-1)
    return jnp.asarray(cos, f32), jnp.asarray(sin, f32)


_PARTNER = np.concatenate([np.arange(8, 16), np.arange(0, 8), np.arange(24, 32), np.arange(16, 24)])


def _rope_partner(w):
    q = QK_ROPE // 4
    return jnp.concatenate([w[..., q:2 * q], w[..., 0:q], w[..., 3 * q:4 * q], w[..., 2 * q:3 * q]], axis=-1)


def _rope_placement():
    place = np.zeros((LANES, 2 * LANES), np.float32)
    d = np.arange(QK_ROPE)
    place[d, ROPE_OFF + d] = 1.0
    place[_PARTNER, LANES + ROPE_OFF + d] = 1.0
    return jnp.asarray(place, bf16)


def _head_blocks(w_nope, w_rope):
    rows = w_nope.shape[0]
    if w_rope is None:
        w_rope = jnp.zeros((rows, N_HEADS, QK_ROPE), w_nope.dtype)
    z = jnp.zeros((rows, N_HEADS, HEAD_W - QK_NOPE - QK_ROPE), w_nope.dtype)
    return jnp.concatenate([w_nope, w_rope, z], axis=-1).reshape(rows, N_HEADS * HEAD_W)


def kernel(x_prompt, x_sample, cache_ckv, cache_kpe, c, c_ctx, w_ada, b_ada, norm1, w_in, conv_dw, conv_dw_b,
           conv_ln_g, conv_ln_b, w_conv_out, q_norm, w_qb, kv_norm, w_kvb, w_o_mla, w_out, norm2, w_router,
           w_e_gate, w_e_up, w_e_down, final_norm):
    assert w_ada.shape[0] == 1, "single trunk layer"
    nb_p, n_p, _ = x_prompt.shape
    nb_s, n_s, _ = x_sample.shape

    win = _wprep(w_in[0].T, _rope_placement())
    wq = w_qb[0].reshape(Q_LORA, N_HEADS, QK_NOPE + QK_ROPE)
    wqb = _head_blocks(wq[..., :QK_NOPE], wq[..., QK_NOPE:]).astype(bf16)
    wqbp = _head_blocks(jnp.zeros_like(wq[..., :QK_NOPE]), _rope_partner(wq[..., QK_NOPE:])).astype(bf16)
    wkv = w_kvb[0].reshape(KV_LORA, N_HEADS, QK_NOPE + V_DIM)
    wk = _head_blocks(wkv[..., :QK_NOPE], None).astype(bf16)
    wv = wkv[..., QK_NOPE:].reshape(KV_LORA, N_HEADS * V_DIM).astype(bf16)
    wco = w_conv_out[0].astype(bf16)
    wo = w_o_mla[0].astype(bf16)
    wout = w_out[0].astype(bf16)
    wr_hi = w_router[0].astype(bf16)
    wr_lo = (w_router[0] - wr_hi.astype(f32)).astype(bf16)
    wr = jnp.concatenate([wr_hi, wr_lo, jnp.zeros((D_MODEL, LANES - 2 * N_EXPERTS), bf16)], axis=-1)
    row = lambda a: a.reshape(1, -1)

    mod = jnp.concatenate([c_ctx[None, :], c, jnp.zeros((8 - 1 - nb_s, D_MODEL), f32)], axis=0)
    m = _ada(mod, w_ada[0], b_ada[0]).reshape(8, 6, D_MODEL)
    m_p, m_s = m[0:1], m[1:1 + nb_s]
    cos, sin = _rope_tables(n_s)

    conv_w = _conv_weights(conv_dw[0], row(conv_dw_b[0]), row(conv_ln_g[0]), row(conv_ln_b[0]), wco, CONV_CHUNK)

    def mixers(x, m3, rope, ctx_ckv, ctx_kpe):
        nseq, n, _ = x.shape
        x2d = x.reshape(nseq * n, D_MODEL)
        cm, q, ckv, kr, g, *kpe = _inproj(x2d, m3, row(norm1[0]), win, row(q_norm[0]), wqb, wqbp, row(kv_norm[0]),
                                          cos, sin, conv_w, n=n, rope=rope, tm=IN_TILE)
        keys_ckv = ckv.reshape(nseq, n, KV_LORA)
        keys_kpe = kr.reshape(nseq, n, LANES)
        if ctx_ckv is not None:
            keys_ckv = jnp.concatenate([ctx_ckv.astype(keys_ckv.dtype), keys_ckv], axis=1)
            keys_kpe = jnp.concatenate([ctx_kpe.astype(keys_kpe.dtype), keys_kpe], axis=1)
        x1, h2, lg = _attn(q, keys_ckv, keys_kpe, cm.reshape(nseq * n, D_MODEL), g, x2d, m3, wk, wv, wo, wout,
                           row(norm2[0]), wr, n=n, tq=min(n, Q_TILE))
        return x1, h2, lg, ckv, kpe

    ctx_kpe = jnp.pad(cache_kpe[:, 0], ((0, 0), (0, 0), (ROPE_OFF, LANES - ROPE_OFF - QK_ROPE)))
    x1_p, h2_p, lg_p, ckv_p, (kpe_p,) = mixers(x_prompt, m_p, False, None, None)
    x1_s, h2_s, lg_s, _, _ = mixers(x_sample, m_s, True, cache_ckv[:, 0], ctx_kpe)

    def moe_tiles(n):
        cap = EC_FACTOR * n // N_EXPERTS
        ne = N_EXPERTS if N_EXPERTS * cap <= MOE_ROWS else 1
        sb = max(1, MOE_TOKENS // n)
        return cap, ne, sb

    def windowed(n):
        return n >= 4 * MOE_BLOCK and EC_FACTOR * n // N_EXPERTS >= 2 * MOE_WIN

    def route_gather(h2, lg, nseq, n):
        cap, ne, sb = moe_tiles(n)
        pos, aff, meta = _route(lg.reshape(nseq, n, LANES), cap=cap)
        pos3 = pos.reshape(nseq, N_EXPERTS, n)
        aff3 = aff.reshape(nseq, N_EXPERTS, n)
        h23 = h2.reshape(nseq, n, D_MODEL)
        if windowed(n):
            xg, vals = _gather_win(meta, pos3, aff3, h23, cap=cap)
        else:
            xg, vals = _gather(pos3, aff3, h23, cap=cap, ne=ne, sb=sb)
        return pos3, meta, xg, vals

    pos_p, meta_p, xg_p, vals_p = route_gather(h2_p, lg_p, nb_p, n_p)
    pos_s, meta_s, xg_s, vals_s = route_gather(h2_s, lg_s, nb_s, n_s)
    y_p, y_s = _experts(xg_p, vals_p, xg_s, vals_s, w_e_gate[0], w_e_up[0], w_e_down[0])
    fn = row(final_norm)

    def scatter(pos, meta, y, x1, m3, nseq, n):
        cap, ne, sb = moe_tiles(n)
        x13 = x1.reshape(nseq, n, D_MODEL)
        if windowed(n):
            return _scatter_win(meta, pos, y, x13, m3, fn, cap=cap)
        return _scatter(pos, y, x13, m3, fn, cap=cap, ne=ne, tn=min(n, SCATTER_TILE), sb=sb)

    y_prompt = scatter(pos_p, meta_p, y_p, x1_p, m_p, nb_p, n_p)
    y_sample = scatter(pos_s, meta_s, y_s, x1_s, m_s, nb_s, n_s)

    new_ckv = ckv_p.reshape(nb_p, 1, n_p, KV_LORA)
    new_kpe = kpe_p.reshape(nb_p, 1, n_p, QK_ROPE)
    return (y_prompt, y_sample, new_ckv, new_kpe)
```

```python
import functools

import jax
import jax.numpy as jnp
import numpy as np
from jax import lax
from jax.experimental import pallas as pl
from jax.experimental.pallas import tpu as pltpu

D_MODEL = 1024
GRID_W = 64
CONV_DIM = 512
CONV_WIDTH = 31
N_HEADS = 8
QK_NOPE = 64
QK_ROPE = 32
V_DIM = 64
Q_LORA = 256
KV_LORA = 128
N_EXPERTS = 16
EXPERT_FF = 512
EC_FACTOR = 2
ROPE_BASE = 10000.0
EPS = 1e-6

LANES = 128
HEAD_W = LANES
ROPE_OFF = QK_NOPE
CONV_HALO = 16
LOG2E = 1.4426950408889634
VMEM_LIMIT = 48 * 1024 * 1024
IN_TILE = 512
CONV_CHUNK = 64
CONV_OUT_ROWS = 256
GATE_CHUNK = 512
EXPERT_ROWS = 512
Q_TILE = 512
ATTN_TOKENS = 1024
SCORE_BYTES = 12 * 1024 * 1024
MOE_ROWS = 512
MOE_TOKENS = 1024
SCATTER_TILE = 512
MOE_BLOCK = 2 * LANES
MOE_WIN = 80
WIN_ALIGN = 16

C_CONV = 0
C_QA = 2 * CONV_DIM
C_KVA = C_QA + Q_LORA
C_KR = C_KVA + KV_LORA
C_GATE = C_KR + LANES
C_END = C_GATE + 2 * D_MODEL
C_KRP = C_END

f32 = jnp.float32
bf16 = jnp.bfloat16


MIB = 1024 * 1024
VMEM_SMALL, VMEM_MID, VMEM_LARGE = 12 * MIB, 28 * MIB, 34 * MIB


def _params(vmem_bytes, *sem):
    assert vmem_bytes <= VMEM_LIMIT
    return pltpu.CompilerParams(dimension_semantics=sem, vmem_limit_bytes=vmem_bytes)


def _dot(a, b):
    return jnp.dot(a, b, preferred_element_type=f32)


def _rms(x, g):
    return x * lax.rsqrt(jnp.mean(x * x, axis=-1, keepdims=True) + EPS) * g


def _const_spec(shape):
    nd = len(shape)
    return pl.BlockSpec(shape, lambda *_: (0,) * nd)


def _mod_spec(m3, seq_of):
    if m3.shape[0] == 1:
        return _const_spec((None, 6, D_MODEL))
    return pl.BlockSpec((None, 6, D_MODEL), lambda *idx: (seq_of(*idx), 0, 0))


def _ada_kernel(s_ref, w_ref, b_ref, o_ref):
    s = s_ref[...]
    s = s * jax.nn.sigmoid(s)
    o_ref[...] = _dot(s.astype(bf16), w_ref[...].astype(bf16)) + b_ref[...]


def _ada(mod, w_ada, b_ada):
    rows = mod.shape[0]
    n_out = w_ada.shape[1]
    tn = D_MODEL
    return pl.pallas_call(
        _ada_kernel,
        grid=(n_out // tn,),
        in_specs=[
            _const_spec((rows, D_MODEL)),
            pl.BlockSpec((D_MODEL, tn), lambda j: (0, j)),
            pl.BlockSpec((1, tn), lambda j: (0, j)),
        ],
        out_specs=pl.BlockSpec((rows, tn), lambda j: (0, j)),
        out_shape=jax.ShapeDtypeStruct((rows, n_out), f32),
        compiler_params=_params(VMEM_SMALL, "arbitrary"),
        name="ada",
    )(mod, w_ada, b_ada.reshape(1, n_out))


def _wprep_kernel(wt_ref, place_ref, o_ref):
    def block(r0):
        return wt_ref[r0:r0 + LANES, :].T.astype(bf16)

    for j in range(C_KR // LANES):
        o_ref[:, j * LANES:(j + 1) * LANES] = block(j * LANES)
    placed = _dot(block(C_KR), place_ref[...])
    o_ref[:, C_KR:C_GATE] = placed[:, 0:LANES].astype(bf16)
    o_ref[:, C_KRP:C_KRP + LANES] = placed[:, LANES:2 * LANES].astype(bf16)
    for j in range(2 * D_MODEL // LANES):
        o_ref[:, C_GATE + j * LANES:C_GATE + (j + 1) * LANES] = block(C_KR + QK_ROPE + j * LANES)


def _wprep(w_in_t, place):
    cols, rows = w_in_t.shape
    return pl.pallas_call(
        _wprep_kernel,
        grid=(rows // LANES,),
        in_specs=[pl.BlockSpec((cols, LANES), lambda i: (0, i)), _const_spec(place.shape)],
        out_specs=pl.BlockSpec((LANES, C_KRP + LANES), lambda i: (i, 0)),
        out_shape=jax.ShapeDtypeStruct((rows, C_KRP + LANES), bf16),
        compiler_params=_params(VMEM_SMALL, "parallel"),
        name="wprep",
    )(w_in_t, place)


def _conv_taps(vpad, r0, shift_ref, dw_ref, dwb_ref, ybuf, y0, ct):
    pad = CONV_WIDTH // 2
    sub = 8
    span = ((CONV_HALO - pad + CONV_WIDTH - 1) // sub) * sub
    cw = 2 * LANES
    for cb in range(CONV_DIM // cw):
        sl = slice(cb * cw, (cb + 1) * cw)
        win = vpad[pl.ds(r0, ct + 2 * CONV_HALO), sl]
        acc = jnp.zeros((ct // sub, sub, cw), f32)
        for ph in range(sub):
            wph = win.astype(f32) if ph == 0 else _dot(shift_ref[ph - 1], win)
            for a in range(span // sub + 1):
                k = a * sub + ph - (CONV_HALO - pad)
                if 0 <= k < CONV_WIDTH:
                    acc = acc + wph[a * sub:a * sub + ct, :].reshape(ct // sub, sub, cw) * dw_ref[k, :, sl][None]
        ybuf[pl.ds(y0, ct), sl] = acc.reshape(ct, cw) + dwb_ref[:, sl]


def _conv_out(y, lng_ref, lnb_ref, wco_ref, gate):
    mu = jnp.mean(y, axis=-1, keepdims=True)
    yc = y - mu
    var = jnp.mean(yc * yc, axis=-1, keepdims=True)
    z = yc * lax.rsqrt(var + EPS) * lng_ref[...] + lnb_ref[...]
    z = z * jax.nn.sigmoid(z)
    return (gate.astype(f32) * _dot(z.astype(bf16), wco_ref[...])).astype(bf16)


def _fill_padded(vpad, v, n):
    zeros = jnp.zeros((CONV_HALO, CONV_DIM), vpad.dtype)
    vpad[0:CONV_HALO, :] = zeros
    vpad[CONV_HALO + n:2 * CONV_HALO + n, :] = zeros
    vpad[CONV_HALO:CONV_HALO + n, :] = v


def _inproj_kernel(*refs, rope, halo, tiles_per_seq, q_scale, n, rt, ct):
    refs = list(refs)
    x_ref = refs.pop(0)
    if halo:
        xp_ref, xn_ref = refs[:2]
        del refs[:2]
    m_ref, n1_ref, win_ref, qn_ref, wqb_ref = refs[:5]
    del refs[:5]
    if rope:
        wqbp_ref = refs.pop(0)
    kvn_ref = refs.pop(0)
    if rope:
        cos_ref, sin_ref = refs[:2]
        del refs[:2]
    shift_ref, dw_ref, dwb_ref, lng_ref, lnb_ref, wco_ref = refs[:6]
    del refs[:6]
    cm_ref, q_ref, ckv_ref, kr_ref, g_ref = refs[:5]
    del refs[:5]
    if not rope:
        kpe_ref = refs.pop(0)
    vpad, ybuf, gc = refs
    tm = x_ref.shape[0]
    lo = CONV_HALO if halo else 0

    sh1 = m_ref[0:1, :]
    sc1 = m_ref[1:2, :]
    x = jnp.concatenate([xp_ref[...], x_ref[...], xn_ref[...]], axis=0) if halo else x_ref[...]
    hb_all = (_rms(x, n1_ref[...]) * (1.0 + sc1) + sh1).astype(bf16)

    left_all = _dot(hb_all, win_ref[:, 0:C_GATE])
    v = (left_all[:, C_CONV:C_CONV + CONV_DIM] * jax.nn.sigmoid(left_all[:, C_CONV + CONV_DIM:C_QA])).astype(bf16)
    hb = hb_all[lo:lo + tm, :]
    left = left_all[lo:lo + tm, :]

    if halo:
        j = pl.program_id(0) % tiles_per_seq
        zeros = jnp.zeros((CONV_HALO, CONV_DIM), bf16)
        vpad[0, 0:lo, :] = jnp.where(j > 0, v[0:lo, :], zeros)
        vpad[0, lo:lo + tm, :] = v[lo:lo + tm, :]
        vpad[0, lo + tm:2 * lo + tm, :] = jnp.where(j < tiles_per_seq - 1, v[lo + tm:2 * lo + tm, :], zeros)
        chunks = [(0, c * ct, c * ct) for c in range(tm // ct)]
    else:
        for s in range(tm // n):
            _fill_padded(vpad.at[s], v[s * n:(s + 1) * n, :], n)
        chunks = [(s, c * ct, s * n + c * ct) for s in range(tm // n) for c in range(n // ct)]

    def conv_some(count):
        for _ in range(min(count, len(chunks))):
            s, r0, y0 = chunks.pop(0)
            _conv_taps(vpad.at[s], r0, shift_ref, dw_ref, dwb_ref, ybuf, y0, ct)

    steps = 2 + 2 * D_MODEL // GATE_CHUNK
    per_step = -(-len(chunks) // steps)

    qa = left[:, C_QA:C_KVA]
    qn = _rms(qa, qn_ref[...]).astype(bf16)
    q = _dot(qn, wqb_ref[...])
    if rope:
        qp = _dot(qn, wqbp_ref[...])
        cos = cos_ref[...]
        sin = sin_ref[...]
        for hd in range(N_HEADS):
            sl = slice(hd * HEAD_W, (hd + 1) * HEAD_W)
            q_ref[:, sl] = ((q[:, sl] * cos + qp[:, sl] * sin) * q_scale).astype(q_ref.dtype)
    else:
        q_ref[...] = (q * q_scale).astype(q_ref.dtype)
    conv_some(per_step)

    kva = left[:, C_KVA:C_KR]
    ckv_ref[...] = _rms(kva, kvn_ref[...]).astype(ckv_ref.dtype)

    kr = left[:, C_KR:C_GATE]
    if rope:
        krp = _dot(hb, win_ref[:, C_KRP:C_KRP + LANES])
        kr = kr * cos_ref[...] + krp * sin_ref[...]
    else:
        kpe_ref[...] = kr[:, ROPE_OFF:ROPE_OFF + QK_ROPE]
    kr_ref[...] = kr.astype(kr_ref.dtype)
    conv_some(per_step)

    gw = GATE_CHUNK
    for jg in range(2 * D_MODEL // gw):
        gate = jax.nn.sigmoid(_dot(hb, win_ref[:, C_GATE + jg * gw:C_GATE + (jg + 1) * gw])).astype(bf16)
        if jg * gw < D_MODEL:
            gc[:, jg * gw:(jg + 1) * gw] = gate
        else:
            g_ref[:, jg * gw - D_MODEL:(jg + 1) * gw - D_MODEL] = gate
        conv_some(per_step)

    conv_some(len(chunks))
    for r0 in range(0, tm, rt):
        cm_ref[r0:r0 + rt, :] = _conv_out(ybuf[r0:r0 + rt, :], lng_ref, lnb_ref, wco_ref, gc[r0:r0 + rt, :])


def _inproj(x2d, m3, norm1, win, q_norm, wqb, wqbp, kv_norm, cos, sin, conv_w, *, n, rope, tm,
            rt=CONV_OUT_ROWS, ct=CONV_CHUNK):
    tokens = x2d.shape[0]
    halo = n > tm
    assert n % tm == 0 if halo else (tm % n == 0 and m3.shape[0] == 1 and not rope)
    tiles_per_seq = max(1, n // tm)
    q_scale = float((QK_NOPE + QK_ROPE) ** -0.5 * LOG2E)
    tile = lambda w: pl.BlockSpec((tm, w), lambda i: (i, 0))
    in_specs = [tile(D_MODEL)]
    args = [x2d]
    if halo:
        hb_tile = tm // CONV_HALO
        hb_seq = n // CONV_HALO
        prev = lambda i: (jnp.maximum(i * hb_tile - 1, (i // tiles_per_seq) * hb_seq), 0)
        nxt = lambda i: (jnp.minimum((i + 1) * hb_tile, (i // tiles_per_seq + 1) * hb_seq - 1), 0)
        in_specs += [pl.BlockSpec((CONV_HALO, D_MODEL), prev), pl.BlockSpec((CONV_HALO, D_MODEL), nxt)]
        args += [x2d, x2d]
    in_specs += [
        _mod_spec(m3, lambda i: i // tiles_per_seq),
        _const_spec((1, D_MODEL)),
        _const_spec((D_MODEL, C_KRP + LANES if rope else C_END)),
        _const_spec((1, Q_LORA)),
        _const_spec(wqb.shape),
    ]
    args += [m3, norm1, win, q_norm, wqb]
    if rope:
        in_specs.append(_const_spec(wqbp.shape))
        args.append(wqbp)
    in_specs.append(_const_spec((1, KV_LORA)))
    args.append(kv_norm)
    if rope:
        tab = pl.BlockSpec((tm, LANES), lambda i: (i % tiles_per_seq, 0))
        in_specs += [tab, tab]
        args += [cos, sin]
    args += list(conv_w)
    in_specs += [_const_spec(w.shape) for w in conv_w]
    pieces = max(1, tm // n)
    scratch = [
        pltpu.VMEM((pieces, tm // pieces + 2 * CONV_HALO, CONV_DIM), bf16),
        pltpu.VMEM((tm, CONV_DIM), f32),
        pltpu.VMEM((tm, D_MODEL), bf16),
    ]
    out_shape = [
        jax.ShapeDtypeStruct((tokens, D_MODEL), bf16),
        jax.ShapeDtypeStruct((tokens, N_HEADS * HEAD_W), bf16),
        jax.ShapeDtypeStruct((tokens, KV_LORA), bf16 if rope else f32),
        jax.ShapeDtypeStruct((tokens, LANES), bf16),
        jax.ShapeDtypeStruct((tokens, D_MODEL), bf16),
    ]
    out_specs = [tile(D_MODEL), tile(N_HEADS * HEAD_W), tile(KV_LORA), tile(LANES), tile(D_MODEL)]
    if not rope:
        out_shape.append(jax.ShapeDtypeStruct((tokens, QK_ROPE), f32))
        out_specs.append(tile(QK_ROPE))
    return pl.pallas_call(
        functools.partial(_inproj_kernel, rope=rope, halo=halo, tiles_per_seq=tiles_per_seq, q_scale=q_scale,
                          n=n, rt=rt, ct=ct),
        grid=(tokens // tm,),
        in_specs=in_specs,
        out_specs=out_specs,
        out_shape=out_shape,
        scratch_shapes=scratch,
        compiler_params=_params(VMEM_MID, "parallel"),
        name="inproj_conv_rope" if rope else "inproj_conv",
    )(*args)


def _conv_weights(dw, dwb, lng, lnb, wco, ct):
    rows = ct + 2 * CONV_HALO
    i = np.arange(rows)
    shifts = jnp.asarray(np.stack([(i[None, :] == i[:, None] + ph) for ph in range(1, 8)]), bf16)
    dw_tiles = jnp.broadcast_to(dw[:, None, :], (CONV_WIDTH, 8, CONV_DIM))
    return shifts, dw_tiles, dwb, lng, lnb, wco


def _attn_kernel(q_ref, ckv_ref, kpe_ref, cm_ref, g_ref, x_ref, m_ref, wk_ref, wv_ref, wo_ref, wout_ref, n2_ref,
                 wr_ref, x1_ref, h2_ref, lg_ref, k_scr, v_scr, q_scr, *, sb, hg):
    s_len = ckv_ref.shape[1]

    @pl.when(pl.program_id(1) == 0)
    def _():
        ckv = ckv_ref[...].reshape(sb * s_len, KV_LORA).astype(bf16)
        kpe = kpe_ref[...].reshape(sb * s_len, LANES).astype(f32)
        k = _dot(ckv, wk_ref[...])
        v = _dot(ckv, wv_ref[...])
        ones = jnp.ones((s_len, V_DIM), bf16)
        for g in range(sb):
            rows = slice(g * s_len, (g + 1) * s_len)
            for hd in range(N_HEADS):
                u = g * N_HEADS + hd
                k_scr[u] = (k[rows, hd * HEAD_W:(hd + 1) * HEAD_W] + kpe[rows, :]).astype(bf16)
                vh = v[rows, hd * V_DIM:(hd + 1) * V_DIM].astype(bf16)
                if hd % 2 == 0:
                    v_scr[u, :, 0:V_DIM] = vh
                    v_scr[u, :, V_DIM:LANES] = ones
                else:
                    v_scr[u, :, 0:V_DIM] = ones
                    v_scr[u, :, V_DIM:LANES] = vh

    tq = q_ref.shape[0] // sb
    units = sb * N_HEADS
    for g in range(sb):
        for hd in range(N_HEADS):
            q_scr[g * N_HEADS + hd] = q_ref[g * tq:(g + 1) * tq, hd * HEAD_W:(hd + 1) * HEAD_W]
    lane = lax.broadcasted_iota(jnp.int32, (tq, LANES), 1)
    pairs = []
    for u0 in range(0, units, hg):
        us = slice(u0, u0 + hg)
        s = lax.dot_general(q_scr[us], k_scr[us], (((2,), (2,)), ((0,), (0,))), preferred_element_type=f32)
        mx = jnp.max(s, axis=-1, keepdims=True)
        p = jnp.exp2(s - mx).astype(bf16)
        r = lax.dot_general(p, v_scr[us], (((2,), (1,)), ((0,), (0,))), preferred_element_type=f32)
        for j in range(hg // 2):
            re, ro = r[2 * j], r[2 * j + 1]
            oe = re * (1.0 / re[:, V_DIM:V_DIM + 1])
            oo = ro * (1.0 / ro[:, 0:1])
            pairs.append(jnp.where(lane < V_DIM, oe, oo))
    per_seq = N_HEADS // 2
    attn = jnp.concatenate(
        [jnp.concatenate(pairs[g * per_seq:(g + 1) * per_seq], axis=-1) for g in range(sb)], axis=0).astype(bf16)
    merged = (cm_ref[...].astype(f32) + g_ref[...].astype(f32) * _dot(attn, wo_ref[...])).astype(bf16)

    g1 = m_ref[2:3, :]
    sh2 = m_ref[3:4, :]
    sc2 = m_ref[4:5, :]
    x1 = x_ref[...] + g1 * _dot(merged, wout_ref[...])
    x1_ref[...] = x1
    h2 = _rms(x1, n2_ref[...]) * (1.0 + sc2) + sh2
    hi = h2.astype(bf16)
    h2_ref[...] = hi
    lo = (h2 - hi.astype(f32)).astype(bf16)
    both = _dot(jnp.concatenate([hi, lo], axis=0), wr_ref[...])
    rows = hi.shape[0]
    s = both[0:rows] + both[rows:2 * rows]
    lg_ref[...] = s + pltpu.roll(s, LANES - N_EXPERTS, axis=1)


def _attn(q2d, ckv3, kpe3, cm2d, g2d, x2d, m3, wk, wv, wo, wout, norm2, wr, *, n, tq):
    nseq, s_len, _ = ckv3.shape
    sb = max(1, ATTN_TOKENS // n) if (tq == n and m3.shape[0] == 1) else 1
    units = sb * N_HEADS
    hg = max(2, min(units, SCORE_BYTES // (tq * s_len * 4)))
    assert units % hg == 0 and hg % 2 == 0 and nseq % sb == 0
    qb = n // tq
    tokens = nseq * n
    tile = lambda w: pl.BlockSpec((sb * tq, w), lambda s, i: (s * qb + i, 0))
    return pl.pallas_call(
        functools.partial(_attn_kernel, sb=sb, hg=hg),
        grid=(nseq // sb, qb),
        in_specs=[
            tile(N_HEADS * HEAD_W),
            pl.BlockSpec((sb, s_len, KV_LORA), lambda s, i: (s, 0, 0)),
            pl.BlockSpec((sb, s_len, LANES), lambda s, i: (s, 0, 0)),
            tile(D_MODEL),
            tile(D_MODEL),
            tile(D_MODEL),
            _mod_spec(m3, lambda s, i: s),
            _const_spec(wk.shape),
            _const_spec(wv.shape),
            _const_spec(wo.shape),
            _const_spec(wout.shape),
            _const_spec((1, D_MODEL)),
            _const_spec(wr.shape),
        ],
        out_specs=(tile(D_MODEL), tile(D_MODEL), tile(LANES)),
        out_shape=(
            jax.ShapeDtypeStruct((tokens, D_MODEL), f32),
            jax.ShapeDtypeStruct((tokens, D_MODEL), bf16),
            jax.ShapeDtypeStruct((tokens, LANES), f32),
        ),
        scratch_shapes=[
            pltpu.VMEM((units, s_len, HEAD_W), bf16),
            pltpu.VMEM((units, s_len, LANES), bf16),
            pltpu.VMEM((units, tq, HEAD_W), bf16),
        ],
        compiler_params=_params(VMEM_LIMIT, "parallel", "arbitrary"),
        name="attn_out",
    )(q2d, ckv3, kpe3, cm2d, g2d, x2d, m3, wk, wv, wo, wout, norm2, wr)


def _route_kernel(lg_ref, pos_ref, aff_ref, meta_ref, *, nseq, n, cap):
    for s in range(nseq):
        lt = lg_ref[s].T[0:N_EXPERTS, :]
        e = jnp.exp(lt - jnp.max(lt, axis=0, keepdims=True))
        aff_ref[s * N_EXPERTS:(s + 1) * N_EXPERTS, :] = e / jnp.sum(e, axis=0, keepdims=True)
    rows = nseq * N_EXPERTS
    capf = float(cap)

    def bit_step(i, t):
        cand = t | (jnp.int32(1) << (30 - i))
        thr = lax.bitcast_convert_type(cand, f32)
        cnt = jnp.sum(jnp.where(aff_ref[...] >= thr, 1.0, 0.0), axis=1, keepdims=True)
        return jnp.where(cnt >= capf, cand, t)

    t = lax.fori_loop(0, 31, bit_step, jnp.zeros((rows, 1), jnp.int32))
    thr = lax.bitcast_convert_type(t, f32)
    need = capf - jnp.sum(jnp.where(aff_ref[...] > thr, 1.0, 0.0), axis=1, keepdims=True)

    blk = MOE_BLOCK
    tri = jnp.where(
        lax.broadcasted_iota(jnp.int32, (blk, blk), 0) < lax.broadcasted_iota(jnp.int32, (blk, blk), 1),
        1.0, 0.0).astype(bf16)
    carry_gt = jnp.zeros((rows, 1), f32)
    carry_eq = jnp.zeros((rows, 1), f32)
    lane = lax.broadcasted_iota(jnp.int32, (rows, LANES), 1)
    meta = jnp.zeros((rows, LANES), f32)
    cmax = jnp.zeros((rows, 1), f32)
    for b in range(n // blk):
        sl = slice(b * blk, (b + 1) * blk)
        ab = aff_ref[:, sl]
        gt = ab > thr
        eq = ab == thr
        gtb = jnp.where(gt, 1.0, 0.0)
        eqb = jnp.where(eq, 1.0, 0.0)
        pre_gt = _dot(gtb.astype(bf16), tri) + carry_gt
        pre_eq = _dot(eqb.astype(bf16), tri) + carry_eq
        meta = jnp.where(lane == b, carry_gt + jnp.minimum(carry_eq, need), meta)
        carry_gt = carry_gt + jnp.sum(gtb, axis=1, keepdims=True)
        carry_eq = carry_eq + jnp.sum(eqb, axis=1, keepdims=True)
        sel = gt | (eq & (pre_eq < need))
        slot = pre_gt + jnp.minimum(pre_eq, need)
        pos_ref[:, sl] = jnp.where(sel, slot, -1.0).astype(jnp.int32)
        cmax = jnp.maximum(cmax, jnp.sum(jnp.where(sel, 1.0, 0.0), axis=1, keepdims=True))
    meta_ref[...] = jnp.where(lane == LANES - 1, cmax, meta).astype(jnp.int32)


def _route(lg3, *, cap):
    nseq, n, _ = lg3.shape
    rows = nseq * N_EXPERTS
    assert n // MOE_BLOCK < LANES
    return pl.pallas_call(
        functools.partial(_route_kernel, nseq=nseq, n=n, cap=cap),
        grid=(1,),
        in_specs=[_const_spec(lg3.shape)],
        out_specs=(_const_spec((rows, n)), _const_spec((rows, n)), _const_spec((rows, LANES))),
        out_shape=(jax.ShapeDtypeStruct((rows, n), jnp.int32), jax.ShapeDtypeStruct((rows, n), f32),
                   jax.ShapeDtypeStruct((rows, LANES), jnp.int32)),
        compiler_params=_params(VMEM_SMALL, "arbitrary"),
        name="route",
    )(lg3)


def _slot_hits(pos_ref, e, cap):
    width = pos_ref.shape[1]
    return lax.broadcasted_iota(jnp.int32, (cap, width), 0) == pos_ref[e:e + 1, :]


def _one_hot(hits):
    return jnp.concatenate([jnp.where(h, 1.0, 0.0).astype(bf16) for h in hits], axis=0)


def _gather_dense(pos_ref, aff_ref, h2_ref, xg_ref, vals_ref, rows, cap, ne):
    for e0 in range(0, N_EXPERTS, ne):
        hits = [_slot_hits(pos_ref, e0 + e, cap) for e in range(ne)]
        xg = _dot(_one_hot(hits), h2_ref[...]).astype(xg_ref.dtype)
        for e in range(ne):
            xg_ref[e0 + e, rows, :] = xg[e * cap:(e + 1) * cap, :]
            vals = jnp.sum(jnp.where(hits[e], aff_ref[e0 + e:e0 + e + 1, :], 0.0), axis=1, keepdims=True)
            vals_ref[e0 + e, rows, :] = jnp.broadcast_to(vals, (cap, LANES))


def _gather_kernel(pos_ref, aff_ref, h2_ref, xg_ref, vals_ref, *, cap, ne, sb):
    for g in range(sb):
        _gather_dense(pos_ref.at[g], aff_ref.at[g], h2_ref.at[g], xg_ref, vals_ref,
                      slice(g * cap, (g + 1) * cap), cap, ne)


def _gather(pos3, aff3, h23, *, cap, ne, sb):
    nseq, _, n = pos3.shape
    return pl.pallas_call(
        functools.partial(_gather_kernel, cap=cap, ne=ne, sb=sb),
        grid=(nseq // sb,),
        in_specs=[
            pl.BlockSpec((sb, N_EXPERTS, n), lambda s: (s, 0, 0)),
            pl.BlockSpec((sb, N_EXPERTS, n), lambda s: (s, 0, 0)),
            pl.BlockSpec((sb, n, D_MODEL), lambda s: (s, 0, 0)),
        ],
        out_specs=(
            pl.BlockSpec((N_EXPERTS, sb * cap, D_MODEL), lambda s: (0, s, 0)),
            pl.BlockSpec((N_EXPERTS, sb * cap, LANES), lambda s: (0, s, 0)),
        ),
        out_shape=(
            jax.ShapeDtypeStruct((N_EXPERTS, nseq * cap, D_MODEL), bf16),
            jax.ShapeDtypeStruct((N_EXPERTS, nseq * cap, LANES), f32),
        ),
        compiler_params=_params(VMEM_MID, "parallel"),
        name="gather",
    )(pos3, aff3, h23)


def _windows_fit(meta_ref):
    cmax = meta_ref[0, LANES - 1]
    for e in range(1, N_EXPERTS):
        cmax = jnp.maximum(cmax, meta_ref[e, LANES - 1])
    return cmax <= MOE_WIN - WIN_ALIGN


def _win_base(meta_ref, e, b, cap):
    start = meta_ref[e, b]
    return pl.multiple_of(jnp.minimum(start - start % WIN_ALIGN, cap - MOE_WIN), WIN_ALIGN)


def _win_hits(pos_row, base):
    slot = lax.broadcasted_iota(jnp.int32, (MOE_WIN, pos_row.shape[1]), 0) + base
    return slot == pos_row


def _gather_win_kernel(meta_ref, pos_ref, aff_ref, h2_ref, xg_ref, vals_ref, *, n, cap):
    fits = _windows_fit(meta_ref)

    @pl.when(fits)
    def _():
        xg_ref[...] = jnp.zeros(xg_ref.shape, xg_ref.dtype)
        vals_ref[...] = jnp.zeros(vals_ref.shape, vals_ref.dtype)

        def block(b, carry):
            cols = pl.ds(pl.multiple_of(b * MOE_BLOCK, MOE_BLOCK), MOE_BLOCK)
            bases = [_win_base(meta_ref, e, b, cap) for e in range(N_EXPERTS)]
            hits = [_win_hits(pos_ref[e:e + 1, cols], bases[e]) for e in range(N_EXPERTS)]
            part = _dot(_one_hot(hits), h2_ref[cols, :])
            for e in range(N_EXPERTS):
                rows = pl.ds(bases[e], MOE_WIN)
                xg_ref[e, rows, :] += part[e * MOE_WIN:(e + 1) * MOE_WIN, :].astype(xg_ref.dtype)
                vals = jnp.sum(jnp.where(hits[e], aff_ref[e:e + 1, cols], 0.0), axis=1, keepdims=True)
                vals_ref[e, rows, :] += jnp.broadcast_to(vals, (MOE_WIN, LANES))
            return carry

        lax.fori_loop(0, n // MOE_BLOCK, block, 0)

    @pl.when(jnp.logical_not(fits))
    def _():
        _gather_dense(pos_ref, aff_ref, h2_ref, xg_ref, vals_ref, slice(0, cap), cap, 1)


def _gather_win(meta, pos3, aff3, h23, *, cap):
    nseq, _, n = pos3.shape
    return pl.pallas_call(
        functools.partial(_gather_win_kernel, n=n, cap=cap),
        grid=(nseq,),
        in_specs=[
            pl.BlockSpec((N_EXPERTS, LANES), lambda s: (s, 0), memory_space=pltpu.SMEM),
            pl.BlockSpec((None, N_EXPERTS, n), lambda s: (s, 0, 0)),
            pl.BlockSpec((None, N_EXPERTS, n), lambda s: (s, 0, 0)),
            pl.BlockSpec((None, n, D_MODEL), lambda s: (s, 0, 0)),
        ],
        out_specs=(
            pl.BlockSpec((N_EXPERTS, cap, D_MODEL), lambda s: (0, s, 0)),
            pl.BlockSpec((N_EXPERTS, cap, LANES), lambda s: (0, s, 0)),
        ),
        out_shape=(
            jax.ShapeDtypeStruct((N_EXPERTS, nseq * cap, D_MODEL), bf16),
            jax.ShapeDtypeStruct((N_EXPERTS, nseq * cap, LANES), f32),
        ),
        compiler_params=_params(VMEM_LARGE, "parallel"),
        name="gather_win",
    )(meta, pos3, aff3, h23)


def _experts_kernel(xp_ref, vp_ref, xs_ref, vs_ref, wg_ref, wu_ref, wd_ref, yp_ref, ys_ref, *, rc):
    wg = wg_ref[...].astype(bf16)
    wu = wu_ref[...].astype(bf16)
    wd = wd_ref[...].astype(bf16)
    for x_ref, v_ref, y_ref in ((xp_ref, vp_ref, yp_ref), (xs_ref, vs_ref, ys_ref)):
        for r0 in range(0, x_ref.shape[0], rc):
            x = x_ref[r0:r0 + rc, :]
            a = _dot(x, wg)
            u = _dot(x, wu)
            hm = (a * jax.nn.sigmoid(a) * u).astype(bf16)
            y = _dot(hm, wd) * v_ref[r0:r0 + rc, 0:1]
            y_ref[r0:r0 + rc, :] = y.astype(y_ref.dtype)


def _experts(xg_p, vals_p, xg_s, vals_s, wg, wu, wd, *, rc=EXPERT_ROWS):
    rp = xg_p.shape[1]
    rs = xg_s.shape[1]
    per_e = lambda r, w: pl.BlockSpec((None, r, w), lambda e: (e, 0, 0))
    return pl.pallas_call(
        functools.partial(_experts_kernel, rc=rc),
        grid=(N_EXPERTS,),
        in_specs=[
            per_e(rp, D_MODEL), per_e(rp, LANES), per_e(rs, D_MODEL), per_e(rs, LANES),
            per_e(D_MODEL, EXPERT_FF), per_e(D_MODEL, EXPERT_FF), per_e(EXPERT_FF, D_MODEL),
        ],
        out_specs=(per_e(rp, D_MODEL), per_e(rs, D_MODEL)),
        out_shape=(
            jax.ShapeDtypeStruct((N_EXPERTS, rp, D_MODEL), bf16),
            jax.ShapeDtypeStruct((N_EXPERTS, rs, D_MODEL), bf16),
        ),
        compiler_params=_params(VMEM_LARGE, "parallel"),
        name="experts",
    )(xg_p, vals_p, xg_s, vals_s, wg, wu, wd)


def _scatter_dense(pos_ref, y_ref, rows, tn, cap, ne):
    moe = jnp.zeros((tn, D_MODEL), f32)
    for e0 in range(0, N_EXPERTS, ne):
        onehot = _one_hot([_slot_hits(pos_ref, e0 + e, cap) for e in range(ne)])
        y = jnp.concatenate([y_ref[e0 + e, rows, :] for e in range(ne)], axis=0)
        moe = moe + lax.dot_general(onehot, y, (((0,), (0,)), ((), ())), preferred_element_type=f32)
    return moe


def _scatter_kernel(pos_ref, y_ref, x1_ref, m_ref, fn_ref, o_ref, *, tn, cap, ne, sb):
    g2 = m_ref[5:6, :]
    for g in range(sb):
        moe = _scatter_dense(pos_ref.at[g], y_ref, slice(g * cap, (g + 1) * cap), tn, cap, ne)
        o_ref[g] = _rms(x1_ref[g] + g2 * moe, fn_ref[...])


def _scatter(pos3, y, x13, m3, fn, *, cap, ne, tn, sb):
    nseq, _, n = pos3.shape
    assert sb == 1 or m3.shape[0] == 1, "sequences sharing a grid step must share their modulation rows"
    return pl.pallas_call(
        functools.partial(_scatter_kernel, tn=tn, cap=cap, ne=ne, sb=sb),
        grid=(nseq // sb, n // tn),
        in_specs=[
            pl.BlockSpec((sb, N_EXPERTS, tn), lambda s, i: (s, 0, i)),
            pl.BlockSpec((N_EXPERTS, sb * cap, D_MODEL), lambda s, i: (0, s, 0)),
            pl.BlockSpec((sb, tn, D_MODEL), lambda s, i: (s, i, 0)),
            _mod_spec(m3, lambda s, i: s),
            _const_spec((1, D_MODEL)),
        ],
        out_specs=pl.BlockSpec((sb, tn, D_MODEL), lambda s, i: (s, i, 0)),
        out_shape=jax.ShapeDtypeStruct((nseq, n, D_MODEL), f32),
        compiler_params=_params(VMEM_MID, "parallel", "arbitrary"),
        name="scatter",
    )(pos3, y, x13, m3, fn)


def _scatter_win_kernel(meta_ref, pos_ref, y_ref, x1_ref, m_ref, fn_ref, o_ref, moe_scr, *, cap):
    b = pl.program_id(1)
    fits = _windows_fit(meta_ref)

    @pl.when(fits)
    def _():
        bases = [_win_base(meta_ref, e, b, cap) for e in range(N_EXPERTS)]
        onehot = _one_hot([_win_hits(pos_ref[e:e + 1, :], bases[e]) for e in range(N_EXPERTS)])
        y = jnp.concatenate([y_ref[e, pl.ds(bases[e], MOE_WIN), :] for e in range(N_EXPERTS)], axis=0)
        moe_scr[...] = lax.dot_general(onehot, y, (((0,), (0,)), ((), ())), preferred_element_type=f32)

    @pl.when(jnp.logical_not(fits))
    def _():
        moe_scr[...] = _scatter_dense(pos_ref, y_ref, slice(0, cap), MOE_BLOCK, cap, 1)

    o_ref[...] = _rms(x1_ref[...] + m_ref[5:6, :] * moe_scr[...], fn_ref[...])


def _scatter_win(meta, pos3, y, x13, m3, fn, *, cap):
    nseq, _, n = pos3.shape
    tn = MOE_BLOCK
    return pl.pallas_call(
        functools.partial(_scatter_win_kernel, cap=cap),
        grid=(nseq, n // tn),
        in_specs=[
            pl.BlockSpec((N_EXPERTS, LANES), lambda s, i: (s, 0), memory_space=pltpu.SMEM),
            pl.BlockSpec((None, N_EXPERTS, tn), lambda s, i: (s, 0, i)),
            pl.BlockSpec((N_EXPERTS, cap, D_MODEL), lambda s, i: (0, s, 0)),
            pl.BlockSpec((None, tn, D_MODEL), lambda s, i: (s, i, 0)),
            _mod_spec(m3, lambda s, i: s),
            _const_spec((1, D_MODEL)),
        ],
        out_specs=pl.BlockSpec((None, tn, D_MODEL), lambda s, i: (s, i, 0)),
        out_shape=jax.ShapeDtypeStruct((nseq, n, D_MODEL), f32),
        scratch_shapes=[pltpu.VMEM((tn, D_MODEL), f32)],
        compiler_params=_params(VMEM_MID, "parallel", "arbitrary"),
        name="scatter_win",
    )(meta, pos3, y, x13, m3, fn)


def _rope_tables(n):
    t = np.arange(n)
    half = QK_ROPE // 2
    freqs = ROPE_BASE ** (-np.arange(0, half, 2, dtype=np.float64) / half)
    ang_r = (t // GRID_W)[:, None] * freqs
    ang_c = (t % GRID_W)[:, None] * freqs
    cr, sr, cc, sc = np.cos(ang_r), np.sin(ang_r), np.cos(ang_c), np.sin(ang_c)
    cos = np.ones((n, HEAD_W))
    sin = np.zeros((n, HEAD_W))
    cos[:, ROPE_OFF:ROPE_OFF + QK_ROPE] = np.concatenate([cr, cr, cc, cc], axis=-1)
    sin[:, ROPE_OFF:ROPE_OFF + QK_ROPE] = np.concatenate([-sr, sr, -sc, sc], axis=-1)
    return jnp.asarray(cos, f32), jnp.asarray(sin, f32)


_PARTNER = np.concatenate([np.arange(8, 16), np.arange(0, 8), np.arange(24, 32), np.arange(16, 24)])


def _rope_partner(w):
    q = QK_ROPE // 4
    return jnp.concatenate([w[..., q:2 * q], w[..., 0:q], w[..., 3 * q:4 * q], w[..., 2 * q:3 * q]], axis=-1)


def _rope_placement():
    place = np.zeros((LANES, 2 * LANES), np.float32)
    d = np.arange(QK_ROPE)
    place[d, ROPE_OFF + d] = 1.0
    place[_PARTNER, LANES + ROPE_OFF + d] = 1.0
    return jnp.asarray(place, bf16)


def _head_blocks(w_nope, w_rope):
    rows = w_nope.shape[0]
    if w_rope is None:
        w_rope = jnp.zeros((rows, N_HEADS, QK_ROPE), w_nope.dtype)
    z = jnp.zeros((rows, N_HEADS, HEAD_W - QK_NOPE - QK_ROPE), w_nope.dtype)
    return jnp.concatenate([w_nope, w_rope, z], axis=-1).reshape(rows, N_HEADS * HEAD_W)


def kernel(x_prompt, x_sample, cache_ckv, cache_kpe, c, c_ctx, w_ada, b_ada, norm1, w_in, conv_dw, conv_dw_b,
           conv_ln_g, conv_ln_b, w_conv_out, q_norm, w_qb, kv_norm, w_kvb, w_o_mla, w_out, norm2, w_router,
           w_e_gate, w_e_up, w_e_down, final_norm):
    assert w_ada.shape[0] == 1, "single trunk layer"
    nb_p, n_p, _ = x_prompt.shape
    nb_s, n_s, _ = x_sample.shape

    win = _wprep(w_in[0].T, _rope_placement())
    wq = w_qb[0].reshape(Q_LORA, N_HEADS, QK_NOPE + QK_ROPE)
    wqb = _head_blocks(wq[..., :QK_NOPE], wq[..., QK_NOPE:]).astype(bf16)
    wqbp = _head_blocks(jnp.zeros_like(wq[..., :QK_NOPE]), _rope_partner(wq[..., QK_NOPE:])).astype(bf16)
    wkv = w_kvb[0].reshape(KV_LORA, N_HEADS, QK_NOPE + V_DIM)
    wk = _head_blocks(wkv[..., :QK_NOPE], None).astype(bf16)
    wv = wkv[..., QK_NOPE:].reshape(KV_LORA, N_HEADS * V_DIM).astype(bf16)
    wco = w_conv_out[0].astype(bf16)
    wo = w_o_mla[0].astype(bf16)
    wout = w_out[0].astype(bf16)
    wr_hi = w_router[0].astype(bf16)
    wr_lo = (w_router[0] - wr_hi.astype(f32)).astype(bf16)
    wr = jnp.concatenate([wr_hi, wr_lo, jnp.zeros((D_MODEL, LANES - 2 * N_EXPERTS), bf16)], axis=-1)
    row = lambda a: a.reshape(1, -1)

    mod = jnp.concatenate([c_ctx[None, :], c, jnp.zeros((8 - 1 - nb_s, D_MODEL), f32)], axis=0)
    m = _ada(mod, w_ada[0], b_ada[0]).reshape(8, 6, D_MODEL)
    m_p, m_s = m[0:1], m[1:1 + nb_s]
    cos, sin = _rope_tables(n_s)

    conv_w = _conv_weights(conv_dw[0], row(conv_dw_b[0]), row(conv_ln_g[0]), row(conv_ln_b[0]), wco, CONV_CHUNK)

    def mixers(x, m3, rope, ctx_ckv, ctx_kpe):
        nseq, n, _ = x.shape
        x2d = x.reshape(nseq * n, D_MODEL)
        cm, q, ckv, kr, g, *kpe = _inproj(x2d, m3, row(norm1[0]), win, row(q_norm[0]), wqb, wqbp, row(kv_norm[0]),
                                          cos, sin, conv_w, n=n, rope=rope, tm=IN_TILE)
        keys_ckv = ckv.reshape(nseq, n, KV_LORA)
        keys_kpe = kr.reshape(nseq, n, LANES)
        if ctx_ckv is not None:
            keys_ckv = jnp.concatenate([ctx_ckv.astype(keys_ckv.dtype), keys_ckv], axis=1)
            keys_kpe = jnp.concatenate([ctx_kpe.astype(keys_kpe.dtype), keys_kpe], axis=1)
        x1, h2, lg = _attn(q, keys_ckv, keys_kpe, cm.reshape(nseq * n, D_MODEL), g, x2d, m3, wk, wv, wo, wout,
                           row(norm2[0]), wr, n=n, tq=min(n, Q_TILE))
        return x1, h2, lg, ckv, kpe

    ctx_kpe = jnp.pad(cache_kpe[:, 0], ((0, 0), (0, 0), (ROPE_OFF, LANES - ROPE_OFF - QK_ROPE)))
    x1_p, h2_p, lg_p, ckv_p, (kpe_p,) = mixers(x_prompt, m_p, False, None, None)
    x1_s, h2_s, lg_s, _, _ = mixers(x_sample, m_s, True, cache_ckv[:, 0], ctx_kpe)

    def moe_tiles(n):
        cap = EC_FACTOR * n // N_EXPERTS
        ne = N_EXPERTS if N_EXPERTS * cap <= MOE_ROWS else 1
        sb = max(1, MOE_TOKENS // n)
        return cap, ne, sb

    def windowed(n):
        return n >= 4 * MOE_BLOCK and EC_FACTOR * n // N_EXPERTS >= 2 * MOE_WIN

    def route_gather(h2, lg, nseq, n):
        cap, ne, sb = moe_tiles(n)
        pos, aff, meta = _route(lg.reshape(nseq, n, LANES), cap=cap)
        pos3 = pos.reshape(nseq, N_EXPERTS, n)
        aff3 = aff.reshape(nseq, N_EXPERTS, n)
        h23 = h2.reshape(nseq, n, D_MODEL)
        if windowed(n):
            xg, vals = _gather_win(meta, pos3, aff3, h23, cap=cap)
        else:
            xg, vals = _gather(pos3, aff3, h23, cap=cap, ne=ne, sb=sb)
        return pos3, meta, xg, vals

    pos_p, meta_p, xg_p, vals_p = route_gather(h2_p, lg_p, nb_p, n_p)
    pos_s, meta_s, xg_s, vals_s = route_gather(h2_s, lg_s, nb_s, n_s)
    y_p, y_s = _experts(xg_p, vals_p, xg_s, vals_s, w_e_gate[0], w_e_up[0], w_e_down[0])
    fn = row(final_norm)

    def scatter(pos, meta, y, x1, m3, nseq, n):
        cap, ne, sb = moe_tiles(n)
        x13 = x1.reshape(nseq, n, D_MODEL)
        if windowed(n):
            return _scatter_win(meta, pos, y, x13, m3, fn, cap=cap)
        return _scatter(pos, y, x13, m3, fn, cap=cap, ne=ne, tn=min(n, SCATTER_TILE), sb=sb)

    y_prompt = scatter(pos_p, meta_p, y_p, x1_p, m_p, nb_p, n_p)
    y_sample = scatter(pos_s, meta_s, y_s, x1_s, m_s, nb_s, n_s)

    new_ckv = ckv_p.reshape(nb_p, 1, n_p, KV_LORA)
    new_kpe = kpe_p.reshape(nb_p, 1, n_p, QK_ROPE)
    return (y_prompt, y_sample, new_ckv, new_kpe)
```

```python
import functools

import jax
import jax.numpy as jnp
import numpy as np
from jax import lax
from jax.experimental import pallas as pl
from jax.experimental.pallas import tpu as pltpu

D_MODEL = 1024
GRID_W = 64
CONV_DIM = 512
CONV_WIDTH = 31
N_HEADS = 8
QK_NOPE = 64
QK_ROPE = 32
V_DIM = 64
Q_LORA = 256
KV_LORA = 128
N_EXPERTS = 16
EXPERT_FF = 512
EC_FACTOR = 2
ROPE_BASE = 10000.0
EPS = 1e-6

LANES = 128
HEAD_W = LANES
ROPE_OFF = QK_NOPE
CONV_HALO = 16
LOG2E = 1.4426950408889634
VMEM_LIMIT = 48 * 1024 * 1024
IN_TILE = 512
CONV_CHUNK = 64
CONV_OUT_ROWS = 256
GATE_CHUNK = 512
EXPERT_ROWS = 512
Q_TILE = 512
ATTN_TOKENS = 1024
SCORE_BYTES = 12 * 1024 * 1024
MOE_ROWS = 512
MOE_TOKENS = 1024
SCATTER_TILE = 512
MOE_BLOCK = 2 * LANES
MOE_WIN = 80
WIN_ALIGN = 16

C_CONV = 0
C_QA = 2 * CONV_DIM
C_KVA = C_QA + Q_LORA
C_KR = C_KVA + KV_LORA
C_GATE = C_KR + LANES
C_END = C_GATE + 2 * D_MODEL
C_KRP = C_END

f32 = jnp.float32
bf16 = jnp.bfloat16


MIB = 1024 * 1024
VMEM_SMALL, VMEM_MID, VMEM_LARGE = 16 * MIB, 32 * MIB, 40 * MIB


def _params(vmem_bytes, *sem):
    assert vmem_bytes <= VMEM_LIMIT
    return pltpu.CompilerParams(dimension_semantics=sem, vmem_limit_bytes=vmem_bytes)


def _dot(a, b):
    return jnp.dot(a, b, preferred_element_type=f32)


def _rms(x, g):
    return x * lax.rsqrt(jnp.mean(x * x, axis=-1, keepdims=True) + EPS) * g


def _const_spec(shape):
    nd = len(shape)
    return pl.BlockSpec(shape, lambda *_: (0,) * nd)


def _mod_spec(m3, seq_of):
    if m3.shape[0] == 1:
        return _const_spec((None, 6, D_MODEL))
    return pl.BlockSpec((None, 6, D_MODEL), lambda *idx: (seq_of(*idx), 0, 0))


def _ada_kernel(s_ref, w_ref, b_ref, o_ref):
    s = s_ref[...]
    s = s * jax.nn.sigmoid(s)
    o_ref[...] = _dot(s.astype(bf16), w_ref[...].astype(bf16)) + b_ref[...]


def _ada(mod, w_ada, b_ada):
    rows = mod.shape[0]
    n_out = w_ada.shape[1]
    tn = D_MODEL
    return pl.pallas_call(
        _ada_kernel,
        grid=(n_out // tn,),
        in_specs=[
            _const_spec((rows, D_MODEL)),
            pl.BlockSpec((D_MODEL, tn), lambda j: (0, j)),
            pl.BlockSpec((1, tn), lambda j: (0, j)),
        ],
        out_specs=pl.BlockSpec((rows, tn), lambda j: (0, j)),
        out_shape=jax.ShapeDtypeStruct((rows, n_out), f32),
        compiler_params=_params(VMEM_SMALL, "arbitrary"),
        name="ada",
    )(mod, w_ada, b_ada.reshape(1, n_out))


def _wprep_kernel(wt_ref, place_ref, o_ref):
    def block(r0):
        return wt_ref[r0:r0 + LANES, :].T.astype(bf16)

    for j in range(C_KR // LANES):
        o_ref[:, j * LANES:(j + 1) * LANES] = block(j * LANES)
    placed = _dot(block(C_KR), place_ref[...])
    o_ref[:, C_KR:C_GATE] = placed[:, 0:LANES].astype(bf16)
    o_ref[:, C_KRP:C_KRP + LANES] = placed[:, LANES:2 * LANES].astype(bf16)
    for j in range(2 * D_MODEL // LANES):
        o_ref[:, C_GATE + j * LANES:C_GATE + (j + 1) * LANES] = block(C_KR + QK_ROPE + j * LANES)


def _wprep(w_in_t, place):
    cols, rows = w_in_t.shape
    return pl.pallas_call(
        _wprep_kernel,
        grid=(rows // LANES,),
        in_specs=[pl.BlockSpec((cols, LANES), lambda i: (0, i)), _const_spec(place.shape)],
        out_specs=pl.BlockSpec((LANES, C_KRP + LANES), lambda i: (i, 0)),
        out_shape=jax.ShapeDtypeStruct((rows, C_KRP + LANES), bf16),
        compiler_params=_params(VMEM_SMALL, "parallel"),
        name="wprep",
    )(w_in_t, place)


def _conv_taps(vpad, r0, shift_ref, dw_ref, dwb_ref, ybuf, y0, ct):
    pad = CONV_WIDTH // 2
    sub = 8
    span = ((CONV_HALO - pad + CONV_WIDTH - 1) // sub) * sub
    cw = 2 * LANES
    for cb in range(CONV_DIM // cw):
        sl = slice(cb * cw, (cb + 1) * cw)
        win = vpad[pl.ds(r0, ct + 2 * CONV_HALO), sl]
        acc = jnp.zeros((ct // sub, sub, cw), f32)
        for ph in range(sub):
            wph = win.astype(f32) if ph == 0 else _dot(shift_ref[ph - 1], win)
            for a in range(span // sub + 1):
                k = a * sub + ph - (CONV_HALO - pad)
                if 0 <= k < CONV_WIDTH:
                    acc = acc + wph[a * sub:a * sub + ct, :].reshape(ct // sub, sub, cw) * dw_ref[k, :, sl][None]
        ybuf[pl.ds(y0, ct), sl] = acc.reshape(ct, cw) + dwb_ref[:, sl]


def _conv_out(y, lng_ref, lnb_ref, wco_ref, gate):
    mu = jnp.mean(y, axis=-1, keepdims=True)
    yc = y - mu
    var = jnp.mean(yc * yc, axis=-1, keepdims=True)
    z = yc * lax.rsqrt(var + EPS) * lng_ref[...] + lnb_ref[...]
    z = z * jax.nn.sigmoid(z)
    return (gate.astype(f32) * _dot(z.astype(bf16), wco_ref[...])).astype(bf16)


def _fill_padded(vpad, v, n):
    zeros = jnp.zeros((CONV_HALO, CONV_DIM), vpad.dtype)
    vpad[0:CONV_HALO, :] = zeros
    vpad[CONV_HALO + n:2 * CONV_HALO + n, :] = zeros
    vpad[CONV_HALO:CONV_HALO + n, :] = v


def _inproj_kernel(*refs, rope, halo, tiles_per_seq, q_scale, n, rt, ct):
    refs = list(refs)
    x_ref = refs.pop(0)
    if halo:
        xp_ref, xn_ref = refs[:2]
        del refs[:2]
    m_ref, n1_ref, win_ref, qn_ref, wqb_ref = refs[:5]
    del refs[:5]
    if rope:
        wqbp_ref = refs.pop(0)
    kvn_ref = refs.pop(0)
    if rope:
        cos_ref, sin_ref = refs[:2]
        del refs[:2]
    shift_ref, dw_ref, dwb_ref, lng_ref, lnb_ref, wco_ref = refs[:6]
    del refs[:6]
    cm_ref, q_ref, ckv_ref, kr_ref, g_ref = refs[:5]
    del refs[:5]
    if not rope:
        kpe_ref = refs.pop(0)
    vpad, ybuf, gc = refs
    tm = x_ref.shape[0]
    lo = CONV_HALO if halo else 0

    sh1 = m_ref[0:1, :]
    sc1 = m_ref[1:2, :]
    x = jnp.concatenate([xp_ref[...], x_ref[...], xn_ref[...]], axis=0) if halo else x_ref[...]
    hb_all = (_rms(x, n1_ref[...]) * (1.0 + sc1) + sh1).astype(bf16)

    left_all = _dot(hb_all, win_ref[:, 0:C_GATE])
    v = (left_all[:, C_CONV:C_CONV + CONV_DIM] * jax.nn.sigmoid(left_all[:, C_CONV + CONV_DIM:C_QA])).astype(bf16)
    hb = hb_all[lo:lo + tm, :]
    left = left_all[lo:lo + tm, :]

    if halo:
        j = pl.program_id(0) % tiles_per_seq
        zeros = jnp.zeros((CONV_HALO, CONV_DIM), bf16)
        vpad[0, 0:lo, :] = jnp.where(j > 0, v[0:lo, :], zeros)
        vpad[0, lo:lo + tm, :] = v[lo:lo + tm, :]
        vpad[0, lo + tm:2 * lo + tm, :] = jnp.where(j < tiles_per_seq - 1, v[lo + tm:2 * lo + tm, :], zeros)
        chunks = [(0, c * ct, c * ct) for c in range(tm // ct)]
    else:
        for s in range(tm // n):
            _fill_padded(vpad.at[s], v[s * n:(s + 1) * n, :], n)
        chunks = [(s, c * ct, s * n + c * ct) for s in range(tm // n) for c in range(n // ct)]

    def conv_some(count):
        for _ in range(min(count, len(chunks))):
            s, r0, y0 = chunks.pop(0)
            _conv_taps(vpad.at[s], r0, shift_ref, dw_ref, dwb_ref, ybuf, y0, ct)

    steps = 2 + 2 * D_MODEL // GATE_CHUNK
    per_step = -(-len(chunks) // steps)

    qa = left[:, C_QA:C_KVA]
    qn = _rms(qa, qn_ref[...]).astype(bf16)
    q = _dot(qn, wqb_ref[...])
    if rope:
        qp = _dot(qn, wqbp_ref[...])
        cos = cos_ref[...]
        sin = sin_ref[...]
        for hd in range(N_HEADS):
            sl = slice(hd * HEAD_W, (hd + 1) * HEAD_W)
            q_ref[:, sl] = ((q[:, sl] * cos + qp[:, sl] * sin) * q_scale).astype(q_ref.dtype)
    else:
        q_ref[...] = (q * q_scale).astype(q_ref.dtype)
    conv_some(per_step)

    kva = left[:, C_KVA:C_KR]
    ckv_ref[...] = _rms(kva, kvn_ref[...]).astype(ckv_ref.dtype)

    kr = left[:, C_KR:C_GATE]
    if rope:
        krp = _dot(hb, win_ref[:, C_KRP:C_KRP + LANES])
        kr = kr * cos_ref[...] + krp * sin_ref[...]
    else:
        kpe_ref[...] = kr[:, ROPE_OFF:ROPE_OFF + QK_ROPE]
    kr_ref[...] = kr.astype(kr_ref.dtype)
    conv_some(per_step)

    gw = GATE_CHUNK
    for jg in range(2 * D_MODEL // gw):
        gate = jax.nn.sigmoid(_dot(hb, win_ref[:, C_GATE + jg * gw:C_GATE + (jg + 1) * gw])).astype(bf16)
        if jg * gw < D_MODEL:
            gc[:, jg * gw:(jg + 1) * gw] = gate
        else:
            g_ref[:, jg * gw - D_MODEL:(jg + 1) * gw - D_MODEL] = gate
        conv_some(per_step)

    conv_some(len(chunks))
    for r0 in range(0, tm, rt):
        cm_ref[r0:r0 + rt, :] = _conv_out(ybuf[r0:r0 + rt, :], lng_ref, lnb_ref, wco_ref, gc[r0:r0 + rt, :])


def _inproj(x2d, m3, norm1, win, q_norm, wqb, wqbp, kv_norm, cos, sin, conv_w, *, n, rope, tm,
            rt=CONV_OUT_ROWS, ct=CONV_CHUNK):
    tokens = x2d.shape[0]
    halo = n > tm
    assert n % tm == 0 if halo else (tm % n == 0 and m3.shape[0] == 1 and not rope)
    tiles_per_seq = max(1, n // tm)
    q_scale = float((QK_NOPE + QK_ROPE) ** -0.5 * LOG2E)
    tile = lambda w: pl.BlockSpec((tm, w), lambda i: (i, 0))
    in_specs = [tile(D_MODEL)]
    args = [x2d]
    if halo:
        hb_tile = tm // CONV_HALO
        hb_seq = n // CONV_HALO
        prev = lambda i: (jnp.maximum(i * hb_tile - 1, (i // tiles_per_seq) * hb_seq), 0)
        nxt = lambda i: (jnp.minimum((i + 1) * hb_tile, (i // tiles_per_seq + 1) * hb_seq - 1), 0)
        in_specs += [pl.BlockSpec((CONV_HALO, D_MODEL), prev), pl.BlockSpec((CONV_HALO, D_MODEL), nxt)]
        args += [x2d, x2d]
    in_specs += [
        _mod_spec(m3, lambda i: i // tiles_per_seq),
        _const_spec((1, D_MODEL)),
        _const_spec((D_MODEL, C_KRP + LANES if rope else C_END)),
        _const_spec((1, Q_LORA)),
        _const_spec(wqb.shape),
    ]
    args += [m3, norm1, win, q_norm, wqb]
    if rope:
        in_specs.append(_const_spec(wqbp.shape))
        args.append(wqbp)
    in_specs.append(_const_spec((1, KV_LORA)))
    args.append(kv_norm)
    if rope:
        tab = pl.BlockSpec((tm, LANES), lambda i: (i % tiles_per_seq, 0))
        in_specs += [tab, tab]
        args += [cos, sin]
    args += list(conv_w)
    in_specs += [_const_spec(w.shape) for w in conv_w]
    pieces = max(1, tm // n)
    scratch = [
        pltpu.VMEM((pieces, tm // pieces + 2 * CONV_HALO, CONV_DIM), bf16),
        pltpu.VMEM((tm, CONV_DIM), f32),
        pltpu.VMEM((tm, D_MODEL), bf16),
    ]
    out_shape = [
        jax.ShapeDtypeStruct((tokens, D_MODEL), bf16),
        jax.ShapeDtypeStruct((tokens, N_HEADS * HEAD_W), bf16),
        jax.ShapeDtypeStruct((tokens, KV_LORA), bf16 if rope else f32),
        jax.ShapeDtypeStruct((tokens, LANES), bf16),
        jax.ShapeDtypeStruct((tokens, D_MODEL), bf16),
    ]
    out_specs = [tile(D_MODEL), tile(N_HEADS * HEAD_W), tile(KV_LORA), tile(LANES), tile(D_MODEL)]
    if not rope:
        out_shape.append(jax.ShapeDtypeStruct((tokens, QK_ROPE), f32))
        out_specs.append(tile(QK_ROPE))
    return pl.pallas_call(
        functools.partial(_inproj_kernel, rope=rope, halo=halo, tiles_per_seq=tiles_per_seq, q_scale=q_scale,
                          n=n, rt=rt, ct=ct),
        grid=(tokens // tm,),
        in_specs=in_specs,
        out_specs=out_specs,
        out_shape=out_shape,
        scratch_shapes=scratch,
        compiler_params=_params(VMEM_MID, "parallel"),
        name="inproj_conv_rope" if rope else "inproj_conv",
    )(*args)


def _conv_weights(dw, dwb, lng, lnb, wco, ct):
    rows = ct + 2 * CONV_HALO
    i = np.arange(rows)
    shifts = jnp.asarray(np.stack([(i[None, :] == i[:, None] + ph) for ph in range(1, 8)]), bf16)
    dw_tiles = jnp.broadcast_to(dw[:, None, :], (CONV_WIDTH, 8, CONV_DIM))
    return shifts, dw_tiles, dwb, lng, lnb, wco


def _attn_kernel(q_ref, ckv_ref, kpe_ref, cm_ref, g_ref, x_ref, m_ref, wk_ref, wv_ref, wo_ref, wout_ref, n2_ref,
                 wr_ref, x1_ref, h2_ref, lg_ref, k_scr, v_scr, q_scr, *, sb, hg):
    s_len = ckv_ref.shape[1]

    @pl.when(pl.program_id(1) == 0)
    def _():
        ckv = ckv_ref[...].reshape(sb * s_len, KV_LORA).astype(bf16)
        kpe = kpe_ref[...].reshape(sb * s_len, LANES).astype(f32)
        k = _dot(ckv, wk_ref[...])
        v = _dot(ckv, wv_ref[...])
        ones = jnp.ones((s_len, V_DIM), bf16)
        for g in range(sb):
            rows = slice(g * s_len, (g + 1) * s_len)
            for hd in range(N_HEADS):
                u = g * N_HEADS + hd
                k_scr[u] = (k[rows, hd * HEAD_W:(hd + 1) * HEAD_W] + kpe[rows, :]).astype(bf16)
                vh = v[rows, hd * V_DIM:(hd + 1) * V_DIM].astype(bf16)
                if hd % 2 == 0:
                    v_scr[u, :, 0:V_DIM] = vh
                    v_scr[u, :, V_DIM:LANES] = ones
                else:
                    v_scr[u, :, 0:V_DIM] = ones
                    v_scr[u, :, V_DIM:LANES] = vh

    tq = q_ref.shape[0] // sb
    units = sb * N_HEADS
    for g in range(sb):
        for hd in range(N_HEADS):
            q_scr[g * N_HEADS + hd] = q_ref[g * tq:(g + 1) * tq, hd * HEAD_W:(hd + 1) * HEAD_W]
    lane = lax.broadcasted_iota(jnp.int32, (tq, LANES), 1)
    pairs = []
    for u0 in range(0, units, hg):
        us = slice(u0, u0 + hg)
        s = lax.dot_general(q_scr[us], k_scr[us], (((2,), (2,)), ((0,), (0,))), preferred_element_type=f32)
        mx = jnp.max(s, axis=-1, keepdims=True)
        p = jnp.exp2(s - mx).astype(bf16)
        r = lax.dot_general(p, v_scr[us], (((2,), (1,)), ((0,), (0,))), preferred_element_type=f32)
        for j in range(hg // 2):
            re, ro = r[2 * j], r[2 * j + 1]
            oe = re * (1.0 / re[:, V_DIM:V_DIM + 1])
            oo = ro * (1.0 / ro[:, 0:1])
            pairs.append(jnp.where(lane < V_DIM, oe, oo))
    per_seq = N_HEADS // 2
    attn = jnp.concatenate(
        [jnp.concatenate(pairs[g * per_seq:(g + 1) * per_seq], axis=-1) for g in range(sb)], axis=0).astype(bf16)
    merged = (cm_ref[...].astype(f32) + g_ref[...].astype(f32) * _dot(attn, wo_ref[...])).astype(bf16)

    g1 = m_ref[2:3, :]
    sh2 = m_ref[3:4, :]
    sc2 = m_ref[4:5, :]
    x1 = x_ref[...] + g1 * _dot(merged, wout_ref[...])
    x1_ref[...] = x1
    h2 = _rms(x1, n2_ref[...]) * (1.0 + sc2) + sh2
    hi = h2.astype(bf16)
    h2_ref[...] = hi
    lo = (h2 - hi.astype(f32)).astype(bf16)
    both = _dot(jnp.concatenate([hi, lo], axis=0), wr_ref[...])
    rows = hi.shape[0]
    s = both[0:rows] + both[rows:2 * rows]
    lg_ref[...] = s + pltpu.roll(s, LANES - N_EXPERTS, axis=1)


def _attn(q2d, ckv3, kpe3, cm2d, g2d, x2d, m3, wk, wv, wo, wout, norm2, wr, *, n, tq):
    nseq, s_len, _ = ckv3.shape
    sb = max(1, ATTN_TOKENS // n) if (tq == n and m3.shape[0] == 1) else 1
    units = sb * N_HEADS
    hg = max(2, min(units, SCORE_BYTES // (tq * s_len * 4)))
    assert units % hg == 0 and hg % 2 == 0 and nseq % sb == 0
    qb = n // tq
    tokens = nseq * n
    tile = lambda w: pl.BlockSpec((sb * tq, w), lambda s, i: (s * qb + i, 0))
    return pl.pallas_call(
        functools.partial(_attn_kernel, sb=sb, hg=hg),
        grid=(nseq // sb, qb),
        in_specs=[
            tile(N_HEADS * HEAD_W),
            pl.BlockSpec((sb, s_len, KV_LORA), lambda s, i: (s, 0, 0)),
            pl.BlockSpec((sb, s_len, LANES), lambda s, i: (s, 0, 0)),
            tile(D_MODEL),
            tile(D_MODEL),
            tile(D_MODEL),
            _mod_spec(m3, lambda s, i: s),
            _const_spec(wk.shape),
            _const_spec(wv.shape),
            _const_spec(wo.shape),
            _const_spec(wout.shape),
            _const_spec((1, D_MODEL)),
            _const_spec(wr.shape),
        ],
        out_specs=(tile(D_MODEL), tile(D_MODEL), tile(LANES)),
        out_shape=(
            jax.ShapeDtypeStruct((tokens, D_MODEL), f32),
            jax.ShapeDtypeStruct((tokens, D_MODEL), bf16),
            jax.ShapeDtypeStruct((tokens, LANES), f32),
        ),
        scratch_shapes=[
            pltpu.VMEM((units, s_len, HEAD_W), bf16),
            pltpu.VMEM((units, s_len, LANES), bf16),
            pltpu.VMEM((units, tq, HEAD_W), bf16),
        ],
        compiler_params=_params(VMEM_LIMIT, "parallel", "arbitrary"),
        name="attn_out",
    )(q2d, ckv3, kpe3, cm2d, g2d, x2d, m3, wk, wv, wo, wout, norm2, wr)


def _route_kernel(lg_ref, pos_ref, aff_ref, meta_ref, *, nseq, n, cap):
    for s in range(nseq):
        lt = lg_ref[s].T[0:N_EXPERTS, :]
        e = jnp.exp(lt - jnp.max(lt, axis=0, keepdims=True))
        aff_ref[s * N_EXPERTS:(s + 1) * N_EXPERTS, :] = e / jnp.sum(e, axis=0, keepdims=True)
    rows = nseq * N_EXPERTS
    capf = float(cap)

    def bit_step(i, t):
        cand = t | (jnp.int32(1) << (30 - i))
        thr = lax.bitcast_convert_type(cand, f32)
        cnt = jnp.sum(jnp.where(aff_ref[...] >= thr, 1.0, 0.0), axis=1, keepdims=True)
        return jnp.where(cnt >= capf, cand, t)

    t = lax.fori_loop(0, 31, bit_step, jnp.zeros((rows, 1), jnp.int32))
    thr = lax.bitcast_convert_type(t, f32)
    need = capf - jnp.sum(jnp.where(aff_ref[...] > thr, 1.0, 0.0), axis=1, keepdims=True)

    blk = MOE_BLOCK
    tri = jnp.where(
        lax.broadcasted_iota(jnp.int32, (blk, blk), 0) < lax.broadcasted_iota(jnp.int32, (blk, blk), 1),
        1.0, 0.0).astype(bf16)
    carry_gt = jnp.zeros((rows, 1), f32)
    carry_eq = jnp.zeros((rows, 1), f32)
    lane = lax.broadcasted_iota(jnp.int32, (rows, LANES), 1)
    meta = jnp.zeros((rows, LANES), f32)
    cmax = jnp.zeros((rows, 1), f32)
    for b in range(n // blk):
        sl = slice(b * blk, (b + 1) * blk)
        ab = aff_ref[:, sl]
        gt = ab > thr
        eq = ab == thr
        gtb = jnp.where(gt, 1.0, 0.0)
        eqb = jnp.where(eq, 1.0, 0.0)
        pre_gt = _dot(gtb.astype(bf16), tri) + carry_gt
        pre_eq = _dot(eqb.astype(bf16), tri) + carry_eq
        meta = jnp.where(lane == b, carry_gt + jnp.minimum(carry_eq, need), meta)
        carry_gt = carry_gt + jnp.sum(gtb, axis=1, keepdims=True)
        carry_eq = carry_eq + jnp.sum(eqb, axis=1, keepdims=True)
        sel = gt | (eq & (pre_eq < need))
        slot = pre_gt + jnp.minimum(pre_eq, need)
        pos_ref[:, sl] = jnp.where(sel, slot, -1.0).astype(jnp.int32)
        cmax = jnp.maximum(cmax, jnp.sum(jnp.where(sel, 1.0, 0.0), axis=1, keepdims=True))
    meta_ref[...] = jnp.where(lane == LANES - 1, cmax, meta).astype(jnp.int32)


def _route(lg3, *, cap):
    nseq, n, _ = lg3.shape
    rows = nseq * N_EXPERTS
    assert n // MOE_BLOCK < LANES
    return pl.pallas_call(
        functools.partial(_route_kernel, nseq=nseq, n=n, cap=cap),
        grid=(1,),
        in_specs=[_const_spec(lg3.shape)],
        out_specs=(_const_spec((rows, n)), _const_spec((rows, n)), _const_spec((rows, LANES))),
        out_shape=(jax.ShapeDtypeStruct((rows, n), jnp.int32), jax.ShapeDtypeStruct((rows, n), f32),
                   jax.ShapeDtypeStruct((rows, LANES), jnp.int32)),
        compiler_params=_params(VMEM_SMALL, "arbitrary"),
        name="route",
    )(lg3)


def _slot_hits(pos_ref, e, cap):
    width = pos_ref.shape[1]
    return lax.broadcasted_iota(jnp.int32, (cap, width), 0) == pos_ref[e:e + 1, :]


def _one_hot(hits):
    return jnp.concatenate([jnp.where(h, 1.0, 0.0).astype(bf16) for h in hits], axis=0)


def _gather_dense(pos_ref, aff_ref, h2_ref, xg_ref, vals_ref, rows, cap, ne):
    for e0 in range(0, N_EXPERTS, ne):
        hits = [_slot_hits(pos_ref, e0 + e, cap) for e in range(ne)]
        xg = _dot(_one_hot(hits), h2_ref[...]).astype(xg_ref.dtype)
        for e in range(ne):
            xg_ref[e0 + e, rows, :] = xg[e * cap:(e + 1) * cap, :]
            vals = jnp.sum(jnp.where(hits[e], aff_ref[e0 + e:e0 + e + 1, :], 0.0), axis=1, keepdims=True)
            vals_ref[e0 + e, rows, :] = jnp.broadcast_to(vals, (cap, LANES))


def _gather_kernel(pos_ref, aff_ref, h2_ref, xg_ref, vals_ref, *, cap, ne, sb):
    for g in range(sb):
        _gather_dense(pos_ref.at[g], aff_ref.at[g], h2_ref.at[g], xg_ref, vals_ref,
                      slice(g * cap, (g + 1) * cap), cap, ne)


def _gather(pos3, aff3, h23, *, cap, ne, sb):
    nseq, _, n = pos3.shape
    return pl.pallas_call(
        functools.partial(_gather_kernel, cap=cap, ne=ne, sb=sb),
        grid=(nseq // sb,),
        in_specs=[
            pl.BlockSpec((sb, N_EXPERTS, n), lambda s: (s, 0, 0)),
            pl.BlockSpec((sb, N_EXPERTS, n), lambda s: (s, 0, 0)),
            pl.BlockSpec((sb, n, D_MODEL), lambda s: (s, 0, 0)),
        ],
        out_specs=(
            pl.BlockSpec((N_EXPERTS, sb * cap, D_MODEL), lambda s: (0, s, 0)),
            pl.BlockSpec((N_EXPERTS, sb * cap, LANES), lambda s: (0, s, 0)),
        ),
        out_shape=(
            jax.ShapeDtypeStruct((N_EXPERTS, nseq * cap, D_MODEL), bf16),
            jax.ShapeDtypeStruct((N_EXPERTS, nseq * cap, LANES), f32),
        ),
        compiler_params=_params(VMEM_SMALL, "parallel"),
        name="gather",
    )(pos3, aff3, h23)


def _windows_fit(meta_ref):
    cmax = meta_ref[0, LANES - 1]
    for e in range(1, N_EXPERTS):
        cmax = jnp.maximum(cmax, meta_ref[e, LANES - 1])
    return cmax <= MOE_WIN - WIN_ALIGN


def _win_base(meta_ref, e, b, cap):
    start = meta_ref[e, b]
    return pl.multiple_of(jnp.minimum(start - start % WIN_ALIGN, cap - MOE_WIN), WIN_ALIGN)


def _win_hits(pos_row, base):
    slot = lax.broadcasted_iota(jnp.int32, (MOE_WIN, pos_row.shape[1]), 0) + base
    return slot == pos_row


def _gather_win_kernel(meta_ref, pos_ref, aff_ref, h2_ref, xg_ref, vals_ref, *, n, cap):
    fits = _windows_fit(meta_ref)

    @pl.when(fits)
    def _():
        xg_ref[...] = jnp.zeros(xg_ref.shape, xg_ref.dtype)
        vals_ref[...] = jnp.zeros(vals_ref.shape, vals_ref.dtype)

        def block(b, carry):
            cols = pl.ds(pl.multiple_of(b * MOE_BLOCK, MOE_BLOCK), MOE_BLOCK)
            bases = [_win_base(meta_ref, e, b, cap) for e in range(N_EXPERTS)]
            hits = [_win_hits(pos_ref[e:e + 1, cols], bases[e]) for e in range(N_EXPERTS)]
            part = _dot(_one_hot(hits), h2_ref[cols, :])
            for e in range(N_EXPERTS):
                rows = pl.ds(bases[e], MOE_WIN)
                xg_ref[e, rows, :] += part[e * MOE_WIN:(e + 1) * MOE_WIN, :].astype(xg_ref.dtype)
                vals = jnp.sum(jnp.where(hits[e], aff_ref[e:e + 1, cols], 0.0), axis=1, keepdims=True)
                vals_ref[e, rows, :] += jnp.broadcast_to(vals, (MOE_WIN, LANES))
            return carry

        lax.fori_loop(0, n // MOE_BLOCK, block, 0)

    @pl.when(jnp.logical_not(fits))
    def _():
        _gather_dense(pos_ref, aff_ref, h2_ref, xg_ref, vals_ref, slice(0, cap), cap, 1)


def _gather_win(meta, pos3, aff3, h23, *, cap):
    nseq, _, n = pos3.shape
    return pl.pallas_call(
        functools.partial(_gather_win_kernel, n=n, cap=cap),
        grid=(nseq,),
        in_specs=[
            pl.BlockSpec((N_EXPERTS, LANES), lambda s: (s, 0), memory_space=pltpu.SMEM),
            pl.BlockSpec((None, N_EXPERTS, n), lambda s: (s, 0, 0)),
            pl.BlockSpec((None, N_EXPERTS, n), lambda s: (s, 0, 0)),
            pl.BlockSpec((None, n, D_MODEL), lambda s: (s, 0, 0)),
        ],
        out_specs=(
            pl.BlockSpec((N_EXPERTS, cap, D_MODEL), lambda s: (0, s, 0)),
            pl.BlockSpec((N_EXPERTS, cap, LANES), lambda s: (0, s, 0)),
        ),
        out_shape=(
            jax.ShapeDtypeStruct((N_EXPERTS, nseq * cap, D_MODEL), bf16),
            jax.ShapeDtypeStruct((N_EXPERTS, nseq * cap, LANES), f32),
        ),
        compiler_params=_params(VMEM_LARGE, "parallel"),
        name="gather_win",
    )(meta, pos3, aff3, h23)


def _experts_kernel(xp_ref, vp_ref, xs_ref, vs_ref, wg_ref, wu_ref, wd_ref, yp_ref, ys_ref, *, rc):
    wg = wg_ref[...].astype(bf16)
    wu = wu_ref[...].astype(bf16)
    wd = wd_ref[...].astype(bf16)
    for x_ref, v_ref, y_ref in ((xp_ref, vp_ref, yp_ref), (xs_ref, vs_ref, ys_ref)):
        for r0 in range(0, x_ref.shape[0], rc):
            x = x_ref[r0:r0 + rc, :]
            a = _dot(x, wg)
            u = _dot(x, wu)
            hm = (a * jax.nn.sigmoid(a) * u).astype(bf16)
            y = _dot(hm, wd) * v_ref[r0:r0 + rc, 0:1]
            y_ref[r0:r0 + rc, :] = y.astype(y_ref.dtype)


def _experts(xg_p, vals_p, xg_s, vals_s, wg, wu, wd, *, rc=EXPERT_ROWS):
    rp = xg_p.shape[1]
    rs = xg_s.shape[1]
    per_e = lambda r, w: pl.BlockSpec((None, r, w), lambda e: (e, 0, 0))
    return pl.pallas_call(
        functools.partial(_experts_kernel, rc=rc),
        grid=(N_EXPERTS,),
        in_specs=[
            per_e(rp, D_MODEL), per_e(rp, LANES), per_e(rs, D_MODEL), per_e(rs, LANES),
            per_e(D_MODEL, EXPERT_FF), per_e(D_MODEL, EXPERT_FF), per_e(EXPERT_FF, D_MODEL),
        ],
        out_specs=(per_e(rp, D_MODEL), per_e(rs, D_MODEL)),
        out_shape=(
            jax.ShapeDtypeStruct((N_EXPERTS, rp, D_MODEL), bf16),
            jax.ShapeDtypeStruct((N_EXPERTS, rs, D_MODEL), bf16),
        ),
        compiler_params=_params(VMEM_LARGE, "parallel"),
        name="experts",
    )(xg_p, vals_p, xg_s, vals_s, wg, wu, wd)


def _scatter_dense(pos_ref, y_ref, rows, tn, cap, ne):
    moe = jnp.zeros((tn, D_MODEL), f32)
    for e0 in range(0, N_EXPERTS, ne):
        onehot = _one_hot([_slot_hits(pos_ref, e0 + e, cap) for e in range(ne)])
        y = jnp.concatenate([y_ref[e0 + e, rows, :] for e in range(ne)], axis=0)
        moe = moe + lax.dot_general(onehot, y, (((0,), (0,)), ((), ())), preferred_element_type=f32)
    return moe


def _scatter_kernel(pos_ref, y_ref, x1_ref, m_ref, fn_ref, o_ref, *, tn, cap, ne, sb):
    g2 = m_ref[5:6, :]
    for g in range(sb):
        moe = _scatter_dense(pos_ref.at[g], y_ref, slice(g * cap, (g + 1) * cap), tn, cap, ne)
        o_ref[g] = _rms(x1_ref[g] + g2 * moe, fn_ref[...])


def _scatter(pos3, y, x13, m3, fn, *, cap, ne, tn, sb):
    nseq, _, n = pos3.shape
    assert sb == 1 or m3.shape[0] == 1, "sequences sharing a grid step must share their modulation rows"
    return pl.pallas_call(
        functools.partial(_scatter_kernel, tn=tn, cap=cap, ne=ne, sb=sb),
        grid=(nseq // sb, n // tn),
        in_specs=[
            pl.BlockSpec((sb, N_EXPERTS, tn), lambda s, i: (s, 0, i)),
            pl.BlockSpec((N_EXPERTS, sb * cap, D_MODEL), lambda s, i: (0, s, 0)),
            pl.BlockSpec((sb, tn, D_MODEL), lambda s, i: (s, i, 0)),
            _mod_spec(m3, lambda s, i: s),
            _const_spec((1, D_MODEL)),
        ],
        out_specs=pl.BlockSpec((sb, tn, D_MODEL), lambda s, i: (s, i, 0)),
        out_shape=jax.ShapeDtypeStruct((nseq, n, D_MODEL), f32),
        compiler_params=_params(VMEM_MID, "parallel", "arbitrary"),
        name="scatter",
    )(pos3, y, x13, m3, fn)


def _scatter_win_kernel(meta_ref, pos_ref, y_ref, x1_ref, m_ref, fn_ref, o_ref, moe_scr, *, cap):
    b = pl.program_id(1)
    fits = _windows_fit(meta_ref)

    @pl.when(fits)
    def _():
        bases = [_win_base(meta_ref, e, b, cap) for e in range(N_EXPERTS)]
        onehot = _one_hot([_win_hits(pos_ref[e:e + 1, :], bases[e]) for e in range(N_EXPERTS)])
        y = jnp.concatenate([y_ref[e, pl.ds(bases[e], MOE_WIN), :] for e in range(N_EXPERTS)], axis=0)
        moe_scr[...] = lax.dot_general(onehot, y, (((0,), (0,)), ((), ())), preferred_element_type=f32)

    @pl.when(jnp.logical_not(fits))
    def _():
        moe_scr[...] = _scatter_dense(pos_ref, y_ref, slice(0, cap), MOE_BLOCK, cap, 1)

    o_ref[...] = _rms(x1_ref[...] + m_ref[5:6, :] * moe_scr[...], fn_ref[...])


def _scatter_win(meta, pos3, y, x13, m3, fn, *, cap):
    nseq, _, n = pos3.shape
    tn = MOE_BLOCK
    return pl.pallas_call(
        functools.partial(_scatter_win_kernel, cap=cap),
        grid=(nseq, n // tn),
        in_specs=[
            pl.BlockSpec((N_EXPERTS, LANES), lambda s, i: (s, 0), memory_space=pltpu.SMEM),
            pl.BlockSpec((None, N_EXPERTS, tn), lambda s, i: (s, 0, i)),
            pl.BlockSpec((N_EXPERTS, cap, D_MODEL), lambda s, i: (0, s, 0)),
            pl.BlockSpec((None, tn, D_MODEL), lambda s, i: (s, i, 0)),
            _mod_spec(m3, lambda s, i: s),
            _const_spec((1, D_MODEL)),
        ],
        out_specs=pl.BlockSpec((None, tn, D_MODEL), lambda s, i: (s, i, 0)),
        out_shape=jax.ShapeDtypeStruct((nseq, n, D_MODEL), f32),
        scratch_shapes=[pltpu.VMEM((tn, D_MODEL), f32)],
        compiler_params=_params(VMEM_MID, "parallel", "arbitrary"),
        name="scatter_win",
    )(meta, pos3, y, x13, m3, fn)


def _rope_tables(n):
    t = np.arange(n)
    half = QK_ROPE // 2
    freqs = ROPE_BASE ** (-np.arange(0, half, 2, dtype=np.float64) / half)
    ang_r = (t // GRID_W)[:, None] * freqs
    ang_c = (t % GRID_W)[:, None] * freqs
    cr, sr, cc, sc = np.cos(ang_r), np.sin(ang_r), np.cos(ang_c), np.sin(ang_c)
    cos = np.ones((n, HEAD_W))
    sin = np.zeros((n, HEAD_W))
    cos[:, ROPE_OFF:ROPE_OFF + QK_ROPE] = np.concatenate([cr, cr, cc, cc], axis=-1)
    sin[:, ROPE_OFF:ROPE_OFF + QK_ROPE] = np.concatenate([-sr, sr, -sc, sc], axis=-1)
    return jnp.asarray(cos, f32), jnp.asarray(sin, f32)


_PARTNER = np.concatenate([np.arange(8, 16), np.arange(0, 8), np.arange(24, 32), np.arange(16, 24)])


def _rope_partner(w):
    q = QK_ROPE // 4
    return jnp.concatenate([w[..., q:2 * q], w[..., 0:q], w[..., 3 * q:4 * q], w[..., 2 * q:3 * q]], axis=-1)


def _rope_placement():
    place = np.zeros((LANES, 2 * LANES), np.float32)
    d = np.arange(QK_ROPE)
    place[d, ROPE_OFF + d] = 1.0
    place[_PARTNER, LANES + ROPE_OFF + d] = 1.0
    return jnp.asarray(place, bf16)


def _head_blocks(w_nope, w_rope):
    rows = w_nope.shape[0]
    if w_rope is None:
        w_rope = jnp.zeros((rows, N_HEADS, QK_ROPE), w_nope.dtype)
    z = jnp.zeros((rows, N_HEADS, HEAD_W - QK_NOPE - QK_ROPE), w_nope.dtype)
    return jnp.concatenate([w_nope, w_rope, z], axis=-1).reshape(rows, N_HEADS * HEAD_W)


def kernel(x_prompt, x_sample, cache_ckv, cache_kpe, c, c_ctx, w_ada, b_ada, norm1, w_in, conv_dw, conv_dw_b,
           conv_ln_g, conv_ln_b, w_conv_out, q_norm, w_qb, kv_norm, w_kvb, w_o_mla, w_out, norm2, w_router,
           w_e_gate, w_e_up, w_e_down, final_norm):
    assert w_ada.shape[0] == 1, "single trunk layer"
    nb_p, n_p, _ = x_prompt.shape
    nb_s, n_s, _ = x_sample.shape

    win = _wprep(w_in[0].T, _rope_placement())
    wq = w_qb[0].reshape(Q_LORA, N_HEADS, QK_NOPE + QK_ROPE)
    wqb = _head_blocks(wq[..., :QK_NOPE], wq[..., QK_NOPE:]).astype(bf16)
    wqbp = _head_blocks(jnp.zeros_like(wq[..., :QK_NOPE]), _rope_partner(wq[..., QK_NOPE:])).astype(bf16)
    wkv = w_kvb[0].reshape(KV_LORA, N_HEADS, QK_NOPE + V_DIM)
    wk = _head_blocks(wkv[..., :QK_NOPE], None).astype(bf16)
    wv = wkv[..., QK_NOPE:].reshape(KV_LORA, N_HEADS * V_DIM).astype(bf16)
    wco = w_conv_out[0].astype(bf16)
    wo = w_o_mla[0].astype(bf16)
    wout = w_out[0].astype(bf16)
    wr_hi = w_router[0].astype(bf16)
    wr_lo = (w_router[0] - wr_hi.astype(f32)).astype(bf16)
    wr = jnp.concatenate([wr_hi, wr_lo, jnp.zeros((D_MODEL, LANES - 2 * N_EXPERTS), bf16)], axis=-1)
    row = lambda a: a.reshape(1, -1)

    mod = jnp.concatenate([c_ctx[None, :], c, jnp.zeros((8 - 1 - nb_s, D_MODEL), f32)], axis=0)
    m = _ada(mod, w_ada[0], b_ada[0]).reshape(8, 6, D_MODEL)
    m_p, m_s = m[0:1], m[1:1 + nb_s]
    cos, sin = _rope_tables(n_s)

    conv_w = _conv_weights(conv_dw[0], row(conv_dw_b[0]), row(conv_ln_g[0]), row(conv_ln_b[0]), wco, CONV_CHUNK)

    def mixers(x, m3, rope, ctx_ckv, ctx_kpe):
        nseq, n, _ = x.shape
        x2d = x.reshape(nseq * n, D_MODEL)
        cm, q, ckv, kr, g, *kpe = _inproj(x2d, m3, row(norm1[0]), win, row(q_norm[0]), wqb, wqbp, row(kv_norm[0]),
                                          cos, sin, conv_w, n=n, rope=rope, tm=IN_TILE)
        keys_ckv = ckv.reshape(nseq, n, KV_LORA)
        keys_kpe = kr.reshape(nseq, n, LANES)
        if ctx_ckv is not None:
            keys_ckv = jnp.concatenate([ctx_ckv.astype(keys_ckv.dtype), keys_ckv], axis=1)
            keys_kpe = jnp.concatenate([ctx_kpe.astype(keys_kpe.dtype), keys_kpe], axis=1)
        x1, h2, lg = _attn(q, keys_ckv, keys_kpe, cm.reshape(nseq * n, D_MODEL), g, x2d, m3, wk, wv, wo, wout,
                           row(norm2[0]), wr, n=n, tq=min(n, Q_TILE))
        return x1, h2, lg, ckv, kpe

    ctx_kpe = jnp.pad(cache_kpe[:, 0], ((0, 0), (0, 0), (ROPE_OFF, LANES - ROPE_OFF - QK_ROPE)))
    x1_p, h2_p, lg_p, ckv_p, (kpe_p,) = mixers(x_prompt, m_p, False, None, None)
    x1_s, h2_s, lg_s, _, _ = mixers(x_sample, m_s, True, cache_ckv[:, 0], ctx_kpe)

    def moe_tiles(n):
        cap = EC_FACTOR * n // N_EXPERTS
        ne = N_EXPERTS if N_EXPERTS * cap <= MOE_ROWS else 1
        sb = max(1, MOE_TOKENS // n)
        return cap, ne, sb

    def windowed(n):
        return n >= 4 * MOE_BLOCK and EC_FACTOR * n // N_EXPERTS >= 2 * MOE_WIN

    def route_gather(h2, lg, nseq, n):
        cap, ne, sb = moe_tiles(n)
        pos, aff, meta = _route(lg.reshape(nseq, n, LANES), cap=cap)
        pos3 = pos.reshape(nseq, N_EXPERTS, n)
        aff3 = aff.reshape(nseq, N_EXPERTS, n)
        h23 = h2.reshape(nseq, n, D_MODEL)
        if windowed(n):
            xg, vals = _gather_win(meta, pos3, aff3, h23, cap=cap)
        else:
            xg, vals = _gather(pos3, aff3, h23, cap=cap, ne=ne, sb=sb)
        return pos3, meta, xg, vals

    pos_p, meta_p, xg_p, vals_p = route_gather(h2_p, lg_p, nb_p, n_p)
    pos_s, meta_s, xg_s, vals_s = route_gather(h2_s, lg_s, nb_s, n_s)
    y_p, y_s = _experts(xg_p, vals_p, xg_s, vals_s, w_e_gate[0], w_e_up[0], w_e_down[0])
    fn = row(final_norm)

    def scatter(pos, meta, y, x1, m3, nseq, n):
        cap, ne, sb = moe_tiles(n)
        x13 = x1.reshape(nseq, n, D_MODEL)
        if windowed(n):
            return _scatter_win(meta, pos, y, x13, m3, fn, cap=cap)
        return _scatter(pos, y, x13, m3, fn, cap=cap, ne=ne, tn=min(n, SCATTER_TILE), sb=sb)

    y_prompt = scatter(pos_p, meta_p, y_p, x1_p, m_p, nb_p, n_p)
    y_sample = scatter(pos_s, meta_s, y_s, x1_s, m_s, nb_s, n_s)

    new_ckv = ckv_p.reshape(nb_p, 1, n_p, KV_LORA)
    new_kpe = kpe_p.reshape(nb_p, 1, n_p, QK_ROPE)
    return (y_prompt, y_sample, new_ckv, new_kpe)
```

```python
import functools

import jax
import jax.numpy as jnp
import numpy as np
from jax import lax
from jax.experimental import pallas as pl
from jax.experimental.pallas import tpu as pltpu

D_MODEL = 1024
GRID_W = 64
CONV_DIM = 512
CONV_WIDTH = 31
N_HEADS = 8
QK_NOPE = 64
QK_ROPE = 32
V_DIM = 64
Q_LORA = 256
KV_LORA = 128
N_EXPERTS = 16
EXPERT_FF = 512
EC_FACTOR = 2
ROPE_BASE = 10000.0
EPS = 1e-6

LANES = 128
HEAD_W = LANES
ROPE_OFF = QK_NOPE
CONV_HALO = 16
LOG2E = 1.4426950408889634
VMEM_LIMIT = 48 * 1024 * 1024
IN_TILE = 512
CONV_CHUNK = 64
CONV_OUT_ROWS = 256
GATE_CHUNK = 512
EXPERT_ROWS = 512
Q_TILE = 512
ATTN_TOKENS = 1024
SCORE_BYTES = 12 * 1024 * 1024
MOE_ROWS = 512
MOE_TOKENS = 1024
SCATTER_TILE = 512
MOE_BLOCK = 2 * LANES
MOE_WIN = 80
WIN_ALIGN = 16

C_CONV = 0
C_QA = 2 * CONV_DIM
C_KVA = C_QA + Q_LORA
C_KR = C_KVA + KV_LORA
C_GATE = C_KR + LANES
C_END = C_GATE + 2 * D_MODEL
C_KRP = C_END

f32 = jnp.float32
bf16 = jnp.bfloat16


MIB = 1024 * 1024
VMEM_SMALL, VMEM_MID, VMEM_LARGE = 16 * MIB, 32 * MIB, 40 * MIB


def _params(vmem_bytes, *sem):
    assert vmem_bytes <= VMEM_LIMIT
    return pltpu.CompilerParams(dimension_semantics=sem, vmem_limit_bytes=vmem_bytes)


def _dot(a, b):
    return jnp.dot(a, b, preferred_element_type=f32)


def _rms(x, g):
    return x * lax.rsqrt(jnp.mean(x * x, axis=-1, keepdims=True) + EPS) * g


def _const_spec(shape):
    nd = len(shape)
    return pl.BlockSpec(shape, lambda *_: (0,) * nd)


def _mod_spec(m3, seq_of):
    if m3.shape[0] == 1:
        return _const_spec((None, 6, D_MODEL))
    return pl.BlockSpec((None, 6, D_MODEL), lambda *idx: (seq_of(*idx), 0, 0))


def _ada_kernel(s_ref, w_ref, b_ref, o_ref):
    s = s_ref[...]
    s = s * jax.nn.sigmoid(s)
    o_ref[...] = _dot(s.astype(bf16), w_ref[...].astype(bf16)) + b_ref[...]


def _ada(mod, w_ada, b_ada):
    rows = mod.shape[0]
    n_out = w_ada.shape[1]
    tn = D_MODEL
    return pl.pallas_call(
        _ada_kernel,
        grid=(n_out // tn,),
        in_specs=[
            _const_spec((rows, D_MODEL)),
            pl.BlockSpec((D_MODEL, tn), lambda j: (0, j)),
            pl.BlockSpec((1, tn), lambda j: (0, j)),
        ],
        out_specs=pl.BlockSpec((rows, tn), lambda j: (0, j)),
        out_shape=jax.ShapeDtypeStruct((rows, n_out), f32),
        compiler_params=_params(VMEM_LIMIT, "arbitrary"),
        name="ada",
    )(mod, w_ada, b_ada.reshape(1, n_out))


def _wprep_kernel(wt_ref, place_ref, o_ref):
    def block(r0):
        return wt_ref[r0:r0 + LANES, :].T.astype(bf16)

    for j in range(C_KR // LANES):
        o_ref[:, j * LANES:(j + 1) * LANES] = block(j * LANES)
    placed = _dot(block(C_KR), place_ref[...])
    o_ref[:, C_KR:C_GATE] = placed[:, 0:LANES].astype(bf16)
    o_ref[:, C_KRP:C_KRP + LANES] = placed[:, LANES:2 * LANES].astype(bf16)
    for j in range(2 * D_MODEL // LANES):
        o_ref[:, C_GATE + j * LANES:C_GATE + (j + 1) * LANES] = block(C_KR + QK_ROPE + j * LANES)


def _wprep(w_in_t, place):
    cols, rows = w_in_t.shape
    return pl.pallas_call(
        _wprep_kernel,
        grid=(rows // LANES,),
        in_specs=[pl.BlockSpec((cols, LANES), lambda i: (0, i)), _const_spec(place.shape)],
        out_specs=pl.BlockSpec((LANES, C_KRP + LANES), lambda i: (i, 0)),
        out_shape=jax.ShapeDtypeStruct((rows, C_KRP + LANES), bf16),
        compiler_params=_params(VMEM_SMALL, "parallel"),
        name="wprep",
    )(w_in_t, place)


def _conv_taps(vpad, r0, shift_ref, dw_ref, dwb_ref, ybuf, y0, ct):
    pad = CONV_WIDTH // 2
    sub = 8
    span = ((CONV_HALO - pad + CONV_WIDTH - 1) // sub) * sub
    cw = 2 * LANES
    for cb in range(CONV_DIM // cw):
        sl = slice(cb * cw, (cb + 1) * cw)
        win = vpad[pl.ds(r0, ct + 2 * CONV_HALO), sl]
        acc = jnp.zeros((ct // sub, sub, cw), f32)
        for ph in range(sub):
            wph = win.astype(f32) if ph == 0 else _dot(shift_ref[ph - 1], win)
            for a in range(span // sub + 1):
                k = a * sub + ph - (CONV_HALO - pad)
                if 0 <= k < CONV_WIDTH:
                    acc = acc + wph[a * sub:a * sub + ct, :].reshape(ct // sub, sub, cw) * dw_ref[k, :, sl][None]
        ybuf[pl.ds(y0, ct), sl] = acc.reshape(ct, cw) + dwb_ref[:, sl]


def _conv_out(y, lng_ref, lnb_ref, wco_ref, gate):
    mu = jnp.mean(y, axis=-1, keepdims=True)
    yc = y - mu
    var = jnp.mean(yc * yc, axis=-1, keepdims=True)
    z = yc * lax.rsqrt(var + EPS) * lng_ref[...] + lnb_ref[...]
    z = z * jax.nn.sigmoid(z)
    return (gate.astype(f32) * _dot(z.astype(bf16), wco_ref[...])).astype(bf16)


def _fill_padded(vpad, v, n):
    zeros = jnp.zeros((CONV_HALO, CONV_DIM), vpad.dtype)
    vpad[0:CONV_HALO, :] = zeros
    vpad[CONV_HALO + n:2 * CONV_HALO + n, :] = zeros
    vpad[CONV_HALO:CONV_HALO + n, :] = v


def _inproj_kernel(*refs, rope, halo, tiles_per_seq, q_scale, n, rt, ct):
    refs = list(refs)
    x_ref = refs.pop(0)
    if halo:
        xp_ref, xn_ref = refs[:2]
        del refs[:2]
    m_ref, n1_ref, win_ref, qn_ref, wqb_ref = refs[:5]
    del refs[:5]
    if rope:
        wqbp_ref = refs.pop(0)
    kvn_ref = refs.pop(0)
    if rope:
        cos_ref, sin_ref = refs[:2]
        del refs[:2]
    shift_ref, dw_ref, dwb_ref, lng_ref, lnb_ref, wco_ref = refs[:6]
    del refs[:6]
    cm_ref, q_ref, ckv_ref, kr_ref, g_ref = refs[:5]
    del refs[:5]
    if not rope:
        kpe_ref = refs.pop(0)
    vpad, ybuf, gc = refs
    tm = x_ref.shape[0]
    lo = CONV_HALO if halo else 0

    sh1 = m_ref[0:1, :]
    sc1 = m_ref[1:2, :]
    x = jnp.concatenate([xp_ref[...], x_ref[...], xn_ref[...]], axis=0) if halo else x_ref[...]
    hb_all = (_rms(x, n1_ref[...]) * (1.0 + sc1) + sh1).astype(bf16)

    left_all = _dot(hb_all, win_ref[:, 0:C_GATE])
    v = (left_all[:, C_CONV:C_CONV + CONV_DIM] * jax.nn.sigmoid(left_all[:, C_CONV + CONV_DIM:C_QA])).astype(bf16)
    hb = hb_all[lo:lo + tm, :]
    left = left_all[lo:lo + tm, :]

    if halo:
        j = pl.program_id(0) % tiles_per_seq
        zeros = jnp.zeros((CONV_HALO, CONV_DIM), bf16)
        vpad[0, 0:lo, :] = jnp.where(j > 0, v[0:lo, :], zeros)
        vpad[0, lo:lo + tm, :] = v[lo:lo + tm, :]
        vpad[0, lo + tm:2 * lo + tm, :] = jnp.where(j < tiles_per_seq - 1, v[lo + tm:2 * lo + tm, :], zeros)
        chunks = [(0, c * ct, c * ct) for c in range(tm // ct)]
    else:
        for s in range(tm // n):
            _fill_padded(vpad.at[s], v[s * n:(s + 1) * n, :], n)
        chunks = [(s, c * ct, s * n + c * ct) for s in range(tm // n) for c in range(n // ct)]

    def conv_some(count):
        for _ in range(min(count, len(chunks))):
            s, r0, y0 = chunks.pop(0)
            _conv_taps(vpad.at[s], r0, shift_ref, dw_ref, dwb_ref, ybuf, y0, ct)

    steps = 2 + 2 * D_MODEL // GATE_CHUNK
    per_step = -(-len(chunks) // steps)

    qa = left[:, C_QA:C_KVA]
    qn = _rms(qa, qn_ref[...]).astype(bf16)
    q = _dot(qn, wqb_ref[...])
    if rope:
        qp = _dot(qn, wqbp_ref[...])
        cos = cos_ref[...]
        sin = sin_ref[...]
        for hd in range(N_HEADS):
            sl = slice(hd * HEAD_W, (hd + 1) * HEAD_W)
            q_ref[:, sl] = ((q[:, sl] * cos + qp[:, sl] * sin) * q_scale).astype(q_ref.dtype)
    else:
        q_ref[...] = (q * q_scale).astype(q_ref.dtype)
    conv_some(per_step)

    kva = left[:, C_KVA:C_KR]
    ckv_ref[...] = _rms(kva, kvn_ref[...]).astype(ckv_ref.dtype)

    kr = left[:, C_KR:C_GATE]
    if rope:
        krp = _dot(hb, win_ref[:, C_KRP:C_KRP + LANES])
        kr = kr * cos_ref[...] + krp * sin_ref[...]
    else:
        kpe_ref[...] = kr[:, ROPE_OFF:ROPE_OFF + QK_ROPE]
    kr_ref[...] = kr.astype(kr_ref.dtype)
    conv_some(per_step)

    gw = GATE_CHUNK
    for jg in range(2 * D_MODEL // gw):
        gate = jax.nn.sigmoid(_dot(hb, win_ref[:, C_GATE + jg * gw:C_GATE + (jg + 1) * gw])).astype(bf16)
        if jg * gw < D_MODEL:
            gc[:, jg * gw:(jg + 1) * gw] = gate
        else:
            g_ref[:, jg * gw - D_MODEL:(jg + 1) * gw - D_MODEL] = gate
        conv_some(per_step)

    conv_some(len(chunks))
    for r0 in range(0, tm, rt):
        cm_ref[r0:r0 + rt, :] = _conv_out(ybuf[r0:r0 + rt, :], lng_ref, lnb_ref, wco_ref, gc[r0:r0 + rt, :])


def _inproj(x2d, m3, norm1, win, q_norm, wqb, wqbp, kv_norm, cos, sin, conv_w, *, n, rope, tm,
            rt=CONV_OUT_ROWS, ct=CONV_CHUNK):
    tokens = x2d.shape[0]
    halo = n > tm
    assert n % tm == 0 if halo else (tm % n == 0 and m3.shape[0] == 1 and not rope)
    tiles_per_seq = max(1, n // tm)
    q_scale = float((QK_NOPE + QK_ROPE) ** -0.5 * LOG2E)
    tile = lambda w: pl.BlockSpec((tm, w), lambda i: (i, 0))
    in_specs = [tile(D_MODEL)]
    args = [x2d]
    if halo:
        hb_tile = tm // CONV_HALO
        hb_seq = n // CONV_HALO
        prev = lambda i: (jnp.maximum(i * hb_tile - 1, (i // tiles_per_seq) * hb_seq), 0)
        nxt = lambda i: (jnp.minimum((i + 1) * hb_tile, (i // tiles_per_seq + 1) * hb_seq - 1), 0)
        in_specs += [pl.BlockSpec((CONV_HALO, D_MODEL), prev), pl.BlockSpec((CONV_HALO, D_MODEL), nxt)]
        args += [x2d, x2d]
    in_specs += [
        _mod_spec(m3, lambda i: i // tiles_per_seq),
        _const_spec((1, D_MODEL)),
        _const_spec((D_MODEL, C_KRP + LANES if rope else C_END)),
        _const_spec((1, Q_LORA)),
        _const_spec(wqb.shape),
    ]
    args += [m3, norm1, win, q_norm, wqb]
    if rope:
        in_specs.append(_const_spec(wqbp.shape))
        args.append(wqbp)
    in_specs.append(_const_spec((1, KV_LORA)))
    args.append(kv_norm)
    if rope:
        tab = pl.BlockSpec((tm, LANES), lambda i: (i % tiles_per_seq, 0))
        in_specs += [tab, tab]
        args += [cos, sin]
    args += list(conv_w)
    in_specs += [_const_spec(w.shape) for w in conv_w]
    pieces = max(1, tm // n)
    scratch = [
        pltpu.VMEM((pieces, tm // pieces + 2 * CONV_HALO, CONV_DIM), bf16),
        pltpu.VMEM((tm, CONV_DIM), f32),
        pltpu.VMEM((tm, D_MODEL), bf16),
    ]
    out_shape = [
        jax.ShapeDtypeStruct((tokens, D_MODEL), bf16),
        jax.ShapeDtypeStruct((tokens, N_HEADS * HEAD_W), bf16),
        jax.ShapeDtypeStruct((tokens, KV_LORA), bf16 if rope else f32),
        jax.ShapeDtypeStruct((tokens, LANES), bf16),
        jax.ShapeDtypeStruct((tokens, D_MODEL), bf16),
    ]
    out_specs = [tile(D_MODEL), tile(N_HEADS * HEAD_W), tile(KV_LORA), tile(LANES), tile(D_MODEL)]
    if not rope:
        out_shape.append(jax.ShapeDtypeStruct((tokens, QK_ROPE), f32))
        out_specs.append(tile(QK_ROPE))
    return pl.pallas_call(
        functools.partial(_inproj_kernel, rope=rope, halo=halo, tiles_per_seq=tiles_per_seq, q_scale=q_scale,
                          n=n, rt=rt, ct=ct),
        grid=(tokens // tm,),
        in_specs=in_specs,
        out_specs=out_specs,
        out_shape=out_shape,
        scratch_shapes=scratch,
        compiler_params=_params(VMEM_MID, "parallel"),
        name="inproj_conv_rope" if rope else "inproj_conv",
    )(*args)


def _conv_weights(dw, dwb, lng, lnb, wco, ct):
    rows = ct + 2 * CONV_HALO
    i = np.arange(rows)
    shifts = jnp.asarray(np.stack([(i[None, :] == i[:, None] + ph) for ph in range(1, 8)]), bf16)
    dw_tiles = jnp.broadcast_to(dw[:, None, :], (CONV_WIDTH, 8, CONV_DIM))
    return shifts, dw_tiles, dwb, lng, lnb, wco


def _attn_kernel(q_ref, ckv_ref, kpe_ref, cm_ref, g_ref, x_ref, m_ref, wk_ref, wv_ref, wo_ref, wout_ref, n2_ref,
                 wr_ref, x1_ref, h2_ref, lg_ref, k_scr, v_scr, q_scr, *, sb, hg):
    s_len = ckv_ref.shape[1]

    @pl.when(pl.program_id(1) == 0)
    def _():
        ckv = ckv_ref[...].reshape(sb * s_len, KV_LORA).astype(bf16)
        kpe = kpe_ref[...].reshape(sb * s_len, LANES).astype(f32)
        k = _dot(ckv, wk_ref[...])
        v = _dot(ckv, wv_ref[...])
        ones = jnp.ones((s_len, V_DIM), bf16)
        for g in range(sb):
            rows = slice(g * s_len, (g + 1) * s_len)
            for hd in range(N_HEADS):
                u = g * N_HEADS + hd
                k_scr[u] = (k[rows, hd * HEAD_W:(hd + 1) * HEAD_W] + kpe[rows, :]).astype(bf16)
                vh = v[rows, hd * V_DIM:(hd + 1) * V_DIM].astype(bf16)
                if hd % 2 == 0:
                    v_scr[u, :, 0:V_DIM] = vh
                    v_scr[u, :, V_DIM:LANES] = ones
                else:
                    v_scr[u, :, 0:V_DIM] = ones
                    v_scr[u, :, V_DIM:LANES] = vh

    tq = q_ref.shape[0] // sb
    units = sb * N_HEADS
    for g in range(sb):
        for hd in range(N_HEADS):
            q_scr[g * N_HEADS + hd] = q_ref[g * tq:(g + 1) * tq, hd * HEAD_W:(hd + 1) * HEAD_W]
    lane = lax.broadcasted_iota(jnp.int32, (tq, LANES), 1)
    pairs = []
    for u0 in range(0, units, hg):
        us = slice(u0, u0 + hg)
        s = lax.dot_general(q_scr[us], k_scr[us], (((2,), (2,)), ((0,), (0,))), preferred_element_type=f32)
        mx = jnp.max(s, axis=-1, keepdims=True)
        p = jnp.exp2(s - mx).astype(bf16)
        r = lax.dot_general(p, v_scr[us], (((2,), (1,)), ((0,), (0,))), preferred_element_type=f32)
        for j in range(hg // 2):
            re, ro = r[2 * j], r[2 * j + 1]
            oe = re * (1.0 / re[:, V_DIM:V_DIM + 1])
            oo = ro * (1.0 / ro[:, 0:1])
            pairs.append(jnp.where(lane < V_DIM, oe, oo))
    per_seq = N_HEADS // 2
    attn = jnp.concatenate(
        [jnp.concatenate(pairs[g * per_seq:(g + 1) * per_seq], axis=-1) for g in range(sb)], axis=0).astype(bf16)
    merged = (cm_ref[...].astype(f32) + g_ref[...].astype(f32) * _dot(attn, wo_ref[...])).astype(bf16)

    g1 = m_ref[2:3, :]
    sh2 = m_ref[3:4, :]
    sc2 = m_ref[4:5, :]
    x1 = x_ref[...] + g1 * _dot(merged, wout_ref[...])
    x1_ref[...] = x1
    h2 = _rms(x1, n2_ref[...]) * (1.0 + sc2) + sh2
    hi = h2.astype(bf16)
    h2_ref[...] = hi
    lo = (h2 - hi.astype(f32)).astype(bf16)
    both = _dot(jnp.concatenate([hi, lo], axis=0), wr_ref[...])
    rows = hi.shape[0]
    s = both[0:rows] + both[rows:2 * rows]
    lg_ref[...] = s + pltpu.roll(s, LANES - N_EXPERTS, axis=1)


def _attn(q2d, ckv3, kpe3, cm2d, g2d, x2d, m3, wk, wv, wo, wout, norm2, wr, *, n, tq):
    nseq, s_len, _ = ckv3.shape
    sb = max(1, ATTN_TOKENS // n) if (tq == n and m3.shape[0] == 1) else 1
    units = sb * N_HEADS
    hg = max(2, min(units, SCORE_BYTES // (tq * s_len * 4)))
    assert units % hg == 0 and hg % 2 == 0 and nseq % sb == 0
    qb = n // tq
    tokens = nseq * n
    tile = lambda w: pl.BlockSpec((sb * tq, w), lambda s, i: (s * qb + i, 0))
    return pl.pallas_call(
        functools.partial(_attn_kernel, sb=sb, hg=hg),
        grid=(nseq // sb, qb),
        in_specs=[
            tile(N_HEADS * HEAD_W),
            pl.BlockSpec((sb, s_len, KV_LORA), lambda s, i: (s, 0, 0)),
            pl.BlockSpec((sb, s_len, LANES), lambda s, i: (s, 0, 0)),
            tile(D_MODEL),
            tile(D_MODEL),
            tile(D_MODEL),
            _mod_spec(m3, lambda s, i: s),
            _const_spec(wk.shape),
            _const_spec(wv.shape),
            _const_spec(wo.shape),
            _const_spec(wout.shape),
            _const_spec((1, D_MODEL)),
            _const_spec(wr.shape),
        ],
        out_specs=(tile(D_MODEL), tile(D_MODEL), tile(LANES)),
        out_shape=(
            jax.ShapeDtypeStruct((tokens, D_MODEL), f32),
            jax.ShapeDtypeStruct((tokens, D_MODEL), bf16),
            jax.ShapeDtypeStruct((tokens, LANES), f32),
        ),
        scratch_shapes=[
            pltpu.VMEM((units, s_len, HEAD_W), bf16),
            pltpu.VMEM((units, s_len, LANES), bf16),
            pltpu.VMEM((units, tq, HEAD_W), bf16),
        ],
        compiler_params=_params(VMEM_LIMIT, "parallel", "arbitrary"),
        name="attn_out",
    )(q2d, ckv3, kpe3, cm2d, g2d, x2d, m3, wk, wv, wo, wout, norm2, wr)


def _route_kernel(lg_ref, pos_ref, aff_ref, meta_ref, *, nseq, n, cap):
    for s in range(nseq):
        lt = lg_ref[s].T[0:N_EXPERTS, :]
        e = jnp.exp(lt - jnp.max(lt, axis=0, keepdims=True))
        aff_ref[s * N_EXPERTS:(s + 1) * N_EXPERTS, :] = e / jnp.sum(e, axis=0, keepdims=True)
    rows = nseq * N_EXPERTS
    capf = float(cap)

    def bit_step(i, t):
        cand = t | (jnp.int32(1) << (30 - i))
        thr = lax.bitcast_convert_type(cand, f32)
        cnt = jnp.sum(jnp.where(aff_ref[...] >= thr, 1.0, 0.0), axis=1, keepdims=True)
        return jnp.where(cnt >= capf, cand, t)

    t = lax.fori_loop(0, 31, bit_step, jnp.zeros((rows, 1), jnp.int32))
    thr = lax.bitcast_convert_type(t, f32)
    need = capf - jnp.sum(jnp.where(aff_ref[...] > thr, 1.0, 0.0), axis=1, keepdims=True)

    blk = MOE_BLOCK
    tri = jnp.where(
        lax.broadcasted_iota(jnp.int32, (blk, blk), 0) < lax.broadcasted_iota(jnp.int32, (blk, blk), 1),
        1.0, 0.0).astype(bf16)
    carry_gt = jnp.zeros((rows, 1), f32)
    carry_eq = jnp.zeros((rows, 1), f32)
    lane = lax.broadcasted_iota(jnp.int32, (rows, LANES), 1)
    meta = jnp.zeros((rows, LANES), f32)
    cmax = jnp.zeros((rows, 1), f32)
    for b in range(n // blk):
        sl = slice(b * blk, (b + 1) * blk)
        ab = aff_ref[:, sl]
        gt = ab > thr
        eq = ab == thr
        gtb = jnp.where(gt, 1.0, 0.0)
        eqb = jnp.where(eq, 1.0, 0.0)
        pre_gt = _dot(gtb.astype(bf16), tri) + carry_gt
        pre_eq = _dot(eqb.astype(bf16), tri) + carry_eq
        meta = jnp.where(lane == b, carry_gt + jnp.minimum(carry_eq, need), meta)
        carry_gt = carry_gt + jnp.sum(gtb, axis=1, keepdims=True)
        carry_eq = carry_eq + jnp.sum(eqb, axis=1, keepdims=True)
        sel = gt | (eq & (pre_eq < need))
        slot = pre_gt + jnp.minimum(pre_eq, need)
        pos_ref[:, sl] = jnp.where(sel, slot, -1.0).astype(jnp.int32)
        cmax = jnp.maximum(cmax, jnp.sum(jnp.where(sel, 1.0, 0.0), axis=1, keepdims=True))
    meta_ref[...] = jnp.where(lane == LANES - 1, cmax, meta).astype(jnp.int32)


def _route(lg3, *, cap):
    nseq, n, _ = lg3.shape
    rows = nseq * N_EXPERTS
    assert n // MOE_BLOCK < LANES
    return pl.pallas_call(
        functools.partial(_route_kernel, nseq=nseq, n=n, cap=cap),
        grid=(1,),
        in_specs=[_const_spec(lg3.shape)],
        out_specs=(_const_spec((rows, n)), _const_spec((rows, n)), _const_spec((rows, LANES))),
        out_shape=(jax.ShapeDtypeStruct((rows, n), jnp.int32), jax.ShapeDtypeStruct((rows, n), f32),
                   jax.ShapeDtypeStruct((rows, LANES), jnp.int32)),
        compiler_params=_params(VMEM_SMALL, "arbitrary"),
        name="route",
    )(lg3)


def _slot_hits(pos_ref, e, cap):
    width = pos_ref.shape[1]
    return lax.broadcasted_iota(jnp.int32, (cap, width), 0) == pos_ref[e:e + 1, :]


def _one_hot(hits):
    return jnp.concatenate([jnp.where(h, 1.0, 0.0).astype(bf16) for h in hits], axis=0)


def _gather_dense(pos_ref, aff_ref, h2_ref, xg_ref, vals_ref, rows, cap, ne):
    for e0 in range(0, N_EXPERTS, ne):
        hits = [_slot_hits(pos_ref, e0 + e, cap) for e in range(ne)]
        xg = _dot(_one_hot(hits), h2_ref[...]).astype(xg_ref.dtype)
        for e in range(ne):
            xg_ref[e0 + e, rows, :] = xg[e * cap:(e + 1) * cap, :]
            vals = jnp.sum(jnp.where(hits[e], aff_ref[e0 + e:e0 + e + 1, :], 0.0), axis=1, keepdims=True)
            vals_ref[e0 + e, rows, :] = jnp.broadcast_to(vals, (cap, LANES))


def _gather_kernel(pos_ref, aff_ref, h2_ref, xg_ref, vals_ref, *, cap, ne, sb):
    for g in range(sb):
        _gather_dense(pos_ref.at[g], aff_ref.at[g], h2_ref.at[g], xg_ref, vals_ref,
                      slice(g * cap, (g + 1) * cap), cap, ne)


def _gather(pos3, aff3, h23, *, cap, ne, sb):
    nseq, _, n = pos3.shape
    return pl.pallas_call(
        functools.partial(_gather_kernel, cap=cap, ne=ne, sb=sb),
        grid=(nseq // sb,),
        in_specs=[
            pl.BlockSpec((sb, N_EXPERTS, n), lambda s: (s, 0, 0)),
            pl.BlockSpec((sb, N_EXPERTS, n), lambda s: (s, 0, 0)),
            pl.BlockSpec((sb, n, D_MODEL), lambda s: (s, 0, 0)),
        ],
        out_specs=(
            pl.BlockSpec((N_EXPERTS, sb * cap, D_MODEL), lambda s: (0, s, 0)),
            pl.BlockSpec((N_EXPERTS, sb * cap, LANES), lambda s: (0, s, 0)),
        ),
        out_shape=(
            jax.ShapeDtypeStruct((N_EXPERTS, nseq * cap, D_MODEL), bf16),
            jax.ShapeDtypeStruct((N_EXPERTS, nseq * cap, LANES), f32),
        ),
        compiler_params=_params(VMEM_MID, "parallel"),
        name="gather",
    )(pos3, aff3, h23)


def _windows_fit(meta_ref):
    cmax = meta_ref[0, LANES - 1]
    for e in range(1, N_EXPERTS):
        cmax = jnp.maximum(cmax, meta_ref[e, LANES - 1])
    return cmax <= MOE_WIN - WIN_ALIGN


def _win_base(meta_ref, e, b, cap):
    start = meta_ref[e, b]
    return pl.multiple_of(jnp.minimum(start - start % WIN_ALIGN, cap - MOE_WIN), WIN_ALIGN)


def _win_hits(pos_row, base):
    slot = lax.broadcasted_iota(jnp.int32, (MOE_WIN, pos_row.shape[1]), 0) + base
    return slot == pos_row


def _gather_win_kernel(meta_ref, pos_ref, aff_ref, h2_ref, xg_ref, vals_ref, *, n, cap):
    fits = _windows_fit(meta_ref)

    @pl.when(fits)
    def _():
        xg_ref[...] = jnp.zeros(xg_ref.shape, xg_ref.dtype)
        vals_ref[...] = jnp.zeros(vals_ref.shape, vals_ref.dtype)

        def block(b, carry):
            cols = pl.ds(pl.multiple_of(b * MOE_BLOCK, MOE_BLOCK), MOE_BLOCK)
            bases = [_win_base(meta_ref, e, b, cap) for e in range(N_EXPERTS)]
            hits = [_win_hits(pos_ref[e:e + 1, cols], bases[e]) for e in range(N_EXPERTS)]
            part = _dot(_one_hot(hits), h2_ref[cols, :])
            for e in range(N_EXPERTS):
                rows = pl.ds(bases[e], MOE_WIN)
                xg_ref[e, rows, :] += part[e * MOE_WIN:(e + 1) * MOE_WIN, :].astype(xg_ref.dtype)
                vals = jnp.sum(jnp.where(hits[e], aff_ref[e:e + 1, cols], 0.0), axis=1, keepdims=True)
                vals_ref[e, rows, :] += jnp.broadcast_to(vals, (MOE_WIN, LANES))
            return carry

        lax.fori_loop(0, n // MOE_BLOCK, block, 0)

    @pl.when(jnp.logical_not(fits))
    def _():
        _gather_dense(pos_ref, aff_ref, h2_ref, xg_ref, vals_ref, slice(0, cap), cap, 1)


def _gather_win(meta, pos3, aff3, h23, *, cap):
    nseq, _, n = pos3.shape
    return pl.pallas_call(
        functools.partial(_gather_win_kernel, n=n, cap=cap),
        grid=(nseq,),
        in_specs=[
            pl.BlockSpec((N_EXPERTS, LANES), lambda s: (s, 0), memory_space=pltpu.SMEM),
            pl.BlockSpec((None, N_EXPERTS, n), lambda s: (s, 0, 0)),
            pl.BlockSpec((None, N_EXPERTS, n), lambda s: (s, 0, 0)),
            pl.BlockSpec((None, n, D_MODEL), lambda s: (s, 0, 0)),
        ],
        out_specs=(
            pl.BlockSpec((N_EXPERTS, cap, D_MODEL), lambda s: (0, s, 0)),
            pl.BlockSpec((N_EXPERTS, cap, LANES), lambda s: (0, s, 0)),
        ),
        out_shape=(
            jax.ShapeDtypeStruct((N_EXPERTS, nseq * cap, D_MODEL), bf16),
            jax.ShapeDtypeStruct((N_EXPERTS, nseq * cap, LANES), f32),
        ),
        compiler_params=_params(VMEM_LARGE, "parallel"),
        name="gather_win",
    )(meta, pos3, aff3, h23)


def _experts_kernel(xp_ref, vp_ref, xs_ref, vs_ref, wg_ref, wu_ref, wd_ref, yp_ref, ys_ref, *, rc):
    wg = wg_ref[...].astype(bf16)
    wu = wu_ref[...].astype(bf16)
    wd = wd_ref[...].astype(bf16)
    for x_ref, v_ref, y_ref in ((xp_ref, vp_ref, yp_ref), (xs_ref, vs_ref, ys_ref)):
        for r0 in range(0, x_ref.shape[0], rc):
            x = x_ref[r0:r0 + rc, :]
            a = _dot(x, wg)
            u = _dot(x, wu)
            hm = (a * jax.nn.sigmoid(a) * u).astype(bf16)
            y = _dot(hm, wd) * v_ref[r0:r0 + rc, 0:1]
            y_ref[r0:r0 + rc, :] = y.astype(y_ref.dtype)


def _experts(xg_p, vals_p, xg_s, vals_s, wg, wu, wd, *, rc=EXPERT_ROWS):
    rp = xg_p.shape[1]
    rs = xg_s.shape[1]
    per_e = lambda r, w: pl.BlockSpec((None, r, w), lambda e: (e, 0, 0))
    return pl.pallas_call(
        functools.partial(_experts_kernel, rc=rc),
        grid=(N_EXPERTS,),
        in_specs=[
            per_e(rp, D_MODEL), per_e(rp, LANES), per_e(rs, D_MODEL), per_e(rs, LANES),
            per_e(D_MODEL, EXPERT_FF), per_e(D_MODEL, EXPERT_FF), per_e(EXPERT_FF, D_MODEL),
        ],
        out_specs=(per_e(rp, D_MODEL), per_e(rs, D_MODEL)),
        out_shape=(
            jax.ShapeDtypeStruct((N_EXPERTS, rp, D_MODEL), bf16),
            jax.ShapeDtypeStruct((N_EXPERTS, rs, D_MODEL), bf16),
        ),
        compiler_params=_params(VMEM_LARGE, "parallel"),
        name="experts",
    )(xg_p, vals_p, xg_s, vals_s, wg, wu, wd)


def _scatter_dense(pos_ref, y_ref, rows, tn, cap, ne):
    moe = jnp.zeros((tn, D_MODEL), f32)
    for e0 in range(0, N_EXPERTS, ne):
        onehot = _one_hot([_slot_hits(pos_ref, e0 + e, cap) for e in range(ne)])
        y = jnp.concatenate([y_ref[e0 + e, rows, :] for e in range(ne)], axis=0)
        moe = moe + lax.dot_general(onehot, y, (((0,), (0,)), ((), ())), preferred_element_type=f32)
    return moe


def _scatter_kernel(pos_ref, y_ref, x1_ref, m_ref, fn_ref, o_ref, *, tn, cap, ne, sb):
    g2 = m_ref[5:6, :]
    for g in range(sb):
        moe = _scatter_dense(pos_ref.at[g], y_ref, slice(g * cap, (g + 1) * cap), tn, cap, ne)
        o_ref[g] = _rms(x1_ref[g] + g2 * moe, fn_ref[...])


def _scatter(pos3, y, x13, m3, fn, *, cap, ne, tn, sb):
    nseq, _, n = pos3.shape
    assert sb == 1 or m3.shape[0] == 1, "sequences sharing a grid step must share their modulation rows"
    return pl.pallas_call(
        functools.partial(_scatter_kernel, tn=tn, cap=cap, ne=ne, sb=sb),
        grid=(nseq // sb, n // tn),
        in_specs=[
            pl.BlockSpec((sb, N_EXPERTS, tn), lambda s, i: (s, 0, i)),
            pl.BlockSpec((N_EXPERTS, sb * cap, D_MODEL), lambda s, i: (0, s, 0)),
            pl.BlockSpec((sb, tn, D_MODEL), lambda s, i: (s, i, 0)),
            _mod_spec(m3, lambda s, i: s),
            _const_spec((1, D_MODEL)),
        ],
        out_specs=pl.BlockSpec((sb, tn, D_MODEL), lambda s, i: (s, i, 0)),
        out_shape=jax.ShapeDtypeStruct((nseq, n, D_MODEL), f32),
        compiler_params=_params(VMEM_MID, "parallel", "arbitrary"),
        name="scatter",
    )(pos3, y, x13, m3, fn)


def _scatter_win_kernel(meta_ref, pos_ref, y_ref, x1_ref, m_ref, fn_ref, o_ref, moe_scr, *, cap):
    b = pl.program_id(1)
    fits = _windows_fit(meta_ref)

    @pl.when(fits)
    def _():
        bases = [_win_base(meta_ref, e, b, cap) for e in range(N_EXPERTS)]
        onehot = _one_hot([_win_hits(pos_ref[e:e + 1, :], bases[e]) for e in range(N_EXPERTS)])
        y = jnp.concatenate([y_ref[e, pl.ds(bases[e], MOE_WIN), :] for e in range(N_EXPERTS)], axis=0)
        moe_scr[...] = lax.dot_general(onehot, y, (((0,), (0,)), ((), ())), preferred_element_type=f32)

    @pl.when(jnp.logical_not(fits))
    def _():
        moe_scr[...] = _scatter_dense(pos_ref, y_ref, slice(0, cap), MOE_BLOCK, cap, 1)

    o_ref[...] = _rms(x1_ref[...] + m_ref[5:6, :] * moe_scr[...], fn_ref[...])


def _scatter_win(meta, pos3, y, x13, m3, fn, *, cap):
    nseq, _, n = pos3.shape
    tn = MOE_BLOCK
    return pl.pallas_call(
        functools.partial(_scatter_win_kernel, cap=cap),
        grid=(nseq, n // tn),
        in_specs=[
            pl.BlockSpec((N_EXPERTS, LANES), lambda s, i: (s, 0), memory_space=pltpu.SMEM),
            pl.BlockSpec((None, N_EXPERTS, tn), lambda s, i: (s, 0, i)),
            pl.BlockSpec((N_EXPERTS, cap, D_MODEL), lambda s, i: (0, s, 0)),
            pl.BlockSpec((None, tn, D_MODEL), lambda s, i: (s, i, 0)),
            _mod_spec(m3, lambda s, i: s),
            _const_spec((1, D_MODEL)),
        ],
        out_specs=pl.BlockSpec((None, tn, D_MODEL), lambda s, i: (s, i, 0)),
        out_shape=jax.ShapeDtypeStruct((nseq, n, D_MODEL), f32),
        scratch_shapes=[pltpu.VMEM((tn, D_MODEL), f32)],
        compiler_params=_params(VMEM_MID, "parallel", "arbitrary"),
        name="scatter_win",
    )(meta, pos3, y, x13, m3, fn)


def _rope_tables(n):
    t = np.arange(n)
    half = QK_ROPE // 2
    freqs = ROPE_BASE ** (-np.arange(0, half, 2, dtype=np.float64) / half)
    ang_r = (t // GRID_W)[:, None] * freqs
    ang_c = (t % GRID_W)[:, None] * freqs
    cr, sr, cc, sc = np.cos(ang_r), np.sin(ang_r), np.cos(ang_c), np.sin(ang_c)
    cos = np.ones((n, HEAD_W))
    sin = np.zeros((n, HEAD_W))
    cos[:, ROPE_OFF:ROPE_OFF + QK_ROPE] = np.concatenate([cr, cr, cc, cc], axis=-1)
    sin[:, ROPE_OFF:ROPE_OFF + QK_ROPE] = np.concatenate([-sr, sr, -sc, sc], axis=-1)
    return jnp.asarray(cos, f32), jnp.asarray(sin, f32)


_PARTNER = np.concatenate([np.arange(8, 16), np.arange(0, 8), np.arange(24, 32), np.arange(16, 24)])


def _rope_partner(w):
    q = QK_ROPE // 4
    return jnp.concatenate([w[..., q:2 * q], w[..., 0:q], w[..., 3 * q:4 * q], w[..., 2 * q:3 * q]], axis=-1)


def _rope_placement():
    place = np.zeros((LANES, 2 * LANES), np.float32)
    d = np.arange(QK_ROPE)
    place[d, ROPE_OFF + d] = 1.0
    place[_PARTNER, LANES + ROPE_OFF + d] = 1.0
    return jnp.asarray(place, bf16)


def _head_blocks(w_nope, w_rope):
    rows = w_nope.shape[0]
    if w_rope is None:
        w_rope = jnp.zeros((rows, N_HEADS, QK_ROPE), w_nope.dtype)
    z = jnp.zeros((rows, N_HEADS, HEAD_W - QK_NOPE - QK_ROPE), w_nope.dtype)
    return jnp.concatenate([w_nope, w_rope, z], axis=-1).reshape(rows, N_HEADS * HEAD_W)


def kernel(x_prompt, x_sample, cache_ckv, cache_kpe, c, c_ctx, w_ada, b_ada, norm1, w_in, conv_dw, conv_dw_b,
           conv_ln_g, conv_ln_b, w_conv_out, q_norm, w_qb, kv_norm, w_kvb, w_o_mla, w_out, norm2, w_router,
           w_e_gate, w_e_up, w_e_down, final_norm):
    assert w_ada.shape[0] == 1, "single trunk layer"
    nb_p, n_p, _ = x_prompt.shape
    nb_s, n_s, _ = x_sample.shape

    win = _wprep(w_in[0].T, _rope_placement())
    wq = w_qb[0].reshape(Q_LORA, N_HEADS, QK_NOPE + QK_ROPE)
    wqb = _head_blocks(wq[..., :QK_NOPE], wq[..., QK_NOPE:]).astype(bf16)
    wqbp = _head_blocks(jnp.zeros_like(wq[..., :QK_NOPE]), _rope_partner(wq[..., QK_NOPE:])).astype(bf16)
    wkv = w_kvb[0].reshape(KV_LORA, N_HEADS, QK_NOPE + V_DIM)
    wk = _head_blocks(wkv[..., :QK_NOPE], None).astype(bf16)
    wv = wkv[..., QK_NOPE:].reshape(KV_LORA, N_HEADS * V_DIM).astype(bf16)
    wco = w_conv_out[0].astype(bf16)
    wo = w_o_mla[0].astype(bf16)
    wout = w_out[0].astype(bf16)
    wr_hi = w_router[0].astype(bf16)
    wr_lo = (w_router[0] - wr_hi.astype(f32)).astype(bf16)
    wr = jnp.concatenate([wr_hi, wr_lo, jnp.zeros((D_MODEL, LANES - 2 * N_EXPERTS), bf16)], axis=-1)
    row = lambda a: a.reshape(1, -1)

    mod = jnp.concatenate([c_ctx[None, :], c, jnp.zeros((8 - 1 - nb_s, D_MODEL), f32)], axis=0)
    m = _ada(mod, w_ada[0], b_ada[0]).reshape(8, 6, D_MODEL)
    m_p, m_s = m[0:1], m[1:1 + nb_s]
    cos, sin = _rope_tables(n_s)

    conv_w = _conv_weights(conv_dw[0], row(conv_dw_b[0]), row(conv_ln_g[0]), row(conv_ln_b[0]), wco, CONV_CHUNK)

    def mixers(x, m3, rope, ctx_ckv, ctx_kpe):
        nseq, n, _ = x.shape
        x2d = x.reshape(nseq * n, D_MODEL)
        cm, q, ckv, kr, g, *kpe = _inproj(x2d, m3, row(norm1[0]), win, row(q_norm[0]), wqb, wqbp, row(kv_norm[0]),
                                          cos, sin, conv_w, n=n, rope=rope, tm=IN_TILE)
        keys_ckv = ckv.reshape(nseq, n, KV_LORA)
        keys_kpe = kr.reshape(nseq, n, LANES)
        if ctx_ckv is not None:
            keys_ckv = jnp.concatenate([ctx_ckv.astype(keys_ckv.dtype), keys_ckv], axis=1)
            keys_kpe = jnp.concatenate([ctx_kpe.astype(keys_kpe.dtype), keys_kpe], axis=1)
        x1, h2, lg = _attn(q, keys_ckv, keys_kpe, cm.reshape(nseq * n, D_MODEL), g, x2d, m3, wk, wv, wo, wout,
                           row(norm2[0]), wr, n=n, tq=min(n, Q_TILE))
        return x1, h2, lg, ckv, kpe

    ctx_kpe = jnp.pad(cache_kpe[:, 0], ((0, 0), (0, 0), (ROPE_OFF, LANES - ROPE_OFF - QK_ROPE)))
    x1_p, h2_p, lg_p, ckv_p, (kpe_p,) = mixers(x_prompt, m_p, False, None, None)
    x1_s, h2_s, lg_s, _, _ = mixers(x_sample, m_s, True, cache_ckv[:, 0], ctx_kpe)

    def moe_tiles(n):
        cap = EC_FACTOR * n // N_EXPERTS
        ne = N_EXPERTS if N_EXPERTS * cap <= MOE_ROWS else 1
        sb = max(1, MOE_TOKENS // n)
        return cap, ne, sb

    def windowed(n):
        return n >= 4 * MOE_BLOCK and EC_FACTOR * n // N_EXPERTS >= 2 * MOE_WIN

    def route_gather(h2, lg, nseq, n):
        cap, ne, sb = moe_tiles(n)
        pos, aff, meta = _route(lg.reshape(nseq, n, LANES), cap=cap)
        pos3 = pos.reshape(nseq, N_EXPERTS, n)
        aff3 = aff.reshape(nseq, N_EXPERTS, n)
        h23 = h2.reshape(nseq, n, D_MODEL)
        if windowed(n):
            xg, vals = _gather_win(meta, pos3, aff3, h23, cap=cap)
        else:
            xg, vals = _gather(pos3, aff3, h23, cap=cap, ne=ne, sb=sb)
        return pos3, meta, xg, vals

    pos_p, meta_p, xg_p, vals_p = route_gather(h2_p, lg_p, nb_p, n_p)
    pos_s, meta_s, xg_s, vals_s = route_gather(h2_s, lg_s, nb_s, n_s)
    y_p, y_s = _experts(xg_p, vals_p, xg_s, vals_s, w_e_gate[0], w_e_up[0], w_e_down[0])
    fn = row(final_norm)

    def scatter(pos, meta, y, x1, m3, nseq, n):
        cap, ne, sb = moe_tiles(n)
        x13 = x1.reshape(nseq, n, D_MODEL)
        if windowed(n):
            return _scatter_win(meta, pos, y, x13, m3, fn, cap=cap)
        return _scatter(pos, y, x13, m3, fn, cap=cap, ne=ne, tn=min(n, SCATTER_TILE), sb=sb)

    y_prompt = scatter(pos_p, meta_p, y_p, x1_p, m_p, nb_p, n_p)
    y_sample = scatter(pos_s, meta_s, y_s, x1_s, m_s, nb_s, n_s)

    new_ckv = ckv_p.reshape(nb_p, 1, n_p, KV_LORA)
    new_kpe = kpe_p.reshape(nb_p, 1, n_p, QK_ROPE)
    return (y_prompt, y_sample, new_ckv, new_kpe)
```

```python
import functools

import jax
import jax.numpy as jnp
import numpy as np
from jax import lax
from jax.experimental import pallas as pl
from jax.experimental.pallas import tpu as pltpu

D_MODEL = 1024
GRID_W = 64
CONV_DIM = 512
CONV_WIDTH = 31
N_HEADS = 8
QK_NOPE = 64
QK_ROPE = 32
V_DIM = 64
Q_LORA = 256
KV_LORA = 128
N_EXPERTS = 16
EXPERT_FF = 512
EC_FACTOR = 2
ROPE_BASE = 10000.0
EPS = 1e-6

LANES = 128
HEAD_W = LANES
ROPE_OFF = QK_NOPE
CONV_HALO = 16
LOG2E = 1.4426950408889634
VMEM_LIMIT = 48 * 1024 * 1024
IN_TILE = 512
CONV_CHUNK = 64
CONV_OUT_ROWS = 256
GATE_CHUNK = 512
EXPERT_ROWS = 512
Q_TILE = 512
ATTN_TOKENS = 1024
SCORE_BYTES = 12 * 1024 * 1024
MOE_ROWS = 512
MOE_TOKENS = 1024
SCATTER_TILE = 512
MOE_BLOCK = 2 * LANES
MOE_WIN = 80
WIN_ALIGN = 16

C_CONV = 0
C_QA = 2 * CONV_DIM
C_KVA = C_QA + Q_LORA
C_KR = C_KVA + KV_LORA
C_GATE = C_KR + LANES
C_END = C_GATE + 2 * D_MODEL
C_KRP = C_END

f32 = jnp.float32
bf16 = jnp.bfloat16


MIB = 1024 * 1024
VMEM_SMALL, VMEM_MID, VMEM_LARGE = 16 * MIB, 32 * MIB, 40 * MIB


def _params(vmem_bytes, *sem):
    assert vmem_bytes <= VMEM_LIMIT
    return pltpu.CompilerParams(dimension_semantics=sem, vmem_limit_bytes=vmem_bytes)


def _dot(a, b):
    return jnp.dot(a, b, preferred_element_type=f32)


def _rms(x, g):
    return x * lax.rsqrt(jnp.mean(x * x, axis=-1, keepdims=True) + EPS) * g


def _const_spec(shape):
    nd = len(shape)
    return pl.BlockSpec(shape, lambda *_: (0,) * nd)


def _mod_spec(m3, seq_of):
    if m3.shape[0] == 1:
        return _const_spec((None, 6, D_MODEL))
    return pl.BlockSpec((None, 6, D_MODEL), lambda *idx: (seq_of(*idx), 0, 0))


def _ada_kernel(s_ref, w_ref, b_ref, o_ref):
    s = s_ref[...]
    s = s * jax.nn.sigmoid(s)
    o_ref[...] = _dot(s.astype(bf16), w_ref[...].astype(bf16)) + b_ref[...]


def _ada(mod, w_ada, b_ada):
    rows = mod.shape[0]
    n_out = w_ada.shape[1]
    tn = D_MODEL
    return pl.pallas_call(
        _ada_kernel,
        grid=(n_out // tn,),
        in_specs=[
            _const_spec((rows, D_MODEL)),
            pl.BlockSpec((D_MODEL, tn), lambda j: (0, j)),
            pl.BlockSpec((1, tn), lambda j: (0, j)),
        ],
        out_specs=pl.BlockSpec((rows, tn), lambda j: (0, j)),
        out_shape=jax.ShapeDtypeStruct((rows, n_out), f32),
        compiler_params=_params(VMEM_LIMIT, "arbitrary"),
        name="ada",
    )(mod, w_ada, b_ada.reshape(1, n_out))


def _wprep_kernel(wt_ref, place_ref, o_ref):
    def block(r0):
        return wt_ref[r0:r0 + LANES, :].T.astype(bf16)

    for j in range(C_KR // LANES):
        o_ref[:, j * LANES:(j + 1) * LANES] = block(j * LANES)
    placed = _dot(block(C_KR), place_ref[...])
    o_ref[:, C_KR:C_GATE] = placed[:, 0:LANES].astype(bf16)
    o_ref[:, C_KRP:C_KRP + LANES] = placed[:, LANES:2 * LANES].astype(bf16)
    for j in range(2 * D_MODEL // LANES):
        o_ref[:, C_GATE + j * LANES:C_GATE + (j + 1) * LANES] = block(C_KR + QK_ROPE + j * LANES)


def _wprep(w_in_t, place):
    cols, rows = w_in_t.shape
    return pl.pallas_call(
        _wprep_kernel,
        grid=(rows // LANES,),
        in_specs=[pl.BlockSpec((cols, LANES), lambda i: (0, i)), _const_spec(place.shape)],
        out_specs=pl.BlockSpec((LANES, C_KRP + LANES), lambda i: (i, 0)),
        out_shape=jax.ShapeDtypeStruct((rows, C_KRP + LANES), bf16),
        compiler_params=_params(VMEM_SMALL, "parallel"),
        name="wprep",
    )(w_in_t, place)


def _conv_taps(vpad, r0, shift_ref, dw_ref, dwb_ref, ybuf, y0, ct):
    pad = CONV_WIDTH // 2
    sub = 8
    span = ((CONV_HALO - pad + CONV_WIDTH - 1) // sub) * sub
    cw = 2 * LANES
    for cb in range(CONV_DIM // cw):
        sl = slice(cb * cw, (cb + 1) * cw)
        win = vpad[pl.ds(r0, ct + 2 * CONV_HALO), sl]
        acc = jnp.zeros((ct // sub, sub, cw), f32)
        for ph in range(sub):
            wph = win.astype(f32) if ph == 0 else _dot(shift_ref[ph - 1], win)
            for a in range(span // sub + 1):
                k = a * sub + ph - (CONV_HALO - pad)
                if 0 <= k < CONV_WIDTH:
                    acc = acc + wph[a * sub:a * sub + ct, :].reshape(ct // sub, sub, cw) * dw_ref[k, :, sl][None]
        ybuf[pl.ds(y0, ct), sl] = acc.reshape(ct, cw) + dwb_ref[:, sl]


def _conv_out(y, lng_ref, lnb_ref, wco_ref, gate):
    mu = jnp.mean(y, axis=-1, keepdims=True)
    yc = y - mu
    var = jnp.mean(yc * yc, axis=-1, keepdims=True)
    z = yc * lax.rsqrt(var + EPS) * lng_ref[...] + lnb_ref[...]
    z = z * jax.nn.sigmoid(z)
    return (gate.astype(f32) * _dot(z.astype(bf16), wco_ref[...])).astype(bf16)


def _fill_padded(vpad, v, n):
    zeros = jnp.zeros((CONV_HALO, CONV_DIM), vpad.dtype)
    vpad[0:CONV_HALO, :] = zeros
    vpad[CONV_HALO + n:2 * CONV_HALO + n, :] = zeros
    vpad[CONV_HALO:CONV_HALO + n, :] = v


def _inproj_kernel(*refs, rope, halo, tiles_per_seq, q_scale, n, rt, ct):
    refs = list(refs)
    x_ref = refs.pop(0)
    if halo:
        xp_ref, xn_ref = refs[:2]
        del refs[:2]
    m_ref, n1_ref, win_ref, qn_ref, wqb_ref = refs[:5]
    del refs[:5]
    if rope:
        wqbp_ref = refs.pop(0)
    kvn_ref = refs.pop(0)
    if rope:
        cos_ref, sin_ref = refs[:2]
        del refs[:2]
    shift_ref, dw_ref, dwb_ref, lng_ref, lnb_ref, wco_ref = refs[:6]
    del refs[:6]
    cm_ref, q_ref, ckv_ref, kr_ref, g_ref = refs[:5]
    del refs[:5]
    if not rope:
        kpe_ref = refs.pop(0)
    vpad, ybuf, gc = refs
    tm = x_ref.shape[0]
    lo = CONV_HALO if halo else 0

    sh1 = m_ref[0:1, :]
    sc1 = m_ref[1:2, :]
    x = jnp.concatenate([xp_ref[...], x_ref[...], xn_ref[...]], axis=0) if halo else x_ref[...]
    hb_all = (_rms(x, n1_ref[...]) * (1.0 + sc1) + sh1).astype(bf16)

    left_all = _dot(hb_all, win_ref[:, 0:C_GATE])
    v = (left_all[:, C_CONV:C_CONV + CONV_DIM] * jax.nn.sigmoid(left_all[:, C_CONV + CONV_DIM:C_QA])).astype(bf16)
    hb = hb_all[lo:lo + tm, :]
    left = left_all[lo:lo + tm, :]

    if halo:
        j = pl.program_id(0) % tiles_per_seq
        zeros = jnp.zeros((CONV_HALO, CONV_DIM), bf16)
        vpad[0, 0:lo, :] = jnp.where(j > 0, v[0:lo, :], zeros)
        vpad[0, lo:lo + tm, :] = v[lo:lo + tm, :]
        vpad[0, lo + tm:2 * lo + tm, :] = jnp.where(j < tiles_per_seq - 1, v[lo + tm:2 * lo + tm, :], zeros)
        chunks = [(0, c * ct, c * ct) for c in range(tm // ct)]
    else:
        for s in range(tm // n):
            _fill_padded(vpad.at[s], v[s * n:(s + 1) * n, :], n)
        chunks = [(s, c * ct, s * n + c * ct) for s in range(tm // n) for c in range(n // ct)]

    def conv_some(count):
        for _ in range(min(count, len(chunks))):
            s, r0, y0 = chunks.pop(0)
            _conv_taps(vpad.at[s], r0, shift_ref, dw_ref, dwb_ref, ybuf, y0, ct)

    steps = 2 + 2 * D_MODEL // GATE_CHUNK
    per_step = -(-len(chunks) // steps)

    qa = left[:, C_QA:C_KVA]
    qn = _rms(qa, qn_ref[...]).astype(bf16)
    q = _dot(qn, wqb_ref[...])
    if rope:
        qp = _dot(qn, wqbp_ref[...])
        cos = cos_ref[...]
        sin = sin_ref[...]
        for hd in range(N_HEADS):
            sl = slice(hd * HEAD_W, (hd + 1) * HEAD_W)
            q_ref[:, sl] = ((q[:, sl] * cos + qp[:, sl] * sin) * q_scale).astype(q_ref.dtype)
    else:
        q_ref[...] = (q * q_scale).astype(q_ref.dtype)
    conv_some(per_step)

    kva = left[:, C_KVA:C_KR]
    ckv_ref[...] = _rms(kva, kvn_ref[...]).astype(ckv_ref.dtype)

    kr = left[:, C_KR:C_GATE]
    if rope:
        krp = _dot(hb, win_ref[:, C_KRP:C_KRP + LANES])
        kr = kr * cos_ref[...] + krp * sin_ref[...]
    else:
        kpe_ref[...] = kr[:, ROPE_OFF:ROPE_OFF + QK_ROPE]
    kr_ref[...] = kr.astype(kr_ref.dtype)
    conv_some(per_step)

    gw = GATE_CHUNK
    for jg in range(2 * D_MODEL // gw):
        gate = jax.nn.sigmoid(_dot(hb, win_ref[:, C_GATE + jg * gw:C_GATE + (jg + 1) * gw])).astype(bf16)
        if jg * gw < D_MODEL:
            gc[:, jg * gw:(jg + 1) * gw] = gate
        else:
            g_ref[:, jg * gw - D_MODEL:(jg + 1) * gw - D_MODEL] = gate
        conv_some(per_step)

    conv_some(len(chunks))
    for r0 in range(0, tm, rt):
        cm_ref[r0:r0 + rt, :] = _conv_out(ybuf[r0:r0 + rt, :], lng_ref, lnb_ref, wco_ref, gc[r0:r0 + rt, :])


def _inproj(x2d, m3, norm1, win, q_norm, wqb, wqbp, kv_norm, cos, sin, conv_w, *, n, rope, tm,
            rt=CONV_OUT_ROWS, ct=CONV_CHUNK):
    tokens = x2d.shape[0]
    halo = n > tm
    assert n % tm == 0 if halo else (tm % n == 0 and m3.shape[0] == 1 and not rope)
    tiles_per_seq = max(1, n // tm)
    q_scale = float((QK_NOPE + QK_ROPE) ** -0.5 * LOG2E)
    tile = lambda w: pl.BlockSpec((tm, w), lambda i: (i, 0))
    in_specs = [tile(D_MODEL)]
    args = [x2d]
    if halo:
        hb_tile = tm // CONV_HALO
        hb_seq = n // CONV_HALO
        prev = lambda i: (jnp.maximum(i * hb_tile - 1, (i // tiles_per_seq) * hb_seq), 0)
        nxt = lambda i: (jnp.minimum((i + 1) * hb_tile, (i // tiles_per_seq + 1) * hb_seq - 1), 0)
        in_specs += [pl.BlockSpec((CONV_HALO, D_MODEL), prev), pl.BlockSpec((CONV_HALO, D_MODEL), nxt)]
        args += [x2d, x2d]
    in_specs += [
        _mod_spec(m3, lambda i: i // tiles_per_seq),
        _const_spec((1, D_MODEL)),
        _const_spec((D_MODEL, C_KRP + LANES if rope else C_END)),
        _const_spec((1, Q_LORA)),
        _const_spec(wqb.shape),
    ]
    args += [m3, norm1, win, q_norm, wqb]
    if rope:
        in_specs.append(_const_spec(wqbp.shape))
        args.append(wqbp)
    in_specs.append(_const_spec((1, KV_LORA)))
    args.append(kv_norm)
    if rope:
        tab = pl.BlockSpec((tm, LANES), lambda i: (i % tiles_per_seq, 0))
        in_specs += [tab, tab]
        args += [cos, sin]
    args += list(conv_w)
    in_specs += [_const_spec(w.shape) for w in conv_w]
    pieces = max(1, tm // n)
    scratch = [
        pltpu.VMEM((pieces, tm // pieces + 2 * CONV_HALO, CONV_DIM), bf16),
        pltpu.VMEM((tm, CONV_DIM), f32),
        pltpu.VMEM((tm, D_MODEL), bf16),
    ]
    out_shape = [
        jax.ShapeDtypeStruct((tokens, D_MODEL), bf16),
        jax.ShapeDtypeStruct((tokens, N_HEADS * HEAD_W), bf16),
        jax.ShapeDtypeStruct((tokens, KV_LORA), bf16 if rope else f32),
        jax.ShapeDtypeStruct((tokens, LANES), bf16),
        jax.ShapeDtypeStruct((tokens, D_MODEL), bf16),
    ]
    out_specs = [tile(D_MODEL), tile(N_HEADS * HEAD_W), tile(KV_LORA), tile(LANES), tile(D_MODEL)]
    if not rope:
        out_shape.append(jax.ShapeDtypeStruct((tokens, QK_ROPE), f32))
        out_specs.append(tile(QK_ROPE))
    return pl.pallas_call(
        functools.partial(_inproj_kernel, rope=rope, halo=halo, tiles_per_seq=tiles_per_seq, q_scale=q_scale,
                          n=n, rt=rt, ct=ct),
        grid=(tokens // tm,),
        in_specs=in_specs,
        out_specs=out_specs,
        out_shape=out_shape,
        scratch_shapes=scratch,
        compiler_params=_params(VMEM_MID, "parallel"),
        name="inproj_conv_rope" if rope else "inproj_conv",
    )(*args)


def _conv_weights(dw, dwb, lng, lnb, wco, ct):
    rows = ct + 2 * CONV_HALO
    i = np.arange(rows)
    shifts = jnp.asarray(np.stack([(i[None, :] == i[:, None] + ph) for ph in range(1, 8)]), bf16)
    dw_tiles = jnp.broadcast_to(dw[:, None, :], (CONV_WIDTH, 8, CONV_DIM))
    return shifts, dw_tiles, dwb, lng, lnb, wco


def _attn_kernel(q_ref, ckv_ref, kpe_ref, cm_ref, g_ref, x_ref, m_ref, wk_ref, wv_ref, wo_ref, wout_ref, n2_ref,
                 wr_ref, x1_ref, h2_ref, lg_ref, k_scr, v_scr, q_scr, *, sb, hg):
    s_len = ckv_ref.shape[1]

    @pl.when(pl.program_id(1) == 0)
    def _():
        ckv = ckv_ref[...].reshape(sb * s_len, KV_LORA).astype(bf16)
        kpe = kpe_ref[...].reshape(sb * s_len, LANES).astype(f32)
        k = _dot(ckv, wk_ref[...])
        v = _dot(ckv, wv_ref[...])
        ones = jnp.ones((s_len, V_DIM), bf16)
        for g in range(sb):
            rows = slice(g * s_len, (g + 1) * s_len)
            for hd in range(N_HEADS):
                u = g * N_HEADS + hd
                k_scr[u] = (k[rows, hd * HEAD_W:(hd + 1) * HEAD_W] + kpe[rows, :]).astype(bf16)
                vh = v[rows, hd * V_DIM:(hd + 1) * V_DIM].astype(bf16)
                if hd % 2 == 0:
                    v_scr[u, :, 0:V_DIM] = vh
                    v_scr[u, :, V_DIM:LANES] = ones
                else:
                    v_scr[u, :, 0:V_DIM] = ones
                    v_scr[u, :, V_DIM:LANES] = vh

    tq = q_ref.shape[0] // sb
    units = sb * N_HEADS
    for g in range(sb):
        for hd in range(N_HEADS):
            q_scr[g * N_HEADS + hd] = q_ref[g * tq:(g + 1) * tq, hd * HEAD_W:(hd + 1) * HEAD_W]
    lane = lax.broadcasted_iota(jnp.int32, (tq, LANES), 1)
    pairs = []
    for u0 in range(0, units, hg):
        us = slice(u0, u0 + hg)
        s = lax.dot_general(q_scr[us], k_scr[us], (((2,), (2,)), ((0,), (0,))), preferred_element_type=f32)
        mx = jnp.max(s, axis=-1, keepdims=True)
        p = jnp.exp2(s - mx).astype(bf16)
        r = lax.dot_general(p, v_scr[us], (((2,), (1,)), ((0,), (0,))), preferred_element_type=f32)
        for j in range(hg // 2):
            re, ro = r[2 * j], r[2 * j + 1]
            oe = re * (1.0 / re[:, V_DIM:V_DIM + 1])
            oo = ro * (1.0 / ro[:, 0:1])
            pairs.append(jnp.where(lane < V_DIM, oe, oo))
    per_seq = N_HEADS // 2
    attn = jnp.concatenate(
        [jnp.concatenate(pairs[g * per_seq:(g + 1) * per_seq], axis=-1) for g in range(sb)], axis=0).astype(bf16)
    merged = (cm_ref[...].astype(f32) + g_ref[...].astype(f32) * _dot(attn, wo_ref[...])).astype(bf16)

    g1 = m_ref[2:3, :]
    sh2 = m_ref[3:4, :]
    sc2 = m_ref[4:5, :]
    x1 = x_ref[...] + g1 * _dot(merged, wout_ref[...])
    x1_ref[...] = x1
    h2 = _rms(x1, n2_ref[...]) * (1.0 + sc2) + sh2
    hi = h2.astype(bf16)
    h2_ref[...] = hi
    lo = (h2 - hi.astype(f32)).astype(bf16)
    both = _dot(jnp.concatenate([hi, lo], axis=0), wr_ref[...])
    rows = hi.shape[0]
    s = both[0:rows] + both[rows:2 * rows]
    lg_ref[...] = s + pltpu.roll(s, LANES - N_EXPERTS, axis=1)


def _attn(q2d, ckv3, kpe3, cm2d, g2d, x2d, m3, wk, wv, wo, wout, norm2, wr, *, n, tq):
    nseq, s_len, _ = ckv3.shape
    sb = max(1, ATTN_TOKENS // n) if (tq == n and m3.shape[0] == 1) else 1
    units = sb * N_HEADS
    hg = max(2, min(units, SCORE_BYTES // (tq * s_len * 4)))
    assert units % hg == 0 and hg % 2 == 0 and nseq % sb == 0
    qb = n // tq
    tokens = nseq * n
    tile = lambda w: pl.BlockSpec((sb * tq, w), lambda s, i: (s * qb + i, 0))
    return pl.pallas_call(
        functools.partial(_attn_kernel, sb=sb, hg=hg),
        grid=(nseq // sb, qb),
        in_specs=[
            tile(N_HEADS * HEAD_W),
            pl.BlockSpec((sb, s_len, KV_LORA), lambda s, i: (s, 0, 0)),
            pl.BlockSpec((sb, s_len, LANES), lambda s, i: (s, 0, 0)),
            tile(D_MODEL),
            tile(D_MODEL),
            tile(D_MODEL),
            _mod_spec(m3, lambda s, i: s),
            _const_spec(wk.shape),
            _const_spec(wv.shape),
            _const_spec(wo.shape),
            _const_spec(wout.shape),
            _const_spec((1, D_MODEL)),
            _const_spec(wr.shape),
        ],
        out_specs=(tile(D_MODEL), tile(D_MODEL), tile(LANES)),
        out_shape=(
            jax.ShapeDtypeStruct((tokens, D_MODEL), f32),
            jax.ShapeDtypeStruct((tokens, D_MODEL), bf16),
            jax.ShapeDtypeStruct((tokens, LANES), f32),
        ),
        scratch_shapes=[
            pltpu.VMEM((units, s_len, HEAD_W), bf16),
            pltpu.VMEM((units, s_len, LANES), bf16),
            pltpu.VMEM((units, tq, HEAD_W), bf16),
        ],
        compiler_params=_params(VMEM_LIMIT, "parallel", "arbitrary"),
        name="attn_out",
    )(q2d, ckv3, kpe3, cm2d, g2d, x2d, m3, wk, wv, wo, wout, norm2, wr)


def _route_kernel(lg_ref, pos_ref, aff_ref, meta_ref, *, nseq, n, cap):
    for s in range(nseq):
        lt = lg_ref[s].T[0:N_EXPERTS, :]
        e = jnp.exp(lt - jnp.max(lt, axis=0, keepdims=True))
        aff_ref[s * N_EXPERTS:(s + 1) * N_EXPERTS, :] = e / jnp.sum(e, axis=0, keepdims=True)
    rows = nseq * N_EXPERTS
    capf = float(cap)

    def bit_step(i, t):
        cand = t | (jnp.int32(1) << (30 - i))
        thr = lax.bitcast_convert_type(cand, f32)
        cnt = jnp.sum(jnp.where(aff_ref[...] >= thr, 1.0, 0.0), axis=1, keepdims=True)
        return jnp.where(cnt >= capf, cand, t)

    t = lax.fori_loop(0, 31, bit_step, jnp.zeros((rows, 1), jnp.int32))
    thr = lax.bitcast_convert_type(t, f32)
    need = capf - jnp.sum(jnp.where(aff_ref[...] > thr, 1.0, 0.0), axis=1, keepdims=True)

    blk = MOE_BLOCK
    tri = jnp.where(
        lax.broadcasted_iota(jnp.int32, (blk, blk), 0) < lax.broadcasted_iota(jnp.int32, (blk, blk), 1),
        1.0, 0.0).astype(bf16)
    carry_gt = jnp.zeros((rows, 1), f32)
    carry_eq = jnp.zeros((rows, 1), f32)
    lane = lax.broadcasted_iota(jnp.int32, (rows, LANES), 1)
    meta = jnp.zeros((rows, LANES), f32)
    cmax = jnp.zeros((rows, 1), f32)
    for b in range(n // blk):
        sl = slice(b * blk, (b + 1) * blk)
        ab = aff_ref[:, sl]
        gt = ab > thr
        eq = ab == thr
        gtb = jnp.where(gt, 1.0, 0.0)
        eqb = jnp.where(eq, 1.0, 0.0)
        pre_gt = _dot(gtb.astype(bf16), tri) + carry_gt
        pre_eq = _dot(eqb.astype(bf16), tri) + carry_eq
        meta = jnp.where(lane == b, carry_gt + jnp.minimum(carry_eq, need), meta)
        carry_gt = carry_gt + jnp.sum(gtb, axis=1, keepdims=True)
        carry_eq = carry_eq + jnp.sum(eqb, axis=1, keepdims=True)
        sel = gt | (eq & (pre_eq < need))
        slot = pre_gt + jnp.minimum(pre_eq, need)
        pos_ref[:, sl] = jnp.where(sel, slot, -1.0).astype(jnp.int32)
        cmax = jnp.maximum(cmax, jnp.sum(jnp.where(sel, 1.0, 0.0), axis=1, keepdims=True))
    meta_ref[...] = jnp.where(lane == LANES - 1, cmax, meta).astype(jnp.int32)


def _route(lg3, *, cap):
    nseq, n, _ = lg3.shape
    rows = nseq * N_EXPERTS
    assert n // MOE_BLOCK < LANES
    return pl.pallas_call(
        functools.partial(_route_kernel, nseq=nseq, n=n, cap=cap),
        grid=(1,),
        in_specs=[_const_spec(lg3.shape)],
        out_specs=(_const_spec((rows, n)), _const_spec((rows, n)), _const_spec((rows, LANES))),
        out_shape=(jax.ShapeDtypeStruct((rows, n), jnp.int32), jax.ShapeDtypeStruct((rows, n), f32),
                   jax.ShapeDtypeStruct((rows, LANES), jnp.int32)),
        compiler_params=_params(VMEM_SMALL, "arbitrary"),
        name="route",
    )(lg3)


def _slot_hits(pos_ref, e, cap):
    width = pos_ref.shape[1]
    return lax.broadcasted_iota(jnp.int32, (cap, width), 0) == pos_ref[e:e + 1, :]


def _one_hot(hits):
    return jnp.concatenate([jnp.where(h, 1.0, 0.0).astype(bf16) for h in hits], axis=0)


def _gather_dense(pos_ref, aff_ref, h2_ref, xg_ref, vals_ref, rows, cap, ne):
    for e0 in range(0, N_EXPERTS, ne):
        hits = [_slot_hits(pos_ref, e0 + e, cap) for e in range(ne)]
        xg = _dot(_one_hot(hits), h2_ref[...]).astype(xg_ref.dtype)
        for e in range(ne):
            xg_ref[e0 + e, rows, :] = xg[e * cap:(e + 1) * cap, :]
            vals = jnp.sum(jnp.where(hits[e], aff_ref[e0 + e:e0 + e + 1, :], 0.0), axis=1, keepdims=True)
            vals_ref[e0 + e, rows, :] = jnp.broadcast_to(vals, (cap, LANES))


def _gather_kernel(pos_ref, aff_ref, h2_ref, xg_ref, vals_ref, *, cap, ne, sb):
    for g in range(sb):
        _gather_dense(pos_ref.at[g], aff_ref.at[g], h2_ref.at[g], xg_ref, vals_ref,
                      slice(g * cap, (g + 1) * cap), cap, ne)


def _gather(pos3, aff3, h23, *, cap, ne, sb):
    nseq, _, n = pos3.shape
    return pl.pallas_call(
        functools.partial(_gather_kernel, cap=cap, ne=ne, sb=sb),
        grid=(nseq // sb,),
        in_specs=[
            pl.BlockSpec((sb, N_EXPERTS, n), lambda s: (s, 0, 0)),
            pl.BlockSpec((sb, N_EXPERTS, n), lambda s: (s, 0, 0)),
            pl.BlockSpec((sb, n, D_MODEL), lambda s: (s, 0, 0)),
        ],
        out_specs=(
            pl.BlockSpec((N_EXPERTS, sb * cap, D_MODEL), lambda s: (0, s, 0)),
            pl.BlockSpec((N_EXPERTS, sb * cap, LANES), lambda s: (0, s, 0)),
        ),
        out_shape=(
            jax.ShapeDtypeStruct((N_EXPERTS, nseq * cap, D_MODEL), bf16),
            jax.ShapeDtypeStruct((N_EXPERTS, nseq * cap, LANES), f32),
        ),
        compiler_params=_params(VMEM_MID, "parallel"),
        name="gather",
    )(pos3, aff3, h23)


def _windows_fit(meta_ref):
    cmax = meta_ref[0, LANES - 1]
    for e in range(1, N_EXPERTS):
        cmax = jnp.maximum(cmax, meta_ref[e, LANES - 1])
    return cmax <= MOE_WIN - WIN_ALIGN


def _win_base(meta_ref, e, b, cap):
    start = meta_ref[e, b]
    return pl.multiple_of(jnp.minimum(start - start % WIN_ALIGN, cap - MOE_WIN), WIN_ALIGN)


def _win_hits(pos_row, base):
    slot = lax.broadcasted_iota(jnp.int32, (MOE_WIN, pos_row.shape[1]), 0) + base
    return slot == pos_row


def _gather_win_kernel(meta_ref, pos_ref, aff_ref, h2_ref, xg_ref, vals_ref, *, n, cap):
    fits = _windows_fit(meta_ref)

    @pl.when(fits)
    def _():
        xg_ref[...] = jnp.zeros(xg_ref.shape, xg_ref.dtype)
        vals_ref[...] = jnp.zeros(vals_ref.shape, vals_ref.dtype)

        def block(b, carry):
            cols = pl.ds(pl.multiple_of(b * MOE_BLOCK, MOE_BLOCK), MOE_BLOCK)
            bases = [_win_base(meta_ref, e, b, cap) for e in range(N_EXPERTS)]
            hits = [_win_hits(pos_ref[e:e + 1, cols], bases[e]) for e in range(N_EXPERTS)]
            part = _dot(_one_hot(hits), h2_ref[cols, :])
            for e in range(N_EXPERTS):
                rows = pl.ds(bases[e], MOE_WIN)
                xg_ref[e, rows, :] += part[e * MOE_WIN:(e + 1) * MOE_WIN, :].astype(xg_ref.dtype)
                vals = jnp.sum(jnp.where(hits[e], aff_ref[e:e + 1, cols], 0.0), axis=1, keepdims=True)
                vals_ref[e, rows, :] += jnp.broadcast_to(vals, (MOE_WIN, LANES))
            return carry

        lax.fori_loop(0, n // MOE_BLOCK, block, 0)

    @pl.when(jnp.logical_not(fits))
    def _():
        _gather_dense(pos_ref, aff_ref, h2_ref, xg_ref, vals_ref, slice(0, cap), cap, 1)


def _gather_win(meta, pos3, aff3, h23, *, cap):
    nseq, _, n = pos3.shape
    return pl.pallas_call(
        functools.partial(_gather_win_kernel, n=n, cap=cap),
        grid=(nseq,),
        in_specs=[
            pl.BlockSpec((N_EXPERTS, LANES), lambda s: (s, 0), memory_space=pltpu.SMEM),
            pl.BlockSpec((None, N_EXPERTS, n), lambda s: (s, 0, 0)),
            pl.BlockSpec((None, N_EXPERTS, n), lambda s: (s, 0, 0)),
            pl.BlockSpec((None, n, D_MODEL), lambda s: (s, 0, 0)),
        ],
        out_specs=(
            pl.BlockSpec((N_EXPERTS, cap, D_MODEL), lambda s: (0, s, 0)),
            pl.BlockSpec((N_EXPERTS, cap, LANES), lambda s: (0, s, 0)),
        ),
        out_shape=(
            jax.ShapeDtypeStruct((N_EXPERTS, nseq * cap, D_MODEL), bf16),
            jax.ShapeDtypeStruct((N_EXPERTS, nseq * cap, LANES), f32),
        ),
        compiler_params=_params(VMEM_LARGE, "parallel"),
        name="gather_win",
    )(meta, pos3, aff3, h23)


def _experts_kernel(xp_ref, vp_ref, xs_ref, vs_ref, wg_ref, wu_ref, wd_ref, yp_ref, ys_ref, *, rc):
    wg = wg_ref[...].astype(bf16)
    wu = wu_ref[...].astype(bf16)
    wd = wd_ref[...].astype(bf16)
    for x_ref, v_ref, y_ref in ((xp_ref, vp_ref, yp_ref), (xs_ref, vs_ref, ys_ref)):
        for r0 in range(0, x_ref.shape[0], rc):
            x = x_ref[r0:r0 + rc, :]
            a = _dot(x, wg)
            u = _dot(x, wu)
            hm = (a * jax.nn.sigmoid(a) * u).astype(bf16)
            y = _dot(hm, wd) * v_ref[r0:r0 + rc, 0:1]
            y_ref[r0:r0 + rc, :] = y.astype(y_ref.dtype)


def _experts(xg_p, vals_p, xg_s, vals_s, wg, wu, wd, *, rc=EXPERT_ROWS):
    rp = xg_p.shape[1]
    rs = xg_s.shape[1]
    per_e = lambda r, w: pl.BlockSpec((None, r, w), lambda e: (e, 0, 0))
    return pl.pallas_call(
        functools.partial(_experts_kernel, rc=rc),
        grid=(N_EXPERTS,),
        in_specs=[
            per_e(rp, D_MODEL), per_e(rp, LANES), per_e(rs, D_MODEL), per_e(rs, LANES),
            per_e(D_MODEL, EXPERT_FF), per_e(D_MODEL, EXPERT_FF), per_e(EXPERT_FF, D_MODEL),
        ],
        out_specs=(per_e(rp, D_MODEL), per_e(rs, D_MODEL)),
        out_shape=(
            jax.ShapeDtypeStruct((N_EXPERTS, rp, D_MODEL), bf16),
            jax.ShapeDtypeStruct((N_EXPERTS, rs, D_MODEL), bf16),
        ),
        compiler_params=_params(VMEM_LARGE, "parallel"),
        name="experts",
    )(xg_p, vals_p, xg_s, vals_s, wg, wu, wd)


def _scatter_dense(pos_ref, y_ref, rows, tn, cap, ne):
    moe = jnp.zeros((tn, D_MODEL), f32)
    for e0 in range(0, N_EXPERTS, ne):
        onehot = _one_hot([_slot_hits(pos_ref, e0 + e, cap) for e in range(ne)])
        y = jnp.concatenate([y_ref[e0 + e, rows, :] for e in range(ne)], axis=0)
        moe = moe + lax.dot_general(onehot, y, (((0,), (0,)), ((), ())), preferred_element_type=f32)
    return moe


def _scatter_kernel(pos_ref, y_ref, x1_ref, m_ref, fn_ref, o_ref, *, tn, cap, ne, sb):
    g2 = m_ref[5:6, :]
    for g in range(sb):
        moe = _scatter_dense(pos_ref.at[g], y_ref, slice(g * cap, (g + 1) * cap), tn, cap, ne)
        o_ref[g] = _rms(x1_ref[g] + g2 * moe, fn_ref[...])


def _scatter(pos3, y, x13, m3, fn, *, cap, ne, tn, sb):
    nseq, _, n = pos3.shape
    assert sb == 1 or m3.shape[0] == 1, "sequences sharing a grid step must share their modulation rows"
    return pl.pallas_call(
        functools.partial(_scatter_kernel, tn=tn, cap=cap, ne=ne, sb=sb),
        grid=(nseq // sb, n // tn),
        in_specs=[
            pl.BlockSpec((sb, N_EXPERTS, tn), lambda s, i: (s, 0, i)),
            pl.BlockSpec((N_EXPERTS, sb * cap, D_MODEL), lambda s, i: (0, s, 0)),
            pl.BlockSpec((sb, tn, D_MODEL), lambda s, i: (s, i, 0)),
            _mod_spec(m3, lambda s, i: s),
            _const_spec((1, D_MODEL)),
        ],
        out_specs=pl.BlockSpec((sb, tn, D_MODEL), lambda s, i: (s, i, 0)),
        out_shape=jax.ShapeDtypeStruct((nseq, n, D_MODEL), f32),
        compiler_params=_params(VMEM_MID, "parallel", "arbitrary"),
        name="scatter",
    )(pos3, y, x13, m3, fn)


def _scatter_win_kernel(meta_ref, pos_ref, y_hbm, x1_ref, m_ref, fn_ref, o_ref, moe_scr, ybuf, ysem, *, cap, nseq):
    s = pl.program_id(0)
    b = pl.program_id(1)

    def y_copy(seq):
        return pltpu.make_async_copy(y_hbm.at[:, pl.ds(seq * cap, cap), :], ybuf.at[seq % 2], ysem.at[seq % 2])

    @pl.when(b == 0)
    def _():
        @pl.when(s == 0)
        def _():
            y_copy(s).start()

        @pl.when(s + 1 < nseq)
        def _():
            y_copy(s + 1).start()

        y_copy(s).wait()

    y_ref = ybuf.at[s % 2]
    fits = _windows_fit(meta_ref)

    @pl.when(fits)
    def _():
        bases = [_win_base(meta_ref, e, b, cap) for e in range(N_EXPERTS)]
        onehot = _one_hot([_win_hits(pos_ref[e:e + 1, :], bases[e]) for e in range(N_EXPERTS)])
        y = jnp.concatenate([y_ref[e, pl.ds(bases[e], MOE_WIN), :] for e in range(N_EXPERTS)], axis=0)
        moe_scr[...] = lax.dot_general(onehot, y, (((0,), (0,)), ((), ())), preferred_element_type=f32)

    @pl.when(jnp.logical_not(fits))
    def _():
        moe_scr[...] = _scatter_dense(pos_ref, y_ref, slice(0, cap), MOE_BLOCK, cap, 1)

    o_ref[...] = _rms(x1_ref[...] + m_ref[5:6, :] * moe_scr[...], fn_ref[...])


def _scatter_win(meta, pos3, y, x13, m3, fn, *, cap):
    nseq, _, n = pos3.shape
    tn = MOE_BLOCK
    return pl.pallas_call(
        functools.partial(_scatter_win_kernel, cap=cap, nseq=nseq),
        grid=(nseq, n // tn),
        in_specs=[
            pl.BlockSpec((N_EXPERTS, LANES), lambda s, i: (s, 0), memory_space=pltpu.SMEM),
            pl.BlockSpec((None, N_EXPERTS, tn), lambda s, i: (s, 0, i)),
            pl.BlockSpec(memory_space=pl.ANY),
            pl.BlockSpec((None, tn, D_MODEL), lambda s, i: (s, i, 0)),
            _mod_spec(m3, lambda s, i: s),
            _const_spec((1, D_MODEL)),
        ],
        out_specs=pl.BlockSpec((None, tn, D_MODEL), lambda s, i: (s, i, 0)),
        out_shape=jax.ShapeDtypeStruct((nseq, n, D_MODEL), f32),
        scratch_shapes=[
            pltpu.VMEM((tn, D_MODEL), f32),
            pltpu.VMEM((2, N_EXPERTS, cap, D_MODEL), y.dtype),
            pltpu.SemaphoreType.DMA((2,)),
        ],
        compiler_params=_params(VMEM_MID, "arbitrary", "arbitrary"),
        name="scatter_win",
    )(meta, pos3, y, x13, m3, fn)


def _rope_tables(n):
    t = np.arange(n)
    half = QK_ROPE // 2
    freqs = ROPE_BASE ** (-np.arange(0, half, 2, dtype=np.float64) / half)
    ang_r = (t // GRID_W)[:, None] * freqs
    ang_c = (t % GRID_W)[:, None] * freqs
    cr, sr, cc, sc = np.cos(ang_r), np.sin(ang_r), np.cos(ang_c), np.sin(ang_c)
    cos = np.ones((n, HEAD_W))
    sin = np.zeros((n, HEAD_W))
    cos[:, ROPE_OFF:ROPE_OFF + QK_ROPE] = np.concatenate([cr, cr, cc, cc], axis=-1)
    sin[:, ROPE_OFF:ROPE_OFF + QK_ROPE] = np.concatenate([-sr, sr, -sc, sc], axis=-1)
    return jnp.asarray(cos, f32), jnp.asarray(sin, f32)


_PARTNER = np.concatenate([np.arange(8, 16), np.arange(0, 8), np.arange(24, 32), np.arange(16, 24)])


def _rope_partner(w):
    q = QK_ROPE // 4
    return jnp.concatenate([w[..., q:2 * q], w[..., 0:q], w[..., 3 * q:4 * q], w[..., 2 * q:3 * q]], axis=-1)


def _rope_placement():
    place = np.zeros((LANES, 2 * LANES), np.float32)
    d = np.arange(QK_ROPE)
    place[d, ROPE_OFF + d] = 1.0
    place[_PARTNER, LANES + ROPE_OFF + d] = 1.0
    return jnp.asarray(place, bf16)


def _head_blocks(w_nope, w_rope):
    rows = w_nope.shape[0]
    if w_rope is None:
        w_rope = jnp.zeros((rows, N_HEADS, QK_ROPE), w_nope.dtype)
    z = jnp.zeros((rows, N_HEADS, HEAD_W - QK_NOPE - QK_ROPE), w_nope.dtype)
    return jnp.concatenate([w_nope, w_rope, z], axis=-1).reshape(rows, N_HEADS * HEAD_W)


def kernel(x_prompt, x_sample, cache_ckv, cache_kpe, c, c_ctx, w_ada, b_ada, norm1, w_in, conv_dw, conv_dw_b,
           conv_ln_g, conv_ln_b, w_conv_out, q_norm, w_qb, kv_norm, w_kvb, w_o_mla, w_out, norm2, w_router,
           w_e_gate, w_e_up, w_e_down, final_norm):
    assert w_ada.shape[0] == 1, "single trunk layer"
    nb_p, n_p, _ = x_prompt.shape
    nb_s, n_s, _ = x_sample.shape

    win = _wprep(w_in[0].T, _rope_placement())
    wq = w_qb[0].reshape(Q_LORA, N_HEADS, QK_NOPE + QK_ROPE)
    wqb = _head_blocks(wq[..., :QK_NOPE], wq[..., QK_NOPE:]).astype(bf16)
    wqbp = _head_blocks(jnp.zeros_like(wq[..., :QK_NOPE]), _rope_partner(wq[..., QK_NOPE:])).astype(bf16)
    wkv = w_kvb[0].reshape(KV_LORA, N_HEADS, QK_NOPE + V_DIM)
    wk = _head_blocks(wkv[..., :QK_NOPE], None).astype(bf16)
    wv = wkv[..., QK_NOPE:].reshape(KV_LORA, N_HEADS * V_DIM).astype(bf16)
    wco = w_conv_out[0].astype(bf16)
    wo = w_o_mla[0].astype(bf16)
    wout = w_out[0].astype(bf16)
    wr_hi = w_router[0].astype(bf16)
    wr_lo = (w_router[0] - wr_hi.astype(f32)).astype(bf16)
    wr = jnp.concatenate([wr_hi, wr_lo, jnp.zeros((D_MODEL, LANES - 2 * N_EXPERTS), bf16)], axis=-1)
    row = lambda a: a.reshape(1, -1)

    mod = jnp.concatenate([c_ctx[None, :], c, jnp.zeros((8 - 1 - nb_s, D_MODEL), f32)], axis=0)
    m = _ada(mod, w_ada[0], b_ada[0]).reshape(8, 6, D_MODEL)
    m_p, m_s = m[0:1], m[1:1 + nb_s]
    cos, sin = _rope_tables(n_s)

    conv_w = _conv_weights(conv_dw[0], row(conv_dw_b[0]), row(conv_ln_g[0]), row(conv_ln_b[0]), wco, CONV_CHUNK)

    def mixers(x, m3, rope, ctx_ckv, ctx_kpe):
        nseq, n, _ = x.shape
        x2d = x.reshape(nseq * n, D_MODEL)
        cm, q, ckv, kr, g, *kpe = _inproj(x2d, m3, row(norm1[0]), win, row(q_norm[0]), wqb, wqbp, row(kv_norm[0]),
                                          cos, sin, conv_w, n=n, rope=rope, tm=IN_TILE)
        keys_ckv = ckv.reshape(nseq, n, KV_LORA)
        keys_kpe = kr.reshape(nseq, n, LANES)
        if ctx_ckv is not None:
            keys_ckv = jnp.concatenate([ctx_ckv.astype(keys_ckv.dtype), keys_ckv], axis=1)
            keys_kpe = jnp.concatenate([ctx_kpe.astype(keys_kpe.dtype), keys_kpe], axis=1)
        x1, h2, lg = _attn(q, keys_ckv, keys_kpe, cm.reshape(nseq * n, D_MODEL), g, x2d, m3, wk, wv, wo, wout,
                           row(norm2[0]), wr, n=n, tq=min(n, Q_TILE))
        return x1, h2, lg, ckv, kpe

    ctx_kpe = jnp.pad(cache_kpe[:, 0], ((0, 0), (0, 0), (ROPE_OFF, LANES - ROPE_OFF - QK_ROPE)))
    x1_p, h2_p, lg_p, ckv_p, (kpe_p,) = mixers(x_prompt, m_p, False, None, None)
    x1_s, h2_s, lg_s, _, _ = mixers(x_sample, m_s, True, cache_ckv[:, 0], ctx_kpe)

    def moe_tiles(n):
        cap = EC_FACTOR * n // N_EXPERTS
        ne = N_EXPERTS if N_EXPERTS * cap <= MOE_ROWS else 1
        sb = max(1, MOE_TOKENS // n)
        return cap, ne, sb

    def windowed(n):
        return n >= 4 * MOE_BLOCK and EC_FACTOR * n // N_EXPERTS >= 2 * MOE_WIN

    def route_gather(h2, lg, nseq, n):
        cap, ne, sb = moe_tiles(n)
        pos, aff, meta = _route(lg.reshape(nseq, n, LANES), cap=cap)
        pos3 = pos.reshape(nseq, N_EXPERTS, n)
        aff3 = aff.reshape(nseq, N_EXPERTS, n)
        h23 = h2.reshape(nseq, n, D_MODEL)
        if windowed(n):
            xg, vals = _gather_win(meta, pos3, aff3, h23, cap=cap)
        else:
            xg, vals = _gather(pos3, aff3, h23, cap=cap, ne=ne, sb=sb)
        return pos3, meta, xg, vals

    pos_p, meta_p, xg_p, vals_p = route_gather(h2_p, lg_p, nb_p, n_p)
    pos_s, meta_s, xg_s, vals_s = route_gather(h2_s, lg_s, nb_s, n_s)
    y_p, y_s = _experts(xg_p, vals_p, xg_s, vals_s, w_e_gate[0], w_e_up[0], w_e_down[0])
    fn = row(final_norm)

    def scatter(pos, meta, y, x1, m3, nseq, n):
        cap, ne, sb = moe_tiles(n)
        x13 = x1.reshape(nseq, n, D_MODEL)
        if windowed(n):
            return _scatter_win(meta, pos, y, x13, m3, fn, cap=cap)
        return _scatter(pos, y, x13, m3, fn, cap=cap, ne=ne, tn=min(n, SCATTER_TILE), sb=sb)

    y_prompt = scatter(pos_p, meta_p, y_p, x1_p, m_p, nb_p, n_p)
    y_sample = scatter(pos_s, meta_s, y_s, x1_s, m_s, nb_s, n_s)

    new_ckv = ckv_p.reshape(nb_p, 1, n_p, KV_LORA)
    new_kpe = kpe_p.reshape(nb_p, 1, n_p, QK_ROPE)
    return (y_prompt, y_sample, new_ckv, new_kpe)
```

```python
import functools

import jax
import jax.numpy as jnp
import numpy as np
from jax import lax
from jax.experimental import pallas as pl
from jax.experimental.pallas import tpu as pltpu

D_MODEL = 1024
GRID_W = 64
CONV_DIM = 512
CONV_WIDTH = 31
N_HEADS = 8
QK_NOPE = 64
QK_ROPE = 32
V_DIM = 64
Q_LORA = 256
KV_LORA = 128
N_EXPERTS = 16
EXPERT_FF = 512
EC_FACTOR = 2
ROPE_BASE = 10000.0
EPS = 1e-6

LANES = 128
HEAD_W = LANES
ROPE_OFF = QK_NOPE
CONV_HALO = 16
LOG2E = 1.4426950408889634
VMEM_LIMIT = 48 * 1024 * 1024
IN_TILE = 512
CONV_CHUNK = 64
CONV_OUT_ROWS = 256
GATE_CHUNK = 512
EXPERT_ROWS = 512
Q_TILE = 512
ATTN_TOKENS = 1024
SCORE_BYTES = 12 * 1024 * 1024
MOE_ROWS = 512
MOE_TOKENS = 1024
SCATTER_TILE = 512
MOE_BLOCK = 2 * LANES
MOE_WIN = 80
WIN_ALIGN = 16

C_CONV = 0
C_QA = 2 * CONV_DIM
C_KVA = C_QA + Q_LORA
C_KR = C_KVA + KV_LORA
C_GATE = C_KR + LANES
C_END = C_GATE + 2 * D_MODEL
C_KRP = C_END

f32 = jnp.float32
bf16 = jnp.bfloat16


MIB = 1024 * 1024
VMEM_SMALL, VMEM_MID, VMEM_LARGE = 16 * MIB, 32 * MIB, 40 * MIB


def _params(vmem_bytes, *sem):
    assert vmem_bytes <= VMEM_LIMIT
    return pltpu.CompilerParams(dimension_semantics=sem, vmem_limit_bytes=vmem_bytes)


def _dot(a, b):
    return jnp.dot(a, b, preferred_element_type=f32)


def _rms(x, g):
    return x * lax.rsqrt(jnp.mean(x * x, axis=-1, keepdims=True) + EPS) * g


def _const_spec(shape):
    nd = len(shape)
    return pl.BlockSpec(shape, lambda *_: (0,) * nd)


def _mod_spec(m3, seq_of):
    if m3.shape[0] == 1:
        return _const_spec((None, 6, D_MODEL))
    return pl.BlockSpec((None, 6, D_MODEL), lambda *idx: (seq_of(*idx), 0, 0))


def _ada_kernel(s_ref, w_ref, b_ref, o_ref):
    s = s_ref[...]
    s = s * jax.nn.sigmoid(s)
    o_ref[...] = _dot(s.astype(bf16), w_ref[...].astype(bf16)) + b_ref[...]


def _ada(mod, w_ada, b_ada):
    rows = mod.shape[0]
    n_out = w_ada.shape[1]
    tn = D_MODEL
    return pl.pallas_call(
        _ada_kernel,
        grid=(n_out // tn,),
        in_specs=[
            _const_spec((rows, D_MODEL)),
            pl.BlockSpec((D_MODEL, tn), lambda j: (0, j)),
            pl.BlockSpec((1, tn), lambda j: (0, j)),
        ],
        out_specs=pl.BlockSpec((rows, tn), lambda j: (0, j)),
        out_shape=jax.ShapeDtypeStruct((rows, n_out), f32),
        compiler_params=_params(VMEM_LIMIT, "arbitrary"),
        name="ada",
    )(mod, w_ada, b_ada.reshape(1, n_out))


def _wprep_kernel(wt_ref, place_ref, o_ref):
    def block(r0):
        return wt_ref[r0:r0 + LANES, :].T.astype(bf16)

    for j in range(C_KR // LANES):
        o_ref[:, j * LANES:(j + 1) * LANES] = block(j * LANES)
    placed = _dot(block(C_KR), place_ref[...])
    o_ref[:, C_KR:C_GATE] = placed[:, 0:LANES].astype(bf16)
    o_ref[:, C_KRP:C_KRP + LANES] = placed[:, LANES:2 * LANES].astype(bf16)
    for j in range(2 * D_MODEL // LANES):
        o_ref[:, C_GATE + j * LANES:C_GATE + (j + 1) * LANES] = block(C_KR + QK_ROPE + j * LANES)


def _wprep(w_in_t, place):
    cols, rows = w_in_t.shape
    return pl.pallas_call(
        _wprep_kernel,
        grid=(rows // LANES,),
        in_specs=[pl.BlockSpec((cols, LANES), lambda i: (0, i)), _const_spec(place.shape)],
        out_specs=pl.BlockSpec((LANES, C_KRP + LANES), lambda i: (i, 0)),
        out_shape=jax.ShapeDtypeStruct((rows, C_KRP + LANES), bf16),
        compiler_params=_params(VMEM_SMALL, "parallel"),
        name="wprep",
    )(w_in_t, place)


def _conv_taps(vpad, r0, shift_ref, dw_ref, dwb_ref, ybuf, y0, ct):
    pad = CONV_WIDTH // 2
    sub = 8
    span = ((CONV_HALO - pad + CONV_WIDTH - 1) // sub) * sub
    cw = 2 * LANES
    for cb in range(CONV_DIM // cw):
        sl = slice(cb * cw, (cb + 1) * cw)
        win = vpad[pl.ds(r0, ct + 2 * CONV_HALO), sl]
        acc = jnp.zeros((ct // sub, sub, cw), f32)
        for ph in range(sub):
            wph = win.astype(f32) if ph == 0 else _dot(shift_ref[ph - 1], win)
            for a in range(span // sub + 1):
                k = a * sub + ph - (CONV_HALO - pad)
                if 0 <= k < CONV_WIDTH:
                    acc = acc + wph[a * sub:a * sub + ct, :].reshape(ct // sub, sub, cw) * dw_ref[k, :, sl][None]
        ybuf[pl.ds(y0, ct), sl] = acc.reshape(ct, cw) + dwb_ref[:, sl]


def _conv_out(y, lng_ref, lnb_ref, wco_ref, gate):
    mu = jnp.mean(y, axis=-1, keepdims=True)
    yc = y - mu
    var = jnp.mean(yc * yc, axis=-1, keepdims=True)
    z = yc * lax.rsqrt(var + EPS) * lng_ref[...] + lnb_ref[...]
    z = z * jax.nn.sigmoid(z)
    return (gate.astype(f32) * _dot(z.astype(bf16), wco_ref[...])).astype(bf16)


def _fill_padded(vpad, v, n):
    zeros = jnp.zeros((CONV_HALO, CONV_DIM), vpad.dtype)
    vpad[0:CONV_HALO, :] = zeros
    vpad[CONV_HALO + n:2 * CONV_HALO + n, :] = zeros
    vpad[CONV_HALO:CONV_HALO + n, :] = v


def _inproj_kernel(*refs, rope, halo, tiles_per_seq, q_scale, n, rt, ct):
    refs = list(refs)
    x_ref = refs.pop(0)
    if halo:
        xp_ref, xn_ref = refs[:2]
        del refs[:2]
    m_ref, n1_ref, win_ref, qn_ref, wqb_ref = refs[:5]
    del refs[:5]
    if rope:
        wqbp_ref = refs.pop(0)
    kvn_ref = refs.pop(0)
    if rope:
        cos_ref, sin_ref = refs[:2]
        del refs[:2]
    shift_ref, dw_ref, dwb_ref, lng_ref, lnb_ref, wco_ref = refs[:6]
    del refs[:6]
    cm_ref, q_ref, ckv_ref, kr_ref, g_ref = refs[:5]
    del refs[:5]
    if not rope:
        kpe_ref = refs.pop(0)
    vpad, ybuf, gc = refs
    tm = x_ref.shape[0]
    lo = CONV_HALO if halo else 0

    sh1 = m_ref[0:1, :]
    sc1 = m_ref[1:2, :]
    x = jnp.concatenate([xp_ref[...], x_ref[...], xn_ref[...]], axis=0) if halo else x_ref[...]
    hb_all = (_rms(x, n1_ref[...]) * (1.0 + sc1) + sh1).astype(bf16)

    left_all = _dot(hb_all, win_ref[:, 0:C_GATE])
    v = (left_all[:, C_CONV:C_CONV + CONV_DIM] * jax.nn.sigmoid(left_all[:, C_CONV + CONV_DIM:C_QA])).astype(bf16)
    hb = hb_all[lo:lo + tm, :]
    left = left_all[lo:lo + tm, :]

    if halo:
        j = pl.program_id(0) % tiles_per_seq
        zeros = jnp.zeros((CONV_HALO, CONV_DIM), bf16)
        vpad[0, 0:lo, :] = jnp.where(j > 0, v[0:lo, :], zeros)
        vpad[0, lo:lo + tm, :] = v[lo:lo + tm, :]
        vpad[0, lo + tm:2 * lo + tm, :] = jnp.where(j < tiles_per_seq - 1, v[lo + tm:2 * lo + tm, :], zeros)
        chunks = [(0, c * ct, c * ct) for c in range(tm // ct)]
    else:
        for s in range(tm // n):
            _fill_padded(vpad.at[s], v[s * n:(s + 1) * n, :], n)
        chunks = [(s, c * ct, s * n + c * ct) for s in range(tm // n) for c in range(n // ct)]

    def conv_some(count):
        for _ in range(min(count, len(chunks))):
            s, r0, y0 = chunks.pop(0)
            _conv_taps(vpad.at[s], r0, shift_ref, dw_ref, dwb_ref, ybuf, y0, ct)

    steps = 2 + 2 * D_MODEL // GATE_CHUNK
    per_step = -(-len(chunks) // steps)

    qa = left[:, C_QA:C_KVA]
    qn = _rms(qa, qn_ref[...]).astype(bf16)
    q = _dot(qn, wqb_ref[...])
    if rope:
        qp = _dot(qn, wqbp_ref[...])
        cos = cos_ref[...]
        sin = sin_ref[...]
        for hd in range(N_HEADS):
            sl = slice(hd * HEAD_W, (hd + 1) * HEAD_W)
            q_ref[:, sl] = ((q[:, sl] * cos + qp[:, sl] * sin) * q_scale).astype(q_ref.dtype)
    else:
        q_ref[...] = (q * q_scale).astype(q_ref.dtype)
    conv_some(per_step)

    kva = left[:, C_KVA:C_KR]
    ckv_ref[...] = _rms(kva, kvn_ref[...]).astype(ckv_ref.dtype)

    kr = left[:, C_KR:C_GATE]
    if rope:
        krp = _dot(hb, win_ref[:, C_KRP:C_KRP + LANES])
        kr = kr * cos_ref[...] + krp * sin_ref[...]
    else:
        kpe_ref[...] = kr[:, ROPE_OFF:ROPE_OFF + QK_ROPE]
    kr_ref[...] = kr.astype(kr_ref.dtype)
    conv_some(per_step)

    gw = GATE_CHUNK
    for jg in range(2 * D_MODEL // gw):
        gate = jax.nn.sigmoid(_dot(hb, win_ref[:, C_GATE + jg * gw:C_GATE + (jg + 1) * gw])).astype(bf16)
        if jg * gw < D_MODEL:
            gc[:, jg * gw:(jg + 1) * gw] = gate
        else:
            g_ref[:, jg * gw - D_MODEL:(jg + 1) * gw - D_MODEL] = gate
        conv_some(per_step)

    conv_some(len(chunks))
    for r0 in range(0, tm, rt):
        cm_ref[r0:r0 + rt, :] = _conv_out(ybuf[r0:r0 + rt, :], lng_ref, lnb_ref, wco_ref, gc[r0:r0 + rt, :])


def _inproj(x2d, m3, norm1, win, q_norm, wqb, wqbp, kv_norm, cos, sin, conv_w, *, n, rope, tm,
            rt=CONV_OUT_ROWS, ct=CONV_CHUNK):
    tokens = x2d.shape[0]
    halo = n > tm
    assert n % tm == 0 if halo else (tm % n == 0 and m3.shape[0] == 1 and not rope)
    tiles_per_seq = max(1, n // tm)
    q_scale = float((QK_NOPE + QK_ROPE) ** -0.5 * LOG2E)
    tile = lambda w: pl.BlockSpec((tm, w), lambda i: (i, 0))
    in_specs = [tile(D_MODEL)]
    args = [x2d]
    if halo:
        hb_tile = tm // CONV_HALO
        hb_seq = n // CONV_HALO
        prev = lambda i: (jnp.maximum(i * hb_tile - 1, (i // tiles_per_seq) * hb_seq), 0)
        nxt = lambda i: (jnp.minimum((i + 1) * hb_tile, (i // tiles_per_seq + 1) * hb_seq - 1), 0)
        in_specs += [pl.BlockSpec((CONV_HALO, D_MODEL), prev), pl.BlockSpec((CONV_HALO, D_MODEL), nxt)]
        args += [x2d, x2d]
    in_specs += [
        _mod_spec(m3, lambda i: i // tiles_per_seq),
        _const_spec((1, D_MODEL)),
        _const_spec((D_MODEL, C_KRP + LANES if rope else C_END)),
        _const_spec((1, Q_LORA)),
        _const_spec(wqb.shape),
    ]
    args += [m3, norm1, win, q_norm, wqb]
    if rope:
        in_specs.append(_const_spec(wqbp.shape))
        args.append(wqbp)
    in_specs.append(_const_spec((1, KV_LORA)))
    args.append(kv_norm)
    if rope:
        tab = pl.BlockSpec((tm, LANES), lambda i: (i % tiles_per_seq, 0))
        in_specs += [tab, tab]
        args += [cos, sin]
    args += list(conv_w)
    in_specs += [_const_spec(w.shape) for w in conv_w]
    pieces = max(1, tm // n)
    scratch = [
        pltpu.VMEM((pieces, tm // pieces + 2 * CONV_HALO, CONV_DIM), bf16),
        pltpu.VMEM((tm, CONV_DIM), f32),
        pltpu.VMEM((tm, D_MODEL), bf16),
    ]
    out_shape = [
        jax.ShapeDtypeStruct((tokens, D_MODEL), bf16),
        jax.ShapeDtypeStruct((tokens, N_HEADS * HEAD_W), bf16),
        jax.ShapeDtypeStruct((tokens, KV_LORA), bf16 if rope else f32),
        jax.ShapeDtypeStruct((tokens, LANES), bf16),
        jax.ShapeDtypeStruct((tokens, D_MODEL), bf16),
    ]
    out_specs = [tile(D_MODEL), tile(N_HEADS * HEAD_W), tile(KV_LORA), tile(LANES), tile(D_MODEL)]
    if not rope:
        out_shape.append(jax.ShapeDtypeStruct((tokens, QK_ROPE), f32))
        out_specs.append(tile(QK_ROPE))
    return pl.pallas_call(
        functools.partial(_inproj_kernel, rope=rope, halo=halo, tiles_per_seq=tiles_per_seq, q_scale=q_scale,
                          n=n, rt=rt, ct=ct),
        grid=(tokens // tm,),
        in_specs=in_specs,
        out_specs=out_specs,
        out_shape=out_shape,
        scratch_shapes=scratch,
        compiler_params=_params(VMEM_MID, "parallel"),
        name="inproj_conv_rope" if rope else "inproj_conv",
    )(*args)


def _conv_weights(dw, dwb, lng, lnb, wco, ct):
    rows = ct + 2 * CONV_HALO
    i = np.arange(rows)
    shifts = jnp.asarray(np.stack([(i[None, :] == i[:, None] + ph) for ph in range(1, 8)]), bf16)
    dw_tiles = jnp.broadcast_to(dw[:, None, :], (CONV_WIDTH, 8, CONV_DIM))
    return shifts, dw_tiles, dwb, lng, lnb, wco


def _attn_kernel(q_ref, ckv_ref, kpe_ref, cm_ref, g_ref, x_ref, m_ref, wk_ref, wv_ref, wo_ref, wout_ref, n2_ref,
                 wr_ref, x1_ref, h2_ref, lg_ref, k_scr, v_scr, q_scr, *, sb, hg):
    s_len = ckv_ref.shape[1]

    @pl.when(pl.program_id(1) == 0)
    def _():
        ckv = ckv_ref[...].reshape(sb * s_len, KV_LORA).astype(bf16)
        kpe = kpe_ref[...].reshape(sb * s_len, LANES).astype(f32)
        k = _dot(ckv, wk_ref[...])
        v = _dot(ckv, wv_ref[...])
        ones = jnp.ones((s_len, V_DIM), bf16)
        for g in range(sb):
            rows = slice(g * s_len, (g + 1) * s_len)
            for hd in range(N_HEADS):
                u = g * N_HEADS + hd
                k_scr[u] = (k[rows, hd * HEAD_W:(hd + 1) * HEAD_W] + kpe[rows, :]).astype(bf16)
                vh = v[rows, hd * V_DIM:(hd + 1) * V_DIM].astype(bf16)
                if hd % 2 == 0:
                    v_scr[u, :, 0:V_DIM] = vh
                    v_scr[u, :, V_DIM:LANES] = ones
                else:
                    v_scr[u, :, 0:V_DIM] = ones
                    v_scr[u, :, V_DIM:LANES] = vh

    tq = q_ref.shape[0] // sb
    units = sb * N_HEADS
    for g in range(sb):
        for hd in range(N_HEADS):
            q_scr[g * N_HEADS + hd] = q_ref[g * tq:(g + 1) * tq, hd * HEAD_W:(hd + 1) * HEAD_W]
    lane = lax.broadcasted_iota(jnp.int32, (tq, LANES), 1)
    pairs = []
    for u0 in range(0, units, hg):
        us = slice(u0, u0 + hg)
        s = lax.dot_general(q_scr[us], k_scr[us], (((2,), (2,)), ((0,), (0,))), preferred_element_type=f32)
        mx = jnp.max(s, axis=-1, keepdims=True)
        p = jnp.exp2(s - mx).astype(bf16)
        r = lax.dot_general(p, v_scr[us], (((2,), (1,)), ((0,), (0,))), preferred_element_type=f32)
        for j in range(hg // 2):
            re, ro = r[2 * j], r[2 * j + 1]
            oe = re * (1.0 / re[:, V_DIM:V_DIM + 1])
            oo = ro * (1.0 / ro[:, 0:1])
            pairs.append(jnp.where(lane < V_DIM, oe, oo))
    per_seq = N_HEADS // 2
    attn = jnp.concatenate(
        [jnp.concatenate(pairs[g * per_seq:(g + 1) * per_seq], axis=-1) for g in range(sb)], axis=0).astype(bf16)
    merged = (cm_ref[...].astype(f32) + g_ref[...].astype(f32) * _dot(attn, wo_ref[...])).astype(bf16)

    g1 = m_ref[2:3, :]
    sh2 = m_ref[3:4, :]
    sc2 = m_ref[4:5, :]
    x1 = x_ref[...] + g1 * _dot(merged, wout_ref[...])
    x1_ref[...] = x1
    h2 = _rms(x1, n2_ref[...]) * (1.0 + sc2) + sh2
    hi = h2.astype(bf16)
    h2_ref[...] = hi
    lo = (h2 - hi.astype(f32)).astype(bf16)
    both = _dot(jnp.concatenate([hi, lo], axis=0), wr_ref[...])
    rows = hi.shape[0]
    s = both[0:rows] + both[rows:2 * rows]
    lg_ref[...] = s + pltpu.roll(s, LANES - N_EXPERTS, axis=1)


def _attn(q2d, ckv3, kpe3, cm2d, g2d, x2d, m3, wk, wv, wo, wout, norm2, wr, *, n, tq):
    nseq, s_len, _ = ckv3.shape
    sb = max(1, ATTN_TOKENS // n) if (tq == n and m3.shape[0] == 1) else 1
    units = sb * N_HEADS
    hg = max(2, min(units, SCORE_BYTES // (tq * s_len * 4)))
    assert units % hg == 0 and hg % 2 == 0 and nseq % sb == 0
    qb = n // tq
    tokens = nseq * n
    tile = lambda w: pl.BlockSpec((sb * tq, w), lambda s, i: (s * qb + i, 0))
    return pl.pallas_call(
        functools.partial(_attn_kernel, sb=sb, hg=hg),
        grid=(nseq // sb, qb),
        in_specs=[
            tile(N_HEADS * HEAD_W),
            pl.BlockSpec((sb, s_len, KV_LORA), lambda s, i: (s, 0, 0)),
            pl.BlockSpec((sb, s_len, LANES), lambda s, i: (s, 0, 0)),
            tile(D_MODEL),
            tile(D_MODEL),
            tile(D_MODEL),
            _mod_spec(m3, lambda s, i: s),
            _const_spec(wk.shape),
            _const_spec(wv.shape),
            _const_spec(wo.shape),
            _const_spec(wout.shape),
            _const_spec((1, D_MODEL)),
            _const_spec(wr.shape),
        ],
        out_specs=(tile(D_MODEL), tile(D_MODEL), tile(LANES)),
        out_shape=(
            jax.ShapeDtypeStruct((tokens, D_MODEL), f32),
            jax.ShapeDtypeStruct((tokens, D_MODEL), bf16),
            jax.ShapeDtypeStruct((tokens, LANES), f32),
        ),
        scratch_shapes=[
            pltpu.VMEM((units, s_len, HEAD_W), bf16),
            pltpu.VMEM((units, s_len, LANES), bf16),
            pltpu.VMEM((units, tq, HEAD_W), bf16),
        ],
        compiler_params=_params(VMEM_LIMIT, "parallel", "arbitrary"),
        name="attn_out",
    )(q2d, ckv3, kpe3, cm2d, g2d, x2d, m3, wk, wv, wo, wout, norm2, wr)


def _route_kernel(lg_ref, pos_ref, aff_ref, meta_ref, *, nseq, n, cap):
    for s in range(nseq):
        lt = lg_ref[s].T[0:N_EXPERTS, :]
        e = jnp.exp(lt - jnp.max(lt, axis=0, keepdims=True))
        aff_ref[s * N_EXPERTS:(s + 1) * N_EXPERTS, :] = e / jnp.sum(e, axis=0, keepdims=True)
    rows = nseq * N_EXPERTS
    capf = float(cap)

    def bit_step(i, t):
        cand = t | (jnp.int32(1) << (30 - i))
        thr = lax.bitcast_convert_type(cand, f32)
        cnt = jnp.sum(jnp.where(aff_ref[...] >= thr, 1.0, 0.0), axis=1, keepdims=True)
        return jnp.where(cnt >= capf, cand, t)

    t = lax.fori_loop(0, 31, bit_step, jnp.zeros((rows, 1), jnp.int32))
    thr = lax.bitcast_convert_type(t, f32)
    need = capf - jnp.sum(jnp.where(aff_ref[...] > thr, 1.0, 0.0), axis=1, keepdims=True)

    blk = MOE_BLOCK
    tri = jnp.where(
        lax.broadcasted_iota(jnp.int32, (blk, blk), 0) < lax.broadcasted_iota(jnp.int32, (blk, blk), 1),
        1.0, 0.0).astype(bf16)
    carry_gt = jnp.zeros((rows, 1), f32)
    carry_eq = jnp.zeros((rows, 1), f32)
    lane = lax.broadcasted_iota(jnp.int32, (rows, LANES), 1)
    meta = jnp.zeros((rows, LANES), f32)
    cmax = jnp.zeros((rows, 1), f32)
    for b in range(n // blk):
        sl = slice(b * blk, (b + 1) * blk)
        ab = aff_ref[:, sl]
        gt = ab > thr
        eq = ab == thr
        gtb = jnp.where(gt, 1.0, 0.0)
        eqb = jnp.where(eq, 1.0, 0.0)
        pre_gt = _dot(gtb.astype(bf16), tri) + carry_gt
        pre_eq = _dot(eqb.astype(bf16), tri) + carry_eq
        meta = jnp.where(lane == b, carry_gt + jnp.minimum(carry_eq, need), meta)
        carry_gt = carry_gt + jnp.sum(gtb, axis=1, keepdims=True)
        carry_eq = carry_eq + jnp.sum(eqb, axis=1, keepdims=True)
        sel = gt | (eq & (pre_eq < need))
        slot = pre_gt + jnp.minimum(pre_eq, need)
        pos_ref[:, sl] = jnp.where(sel, slot, -1.0).astype(jnp.int32)
        cmax = jnp.maximum(cmax, jnp.sum(jnp.where(sel, 1.0, 0.0), axis=1, keepdims=True))
    meta_ref[...] = jnp.where(lane == LANES - 1, cmax, meta).astype(jnp.int32)


def _route(lg3, *, cap):
    nseq, n, _ = lg3.shape
    rows = nseq * N_EXPERTS
    assert n // MOE_BLOCK < LANES
    return pl.pallas_call(
        functools.partial(_route_kernel, nseq=nseq, n=n, cap=cap),
        grid=(1,),
        in_specs=[_const_spec(lg3.shape)],
        out_specs=(_const_spec((rows, n)), _const_spec((rows, n)), _const_spec((rows, LANES))),
        out_shape=(jax.ShapeDtypeStruct((rows, n), jnp.int32), jax.ShapeDtypeStruct((rows, n), f32),
                   jax.ShapeDtypeStruct((rows, LANES), jnp.int32)),
        compiler_params=_params(VMEM_SMALL, "arbitrary"),
        name="route",
    )(lg3)


def _slot_hits(pos_ref, e, cap):
    width = pos_ref.shape[1]
    return lax.broadcasted_iota(jnp.int32, (cap, width), 0) == pos_ref[e:e + 1, :]


def _one_hot(hits):
    return jnp.concatenate([jnp.where(h, 1.0, 0.0).astype(bf16) for h in hits], axis=0)


def _gather_dense(pos_ref, aff_ref, h2_ref, xg_ref, vals_ref, rows, cap, ne):
    for e0 in range(0, N_EXPERTS, ne):
        hits = [_slot_hits(pos_ref, e0 + e, cap) for e in range(ne)]
        xg = _dot(_one_hot(hits), h2_ref[...]).astype(xg_ref.dtype)
        for e in range(ne):
            xg_ref[e0 + e, rows, :] = xg[e * cap:(e + 1) * cap, :]
            vals = jnp.sum(jnp.where(hits[e], aff_ref[e0 + e:e0 + e + 1, :], 0.0), axis=1, keepdims=True)
            vals_ref[e0 + e, rows, :] = jnp.broadcast_to(vals, (cap, LANES))


def _gather_kernel(pos_ref, aff_ref, h2_ref, xg_ref, vals_ref, *, cap, ne, sb):
    for g in range(sb):
        _gather_dense(pos_ref.at[g], aff_ref.at[g], h2_ref.at[g], xg_ref, vals_ref,
                      slice(g * cap, (g + 1) * cap), cap, ne)


def _gather(pos3, aff3, h23, *, cap, ne, sb):
    nseq, _, n = pos3.shape
    return pl.pallas_call(
        functools.partial(_gather_kernel, cap=cap, ne=ne, sb=sb),
        grid=(nseq // sb,),
        in_specs=[
            pl.BlockSpec((sb, N_EXPERTS, n), lambda s: (s, 0, 0)),
            pl.BlockSpec((sb, N_EXPERTS, n), lambda s: (s, 0, 0)),
            pl.BlockSpec((sb, n, D_MODEL), lambda s: (s, 0, 0)),
        ],
        out_specs=(
            pl.BlockSpec((N_EXPERTS, sb * cap, D_MODEL), lambda s: (0, s, 0)),
            pl.BlockSpec((N_EXPERTS, sb * cap, LANES), lambda s: (0, s, 0)),
        ),
        out_shape=(
            jax.ShapeDtypeStruct((N_EXPERTS, nseq * cap, D_MODEL), bf16),
            jax.ShapeDtypeStruct((N_EXPERTS, nseq * cap, LANES), f32),
        ),
        compiler_params=_params(VMEM_MID, "parallel"),
        name="gather",
    )(pos3, aff3, h23)


def _windows_fit(meta_ref):
    cmax = meta_ref[0, LANES - 1]
    for e in range(1, N_EXPERTS):
        cmax = jnp.maximum(cmax, meta_ref[e, LANES - 1])
    return cmax <= MOE_WIN - WIN_ALIGN


def _win_base(meta_ref, e, b, cap):
    start = meta_ref[e, b]
    return pl.multiple_of(jnp.minimum(start - start % WIN_ALIGN, cap - MOE_WIN), WIN_ALIGN)


def _win_hits(pos_row, base):
    slot = lax.broadcasted_iota(jnp.int32, (MOE_WIN, pos_row.shape[1]), 0) + base
    return slot == pos_row


def _gather_win_kernel(meta_ref, pos_ref, aff_ref, h2_ref, xg_ref, vals_ref, *, n, cap):
    fits = _windows_fit(meta_ref)

    @pl.when(fits)
    def _():
        xg_ref[...] = jnp.zeros(xg_ref.shape, xg_ref.dtype)
        vals_ref[...] = jnp.zeros(vals_ref.shape, vals_ref.dtype)

        def block(b, carry):
            cols = pl.ds(pl.multiple_of(b * MOE_BLOCK, MOE_BLOCK), MOE_BLOCK)
            bases = [_win_base(meta_ref, e, b, cap) for e in range(N_EXPERTS)]
            hits = [_win_hits(pos_ref[e:e + 1, cols], bases[e]) for e in range(N_EXPERTS)]
            part = _dot(_one_hot(hits), h2_ref[cols, :])
            for e in range(N_EXPERTS):
                rows = pl.ds(bases[e], MOE_WIN)
                xg_ref[e, rows, :] += part[e * MOE_WIN:(e + 1) * MOE_WIN, :].astype(xg_ref.dtype)
                vals = jnp.sum(jnp.where(hits[e], aff_ref[e:e + 1, cols], 0.0), axis=1, keepdims=True)
                vals_ref[e, rows, :] += jnp.broadcast_to(vals, (MOE_WIN, LANES))
            return carry

        lax.fori_loop(0, n // MOE_BLOCK, block, 0)

    @pl.when(jnp.logical_not(fits))
    def _():
        _gather_dense(pos_ref, aff_ref, h2_ref, xg_ref, vals_ref, slice(0, cap), cap, 1)


def _gather_win(meta, pos3, aff3, h23, *, cap):
    nseq, _, n = pos3.shape
    return pl.pallas_call(
        functools.partial(_gather_win_kernel, n=n, cap=cap),
        grid=(nseq,),
        in_specs=[
            pl.BlockSpec((N_EXPERTS, LANES), lambda s: (s, 0), memory_space=pltpu.SMEM),
            pl.BlockSpec((None, N_EXPERTS, n), lambda s: (s, 0, 0)),
            pl.BlockSpec((None, N_EXPERTS, n), lambda s: (s, 0, 0)),
            pl.BlockSpec((None, n, D_MODEL), lambda s: (s, 0, 0)),
        ],
        out_specs=(
            pl.BlockSpec((N_EXPERTS, cap, D_MODEL), lambda s: (0, s, 0)),
            pl.BlockSpec((N_EXPERTS, cap, LANES), lambda s: (0, s, 0)),
        ),
        out_shape=(
            jax.ShapeDtypeStruct((N_EXPERTS, nseq * cap, D_MODEL), bf16),
            jax.ShapeDtypeStruct((N_EXPERTS, nseq * cap, LANES), f32),
        ),
        compiler_params=_params(VMEM_LARGE, "parallel"),
        name="gather_win",
    )(meta, pos3, aff3, h23)


def _experts_kernel(xp_ref, vp_ref, xs_ref, vs_ref, wg_ref, wu_ref, wd_ref, yp_ref, ys_ref, *, rc):
    wg = wg_ref[...].astype(bf16)
    wu = wu_ref[...].astype(bf16)
    wd = wd_ref[...].astype(bf16)
    for x_ref, v_ref, y_ref in ((xp_ref, vp_ref, yp_ref), (xs_ref, vs_ref, ys_ref)):
        for r0 in range(0, x_ref.shape[0], rc):
            x = x_ref[r0:r0 + rc, :]
            a = _dot(x, wg)
            u = _dot(x, wu)
            hm = (a * jax.nn.sigmoid(a) * u).astype(bf16)
            y = _dot(hm, wd) * v_ref[r0:r0 + rc, 0:1]
            y_ref[r0:r0 + rc, :] = y.astype(y_ref.dtype)


def _experts(xg_p, vals_p, xg_s, vals_s, wg, wu, wd, *, rc=EXPERT_ROWS):
    rp = xg_p.shape[1]
    rs = xg_s.shape[1]
    per_e = lambda r, w: pl.BlockSpec((None, r, w), lambda e: (e, 0, 0))
    return pl.pallas_call(
        functools.partial(_experts_kernel, rc=rc),
        grid=(N_EXPERTS,),
        in_specs=[
            per_e(rp, D_MODEL), per_e(rp, LANES), per_e(rs, D_MODEL), per_e(rs, LANES),
            per_e(D_MODEL, EXPERT_FF), per_e(D_MODEL, EXPERT_FF), per_e(EXPERT_FF, D_MODEL),
        ],
        out_specs=(per_e(rp, D_MODEL), per_e(rs, D_MODEL)),
        out_shape=(
            jax.ShapeDtypeStruct((N_EXPERTS, rp, D_MODEL), bf16),
            jax.ShapeDtypeStruct((N_EXPERTS, rs, D_MODEL), bf16),
        ),
        compiler_params=_params(VMEM_LARGE, "parallel"),
        name="experts",
    )(xg_p, vals_p, xg_s, vals_s, wg, wu, wd)


def _scatter_dense(pos_ref, y_ref, rows, tn, cap, ne):
    moe = jnp.zeros((tn, D_MODEL), f32)
    for e0 in range(0, N_EXPERTS, ne):
        onehot = _one_hot([_slot_hits(pos_ref, e0 + e, cap) for e in range(ne)])
        y = jnp.concatenate([y_ref[e0 + e, rows, :] for e in range(ne)], axis=0)
        moe = moe + lax.dot_general(onehot, y, (((0,), (0,)), ((), ())), preferred_element_type=f32)
    return moe


SCATTER_RING = 3


def _scatter_kernel(pos_ref, y_hbm, x1_hbm, m_ref, fn_ref, o_ref, ybuf, xbuf, sems, *, tn, cap, ne, sb, steps):
    t = pl.program_id(0)

    def copies(step):
        slot = step % SCATTER_RING
        return (
            pltpu.make_async_copy(y_hbm.at[:, pl.ds(step * sb * cap, sb * cap), :], ybuf.at[slot], sems.at[0, slot]),
            pltpu.make_async_copy(x1_hbm.at[pl.ds(step * sb, sb)], xbuf.at[slot], sems.at[1, slot]),
        )

    @pl.when(t == 0)
    def _():
        for k in range(min(SCATTER_RING - 1, steps)):
            for c in copies(k):
                c.start()

    @pl.when(t + SCATTER_RING - 1 < steps)
    def _():
        for c in copies(t + SCATTER_RING - 1):
            c.start()

    for c in copies(t):
        c.wait()
    y_ref = ybuf.at[t % SCATTER_RING]
    x1_ref = xbuf.at[t % SCATTER_RING]
    g2 = m_ref[5:6, :]
    for g in range(sb):
        moe = _scatter_dense(pos_ref.at[g], y_ref, slice(g * cap, (g + 1) * cap), tn, cap, ne)
        o_ref[g] = _rms(x1_ref[g] + g2 * moe, fn_ref[...])


def _scatter(pos3, y, x13, m3, fn, *, cap, ne, tn, sb):
    nseq, _, n = pos3.shape
    assert m3.shape[0] == 1, "sequences sharing a grid step must share their modulation rows"
    assert tn == n, "one grid step covers whole sequences"
    steps = nseq // sb
    return pl.pallas_call(
        functools.partial(_scatter_kernel, tn=tn, cap=cap, ne=ne, sb=sb, steps=steps),
        grid=(steps,),
        in_specs=[
            pl.BlockSpec((sb, N_EXPERTS, tn), lambda s: (s, 0, 0)),
            pl.BlockSpec(memory_space=pl.ANY),
            pl.BlockSpec(memory_space=pl.ANY),
            _mod_spec(m3, lambda s: s),
            _const_spec((1, D_MODEL)),
        ],
        out_specs=pl.BlockSpec((sb, tn, D_MODEL), lambda s: (s, 0, 0)),
        out_shape=jax.ShapeDtypeStruct((nseq, n, D_MODEL), f32),
        scratch_shapes=[
            pltpu.VMEM((SCATTER_RING, N_EXPERTS, sb * cap, D_MODEL), y.dtype),
            pltpu.VMEM((SCATTER_RING, sb, tn, D_MODEL), f32),
            pltpu.SemaphoreType.DMA((2, SCATTER_RING)),
        ],
        compiler_params=_params(VMEM_LIMIT, "arbitrary"),
        name="scatter",
    )(pos3, y, x13, m3, fn)


def _scatter_win_kernel(meta_ref, pos_ref, y_ref, x1_ref, m_ref, fn_ref, o_ref, moe_scr, *, cap):
    b = pl.program_id(1)
    fits = _windows_fit(meta_ref)

    @pl.when(fits)
    def _():
        bases = [_win_base(meta_ref, e, b, cap) for e in range(N_EXPERTS)]
        onehot = _one_hot([_win_hits(pos_ref[e:e + 1, :], bases[e]) for e in range(N_EXPERTS)])
        y = jnp.concatenate([y_ref[e, pl.ds(bases[e], MOE_WIN), :] for e in range(N_EXPERTS)], axis=0)
        moe_scr[...] = lax.dot_general(onehot, y, (((0,), (0,)), ((), ())), preferred_element_type=f32)

    @pl.when(jnp.logical_not(fits))
    def _():
        moe_scr[...] = _scatter_dense(pos_ref, y_ref, slice(0, cap), MOE_BLOCK, cap, 1)

    o_ref[...] = _rms(x1_ref[...] + m_ref[5:6, :] * moe_scr[...], fn_ref[...])


def _scatter_win(meta, pos3, y, x13, m3, fn, *, cap):
    nseq, _, n = pos3.shape
    tn = MOE_BLOCK
    return pl.pallas_call(
        functools.partial(_scatter_win_kernel, cap=cap),
        grid=(nseq, n // tn),
        in_specs=[
            pl.BlockSpec((N_EXPERTS, LANES), lambda s, i: (s, 0), memory_space=pltpu.SMEM),
            pl.BlockSpec((None, N_EXPERTS, tn), lambda s, i: (s, 0, i)),
            pl.BlockSpec((N_EXPERTS, cap, D_MODEL), lambda s, i: (0, s, 0)),
            pl.BlockSpec((None, tn, D_MODEL), lambda s, i: (s, i, 0)),
            _mod_spec(m3, lambda s, i: s),
            _const_spec((1, D_MODEL)),
        ],
        out_specs=pl.BlockSpec((None, tn, D_MODEL), lambda s, i: (s, i, 0)),
        out_shape=jax.ShapeDtypeStruct((nseq, n, D_MODEL), f32),
        scratch_shapes=[pltpu.VMEM((tn, D_MODEL), f32)],
        compiler_params=_params(VMEM_MID, "parallel", "arbitrary"),
        name="scatter_win",
    )(meta, pos3, y, x13, m3, fn)


def _rope_tables(n):
    t = np.arange(n)
    half = QK_ROPE // 2
    freqs = ROPE_BASE ** (-np.arange(0, half, 2, dtype=np.float64) / half)
    ang_r = (t // GRID_W)[:, None] * freqs
    ang_c = (t % GRID_W)[:, None] * freqs
    cr, sr, cc, sc = np.cos(ang_r), np.sin(ang_r), np.cos(ang_c), np.sin(ang_c)
    cos = np.ones((n, HEAD_W))
    sin = np.zeros((n, HEAD_W))
    cos[:, ROPE_OFF:ROPE_OFF + QK_ROPE] = np.concatenate([cr, cr, cc, cc], axis=-1)
    sin[:, ROPE_OFF:ROPE_OFF + QK_ROPE] = np.concatenate([-sr, sr, -sc, sc], axis=-1)
    return jnp.asarray(cos, f32), jnp.asarray(sin, f32)


_PARTNER = np.concatenate([np.arange(8, 16), np.arange(0, 8), np.arange(24, 32), np.arange(16, 24)])


def _rope_partner(w):
    q = QK_ROPE // 4
    return jnp.concatenate([w[..., q:2 * q], w[..., 0:q], w[..., 3 * q:4 * q], w[..., 2 * q:3 * q]], axis=-1)


def _rope_placement():
    place = np.zeros((LANES, 2 * LANES), np.float32)
    d = np.arange(QK_ROPE)
    place[d, ROPE_OFF + d] = 1.0
    place[_PARTNER, LANES + ROPE_OFF + d] = 1.0
    return jnp.asarray(place, bf16)


def _head_blocks(w_nope, w_rope):
    rows = w_nope.shape[0]
    if w_rope is None:
        w_rope = jnp.zeros((rows, N_HEADS, QK_ROPE), w_nope.dtype)
    z = jnp.zeros((rows, N_HEADS, HEAD_W - QK_NOPE - QK_ROPE), w_nope.dtype)
    return jnp.concatenate([w_nope, w_rope, z], axis=-1).reshape(rows, N_HEADS * HEAD_W)


def kernel(x_prompt, x_sample, cache_ckv, cache_kpe, c, c_ctx, w_ada, b_ada, norm1, w_in, conv_dw, conv_dw_b,
           conv_ln_g, conv_ln_b, w_conv_out, q_norm, w_qb, kv_norm, w_kvb, w_o_mla, w_out, norm2, w_router,
           w_e_gate, w_e_up, w_e_down, final_norm):
    assert w_ada.shape[0] == 1, "single trunk layer"
    nb_p, n_p, _ = x_prompt.shape
    nb_s, n_s, _ = x_sample.shape

    win = _wprep(w_in[0].T, _rope_placement())
    wq = w_qb[0].reshape(Q_LORA, N_HEADS, QK_NOPE + QK_ROPE)
    wqb = _head_blocks(wq[..., :QK_NOPE], wq[..., QK_NOPE:]).astype(bf16)
    wqbp = _head_blocks(jnp.zeros_like(wq[..., :QK_NOPE]), _rope_partner(wq[..., QK_NOPE:])).astype(bf16)
    wkv = w_kvb[0].reshape(KV_LORA, N_HEADS, QK_NOPE + V_DIM)
    wk = _head_blocks(wkv[..., :QK_NOPE], None).astype(bf16)
    wv = wkv[..., QK_NOPE:].reshape(KV_LORA, N_HEADS * V_DIM).astype(bf16)
    wco = w_conv_out[0].astype(bf16)
    wo = w_o_mla[0].astype(bf16)
    wout = w_out[0].astype(bf16)
    wr_hi = w_router[0].astype(bf16)
    wr_lo = (w_router[0] - wr_hi.astype(f32)).astype(bf16)
    wr = jnp.concatenate([wr_hi, wr_lo, jnp.zeros((D_MODEL, LANES - 2 * N_EXPERTS), bf16)], axis=-1)
    row = lambda a: a.reshape(1, -1)

    mod = jnp.concatenate([c_ctx[None, :], c, jnp.zeros((8 - 1 - nb_s, D_MODEL), f32)], axis=0)
    m = _ada(mod, w_ada[0], b_ada[0]).reshape(8, 6, D_MODEL)
    m_p, m_s = m[0:1], m[1:1 + nb_s]
    cos, sin = _rope_tables(n_s)

    conv_w = _conv_weights(conv_dw[0], row(conv_dw_b[0]), row(conv_ln_g[0]), row(conv_ln_b[0]), wco, CONV_CHUNK)

    def mixers(x, m3, rope, ctx_ckv, ctx_kpe):
        nseq, n, _ = x.shape
        x2d = x.reshape(nseq * n, D_MODEL)
        cm, q, ckv, kr, g, *kpe = _inproj(x2d, m3, row(norm1[0]), win, row(q_norm[0]), wqb, wqbp, row(kv_norm[0]),
                                          cos, sin, conv_w, n=n, rope=rope, tm=IN_TILE)
        keys_ckv = ckv.reshape(nseq, n, KV_LORA)
        keys_kpe = kr.reshape(nseq, n, LANES)
        if ctx_ckv is not None:
            keys_ckv = jnp.concatenate([ctx_ckv.astype(keys_ckv.dtype), keys_ckv], axis=1)
            keys_kpe = jnp.concatenate([ctx_kpe.astype(keys_kpe.dtype), keys_kpe], axis=1)
        x1, h2, lg = _attn(q, keys_ckv, keys_kpe, cm.reshape(nseq * n, D_MODEL), g, x2d, m3, wk, wv, wo, wout,
                           row(norm2[0]), wr, n=n, tq=min(n, Q_TILE))
        return x1, h2, lg, ckv, kpe

    ctx_kpe = jnp.pad(cache_kpe[:, 0], ((0, 0), (0, 0), (ROPE_OFF, LANES - ROPE_OFF - QK_ROPE)))
    x1_p, h2_p, lg_p, ckv_p, (kpe_p,) = mixers(x_prompt, m_p, False, None, None)
    x1_s, h2_s, lg_s, _, _ = mixers(x_sample, m_s, True, cache_ckv[:, 0], ctx_kpe)

    def moe_tiles(n):
        cap = EC_FACTOR * n // N_EXPERTS
        ne = N_EXPERTS if N_EXPERTS * cap <= MOE_ROWS else 1
        sb = max(1, MOE_TOKENS // n)
        return cap, ne, sb

    def windowed(n):
        return n >= 4 * MOE_BLOCK and EC_FACTOR * n // N_EXPERTS >= 2 * MOE_WIN

    def route_gather(h2, lg, nseq, n):
        cap, ne, sb = moe_tiles(n)
        pos, aff, meta = _route(lg.reshape(nseq, n, LANES), cap=cap)
        pos3 = pos.reshape(nseq, N_EXPERTS, n)
        aff3 = aff.reshape(nseq, N_EXPERTS, n)
        h23 = h2.reshape(nseq, n, D_MODEL)
        if windowed(n):
            xg, vals = _gather_win(meta, pos3, aff3, h23, cap=cap)
        else:
            xg, vals = _gather(pos3, aff3, h23, cap=cap, ne=ne, sb=sb)
        return pos3, meta, xg, vals

    pos_p, meta_p, xg_p, vals_p = route_gather(h2_p, lg_p, nb_p, n_p)
    pos_s, meta_s, xg_s, vals_s = route_gather(h2_s, lg_s, nb_s, n_s)
    y_p, y_s = _experts(xg_p, vals_p, xg_s, vals_s, w_e_gate[0], w_e_up[0], w_e_down[0])
    fn = row(final_norm)

    def scatter(pos, meta, y, x1, m3, nseq, n):
        cap, ne, sb = moe_tiles(n)
        x13 = x1.reshape(nseq, n, D_MODEL)
        if windowed(n):
            return _scatter_win(meta, pos, y, x13, m3, fn, cap=cap)
        return _scatter(pos, y, x13, m3, fn, cap=cap, ne=ne, tn=min(n, SCATTER_TILE), sb=sb)

    y_prompt = scatter(pos_p, meta_p, y_p, x1_p, m_p, nb_p, n_p)
    y_sample = scatter(pos_s, meta_s, y_s, x1_s, m_s, nb_s, n_s)

    new_ckv = ckv_p.reshape(nb_p, 1, n_p, KV_LORA)
    new_kpe = kpe_p.reshape(nb_p, 1, n_p, QK_ROPE)
    return (y_prompt, y_sample, new_ckv, new_kpe)
```

```python
import functools

import jax
import jax.numpy as jnp
import numpy as np
from jax import lax
from jax.experimental import pallas as pl
from jax.experimental.pallas import tpu as pltpu

D_MODEL = 1024
GRID_W = 64
CONV_DIM = 512
CONV_WIDTH = 31
N_HEADS = 8
QK_NOPE = 64
QK_ROPE = 32
V_DIM = 64
Q_LORA = 256
KV_LORA = 128
N_EXPERTS = 16
EXPERT_FF = 512
EC_FACTOR = 2
ROPE_BASE = 10000.0
EPS = 1e-6

LANES = 128
HEAD_W = LANES
ROPE_OFF = QK_NOPE
CONV_HALO = 16
LOG2E = 1.4426950408889634
VMEM_LIMIT = 48 * 1024 * 1024
IN_TILE = 512
CONV_CHUNK = 64
CONV_OUT_ROWS = 256
GATE_CHUNK = 512
EXPERT_ROWS = 512
Q_TILE = 512
ATTN_TOKENS = 1024
SCORE_BYTES = 12 * 1024 * 1024
MOE_ROWS = 512
MOE_TOKENS = 1024
SCATTER_TILE = 512
MOE_BLOCK = 2 * LANES
MOE_WIN = 80
WIN_ALIGN = 16

C_CONV = 0
C_QA = 2 * CONV_DIM
C_KVA = C_QA + Q_LORA
C_KR = C_KVA + KV_LORA
C_GATE = C_KR + LANES
C_END = C_GATE + 2 * D_MODEL
C_KRP = C_END

f32 = jnp.float32
bf16 = jnp.bfloat16


MIB = 1024 * 1024
VMEM_SMALL, VMEM_MID, VMEM_LARGE = 16 * MIB, 32 * MIB, 40 * MIB


def _params(vmem_bytes, *sem):
    assert vmem_bytes <= VMEM_LIMIT
    return pltpu.CompilerParams(dimension_semantics=sem, vmem_limit_bytes=vmem_bytes)


def _dot(a, b):
    return jnp.dot(a, b, preferred_element_type=f32)


def _rms(x, g):
    return x * lax.rsqrt(jnp.mean(x * x, axis=-1, keepdims=True) + EPS) * g


def _const_spec(shape):
    nd = len(shape)
    return pl.BlockSpec(shape, lambda *_: (0,) * nd)


def _mod_spec(m3, seq_of):
    if m3.shape[0] == 1:
        return _const_spec((None, 6, D_MODEL))
    return pl.BlockSpec((None, 6, D_MODEL), lambda *idx: (seq_of(*idx), 0, 0))


def _ada_kernel(s_ref, w_ref, b_ref, o_ref):
    s = s_ref[...]
    s = s * jax.nn.sigmoid(s)
    o_ref[...] = _dot(s.astype(bf16), w_ref[...].astype(bf16)) + b_ref[...]


def _ada(mod, w_ada, b_ada):
    rows = mod.shape[0]
    n_out = w_ada.shape[1]
    tn = D_MODEL
    return pl.pallas_call(
        _ada_kernel,
        grid=(n_out // tn,),
        in_specs=[
            _const_spec((rows, D_MODEL)),
            pl.BlockSpec((D_MODEL, tn), lambda j: (0, j)),
            pl.BlockSpec((1, tn), lambda j: (0, j)),
        ],
        out_specs=pl.BlockSpec((rows, tn), lambda j: (0, j)),
        out_shape=jax.ShapeDtypeStruct((rows, n_out), f32),
        compiler_params=_params(VMEM_LIMIT, "arbitrary"),
        name="ada",
    )(mod, w_ada, b_ada.reshape(1, n_out))


def _wprep_kernel(wt_ref, place_ref, o_ref):
    def block(r0):
        return wt_ref[r0:r0 + LANES, :].T.astype(bf16)

    for j in range(C_KR // LANES):
        o_ref[:, j * LANES:(j + 1) * LANES] = block(j * LANES)
    placed = _dot(block(C_KR), place_ref[...])
    o_ref[:, C_KR:C_GATE] = placed[:, 0:LANES].astype(bf16)
    o_ref[:, C_KRP:C_KRP + LANES] = placed[:, LANES:2 * LANES].astype(bf16)
    for j in range(2 * D_MODEL // LANES):
        o_ref[:, C_GATE + j * LANES:C_GATE + (j + 1) * LANES] = block(C_KR + QK_ROPE + j * LANES)


def _wprep(w_in_t, place):
    cols, rows = w_in_t.shape
    return pl.pallas_call(
        _wprep_kernel,
        grid=(rows // LANES,),
        in_specs=[pl.BlockSpec((cols, LANES), lambda i: (0, i)), _const_spec(place.shape)],
        out_specs=pl.BlockSpec((LANES, C_KRP + LANES), lambda i: (i, 0)),
        out_shape=jax.ShapeDtypeStruct((rows, C_KRP + LANES), bf16),
        compiler_params=_params(VMEM_SMALL, "parallel"),
        name="wprep",
    )(w_in_t, place)


def _conv_taps(vpad, r0, shift_ref, dw_ref, dwb_ref, ybuf, y0, ct):
    pad = CONV_WIDTH // 2
    sub = 8
    span = ((CONV_HALO - pad + CONV_WIDTH - 1) // sub) * sub
    cw = 2 * LANES
    for cb in range(CONV_DIM // cw):
        sl = slice(cb * cw, (cb + 1) * cw)
        win = vpad[pl.ds(r0, ct + 2 * CONV_HALO), sl]
        acc = jnp.zeros((ct // sub, sub, cw), f32)
        for ph in range(sub):
            wph = win.astype(f32) if ph == 0 else _dot(shift_ref[ph - 1], win)
            for a in range(span // sub + 1):
                k = a * sub + ph - (CONV_HALO - pad)
                if 0 <= k < CONV_WIDTH:
                    acc = acc + wph[a * sub:a * sub + ct, :].reshape(ct // sub, sub, cw) * dw_ref[k, :, sl][None]
        ybuf[pl.ds(y0, ct), sl] = acc.reshape(ct, cw) + dwb_ref[:, sl]


def _conv_out(y, lng_ref, lnb_ref, wco_ref, gate):
    mu = jnp.mean(y, axis=-1, keepdims=True)
    yc = y - mu
    var = jnp.mean(yc * yc, axis=-1, keepdims=True)
    z = yc * lax.rsqrt(var + EPS) * lng_ref[...] + lnb_ref[...]
    z = z * jax.nn.sigmoid(z)
    return (gate.astype(f32) * _dot(z.astype(bf16), wco_ref[...])).astype(bf16)


def _fill_padded(vpad, v, n):
    zeros = jnp.zeros((CONV_HALO, CONV_DIM), vpad.dtype)
    vpad[0:CONV_HALO, :] = zeros
    vpad[CONV_HALO + n:2 * CONV_HALO + n, :] = zeros
    vpad[CONV_HALO:CONV_HALO + n, :] = v


def _inproj_kernel(*refs, rope, halo, tiles_per_seq, q_scale, n, rt, ct):
    refs = list(refs)
    x_ref = refs.pop(0)
    if halo:
        xp_ref, xn_ref = refs[:2]
        del refs[:2]
    m_ref, n1_ref, win_ref, qn_ref, wqb_ref = refs[:5]
    del refs[:5]
    if rope:
        wqbp_ref = refs.pop(0)
    kvn_ref = refs.pop(0)
    if rope:
        cos_ref, sin_ref = refs[:2]
        del refs[:2]
    shift_ref, dw_ref, dwb_ref, lng_ref, lnb_ref, wco_ref = refs[:6]
    del refs[:6]
    cm_ref, q_ref, ckv_ref, kr_ref, g_ref = refs[:5]
    del refs[:5]
    if not rope:
        kpe_ref = refs.pop(0)
    vpad, ybuf, gc = refs
    tm = x_ref.shape[0]
    lo = CONV_HALO if halo else 0

    sh1 = m_ref[0:1, :]
    sc1 = m_ref[1:2, :]
    x = jnp.concatenate([xp_ref[...], x_ref[...], xn_ref[...]], axis=0) if halo else x_ref[...]
    hb_all = (_rms(x, n1_ref[...]) * (1.0 + sc1) + sh1).astype(bf16)

    left_all = _dot(hb_all, win_ref[:, 0:C_GATE])
    v = (left_all[:, C_CONV:C_CONV + CONV_DIM] * jax.nn.sigmoid(left_all[:, C_CONV + CONV_DIM:C_QA])).astype(bf16)
    hb = hb_all[lo:lo + tm, :]
    left = left_all[lo:lo + tm, :]

    if halo:
        j = pl.program_id(0) % tiles_per_seq
        zeros = jnp.zeros((CONV_HALO, CONV_DIM), bf16)
        vpad[0, 0:lo, :] = jnp.where(j > 0, v[0:lo, :], zeros)
        vpad[0, lo:lo + tm, :] = v[lo:lo + tm, :]
        vpad[0, lo + tm:2 * lo + tm, :] = jnp.where(j < tiles_per_seq - 1, v[lo + tm:2 * lo + tm, :], zeros)
        chunks = [(0, c * ct, c * ct) for c in range(tm // ct)]
    else:
        for s in range(tm // n):
            _fill_padded(vpad.at[s], v[s * n:(s + 1) * n, :], n)
        chunks = [(s, c * ct, s * n + c * ct) for s in range(tm // n) for c in range(n // ct)]

    def conv_some(count):
        for _ in range(min(count, len(chunks))):
            s, r0, y0 = chunks.pop(0)
            _conv_taps(vpad.at[s], r0, shift_ref, dw_ref, dwb_ref, ybuf, y0, ct)

    steps = 2 + 2 * D_MODEL // GATE_CHUNK
    per_step = -(-len(chunks) // steps)

    qa = left[:, C_QA:C_KVA]
    qn = _rms(qa, qn_ref[...]).astype(bf16)
    q = _dot(qn, wqb_ref[...])
    if rope:
        qp = _dot(qn, wqbp_ref[...])
        cos = cos_ref[...]
        sin = sin_ref[...]
        for hd in range(N_HEADS):
            sl = slice(hd * HEAD_W, (hd + 1) * HEAD_W)
            q_ref[:, sl] = ((q[:, sl] * cos + qp[:, sl] * sin) * q_scale).astype(q_ref.dtype)
    else:
        q_ref[...] = (q * q_scale).astype(q_ref.dtype)
    conv_some(per_step)

    kva = left[:, C_KVA:C_KR]
    ckv_ref[...] = _rms(kva, kvn_ref[...]).astype(ckv_ref.dtype)

    kr = left[:, C_KR:C_GATE]
    if rope:
        krp = _dot(hb, win_ref[:, C_KRP:C_KRP + LANES])
        kr = kr * cos_ref[...] + krp * sin_ref[...]
    else:
        kpe_ref[...] = kr[:, ROPE_OFF:ROPE_OFF + QK_ROPE]
    kr_ref[...] = kr.astype(kr_ref.dtype)
    conv_some(per_step)

    gw = GATE_CHUNK
    for jg in range(2 * D_MODEL // gw):
        gate = jax.nn.sigmoid(_dot(hb, win_ref[:, C_GATE + jg * gw:C_GATE + (jg + 1) * gw])).astype(bf16)
        if jg * gw < D_MODEL:
            gc[:, jg * gw:(jg + 1) * gw] = gate
        else:
            g_ref[:, jg * gw - D_MODEL:(jg + 1) * gw - D_MODEL] = gate
        conv_some(per_step)

    conv_some(len(chunks))
    for r0 in range(0, tm, rt):
        cm_ref[r0:r0 + rt, :] = _conv_out(ybuf[r0:r0 + rt, :], lng_ref, lnb_ref, wco_ref, gc[r0:r0 + rt, :])


def _inproj(x2d, m3, norm1, win, q_norm, wqb, wqbp, kv_norm, cos, sin, conv_w, *, n, rope, tm,
            rt=CONV_OUT_ROWS, ct=CONV_CHUNK):
    tokens = x2d.shape[0]
    halo = n > tm
    assert n % tm == 0 if halo else (tm % n == 0 and m3.shape[0] == 1 and not rope)
    tiles_per_seq = max(1, n // tm)
    q_scale = float((QK_NOPE + QK_ROPE) ** -0.5 * LOG2E)
    tile = lambda w: pl.BlockSpec((tm, w), lambda i: (i, 0))
    in_specs = [tile(D_MODEL)]
    args = [x2d]
    if halo:
        hb_tile = tm // CONV_HALO
        hb_seq = n // CONV_HALO
        prev = lambda i: (jnp.maximum(i * hb_tile - 1, (i // tiles_per_seq) * hb_seq), 0)
        nxt = lambda i: (jnp.minimum((i + 1) * hb_tile, (i // tiles_per_seq + 1) * hb_seq - 1), 0)
        in_specs += [pl.BlockSpec((CONV_HALO, D_MODEL), prev), pl.BlockSpec((CONV_HALO, D_MODEL), nxt)]
        args += [x2d, x2d]
    in_specs += [
        _mod_spec(m3, lambda i: i // tiles_per_seq),
        _const_spec((1, D_MODEL)),
        _const_spec((D_MODEL, C_KRP + LANES if rope else C_END)),
        _const_spec((1, Q_LORA)),
        _const_spec(wqb.shape),
    ]
    args += [m3, norm1, win, q_norm, wqb]
    if rope:
        in_specs.append(_const_spec(wqbp.shape))
        args.append(wqbp)
    in_specs.append(_const_spec((1, KV_LORA)))
    args.append(kv_norm)
    if rope:
        tab = pl.BlockSpec((tm, LANES), lambda i: (i % tiles_per_seq, 0))
        in_specs += [tab, tab]
        args += [cos, sin]
    args += list(conv_w)
    in_specs += [_const_spec(w.shape) for w in conv_w]
    pieces = max(1, tm // n)
    scratch = [
        pltpu.VMEM((pieces, tm // pieces + 2 * CONV_HALO, CONV_DIM), bf16),
        pltpu.VMEM((tm, CONV_DIM), f32),
        pltpu.VMEM((tm, D_MODEL), bf16),
    ]
    out_shape = [
        jax.ShapeDtypeStruct((tokens, D_MODEL), bf16),
        jax.ShapeDtypeStruct((tokens, N_HEADS * HEAD_W), bf16),
        jax.ShapeDtypeStruct((tokens, KV_LORA), bf16 if rope else f32),
        jax.ShapeDtypeStruct((tokens, LANES), bf16),
        jax.ShapeDtypeStruct((tokens, D_MODEL), bf16),
    ]
    out_specs = [tile(D_MODEL), tile(N_HEADS * HEAD_W), tile(KV_LORA), tile(LANES), tile(D_MODEL)]
    if not rope:
        out_shape.append(jax.ShapeDtypeStruct((tokens, QK_ROPE), f32))
        out_specs.append(tile(QK_ROPE))
    return pl.pallas_call(
        functools.partial(_inproj_kernel, rope=rope, halo=halo, tiles_per_seq=tiles_per_seq, q_scale=q_scale,
                          n=n, rt=rt, ct=ct),
        grid=(tokens // tm,),
        in_specs=in_specs,
        out_specs=out_specs,
        out_shape=out_shape,
        scratch_shapes=scratch,
        compiler_params=_params(VMEM_MID, "parallel"),
        name="inproj_conv_rope" if rope else "inproj_conv",
    )(*args)


def _conv_weights(dw, dwb, lng, lnb, wco, ct):
    rows = ct + 2 * CONV_HALO
    i = np.arange(rows)
    shifts = jnp.asarray(np.stack([(i[None, :] == i[:, None] + ph) for ph in range(1, 8)]), bf16)
    dw_tiles = jnp.broadcast_to(dw[:, None, :], (CONV_WIDTH, 8, CONV_DIM))
    return shifts, dw_tiles, dwb, lng, lnb, wco


def _attn_kernel(q_ref, ckv_ref, kpe_ref, cm_ref, g_ref, x_ref, m_ref, wk_ref, wv_ref, wo_ref, wout_ref, n2_ref,
                 wr_ref, x1_ref, h2_ref, lg_ref, k_scr, v_scr, q_scr, *, sb, hg):
    s_len = ckv_ref.shape[1]

    @pl.when(pl.program_id(1) == 0)
    def _():
        ckv = ckv_ref[...].reshape(sb * s_len, KV_LORA).astype(bf16)
        kpe = kpe_ref[...].reshape(sb * s_len, LANES).astype(f32)
        k = _dot(ckv, wk_ref[...])
        v = _dot(ckv, wv_ref[...])
        ones = jnp.ones((s_len, V_DIM), bf16)
        for g in range(sb):
            rows = slice(g * s_len, (g + 1) * s_len)
            for hd in range(N_HEADS):
                u = g * N_HEADS + hd
                k_scr[u] = (k[rows, hd * HEAD_W:(hd + 1) * HEAD_W] + kpe[rows, :]).astype(bf16)
                vh = v[rows, hd * V_DIM:(hd + 1) * V_DIM].astype(bf16)
                if hd % 2 == 0:
                    v_scr[u, :, 0:V_DIM] = vh
                    v_scr[u, :, V_DIM:LANES] = ones
                else:
                    v_scr[u, :, 0:V_DIM] = ones
                    v_scr[u, :, V_DIM:LANES] = vh

    tq = q_ref.shape[0] // sb
    units = sb * N_HEADS
    for g in range(sb):
        for hd in range(N_HEADS):
            q_scr[g * N_HEADS + hd] = q_ref[g * tq:(g + 1) * tq, hd * HEAD_W:(hd + 1) * HEAD_W]
    lane = lax.broadcasted_iota(jnp.int32, (tq, LANES), 1)
    pairs = []
    for u0 in range(0, units, hg):
        us = slice(u0, u0 + hg)
        s = lax.dot_general(q_scr[us], k_scr[us], (((2,), (2,)), ((0,), (0,))), preferred_element_type=f32)
        mx = jnp.max(s, axis=-1, keepdims=True)
        p = jnp.exp2(s - mx).astype(bf16)
        r = lax.dot_general(p, v_scr[us], (((2,), (1,)), ((0,), (0,))), preferred_element_type=f32)
        for j in range(hg // 2):
            re, ro = r[2 * j], r[2 * j + 1]
            oe = re * (1.0 / re[:, V_DIM:V_DIM + 1])
            oo = ro * (1.0 / ro[:, 0:1])
            pairs.append(jnp.where(lane < V_DIM, oe, oo))
    per_seq = N_HEADS // 2
    attn = jnp.concatenate(
        [jnp.concatenate(pairs[g * per_seq:(g + 1) * per_seq], axis=-1) for g in range(sb)], axis=0).astype(bf16)
    merged = (cm_ref[...].astype(f32) + g_ref[...].astype(f32) * _dot(attn, wo_ref[...])).astype(bf16)

    g1 = m_ref[2:3, :]
    sh2 = m_ref[3:4, :]
    sc2 = m_ref[4:5, :]
    x1 = x_ref[...] + g1 * _dot(merged, wout_ref[...])
    x1_ref[...] = x1
    h2 = _rms(x1, n2_ref[...]) * (1.0 + sc2) + sh2
    hi = h2.astype(bf16)
    h2_ref[...] = hi
    lo = (h2 - hi.astype(f32)).astype(bf16)
    both = _dot(jnp.concatenate([hi, lo], axis=0), wr_ref[...])
    rows = hi.shape[0]
    s = both[0:rows] + both[rows:2 * rows]
    lg_ref[...] = s + pltpu.roll(s, LANES - N_EXPERTS, axis=1)


def _attn(q2d, ckv3, kpe3, cm2d, g2d, x2d, m3, wk, wv, wo, wout, norm2, wr, *, n, tq):
    nseq, s_len, _ = ckv3.shape
    sb = max(1, ATTN_TOKENS // n) if (tq == n and m3.shape[0] == 1) else 1
    units = sb * N_HEADS
    hg = max(2, min(units, SCORE_BYTES // (tq * s_len * 4)))
    assert units % hg == 0 and hg % 2 == 0 and nseq % sb == 0
    qb = n // tq
    tokens = nseq * n
    tile = lambda w: pl.BlockSpec((sb * tq, w), lambda s, i: (s * qb + i, 0))
    single = lambda shape: pl.BlockSpec(shape, lambda s, i: (0,) * len(shape), pipeline_mode=pl.Buffered(1))
    return pl.pallas_call(
        functools.partial(_attn_kernel, sb=sb, hg=hg),
        grid=(nseq // sb, qb),
        in_specs=[
            tile(N_HEADS * HEAD_W),
            pl.BlockSpec((sb, s_len, KV_LORA), lambda s, i: (s, 0, 0)),
            pl.BlockSpec((sb, s_len, LANES), lambda s, i: (s, 0, 0)),
            tile(D_MODEL),
            tile(D_MODEL),
            tile(D_MODEL),
            _mod_spec(m3, lambda s, i: s),
            single(wk.shape),
            single(wv.shape),
            single(wo.shape),
            single(wout.shape),
            _const_spec((1, D_MODEL)),
            single(wr.shape),
        ],
        out_specs=(tile(D_MODEL), tile(D_MODEL), tile(LANES)),
        out_shape=(
            jax.ShapeDtypeStruct((tokens, D_MODEL), f32),
            jax.ShapeDtypeStruct((tokens, D_MODEL), bf16),
            jax.ShapeDtypeStruct((tokens, LANES), f32),
        ),
        scratch_shapes=[
            pltpu.VMEM((units, s_len, HEAD_W), bf16),
            pltpu.VMEM((units, s_len, LANES), bf16),
            pltpu.VMEM((units, tq, HEAD_W), bf16),
        ],
        compiler_params=_params(VMEM_LIMIT if sb > 1 else VMEM_LARGE, "parallel", "arbitrary"),
        name="attn_out",
    )(q2d, ckv3, kpe3, cm2d, g2d, x2d, m3, wk, wv, wo, wout, norm2, wr)


def _route_kernel(lg_ref, pos_ref, aff_ref, meta_ref, *, nseq, n, cap):
    for s in range(nseq):
        lt = lg_ref[s].T[0:N_EXPERTS, :]
        e = jnp.exp(lt - jnp.max(lt, axis=0, keepdims=True))
        aff_ref[s * N_EXPERTS:(s + 1) * N_EXPERTS, :] = e / jnp.sum(e, axis=0, keepdims=True)
    rows = nseq * N_EXPERTS
    capf = float(cap)

    def bit_step(i, t):
        cand = t | (jnp.int32(1) << (30 - i))
        thr = lax.bitcast_convert_type(cand, f32)
        cnt = jnp.sum(jnp.where(aff_ref[...] >= thr, 1.0, 0.0), axis=1, keepdims=True)
        return jnp.where(cnt >= capf, cand, t)

    t = lax.fori_loop(0, 31, bit_step, jnp.zeros((rows, 1), jnp.int32))
    thr = lax.bitcast_convert_type(t, f32)
    need = capf - jnp.sum(jnp.where(aff_ref[...] > thr, 1.0, 0.0), axis=1, keepdims=True)

    blk = MOE_BLOCK
    tri = jnp.where(
        lax.broadcasted_iota(jnp.int32, (blk, blk), 0) < lax.broadcasted_iota(jnp.int32, (blk, blk), 1),
        1.0, 0.0).astype(bf16)
    carry_gt = jnp.zeros((rows, 1), f32)
    carry_eq = jnp.zeros((rows, 1), f32)
    lane = lax.broadcasted_iota(jnp.int32, (rows, LANES), 1)
    meta = jnp.zeros((rows, LANES), f32)
    cmax = jnp.zeros((rows, 1), f32)
    for b in range(n // blk):
        sl = slice(b * blk, (b + 1) * blk)
        ab = aff_ref[:, sl]
        gt = ab > thr
        eq = ab == thr
        gtb = jnp.where(gt, 1.0, 0.0)
        eqb = jnp.where(eq, 1.0, 0.0)
        pre_gt = _dot(gtb.astype(bf16), tri) + carry_gt
        pre_eq = _dot(eqb.astype(bf16), tri) + carry_eq
        meta = jnp.where(lane == b, carry_gt + jnp.minimum(carry_eq, need), meta)
        carry_gt = carry_gt + jnp.sum(gtb, axis=1, keepdims=True)
        carry_eq = carry_eq + jnp.sum(eqb, axis=1, keepdims=True)
        sel = gt | (eq & (pre_eq < need))
        slot = pre_gt + jnp.minimum(pre_eq, need)
        pos_ref[:, sl] = jnp.where(sel, slot, -1.0).astype(jnp.int32)
        cmax = jnp.maximum(cmax, jnp.sum(jnp.where(sel, 1.0, 0.0), axis=1, keepdims=True))
    meta_ref[...] = jnp.where(lane == LANES - 1, cmax, meta).astype(jnp.int32)


def _route(lg3, *, cap):
    nseq, n, _ = lg3.shape
    rows = nseq * N_EXPERTS
    assert n // MOE_BLOCK < LANES
    return pl.pallas_call(
        functools.partial(_route_kernel, nseq=nseq, n=n, cap=cap),
        grid=(1,),
        in_specs=[_const_spec(lg3.shape)],
        out_specs=(_const_spec((rows, n)), _const_spec((rows, n)), _const_spec((rows, LANES))),
        out_shape=(jax.ShapeDtypeStruct((rows, n), jnp.int32), jax.ShapeDtypeStruct((rows, n), f32),
                   jax.ShapeDtypeStruct((rows, LANES), jnp.int32)),
        compiler_params=_params(VMEM_SMALL, "arbitrary"),
        name="route",
    )(lg3)


def _slot_hits(pos_ref, e, cap):
    width = pos_ref.shape[1]
    return lax.broadcasted_iota(jnp.int32, (cap, width), 0) == pos_ref[e:e + 1, :]


def _one_hot(hits):
    return jnp.concatenate([jnp.where(h, 1.0, 0.0).astype(bf16) for h in hits], axis=0)


def _gather_dense(pos_ref, aff_ref, h2_ref, xg_ref, vals_ref, rows, cap, ne):
    for e0 in range(0, N_EXPERTS, ne):
        hits = [_slot_hits(pos_ref, e0 + e, cap) for e in range(ne)]
        xg = _dot(_one_hot(hits), h2_ref[...]).astype(xg_ref.dtype)
        for e in range(ne):
            xg_ref[e0 + e, rows, :] = xg[e * cap:(e + 1) * cap, :]
            vals = jnp.sum(jnp.where(hits[e], aff_ref[e0 + e:e0 + e + 1, :], 0.0), axis=1, keepdims=True)
            vals_ref[e0 + e, rows, :] = jnp.broadcast_to(vals, (cap, LANES))


def _gather_kernel(pos_ref, aff_ref, h2_ref, xg_ref, vals_ref, *, cap, ne, sb):
    for g in range(sb):
        _gather_dense(pos_ref.at[g], aff_ref.at[g], h2_ref.at[g], xg_ref, vals_ref,
                      slice(g * cap, (g + 1) * cap), cap, ne)


def _gather(pos3, aff3, h23, *, cap, ne, sb):
    nseq, _, n = pos3.shape
    return pl.pallas_call(
        functools.partial(_gather_kernel, cap=cap, ne=ne, sb=sb),
        grid=(nseq // sb,),
        in_specs=[
            pl.BlockSpec((sb, N_EXPERTS, n), lambda s: (s, 0, 0)),
            pl.BlockSpec((sb, N_EXPERTS, n), lambda s: (s, 0, 0)),
            pl.BlockSpec((sb, n, D_MODEL), lambda s: (s, 0, 0)),
        ],
        out_specs=(
            pl.BlockSpec((N_EXPERTS, sb * cap, D_MODEL), lambda s: (0, s, 0)),
            pl.BlockSpec((N_EXPERTS, sb * cap, LANES), lambda s: (0, s, 0)),
        ),
        out_shape=(
            jax.ShapeDtypeStruct((N_EXPERTS, nseq * cap, D_MODEL), bf16),
            jax.ShapeDtypeStruct((N_EXPERTS, nseq * cap, LANES), f32),
        ),
        compiler_params=_params(VMEM_MID, "parallel"),
        name="gather",
    )(pos3, aff3, h23)


def _windows_fit(meta_ref):
    cmax = meta_ref[0, LANES - 1]
    for e in range(1, N_EXPERTS):
        cmax = jnp.maximum(cmax, meta_ref[e, LANES - 1])
    return cmax <= MOE_WIN - WIN_ALIGN


def _win_base(meta_ref, e, b, cap):
    start = meta_ref[e, b]
    return pl.multiple_of(jnp.minimum(start - start % WIN_ALIGN, cap - MOE_WIN), WIN_ALIGN)


def _win_hits(pos_row, base):
    slot = lax.broadcasted_iota(jnp.int32, (MOE_WIN, pos_row.shape[1]), 0) + base
    return slot == pos_row


def _gather_win_kernel(meta_ref, pos_ref, aff_ref, h2_ref, xg_ref, vals_ref, *, n, cap):
    fits = _windows_fit(meta_ref)

    @pl.when(fits)
    def _():
        xg_ref[...] = jnp.zeros(xg_ref.shape, xg_ref.dtype)
        vals_ref[...] = jnp.zeros(vals_ref.shape, vals_ref.dtype)

        def block(b, carry):
            cols = pl.ds(pl.multiple_of(b * MOE_BLOCK, MOE_BLOCK), MOE_BLOCK)
            bases = [_win_base(meta_ref, e, b, cap) for e in range(N_EXPERTS)]
            hits = [_win_hits(pos_ref[e:e + 1, cols], bases[e]) for e in range(N_EXPERTS)]
            part = _dot(_one_hot(hits), h2_ref[cols, :])
            for e in range(N_EXPERTS):
                rows = pl.ds(bases[e], MOE_WIN)
                xg_ref[e, rows, :] += part[e * MOE_WIN:(e + 1) * MOE_WIN, :].astype(xg_ref.dtype)
                vals = jnp.sum(jnp.where(hits[e], aff_ref[e:e + 1, cols], 0.0), axis=1, keepdims=True)
                vals_ref[e, rows, :] += jnp.broadcast_to(vals, (MOE_WIN, LANES))
            return carry

        lax.fori_loop(0, n // MOE_BLOCK, block, 0)

    @pl.when(jnp.logical_not(fits))
    def _():
        _gather_dense(pos_ref, aff_ref, h2_ref, xg_ref, vals_ref, slice(0, cap), cap, 1)


def _gather_win(meta, pos3, aff3, h23, *, cap):
    nseq, _, n = pos3.shape
    return pl.pallas_call(
        functools.partial(_gather_win_kernel, n=n, cap=cap),
        grid=(nseq,),
        in_specs=[
            pl.BlockSpec((N_EXPERTS, LANES), lambda s: (s, 0), memory_space=pltpu.SMEM),
            pl.BlockSpec((None, N_EXPERTS, n), lambda s: (s, 0, 0)),
            pl.BlockSpec((None, N_EXPERTS, n), lambda s: (s, 0, 0)),
            pl.BlockSpec((None, n, D_MODEL), lambda s: (s, 0, 0)),
        ],
        out_specs=(
            pl.BlockSpec((N_EXPERTS, cap, D_MODEL), lambda s: (0, s, 0)),
            pl.BlockSpec((N_EXPERTS, cap, LANES), lambda s: (0, s, 0)),
        ),
        out_shape=(
            jax.ShapeDtypeStruct((N_EXPERTS, nseq * cap, D_MODEL), bf16),
            jax.ShapeDtypeStruct((N_EXPERTS, nseq * cap, LANES), f32),
        ),
        compiler_params=_params(VMEM_LARGE, "parallel"),
        name="gather_win",
    )(meta, pos3, aff3, h23)


def _experts_kernel(xp_ref, vp_ref, xs_ref, vs_ref, wg_ref, wu_ref, wd_ref, yp_ref, ys_ref, *, rc):
    wg = wg_ref[...].astype(bf16)
    wu = wu_ref[...].astype(bf16)
    wd = wd_ref[...].astype(bf16)
    for x_ref, v_ref, y_ref in ((xp_ref, vp_ref, yp_ref), (xs_ref, vs_ref, ys_ref)):
        for r0 in range(0, x_ref.shape[0], rc):
            x = x_ref[r0:r0 + rc, :]
            a = _dot(x, wg)
            u = _dot(x, wu)
            hm = (a * jax.nn.sigmoid(a) * u).astype(bf16)
            y = _dot(hm, wd) * v_ref[r0:r0 + rc, 0:1]
            y_ref[r0:r0 + rc, :] = y.astype(y_ref.dtype)


def _experts(xg_p, vals_p, xg_s, vals_s, wg, wu, wd, *, rc=EXPERT_ROWS):
    rp = xg_p.shape[1]
    rs = xg_s.shape[1]
    per_e = lambda r, w: pl.BlockSpec((None, r, w), lambda e: (e, 0, 0))
    return pl.pallas_call(
        functools.partial(_experts_kernel, rc=rc),
        grid=(N_EXPERTS,),
        in_specs=[
            per_e(rp, D_MODEL), per_e(rp, LANES), per_e(rs, D_MODEL), per_e(rs, LANES),
            per_e(D_MODEL, EXPERT_FF), per_e(D_MODEL, EXPERT_FF), per_e(EXPERT_FF, D_MODEL),
        ],
        out_specs=(per_e(rp, D_MODEL), per_e(rs, D_MODEL)),
        out_shape=(
            jax.ShapeDtypeStruct((N_EXPERTS, rp, D_MODEL), bf16),
            jax.ShapeDtypeStruct((N_EXPERTS, rs, D_MODEL), bf16),
        ),
        compiler_params=_params(VMEM_LARGE, "parallel"),
        name="experts",
    )(xg_p, vals_p, xg_s, vals_s, wg, wu, wd)


def _scatter_dense(pos_ref, y_ref, rows, tn, cap, ne):
    moe = jnp.zeros((tn, D_MODEL), f32)
    for e0 in range(0, N_EXPERTS, ne):
        onehot = _one_hot([_slot_hits(pos_ref, e0 + e, cap) for e in range(ne)])
        y = jnp.concatenate([y_ref[e0 + e, rows, :] for e in range(ne)], axis=0)
        moe = moe + lax.dot_general(onehot, y, (((0,), (0,)), ((), ())), preferred_element_type=f32)
    return moe


def _scatter_kernel(pos_ref, y_ref, x1_ref, m_ref, fn_ref, o_ref, *, tn, cap, ne, sb):
    g2 = m_ref[5:6, :]
    for g in range(sb):
        moe = _scatter_dense(pos_ref.at[g], y_ref, slice(g * cap, (g + 1) * cap), tn, cap, ne)
        o_ref[g] = _rms(x1_ref[g] + g2 * moe, fn_ref[...])


def _scatter(pos3, y, x13, m3, fn, *, cap, ne, tn, sb):
    nseq, _, n = pos3.shape
    assert sb == 1 or m3.shape[0] == 1, "sequences sharing a grid step must share their modulation rows"
    return pl.pallas_call(
        functools.partial(_scatter_kernel, tn=tn, cap=cap, ne=ne, sb=sb),
        grid=(nseq // sb, n // tn),
        in_specs=[
            pl.BlockSpec((sb, N_EXPERTS, tn), lambda s, i: (s, 0, i)),
            pl.BlockSpec((N_EXPERTS, sb * cap, D_MODEL), lambda s, i: (0, s, 0)),
            pl.BlockSpec((sb, tn, D_MODEL), lambda s, i: (s, i, 0)),
            _mod_spec(m3, lambda s, i: s),
            _const_spec((1, D_MODEL)),
        ],
        out_specs=pl.BlockSpec((sb, tn, D_MODEL), lambda s, i: (s, i, 0)),
        out_shape=jax.ShapeDtypeStruct((nseq, n, D_MODEL), f32),
        compiler_params=_params(VMEM_MID, "parallel", "arbitrary"),
        name="scatter",
    )(pos3, y, x13, m3, fn)


def _scatter_win_kernel(meta_ref, pos_ref, y_ref, x1_ref, m_ref, fn_ref, o_ref, moe_scr, *, cap):
    b = pl.program_id(1)
    fits = _windows_fit(meta_ref)

    @pl.when(fits)
    def _():
        bases = [_win_base(meta_ref, e, b, cap) for e in range(N_EXPERTS)]
        onehot = _one_hot([_win_hits(pos_ref[e:e + 1, :], bases[e]) for e in range(N_EXPERTS)])
        y = jnp.concatenate([y_ref[e, pl.ds(bases[e], MOE_WIN), :] for e in range(N_EXPERTS)], axis=0)
        moe_scr[...] = lax.dot_general(onehot, y, (((0,), (0,)), ((), ())), preferred_element_type=f32)

    @pl.when(jnp.logical_not(fits))
    def _():
        moe_scr[...] = _scatter_dense(pos_ref, y_ref, slice(0, cap), MOE_BLOCK, cap, 1)

    o_ref[...] = _rms(x1_ref[...] + m_ref[5:6, :] * moe_scr[...], fn_ref[...])


def _scatter_win(meta, pos3, y, x13, m3, fn, *, cap):
    nseq, _, n = pos3.shape
    tn = MOE_BLOCK
    return pl.pallas_call(
        functools.partial(_scatter_win_kernel, cap=cap),
        grid=(nseq, n // tn),
        in_specs=[
            pl.BlockSpec((N_EXPERTS, LANES), lambda s, i: (s, 0), memory_space=pltpu.SMEM),
            pl.BlockSpec((None, N_EXPERTS, tn), lambda s, i: (s, 0, i)),
            pl.BlockSpec((N_EXPERTS, cap, D_MODEL), lambda s, i: (0, s, 0)),
            pl.BlockSpec((None, tn, D_MODEL), lambda s, i: (s, i, 0)),
            _mod_spec(m3, lambda s, i: s),
            _const_spec((1, D_MODEL)),
        ],
        out_specs=pl.BlockSpec((None, tn, D_MODEL), lambda s, i: (s, i, 0)),
        out_shape=jax.ShapeDtypeStruct((nseq, n, D_MODEL), f32),
        scratch_shapes=[pltpu.VMEM((tn, D_MODEL), f32)],
        compiler_params=_params(VMEM_MID, "parallel", "arbitrary"),
        name="scatter_win",
    )(meta, pos3, y, x13, m3, fn)


def _rope_tables(n):
    t = np.arange(n)
    half = QK_ROPE // 2
    freqs = ROPE_BASE ** (-np.arange(0, half, 2, dtype=np.float64) / half)
    ang_r = (t // GRID_W)[:, None] * freqs
    ang_c = (t % GRID_W)[:, None] * freqs
    cr, sr, cc, sc = np.cos(ang_r), np.sin(ang_r), np.cos(ang_c), np.sin(ang_c)
    cos = np.ones((n, HEAD_W))
    sin = np.zeros((n, HEAD_W))
    cos[:, ROPE_OFF:ROPE_OFF + QK_ROPE] = np.concatenate([cr, cr, cc, cc], axis=-1)
    sin[:, ROPE_OFF:ROPE_OFF + QK_ROPE] = np.concatenate([-sr, sr, -sc, sc], axis=-1)
    return jnp.asarray(cos, f32), jnp.asarray(sin, f32)


_PARTNER = np.concatenate([np.arange(8, 16), np.arange(0, 8), np.arange(24, 32), np.arange(16, 24)])


def _rope_partner(w):
    q = QK_ROPE // 4
    return jnp.concatenate([w[..., q:2 * q], w[..., 0:q], w[..., 3 * q:4 * q], w[..., 2 * q:3 * q]], axis=-1)


def _rope_placement():
    place = np.zeros((LANES, 2 * LANES), np.float32)
    d = np.arange(QK_ROPE)
    place[d, ROPE_OFF + d] = 1.0
    place[_PARTNER, LANES + ROPE_OFF + d] = 1.0
    return jnp.asarray(place, bf16)


def _head_blocks(w_nope, w_rope):
    rows = w_nope.shape[0]
    if w_rope is None:
        w_rope = jnp.zeros((rows, N_HEADS, QK_ROPE), w_nope.dtype)
    z = jnp.zeros((rows, N_HEADS, HEAD_W - QK_NOPE - QK_ROPE), w_nope.dtype)
    return jnp.concatenate([w_nope, w_rope, z], axis=-1).reshape(rows, N_HEADS * HEAD_W)


def kernel(x_prompt, x_sample, cache_ckv, cache_kpe, c, c_ctx, w_ada, b_ada, norm1, w_in, conv_dw, conv_dw_b,
           conv_ln_g, conv_ln_b, w_conv_out, q_norm, w_qb, kv_norm, w_kvb, w_o_mla, w_out, norm2, w_router,
           w_e_gate, w_e_up, w_e_down, final_norm):
    assert w_ada.shape[0] == 1, "single trunk layer"
    nb_p, n_p, _ = x_prompt.shape
    nb_s, n_s, _ = x_sample.shape

    win = _wprep(w_in[0].T, _rope_placement())
    wq = w_qb[0].reshape(Q_LORA, N_HEADS, QK_NOPE + QK_ROPE)
    wqb = _head_blocks(wq[..., :QK_NOPE], wq[..., QK_NOPE:]).astype(bf16)
    wqbp = _head_blocks(jnp.zeros_like(wq[..., :QK_NOPE]), _rope_partner(wq[..., QK_NOPE:])).astype(bf16)
    wkv = w_kvb[0].reshape(KV_LORA, N_HEADS, QK_NOPE + V_DIM)
    wk = _head_blocks(wkv[..., :QK_NOPE], None).astype(bf16)
    wv = wkv[..., QK_NOPE:].reshape(KV_LORA, N_HEADS * V_DIM).astype(bf16)
    wco = w_conv_out[0].astype(bf16)
    wo = w_o_mla[0].astype(bf16)
    wout = w_out[0].astype(bf16)
    wr_hi = w_router[0].astype(bf16)
    wr_lo = (w_router[0] - wr_hi.astype(f32)).astype(bf16)
    wr = jnp.concatenate([wr_hi, wr_lo, jnp.zeros((D_MODEL, LANES - 2 * N_EXPERTS), bf16)], axis=-1)
    row = lambda a: a.reshape(1, -1)

    mod = jnp.concatenate([c_ctx[None, :], c, jnp.zeros((8 - 1 - nb_s, D_MODEL), f32)], axis=0)
    m = _ada(mod, w_ada[0], b_ada[0]).reshape(8, 6, D_MODEL)
    m_p, m_s = m[0:1], m[1:1 + nb_s]
    cos, sin = _rope_tables(n_s)

    conv_w = _conv_weights(conv_dw[0], row(conv_dw_b[0]), row(conv_ln_g[0]), row(conv_ln_b[0]), wco, CONV_CHUNK)

    def mixers(x, m3, rope, ctx_ckv, ctx_kpe):
        nseq, n, _ = x.shape
        x2d = x.reshape(nseq * n, D_MODEL)
        cm, q, ckv, kr, g, *kpe = _inproj(x2d, m3, row(norm1[0]), win, row(q_norm[0]), wqb, wqbp, row(kv_norm[0]),
                                          cos, sin, conv_w, n=n, rope=rope, tm=IN_TILE)
        keys_ckv = ckv.reshape(nseq, n, KV_LORA)
        keys_kpe = kr.reshape(nseq, n, LANES)
        if ctx_ckv is not None:
            keys_ckv = jnp.concatenate([ctx_ckv.astype(keys_ckv.dtype), keys_ckv], axis=1)
            keys_kpe = jnp.concatenate([ctx_kpe.astype(keys_kpe.dtype), keys_kpe], axis=1)
        x1, h2, lg = _attn(q, keys_ckv, keys_kpe, cm.reshape(nseq * n, D_MODEL), g, x2d, m3, wk, wv, wo, wout,
                           row(norm2[0]), wr, n=n, tq=min(n, Q_TILE))
        return x1, h2, lg, ckv, kpe

    ctx_kpe = jnp.pad(cache_kpe[:, 0], ((0, 0), (0, 0), (ROPE_OFF, LANES - ROPE_OFF - QK_ROPE)))
    x1_p, h2_p, lg_p, ckv_p, (kpe_p,) = mixers(x_prompt, m_p, False, None, None)
    x1_s, h2_s, lg_s, _, _ = mixers(x_sample, m_s, True, cache_ckv[:, 0], ctx_kpe)

    def moe_tiles(n):
        cap = EC_FACTOR * n // N_EXPERTS
        ne = N_EXPERTS if N_EXPERTS * cap <= MOE_ROWS else 1
        sb = max(1, MOE_TOKENS // n)
        return cap, ne, sb

    def windowed(n):
        return n >= 4 * MOE_BLOCK and EC_FACTOR * n // N_EXPERTS >= 2 * MOE_WIN

    def route_gather(h2, lg, nseq, n):
        cap, ne, sb = moe_tiles(n)
        pos, aff, meta = _route(lg.reshape(nseq, n, LANES), cap=cap)
        pos3 = pos.reshape(nseq, N_EXPERTS, n)
        aff3 = aff.reshape(nseq, N_EXPERTS, n)
        h23 = h2.reshape(nseq, n, D_MODEL)
        if windowed(n):
            xg, vals = _gather_win(meta, pos3, aff3, h23, cap=cap)
        else:
            xg, vals = _gather(pos3, aff3, h23, cap=cap, ne=ne, sb=sb)
        return pos3, meta, xg, vals

    pos_p, meta_p, xg_p, vals_p = route_gather(h2_p, lg_p, nb_p, n_p)
    pos_s, meta_s, xg_s, vals_s = route_gather(h2_s, lg_s, nb_s, n_s)
    y_p, y_s = _experts(xg_p, vals_p, xg_s, vals_s, w_e_gate[0], w_e_up[0], w_e_down[0])
    fn = row(final_norm)

    def scatter(pos, meta, y, x1, m3, nseq, n):
        cap, ne, sb = moe_tiles(n)
        x13 = x1.reshape(nseq, n, D_MODEL)
        if windowed(n):
            return _scatter_win(meta, pos, y, x13, m3, fn, cap=cap)
        return _scatter(pos, y, x13, m3, fn, cap=cap, ne=ne, tn=min(n, SCATTER_TILE), sb=sb)

    y_prompt = scatter(pos_p, meta_p, y_p, x1_p, m_p, nb_p, n_p)
    y_sample = scatter(pos_s, meta_s, y_s, x1_s, m_s, nb_s, n_s)

    new_ckv = ckv_p.reshape(nb_p, 1, n_p, KV_LORA)
    new_kpe = kpe_p.reshape(nb_p, 1, n_p, QK_ROPE)
    return (y_prompt, y_sample, new_ckv, new_kpe)
```

```python
import functools

import jax
import jax.numpy as jnp
import numpy as np
from jax import lax
from jax.experimental import pallas as pl
from jax.experimental.pallas import tpu as pltpu

D_MODEL = 1024
GRID_W = 64
CONV_DIM = 512
CONV_WIDTH = 31
N_HEADS = 8
QK_NOPE = 64
QK_ROPE = 32
V_DIM = 64
Q_LORA = 256
KV_LORA = 128
N_EXPERTS = 16
EXPERT_FF = 512
EC_FACTOR = 2
ROPE_BASE = 10000.0
EPS = 1e-6

LANES = 128
HEAD_W = LANES
ROPE_OFF = QK_NOPE
CONV_HALO = 16
LOG2E = 1.4426950408889634
VMEM_LIMIT = 48 * 1024 * 1024
IN_TILE = 512
CONV_CHUNK = 64
CONV_OUT_ROWS = 256
GATE_CHUNK = 512
EXPERT_ROWS = 768
Q_TILE = 512
ATTN_TOKENS = 1024
SCORE_BYTES = 12 * 1024 * 1024
MOE_ROWS = 512
MOE_TOKENS = 1024
SCATTER_TILE = 512
MOE_BLOCK = 2 * LANES
MOE_WIN = 80
WIN_ALIGN = 16

C_CONV = 0
C_QA = 2 * CONV_DIM
C_KVA = C_QA + Q_LORA
C_KR = C_KVA + KV_LORA
C_GATE = C_KR + LANES
C_END = C_GATE + 2 * D_MODEL
C_KRP = C_END

f32 = jnp.float32
bf16 = jnp.bfloat16


MIB = 1024 * 1024
VMEM_SMALL, VMEM_MID, VMEM_LARGE = 16 * MIB, 32 * MIB, 40 * MIB


def _params(vmem_bytes, *sem):
    assert vmem_bytes <= VMEM_LIMIT
    return pltpu.CompilerParams(dimension_semantics=sem, vmem_limit_bytes=vmem_bytes)


def _dot(a, b):
    return jnp.dot(a, b, preferred_element_type=f32)


def _rms(x, g):
    return x * lax.rsqrt(jnp.mean(x * x, axis=-1, keepdims=True) + EPS) * g


def _const_spec(shape):
    nd = len(shape)
    return pl.BlockSpec(shape, lambda *_: (0,) * nd)


def _mod_spec(m3, seq_of):
    if m3.shape[0] == 1:
        return _const_spec((None, 6, D_MODEL))
    return pl.BlockSpec((None, 6, D_MODEL), lambda *idx: (seq_of(*idx), 0, 0))


def _ada_kernel(s_ref, w_ref, b_ref, o_ref):
    s = s_ref[...]
    s = s * jax.nn.sigmoid(s)
    o_ref[...] = _dot(s.astype(bf16), w_ref[...].astype(bf16)) + b_ref[...]


def _ada(mod, w_ada, b_ada):
    rows = mod.shape[0]
    n_out = w_ada.shape[1]
    tn = D_MODEL
    return pl.pallas_call(
        _ada_kernel,
        grid=(n_out // tn,),
        in_specs=[
            _const_spec((rows, D_MODEL)),
            pl.BlockSpec((D_MODEL, tn), lambda j: (0, j)),
            pl.BlockSpec((1, tn), lambda j: (0, j)),
        ],
        out_specs=pl.BlockSpec((rows, tn), lambda j: (0, j)),
        out_shape=jax.ShapeDtypeStruct((rows, n_out), f32),
        compiler_params=_params(VMEM_LIMIT, "arbitrary"),
        name="ada",
    )(mod, w_ada, b_ada.reshape(1, n_out))


def _wprep_kernel(wt_ref, place_ref, o_ref):
    def block(r0):
        return wt_ref[r0:r0 + LANES, :].T.astype(bf16)

    for j in range(C_KR // LANES):
        o_ref[:, j * LANES:(j + 1) * LANES] = block(j * LANES)
    placed = _dot(block(C_KR), place_ref[...])
    o_ref[:, C_KR:C_GATE] = placed[:, 0:LANES].astype(bf16)
    o_ref[:, C_KRP:C_KRP + LANES] = placed[:, LANES:2 * LANES].astype(bf16)
    for j in range(2 * D_MODEL // LANES):
        o_ref[:, C_GATE + j * LANES:C_GATE + (j + 1) * LANES] = block(C_KR + QK_ROPE + j * LANES)


def _wprep(w_in_t, place):
    cols, rows = w_in_t.shape
    return pl.pallas_call(
        _wprep_kernel,
        grid=(rows // LANES,),
        in_specs=[pl.BlockSpec((cols, LANES), lambda i: (0, i)), _const_spec(place.shape)],
        out_specs=pl.BlockSpec((LANES, C_KRP + LANES), lambda i: (i, 0)),
        out_shape=jax.ShapeDtypeStruct((rows, C_KRP + LANES), bf16),
        compiler_params=_params(VMEM_SMALL, "parallel"),
        name="wprep",
    )(w_in_t, place)


def _conv_taps(vpad, r0, shift_ref, dw_ref, dwb_ref, ybuf, y0, ct):
    pad = CONV_WIDTH // 2
    sub = 8
    span = ((CONV_HALO - pad + CONV_WIDTH - 1) // sub) * sub
    cw = 2 * LANES
    for cb in range(CONV_DIM // cw):
        sl = slice(cb * cw, (cb + 1) * cw)
        win = vpad[pl.ds(r0, ct + 2 * CONV_HALO), sl]
        acc = jnp.zeros((ct // sub, sub, cw), f32)
        for ph in range(sub):
            wph = win.astype(f32) if ph == 0 else _dot(shift_ref[ph - 1], win)
            for a in range(span // sub + 1):
                k = a * sub + ph - (CONV_HALO - pad)
                if 0 <= k < CONV_WIDTH:
                    acc = acc + wph[a * sub:a * sub + ct, :].reshape(ct // sub, sub, cw) * dw_ref[k, :, sl][None]
        ybuf[pl.ds(y0, ct), sl] = acc.reshape(ct, cw) + dwb_ref[:, sl]


def _conv_out(y, lng_ref, lnb_ref, wco_ref, gate):
    mu = jnp.mean(y, axis=-1, keepdims=True)
    yc = y - mu
    var = jnp.mean(yc * yc, axis=-1, keepdims=True)
    z = yc * lax.rsqrt(var + EPS) * lng_ref[...] + lnb_ref[...]
    z = z * jax.nn.sigmoid(z)
    return (gate.astype(f32) * _dot(z.astype(bf16), wco_ref[...])).astype(bf16)


def _fill_padded(vpad, v, n):
    zeros = jnp.zeros((CONV_HALO, CONV_DIM), vpad.dtype)
    vpad[0:CONV_HALO, :] = zeros
    vpad[CONV_HALO + n:2 * CONV_HALO + n, :] = zeros
    vpad[CONV_HALO:CONV_HALO + n, :] = v


def _inproj_kernel(*refs, rope, halo, tiles_per_seq, q_scale, n, rt, ct):
    refs = list(refs)
    x_ref = refs.pop(0)
    if halo:
        xp_ref, xn_ref = refs[:2]
        del refs[:2]
    m_ref, n1_ref, win_ref, qn_ref, wqb_ref = refs[:5]
    del refs[:5]
    if rope:
        wqbp_ref = refs.pop(0)
    kvn_ref = refs.pop(0)
    if rope:
        cos_ref, sin_ref = refs[:2]
        del refs[:2]
    shift_ref, dw_ref, dwb_ref, lng_ref, lnb_ref, wco_ref = refs[:6]
    del refs[:6]
    cm_ref, q_ref, ckv_ref, kr_ref, g_ref = refs[:5]
    del refs[:5]
    if not rope:
        kpe_ref = refs.pop(0)
    vpad, ybuf, gc = refs
    tm = x_ref.shape[0]
    lo = CONV_HALO if halo else 0

    sh1 = m_ref[0:1, :]
    sc1 = m_ref[1:2, :]
    x = jnp.concatenate([xp_ref[...], x_ref[...], xn_ref[...]], axis=0) if halo else x_ref[...]
    hb_all = (_rms(x, n1_ref[...]) * (1.0 + sc1) + sh1).astype(bf16)

    left_all = _dot(hb_all, win_ref[:, 0:C_GATE])
    v = (left_all[:, C_CONV:C_CONV + CONV_DIM] * jax.nn.sigmoid(left_all[:, C_CONV + CONV_DIM:C_QA])).astype(bf16)
    hb = hb_all[lo:lo + tm, :]
    left = left_all[lo:lo + tm, :]

    if halo:
        j = pl.program_id(0) % tiles_per_seq
        zeros = jnp.zeros((CONV_HALO, CONV_DIM), bf16)
        vpad[0, 0:lo, :] = jnp.where(j > 0, v[0:lo, :], zeros)
        vpad[0, lo:lo + tm, :] = v[lo:lo + tm, :]
        vpad[0, lo + tm:2 * lo + tm, :] = jnp.where(j < tiles_per_seq - 1, v[lo + tm:2 * lo + tm, :], zeros)
        chunks = [(0, c * ct, c * ct) for c in range(tm // ct)]
    else:
        for s in range(tm // n):
            _fill_padded(vpad.at[s], v[s * n:(s + 1) * n, :], n)
        chunks = [(s, c * ct, s * n + c * ct) for s in range(tm // n) for c in range(n // ct)]

    def conv_some(count):
        for _ in range(min(count, len(chunks))):
            s, r0, y0 = chunks.pop(0)
            _conv_taps(vpad.at[s], r0, shift_ref, dw_ref, dwb_ref, ybuf, y0, ct)

    steps = 2 + 2 * D_MODEL // GATE_CHUNK
    per_step = -(-len(chunks) // steps)

    qa = left[:, C_QA:C_KVA]
    qn = _rms(qa, qn_ref[...]).astype(bf16)
    q = _dot(qn, wqb_ref[...])
    if rope:
        qp = _dot(qn, wqbp_ref[...])
        cos = cos_ref[...]
        sin = sin_ref[...]
        for hd in range(N_HEADS):
            sl = slice(hd * HEAD_W, (hd + 1) * HEAD_W)
            q_ref[:, sl] = ((q[:, sl] * cos + qp[:, sl] * sin) * q_scale).astype(q_ref.dtype)
    else:
        q_ref[...] = (q * q_scale).astype(q_ref.dtype)
    conv_some(per_step)

    kva = left[:, C_KVA:C_KR]
    ckv_ref[...] = _rms(kva, kvn_ref[...]).astype(ckv_ref.dtype)

    kr = left[:, C_KR:C_GATE]
    if rope:
        krp = _dot(hb, win_ref[:, C_KRP:C_KRP + LANES])
        kr = kr * cos_ref[...] + krp * sin_ref[...]
    else:
        kpe_ref[...] = kr[:, ROPE_OFF:ROPE_OFF + QK_ROPE]
    kr_ref[...] = kr.astype(kr_ref.dtype)
    conv_some(per_step)

    gw = GATE_CHUNK
    for jg in range(2 * D_MODEL // gw):
        gate = jax.nn.sigmoid(_dot(hb, win_ref[:, C_GATE + jg * gw:C_GATE + (jg + 1) * gw])).astype(bf16)
        if jg * gw < D_MODEL:
            gc[:, jg * gw:(jg + 1) * gw] = gate
        else:
            g_ref[:, jg * gw - D_MODEL:(jg + 1) * gw - D_MODEL] = gate
        conv_some(per_step)

    conv_some(len(chunks))
    for r0 in range(0, tm, rt):
        cm_ref[r0:r0 + rt, :] = _conv_out(ybuf[r0:r0 + rt, :], lng_ref, lnb_ref, wco_ref, gc[r0:r0 + rt, :])


def _inproj(x2d, m3, norm1, win, q_norm, wqb, wqbp, kv_norm, cos, sin, conv_w, *, n, rope, tm,
            rt=CONV_OUT_ROWS, ct=CONV_CHUNK):
    tokens = x2d.shape[0]
    halo = n > tm
    assert n % tm == 0 if halo else (tm % n == 0 and m3.shape[0] == 1 and not rope)
    tiles_per_seq = max(1, n // tm)
    q_scale = float((QK_NOPE + QK_ROPE) ** -0.5 * LOG2E)
    tile = lambda w: pl.BlockSpec((tm, w), lambda i: (i, 0))
    in_specs = [tile(D_MODEL)]
    args = [x2d]
    if halo:
        hb_tile = tm // CONV_HALO
        hb_seq = n // CONV_HALO
        prev = lambda i: (jnp.maximum(i * hb_tile - 1, (i // tiles_per_seq) * hb_seq), 0)
        nxt = lambda i: (jnp.minimum((i + 1) * hb_tile, (i // tiles_per_seq + 1) * hb_seq - 1), 0)
        in_specs += [pl.BlockSpec((CONV_HALO, D_MODEL), prev), pl.BlockSpec((CONV_HALO, D_MODEL), nxt)]
        args += [x2d, x2d]
    in_specs += [
        _mod_spec(m3, lambda i: i // tiles_per_seq),
        _const_spec((1, D_MODEL)),
        _const_spec((D_MODEL, C_KRP + LANES if rope else C_END)),
        _const_spec((1, Q_LORA)),
        _const_spec(wqb.shape),
    ]
    args += [m3, norm1, win, q_norm, wqb]
    if rope:
        in_specs.append(_const_spec(wqbp.shape))
        args.append(wqbp)
    in_specs.append(_const_spec((1, KV_LORA)))
    args.append(kv_norm)
    if rope:
        tab = pl.BlockSpec((tm, LANES), lambda i: (i % tiles_per_seq, 0))
        in_specs += [tab, tab]
        args += [cos, sin]
    args += list(conv_w)
    in_specs += [_const_spec(w.shape) for w in conv_w]
    pieces = max(1, tm // n)
    scratch = [
        pltpu.VMEM((pieces, tm // pieces + 2 * CONV_HALO, CONV_DIM), bf16),
        pltpu.VMEM((tm, CONV_DIM), f32),
        pltpu.VMEM((tm, D_MODEL), bf16),
    ]
    out_shape = [
        jax.ShapeDtypeStruct((tokens, D_MODEL), bf16),
        jax.ShapeDtypeStruct((tokens, N_HEADS * HEAD_W), bf16),
        jax.ShapeDtypeStruct((tokens, KV_LORA), bf16 if rope else f32),
        jax.ShapeDtypeStruct((tokens, LANES), bf16),
        jax.ShapeDtypeStruct((tokens, D_MODEL), bf16),
    ]
    out_specs = [tile(D_MODEL), tile(N_HEADS * HEAD_W), tile(KV_LORA), tile(LANES), tile(D_MODEL)]
    if not rope:
        out_shape.append(jax.ShapeDtypeStruct((tokens, QK_ROPE), f32))
        out_specs.append(tile(QK_ROPE))
    return pl.pallas_call(
        functools.partial(_inproj_kernel, rope=rope, halo=halo, tiles_per_seq=tiles_per_seq, q_scale=q_scale,
                          n=n, rt=rt, ct=ct),
        grid=(tokens // tm,),
        in_specs=in_specs,
        out_specs=out_specs,
        out_shape=out_shape,
        scratch_shapes=scratch,
        compiler_params=_params(VMEM_MID, "parallel"),
        name="inproj_conv_rope" if rope else "inproj_conv",
    )(*args)


def _conv_weights(dw, dwb, lng, lnb, wco, ct):
    rows = ct + 2 * CONV_HALO
    i = np.arange(rows)
    shifts = jnp.asarray(np.stack([(i[None, :] == i[:, None] + ph) for ph in range(1, 8)]), bf16)
    dw_tiles = jnp.broadcast_to(dw[:, None, :], (CONV_WIDTH, 8, CONV_DIM))
    return shifts, dw_tiles, dwb, lng, lnb, wco


def _attn_kernel(q_ref, ckv_ref, kpe_ref, cm_ref, g_ref, x_ref, m_ref, wk_ref, wv_ref, wo_ref, wout_ref, n2_ref,
                 wr_ref, x1_ref, h2_ref, lg_ref, k_scr, v_scr, q_scr, *, sb, hg):
    s_len = ckv_ref.shape[1]

    @pl.when(pl.program_id(1) == 0)
    def _():
        ckv = ckv_ref[...].reshape(sb * s_len, KV_LORA).astype(bf16)
        kpe = kpe_ref[...].reshape(sb * s_len, LANES).astype(f32)
        k = _dot(ckv, wk_ref[...])
        v = _dot(ckv, wv_ref[...])
        ones = jnp.ones((s_len, V_DIM), bf16)
        for g in range(sb):
            rows = slice(g * s_len, (g + 1) * s_len)
            for hd in range(N_HEADS):
                u = g * N_HEADS + hd
                k_scr[u] = (k[rows, hd * HEAD_W:(hd + 1) * HEAD_W] + kpe[rows, :]).astype(bf16)
                vh = v[rows, hd * V_DIM:(hd + 1) * V_DIM].astype(bf16)
                if hd % 2 == 0:
                    v_scr[u, :, 0:V_DIM] = vh
                    v_scr[u, :, V_DIM:LANES] = ones
                else:
                    v_scr[u, :, 0:V_DIM] = ones
                    v_scr[u, :, V_DIM:LANES] = vh

    tq = q_ref.shape[0] // sb
    units = sb * N_HEADS
    for g in range(sb):
        for hd in range(N_HEADS):
            q_scr[g * N_HEADS + hd] = q_ref[g * tq:(g + 1) * tq, hd * HEAD_W:(hd + 1) * HEAD_W]
    lane = lax.broadcasted_iota(jnp.int32, (tq, LANES), 1)
    pairs = []
    for u0 in range(0, units, hg):
        us = slice(u0, u0 + hg)
        s = lax.dot_general(q_scr[us], k_scr[us], (((2,), (2,)), ((0,), (0,))), preferred_element_type=f32)
        mx = jnp.max(s, axis=-1, keepdims=True)
        p = jnp.exp2(s - mx).astype(bf16)
        r = lax.dot_general(p, v_scr[us], (((2,), (1,)), ((0,), (0,))), preferred_element_type=f32)
        for j in range(hg // 2):
            re, ro = r[2 * j], r[2 * j + 1]
            oe = re * (1.0 / re[:, V_DIM:V_DIM + 1])
            oo = ro * (1.0 / ro[:, 0:1])
            pairs.append(jnp.where(lane < V_DIM, oe, oo))
    per_seq = N_HEADS // 2
    attn = jnp.concatenate(
        [jnp.concatenate(pairs[g * per_seq:(g + 1) * per_seq], axis=-1) for g in range(sb)], axis=0).astype(bf16)
    merged = (cm_ref[...].astype(f32) + g_ref[...].astype(f32) * _dot(attn, wo_ref[...])).astype(bf16)

    g1 = m_ref[2:3, :]
    sh2 = m_ref[3:4, :]
    sc2 = m_ref[4:5, :]
    x1 = x_ref[...] + g1 * _dot(merged, wout_ref[...])
    x1_ref[...] = x1
    h2 = _rms(x1, n2_ref[...]) * (1.0 + sc2) + sh2
    hi = h2.astype(bf16)
    h2_ref[...] = hi
    lo = (h2 - hi.astype(f32)).astype(bf16)
    both = _dot(jnp.concatenate([hi, lo], axis=0), wr_ref[...])
    rows = hi.shape[0]
    s = both[0:rows] + both[rows:2 * rows]
    lg_ref[...] = s + pltpu.roll(s, LANES - N_EXPERTS, axis=1)


def _attn(q2d, ckv3, kpe3, cm2d, g2d, x2d, m3, wk, wv, wo, wout, norm2, wr, *, n, tq):
    nseq, s_len, _ = ckv3.shape
    sb = max(1, ATTN_TOKENS // n) if (tq == n and m3.shape[0] == 1) else 1
    units = sb * N_HEADS
    hg = max(2, min(units, SCORE_BYTES // (tq * s_len * 4)))
    assert units % hg == 0 and hg % 2 == 0 and nseq % sb == 0
    qb = n // tq
    tokens = nseq * n
    tile = lambda w: pl.BlockSpec((sb * tq, w), lambda s, i: (s * qb + i, 0))
    return pl.pallas_call(
        functools.partial(_attn_kernel, sb=sb, hg=hg),
        grid=(nseq // sb, qb),
        in_specs=[
            tile(N_HEADS * HEAD_W),
            pl.BlockSpec((sb, s_len, KV_LORA), lambda s, i: (s, 0, 0)),
            pl.BlockSpec((sb, s_len, LANES), lambda s, i: (s, 0, 0)),
            tile(D_MODEL),
            tile(D_MODEL),
            tile(D_MODEL),
            _mod_spec(m3, lambda s, i: s),
            _const_spec(wk.shape),
            _const_spec(wv.shape),
            _const_spec(wo.shape),
            _const_spec(wout.shape),
            _const_spec((1, D_MODEL)),
            _const_spec(wr.shape),
        ],
        out_specs=(tile(D_MODEL), tile(D_MODEL), tile(LANES)),
        out_shape=(
            jax.ShapeDtypeStruct((tokens, D_MODEL), f32),
            jax.ShapeDtypeStruct((tokens, D_MODEL), bf16),
            jax.ShapeDtypeStruct((tokens, LANES), f32),
        ),
        scratch_shapes=[
            pltpu.VMEM((units, s_len, HEAD_W), bf16),
            pltpu.VMEM((units, s_len, LANES), bf16),
            pltpu.VMEM((units, tq, HEAD_W), bf16),
        ],
        compiler_params=_params(VMEM_LIMIT, "parallel", "arbitrary"),
        name="attn_out",
    )(q2d, ckv3, kpe3, cm2d, g2d, x2d, m3, wk, wv, wo, wout, norm2, wr)


def _route_kernel(lg_ref, pos_ref, aff_ref, meta_ref, *, nseq, n, cap):
    for s in range(nseq):
        lt = lg_ref[s].T[0:N_EXPERTS, :]
        e = jnp.exp(lt - jnp.max(lt, axis=0, keepdims=True))
        aff_ref[s * N_EXPERTS:(s + 1) * N_EXPERTS, :] = e / jnp.sum(e, axis=0, keepdims=True)
    rows = nseq * N_EXPERTS
    capf = float(cap)

    def bit_step(i, t):
        cand = t | (jnp.int32(1) << (30 - i))
        thr = lax.bitcast_convert_type(cand, f32)
        cnt = jnp.sum(jnp.where(aff_ref[...] >= thr, 1.0, 0.0), axis=1, keepdims=True)
        return jnp.where(cnt >= capf, cand, t)

    t = lax.fori_loop(0, 31, bit_step, jnp.zeros((rows, 1), jnp.int32))
    thr = lax.bitcast_convert_type(t, f32)
    need = capf - jnp.sum(jnp.where(aff_ref[...] > thr, 1.0, 0.0), axis=1, keepdims=True)

    blk = MOE_BLOCK
    tri = jnp.where(
        lax.broadcasted_iota(jnp.int32, (blk, blk), 0) < lax.broadcasted_iota(jnp.int32, (blk, blk), 1),
        1.0, 0.0).astype(bf16)
    carry_gt = jnp.zeros((rows, 1), f32)
    carry_eq = jnp.zeros((rows, 1), f32)
    lane = lax.broadcasted_iota(jnp.int32, (rows, LANES), 1)
    meta = jnp.zeros((rows, LANES), f32)
    cmax = jnp.zeros((rows, 1), f32)
    for b in range(n // blk):
        sl = slice(b * blk, (b + 1) * blk)
        ab = aff_ref[:, sl]
        gt = ab > thr
        eq = ab == thr
        gtb = jnp.where(gt, 1.0, 0.0)
        eqb = jnp.where(eq, 1.0, 0.0)
        pre_gt = _dot(gtb.astype(bf16), tri) + carry_gt
        pre_eq = _dot(eqb.astype(bf16), tri) + carry_eq
        meta = jnp.where(lane == b, carry_gt + jnp.minimum(carry_eq, need), meta)
        carry_gt = carry_gt + jnp.sum(gtb, axis=1, keepdims=True)
        carry_eq = carry_eq + jnp.sum(eqb, axis=1, keepdims=True)
        sel = gt | (eq & (pre_eq < need))
        slot = pre_gt + jnp.minimum(pre_eq, need)
        pos_ref[:, sl] = jnp.where(sel, slot, -1.0).astype(jnp.int32)
        cmax = jnp.maximum(cmax, jnp.sum(jnp.where(sel, 1.0, 0.0), axis=1, keepdims=True))
    meta_ref[...] = jnp.where(lane == LANES - 1, cmax, meta).astype(jnp.int32)


def _route(lg3, *, cap):
    nseq, n, _ = lg3.shape
    rows = nseq * N_EXPERTS
    assert n // MOE_BLOCK < LANES
    return pl.pallas_call(
        functools.partial(_route_kernel, nseq=nseq, n=n, cap=cap),
        grid=(1,),
        in_specs=[_const_spec(lg3.shape)],
        out_specs=(_const_spec((rows, n)), _const_spec((rows, n)), _const_spec((rows, LANES))),
        out_shape=(jax.ShapeDtypeStruct((rows, n), jnp.int32), jax.ShapeDtypeStruct((rows, n), f32),
                   jax.ShapeDtypeStruct((rows, LANES), jnp.int32)),
        compiler_params=_params(VMEM_SMALL, "arbitrary"),
        name="route",
    )(lg3)


def _slot_hits(pos_ref, e, cap):
    width = pos_ref.shape[1]
    return lax.broadcasted_iota(jnp.int32, (cap, width), 0) == pos_ref[e:e + 1, :]


def _one_hot(hits):
    return jnp.concatenate([jnp.where(h, 1.0, 0.0).astype(bf16) for h in hits], axis=0)


def _gather_dense(pos_ref, aff_ref, h2_ref, xg_ref, vals_ref, rows, cap, ne):
    for e0 in range(0, N_EXPERTS, ne):
        hits = [_slot_hits(pos_ref, e0 + e, cap) for e in range(ne)]
        xg = _dot(_one_hot(hits), h2_ref[...]).astype(xg_ref.dtype)
        for e in range(ne):
            xg_ref[e0 + e, rows, :] = xg[e * cap:(e + 1) * cap, :]
            vals = jnp.sum(jnp.where(hits[e], aff_ref[e0 + e:e0 + e + 1, :], 0.0), axis=1, keepdims=True)
            vals_ref[e0 + e, rows, :] = jnp.broadcast_to(vals, (cap, LANES))


def _gather_kernel(pos_ref, aff_ref, h2_ref, xg_ref, vals_ref, *, cap, ne, sb):
    for g in range(sb):
        _gather_dense(pos_ref.at[g], aff_ref.at[g], h2_ref.at[g], xg_ref, vals_ref,
                      slice(g * cap, (g + 1) * cap), cap, ne)


def _gather(pos3, aff3, h23, *, cap, ne, sb):
    nseq, _, n = pos3.shape
    return pl.pallas_call(
        functools.partial(_gather_kernel, cap=cap, ne=ne, sb=sb),
        grid=(nseq // sb,),
        in_specs=[
            pl.BlockSpec((sb, N_EXPERTS, n), lambda s: (s, 0, 0)),
            pl.BlockSpec((sb, N_EXPERTS, n), lambda s: (s, 0, 0)),
            pl.BlockSpec((sb, n, D_MODEL), lambda s: (s, 0, 0)),
        ],
        out_specs=(
            pl.BlockSpec((N_EXPERTS, sb * cap, D_MODEL), lambda s: (0, s, 0)),
            pl.BlockSpec((N_EXPERTS, sb * cap, LANES), lambda s: (0, s, 0)),
        ),
        out_shape=(
            jax.ShapeDtypeStruct((N_EXPERTS, nseq * cap, D_MODEL), bf16),
            jax.ShapeDtypeStruct((N_EXPERTS, nseq * cap, LANES), f32),
        ),
        compiler_params=_params(VMEM_MID, "parallel"),
        name="gather",
    )(pos3, aff3, h23)


def _windows_fit(meta_ref):
    cmax = meta_ref[0, LANES - 1]
    for e in range(1, N_EXPERTS):
        cmax = jnp.maximum(cmax, meta_ref[e, LANES - 1])
    return cmax <= MOE_WIN - WIN_ALIGN


def _win_base(meta_ref, e, b, cap):
    start = meta_ref[e, b]
    return pl.multiple_of(jnp.minimum(start - start % WIN_ALIGN, cap - MOE_WIN), WIN_ALIGN)


def _win_hits(pos_row, base):
    slot = lax.broadcasted_iota(jnp.int32, (MOE_WIN, pos_row.shape[1]), 0) + base
    return slot == pos_row


def _gather_win_kernel(meta_ref, pos_ref, aff_ref, h2_ref, xg_ref, vals_ref, *, n, cap):
    fits = _windows_fit(meta_ref)

    @pl.when(fits)
    def _():
        xg_ref[...] = jnp.zeros(xg_ref.shape, xg_ref.dtype)
        vals_ref[...] = jnp.zeros(vals_ref.shape, vals_ref.dtype)

        def block(b, carry):
            cols = pl.ds(pl.multiple_of(b * MOE_BLOCK, MOE_BLOCK), MOE_BLOCK)
            bases = [_win_base(meta_ref, e, b, cap) for e in range(N_EXPERTS)]
            hits = [_win_hits(pos_ref[e:e + 1, cols], bases[e]) for e in range(N_EXPERTS)]
            part = _dot(_one_hot(hits), h2_ref[cols, :])
            for e in range(N_EXPERTS):
                rows = pl.ds(bases[e], MOE_WIN)
                xg_ref[e, rows, :] += part[e * MOE_WIN:(e + 1) * MOE_WIN, :].astype(xg_ref.dtype)
                vals = jnp.sum(jnp.where(hits[e], aff_ref[e:e + 1, cols], 0.0), axis=1, keepdims=True)
                vals_ref[e, rows, :] += jnp.broadcast_to(vals, (MOE_WIN, LANES))
            return carry

        lax.fori_loop(0, n // MOE_BLOCK, block, 0)

    @pl.when(jnp.logical_not(fits))
    def _():
        _gather_dense(pos_ref, aff_ref, h2_ref, xg_ref, vals_ref, slice(0, cap), cap, 1)


def _gather_win(meta, pos3, aff3, h23, *, cap):
    nseq, _, n = pos3.shape
    return pl.pallas_call(
        functools.partial(_gather_win_kernel, n=n, cap=cap),
        grid=(nseq,),
        in_specs=[
            pl.BlockSpec((N_EXPERTS, LANES), lambda s: (s, 0), memory_space=pltpu.SMEM),
            pl.BlockSpec((None, N_EXPERTS, n), lambda s: (s, 0, 0)),
            pl.BlockSpec((None, N_EXPERTS, n), lambda s: (s, 0, 0)),
            pl.BlockSpec((None, n, D_MODEL), lambda s: (s, 0, 0)),
        ],
        out_specs=(
            pl.BlockSpec((N_EXPERTS, cap, D_MODEL), lambda s: (0, s, 0)),
            pl.BlockSpec((N_EXPERTS, cap, LANES), lambda s: (0, s, 0)),
        ),
        out_shape=(
            jax.ShapeDtypeStruct((N_EXPERTS, nseq * cap, D_MODEL), bf16),
            jax.ShapeDtypeStruct((N_EXPERTS, nseq * cap, LANES), f32),
        ),
        compiler_params=_params(VMEM_LARGE, "parallel"),
        name="gather_win",
    )(meta, pos3, aff3, h23)


def _experts_kernel(xp_ref, vp_ref, xs_ref, vs_ref, wg_ref, wu_ref, wd_ref, yp_ref, ys_ref, *, rc):
    wg = wg_ref[...].astype(bf16)
    wu = wu_ref[...].astype(bf16)
    wd = wd_ref[...].astype(bf16)
    for x_ref, v_ref, y_ref in ((xp_ref, vp_ref, yp_ref), (xs_ref, vs_ref, ys_ref)):
        for r0 in range(0, x_ref.shape[0], rc):
            x = x_ref[r0:r0 + rc, :]
            a = _dot(x, wg)
            u = _dot(x, wu)
            hm = (a * jax.nn.sigmoid(a) * u).astype(bf16)
            y = _dot(hm, wd) * v_ref[r0:r0 + rc, 0:1]
            y_ref[r0:r0 + rc, :] = y.astype(y_ref.dtype)


def _experts(xg_p, vals_p, xg_s, vals_s, wg, wu, wd, *, rc=EXPERT_ROWS):
    rp = xg_p.shape[1]
    rs = xg_s.shape[1]
    per_e = lambda r, w: pl.BlockSpec((None, r, w), lambda e: (e, 0, 0))
    return pl.pallas_call(
        functools.partial(_experts_kernel, rc=rc),
        grid=(N_EXPERTS,),
        in_specs=[
            per_e(rp, D_MODEL), per_e(rp, LANES), per_e(rs, D_MODEL), per_e(rs, LANES),
            per_e(D_MODEL, EXPERT_FF), per_e(D_MODEL, EXPERT_FF), per_e(EXPERT_FF, D_MODEL),
        ],
        out_specs=(per_e(rp, D_MODEL), per_e(rs, D_MODEL)),
        out_shape=(
            jax.ShapeDtypeStruct((N_EXPERTS, rp, D_MODEL), bf16),
            jax.ShapeDtypeStruct((N_EXPERTS, rs, D_MODEL), bf16),
        ),
        compiler_params=_params(VMEM_LARGE, "parallel"),
        name="experts",
    )(xg_p, vals_p, xg_s, vals_s, wg, wu, wd)


def _scatter_dense(pos_ref, y_ref, rows, tn, cap, ne):
    moe = jnp.zeros((tn, D_MODEL), f32)
    for e0 in range(0, N_EXPERTS, ne):
        onehot = _one_hot([_slot_hits(pos_ref, e0 + e, cap) for e in range(ne)])
        y = jnp.concatenate([y_ref[e0 + e, rows, :] for e in range(ne)], axis=0)
        moe = moe + lax.dot_general(onehot, y, (((0,), (0,)), ((), ())), preferred_element_type=f32)
    return moe


def _scatter_kernel(pos_ref, y_ref, x1_ref, m_ref, fn_ref, o_ref, *, tn, cap, ne, sb):
    g2 = m_ref[5:6, :]
    for g in range(sb):
        moe = _scatter_dense(pos_ref.at[g], y_ref, slice(g * cap, (g + 1) * cap), tn, cap, ne)
        o_ref[g] = _rms(x1_ref[g] + g2 * moe, fn_ref[...])


def _scatter(pos3, y, x13, m3, fn, *, cap, ne, tn, sb):
    nseq, _, n = pos3.shape
    assert sb == 1 or m3.shape[0] == 1, "sequences sharing a grid step must share their modulation rows"
    return pl.pallas_call(
        functools.partial(_scatter_kernel, tn=tn, cap=cap, ne=ne, sb=sb),
        grid=(nseq // sb, n // tn),
        in_specs=[
            pl.BlockSpec((sb, N_EXPERTS, tn), lambda s, i: (s, 0, i)),
            pl.BlockSpec((N_EXPERTS, sb * cap, D_MODEL), lambda s, i: (0, s, 0)),
            pl.BlockSpec((sb, tn, D_MODEL), lambda s, i: (s, i, 0)),
            _mod_spec(m3, lambda s, i: s),
            _const_spec((1, D_MODEL)),
        ],
        out_specs=pl.BlockSpec((sb, tn, D_MODEL), lambda s, i: (s, i, 0)),
        out_shape=jax.ShapeDtypeStruct((nseq, n, D_MODEL), f32),
        compiler_params=_params(VMEM_MID, "parallel", "arbitrary"),
        name="scatter",
    )(pos3, y, x13, m3, fn)


def _scatter_win_kernel(meta_ref, pos_ref, y_ref, x1_ref, m_ref, fn_ref, o_ref, moe_scr, *, cap):
    b = pl.program_id(1)
    fits = _windows_fit(meta_ref)

    @pl.when(fits)
    def _():
        bases = [_win_base(meta_ref, e, b, cap) for e in range(N_EXPERTS)]
        onehot = _one_hot([_win_hits(pos_ref[e:e + 1, :], bases[e]) for e in range(N_EXPERTS)])
        y = jnp.concatenate([y_ref[e, pl.ds(bases[e], MOE_WIN), :] for e in range(N_EXPERTS)], axis=0)
        moe_scr[...] = lax.dot_general(onehot, y, (((0,), (0,)), ((), ())), preferred_element_type=f32)

    @pl.when(jnp.logical_not(fits))
    def _():
        moe_scr[...] = _scatter_dense(pos_ref, y_ref, slice(0, cap), MOE_BLOCK, cap, 1)

    o_ref[...] = _rms(x1_ref[...] + m_ref[5:6, :] * moe_scr[...], fn_ref[...])


def _scatter_win(meta, pos3, y, x13, m3, fn, *, cap):
    nseq, _, n = pos3.shape
    tn = MOE_BLOCK
    return pl.pallas_call(
        functools.partial(_scatter_win_kernel, cap=cap),
        grid=(nseq, n // tn),
        in_specs=[
            pl.BlockSpec((N_EXPERTS, LANES), lambda s, i: (s, 0), memory_space=pltpu.SMEM),
            pl.BlockSpec((None, N_EXPERTS, tn), lambda s, i: (s, 0, i)),
            pl.BlockSpec((N_EXPERTS, cap, D_MODEL), lambda s, i: (0, s, 0)),
            pl.BlockSpec((None, tn, D_MODEL), lambda s, i: (s, i, 0)),
            _mod_spec(m3, lambda s, i: s),
            _const_spec((1, D_MODEL)),
        ],
        out_specs=pl.BlockSpec((None, tn, D_MODEL), lambda s, i: (s, i, 0)),
        out_shape=jax.ShapeDtypeStruct((nseq, n, D_MODEL), f32),
        scratch_shapes=[pltpu.VMEM((tn, D_MODEL), f32)],
        compiler_params=_params(VMEM_MID, "parallel", "arbitrary"),
        name="scatter_win",
    )(meta, pos3, y, x13, m3, fn)


def _rope_tables(n):
    t = np.arange(n)
    half = QK_ROPE // 2
    freqs = ROPE_BASE ** (-np.arange(0, half, 2, dtype=np.float64) / half)
    ang_r = (t // GRID_W)[:, None] * freqs
    ang_c = (t % GRID_W)[:, None] * freqs
    cr, sr, cc, sc = np.cos(ang_r), np.sin(ang_r), np.cos(ang_c), np.sin(ang_c)
    cos = np.ones((n, HEAD_W))
    sin = np.zeros((n, HEAD_W))
    cos[:, ROPE_OFF:ROPE_OFF + QK_ROPE] = np.concatenate([cr, cr, cc, cc], axis=-1)
    sin[:, ROPE_OFF:ROPE_OFF + QK_ROPE] = np.concatenate([-sr, sr, -sc, sc], axis=-1)
    return jnp.asarray(cos, f32), jnp.asarray(sin, f32)


_PARTNER = np.concatenate([np.arange(8, 16), np.arange(0, 8), np.arange(24, 32), np.arange(16, 24)])


def _rope_partner(w):
    q = QK_ROPE // 4
    return jnp.concatenate([w[..., q:2 * q], w[..., 0:q], w[..., 3 * q:4 * q], w[..., 2 * q:3 * q]], axis=-1)


def _rope_placement():
    place = np.zeros((LANES, 2 * LANES), np.float32)
    d = np.arange(QK_ROPE)
    place[d, ROPE_OFF + d] = 1.0
    place[_PARTNER, LANES + ROPE_OFF + d] = 1.0
    return jnp.asarray(place, bf16)


def _head_blocks(w_nope, w_rope):
    rows = w_nope.shape[0]
    if w_rope is None:
        w_rope = jnp.zeros((rows, N_HEADS, QK_ROPE), w_nope.dtype)
    z = jnp.zeros((rows, N_HEADS, HEAD_W - QK_NOPE - QK_ROPE), w_nope.dtype)
    return jnp.concatenate([w_nope, w_rope, z], axis=-1).reshape(rows, N_HEADS * HEAD_W)


def kernel(x_prompt, x_sample, cache_ckv, cache_kpe, c, c_ctx, w_ada, b_ada, norm1, w_in, conv_dw, conv_dw_b,
           conv_ln_g, conv_ln_b, w_conv_out, q_norm, w_qb, kv_norm, w_kvb, w_o_mla, w_out, norm2, w_router,
           w_e_gate, w_e_up, w_e_down, final_norm):
    assert w_ada.shape[0] == 1, "single trunk layer"
    nb_p, n_p, _ = x_prompt.shape
    nb_s, n_s, _ = x_sample.shape

    win = _wprep(w_in[0].T, _rope_placement())
    wq = w_qb[0].reshape(Q_LORA, N_HEADS, QK_NOPE + QK_ROPE)
    wqb = _head_blocks(wq[..., :QK_NOPE], wq[..., QK_NOPE:]).astype(bf16)
    wqbp = _head_blocks(jnp.zeros_like(wq[..., :QK_NOPE]), _rope_partner(wq[..., QK_NOPE:])).astype(bf16)
    wkv = w_kvb[0].reshape(KV_LORA, N_HEADS, QK_NOPE + V_DIM)
    wk = _head_blocks(wkv[..., :QK_NOPE], None).astype(bf16)
    wv = wkv[..., QK_NOPE:].reshape(KV_LORA, N_HEADS * V_DIM).astype(bf16)
    wco = w_conv_out[0].astype(bf16)
    wo = w_o_mla[0].astype(bf16)
    wout = w_out[0].astype(bf16)
    wr_hi = w_router[0].astype(bf16)
    wr_lo = (w_router[0] - wr_hi.astype(f32)).astype(bf16)
    wr = jnp.concatenate([wr_hi, wr_lo, jnp.zeros((D_MODEL, LANES - 2 * N_EXPERTS), bf16)], axis=-1)
    row = lambda a: a.reshape(1, -1)

    mod = jnp.concatenate([c_ctx[None, :], c, jnp.zeros((8 - 1 - nb_s, D_MODEL), f32)], axis=0)
    m = _ada(mod, w_ada[0], b_ada[0]).reshape(8, 6, D_MODEL)
    m_p, m_s = m[0:1], m[1:1 + nb_s]
    cos, sin = _rope_tables(n_s)

    conv_w = _conv_weights(conv_dw[0], row(conv_dw_b[0]), row(conv_ln_g[0]), row(conv_ln_b[0]), wco, CONV_CHUNK)

    def mixers(x, m3, rope, ctx_ckv, ctx_kpe):
        nseq, n, _ = x.shape
        x2d = x.reshape(nseq * n, D_MODEL)
        cm, q, ckv, kr, g, *kpe = _inproj(x2d, m3, row(norm1[0]), win, row(q_norm[0]), wqb, wqbp, row(kv_norm[0]),
                                          cos, sin, conv_w, n=n, rope=rope, tm=IN_TILE)
        keys_ckv = ckv.reshape(nseq, n, KV_LORA)
        keys_kpe = kr.reshape(nseq, n, LANES)
        if ctx_ckv is not None:
            keys_ckv = jnp.concatenate([ctx_ckv.astype(keys_ckv.dtype), keys_ckv], axis=1)
            keys_kpe = jnp.concatenate([ctx_kpe.astype(keys_kpe.dtype), keys_kpe], axis=1)
        x1, h2, lg = _attn(q, keys_ckv, keys_kpe, cm.reshape(nseq * n, D_MODEL), g, x2d, m3, wk, wv, wo, wout,
                           row(norm2[0]), wr, n=n, tq=min(n, Q_TILE))
        return x1, h2, lg, ckv, kpe

    ctx_kpe = jnp.pad(cache_kpe[:, 0], ((0, 0), (0, 0), (ROPE_OFF, LANES - ROPE_OFF - QK_ROPE)))
    x1_p, h2_p, lg_p, ckv_p, (kpe_p,) = mixers(x_prompt, m_p, False, None, None)
    x1_s, h2_s, lg_s, _, _ = mixers(x_sample, m_s, True, cache_ckv[:, 0], ctx_kpe)

    def moe_tiles(n):
        cap = EC_FACTOR * n // N_EXPERTS
        ne = N_EXPERTS if N_EXPERTS * cap <= MOE_ROWS else 1
        sb = max(1, MOE_TOKENS // n)
        return cap, ne, sb

    def windowed(n):
        return n >= 4 * MOE_BLOCK and EC_FACTOR * n // N_EXPERTS >= 2 * MOE_WIN

    def route_gather(h2, lg, nseq, n):
        cap, ne, sb = moe_tiles(n)
        pos, aff, meta = _route(lg.reshape(nseq, n, LANES), cap=cap)
        pos3 = pos.reshape(nseq, N_EXPERTS, n)
        aff3 = aff.reshape(nseq, N_EXPERTS, n)
        h23 = h2.reshape(nseq, n, D_MODEL)
        if windowed(n):
            xg, vals = _gather_win(meta, pos3, aff3, h23, cap=cap)
        else:
            xg, vals = _gather(pos3, aff3, h23, cap=cap, ne=ne, sb=sb)
        return pos3, meta, xg, vals

    pos_p, meta_p, xg_p, vals_p = route_gather(h2_p, lg_p, nb_p, n_p)
    pos_s, meta_s, xg_s, vals_s = route_gather(h2_s, lg_s, nb_s, n_s)
    y_p, y_s = _experts(xg_p, vals_p, xg_s, vals_s, w_e_gate[0], w_e_up[0], w_e_down[0])
    fn = row(final_norm)

    def scatter(pos, meta, y, x1, m3, nseq, n):
        cap, ne, sb = moe_tiles(n)
        x13 = x1.reshape(nseq, n, D_MODEL)
        if windowed(n):
            return _scatter_win(meta, pos, y, x13, m3, fn, cap=cap)
        return _scatter(pos, y, x13, m3, fn, cap=cap, ne=ne, tn=min(n, SCATTER_TILE), sb=sb)

    y_prompt = scatter(pos_p, meta_p, y_p, x1_p, m_p, nb_p, n_p)
    y_sample = scatter(pos_s, meta_s, y_s, x1_s, m_s, nb_s, n_s)

    new_ckv = ckv_p.reshape(nb_p, 1, n_p, KV_LORA)
    new_kpe = kpe_p.reshape(nb_p, 1, n_p, QK_ROPE)
    return (y_prompt, y_sample, new_ckv, new_kpe)
```

```python
import functools

import jax
import jax.numpy as jnp
import numpy as np
from jax import lax
from jax.experimental import pallas as pl
from jax.experimental.pallas import tpu as pltpu

D_MODEL = 1024
GRID_W = 64
CONV_DIM = 512
CONV_WIDTH = 31
N_HEADS = 8
QK_NOPE = 64
QK_ROPE = 32
V_DIM = 64
Q_LORA = 256
KV_LORA = 128
N_EXPERTS = 16
EXPERT_FF = 512
EC_FACTOR = 2
ROPE_BASE = 10000.0
EPS = 1e-6

LANES = 128
HEAD_W = LANES
ROPE_OFF = QK_NOPE
CONV_HALO = 16
LOG2E = 1.4426950408889634
VMEM_LIMIT = 48 * 1024 * 1024
IN_TILE = 512
CONV_CHUNK = 64
CONV_OUT_ROWS = 256
GATE_CHUNK = 512
EXPERT_ROWS = 768
Q_TILE = 512
ATTN_TOKENS = 1024
SCORE_BYTES = 12 * 1024 * 1024
MOE_ROWS = 512
MOE_TOKENS = 1024
SCATTER_TILE = 512
MOE_BLOCK = 2 * LANES
MOE_WIN = 80
WIN_ALIGN = 16

C_CONV = 0
C_QA = 2 * CONV_DIM
C_KVA = C_QA + Q_LORA
C_KR = C_KVA + KV_LORA
C_GATE = C_KR + LANES
C_END = C_GATE + 2 * D_MODEL
C_KRP = C_END

f32 = jnp.float32
bf16 = jnp.bfloat16


MIB = 1024 * 1024
VMEM_SMALL, VMEM_MID, VMEM_LARGE = 16 * MIB, 32 * MIB, 40 * MIB


def _params(vmem_bytes, *sem):
    assert vmem_bytes <= VMEM_LIMIT
    return pltpu.CompilerParams(dimension_semantics=sem, vmem_limit_bytes=vmem_bytes)


def _dot(a, b):
    return jnp.dot(a, b, preferred_element_type=f32)


def _rms(x, g):
    return x * lax.rsqrt(jnp.mean(x * x, axis=-1, keepdims=True) + EPS) * g


def _const_spec(shape):
    nd = len(shape)
    return pl.BlockSpec(shape, lambda *_: (0,) * nd)


def _mod_spec(m3, seq_of):
    if m3.shape[0] == 1:
        return _const_spec((None, 6, D_MODEL))
    return pl.BlockSpec((None, 6, D_MODEL), lambda *idx: (seq_of(*idx), 0, 0))


def _ada_kernel(s_ref, w_ref, b_ref, o_ref):
    s = s_ref[...]
    s = s * jax.nn.sigmoid(s)
    o_ref[...] = _dot(s.astype(bf16), w_ref[...].astype(bf16)) + b_ref[...]


def _ada(mod, w_ada, b_ada):
    rows = mod.shape[0]
    n_out = w_ada.shape[1]
    tn = D_MODEL
    return pl.pallas_call(
        _ada_kernel,
        grid=(n_out // tn,),
        in_specs=[
            _const_spec((rows, D_MODEL)),
            pl.BlockSpec((D_MODEL, tn), lambda j: (0, j)),
            pl.BlockSpec((1, tn), lambda j: (0, j)),
        ],
        out_specs=pl.BlockSpec((rows, tn), lambda j: (0, j)),
        out_shape=jax.ShapeDtypeStruct((rows, n_out), f32),
        compiler_params=_params(VMEM_LIMIT, "arbitrary"),
        name="ada",
    )(mod, w_ada, b_ada.reshape(1, n_out))


def _wprep_kernel(wt_ref, place_ref, o_ref):
    def block(r0):
        return wt_ref[r0:r0 + LANES, :].T.astype(bf16)

    for j in range(C_KR // LANES):
        o_ref[:, j * LANES:(j + 1) * LANES] = block(j * LANES)
    placed = _dot(block(C_KR), place_ref[...])
    o_ref[:, C_KR:C_GATE] = placed[:, 0:LANES].astype(bf16)
    o_ref[:, C_KRP:C_KRP + LANES] = placed[:, LANES:2 * LANES].astype(bf16)
    for j in range(2 * D_MODEL // LANES):
        o_ref[:, C_GATE + j * LANES:C_GATE + (j + 1) * LANES] = block(C_KR + QK_ROPE + j * LANES)


def _wprep(w_in_t, place):
    cols, rows = w_in_t.shape
    return pl.pallas_call(
        _wprep_kernel,
        grid=(rows // LANES,),
        in_specs=[pl.BlockSpec((cols, LANES), lambda i: (0, i)), _const_spec(place.shape)],
        out_specs=pl.BlockSpec((LANES, C_KRP + LANES), lambda i: (i, 0)),
        out_shape=jax.ShapeDtypeStruct((rows, C_KRP + LANES), bf16),
        compiler_params=_params(VMEM_SMALL, "parallel"),
        name="wprep",
    )(w_in_t, place)


def _conv_taps(vpad, r0, shift_ref, dw_ref, dwb_ref, ybuf, y0, ct):
    pad = CONV_WIDTH // 2
    sub = 8
    span = ((CONV_HALO - pad + CONV_WIDTH - 1) // sub) * sub
    cw = 2 * LANES
    for cb in range(CONV_DIM // cw):
        sl = slice(cb * cw, (cb + 1) * cw)
        win = vpad[pl.ds(r0, ct + 2 * CONV_HALO), sl]
        acc = jnp.zeros((ct // sub, sub, cw), f32)
        for ph in range(sub):
            wph = win.astype(f32) if ph == 0 else _dot(shift_ref[ph - 1], win)
            for a in range(span // sub + 1):
                k = a * sub + ph - (CONV_HALO - pad)
                if 0 <= k < CONV_WIDTH:
                    acc = acc + wph[a * sub:a * sub + ct, :].reshape(ct // sub, sub, cw) * dw_ref[k, :, sl][None]
        ybuf[pl.ds(y0, ct), sl] = acc.reshape(ct, cw) + dwb_ref[:, sl]


def _conv_out(y, lng_ref, lnb_ref, wco_ref, gate):
    mu = jnp.mean(y, axis=-1, keepdims=True)
    yc = y - mu
    var = jnp.mean(yc * yc, axis=-1, keepdims=True)
    z = yc * lax.rsqrt(var + EPS) * lng_ref[...] + lnb_ref[...]
    z = z * jax.nn.sigmoid(z)
    return (gate.astype(f32) * _dot(z.astype(bf16), wco_ref[...])).astype(bf16)


def _fill_padded(vpad, v, n):
    zeros = jnp.zeros((CONV_HALO, CONV_DIM), vpad.dtype)
    vpad[0:CONV_HALO, :] = zeros
    vpad[CONV_HALO + n:2 * CONV_HALO + n, :] = zeros
    vpad[CONV_HALO:CONV_HALO + n, :] = v


def _inproj_kernel(*refs, rope, halo, tiles_per_seq, q_scale, n, rt, ct):
    refs = list(refs)
    x_ref = refs.pop(0)
    if halo:
        xp_ref, xn_ref = refs[:2]
        del refs[:2]
    m_ref, n1_ref, win_ref, qn_ref, wqb_ref = refs[:5]
    del refs[:5]
    if rope:
        wqbp_ref = refs.pop(0)
    kvn_ref = refs.pop(0)
    if rope:
        cos_ref, sin_ref = refs[:2]
        del refs[:2]
    shift_ref, dw_ref, dwb_ref, lng_ref, lnb_ref, wco_ref = refs[:6]
    del refs[:6]
    cm_ref, q_ref, ckv_ref, kr_ref, g_ref = refs[:5]
    del refs[:5]
    if not rope:
        kpe_ref = refs.pop(0)
    vpad, ybuf, gc = refs
    tm = x_ref.shape[0]
    lo = CONV_HALO if halo else 0

    sh1 = m_ref[0:1, :]
    sc1 = m_ref[1:2, :]
    x = jnp.concatenate([xp_ref[...], x_ref[...], xn_ref[...]], axis=0) if halo else x_ref[...]
    hb_all = (_rms(x, n1_ref[...]) * (1.0 + sc1) + sh1).astype(bf16)

    left_all = _dot(hb_all, win_ref[:, 0:C_GATE])
    v = (left_all[:, C_CONV:C_CONV + CONV_DIM] * jax.nn.sigmoid(left_all[:, C_CONV + CONV_DIM:C_QA])).astype(bf16)
    hb = hb_all[lo:lo + tm, :]
    left = left_all[lo:lo + tm, :]

    if halo:
        j = pl.program_id(0) % tiles_per_seq
        zeros = jnp.zeros((CONV_HALO, CONV_DIM), bf16)
        vpad[0, 0:lo, :] = jnp.where(j > 0, v[0:lo, :], zeros)
        vpad[0, lo:lo + tm, :] = v[lo:lo + tm, :]
        vpad[0, lo + tm:2 * lo + tm, :] = jnp.where(j < tiles_per_seq - 1, v[lo + tm:2 * lo + tm, :], zeros)
        chunks = [(0, c * ct, c * ct) for c in range(tm // ct)]
    else:
        for s in range(tm // n):
            _fill_padded(vpad.at[s], v[s * n:(s + 1) * n, :], n)
        chunks = [(s, c * ct, s * n + c * ct) for s in range(tm // n) for c in range(n // ct)]

    def conv_some(count):
        for _ in range(min(count, len(chunks))):
            s, r0, y0 = chunks.pop(0)
            _conv_taps(vpad.at[s], r0, shift_ref, dw_ref, dwb_ref, ybuf, y0, ct)

    steps = 2 + 2 * D_MODEL // GATE_CHUNK
    per_step = -(-len(chunks) // steps)

    qa = left[:, C_QA:C_KVA]
    qn = _rms(qa, qn_ref[...]).astype(bf16)
    q = _dot(qn, wqb_ref[...])
    if rope:
        qp = _dot(qn, wqbp_ref[...])
        cos = cos_ref[...]
        sin = sin_ref[...]
        for hd in range(N_HEADS):
            sl = slice(hd * HEAD_W, (hd + 1) * HEAD_W)
            q_ref[:, sl] = ((q[:, sl] * cos + qp[:, sl] * sin) * q_scale).astype(q_ref.dtype)
    else:
        q_ref[...] = (q * q_scale).astype(q_ref.dtype)
    conv_some(per_step)

    kva = left[:, C_KVA:C_KR]
    ckv_ref[...] = _rms(kva, kvn_ref[...]).astype(ckv_ref.dtype)

    kr = left[:, C_KR:C_GATE]
    if rope:
        krp = _dot(hb, win_ref[:, C_KRP:C_KRP + LANES])
        kr = kr * cos_ref[...] + krp * sin_ref[...]
    else:
        kpe_ref[...] = kr[:, ROPE_OFF:ROPE_OFF + QK_ROPE]
    kr_ref[...] = kr.astype(kr_ref.dtype)
    conv_some(per_step)

    gw = GATE_CHUNK
    for jg in range(2 * D_MODEL // gw):
        gate = jax.nn.sigmoid(_dot(hb, win_ref[:, C_GATE + jg * gw:C_GATE + (jg + 1) * gw])).astype(bf16)
        if jg * gw < D_MODEL:
            gc[:, jg * gw:(jg + 1) * gw] = gate
        else:
            g_ref[:, jg * gw - D_MODEL:(jg + 1) * gw - D_MODEL] = gate
        conv_some(per_step)

    conv_some(len(chunks))
    for r0 in range(0, tm, rt):
        cm_ref[r0:r0 + rt, :] = _conv_out(ybuf[r0:r0 + rt, :], lng_ref, lnb_ref, wco_ref, gc[r0:r0 + rt, :])


def _inproj(x2d, m3, norm1, win, q_norm, wqb, wqbp, kv_norm, cos, sin, conv_w, *, n, rope, tm,
            rt=CONV_OUT_ROWS, ct=CONV_CHUNK):
    tokens = x2d.shape[0]
    halo = n > tm
    assert n % tm == 0 if halo else (tm % n == 0 and m3.shape[0] == 1 and not rope)
    tiles_per_seq = max(1, n // tm)
    q_scale = float((QK_NOPE + QK_ROPE) ** -0.5 * LOG2E)
    tile = lambda w: pl.BlockSpec((tm, w), lambda i: (i, 0))
    in_specs = [tile(D_MODEL)]
    args = [x2d]
    if halo:
        hb_tile = tm // CONV_HALO
        hb_seq = n // CONV_HALO
        prev = lambda i: (jnp.maximum(i * hb_tile - 1, (i // tiles_per_seq) * hb_seq), 0)
        nxt = lambda i: (jnp.minimum((i + 1) * hb_tile, (i // tiles_per_seq + 1) * hb_seq - 1), 0)
        in_specs += [pl.BlockSpec((CONV_HALO, D_MODEL), prev), pl.BlockSpec((CONV_HALO, D_MODEL), nxt)]
        args += [x2d, x2d]
    in_specs += [
        _mod_spec(m3, lambda i: i // tiles_per_seq),
        _const_spec((1, D_MODEL)),
        _const_spec((D_MODEL, C_KRP + LANES if rope else C_END)),
        _const_spec((1, Q_LORA)),
        _const_spec(wqb.shape),
    ]
    args += [m3, norm1, win, q_norm, wqb]
    if rope:
        in_specs.append(_const_spec(wqbp.shape))
        args.append(wqbp)
    in_specs.append(_const_spec((1, KV_LORA)))
    args.append(kv_norm)
    if rope:
        tab = pl.BlockSpec((tm, LANES), lambda i: (i % tiles_per_seq, 0))
        in_specs += [tab, tab]
        args += [cos, sin]
    args += list(conv_w)
    in_specs += [_const_spec(w.shape) for w in conv_w]
    pieces = max(1, tm // n)
    scratch = [
        pltpu.VMEM((pieces, tm // pieces + 2 * CONV_HALO, CONV_DIM), bf16),
        pltpu.VMEM((tm, CONV_DIM), f32),
        pltpu.VMEM((tm, D_MODEL), bf16),
    ]
    out_shape = [
        jax.ShapeDtypeStruct((tokens, D_MODEL), bf16),
        jax.ShapeDtypeStruct((tokens, N_HEADS * HEAD_W), bf16),
        jax.ShapeDtypeStruct((tokens, KV_LORA), bf16 if rope else f32),
        jax.ShapeDtypeStruct((tokens, LANES), bf16),
        jax.ShapeDtypeStruct((tokens, D_MODEL), bf16),
    ]
    out_specs = [tile(D_MODEL), tile(N_HEADS * HEAD_W), tile(KV_LORA), tile(LANES), tile(D_MODEL)]
    if not rope:
        out_shape.append(jax.ShapeDtypeStruct((tokens, QK_ROPE), f32))
        out_specs.append(tile(QK_ROPE))
    return pl.pallas_call(
        functools.partial(_inproj_kernel, rope=rope, halo=halo, tiles_per_seq=tiles_per_seq, q_scale=q_scale,
                          n=n, rt=rt, ct=ct),
        grid=(tokens // tm,),
        in_specs=in_specs,
        out_specs=out_specs,
        out_shape=out_shape,
        scratch_shapes=scratch,
        compiler_params=_params(VMEM_MID, "parallel"),
        name="inproj_conv_rope" if rope else "inproj_conv",
    )(*args)


def _conv_weights(dw, dwb, lng, lnb, wco, ct):
    rows = ct + 2 * CONV_HALO
    i = np.arange(rows)
    shifts = jnp.asarray(np.stack([(i[None, :] == i[:, None] + ph) for ph in range(1, 8)]), bf16)
    dw_tiles = jnp.broadcast_to(dw[:, None, :], (CONV_WIDTH, 8, CONV_DIM))
    return shifts, dw_tiles, dwb, lng, lnb, wco


def _attn_kernel(q_ref, ckv_ref, kpe_ref, cm_ref, g_ref, x_ref, m_ref, wk_ref, wv_ref, wo_ref, wout_ref, n2_ref,
                 wr_ref, x1_ref, h2_ref, lg_ref, k_scr, v_scr, q_scr, *, sb, hg):
    s_len = ckv_ref.shape[1]

    @pl.when(pl.program_id(1) == 0)
    def _():
        ckv = ckv_ref[...].reshape(sb * s_len, KV_LORA).astype(bf16)
        kpe = kpe_ref[...].reshape(sb * s_len, LANES).astype(f32)
        k = _dot(ckv, wk_ref[...])
        v = _dot(ckv, wv_ref[...])
        ones = jnp.ones((s_len, V_DIM), bf16)
        for g in range(sb):
            rows = slice(g * s_len, (g + 1) * s_len)
            for hd in range(N_HEADS):
                u = g * N_HEADS + hd
                k_scr[u] = (k[rows, hd * HEAD_W:(hd + 1) * HEAD_W] + kpe[rows, :]).astype(bf16)
                vh = v[rows, hd * V_DIM:(hd + 1) * V_DIM].astype(bf16)
                if hd % 2 == 0:
                    v_scr[u, :, 0:V_DIM] = vh
                    v_scr[u, :, V_DIM:LANES] = ones
                else:
                    v_scr[u, :, 0:V_DIM] = ones
                    v_scr[u, :, V_DIM:LANES] = vh

    tq = q_ref.shape[0] // sb
    units = sb * N_HEADS
    for g in range(sb):
        for hd in range(N_HEADS):
            q_scr[g * N_HEADS + hd] = q_ref[g * tq:(g + 1) * tq, hd * HEAD_W:(hd + 1) * HEAD_W]
    lane = lax.broadcasted_iota(jnp.int32, (tq, LANES), 1)
    pairs = []
    for u0 in range(0, units, hg):
        us = slice(u0, u0 + hg)
        s = lax.dot_general(q_scr[us], k_scr[us], (((2,), (2,)), ((0,), (0,))), preferred_element_type=f32)
        mx = jnp.max(s, axis=-1, keepdims=True)
        p = jnp.exp2(s - mx).astype(bf16)
        r = lax.dot_general(p, v_scr[us], (((2,), (1,)), ((0,), (0,))), preferred_element_type=f32)
        for j in range(hg // 2):
            re, ro = r[2 * j], r[2 * j + 1]
            oe = re * (1.0 / re[:, V_DIM:V_DIM + 1])
            oo = ro * (1.0 / ro[:, 0:1])
            pairs.append(jnp.where(lane < V_DIM, oe, oo))
    per_seq = N_HEADS // 2
    attn = jnp.concatenate(
        [jnp.concatenate(pairs[g * per_seq:(g + 1) * per_seq], axis=-1) for g in range(sb)], axis=0).astype(bf16)
    merged = (cm_ref[...].astype(f32) + g_ref[...].astype(f32) * _dot(attn, wo_ref[...])).astype(bf16)

    g1 = m_ref[2:3, :]
    sh2 = m_ref[3:4, :]
    sc2 = m_ref[4:5, :]
    x1 = x_ref[...] + g1 * _dot(merged, wout_ref[...])
    x1_ref[...] = x1
    h2 = _rms(x1, n2_ref[...]) * (1.0 + sc2) + sh2
    hi = h2.astype(bf16)
    h2_ref[...] = hi
    lo = (h2 - hi.astype(f32)).astype(bf16)
    both = _dot(jnp.concatenate([hi, lo], axis=0), wr_ref[...])
    rows = hi.shape[0]
    s = both[0:rows] + both[rows:2 * rows]
    lg_ref[...] = s + pltpu.roll(s, LANES - N_EXPERTS, axis=1)


def _attn(q2d, ckv3, kpe3, cm2d, g2d, x2d, m3, wk, wv, wo, wout, norm2, wr, *, n, tq):
    nseq, s_len, _ = ckv3.shape
    sb = max(1, ATTN_TOKENS // n) if (tq == n and m3.shape[0] == 1) else 1
    units = sb * N_HEADS
    hg = max(2, min(units, SCORE_BYTES // (tq * s_len * 4)))
    assert units % hg == 0 and hg % 2 == 0 and nseq % sb == 0
    qb = n // tq
    tokens = nseq * n
    tile = lambda w: pl.BlockSpec((sb * tq, w), lambda s, i: (s * qb + i, 0))
    return pl.pallas_call(
        functools.partial(_attn_kernel, sb=sb, hg=hg),
        grid=(nseq // sb, qb),
        in_specs=[
            tile(N_HEADS * HEAD_W),
            pl.BlockSpec((sb, s_len, KV_LORA), lambda s, i: (s, 0, 0)),
            pl.BlockSpec((sb, s_len, LANES), lambda s, i: (s, 0, 0)),
            tile(D_MODEL),
            tile(D_MODEL),
            tile(D_MODEL),
            _mod_spec(m3, lambda s, i: s),
            _const_spec(wk.shape),
            _const_spec(wv.shape),
            _const_spec(wo.shape),
            _const_spec(wout.shape),
            _const_spec((1, D_MODEL)),
            _const_spec(wr.shape),
        ],
        out_specs=(tile(D_MODEL), tile(D_MODEL), tile(LANES)),
        out_shape=(
            jax.ShapeDtypeStruct((tokens, D_MODEL), f32),
            jax.ShapeDtypeStruct((tokens, D_MODEL), bf16),
            jax.ShapeDtypeStruct((tokens, LANES), f32),
        ),
        scratch_shapes=[
            pltpu.VMEM((units, s_len, HEAD_W), bf16),
            pltpu.VMEM((units, s_len, LANES), bf16),
            pltpu.VMEM((units, tq, HEAD_W), bf16),
        ],
        compiler_params=_params(VMEM_LIMIT, "parallel", "arbitrary"),
        name="attn_out",
    )(q2d, ckv3, kpe3, cm2d, g2d, x2d, m3, wk, wv, wo, wout, norm2, wr)


def _route_kernel(lg_ref, pos_ref, aff_ref, meta_ref, *, nseq, n, cap):
    for s in range(nseq):
        lt = lg_ref[s].T[0:N_EXPERTS, :]
        e = jnp.exp(lt - jnp.max(lt, axis=0, keepdims=True))
        aff_ref[s * N_EXPERTS:(s + 1) * N_EXPERTS, :] = e / jnp.sum(e, axis=0, keepdims=True)
    rows = nseq * N_EXPERTS
    capf = float(cap)

    def bit_step(i, t):
        cand = t | (jnp.int32(1) << (30 - i))
        thr = lax.bitcast_convert_type(cand, f32)
        cnt = jnp.sum(jnp.where(aff_ref[...] >= thr, 1.0, 0.0), axis=1, keepdims=True)
        return jnp.where(cnt >= capf, cand, t)

    t = lax.fori_loop(0, 31, bit_step, jnp.zeros((rows, 1), jnp.int32))
    thr = lax.bitcast_convert_type(t, f32)
    need = capf - jnp.sum(jnp.where(aff_ref[...] > thr, 1.0, 0.0), axis=1, keepdims=True)

    blk = MOE_BLOCK
    tri = jnp.where(
        lax.broadcasted_iota(jnp.int32, (blk, blk), 0) < lax.broadcasted_iota(jnp.int32, (blk, blk), 1),
        1.0, 0.0).astype(bf16)
    carry_gt = jnp.zeros((rows, 1), f32)
    carry_eq = jnp.zeros((rows, 1), f32)
    lane = lax.broadcasted_iota(jnp.int32, (rows, LANES), 1)
    meta = jnp.zeros((rows, LANES), f32)
    cmax = jnp.zeros((rows, 1), f32)
    for b in range(n // blk):
        sl = slice(b * blk, (b + 1) * blk)
        ab = aff_ref[:, sl]
        gt = ab > thr
        eq = ab == thr
        gtb = jnp.where(gt, 1.0, 0.0)
        eqb = jnp.where(eq, 1.0, 0.0)
        pre_gt = _dot(gtb.astype(bf16), tri) + carry_gt
        pre_eq = _dot(eqb.astype(bf16), tri) + carry_eq
        meta = jnp.where(lane == b, carry_gt + jnp.minimum(carry_eq, need), meta)
        carry_gt = carry_gt + jnp.sum(gtb, axis=1, keepdims=True)
        carry_eq = carry_eq + jnp.sum(eqb, axis=1, keepdims=True)
        sel = gt | (eq & (pre_eq < need))
        slot = pre_gt + jnp.minimum(pre_eq, need)
        pos_ref[:, sl] = jnp.where(sel, slot, -1.0).astype(jnp.int32)
        cmax = jnp.maximum(cmax, jnp.sum(jnp.where(sel, 1.0, 0.0), axis=1, keepdims=True))
    meta_ref[...] = jnp.where(lane == LANES - 1, cmax, meta).astype(jnp.int32)


def _route(lg3, *, cap):
    nseq, n, _ = lg3.shape
    rows = nseq * N_EXPERTS
    assert n // MOE_BLOCK < LANES
    return pl.pallas_call(
        functools.partial(_route_kernel, nseq=nseq, n=n, cap=cap),
        grid=(1,),
        in_specs=[_const_spec(lg3.shape)],
        out_specs=(_const_spec((rows, n)), _const_spec((rows, n)), _const_spec((rows, LANES))),
        out_shape=(jax.ShapeDtypeStruct((rows, n), jnp.int32), jax.ShapeDtypeStruct((rows, n), f32),
                   jax.ShapeDtypeStruct((rows, LANES), jnp.int32)),
        compiler_params=_params(VMEM_SMALL, "arbitrary"),
        name="route",
    )(lg3)


def _slot_hits(pos_ref, e, cap):
    width = pos_ref.shape[1]
    return lax.broadcasted_iota(jnp.int32, (cap, width), 0) == pos_ref[e:e + 1, :]


def _one_hot(hits):
    return jnp.concatenate([jnp.where(h, 1.0, 0.0).astype(bf16) for h in hits], axis=0)


def _gather_dense(pos_ref, aff_ref, h2_ref, xg_ref, vals_ref, rows, cap, ne):
    for e0 in range(0, N_EXPERTS, ne):
        hits = [_slot_hits(pos_ref, e0 + e, cap) for e in range(ne)]
        xg = _dot(_one_hot(hits), h2_ref[...]).astype(xg_ref.dtype)
        for e in range(ne):
            xg_ref[e0 + e, rows, :] = xg[e * cap:(e + 1) * cap, :]
            vals = jnp.sum(jnp.where(hits[e], aff_ref[e0 + e:e0 + e + 1, :], 0.0), axis=1, keepdims=True)
            vals_ref[e0 + e, rows, :] = jnp.broadcast_to(vals, (cap, LANES))


def _gather_kernel(pos_ref, aff_ref, h2_ref, xg_ref, vals_ref, *, cap, ne, sb):
    for g in range(sb):
        _gather_dense(pos_ref.at[g], aff_ref.at[g], h2_ref.at[g], xg_ref, vals_ref,
                      slice(g * cap, (g + 1) * cap), cap, ne)


def _gather(pos3, aff3, h23, *, cap, ne, sb):
    nseq, _, n = pos3.shape
    return pl.pallas_call(
        functools.partial(_gather_kernel, cap=cap, ne=ne, sb=sb),
        grid=(nseq // sb,),
        in_specs=[
            pl.BlockSpec((sb, N_EXPERTS, n), lambda s: (s, 0, 0)),
            pl.BlockSpec((sb, N_EXPERTS, n), lambda s: (s, 0, 0)),
            pl.BlockSpec((sb, n, D_MODEL), lambda s: (s, 0, 0)),
        ],
        out_specs=(
            pl.BlockSpec((N_EXPERTS, sb * cap, D_MODEL), lambda s: (0, s, 0)),
            pl.BlockSpec((N_EXPERTS, sb * cap, LANES), lambda s: (0, s, 0)),
        ),
        out_shape=(
            jax.ShapeDtypeStruct((N_EXPERTS, nseq * cap, D_MODEL), bf16),
            jax.ShapeDtypeStruct((N_EXPERTS, nseq * cap, LANES), f32),
        ),
        compiler_params=_params(VMEM_MID, "parallel"),
        name="gather",
    )(pos3, aff3, h23)


def _windows_fit(meta_ref):
    cmax = meta_ref[0, LANES - 1]
    for e in range(1, N_EXPERTS):
        cmax = jnp.maximum(cmax, meta_ref[e, LANES - 1])
    return cmax <= MOE_WIN - WIN_ALIGN


def _win_base(meta_ref, e, b, cap):
    start = meta_ref[e, b]
    return pl.multiple_of(jnp.minimum(start - start % WIN_ALIGN, cap - MOE_WIN), WIN_ALIGN)


def _win_hits(pos_row, base):
    slot = lax.broadcasted_iota(jnp.int32, (MOE_WIN, pos_row.shape[1]), 0) + base
    return slot == pos_row


def _gather_win_kernel(meta_ref, pos_ref, aff_ref, h2_ref, xg_ref, vals_ref, *, n, cap):
    fits = _windows_fit(meta_ref)

    @pl.when(fits)
    def _():
        xg_ref[...] = jnp.zeros(xg_ref.shape, xg_ref.dtype)
        vals_ref[...] = jnp.zeros(vals_ref.shape, vals_ref.dtype)

        def block(b, carry):
            cols = pl.ds(pl.multiple_of(b * MOE_BLOCK, MOE_BLOCK), MOE_BLOCK)
            bases = [_win_base(meta_ref, e, b, cap) for e in range(N_EXPERTS)]
            hits = [_win_hits(pos_ref[e:e + 1, cols], bases[e]) for e in range(N_EXPERTS)]
            part = _dot(_one_hot(hits), h2_ref[cols, :])
            for e in range(N_EXPERTS):
                rows = pl.ds(bases[e], MOE_WIN)
                xg_ref[e, rows, :] += part[e * MOE_WIN:(e + 1) * MOE_WIN, :].astype(xg_ref.dtype)
                vals = jnp.sum(jnp.where(hits[e], aff_ref[e:e + 1, cols], 0.0), axis=1, keepdims=True)
                vals_ref[e, rows, :] += jnp.broadcast_to(vals, (MOE_WIN, LANES))
            return carry

        lax.fori_loop(0, n // MOE_BLOCK, block, 0)

    @pl.when(jnp.logical_not(fits))
    def _():
        _gather_dense(pos_ref, aff_ref, h2_ref, xg_ref, vals_ref, slice(0, cap), cap, 1)


def _gather_win(meta, pos3, aff3, h23, *, cap):
    nseq, _, n = pos3.shape
    return pl.pallas_call(
        functools.partial(_gather_win_kernel, n=n, cap=cap),
        grid=(nseq,),
        in_specs=[
            pl.BlockSpec((N_EXPERTS, LANES), lambda s: (s, 0), memory_space=pltpu.SMEM),
            pl.BlockSpec((None, N_EXPERTS, n), lambda s: (s, 0, 0)),
            pl.BlockSpec((None, N_EXPERTS, n), lambda s: (s, 0, 0)),
            pl.BlockSpec((None, n, D_MODEL), lambda s: (s, 0, 0)),
        ],
        out_specs=(
            pl.BlockSpec((N_EXPERTS, cap, D_MODEL), lambda s: (0, s, 0)),
            pl.BlockSpec((N_EXPERTS, cap, LANES), lambda s: (0, s, 0)),
        ),
        out_shape=(
            jax.ShapeDtypeStruct((N_EXPERTS, nseq * cap, D_MODEL), bf16),
            jax.ShapeDtypeStruct((N_EXPERTS, nseq * cap, LANES), f32),
        ),
        compiler_params=_params(VMEM_LARGE, "parallel"),
        name="gather_win",
    )(meta, pos3, aff3, h23)


def _experts_kernel(xp_ref, vp_ref, xs_ref, vs_ref, wg_ref, wu_ref, wd_ref, yp_ref, ys_ref, *, rc):
    wg = wg_ref[...].astype(bf16)
    wu = wu_ref[...].astype(bf16)
    wd = wd_ref[...].astype(bf16)
    rp = xp_ref.shape[0]
    x_all = jnp.concatenate([xp_ref[...], xs_ref[...]], axis=0)
    v_all = jnp.concatenate([vp_ref[:, 0:1], vs_ref[:, 0:1]], axis=0)
    for r0 in range(0, x_all.shape[0], rc):
        x = x_all[r0:r0 + rc, :]
        a = _dot(x, wg)
        u = _dot(x, wu)
        hm = (a * jax.nn.sigmoid(a) * u).astype(bf16)
        y = (_dot(hm, wd) * v_all[r0:r0 + rc, :]).astype(yp_ref.dtype)
        r1 = r0 + y.shape[0]
        if r0 < rp:
            yp_ref[r0:min(r1, rp), :] = y[0:min(r1, rp) - r0, :]
        if r1 > rp:
            lo = max(r0, rp)
            ys_ref[lo - rp:r1 - rp, :] = y[lo - r0:r1 - r0, :]


def _experts(xg_p, vals_p, xg_s, vals_s, wg, wu, wd, *, rc=EXPERT_ROWS):
    rp = xg_p.shape[1]
    rs = xg_s.shape[1]
    per_e = lambda r, w: pl.BlockSpec((None, r, w), lambda e: (e, 0, 0))
    return pl.pallas_call(
        functools.partial(_experts_kernel, rc=rc),
        grid=(N_EXPERTS,),
        in_specs=[
            per_e(rp, D_MODEL), per_e(rp, LANES), per_e(rs, D_MODEL), per_e(rs, LANES),
            per_e(D_MODEL, EXPERT_FF), per_e(D_MODEL, EXPERT_FF), per_e(EXPERT_FF, D_MODEL),
        ],
        out_specs=(per_e(rp, D_MODEL), per_e(rs, D_MODEL)),
        out_shape=(
            jax.ShapeDtypeStruct((N_EXPERTS, rp, D_MODEL), bf16),
            jax.ShapeDtypeStruct((N_EXPERTS, rs, D_MODEL), bf16),
        ),
        compiler_params=_params(VMEM_LARGE, "parallel"),
        name="experts",
    )(xg_p, vals_p, xg_s, vals_s, wg, wu, wd)


def _scatter_dense(pos_ref, y_ref, rows, tn, cap, ne):
    moe = jnp.zeros((tn, D_MODEL), f32)
    for e0 in range(0, N_EXPERTS, ne):
        onehot = _one_hot([_slot_hits(pos_ref, e0 + e, cap) for e in range(ne)])
        y = jnp.concatenate([y_ref[e0 + e, rows, :] for e in range(ne)], axis=0)
        moe = moe + lax.dot_general(onehot, y, (((0,), (0,)), ((), ())), preferred_element_type=f32)
    return moe


def _scatter_kernel(pos_ref, y_ref, x1_ref, m_ref, fn_ref, o_ref, *, tn, cap, ne, sb):
    g2 = m_ref[5:6, :]
    for g in range(sb):
        moe = _scatter_dense(pos_ref.at[g], y_ref, slice(g * cap, (g + 1) * cap), tn, cap, ne)
        o_ref[g] = _rms(x1_ref[g] + g2 * moe, fn_ref[...])


def _scatter(pos3, y, x13, m3, fn, *, cap, ne, tn, sb):
    nseq, _, n = pos3.shape
    assert sb == 1 or m3.shape[0] == 1, "sequences sharing a grid step must share their modulation rows"
    return pl.pallas_call(
        functools.partial(_scatter_kernel, tn=tn, cap=cap, ne=ne, sb=sb),
        grid=(nseq // sb, n // tn),
        in_specs=[
            pl.BlockSpec((sb, N_EXPERTS, tn), lambda s, i: (s, 0, i)),
            pl.BlockSpec((N_EXPERTS, sb * cap, D_MODEL), lambda s, i: (0, s, 0)),
            pl.BlockSpec((sb, tn, D_MODEL), lambda s, i: (s, i, 0)),
            _mod_spec(m3, lambda s, i: s),
            _const_spec((1, D_MODEL)),
        ],
        out_specs=pl.BlockSpec((sb, tn, D_MODEL), lambda s, i: (s, i, 0)),
        out_shape=jax.ShapeDtypeStruct((nseq, n, D_MODEL), f32),
        compiler_params=_params(VMEM_MID, "parallel", "arbitrary"),
        name="scatter",
    )(pos3, y, x13, m3, fn)


def _scatter_win_kernel(meta_ref, pos_ref, y_ref, x1_ref, m_ref, fn_ref, o_ref, moe_scr, *, cap):
    b = pl.program_id(1)
    fits = _windows_fit(meta_ref)

    @pl.when(fits)
    def _():
        bases = [_win_base(meta_ref, e, b, cap) for e in range(N_EXPERTS)]
        onehot = _one_hot([_win_hits(pos_ref[e:e + 1, :], bases[e]) for e in range(N_EXPERTS)])
        y = jnp.concatenate([y_ref[e, pl.ds(bases[e], MOE_WIN), :] for e in range(N_EXPERTS)], axis=0)
        moe_scr[...] = lax.dot_general(onehot, y, (((0,), (0,)), ((), ())), preferred_element_type=f32)

    @pl.when(jnp.logical_not(fits))
    def _():
        moe_scr[...] = _scatter_dense(pos_ref, y_ref, slice(0, cap), MOE_BLOCK, cap, 1)

    o_ref[...] = _rms(x1_ref[...] + m_ref[5:6, :] * moe_scr[...], fn_ref[...])


def _scatter_win(meta, pos3, y, x13, m3, fn, *, cap):
    nseq, _, n = pos3.shape
    tn = MOE_BLOCK
    return pl.pallas_call(
        functools.partial(_scatter_win_kernel, cap=cap),
        grid=(nseq, n // tn),
        in_specs=[
            pl.BlockSpec((N_EXPERTS, LANES), lambda s, i: (s, 0), memory_space=pltpu.SMEM),
            pl.BlockSpec((None, N_EXPERTS, tn), lambda s, i: (s, 0, i)),
            pl.BlockSpec((N_EXPERTS, cap, D_MODEL), lambda s, i: (0, s, 0)),
            pl.BlockSpec((None, tn, D_MODEL), lambda s, i: (s, i, 0)),
            _mod_spec(m3, lambda s, i: s),
            _const_spec((1, D_MODEL)),
        ],
        out_specs=pl.BlockSpec((None, tn, D_MODEL), lambda s, i: (s, i, 0)),
        out_shape=jax.ShapeDtypeStruct((nseq, n, D_MODEL), f32),
        scratch_shapes=[pltpu.VMEM((tn, D_MODEL), f32)],
        compiler_params=_params(VMEM_MID, "parallel", "arbitrary"),
        name="scatter_win",
    )(meta, pos3, y, x13, m3, fn)


def _rope_tables(n):
    t = np.arange(n)
    half = QK_ROPE // 2
    freqs = ROPE_BASE ** (-np.arange(0, half, 2, dtype=np.float64) / half)
    ang_r = (t // GRID_W)[:, None] * freqs
    ang_c = (t % GRID_W)[:, None] * freqs
    cr, sr, cc, sc = np.cos(ang_r), np.sin(ang_r), np.cos(ang_c), np.sin(ang_c)
    cos = np.ones((n, HEAD_W))
    sin = np.zeros((n, HEAD_W))
    cos[:, ROPE_OFF:ROPE_OFF + QK_ROPE] = np.concatenate([cr, cr, cc, cc], axis=-1)
    sin[:, ROPE_OFF:ROPE_OFF + QK_ROPE] = np.concatenate([-sr, sr, -sc, sc], axis=-1)
    return jnp.asarray(cos, f32), jnp.asarray(sin, f32)


_PARTNER = np.concatenate([np.arange(8, 16), np.arange(0, 8), np.arange(24, 32), np.arange(16, 24)])


def _rope_partner(w):
    q = QK_ROPE // 4
    return jnp.concatenate([w[..., q:2 * q], w[..., 0:q], w[..., 3 * q:4 * q], w[..., 2 * q:3 * q]], axis=-1)


def _rope_placement():
    place = np.zeros((LANES, 2 * LANES), np.float32)
    d = np.arange(QK_ROPE)
    place[d, ROPE_OFF + d] = 1.0
    place[_PARTNER, LANES + ROPE_OFF + d] = 1.0
    return jnp.asarray(place, bf16)


def _head_blocks(w_nope, w_rope):
    rows = w_nope.shape[0]
    if w_rope is None:
        w_rope = jnp.zeros((rows, N_HEADS, QK_ROPE), w_nope.dtype)
    z = jnp.zeros((rows, N_HEADS, HEAD_W - QK_NOPE - QK_ROPE), w_nope.dtype)
    return jnp.concatenate([w_nope, w_rope, z], axis=-1).reshape(rows, N_HEADS * HEAD_W)


def kernel(x_prompt, x_sample, cache_ckv, cache_kpe, c, c_ctx, w_ada, b_ada, norm1, w_in, conv_dw, conv_dw_b,
           conv_ln_g, conv_ln_b, w_conv_out, q_norm, w_qb, kv_norm, w_kvb, w_o_mla, w_out, norm2, w_router,
           w_e_gate, w_e_up, w_e_down, final_norm):
    assert w_ada.shape[0] == 1, "single trunk layer"
    nb_p, n_p, _ = x_prompt.shape
    nb_s, n_s, _ = x_sample.shape

    win = _wprep(w_in[0].T, _rope_placement())
    wq = w_qb[0].reshape(Q_LORA, N_HEADS, QK_NOPE + QK_ROPE)
    wqb = _head_blocks(wq[..., :QK_NOPE], wq[..., QK_NOPE:]).astype(bf16)
    wqbp = _head_blocks(jnp.zeros_like(wq[..., :QK_NOPE]), _rope_partner(wq[..., QK_NOPE:])).astype(bf16)
    wkv = w_kvb[0].reshape(KV_LORA, N_HEADS, QK_NOPE + V_DIM)
    wk = _head_blocks(wkv[..., :QK_NOPE], None).astype(bf16)
    wv = wkv[..., QK_NOPE:].reshape(KV_LORA, N_HEADS * V_DIM).astype(bf16)
    wco = w_conv_out[0].astype(bf16)
    wo = w_o_mla[0].astype(bf16)
    wout = w_out[0].astype(bf16)
    wr_hi = w_router[0].astype(bf16)
    wr_lo = (w_router[0] - wr_hi.astype(f32)).astype(bf16)
    wr = jnp.concatenate([wr_hi, wr_lo, jnp.zeros((D_MODEL, LANES - 2 * N_EXPERTS), bf16)], axis=-1)
    row = lambda a: a.reshape(1, -1)

    mod = jnp.concatenate([c_ctx[None, :], c, jnp.zeros((8 - 1 - nb_s, D_MODEL), f32)], axis=0)
    m = _ada(mod, w_ada[0], b_ada[0]).reshape(8, 6, D_MODEL)
    m_p, m_s = m[0:1], m[1:1 + nb_s]
    cos, sin = _rope_tables(n_s)

    conv_w = _conv_weights(conv_dw[0], row(conv_dw_b[0]), row(conv_ln_g[0]), row(conv_ln_b[0]), wco, CONV_CHUNK)

    def mixers(x, m3, rope, ctx_ckv, ctx_kpe):
        nseq, n, _ = x.shape
        x2d = x.reshape(nseq * n, D_MODEL)
        cm, q, ckv, kr, g, *kpe = _inproj(x2d, m3, row(norm1[0]), win, row(q_norm[0]), wqb, wqbp, row(kv_norm[0]),
                                          cos, sin, conv_w, n=n, rope=rope, tm=IN_TILE)
        keys_ckv = ckv.reshape(nseq, n, KV_LORA)
        keys_kpe = kr.reshape(nseq, n, LANES)
        if ctx_ckv is not None:
            keys_ckv = jnp.concatenate([ctx_ckv.astype(keys_ckv.dtype), keys_ckv], axis=1)
            keys_kpe = jnp.concatenate([ctx_kpe.astype(keys_kpe.dtype), keys_kpe], axis=1)
        x1, h2, lg = _attn(q, keys_ckv, keys_kpe, cm.reshape(nseq * n, D_MODEL), g, x2d, m3, wk, wv, wo, wout,
                           row(norm2[0]), wr, n=n, tq=min(n, Q_TILE))
        return x1, h2, lg, ckv, kpe

    ctx_kpe = jnp.pad(cache_kpe[:, 0], ((0, 0), (0, 0), (ROPE_OFF, LANES - ROPE_OFF - QK_ROPE)))
    x1_p, h2_p, lg_p, ckv_p, (kpe_p,) = mixers(x_prompt, m_p, False, None, None)
    x1_s, h2_s, lg_s, _, _ = mixers(x_sample, m_s, True, cache_ckv[:, 0], ctx_kpe)

    def moe_tiles(n):
        cap = EC_FACTOR * n // N_EXPERTS
        ne = N_EXPERTS if N_EXPERTS * cap <= MOE_ROWS else 1
        sb = max(1, MOE_TOKENS // n)
        return cap, ne, sb

    def windowed(n):
        return n >= 4 * MOE_BLOCK and EC_FACTOR * n // N_EXPERTS >= 2 * MOE_WIN

    def route_gather(h2, lg, nseq, n):
        cap, ne, sb = moe_tiles(n)
        pos, aff, meta = _route(lg.reshape(nseq, n, LANES), cap=cap)
        pos3 = pos.reshape(nseq, N_EXPERTS, n)
        aff3 = aff.reshape(nseq, N_EXPERTS, n)
        h23 = h2.reshape(nseq, n, D_MODEL)
        if windowed(n):
            xg, vals = _gather_win(meta, pos3, aff3, h23, cap=cap)
        else:
            xg, vals = _gather(pos3, aff3, h23, cap=cap, ne=ne, sb=sb)
        return pos3, meta, xg, vals

    pos_p, meta_p, xg_p, vals_p = route_gather(h2_p, lg_p, nb_p, n_p)
    pos_s, meta_s, xg_s, vals_s = route_gather(h2_s, lg_s, nb_s, n_s)
    y_p, y_s = _experts(xg_p, vals_p, xg_s, vals_s, w_e_gate[0], w_e_up[0], w_e_down[0])
    fn = row(final_norm)

    def scatter(pos, meta, y, x1, m3, nseq, n):
        cap, ne, sb = moe_tiles(n)
        x13 = x1.reshape(nseq, n, D_MODEL)
        if windowed(n):
            return _scatter_win(meta, pos, y, x13, m3, fn, cap=cap)
        return _scatter(pos, y, x13, m3, fn, cap=cap, ne=ne, tn=min(n, SCATTER_TILE), sb=sb)

    y_prompt = scatter(pos_p, meta_p, y_p, x1_p, m_p, nb_p, n_p)
    y_sample = scatter(pos_s, meta_s, y_s, x1_s, m_s, nb_s, n_s)

    new_ckv = ckv_p.reshape(nb_p, 1, n_p, KV_LORA)
    new_kpe = kpe_p.reshape(nb_p, 1, n_p, QK_ROPE)
    return (y_prompt, y_sample, new_ckv, new_kpe)
```
